```python
import math
import jax, jax.numpy as jnp
from jax import lax
import numpy as np

D_MODEL = 2048
BATCH = 4
SEQ = 2048
DEPTH = 1

CHUNK = 64
EPS = 1e-6
MLA_HEADS = 8
QK_NOPE = 128
QK_ROPE = 64
V_HEAD = 128
Q_LORA = 512
KV_LORA = 512
ROPE_THETA = 10000.0
Q_BLOCK = 128
S5_CH = 1024
S5_GROUP = 16
S5_GROUPS = S5_CH // S5_GROUP
S5_STATE = 64
DT_MIN = 1e-3
DT_MAX = 1e-1
N_EXPERTS = 64
TOP_K = 6
N_EXPERT_GROUPS = 8
TOPK_GROUPS = 4
D_EXPERT = 512
D_SHARED = 512
ROUTED_SCALE = 2.5
EXPERT_BLOCK = 128

IN_SPLITS = [Q_LORA, Q_LORA + KV_LORA, Q_LORA + KV_LORA + QK_ROPE,
             Q_LORA + KV_LORA + QK_ROPE + S5_CH, Q_LORA + KV_LORA + QK_ROPE + S5_CH + D_MODEL]
IN_COLS = Q_LORA + KV_LORA + QK_ROPE + S5_CH + 2 * D_MODEL

kernel_name = "hybrid_mla_s5_moe_chunk_causal_block"


def rmsnorm(x, g):
    xf = x.astype(jnp.float32)
    y = xf * lax.rsqrt(jnp.mean(xf * xf, axis=-1, keepdims=True) + EPS)
    return (y * g.astype(jnp.float32)).astype(x.dtype)


def rope(x, cos, sin):
    x1, x2 = jnp.split(x, 2, axis=-1)
    return jnp.concatenate([x1 * cos - x2 * sin, x2 * cos + x1 * sin], axis=-1)


def mla(q_lat, kv_lat, k_pe_raw, positions, g_q, g_kv, w_uq, w_uk, w_uv):
    Bn, L, _ = q_lat.shape
    dt = q_lat.dtype
    q = (rmsnorm(q_lat, g_q) @ w_uq).reshape(Bn, L, MLA_HEADS, QK_NOPE + QK_ROPE)
    q_nope, q_pe = q[..., :QK_NOPE], q[..., QK_NOPE:]
    ckv = rmsnorm(kv_lat, g_kv)
    k_nope = (ckv @ w_uk).reshape(Bn, L, MLA_HEADS, QK_NOPE)
    v = (ckv @ w_uv).reshape(Bn, L, MLA_HEADS, V_HEAD)
    inv_freq = ROPE_THETA ** (-jnp.arange(QK_ROPE // 2, dtype=jnp.float32) / (QK_ROPE // 2))
    ang = positions.astype(jnp.float32)[..., None] * inv_freq
    cos, sin = jnp.cos(ang).astype(dt), jnp.sin(ang).astype(dt)
    q_pe = rope(q_pe, cos[:, :, None, :], sin[:, :, None, :])
    k_pe = rope(k_pe_raw, cos, sin)
    scale = (QK_NOPE + QK_ROPE) ** -0.5
    chunk_id = positions // CHUNK

    def block(i):
        s = i * Q_BLOCK
        qn = lax.dynamic_slice_in_dim(q_nope, s, Q_BLOCK, axis=1)
        qp = lax.dynamic_slice_in_dim(q_pe, s, Q_BLOCK, axis=1)
        qc = lax.dynamic_slice_in_dim(chunk_id, s, Q_BLOCK, axis=1)
        sc = (jnp.einsum('bqhd,bkhd->bhqk', qn, k_nope, preferred_element_type=jnp.float32)
              + jnp.einsum('bqhd,bkd->bhqk', qp, k_pe, preferred_element_type=jnp.float32)) * scale
        mask = chunk_id[:, None, :] <= qc[:, :, None]
        sc = jnp.where(mask[:, None], sc, jnp.float32(-1e30))
        p = jax.nn.softmax(sc, axis=-1).astype(v.dtype)
        return jnp.einsum('bhqk,bkhd->bqhd', p, v)

    out = lax.map(block, jnp.arange(L // Q_BLOCK))
    return jnp.moveaxis(out, 0, 1).reshape(Bn, L, MLA_HEADS * V_HEAD)


def s5(u, a_re, a_im, log_dt, b_re, b_im, c_re, c_im, d_skip, w_glu):
    Bn, L, _ = u.shape
    uf = u.astype(jnp.float32).reshape(Bn, L, S5_GROUPS, S5_GROUP)
    step = jnp.exp(log_dt.astype(jnp.float32))[:, None]
    ar, ai = a_re.astype(jnp.float32), a_im.astype(jnp.float32)
    mag = jnp.exp(ar * step)
    abar_re, abar_im = mag * jnp.cos(ai * step), mag * jnp.sin(ai * step)
    den = ar * ar + ai * ai
    nr, ni = abar_re - 1.0, abar_im
    f_re, f_im = (nr * ar + ni * ai) / den, (ni * ar - nr * ai) / den
    br, bi = b_re.astype(jnp.float32), b_im.astype(jnp.float32)
    bbar_re = f_re[..., None] * br - f_im[..., None] * bi
    bbar_im = f_re[..., None] * bi + f_im[..., None] * br
    bu_re = jnp.einsum('blgc,gpc->blgp', uf, bbar_re)
    bu_im = jnp.einsum('blgc,gpc->blgp', uf, bbar_im)
    a_re_t = jnp.broadcast_to(abar_re, bu_re.shape)
    a_im_t = jnp.broadcast_to(abar_im, bu_im.shape)

    def combine(e1, e2):
        a1r, a1i, b1r, b1i = e1
        a2r, a2i, b2r, b2i = e2
        return (a2r * a1r - a2i * a1i, a2r * a1i + a2i * a1r,
                a2r * b1r - a2i * b1i + b2r, a2r * b1i + a2i * b1r + b2i)

    _, _, xr, xi = lax.associative_scan(combine, (a_re_t, a_im_t, bu_re, bu_im), axis=1)
    y = (jnp.einsum('blgp,gcp->blgc', xr, c_re.astype(jnp.float32))
         - jnp.einsum('blgp,gcp->blgc', xi, c_im.astype(jnp.float32))
         + d_skip.astype(jnp.float32) * uf)
    y = jax.nn.gelu(y.reshape(Bn, L, S5_CH)).astype(u.dtype)
    return y * jax.nn.sigmoid(y @ w_glu)


def route(h, w_router, router_bias):
    N = h.shape[0]
    scores = jax.nn.sigmoid(jnp.einsum('nd,de->ne', h, w_router, preferred_element_type=jnp.float32))
    sel = scores + router_bias.astype(jnp.float32)
    grp = sel.reshape(N, N_EXPERT_GROUPS, N_EXPERTS // N_EXPERT_GROUPS)
    gscore = lax.top_k(grp, 2)[0].sum(-1)
    _, gidx = lax.top_k(gscore, TOPK_GROUPS)
    gmask = jnp.any(gidx[..., None] == jnp.arange(N_EXPERT_GROUPS), axis=-2)
    emask = jnp.repeat(gmask, N_EXPERTS // N_EXPERT_GROUPS, axis=-1)
    _, topi = lax.top_k(jnp.where(emask, sel, -jnp.inf), TOP_K)
    w = jnp.take_along_axis(scores, topi, axis=-1)
    w = w / jnp.sum(w, axis=-1, keepdims=True) * ROUTED_SCALE
    return topi, w.astype(h.dtype)


def routed_experts(xt, topi, topw, w_gate, w_up, w_down):
    N, D = xt.shape
    NK = N * TOP_K
    flat_e = topi.reshape(NK)
    flat_tok = jnp.repeat(jnp.arange(N, dtype=jnp.int32), TOP_K)
    flat_w = topw.reshape(NK)
    order = jnp.argsort(flat_e)
    se, stok, sw = flat_e[order], flat_tok[order], flat_w[order]
    counts = jnp.bincount(flat_e, length=N_EXPERTS)
    padded = (counts + EXPERT_BLOCK - 1) // EXPERT_BLOCK * EXPERT_BLOCK
    start = jnp.cumsum(counts) - counts
    pad_end = jnp.cumsum(padded)
    pad_start = pad_end - padded
    dest = pad_start[se] + jnp.arange(NK) - start[se]
    n_blocks = NK // EXPERT_BLOCK + N_EXPERTS
    P = n_blocks * EXPERT_BLOCK
    buf_tok = jnp.zeros((P,), jnp.int32).at[dest].set(stok)
    buf_w = jnp.zeros((P,), xt.dtype).at[dest].set(sw)
    blk_e = jnp.minimum(jnp.searchsorted(pad_end, jnp.arange(n_blocks) * EXPERT_BLOCK, side='right'),
                        N_EXPERTS - 1)

    def run(args):
        tok, e = args
        xb = xt[tok]
        return (jax.nn.silu(xb @ w_gate[e]) * (xb @ w_up[e])) @ w_down[e]

    y = lax.map(run, (buf_tok.reshape(n_blocks, EXPERT_BLOCK), blk_e)).reshape(P, D)
    return jnp.zeros_like(xt).at[buf_tok].add(y * buf_w[:, None].astype(y.dtype))


def setup_inputs(seed: int = 0) -> dict:
    key = jax.random.key(seed)
    ks = iter(jax.random.split(key, 48))
    f32 = jnp.float32

    def nrm(shape, scale):
        return jax.random.normal(next(ks), shape, f32) * scale

    def gain(shape):
        return 1.0 + nrm(shape, 0.05)

    G, P, Cg, Ld, D = S5_GROUPS, S5_STATE, S5_GROUP, DEPTH, D_MODEL
    x = nrm((BATCH, SEQ, D), 1.0)
    c = nrm((BATCH, D), 1.0)
    positions = (jax.random.randint(next(ks), (BATCH, 1), 0, 64, jnp.int32) * CHUNK
                 + jnp.arange(SEQ, dtype=jnp.int32)[None, :])
    return {
        "x": x, "c": c, "positions": positions,
        "w_ada": nrm((Ld, D, 6 * D), 0.5 * D ** -0.5),
        "b_ada": nrm((Ld, 6 * D), 0.01),
        "g_pre_mix": gain((Ld, D)), "g_post_mix": gain((Ld, D)),
        "g_pre_ffn": gain((Ld, D)), "g_post_ffn": gain((Ld, D)),
        "w_in": nrm((Ld, D, IN_COLS), D ** -0.5),
        "g_q": gain((Ld, Q_LORA)), "g_kv": gain((Ld, KV_LORA)),
        "w_uq": nrm((Ld, Q_LORA, MLA_HEADS * (QK_NOPE + QK_ROPE)), Q_LORA ** -0.5),
        "w_uk": nrm((Ld, KV_LORA, MLA_HEADS * QK_NOPE), KV_LORA ** -0.5),
        "w_uv": nrm((Ld, KV_LORA, MLA_HEADS * V_HEAD), KV_LORA ** -0.5),
        "a_re": -0.5 * jnp.exp(nrm((Ld, G, P), 0.05)),
        "a_im": jnp.pi * jnp.arange(P, dtype=f32) + nrm((Ld, G, P), 0.05),
        "log_dt": jax.random.uniform(next(ks), (Ld, G), f32, math.log(DT_MIN), math.log(DT_MAX)),
        "b_re": nrm((Ld, G, P, Cg), (2 * Cg) ** -0.5),
        "b_im": nrm((Ld, G, P, Cg), (2 * Cg) ** -0.5),
        "c_re": nrm((Ld, G, Cg, P), (2 * P) ** -0.5),
        "c_im": nrm((Ld, G, Cg, P), (2 * P) ** -0.5),
        "d_skip": nrm((Ld, G, Cg), 1.0),
        "w_glu": nrm((Ld, S5_CH, S5_CH), S5_CH ** -0.5),
        "w_br_mla": nrm((Ld, MLA_HEADS * V_HEAD, D), (MLA_HEADS * V_HEAD) ** -0.5),
        "w_br_s5": nrm((Ld, S5_CH, D), S5_CH ** -0.5),
        "w_out": nrm((Ld, D, D), D ** -0.5),
        "w_router": nrm((Ld, D, N_EXPERTS), D ** -0.5),
        "router_bias": nrm((Ld, N_EXPERTS), 0.01),
        "w_exp_gate": nrm((Ld, N_EXPERTS, D, D_EXPERT), D ** -0.5),
        "w_exp_up": nrm((Ld, N_EXPERTS, D, D_EXPERT), D ** -0.5),
        "w_exp_down": nrm((Ld, N_EXPERTS, D_EXPERT, D), D_EXPERT ** -0.5),
        "w_sh_gate": nrm((Ld, D, D_SHARED), D ** -0.5),
        "w_sh_up": nrm((Ld, D, D_SHARED), D ** -0.5),
        "w_sh_down": nrm((Ld, D_SHARED, D), D_SHARED ** -0.5),
    }


def reference(x, c, positions, w_ada, b_ada, g_pre_mix, g_post_mix, g_pre_ffn, g_post_ffn, w_in,
              g_q, g_kv, w_uq, w_uk, w_uv, a_re, a_im, log_dt, b_re, b_im, c_re, c_im, d_skip, w_glu,
              w_br_mla, w_br_s5, w_out, w_router, router_bias, w_exp_gate, w_exp_up, w_exp_down,
              w_sh_gate, w_sh_up, w_sh_down):
    Bn, L, D = x.shape
    for l in range(DEPTH):
        mod = (jax.nn.silu(c) @ w_ada[l] + b_ada[l])[:, None, :]
        sh1, sc1, gt1, sh2, sc2, gt2 = jnp.split(mod, 6, axis=-1)
        h = rmsnorm(x, g_pre_mix[l]) * (1.0 + sc1) + sh1
        q_lat, kv_lat, k_pe, u, gate_mla, gate_s5 = jnp.split(h @ w_in[l], IN_SPLITS, axis=-1)
        y_mla = mla(q_lat, kv_lat, k_pe, positions, g_q[l], g_kv[l], w_uq[l], w_uk[l], w_uv[l]) @ w_br_mla[l]
        y_s5 = s5(u, a_re[l], a_im[l], log_dt[l], b_re[l], b_im[l], c_re[l], c_im[l], d_skip[l],
                  w_glu[l]) @ w_br_s5[l]
        mixed = (jax.nn.sigmoid(gate_mla) * y_mla + jax.nn.sigmoid(gate_s5) * y_s5) @ w_out[l]
        x = x + gt1 * rmsnorm(mixed, g_post_mix[l])
        h = (rmsnorm(x, g_pre_ffn[l]) * (1.0 + sc2) + sh2).reshape(Bn * L, D)
        topi, topw = route(h, w_router[l], router_bias[l])
        shared = (jax.nn.silu(h @ w_sh_gate[l]) * (h @ w_sh_up[l])) @ w_sh_down[l]
        ffn = routed_experts(h, topi, topw, w_exp_gate[l], w_exp_up[l], w_exp_down[l]) + shared
        x = x + gt2 * rmsnorm(ffn.reshape(Bn, L, D), g_post_ffn[l])
    return x
```

```python
import functools

import jax
import jax.numpy as jnp
from jax import lax
from jax.experimental import pallas as pl
from jax.experimental.pallas import tpu as pltpu

F32 = jnp.float32
BF16 = jnp.bfloat16

D_MODEL = 2048
CHUNK = 64
EPS = 1e-6
MLA_HEADS = 8
QK_NOPE = 128
QK_ROPE = 64
V_HEAD = 128
Q_LORA = 512
KV_LORA = 512
ROPE_THETA = 10000.0
S5_CH = 1024
S5_GROUP = 16
S5_GROUPS = S5_CH // S5_GROUP
S5_STATE = 64
N_EXPERTS = 64
TOP_K = 6
N_EXPERT_GROUPS = 8
TOPK_GROUPS = 4
D_EXPERT = 512
ROUTED_SCALE = 2.5

LANES = 128
SUBLANES = 8
QK_PAD = 2 * LANES
VMEM_LIMIT = 56 * 1024 * 1024
NEG = -1e30

S5_GB = LANES // S5_GROUP
S5_NBLK = S5_CH // LANES
S5_PAIRS = S5_NBLK // 2
S5_LANES = S5_GB * S5_STATE
S5_TC = 256
S5_RB = 512

ATT_TQ = 256
ATT_TK = 256
MOE_TB = 256


def _cparams(sem):
    return pltpu.CompilerParams(dimension_semantics=sem, vmem_limit_bytes=VMEM_LIMIT)


def _rms(x, g):
    return x * lax.rsqrt(jnp.mean(x * x, axis=-1, keepdims=True) + EPS) * g


def _adaln_kernel(c_ref, w_ref, b_ref, o_ref):
    c = c_ref[...]
    a = (c * jax.nn.sigmoid(c)).astype(BF16)
    o_ref[...] = jnp.dot(a, w_ref[...].astype(BF16), preferred_element_type=F32) + b_ref[...]


def _adaln(c, w, b):
    bn, d = c.shape
    n = w.shape[1]
    tn = 1024
    return pl.pallas_call(
        _adaln_kernel,
        grid=(n // tn,),
        in_specs=[pl.BlockSpec((bn, d), lambda j: (0, 0)),
                  pl.BlockSpec((d, tn), lambda j: (0, j)),
                  pl.BlockSpec((1, tn), lambda j: (0, j))],
        out_specs=pl.BlockSpec((bn, tn), lambda j: (0, j)),
        out_shape=jax.ShapeDtypeStruct((bn, n), F32),
        compiler_params=_cparams(("arbitrary",)),
        name="adaln",
    )(c, w, b.reshape(1, n))


def _prenorm_kernel(x_ref, mod_ref, g_ref, o_ref, *, sh_row, sc_row):
    y = _rms(x_ref[0], g_ref[...])
    o_ref[0] = (y * (1.0 + mod_ref[0, sc_row:sc_row + 1, :]) + mod_ref[0, sh_row:sh_row + 1, :]).astype(o_ref.dtype)


def _prenorm(x, mod, g, sh_row, sc_row):
    bn, l, d = x.shape
    tl = 512
    return pl.pallas_call(
        functools.partial(_prenorm_kernel, sh_row=sh_row, sc_row=sc_row),
        grid=(bn, l // tl),
        in_specs=[pl.BlockSpec((1, tl, d), lambda b, i: (b, i, 0)),
                  pl.BlockSpec((1, 6, d), lambda b, i: (b, 0, 0)),
                  pl.BlockSpec((1, d), lambda b, i: (0, 0))],
        out_specs=pl.BlockSpec((1, tl, d), lambda b, i: (b, i, 0)),
        out_shape=jax.ShapeDtypeStruct((bn, l, d), BF16),
        compiler_params=_cparams(("arbitrary", "arbitrary")),
        name="prenorm",
    )(x, mod, g.reshape(1, d))


def _mm_kernel(a_ref, w_ref, o_ref, *, act):
    acc = jnp.dot(a_ref[...], w_ref[...], preferred_element_type=F32)
    if act == "sigmoid":
        acc = jax.nn.sigmoid(acc)
    o_ref[...] = acc.astype(o_ref.dtype)


def _mm(a, w, out_dtype, tm, tn, act=None, name="mm"):
    m, k = a.shape
    n = w.shape[1]
    return pl.pallas_call(
        functools.partial(_mm_kernel, act=act),
        grid=(n // tn, m // tm),
        in_specs=[pl.BlockSpec((tm, k), lambda j, i: (i, 0)),
                  pl.BlockSpec((k, tn), lambda j, i: (0, j))],
        out_specs=pl.BlockSpec((tm, tn), lambda j, i: (i, j)),
        out_shape=jax.ShapeDtypeStruct((m, n), out_dtype),
        compiler_params=_cparams(("arbitrary", "arbitrary")),
        name=name,
    )(a, w)


def _mm_u_kernel(a_ref, w_ref, o_ref):
    o_ref[0] = jnp.dot(a_ref[...], w_ref[...], preferred_element_type=F32)


def _mm_u(h, w, bn, l):
    m, k = h.shape
    tm = 1024
    nl = l // tm
    tn = 2 * LANES
    return pl.pallas_call(
        _mm_u_kernel,
        grid=(S5_PAIRS, bn, nl),
        in_specs=[pl.BlockSpec((tm, k), lambda j, b, i: (b * nl + i, 0)),
                  pl.BlockSpec((k, tn), lambda j, b, i: (0, j))],
        out_specs=pl.BlockSpec((1, tm, tn), lambda j, b, i: (j, i, b)),
        out_shape=jax.ShapeDtypeStruct((S5_PAIRS, l, bn * tn), F32),
        compiler_params=_cparams(("arbitrary", "arbitrary", "arbitrary")),
        name="mm_u",
    )(h, w)


def _rope_tile(t, c_ref, s1_ref, s2_ref):
    return (t * c_ref[...] + pltpu.roll(t, LANES - QK_ROPE // 2, 1) * s1_ref[...]
            + pltpu.roll(t, QK_ROPE // 2, 1) * s2_ref[...])


def _qproj_kernel(lat_ref, g_ref, w_ref, c_ref, s1_ref, s2_ref, o_ref, *, scale):
    xn = _rms(lat_ref[...], g_ref[...]).astype(BF16)
    q = jnp.dot(xn, w_ref[...], preferred_element_type=F32)
    t = _rope_tile(q[:, LANES:], c_ref, s1_ref, s2_ref)
    o_ref[...] = (jnp.concatenate([q[:, :LANES], t], axis=1) * scale).astype(o_ref.dtype)


def _qproj(lat, g_q, w_q, rope_c, rope_s1, rope_s2):
    n = lat.shape[0]
    tm = 1024
    tab = pl.BlockSpec((tm, LANES), lambda h, i: (i, 0))
    return pl.pallas_call(
        functools.partial(_qproj_kernel, scale=(QK_NOPE + QK_ROPE) ** -0.5),
        grid=(MLA_HEADS, n // tm),
        in_specs=[pl.BlockSpec((tm, Q_LORA), lambda h, i: (i, 0)),
                  pl.BlockSpec((1, Q_LORA), lambda h, i: (0, 0)),
                  pl.BlockSpec((Q_LORA, QK_PAD), lambda h, i: (0, h)),
                  tab, tab, tab],
        out_specs=pl.BlockSpec((tm, QK_PAD), lambda h, i: (i, h)),
        out_shape=jax.ShapeDtypeStruct((n, MLA_HEADS * QK_PAD), BF16),
        compiler_params=_cparams(("arbitrary", "arbitrary")),
        name="qproj",
    )(lat, g_q.reshape(1, Q_LORA), w_q, rope_c, rope_s1, rope_s2)


def _kvproj_kernel(lat_ref, g_ref, wk_ref, wv_ref, kpe_ref, c_ref, s1_ref, s2_ref, k_ref, v_ref):
    xn = _rms(lat_ref[...], g_ref[...]).astype(BF16)
    kn = jnp.dot(xn, wk_ref[...], preferred_element_type=F32)
    t = _rope_tile(kpe_ref[...], c_ref, s1_ref, s2_ref)
    k_ref[...] = jnp.concatenate([kn, t], axis=1).astype(k_ref.dtype)
    v_ref[...] = jnp.dot(xn, wv_ref[...], preferred_element_type=F32).astype(v_ref.dtype)


def _kvproj(lat, g_kv, w_k, w_v, kpe, rope_c, rope_s1, rope_s2):
    n = lat.shape[0]
    tm = 1024
    tab = pl.BlockSpec((tm, LANES), lambda h, i: (i, 0))
    return pl.pallas_call(
        _kvproj_kernel,
        grid=(MLA_HEADS, n // tm),
        in_specs=[pl.BlockSpec((tm, KV_LORA), lambda h, i: (i, 1)),
                  pl.BlockSpec((1, KV_LORA), lambda h, i: (0, 0)),
                  pl.BlockSpec((KV_LORA, QK_NOPE), lambda h, i: (0, h)),
                  pl.BlockSpec((KV_LORA, V_HEAD), lambda h, i: (0, h)),
                  tab, tab, tab, tab],
        out_specs=[pl.BlockSpec((tm, QK_PAD), lambda h, i: (i, h)),
                   pl.BlockSpec((tm, V_HEAD), lambda h, i: (i, h))],
        out_shape=[jax.ShapeDtypeStruct((n, MLA_HEADS * QK_PAD), BF16),
                   jax.ShapeDtypeStruct((n, MLA_HEADS * V_HEAD), BF16)],
        compiler_params=_cparams(("arbitrary", "arbitrary")),
        name="kvproj",
    )(lat, g_kv.reshape(1, KV_LORA), w_k, w_v, kpe, rope_c, rope_s1, rope_s2)


def _attn_kernel(hi_ref, q_ref, k_ref, v_ref, qc_ref, kc_ref, o_ref, *, nq):
    b = pl.program_id(0)
    i = pl.program_id(2)
    q = q_ref[...]
    qc = qc_ref[...]

    def body(j, carry):
        m, l, acc = carry
        off = pl.multiple_of(j * ATT_TK, ATT_TK)
        k = k_ref[pl.ds(off, ATT_TK), :]
        s = lax.dot_general(q, k, (((1,), (1,)), ((), ())), preferred_element_type=F32)
        s = jnp.where(kc_ref[j] <= qc, s, NEG)
        m_new = jnp.maximum(m, jnp.max(s, axis=-1, keepdims=True))
        p = jnp.exp(s - m_new)
        alpha = jnp.exp(m - m_new)
        l = alpha * l + jnp.sum(p, axis=-1, keepdims=True)
        acc = alpha * acc + jnp.dot(p.astype(BF16), v_ref[pl.ds(off, ATT_TK), :], preferred_element_type=F32)
        return m_new, l, acc

    init = (jnp.full((ATT_TQ, 1), NEG, F32), jnp.zeros((ATT_TQ, 1), F32), jnp.zeros((ATT_TQ, V_HEAD), F32))
    _, l, acc = lax.fori_loop(0, hi_ref[b * nq + i], body, init)
    o_ref[...] = (acc / l).astype(o_ref.dtype)


def _attention(q, k, v, chunk_id, bn, l):
    nq = l // ATT_TQ
    nk = l // ATT_TK
    q_max = jnp.max(chunk_id.reshape(bn, nq, ATT_TQ), axis=-1)
    k_min = jnp.min(chunk_id.reshape(bn, nk, ATT_TK), axis=-1)
    needed = k_min[:, None, :] <= q_max[:, :, None]
    hi = jnp.max(jnp.where(needed, jnp.arange(1, nk + 1, dtype=jnp.int32), 0), axis=-1).reshape(bn * nq)
    qc = chunk_id.reshape(bn * l, 1)
    kc = chunk_id.reshape(bn * nk, 1, ATT_TK)
    grid_spec = pltpu.PrefetchScalarGridSpec(
        num_scalar_prefetch=1,
        grid=(bn, MLA_HEADS, nq),
        in_specs=[pl.BlockSpec((ATT_TQ, QK_PAD), lambda b, h, i, hi: (b * nq + i, h)),
                  pl.BlockSpec((l, QK_PAD), lambda b, h, i, hi: (b, h)),
                  pl.BlockSpec((l, V_HEAD), lambda b, h, i, hi: (b, h)),
                  pl.BlockSpec((ATT_TQ, 1), lambda b, h, i, hi: (b * nq + i, 0)),
                  pl.BlockSpec((nk, 1, ATT_TK), lambda b, h, i, hi: (b, 0, 0))],
        out_specs=pl.BlockSpec((ATT_TQ, V_HEAD), lambda b, h, i, hi: (b * nq + i, h)),
    )
    return pl.pallas_call(
        functools.partial(_attn_kernel, nq=nq),
        grid_spec=grid_spec,
        out_shape=jax.ShapeDtypeStruct((bn * l, MLA_HEADS * V_HEAD), BF16),
        compiler_params=_cparams(("arbitrary", "arbitrary", "arbitrary")),
        name="attention",
    )(hi, q, k, v, qc, kc)


def _s5_kernel(u_ref, wb_ref, a_ref, wc_ref, d_ref, o_ref, x_scr, st_scr):
    rows = S5_TC * SUBLANES
    nsub = rows // S5_RB

    @pl.when(pl.program_id(1) == 0)
    def _():
        st_scr[...] = jnp.zeros_like(st_scr)

    even = (lax.broadcasted_iota(jnp.int32, (S5_RB, 1), 0) & 1) == 0

    def mm_in(r, _):
        off = pl.multiple_of(r * S5_RB, S5_RB)
        out = jnp.dot(u_ref[0, pl.ds(off, S5_RB), :].astype(BF16), wb_ref[0], preferred_element_type=F32)
        x_scr[pl.ds(off, S5_RB), :] = jnp.where(even, out[:, :2 * S5_LANES], out[:, 2 * S5_LANES:])
        return 0

    lax.fori_loop(0, nsub, mm_in, 0)

    ar = a_ref[0, :, :S5_LANES]
    ai = a_ref[0, :, S5_LANES:]

    def step(t, carry):
        xr, xi = carry
        off = pl.multiple_of(t * SUBLANES, SUBLANES)
        nr = ar * xr - ai * xi + x_scr[pl.ds(off, SUBLANES), :S5_LANES]
        ni = ar * xi + ai * xr + x_scr[pl.ds(off, SUBLANES), S5_LANES:]
        x_scr[pl.ds(off, SUBLANES), :S5_LANES] = nr
        x_scr[pl.ds(off, SUBLANES), S5_LANES:] = ni
        return nr, ni

    xr, xi = lax.fori_loop(0, S5_TC, step, (st_scr[:, :S5_LANES], st_scr[:, S5_LANES:]), unroll=8)
    st_scr[:, :S5_LANES] = xr
    st_scr[:, S5_LANES:] = xi

    d = jnp.concatenate([d_ref[0]] * (S5_RB // SUBLANES), axis=0)

    def mm_out(r, _):
        off = pl.multiple_of(r * S5_RB, S5_RB)
        out = jnp.dot(x_scr[pl.ds(off, S5_RB), :].astype(BF16), wc_ref[0], preferred_element_type=F32)
        y = jnp.where(even, out[:, :LANES], out[:, LANES:]) + d * u_ref[0, pl.ds(off, S5_RB), :]
        o_ref[0, pl.ds(off, S5_RB), :] = jax.nn.gelu(y).astype(o_ref.dtype)
        return 0

    lax.fori_loop(0, nsub, mm_out, 0)


def _s5_params(a_re, a_im, log_dt, b_re, b_im, c_re, c_im, d_skip):
    step = jnp.exp(log_dt)[:, None]
    mag = jnp.exp(a_re * step)
    abar_re, abar_im = mag * jnp.cos(a_im * step), mag * jnp.sin(a_im * step)
    den = a_re * a_re + a_im * a_im
    nr, ni = abar_re - 1.0, abar_im
    f_re, f_im = (nr * a_re + ni * a_im) / den, (ni * a_re - nr * a_im) / den
    bbar_re = f_re[..., None] * b_re - f_im[..., None] * b_im
    bbar_im = f_re[..., None] * b_im + f_im[..., None] * b_re
    eye = jnp.eye(S5_GB, dtype=F32)
    bb = jnp.stack([bbar_re, bbar_im]).reshape(2, S5_NBLK, S5_GB, S5_STATE, S5_GROUP)
    wb = jnp.einsum('ab,rjapc->jacrbp', eye, bb).reshape(S5_NBLK, LANES, 2 * S5_LANES)
    wb = wb.reshape(S5_PAIRS, 2, LANES, 2 * S5_LANES).transpose(0, 2, 1, 3).reshape(S5_PAIRS, LANES, 4 * S5_LANES)
    cc = jnp.stack([c_re, -c_im]).reshape(2, S5_NBLK, S5_GB, S5_GROUP, S5_STATE)
    wc = jnp.einsum('ab,rjacp->jrapbc', eye, cc).reshape(S5_NBLK, 2 * S5_LANES, LANES)
    wc = wc.reshape(S5_PAIRS, 2, 2 * S5_LANES, LANES).transpose(0, 2, 1, 3).reshape(S5_PAIRS, 2 * S5_LANES, 2 * LANES)
    ab = jnp.concatenate([abar_re.reshape(S5_NBLK, S5_LANES), abar_im.reshape(S5_NBLK, S5_LANES)], axis=1)
    ab = jnp.tile(ab.reshape(S5_PAIRS, 2, 2 * S5_LANES), (1, SUBLANES // 2, 1))
    dd = jnp.tile(d_skip.reshape(S5_PAIRS, 2, LANES), (1, SUBLANES // 2, 1))
    return wb.astype(BF16), ab, wc.astype(BF16), dd


def _s5(u6, wb, ab, wc, dd, l):
    rows = S5_TC * SUBLANES
    return pl.pallas_call(
        _s5_kernel,
        grid=(S5_PAIRS, l // S5_TC),
        in_specs=[pl.BlockSpec((1, rows, LANES), lambda k, c: (k, c, 0)),
                  pl.BlockSpec((1, LANES, 4 * S5_LANES), lambda k, c: (k, 0, 0)),
                  pl.BlockSpec((1, SUBLANES, 2 * S5_LANES), lambda k, c: (k, 0, 0)),
                  pl.BlockSpec((1, 2 * S5_LANES, 2 * LANES), lambda k, c: (k, 0, 0)),
                  pl.BlockSpec((1, SUBLANES, LANES), lambda k, c: (k, 0, 0))],
        out_specs=pl.BlockSpec((1, rows, LANES), lambda k, c: (k, c, 0)),
        out_shape=jax.ShapeDtypeStruct(u6.shape, BF16),
        scratch_shapes=[pltpu.VMEM((rows, 2 * S5_LANES), F32),
                        pltpu.VMEM((SUBLANES, 2 * S5_LANES), F32)],
        compiler_params=_cparams(("arbitrary", "arbitrary")),
        name="s5",
    )(u6, wb, ab, wc, dd)


def _glu_kernel(y_ref, w_ref, o_ref):
    y = jnp.concatenate([y_ref[k] for k in range(S5_PAIRS)], axis=1)
    g = jnp.dot(y, w_ref[...], preferred_element_type=F32)
    o_ref[...] = (y.astype(F32) * jax.nn.sigmoid(g)).astype(o_ref.dtype)


def _glu(y4, w_glu):
    m = y4.shape[1]
    tm = 1024
    return pl.pallas_call(
        _glu_kernel,
        grid=(m // tm,),
        in_specs=[pl.BlockSpec((S5_PAIRS, tm, 2 * LANES), lambda i: (0, i, 0)),
                  pl.BlockSpec((S5_CH, S5_CH), lambda i: (0, 0))],
        out_specs=pl.BlockSpec((tm, S5_CH), lambda i: (i, 0)),
        out_shape=jax.ShapeDtypeStruct((m, S5_CH), BF16),
        compiler_params=_cparams(("arbitrary",)),
        name="glu",
    )(y4, w_glu)


def _merge_kernel(att_ref, z_ref, gm_ref, gs_ref, x_ref, mod_ref, gpost_ref, gpre_ref,
                  wbm_ref, wbs_ref, wo_ref, wrh_ref, wrl_ref, x1_ref, h2_ref, lg_ref):
    ym = jnp.dot(att_ref[...], wbm_ref[...], preferred_element_type=F32)
    ys = jnp.dot(z_ref[...], wbs_ref[...], preferred_element_type=F32)
    mixed_in = (gm_ref[...].astype(F32) * ym + gs_ref[...].astype(F32) * ys).astype(BF16)
    mixed = jnp.dot(mixed_in, wo_ref[...], preferred_element_type=F32)
    x1 = x_ref[0] + mod_ref[0, 2:3, :] * _rms(mixed, gpost_ref[...])
    x1_ref[0] = x1
    h2 = _rms(x1, gpre_ref[...]) * (1.0 + mod_ref[0, 4:5, :]) + mod_ref[0, 3:4, :]
    h2_ref[0] = h2
    h2_hi = h2.astype(BF16)
    h2_lo = (h2 - h2_hi.astype(F32)).astype(BF16)
    nt = (((1,), (1,)), ((), ()))
    lg_ref[...] = (lax.dot_general(wrh_ref[...], h2_hi, nt, preferred_element_type=F32)
                   + lax.dot_general(wrl_ref[...], h2_hi, nt, preferred_element_type=F32)
                   + lax.dot_general(wrh_ref[...], h2_lo, nt, preferred_element_type=F32))


def _merge(att, z_tm, gates, x, mod, g_post, g_pre_ffn, w_br_mla, w_br_s5, w_out, wr_hi, wr_lo):
    bn, l, d = x.shape
    tm = 256
    nl = l // tm
    row = lambda b, i: (b * nl + i, 0)
    const = lambda b, i: (0, 0)
    one = pl.Buffered(1)
    return pl.pallas_call(
        _merge_kernel,
        grid=(bn, nl),
        in_specs=[pl.BlockSpec((tm, MLA_HEADS * V_HEAD), row),
                  pl.BlockSpec((tm, S5_CH), lambda b, i: (i, b)),
                  pl.BlockSpec((tm, d), lambda b, i: (b * nl + i, 0)),
                  pl.BlockSpec((tm, d), lambda b, i: (b * nl + i, 1)),
                  pl.BlockSpec((1, tm, d), lambda b, i: (b, i, 0)),
                  pl.BlockSpec((1, 6, d), lambda b, i: (b, 0, 0)),
                  pl.BlockSpec((1, d), const),
                  pl.BlockSpec((1, d), const),
                  pl.BlockSpec((MLA_HEADS * V_HEAD, d), const, pipeline_mode=one),
                  pl.BlockSpec((S5_CH, d), const, pipeline_mode=one),
                  pl.BlockSpec((d, d), const, pipeline_mode=one),
                  pl.BlockSpec((N_EXPERTS, d), const, pipeline_mode=one),
                  pl.BlockSpec((N_EXPERTS, d), const, pipeline_mode=one)],
        out_specs=[pl.BlockSpec((1, tm, d), lambda b, i: (b, i, 0)),
                   pl.BlockSpec((1, tm, d), lambda b, i: (b, i, 0)),
                   pl.BlockSpec((N_EXPERTS, tm), lambda b, i: (0, b * nl + i))],
        out_shape=[jax.ShapeDtypeStruct((bn, l, d), F32),
                   jax.ShapeDtypeStruct((bn, l, d), F32),
                   jax.ShapeDtypeStruct((N_EXPERTS, bn * l), F32)],
        compiler_params=_cparams(("arbitrary", "arbitrary")),
        name="merge",
    )(att, z_tm, gates, gates, x, mod, g_post.reshape(1, d), g_pre_ffn.reshape(1, d),
      w_br_mla, w_br_s5, w_out, wr_hi, wr_lo)


def _route_kernel(lg_ref, b_ref, ti_ref, tw_ref):
    ng = N_EXPERT_GROUPS
    gsz = N_EXPERTS // ng
    lg = lg_ref[...]
    t = lg.shape[-1]
    sc = jax.nn.sigmoid(lg)
    sel = sc + b_ref[...]
    ninf = -jnp.inf
    i_in = lax.broadcasted_iota(jnp.int32, (ng, gsz, t), 1).astype(F32)
    m1 = jnp.max(sel, axis=1, keepdims=True)
    idx1 = jnp.min(jnp.where(sel == m1, i_in, float(gsz)), axis=1, keepdims=True)
    m2 = jnp.max(jnp.where(i_in == idx1, ninf, sel), axis=1, keepdims=True)
    gs = m1 + m2
    g_i = lax.broadcasted_iota(jnp.int32, (ng, 1, t), 0).astype(F32)
    picked = jnp.zeros((ng, 1, t), F32)
    cur = gs
    for _ in range(TOPK_GROUPS):
        m = jnp.max(cur, axis=0, keepdims=True)
        idx = jnp.min(jnp.where(cur == m, g_i, float(ng)), axis=0, keepdims=True)
        hit = g_i == idx
        picked = jnp.where(hit, 1.0, picked)
        cur = jnp.where(hit, ninf, cur)
    cand = jnp.where(picked > 0.5, sel, ninf)
    e_i = lax.broadcasted_iota(jnp.int32, (ng, gsz, t), 0).astype(F32) * float(gsz) + i_in
    ws = []
    for r in range(TOP_K):
        m = jnp.max(jnp.max(cand, axis=1, keepdims=True), axis=0, keepdims=True)
        idx = jnp.min(jnp.min(jnp.where(cand == m, e_i, float(N_EXPERTS)), axis=1, keepdims=True),
                      axis=0, keepdims=True)
        hit = e_i == idx
        w = jnp.sum(jnp.sum(jnp.where(hit, sc, 0.0), axis=1, keepdims=True), axis=0, keepdims=True)
        ti_ref[r:r + 1, :] = idx[0].astype(jnp.int32)
        ws.append(w[0])
        cand = jnp.where(hit, ninf, cand)
    tot = ws[0]
    for w in ws[1:]:
        tot = tot + w
    for r in range(TOP_K):
        tw_ref[r:r + 1, :] = ws[r] / tot * ROUTED_SCALE
    for r in range(TOP_K, SUBLANES):
        ti_ref[r:r + 1, :] = jnp.zeros((1, t), jnp.int32)
        tw_ref[r:r + 1, :] = jnp.zeros((1, t), F32)


def _route(logits_t, router_bias):
    n = logits_t.shape[1]
    ng = N_EXPERT_GROUPS
    gsz = N_EXPERTS // ng
    tn = 1024
    return pl.pallas_call(
        _route_kernel,
        grid=(n // tn,),
        in_specs=[pl.BlockSpec((ng, gsz, tn), lambda i: (0, 0, i)),
                  pl.BlockSpec((ng, gsz, 1), lambda i: (0, 0, 0))],
        out_specs=[pl.BlockSpec((SUBLANES, tn), lambda i: (0, i)),
                   pl.BlockSpec((SUBLANES, tn), lambda i: (0, i))],
        out_shape=[jax.ShapeDtypeStruct((SUBLANES, n), jnp.int32),
                   jax.ShapeDtypeStruct((SUBLANES, n), F32)],
        compiler_params=_cparams(("arbitrary",)),
        name="route",
    )(logits_t.reshape(ng, gsz, n), router_bias.reshape(ng, gsz, 1))


def _moe_kernel(blk_e_ref, nused_ref, nvalid_ref, tok_ref, tokn_ref, wrow_ref, wslot_ref, h_hbm, wg_ref, wu_ref,
                wd_ref, y_hbm, xbuf, ybuf, wg_s, wu_s, wd_s, gsem, wsem):
    j = pl.program_id(0)
    n_used = nused_ref[0]
    slot = lax.rem(j, 2)
    nv = nvalid_ref[j]
    nv_prev = nvalid_ref[jnp.maximum(j - 1, 0)]

    def gather_copy(idx_ref, r, s):
        return pltpu.make_async_copy(h_hbm.at[pl.ds(idx_ref[0, 0, r], 1), :], xbuf.at[s, pl.ds(r, 1), :], gsem.at[s])

    def write_copy(r):
        return pltpu.make_async_copy(ybuf.at[pl.ds(r, 1), :], y_hbm.at[pl.ds(wrow_ref[0, 0, r], 1), :], wsem.at[0])

    def start_gather(idx_ref, s):
        def body(r, _):
            gather_copy(idx_ref, r, s).start()
            return 0
        lax.fori_loop(0, MOE_TB, body, 0)

    def wait_rows(make, count):
        def body(r, _):
            make(r).wait()
            return 0
        lax.fori_loop(0, count, body, 0)

    @pl.when(j < n_used)
    def _():
        @pl.when(j == 0)
        def _():
            start_gather(tok_ref, 0)

        first = jnp.logical_or(j == 0, blk_e_ref[j] != blk_e_ref[jnp.maximum(j - 1, 0)])

        @pl.when(first)
        def _():
            wg_s[...] = wg_ref[0].astype(BF16)
            wu_s[...] = wu_ref[0].astype(BF16)
            wd_s[...] = wd_ref[0].astype(BF16)

        wait_rows(lambda r: gather_copy(tok_ref, r, slot), MOE_TB)

        @pl.when(j + 1 < n_used)
        def _():
            start_gather(tokn_ref, 1 - slot)

        x = xbuf[slot].astype(BF16)
        g = jnp.dot(x, wg_s[...], preferred_element_type=F32)
        u = jnp.dot(x, wu_s[...], preferred_element_type=F32)
        hm = (g * jax.nn.sigmoid(g) * u).astype(BF16)
        y = jnp.dot(hm, wd_s[...], preferred_element_type=F32) * wslot_ref[...]

        @pl.when(j > 0)
        def _():
            wait_rows(write_copy, nv_prev)

        ybuf[...] = y

        def wbody(r, _):
            write_copy(r).start()
            return 0
        lax.fori_loop(0, nv, wbody, 0)

        @pl.when(j == n_used - 1)
        def _():
            wait_rows(write_copy, nv)


def _moe(h2, topi, topw, w_gate, w_up, w_down):
    n, d = h2.shape
    nk = n * TOP_K
    tb = MOE_TB
    n_blocks = nk // tb + N_EXPERTS
    n_slots = n_blocks * tb
    flat_e = topi[:TOP_K].T.reshape(nk)
    flat_w = topw[:TOP_K].T.reshape(nk)
    order = jnp.argsort(flat_e, stable=True).astype(jnp.int32)
    counts = jnp.sum((flat_e[:, None] == jnp.arange(N_EXPERTS, dtype=jnp.int32)[None, :]).astype(jnp.int32), axis=0)
    padded = (counts + tb - 1) // tb * tb
    start = jnp.cumsum(counts) - counts
    pad_end = jnp.cumsum(padded)
    pad_start = pad_end - padded
    n_used = (pad_end[-1] // tb).astype(jnp.int32).reshape(1)
    blk_e = jnp.minimum(jnp.searchsorted(pad_end, jnp.arange(n_blocks, dtype=jnp.int32) * tb, side='right'),
                        N_EXPERTS - 1).astype(jnp.int32)
    p = jnp.arange(n_slots, dtype=jnp.int32)
    e_p = jnp.repeat(blk_e, tb)
    off = p - pad_start[e_p]
    valid = off < counts[e_p]
    a = order[jnp.clip(start[e_p] + off, 0, nk - 1)]
    tok = jnp.where(valid, a // TOP_K, 0).astype(jnp.int32)
    wrow = jnp.where(valid, (a % TOP_K) * n + a // TOP_K, 0).astype(jnp.int32)
    wslot = jnp.where(valid, flat_w[a], 0.0).reshape(n_slots, 1)
    blk_p0 = jnp.arange(n_blocks, dtype=jnp.int32) * tb
    nvalid = jnp.clip(counts[blk_e] - (blk_p0 - pad_start[blk_e]), 0, tb).astype(jnp.int32)
    tok3 = tok.reshape(n_blocks, 1, tb)
    wrow3 = wrow.reshape(n_blocks, 1, tb)

    smem_blk = lambda f: pl.BlockSpec((1, 1, tb), f, memory_space=pltpu.SMEM)
    grid_spec = pltpu.PrefetchScalarGridSpec(
        num_scalar_prefetch=3,
        grid=(n_blocks,),
        in_specs=[smem_blk(lambda j, be, nu, nv: (j, 0, 0)),
                  smem_blk(lambda j, be, nu, nv: (jnp.minimum(j + 1, n_blocks - 1), 0, 0)),
                  smem_blk(lambda j, be, nu, nv: (j, 0, 0)),
                  pl.BlockSpec((tb, 1), lambda j, be, nu, nv: (j, 0)),
                  pl.BlockSpec(memory_space=pl.ANY),
                  pl.BlockSpec((1, d, D_EXPERT), lambda j, be, nu, nv: (be[j], 0, 0)),
                  pl.BlockSpec((1, d, D_EXPERT), lambda j, be, nu, nv: (be[j], 0, 0)),
                  pl.BlockSpec((1, D_EXPERT, d), lambda j, be, nu, nv: (be[j], 0, 0))],
        out_specs=pl.BlockSpec(memory_space=pl.ANY),
        scratch_shapes=[pltpu.VMEM((2, tb, d), F32),
                        pltpu.VMEM((tb, d), F32),
                        pltpu.VMEM((d, D_EXPERT), BF16),
                        pltpu.VMEM((d, D_EXPERT), BF16),
                        pltpu.VMEM((D_EXPERT, d), BF16),
                        pltpu.SemaphoreType.DMA((2,)),
                        pltpu.SemaphoreType.DMA((1,))],
    )
    return pl.pallas_call(
        _moe_kernel,
        grid_spec=grid_spec,
        out_shape=jax.ShapeDtypeStruct((TOP_K * n, d), F32),
        compiler_params=_cparams(("arbitrary",)),
        name="moe",
    )(blk_e, n_used, nvalid, tok3, tok3, wrow3, wslot, h2, w_gate, w_up, w_down)


def _final_kernel(h_ref, y0, y1, y2, y3, y4, y5, x1_ref, mod_ref, g_ref, wg_ref, wu_ref, wd_ref, o_ref):
    h = h_ref[...].astype(BF16)
    g = jnp.dot(h, wg_ref[...], preferred_element_type=F32)
    u = jnp.dot(h, wu_ref[...], preferred_element_type=F32)
    hm = (g * jax.nn.sigmoid(g) * u).astype(BF16)
    ffn = jnp.dot(hm, wd_ref[...], preferred_element_type=F32)
    for y in (y0, y1, y2, y3, y4, y5):
        ffn = ffn + y[...]
    o_ref[0] = x1_ref[0] + mod_ref[0, 5:6, :] * _rms(ffn, g_ref[...])


def _final(h2, y_flat, x1, mod, g_post_ffn, w_sg, w_su, w_sd):
    bn, l, d = x1.shape
    n = bn * l
    tm = 256
    nl = l // tm
    const = lambda b, i: (0, 0)
    one = pl.Buffered(1)
    y_specs = [pl.BlockSpec((tm, d), functools.partial(lambda b, i, k: (k * (n // tm) + b * nl + i, 0), k=k))
               for k in range(TOP_K)]
    return pl.pallas_call(
        _final_kernel,
        grid=(bn, nl),
        in_specs=[pl.BlockSpec((tm, d), lambda b, i: (b * nl + i, 0))] + y_specs + [
            pl.BlockSpec((1, tm, d), lambda b, i: (b, i, 0)),
            pl.BlockSpec((1, 6, d), lambda b, i: (b, 0, 0)),
            pl.BlockSpec((1, d), const),
            pl.BlockSpec((d, D_EXPERT), const, pipeline_mode=one),
            pl.BlockSpec((d, D_EXPERT), const, pipeline_mode=one),
            pl.BlockSpec((D_EXPERT, d), const, pipeline_mode=one)],
        out_specs=pl.BlockSpec((1, tm, d), lambda b, i: (b, i, 0)),
        out_shape=jax.ShapeDtypeStruct((bn, l, d), F32),
        compiler_params=_cparams(("arbitrary", "arbitrary")),
        name="final",
    )(h2, *([y_flat] * TOP_K), x1, mod, g_post_ffn.reshape(1, d), w_sg, w_su, w_sd)


def _layer(x, c, positions, w_ada, b_ada, g_pre_mix, g_post_mix, g_pre_ffn, g_post_ffn, w_in, g_q, g_kv,
           w_uq, w_uk, w_uv, a_re, a_im, log_dt, b_re, b_im, c_re, c_im, d_skip, w_glu, w_br_mla, w_br_s5,
           w_out, w_router, router_bias, w_exp_gate, w_exp_up, w_exp_down, w_sh_gate, w_sh_up, w_sh_down):
    bn, l, d = x.shape
    n = bn * l
    mod = _adaln(c, w_ada, b_ada).reshape(bn, 6, d)

    o_kpe = Q_LORA + KV_LORA
    o_u = o_kpe + QK_ROPE
    o_g = o_u + S5_CH
    w_lat = w_in[:, :o_kpe].astype(BF16)
    w_kpe = jnp.pad(w_in[:, o_kpe:o_u], ((0, 0), (0, LANES - QK_ROPE))).astype(BF16)
    w_u = w_in[:, o_u:o_g].astype(BF16)
    w_gates = w_in[:, o_g:].astype(BF16)
    w_q = jnp.pad(w_uq.reshape(Q_LORA, MLA_HEADS, QK_NOPE + QK_ROPE),
                  ((0, 0), (0, 0), (0, QK_PAD - QK_NOPE - QK_ROPE))).reshape(Q_LORA, MLA_HEADS * QK_PAD).astype(BF16)
    wr_t = w_router.T
    wr_hi = wr_t.astype(BF16)
    wr_lo = (wr_t - wr_hi.astype(F32)).astype(BF16)

    half = QK_ROPE // 2
    inv_freq = ROPE_THETA ** (-jnp.arange(half, dtype=F32) / half)
    ang = positions.astype(F32).reshape(n, 1) * inv_freq
    cos, sin = jnp.cos(ang), jnp.sin(ang)
    zh = jnp.zeros((n, half), F32)
    z2 = jnp.zeros((n, LANES - QK_ROPE), F32)
    rope_c = jnp.concatenate([cos, cos, z2], axis=1)
    rope_s1 = jnp.concatenate([-sin, zh, z2], axis=1)
    rope_s2 = jnp.concatenate([zh, sin, z2], axis=1)
    chunk_id = positions // CHUNK

    h = _prenorm(x, mod, g_pre_mix, 0, 1).reshape(n, d)
    lat = _mm(h, w_lat, F32, 1024, 512, name="mm_lat")
    kpe = _mm(h, w_kpe, F32, 1024, LANES, name="mm_kpe")
    gates = _mm(h, w_gates, BF16, 1024, 512, act="sigmoid", name="mm_gates")
    u6 = _mm_u(h, w_u, bn, l).reshape(S5_PAIRS, l * SUBLANES, LANES)

    q = _qproj(lat, g_q, w_q, rope_c, rope_s1, rope_s2)
    k, v = _kvproj(lat, g_kv, w_uk.astype(BF16), w_uv.astype(BF16), kpe, rope_c, rope_s1, rope_s2)
    att = _attention(q, k, v, chunk_id, bn, l)

    wb, ab, wc, dd = _s5_params(a_re, a_im, log_dt, b_re, b_im, c_re, c_im, d_skip)
    y6 = _s5(u6, wb, ab, wc, dd, l)
    z = _glu(y6.reshape(S5_PAIRS, l * bn, 2 * LANES), w_glu.astype(BF16))
    z_tm = z.reshape(l, bn * S5_CH)

    x1, h2, logits_t = _merge(att, z_tm, gates, x, mod, g_post_mix, g_pre_ffn, w_br_mla.astype(BF16),
                              w_br_s5.astype(BF16), w_out.astype(BF16), wr_hi, wr_lo)

    topi, topw = _route(logits_t, router_bias)
    h2f = h2.reshape(n, d)
    y_flat = _moe(h2f, topi, topw, w_exp_gate, w_exp_up, w_exp_down)
    return _final(h2f, y_flat, x1, mod, g_post_ffn, w_sh_gate.astype(BF16), w_sh_up.astype(BF16),
                  w_sh_down.astype(BF16))


def kernel(x, c, positions, w_ada, b_ada, g_pre_mix, g_post_mix, g_pre_ffn, g_post_ffn, w_in, g_q, g_kv, w_uq, w_uk, w_uv, a_re, a_im, log_dt, b_re, b_im, c_re, c_im, d_skip, w_glu, w_br_mla, w_br_s5, w_out, w_router, router_bias, w_exp_gate, w_exp_up, w_exp_down, w_sh_gate, w_sh_up, w_sh_down):
    depth = w_ada.shape[0]
    for li in range(depth):
        x = _layer(x, c, positions, w_ada[li], b_ada[li], g_pre_mix[li], g_post_mix[li], g_pre_ffn[li],
                   g_post_ffn[li], w_in[li], g_q[li], g_kv[li], w_uq[li], w_uk[li], w_uv[li], a_re[li], a_im[li],
                   log_dt[li], b_re[li], b_im[li], c_re[li], c_im[li], d_skip[li], w_glu[li], w_br_mla[li],
                   w_br_s5[li], w_out[li], w_router[li], router_bias[li], w_exp_gate[li], w_exp_up[li],
                   w_exp_down[li], w_sh_gate[li], w_sh_up[li], w_sh_down[li])
    return x
```

```python
import functools

import jax
import jax.numpy as jnp
from jax import lax
from jax.experimental import pallas as pl
from jax.experimental.pallas import tpu as pltpu

F32 = jnp.float32
BF16 = jnp.bfloat16

D_MODEL = 2048
CHUNK = 64
EPS = 1e-6
MLA_HEADS = 8
QK_NOPE = 128
QK_ROPE = 64
V_HEAD = 128
Q_LORA = 512
KV_LORA = 512
ROPE_THETA = 10000.0
S5_CH = 1024
S5_GROUP = 16
S5_GROUPS = S5_CH // S5_GROUP
S5_STATE = 64
N_EXPERTS = 64
TOP_K = 6
N_EXPERT_GROUPS = 8
TOPK_GROUPS = 4
D_EXPERT = 512
ROUTED_SCALE = 2.5

LANES = 128
SUBLANES = 8
QK_PAD = 2 * LANES
VMEM_LIMIT = 56 * 1024 * 1024
NEG = -1e30

S5_GB = LANES // S5_GROUP
S5_NBLK = S5_CH // LANES
S5_PAIRS = S5_NBLK // 2
S5_LANES = S5_GB * S5_STATE
S5_TC = 256
S5_RB = 512

ATT_TQ = 256
ATT_TK = 256
MOE_TB = 256


def _cparams(sem):
    return pltpu.CompilerParams(dimension_semantics=sem, vmem_limit_bytes=VMEM_LIMIT)


def _rms(x, g):
    return x * lax.rsqrt(jnp.mean(x * x, axis=-1, keepdims=True) + EPS) * g


def _adaln_kernel(c_ref, w_ref, b_ref, o_ref):
    c = c_ref[...]
    a = (c * jax.nn.sigmoid(c)).astype(BF16)
    o_ref[...] = jnp.dot(a, w_ref[...].astype(BF16), preferred_element_type=F32) + b_ref[...]


def _adaln(c, w, b):
    bn, d = c.shape
    n = w.shape[1]
    tn = 1024
    return pl.pallas_call(
        _adaln_kernel,
        grid=(n // tn,),
        in_specs=[pl.BlockSpec((bn, d), lambda j: (0, 0)),
                  pl.BlockSpec((d, tn), lambda j: (0, j)),
                  pl.BlockSpec((1, tn), lambda j: (0, j))],
        out_specs=pl.BlockSpec((bn, tn), lambda j: (0, j)),
        out_shape=jax.ShapeDtypeStruct((bn, n), F32),
        compiler_params=_cparams(("arbitrary",)),
        name="adaln",
    )(c, w, b.reshape(1, n))


def _prenorm_kernel(x_ref, mod_ref, g_ref, o_ref, *, sh_row, sc_row):
    y = _rms(x_ref[0], g_ref[...])
    o_ref[0] = (y * (1.0 + mod_ref[0, sc_row:sc_row + 1, :]) + mod_ref[0, sh_row:sh_row + 1, :]).astype(o_ref.dtype)


def _prenorm(x, mod, g, sh_row, sc_row):
    bn, l, d = x.shape
    tl = 512
    return pl.pallas_call(
        functools.partial(_prenorm_kernel, sh_row=sh_row, sc_row=sc_row),
        grid=(bn, l // tl),
        in_specs=[pl.BlockSpec((1, tl, d), lambda b, i: (b, i, 0)),
                  pl.BlockSpec((1, 6, d), lambda b, i: (b, 0, 0)),
                  pl.BlockSpec((1, d), lambda b, i: (0, 0))],
        out_specs=pl.BlockSpec((1, tl, d), lambda b, i: (b, i, 0)),
        out_shape=jax.ShapeDtypeStruct((bn, l, d), BF16),
        compiler_params=_cparams(("arbitrary", "arbitrary")),
        name="prenorm",
    )(x, mod, g.reshape(1, d))


def _mm_kernel(a_ref, w_ref, o_ref, *, act):
    acc = jnp.dot(a_ref[...], w_ref[...], preferred_element_type=F32)
    if act == "sigmoid":
        acc = jax.nn.sigmoid(acc)
    o_ref[...] = acc.astype(o_ref.dtype)


def _mm(a, w, out_dtype, tm, tn, act=None, name="mm"):
    m, k = a.shape
    n = w.shape[1]
    return pl.pallas_call(
        functools.partial(_mm_kernel, act=act),
        grid=(n // tn, m // tm),
        in_specs=[pl.BlockSpec((tm, k), lambda j, i: (i, 0)),
                  pl.BlockSpec((k, tn), lambda j, i: (0, j))],
        out_specs=pl.BlockSpec((tm, tn), lambda j, i: (i, j)),
        out_shape=jax.ShapeDtypeStruct((m, n), out_dtype),
        compiler_params=_cparams(("arbitrary", "arbitrary")),
        name=name,
    )(a, w)


def _mm_u_kernel(a_ref, w_ref, o_ref):
    o_ref[0] = jnp.dot(a_ref[...], w_ref[...], preferred_element_type=F32)


def _mm_u(h, w, bn, l):
    m, k = h.shape
    tm = 1024
    nl = l // tm
    tn = 2 * LANES
    return pl.pallas_call(
        _mm_u_kernel,
        grid=(S5_PAIRS, bn, nl),
        in_specs=[pl.BlockSpec((tm, k), lambda j, b, i: (b * nl + i, 0)),
                  pl.BlockSpec((k, tn), lambda j, b, i: (0, j))],
        out_specs=pl.BlockSpec((1, tm, tn), lambda j, b, i: (j, i, b)),
        out_shape=jax.ShapeDtypeStruct((S5_PAIRS, l, bn * tn), F32),
        compiler_params=_cparams(("arbitrary", "arbitrary", "arbitrary")),
        name="mm_u",
    )(h, w)


def _rope_tab_kernel(pos_ref, k_ref, c_ref, s1_ref, s2_ref):
    ang = pos_ref[...].astype(F32) * k_ref[0:1, :]
    s = jnp.sin(ang)
    c_ref[...] = jnp.cos(ang) * k_ref[1:2, :]
    s1_ref[...] = s * k_ref[2:3, :]
    s2_ref[...] = s * k_ref[3:4, :]


def _rope_tables(positions):
    n = positions.size
    half = QK_ROPE // 2
    inv_freq = ROPE_THETA ** (-jnp.arange(half, dtype=F32) / half)
    zh, oh = jnp.zeros((half,), F32), jnp.ones((half,), F32)
    z2 = jnp.zeros((LANES - QK_ROPE,), F32)
    rows = [jnp.concatenate([inv_freq, inv_freq, z2]), jnp.concatenate([oh, oh, z2]),
            jnp.concatenate([-oh, zh, z2]), jnp.concatenate([zh, oh, z2])]
    consts = jnp.stack(rows + [jnp.zeros((LANES,), F32)] * (SUBLANES - len(rows)))
    tm = 1024
    tab = jax.ShapeDtypeStruct((n, LANES), F32)
    return pl.pallas_call(
        _rope_tab_kernel,
        grid=(n // tm,),
        in_specs=[pl.BlockSpec((tm, 1), lambda i: (i, 0)),
                  pl.BlockSpec((SUBLANES, LANES), lambda i: (0, 0))],
        out_specs=[pl.BlockSpec((tm, LANES), lambda i: (i, 0))] * 3,
        out_shape=[tab, tab, tab],
        compiler_params=_cparams(("arbitrary",)),
        name="rope_tables",
    )(positions.reshape(n, 1), consts)


def _rope_tile(t, c_ref, s1_ref, s2_ref):
    return (t * c_ref[...] + pltpu.roll(t, LANES - QK_ROPE // 2, 1) * s1_ref[...]
            + pltpu.roll(t, QK_ROPE // 2, 1) * s2_ref[...])


def _qproj_kernel(lat_ref, g_ref, w_ref, c_ref, s1_ref, s2_ref, o_ref, *, scale):
    xn = _rms(lat_ref[...], g_ref[...]).astype(BF16)
    q = jnp.dot(xn, w_ref[...], preferred_element_type=F32)
    t = _rope_tile(q[:, LANES:], c_ref, s1_ref, s2_ref)
    o_ref[...] = (jnp.concatenate([q[:, :LANES], t], axis=1) * scale).astype(o_ref.dtype)


def _qproj(lat, g_q, w_q, rope_c, rope_s1, rope_s2):
    n = lat.shape[0]
    tm = 1024
    tab = pl.BlockSpec((tm, LANES), lambda h, i: (i, 0))
    return pl.pallas_call(
        functools.partial(_qproj_kernel, scale=(QK_NOPE + QK_ROPE) ** -0.5),
        grid=(MLA_HEADS, n // tm),
        in_specs=[pl.BlockSpec((tm, Q_LORA), lambda h, i: (i, 0)),
                  pl.BlockSpec((1, Q_LORA), lambda h, i: (0, 0)),
                  pl.BlockSpec((Q_LORA, QK_PAD), lambda h, i: (0, h)),
                  tab, tab, tab],
        out_specs=pl.BlockSpec((tm, QK_PAD), lambda h, i: (i, h)),
        out_shape=jax.ShapeDtypeStruct((n, MLA_HEADS * QK_PAD), BF16),
        compiler_params=_cparams(("arbitrary", "arbitrary")),
        name="qproj",
    )(lat, g_q.reshape(1, Q_LORA), w_q, rope_c, rope_s1, rope_s2)


def _kvproj_kernel(lat_ref, g_ref, wk_ref, wv_ref, kpe_ref, c_ref, s1_ref, s2_ref, k_ref, v_ref):
    xn = _rms(lat_ref[...], g_ref[...]).astype(BF16)
    kn = jnp.dot(xn, wk_ref[...], preferred_element_type=F32)
    t = _rope_tile(kpe_ref[...], c_ref, s1_ref, s2_ref)
    k_ref[...] = jnp.concatenate([kn, t], axis=1).astype(k_ref.dtype)
    v_ref[...] = jnp.dot(xn, wv_ref[...], preferred_element_type=F32).astype(v_ref.dtype)


def _kvproj(lat, g_kv, w_k, w_v, kpe, rope_c, rope_s1, rope_s2):
    n = lat.shape[0]
    tm = 1024
    tab = pl.BlockSpec((tm, LANES), lambda h, i: (i, 0))
    return pl.pallas_call(
        _kvproj_kernel,
        grid=(MLA_HEADS, n // tm),
        in_specs=[pl.BlockSpec((tm, KV_LORA), lambda h, i: (i, 1)),
                  pl.BlockSpec((1, KV_LORA), lambda h, i: (0, 0)),
                  pl.BlockSpec((KV_LORA, QK_NOPE), lambda h, i: (0, h)),
                  pl.BlockSpec((KV_LORA, V_HEAD), lambda h, i: (0, h)),
                  tab, tab, tab, tab],
        out_specs=[pl.BlockSpec((tm, QK_PAD), lambda h, i: (i, h)),
                   pl.BlockSpec((tm, V_HEAD), lambda h, i: (i, h))],
        out_shape=[jax.ShapeDtypeStruct((n, MLA_HEADS * QK_PAD), BF16),
                   jax.ShapeDtypeStruct((n, MLA_HEADS * V_HEAD), BF16)],
        compiler_params=_cparams(("arbitrary", "arbitrary")),
        name="kvproj",
    )(lat, g_kv.reshape(1, KV_LORA), w_k, w_v, kpe, rope_c, rope_s1, rope_s2)


def _attn_kernel(hi_ref, q_ref, k_ref, v_ref, qc_ref, kc_ref, o_ref, m_scr, l_scr, acc_scr, *, nq):
    b = pl.program_id(0)
    i = pl.program_id(1)
    qc = qc_ref[...]
    m_scr[...] = jnp.full(m_scr.shape, NEG, F32)
    l_scr[...] = jnp.zeros(l_scr.shape, F32)
    acc_scr[...] = jnp.zeros(acc_scr.shape, F32)

    def body(j, _):
        off = pl.multiple_of(j * ATT_TK, ATT_TK)
        mask = kc_ref[j] <= qc
        for h in range(MLA_HEADS):
            q = q_ref[:, h * QK_PAD:(h + 1) * QK_PAD]
            k = k_ref[pl.ds(off, ATT_TK), h * QK_PAD:(h + 1) * QK_PAD]
            s = lax.dot_general(q, k, (((1,), (1,)), ((), ())), preferred_element_type=F32)
            s = jnp.where(mask, s, NEG)
            m_old = m_scr[h]
            m_new = jnp.maximum(m_old, jnp.max(s, axis=-1, keepdims=True))
            p = jnp.exp(s - m_new)
            alpha = jnp.exp(m_old - m_new)
            l_scr[h] = alpha * l_scr[h] + jnp.sum(p, axis=-1, keepdims=True)
            v = v_ref[pl.ds(off, ATT_TK), h * V_HEAD:(h + 1) * V_HEAD]
            acc_scr[h] = alpha * acc_scr[h] + jnp.dot(p.astype(BF16), v, preferred_element_type=F32)
            m_scr[h] = m_new
        return 0

    lax.fori_loop(0, hi_ref[b * nq + i], body, 0)
    for h in range(MLA_HEADS):
        o_ref[:, h * V_HEAD:(h + 1) * V_HEAD] = (acc_scr[h] / l_scr[h]).astype(o_ref.dtype)


def _attention(q, k, v, chunk_id, bn, l):
    nq = l // ATT_TQ
    nk = l // ATT_TK
    q_max = jnp.max(chunk_id.reshape(bn, nq, ATT_TQ), axis=-1)
    k_min = jnp.min(chunk_id.reshape(bn, nk, ATT_TK), axis=-1)
    needed = k_min[:, None, :] <= q_max[:, :, None]
    hi = jnp.max(jnp.where(needed, jnp.arange(1, nk + 1, dtype=jnp.int32), 0), axis=-1).reshape(bn * nq)
    qc = chunk_id.reshape(bn * l, 1)
    kc = chunk_id.reshape(bn * nk, 1, ATT_TK)
    grid_spec = pltpu.PrefetchScalarGridSpec(
        num_scalar_prefetch=1,
        grid=(bn, nq),
        in_specs=[pl.BlockSpec((ATT_TQ, MLA_HEADS * QK_PAD), lambda b, i, hi: (b * nq + i, 0)),
                  pl.BlockSpec((l, MLA_HEADS * QK_PAD), lambda b, i, hi: (b, 0)),
                  pl.BlockSpec((l, MLA_HEADS * V_HEAD), lambda b, i, hi: (b, 0)),
                  pl.BlockSpec((ATT_TQ, 1), lambda b, i, hi: (b * nq + i, 0)),
                  pl.BlockSpec((nk, 1, ATT_TK), lambda b, i, hi: (b, 0, 0))],
        out_specs=pl.BlockSpec((ATT_TQ, MLA_HEADS * V_HEAD), lambda b, i, hi: (b * nq + i, 0)),
        scratch_shapes=[pltpu.VMEM((MLA_HEADS, ATT_TQ, 1), F32),
                        pltpu.VMEM((MLA_HEADS, ATT_TQ, 1), F32),
                        pltpu.VMEM((MLA_HEADS, ATT_TQ, V_HEAD), F32)],
    )
    return pl.pallas_call(
        functools.partial(_attn_kernel, nq=nq),
        grid_spec=grid_spec,
        out_shape=jax.ShapeDtypeStruct((bn * l, MLA_HEADS * V_HEAD), BF16),
        compiler_params=_cparams(("arbitrary", "arbitrary")),
        name="attention",
    )(hi, q, k, v, qc, kc)


def _s5_kernel(u_ref, wb_ref, a_ref, wc_ref, d_ref, o_ref, x_scr, st_scr):
    rows = S5_TC * SUBLANES
    nsub = rows // S5_RB

    @pl.when(pl.program_id(1) == 0)
    def _():
        st_scr[...] = jnp.zeros_like(st_scr)

    even = (lax.broadcasted_iota(jnp.int32, (S5_RB, 1), 0) & 1) == 0

    def mm_in(r, _):
        off = pl.multiple_of(r * S5_RB, S5_RB)
        out = jnp.dot(u_ref[0, pl.ds(off, S5_RB), :].astype(BF16), wb_ref[0], preferred_element_type=F32)
        x_scr[pl.ds(off, S5_RB), :] = jnp.where(even, out[:, :2 * S5_LANES], out[:, 2 * S5_LANES:])
        return 0

    lax.fori_loop(0, nsub, mm_in, 0)

    ar = a_ref[0, :, :S5_LANES]
    ai = a_ref[0, :, S5_LANES:]

    def step(t, carry):
        xr, xi = carry
        off = pl.multiple_of(t * SUBLANES, SUBLANES)
        nr = ar * xr - ai * xi + x_scr[pl.ds(off, SUBLANES), :S5_LANES]
        ni = ar * xi + ai * xr + x_scr[pl.ds(off, SUBLANES), S5_LANES:]
        x_scr[pl.ds(off, SUBLANES), :S5_LANES] = nr
        x_scr[pl.ds(off, SUBLANES), S5_LANES:] = ni
        return nr, ni

    xr, xi = lax.fori_loop(0, S5_TC, step, (st_scr[:, :S5_LANES], st_scr[:, S5_LANES:]), unroll=8)
    st_scr[:, :S5_LANES] = xr
    st_scr[:, S5_LANES:] = xi

    d = jnp.concatenate([d_ref[0]] * (S5_RB // SUBLANES), axis=0)

    def mm_out(r, _):
        off = pl.multiple_of(r * S5_RB, S5_RB)
        out = jnp.dot(x_scr[pl.ds(off, S5_RB), :].astype(BF16), wc_ref[0], preferred_element_type=F32)
        y = jnp.where(even, out[:, :LANES], out[:, LANES:]) + d * u_ref[0, pl.ds(off, S5_RB), :]
        o_ref[0, pl.ds(off, S5_RB), :] = jax.nn.gelu(y).astype(o_ref.dtype)
        return 0

    lax.fori_loop(0, nsub, mm_out, 0)


def _s5_params(a_re, a_im, log_dt, b_re, b_im, c_re, c_im, d_skip):
    step = jnp.exp(log_dt)[:, None]
    mag = jnp.exp(a_re * step)
    abar_re, abar_im = mag * jnp.cos(a_im * step), mag * jnp.sin(a_im * step)
    den = a_re * a_re + a_im * a_im
    nr, ni = abar_re - 1.0, abar_im
    f_re, f_im = (nr * a_re + ni * a_im) / den, (ni * a_re - nr * a_im) / den
    bbar_re = f_re[..., None] * b_re - f_im[..., None] * b_im
    bbar_im = f_re[..., None] * b_im + f_im[..., None] * b_re
    eye = jnp.eye(S5_GB, dtype=F32)
    bb = jnp.stack([bbar_re, bbar_im]).reshape(2, S5_NBLK, S5_GB, S5_STATE, S5_GROUP)
    wb = jnp.einsum('ab,rjapc->jacrbp', eye, bb).reshape(S5_NBLK, LANES, 2 * S5_LANES)
    wb = wb.reshape(S5_PAIRS, 2, LANES, 2 * S5_LANES).transpose(0, 2, 1, 3).reshape(S5_PAIRS, LANES, 4 * S5_LANES)
    cc = jnp.stack([c_re, -c_im]).reshape(2, S5_NBLK, S5_GB, S5_GROUP, S5_STATE)
    wc = jnp.einsum('ab,rjacp->jrapbc', eye, cc).reshape(S5_NBLK, 2 * S5_LANES, LANES)
    wc = wc.reshape(S5_PAIRS, 2, 2 * S5_LANES, LANES).transpose(0, 2, 1, 3).reshape(S5_PAIRS, 2 * S5_LANES, 2 * LANES)
    ab = jnp.concatenate([abar_re.reshape(S5_NBLK, S5_LANES), abar_im.reshape(S5_NBLK, S5_LANES)], axis=1)
    ab = jnp.tile(ab.reshape(S5_PAIRS, 2, 2 * S5_LANES), (1, SUBLANES // 2, 1))
    dd = jnp.tile(d_skip.reshape(S5_PAIRS, 2, LANES), (1, SUBLANES // 2, 1))
    return wb.astype(BF16), ab, wc.astype(BF16), dd


def _s5(u6, wb, ab, wc, dd, l):
    rows = S5_TC * SUBLANES
    return pl.pallas_call(
        _s5_kernel,
        grid=(S5_PAIRS, l // S5_TC),
        in_specs=[pl.BlockSpec((1, rows, LANES), lambda k, c: (k, c, 0)),
                  pl.BlockSpec((1, LANES, 4 * S5_LANES), lambda k, c: (k, 0, 0)),
                  pl.BlockSpec((1, SUBLANES, 2 * S5_LANES), lambda k, c: (k, 0, 0)),
                  pl.BlockSpec((1, 2 * S5_LANES, 2 * LANES), lambda k, c: (k, 0, 0)),
                  pl.BlockSpec((1, SUBLANES, LANES), lambda k, c: (k, 0, 0))],
        out_specs=pl.BlockSpec((1, rows, LANES), lambda k, c: (k, c, 0)),
        out_shape=jax.ShapeDtypeStruct(u6.shape, BF16),
        scratch_shapes=[pltpu.VMEM((rows, 2 * S5_LANES), F32),
                        pltpu.VMEM((SUBLANES, 2 * S5_LANES), F32)],
        compiler_params=_cparams(("arbitrary", "arbitrary")),
        name="s5",
    )(u6, wb, ab, wc, dd)


def _glu_kernel(y_ref, w_ref, o_ref):
    y = jnp.concatenate([y_ref[k] for k in range(S5_PAIRS)], axis=1)
    g = jnp.dot(y, w_ref[...], preferred_element_type=F32)
    o_ref[...] = (y.astype(F32) * jax.nn.sigmoid(g)).astype(o_ref.dtype)


def _glu(y4, w_glu):
    m = y4.shape[1]
    tm = 1024
    return pl.pallas_call(
        _glu_kernel,
        grid=(m // tm,),
        in_specs=[pl.BlockSpec((S5_PAIRS, tm, 2 * LANES), lambda i: (0, i, 0)),
                  pl.BlockSpec((S5_CH, S5_CH), lambda i: (0, 0))],
        out_specs=pl.BlockSpec((tm, S5_CH), lambda i: (i, 0)),
        out_shape=jax.ShapeDtypeStruct((m, S5_CH), BF16),
        compiler_params=_cparams(("arbitrary",)),
        name="glu",
    )(y4, w_glu)


def _merge_kernel(att_ref, z_ref, gm_ref, gs_ref, x_ref, mod_ref, gpost_ref, gpre_ref,
                  wbm_ref, wbs_ref, wo_ref, wrh_ref, wrl_ref, x1_ref, h2_ref, lg_ref):
    ym = jnp.dot(att_ref[...], wbm_ref[...], preferred_element_type=F32)
    ys = jnp.dot(z_ref[...], wbs_ref[...], preferred_element_type=F32)
    mixed_in = (gm_ref[...].astype(F32) * ym + gs_ref[...].astype(F32) * ys).astype(BF16)
    mixed = jnp.dot(mixed_in, wo_ref[...], preferred_element_type=F32)
    x1 = x_ref[0] + mod_ref[0, 2:3, :] * _rms(mixed, gpost_ref[...])
    x1_ref[0] = x1
    h2 = _rms(x1, gpre_ref[...]) * (1.0 + mod_ref[0, 4:5, :]) + mod_ref[0, 3:4, :]
    h2_ref[0] = h2
    h2_hi = h2.astype(BF16)
    h2_lo = (h2 - h2_hi.astype(F32)).astype(BF16)
    nt = (((1,), (1,)), ((), ()))
    lg_ref[...] = (lax.dot_general(wrh_ref[...], h2_hi, nt, preferred_element_type=F32)
                   + lax.dot_general(wrl_ref[...], h2_hi, nt, preferred_element_type=F32)
                   + lax.dot_general(wrh_ref[...], h2_lo, nt, preferred_element_type=F32))


def _merge(att, z_tm, gates, x, mod, g_post, g_pre_ffn, w_br_mla, w_br_s5, w_out, wr_hi, wr_lo):
    bn, l, d = x.shape
    tm = 256
    nl = l // tm
    row = lambda b, i: (b * nl + i, 0)
    const = lambda b, i: (0, 0)
    one = pl.Buffered(1)
    return pl.pallas_call(
        _merge_kernel,
        grid=(bn, nl),
        in_specs=[pl.BlockSpec((tm, MLA_HEADS * V_HEAD), row),
                  pl.BlockSpec((tm, S5_CH), lambda b, i: (i, b)),
                  pl.BlockSpec((tm, d), lambda b, i: (b * nl + i, 0)),
                  pl.BlockSpec((tm, d), lambda b, i: (b * nl + i, 1)),
                  pl.BlockSpec((1, tm, d), lambda b, i: (b, i, 0)),
                  pl.BlockSpec((1, 6, d), lambda b, i: (b, 0, 0)),
                  pl.BlockSpec((1, d), const),
                  pl.BlockSpec((1, d), const),
                  pl.BlockSpec((MLA_HEADS * V_HEAD, d), const, pipeline_mode=one),
                  pl.BlockSpec((S5_CH, d), const, pipeline_mode=one),
                  pl.BlockSpec((d, d), const, pipeline_mode=one),
                  pl.BlockSpec((N_EXPERTS, d), const, pipeline_mode=one),
                  pl.BlockSpec((N_EXPERTS, d), const, pipeline_mode=one)],
        out_specs=[pl.BlockSpec((1, tm, d), lambda b, i: (b, i, 0)),
                   pl.BlockSpec((1, tm, d), lambda b, i: (b, i, 0)),
                   pl.BlockSpec((N_EXPERTS, tm), lambda b, i: (0, b * nl + i))],
        out_shape=[jax.ShapeDtypeStruct((bn, l, d), F32),
                   jax.ShapeDtypeStruct((bn, l, d), F32),
                   jax.ShapeDtypeStruct((N_EXPERTS, bn * l), F32)],
        compiler_params=_cparams(("arbitrary", "arbitrary")),
        name="merge",
    )(att, z_tm, gates, gates, x, mod, g_post.reshape(1, d), g_pre_ffn.reshape(1, d),
      w_br_mla, w_br_s5, w_out, wr_hi, wr_lo)


def _route_kernel(lg_ref, b_ref, ti_ref, tw_ref):
    ng = N_EXPERT_GROUPS
    gsz = N_EXPERTS // ng
    lg = lg_ref[...]
    t = lg.shape[-1]
    sc = jax.nn.sigmoid(lg)
    sel = sc + b_ref[...]
    ninf = -jnp.inf
    i_in = lax.broadcasted_iota(jnp.int32, (ng, gsz, t), 1).astype(F32)
    m1 = jnp.max(sel, axis=1, keepdims=True)
    idx1 = jnp.min(jnp.where(sel == m1, i_in, float(gsz)), axis=1, keepdims=True)
    m2 = jnp.max(jnp.where(i_in == idx1, ninf, sel), axis=1, keepdims=True)
    gs = m1 + m2
    g_i = lax.broadcasted_iota(jnp.int32, (ng, 1, t), 0).astype(F32)
    picked = jnp.zeros((ng, 1, t), F32)
    cur = gs
    for _ in range(TOPK_GROUPS):
        m = jnp.max(cur, axis=0, keepdims=True)
        idx = jnp.min(jnp.where(cur == m, g_i, float(ng)), axis=0, keepdims=True)
        hit = g_i == idx
        picked = jnp.where(hit, 1.0, picked)
        cur = jnp.where(hit, ninf, cur)
    cand = jnp.where(picked > 0.5, sel, ninf)
    e_i = lax.broadcasted_iota(jnp.int32, (ng, gsz, t), 0).astype(F32) * float(gsz) + i_in
    ws = []
    for r in range(TOP_K):
        m = jnp.max(jnp.max(cand, axis=1, keepdims=True), axis=0, keepdims=True)
        idx = jnp.min(jnp.min(jnp.where(cand == m, e_i, float(N_EXPERTS)), axis=1, keepdims=True),
                      axis=0, keepdims=True)
        hit = e_i == idx
        w = jnp.sum(jnp.sum(jnp.where(hit, sc, 0.0), axis=1, keepdims=True), axis=0, keepdims=True)
        ti_ref[r:r + 1, :] = idx[0].astype(jnp.int32)
        ws.append(w[0])
        cand = jnp.where(hit, ninf, cand)
    tot = ws[0]
    for w in ws[1:]:
        tot = tot + w
    for r in range(TOP_K):
        tw_ref[r:r + 1, :] = ws[r] / tot * ROUTED_SCALE
    for r in range(TOP_K, SUBLANES):
        ti_ref[r:r + 1, :] = jnp.zeros((1, t), jnp.int32)
        tw_ref[r:r + 1, :] = jnp.zeros((1, t), F32)


def _route(logits_t, router_bias):
    n = logits_t.shape[1]
    ng = N_EXPERT_GROUPS
    gsz = N_EXPERTS // ng
    tn = 1024
    return pl.pallas_call(
        _route_kernel,
        grid=(n // tn,),
        in_specs=[pl.BlockSpec((ng, gsz, tn), lambda i: (0, 0, i)),
                  pl.BlockSpec((ng, gsz, 1), lambda i: (0, 0, 0))],
        out_specs=[pl.BlockSpec((SUBLANES, tn), lambda i: (0, i)),
                   pl.BlockSpec((SUBLANES, tn), lambda i: (0, i))],
        out_shape=[jax.ShapeDtypeStruct((SUBLANES, n), jnp.int32),
                   jax.ShapeDtypeStruct((SUBLANES, n), F32)],
        compiler_params=_cparams(("arbitrary",)),
        name="route",
    )(logits_t.reshape(ng, gsz, n), router_bias.reshape(ng, gsz, 1))


def _moe_kernel(blk_e_ref, nused_ref, nvalid_ref, a_ref, an_ref, h_hbm, wg_ref, wu_ref, wd_ref,
                y_hbm, xbuf, ybuf, wg_s, wu_s, wd_s, gsem, wsem, *, n_tok):
    j = pl.program_id(0)
    n_used = nused_ref[0]
    slot = lax.rem(j, 2)
    nv = nvalid_ref[j]
    nv_prev = nvalid_ref[jnp.maximum(j - 1, 0)]

    def start_gather(idx_ref, s):
        def body(r, _):
            tok = lax.rem(idx_ref[0, 0, r], n_tok)
            pltpu.make_async_copy(h_hbm.at[pl.ds(tok, 1), :], xbuf.at[s, pl.ds(r, 1), :], gsem.at[s]).start()
            return 0
        lax.fori_loop(0, MOE_TB, body, 0, unroll=8)

    def wait_gather(s):
        pltpu.make_async_copy(h_hbm.at[pl.ds(0, MOE_TB), :], xbuf.at[s], gsem.at[s]).wait()

    def wait_writes(count):
        p = MOE_TB
        while p >= 1:
            @pl.when((count & p) != 0)
            def _(p=p):
                pltpu.make_async_copy(ybuf.at[pl.ds(0, p), :], y_hbm.at[pl.ds(0, p), :], wsem.at[0]).wait()
            p //= 2

    @pl.when(j < n_used)
    def _():
        @pl.when(j == 0)
        def _():
            start_gather(a_ref, 0)

        first = jnp.logical_or(j == 0, blk_e_ref[j] != blk_e_ref[jnp.maximum(j - 1, 0)])

        @pl.when(first)
        def _():
            wg_s[...] = wg_ref[0].astype(BF16)
            wu_s[...] = wu_ref[0].astype(BF16)
            wd_s[...] = wd_ref[0].astype(BF16)

        wait_gather(slot)

        @pl.when(j + 1 < n_used)
        def _():
            start_gather(an_ref, 1 - slot)

        x = xbuf[slot].astype(BF16)
        g = jnp.dot(x, wg_s[...], preferred_element_type=F32)
        u = jnp.dot(x, wu_s[...], preferred_element_type=F32)
        hm = (g * jax.nn.sigmoid(g) * u).astype(BF16)
        y = jnp.dot(hm, wd_s[...], preferred_element_type=F32)

        @pl.when(j > 0)
        def _():
            wait_writes(nv_prev)

        ybuf[...] = y

        def wbody(r, _):
            pltpu.make_async_copy(ybuf.at[pl.ds(r, 1), :], y_hbm.at[pl.ds(a_ref[0, 0, r], 1), :], wsem.at[0]).start()
            return 0
        lax.fori_loop(0, nv, wbody, 0)

        @pl.when(j == n_used - 1)
        def _():
            wait_writes(nv)


def _moe(h2, topi, w_gate, w_up, w_down):
    n, d = h2.shape
    nk = n * TOP_K
    tb = MOE_TB
    n_blocks = nk // tb + N_EXPERTS
    flat_e = topi[:TOP_K].reshape(nk)
    se, order = lax.sort_key_val(flat_e, jnp.arange(nk, dtype=jnp.int32))
    bounds = jnp.searchsorted(se, jnp.arange(N_EXPERTS + 1, dtype=jnp.int32), side='left').astype(jnp.int32)
    start = bounds[:-1]
    counts = bounds[1:] - start
    padded = (counts + tb - 1) // tb * tb
    pad_end = jnp.cumsum(padded)
    pad_start = pad_end - padded
    n_used = (pad_end[-1] // tb).astype(jnp.int32).reshape(1)
    blk_p0 = jnp.arange(n_blocks, dtype=jnp.int32) * tb
    blk_e = jnp.minimum(jnp.searchsorted(pad_end, blk_p0, side='right'), N_EXPERTS - 1).astype(jnp.int32)
    blk_off = blk_p0 - pad_start[blk_e]
    nvalid = jnp.clip(counts[blk_e] - blk_off, 0, tb).astype(jnp.int32)
    rank = (start[blk_e] + blk_off)[:, None] + jnp.arange(tb, dtype=jnp.int32)[None, :]
    a3 = order[jnp.clip(rank, 0, nk - 1)].reshape(n_blocks, 1, tb)

    smem_blk = lambda f: pl.BlockSpec((1, 1, tb), f, memory_space=pltpu.SMEM)
    grid_spec = pltpu.PrefetchScalarGridSpec(
        num_scalar_prefetch=3,
        grid=(n_blocks,),
        in_specs=[smem_blk(lambda j, be, nu, nv: (j, 0, 0)),
                  smem_blk(lambda j, be, nu, nv: (jnp.minimum(j + 1, n_blocks - 1), 0, 0)),
                  pl.BlockSpec(memory_space=pl.ANY),
                  pl.BlockSpec((1, d, D_EXPERT), lambda j, be, nu, nv: (be[j], 0, 0)),
                  pl.BlockSpec((1, d, D_EXPERT), lambda j, be, nu, nv: (be[j], 0, 0)),
                  pl.BlockSpec((1, D_EXPERT, d), lambda j, be, nu, nv: (be[j], 0, 0))],
        out_specs=pl.BlockSpec(memory_space=pl.ANY),
        scratch_shapes=[pltpu.VMEM((2, tb, d), F32),
                        pltpu.VMEM((tb, d), F32),
                        pltpu.VMEM((d, D_EXPERT), BF16),
                        pltpu.VMEM((d, D_EXPERT), BF16),
                        pltpu.VMEM((D_EXPERT, d), BF16),
                        pltpu.SemaphoreType.DMA((2,)),
                        pltpu.SemaphoreType.DMA((1,))],
    )
    return pl.pallas_call(
        functools.partial(_moe_kernel, n_tok=n),
        grid_spec=grid_spec,
        out_shape=jax.ShapeDtypeStruct((TOP_K * n, d), F32),
        compiler_params=_cparams(("arbitrary",)),
        name="moe",
    )(blk_e, n_used, nvalid, a3, a3, h2, w_gate, w_up, w_down)


def _final_kernel(h_ref, y0, y1, y2, y3, y4, y5, tw_ref, x1_ref, mod_ref, g_ref, wg_ref, wu_ref, wd_ref, o_ref):
    h = h_ref[...].astype(BF16)
    g = jnp.dot(h, wg_ref[...], preferred_element_type=F32)
    u = jnp.dot(h, wu_ref[...], preferred_element_type=F32)
    hm = (g * jax.nn.sigmoid(g) * u).astype(BF16)
    ffn = jnp.dot(hm, wd_ref[...], preferred_element_type=F32)
    tw = tw_ref[...]
    for k, y in enumerate((y0, y1, y2, y3, y4, y5)):
        ffn = ffn + y[...] * tw[:, k:k + 1]
    o_ref[0] = x1_ref[0] + mod_ref[0, 5:6, :] * _rms(ffn, g_ref[...])


def _final(h2, y_flat, topw_t, x1, mod, g_post_ffn, w_sg, w_su, w_sd):
    bn, l, d = x1.shape
    n = bn * l
    tm = 256
    nl = l // tm
    const = lambda b, i: (0, 0)
    one = pl.Buffered(1)
    y_specs = [pl.BlockSpec((tm, d), functools.partial(lambda b, i, k: (k * (n // tm) + b * nl + i, 0), k=k))
               for k in range(TOP_K)]
    return pl.pallas_call(
        _final_kernel,
        grid=(bn, nl),
        in_specs=[pl.BlockSpec((tm, d), lambda b, i: (b * nl + i, 0))] + y_specs + [
            pl.BlockSpec((tm, SUBLANES), lambda b, i: (b * nl + i, 0)),
            pl.BlockSpec((1, tm, d), lambda b, i: (b, i, 0)),
            pl.BlockSpec((1, 6, d), lambda b, i: (b, 0, 0)),
            pl.BlockSpec((1, d), const),
            pl.BlockSpec((d, D_EXPERT), const, pipeline_mode=one),
            pl.BlockSpec((d, D_EXPERT), const, pipeline_mode=one),
            pl.BlockSpec((D_EXPERT, d), const, pipeline_mode=one)],
        out_specs=pl.BlockSpec((1, tm, d), lambda b, i: (b, i, 0)),
        out_shape=jax.ShapeDtypeStruct((bn, l, d), F32),
        compiler_params=_cparams(("arbitrary", "arbitrary")),
        name="final",
    )(h2, *([y_flat] * TOP_K), topw_t, x1, mod, g_post_ffn.reshape(1, d), w_sg, w_su, w_sd)


def _layer(x, c, positions, w_ada, b_ada, g_pre_mix, g_post_mix, g_pre_ffn, g_post_ffn, w_in, g_q, g_kv,
           w_uq, w_uk, w_uv, a_re, a_im, log_dt, b_re, b_im, c_re, c_im, d_skip, w_glu, w_br_mla, w_br_s5,
           w_out, w_router, router_bias, w_exp_gate, w_exp_up, w_exp_down, w_sh_gate, w_sh_up, w_sh_down):
    bn, l, d = x.shape
    n = bn * l
    mod = _adaln(c, w_ada, b_ada).reshape(bn, 6, d)

    o_kpe = Q_LORA + KV_LORA
    o_u = o_kpe + QK_ROPE
    o_g = o_u + S5_CH
    w_lat = w_in[:, :o_kpe].astype(BF16)
    w_kpe = jnp.pad(w_in[:, o_kpe:o_u], ((0, 0), (0, LANES - QK_ROPE))).astype(BF16)
    w_u = w_in[:, o_u:o_g].astype(BF16)
    w_gates = w_in[:, o_g:].astype(BF16)
    w_q = jnp.pad(w_uq.reshape(Q_LORA, MLA_HEADS, QK_NOPE + QK_ROPE),
                  ((0, 0), (0, 0), (0, QK_PAD - QK_NOPE - QK_ROPE))).reshape(Q_LORA, MLA_HEADS * QK_PAD).astype(BF16)
    wr_t = w_router.T
    wr_hi = wr_t.astype(BF16)
    wr_lo = (wr_t - wr_hi.astype(F32)).astype(BF16)

    rope_c, rope_s1, rope_s2 = _rope_tables(positions)
    chunk_id = positions // CHUNK

    h = _prenorm(x, mod, g_pre_mix, 0, 1).reshape(n, d)
    lat = _mm(h, w_lat, F32, 1024, 512, name="mm_lat")
    kpe = _mm(h, w_kpe, F32, 1024, LANES, name="mm_kpe")
    gates = _mm(h, w_gates, BF16, 1024, 512, act="sigmoid", name="mm_gates")
    u6 = _mm_u(h, w_u, bn, l).reshape(S5_PAIRS, l * SUBLANES, LANES)

    q = _qproj(lat, g_q, w_q, rope_c, rope_s1, rope_s2)
    k, v = _kvproj(lat, g_kv, w_uk.astype(BF16), w_uv.astype(BF16), kpe, rope_c, rope_s1, rope_s2)
    att = _attention(q, k, v, chunk_id, bn, l)

    wb, ab, wc, dd = _s5_params(a_re, a_im, log_dt, b_re, b_im, c_re, c_im, d_skip)
    y6 = _s5(u6, wb, ab, wc, dd, l)
    z = _glu(y6.reshape(S5_PAIRS, l * bn, 2 * LANES), w_glu.astype(BF16))
    z_tm = z.reshape(l, bn * S5_CH)

    x1, h2, logits_t = _merge(att, z_tm, gates, x, mod, g_post_mix, g_pre_ffn, w_br_mla.astype(BF16),
                              w_br_s5.astype(BF16), w_out.astype(BF16), wr_hi, wr_lo)

    topi, topw = _route(logits_t, router_bias)
    h2f = h2.reshape(n, d)
    y_flat = _moe(h2f, topi, w_exp_gate, w_exp_up, w_exp_down)
    return _final(h2f, y_flat, topw.T, x1, mod, g_post_ffn, w_sh_gate.astype(BF16), w_sh_up.astype(BF16),
                  w_sh_down.astype(BF16))


def kernel(x, c, positions, w_ada, b_ada, g_pre_mix, g_post_mix, g_pre_ffn, g_post_ffn, w_in, g_q, g_kv, w_uq, w_uk, w_uv, a_re, a_im, log_dt, b_re, b_im, c_re, c_im, d_skip, w_glu, w_br_mla, w_br_s5, w_out, w_router, router_bias, w_exp_gate, w_exp_up, w_exp_down, w_sh_gate, w_sh_up, w_sh_down):
    depth = w_ada.shape[0]
    for li in range(depth):
        x = _layer(x, c, positions, w_ada[li], b_ada[li], g_pre_mix[li], g_post_mix[li], g_pre_ffn[li],
                   g_post_ffn[li], w_in[li], g_q[li], g_kv[li], w_uq[li], w_uk[li], w_uv[li], a_re[li], a_im[li],
                   log_dt[li], b_re[li], b_im[li], c_re[li], c_im[li], d_skip[li], w_glu[li], w_br_mla[li],
                   w_br_s5[li], w_out[li], w_router[li], router_bias[li], w_exp_gate[li], w_exp_up[li],
                   w_exp_down[li], w_sh_gate[li], w_sh_up[li], w_sh_down[li])
    return x
```

```python
import functools

import jax
import jax.numpy as jnp
from jax import lax
from jax.experimental import pallas as pl
from jax.experimental.pallas import tpu as pltpu

F32 = jnp.float32
BF16 = jnp.bfloat16

D_MODEL = 2048
CHUNK = 64
EPS = 1e-6
MLA_HEADS = 8
QK_NOPE = 128
QK_ROPE = 64
V_HEAD = 128
Q_LORA = 512
KV_LORA = 512
ROPE_THETA = 10000.0
S5_CH = 1024
S5_GROUP = 16
S5_GROUPS = S5_CH // S5_GROUP
S5_STATE = 64
N_EXPERTS = 64
TOP_K = 6
N_EXPERT_GROUPS = 8
TOPK_GROUPS = 4
D_EXPERT = 512
ROUTED_SCALE = 2.5

LANES = 128
SUBLANES = 8
QK_PAD = 2 * LANES
VMEM_LIMIT = 56 * 1024 * 1024
NEG = -1e30

S5_GB = LANES // S5_GROUP
S5_NBLK = S5_CH // LANES
S5_PAIRS = S5_NBLK // 2
S5_LANES = S5_GB * S5_STATE
S5_TC = 256
S5_RB = 512

ATT_TQ = 256
ATT_TK = 256
MOE_TB = 256


def _cparams(sem):
    return pltpu.CompilerParams(dimension_semantics=sem, vmem_limit_bytes=VMEM_LIMIT)


def _rms(x, g):
    return x * lax.rsqrt(jnp.mean(x * x, axis=-1, keepdims=True) + EPS) * g


def _adaln_kernel(c_ref, w_ref, b_ref, o_ref):
    c = c_ref[...]
    a = (c * jax.nn.sigmoid(c)).astype(BF16)
    o_ref[...] = jnp.dot(a, w_ref[...].astype(BF16), preferred_element_type=F32) + b_ref[...]


def _adaln(c, w, b):
    bn, d = c.shape
    n = w.shape[1]
    tn = 1024
    return pl.pallas_call(
        _adaln_kernel,
        grid=(n // tn,),
        in_specs=[pl.BlockSpec((bn, d), lambda j: (0, 0)),
                  pl.BlockSpec((d, tn), lambda j: (0, j)),
                  pl.BlockSpec((1, tn), lambda j: (0, j))],
        out_specs=pl.BlockSpec((bn, tn), lambda j: (0, j)),
        out_shape=jax.ShapeDtypeStruct((bn, n), F32),
        compiler_params=_cparams(("arbitrary",)),
        name="adaln",
    )(c, w, b.reshape(1, n))


def _prenorm_kernel(x_ref, mod_ref, g_ref, o_ref, *, sh_row, sc_row):
    y = _rms(x_ref[0], g_ref[...])
    o_ref[0] = (y * (1.0 + mod_ref[0, sc_row:sc_row + 1, :]) + mod_ref[0, sh_row:sh_row + 1, :]).astype(o_ref.dtype)


def _prenorm(x, mod, g, sh_row, sc_row):
    bn, l, d = x.shape
    tl = 512
    return pl.pallas_call(
        functools.partial(_prenorm_kernel, sh_row=sh_row, sc_row=sc_row),
        grid=(bn, l // tl),
        in_specs=[pl.BlockSpec((1, tl, d), lambda b, i: (b, i, 0)),
                  pl.BlockSpec((1, 6, d), lambda b, i: (b, 0, 0)),
                  pl.BlockSpec((1, d), lambda b, i: (0, 0))],
        out_specs=pl.BlockSpec((1, tl, d), lambda b, i: (b, i, 0)),
        out_shape=jax.ShapeDtypeStruct((bn, l, d), BF16),
        compiler_params=_cparams(("arbitrary", "arbitrary")),
        name="prenorm",
    )(x, mod, g.reshape(1, d))


def _mm_kernel(a_ref, w_ref, o_ref, *, act):
    acc = jnp.dot(a_ref[...], w_ref[...], preferred_element_type=F32)
    if act == "sigmoid":
        acc = jax.nn.sigmoid(acc)
    o_ref[...] = acc.astype(o_ref.dtype)


def _mm(a, w, out_dtype, tm, tn, act=None, name="mm"):
    m, k = a.shape
    n = w.shape[1]
    return pl.pallas_call(
        functools.partial(_mm_kernel, act=act),
        grid=(n // tn, m // tm),
        in_specs=[pl.BlockSpec((tm, k), lambda j, i: (i, 0)),
                  pl.BlockSpec((k, tn), lambda j, i: (0, j))],
        out_specs=pl.BlockSpec((tm, tn), lambda j, i: (i, j)),
        out_shape=jax.ShapeDtypeStruct((m, n), out_dtype),
        compiler_params=_cparams(("arbitrary", "arbitrary")),
        name=name,
    )(a, w)


def _mm_u_kernel(a_ref, w_ref, o_ref):
    o_ref[0] = jnp.dot(a_ref[...], w_ref[...], preferred_element_type=F32)


def _mm_u(h, w, bn, l):
    m, k = h.shape
    tm = 1024
    nl = l // tm
    tn = 2 * LANES
    return pl.pallas_call(
        _mm_u_kernel,
        grid=(S5_PAIRS, bn, nl),
        in_specs=[pl.BlockSpec((tm, k), lambda j, b, i: (b * nl + i, 0)),
                  pl.BlockSpec((k, tn), lambda j, b, i: (0, j))],
        out_specs=pl.BlockSpec((1, tm, tn), lambda j, b, i: (j, i, b)),
        out_shape=jax.ShapeDtypeStruct((S5_PAIRS, l, bn * tn), F32),
        compiler_params=_cparams(("arbitrary", "arbitrary", "arbitrary")),
        name="mm_u",
    )(h, w)


def _rope_tab_kernel(pos_ref, k_ref, c_ref, s1_ref, s2_ref):
    ang = pos_ref[...].astype(F32) * k_ref[0:1, :]
    s = jnp.sin(ang)
    c_ref[...] = jnp.cos(ang) * k_ref[1:2, :]
    s1_ref[...] = s * k_ref[2:3, :]
    s2_ref[...] = s * k_ref[3:4, :]


def _rope_tables(positions):
    n = positions.size
    half = QK_ROPE // 2
    inv_freq = ROPE_THETA ** (-jnp.arange(half, dtype=F32) / half)
    zh, oh = jnp.zeros((half,), F32), jnp.ones((half,), F32)
    z2 = jnp.zeros((LANES - QK_ROPE,), F32)
    rows = [jnp.concatenate([inv_freq, inv_freq, z2]), jnp.concatenate([oh, oh, z2]),
            jnp.concatenate([-oh, zh, z2]), jnp.concatenate([zh, oh, z2])]
    consts = jnp.stack(rows + [jnp.zeros((LANES,), F32)] * (SUBLANES - len(rows)))
    tm = 1024
    tab = jax.ShapeDtypeStruct((n, LANES), F32)
    return pl.pallas_call(
        _rope_tab_kernel,
        grid=(n // tm,),
        in_specs=[pl.BlockSpec((tm, 1), lambda i: (i, 0)),
                  pl.BlockSpec((SUBLANES, LANES), lambda i: (0, 0))],
        out_specs=[pl.BlockSpec((tm, LANES), lambda i: (i, 0))] * 3,
        out_shape=[tab, tab, tab],
        compiler_params=_cparams(("arbitrary",)),
        name="rope_tables",
    )(positions.reshape(n, 1), consts)


def _rope_tile(t, c_ref, s1_ref, s2_ref):
    return (t * c_ref[...] + pltpu.roll(t, LANES - QK_ROPE // 2, 1) * s1_ref[...]
            + pltpu.roll(t, QK_ROPE // 2, 1) * s2_ref[...])


def _qkvproj_kernel(lat_ref, gq_ref, gkv_ref, wq_ref, wk_ref, wv_ref, kpe_ref, c_ref, s1_ref, s2_ref,
                    q_ref, k_ref, v_ref, *, scale):
    lat = lat_ref[...]
    qn = _rms(lat[:, :Q_LORA], gq_ref[...]).astype(BF16)
    cn = _rms(lat[:, Q_LORA:], gkv_ref[...]).astype(BF16)
    q = jnp.dot(qn, wq_ref[...], preferred_element_type=F32)
    kn = jnp.dot(cn, wk_ref[...], preferred_element_type=F32)
    v_ref[...] = jnp.dot(cn, wv_ref[...], preferred_element_type=F32).astype(v_ref.dtype)
    kt = _rope_tile(kpe_ref[...], c_ref, s1_ref, s2_ref).astype(k_ref.dtype)
    for h in range(MLA_HEADS):
        o = h * QK_PAD
        q_ref[:, o:o + LANES] = (q[:, o:o + LANES] * scale).astype(q_ref.dtype)
        qt = _rope_tile(q[:, o + LANES:o + QK_PAD], c_ref, s1_ref, s2_ref)
        q_ref[:, o + LANES:o + QK_PAD] = (qt * scale).astype(q_ref.dtype)
        k_ref[:, o:o + LANES] = kn[:, h * QK_NOPE:(h + 1) * QK_NOPE].astype(k_ref.dtype)
        k_ref[:, o + LANES:o + QK_PAD] = kt


def _qkvproj(lat, g_q, g_kv, w_q, w_k, w_v, kpe, rope_c, rope_s1, rope_s2):
    n = lat.shape[0]
    tm = 512
    row = lambda i: (i, 0)
    const = lambda i: (0, 0)
    tab = pl.BlockSpec((tm, LANES), row)
    return pl.pallas_call(
        functools.partial(_qkvproj_kernel, scale=(QK_NOPE + QK_ROPE) ** -0.5),
        grid=(n // tm,),
        in_specs=[pl.BlockSpec((tm, Q_LORA + KV_LORA), row),
                  pl.BlockSpec((1, Q_LORA), const),
                  pl.BlockSpec((1, KV_LORA), const),
                  pl.BlockSpec((Q_LORA, MLA_HEADS * QK_PAD), const),
                  pl.BlockSpec((KV_LORA, MLA_HEADS * QK_NOPE), const),
                  pl.BlockSpec((KV_LORA, MLA_HEADS * V_HEAD), const),
                  tab, tab, tab, tab],
        out_specs=[pl.BlockSpec((tm, MLA_HEADS * QK_PAD), row),
                   pl.BlockSpec((tm, MLA_HEADS * QK_PAD), row),
                   pl.BlockSpec((tm, MLA_HEADS * V_HEAD), row)],
        out_shape=[jax.ShapeDtypeStruct((n, MLA_HEADS * QK_PAD), BF16),
                   jax.ShapeDtypeStruct((n, MLA_HEADS * QK_PAD), BF16),
                   jax.ShapeDtypeStruct((n, MLA_HEADS * V_HEAD), BF16)],
        compiler_params=_cparams(("arbitrary",)),
        name="qkvproj",
    )(lat, g_q.reshape(1, Q_LORA), g_kv.reshape(1, KV_LORA), w_q, w_k, w_v, kpe, rope_c, rope_s1, rope_s2)


def _attn_kernel(hi_ref, q_ref, k_ref, v_ref, qc_ref, kc_ref, o_ref, m_scr, l_scr, acc_scr, *, nq):
    b = pl.program_id(0)
    i = pl.program_id(1)
    qc = qc_ref[...]
    m_scr[...] = jnp.full(m_scr.shape, NEG, F32)
    l_scr[...] = jnp.zeros(l_scr.shape, F32)
    acc_scr[...] = jnp.zeros(acc_scr.shape, F32)

    def body(j, _):
        off = pl.multiple_of(j * ATT_TK, ATT_TK)
        mask = kc_ref[j] <= qc
        for h in range(MLA_HEADS):
            q = q_ref[:, h * QK_PAD:(h + 1) * QK_PAD]
            k = k_ref[pl.ds(off, ATT_TK), h * QK_PAD:(h + 1) * QK_PAD]
            s = lax.dot_general(q, k, (((1,), (1,)), ((), ())), preferred_element_type=F32)
            s = jnp.where(mask, s, NEG)
            m_old = m_scr[h]
            m_new = jnp.maximum(m_old, jnp.max(s, axis=-1, keepdims=True))
            p = jnp.exp(s - jnp.concatenate([m_new] * (ATT_TK // LANES), axis=1))
            alpha = jnp.exp(m_old - m_new)
            l_scr[h] = alpha * l_scr[h] + jnp.sum(p, axis=-1, keepdims=True)
            v = v_ref[pl.ds(off, ATT_TK), h * V_HEAD:(h + 1) * V_HEAD]
            acc_scr[h] = alpha * acc_scr[h] + jnp.dot(p.astype(BF16), v, preferred_element_type=F32)
            m_scr[h] = m_new
        return 0

    lax.fori_loop(0, hi_ref[b * nq + i], body, 0)
    for h in range(MLA_HEADS):
        o_ref[:, h * V_HEAD:(h + 1) * V_HEAD] = (acc_scr[h] / l_scr[h]).astype(o_ref.dtype)


def _attention(q, k, v, chunk_id, bn, l):
    nq = l // ATT_TQ
    nk = l // ATT_TK
    q_max = jnp.max(chunk_id.reshape(bn, nq, ATT_TQ), axis=-1)
    k_min = jnp.min(chunk_id.reshape(bn, nk, ATT_TK), axis=-1)
    needed = k_min[:, None, :] <= q_max[:, :, None]
    hi = jnp.max(jnp.where(needed, jnp.arange(1, nk + 1, dtype=jnp.int32), 0), axis=-1).reshape(bn * nq)
    qc = chunk_id.reshape(bn * l, 1)
    kc = chunk_id.reshape(bn * nk, 1, ATT_TK)
    grid_spec = pltpu.PrefetchScalarGridSpec(
        num_scalar_prefetch=1,
        grid=(bn, nq),
        in_specs=[pl.BlockSpec((ATT_TQ, MLA_HEADS * QK_PAD), lambda b, i, hi: (b * nq + i, 0)),
                  pl.BlockSpec((l, MLA_HEADS * QK_PAD), lambda b, i, hi: (b, 0)),
                  pl.BlockSpec((l, MLA_HEADS * V_HEAD), lambda b, i, hi: (b, 0)),
                  pl.BlockSpec((ATT_TQ, 1), lambda b, i, hi: (b * nq + i, 0)),
                  pl.BlockSpec((nk, 1, ATT_TK), lambda b, i, hi: (b, 0, 0))],
        out_specs=pl.BlockSpec((ATT_TQ, MLA_HEADS * V_HEAD), lambda b, i, hi: (b * nq + i, 0)),
        scratch_shapes=[pltpu.VMEM((MLA_HEADS, ATT_TQ, LANES), F32),
                        pltpu.VMEM((MLA_HEADS, ATT_TQ, LANES), F32),
                        pltpu.VMEM((MLA_HEADS, ATT_TQ, V_HEAD), F32)],
    )
    return pl.pallas_call(
        functools.partial(_attn_kernel, nq=nq),
        grid_spec=grid_spec,
        out_shape=jax.ShapeDtypeStruct((bn * l, MLA_HEADS * V_HEAD), BF16),
        compiler_params=_cparams(("arbitrary", "arbitrary")),
        name="attention",
    )(hi, q, k, v, qc, kc)


def _s5_kernel(u_ref, wb_ref, a_ref, wc_ref, d_ref, o_ref, x_scr, st_scr):
    rows = S5_TC * SUBLANES
    nsub = rows // S5_RB

    @pl.when(pl.program_id(1) == 0)
    def _():
        st_scr[...] = jnp.zeros_like(st_scr)

    even = (lax.broadcasted_iota(jnp.int32, (S5_RB, 1), 0) & 1) == 0

    def mm_in(r, _):
        off = pl.multiple_of(r * S5_RB, S5_RB)
        out = jnp.dot(u_ref[0, pl.ds(off, S5_RB), :].astype(BF16), wb_ref[0], preferred_element_type=F32)
        x_scr[pl.ds(off, S5_RB), :] = jnp.where(even, out[:, :2 * S5_LANES], out[:, 2 * S5_LANES:])
        return 0

    lax.fori_loop(0, nsub, mm_in, 0)

    ar = a_ref[0, :, :S5_LANES]
    ai = a_ref[0, :, S5_LANES:]

    def step(t, carry):
        xr, xi = carry
        off = pl.multiple_of(t * SUBLANES, SUBLANES)
        nr = ar * xr - ai * xi + x_scr[pl.ds(off, SUBLANES), :S5_LANES]
        ni = ar * xi + ai * xr + x_scr[pl.ds(off, SUBLANES), S5_LANES:]
        x_scr[pl.ds(off, SUBLANES), :S5_LANES] = nr
        x_scr[pl.ds(off, SUBLANES), S5_LANES:] = ni
        return nr, ni

    xr, xi = lax.fori_loop(0, S5_TC, step, (st_scr[:, :S5_LANES], st_scr[:, S5_LANES:]), unroll=8)
    st_scr[:, :S5_LANES] = xr
    st_scr[:, S5_LANES:] = xi

    d = jnp.concatenate([d_ref[0]] * (S5_RB // SUBLANES), axis=0)

    def mm_out(r, _):
        off = pl.multiple_of(r * S5_RB, S5_RB)
        out = jnp.dot(x_scr[pl.ds(off, S5_RB), :].astype(BF16), wc_ref[0], preferred_element_type=F32)
        y = jnp.where(even, out[:, :LANES], out[:, LANES:]) + d * u_ref[0, pl.ds(off, S5_RB), :]
        o_ref[0, pl.ds(off, S5_RB), :] = jax.nn.gelu(y).astype(o_ref.dtype)
        return 0

    lax.fori_loop(0, nsub, mm_out, 0)


def _s5_params(a_re, a_im, log_dt, b_re, b_im, c_re, c_im, d_skip):
    step = jnp.exp(log_dt)[:, None]
    mag = jnp.exp(a_re * step)
    abar_re, abar_im = mag * jnp.cos(a_im * step), mag * jnp.sin(a_im * step)
    den = a_re * a_re + a_im * a_im
    nr, ni = abar_re - 1.0, abar_im
    f_re, f_im = (nr * a_re + ni * a_im) / den, (ni * a_re - nr * a_im) / den
    bbar_re = f_re[..., None] * b_re - f_im[..., None] * b_im
    bbar_im = f_re[..., None] * b_im + f_im[..., None] * b_re
    eye = jnp.eye(S5_GB, dtype=F32)
    bb = jnp.stack([bbar_re, bbar_im]).reshape(2, S5_NBLK, S5_GB, S5_STATE, S5_GROUP)
    wb = jnp.einsum('ab,rjapc->jacrbp', eye, bb).reshape(S5_NBLK, LANES, 2 * S5_LANES)
    wb = wb.reshape(S5_PAIRS, 2, LANES, 2 * S5_LANES).transpose(0, 2, 1, 3).reshape(S5_PAIRS, LANES, 4 * S5_LANES)
    cc = jnp.stack([c_re, -c_im]).reshape(2, S5_NBLK, S5_GB, S5_GROUP, S5_STATE)
    wc = jnp.einsum('ab,rjacp->jrapbc', eye, cc).reshape(S5_NBLK, 2 * S5_LANES, LANES)
    wc = wc.reshape(S5_PAIRS, 2, 2 * S5_LANES, LANES).transpose(0, 2, 1, 3).reshape(S5_PAIRS, 2 * S5_LANES, 2 * LANES)
    ab = jnp.concatenate([abar_re.reshape(S5_NBLK, S5_LANES), abar_im.reshape(S5_NBLK, S5_LANES)], axis=1)
    ab = jnp.tile(ab.reshape(S5_PAIRS, 2, 2 * S5_LANES), (1, SUBLANES // 2, 1))
    dd = jnp.tile(d_skip.reshape(S5_PAIRS, 2, LANES), (1, SUBLANES // 2, 1))
    return wb.astype(BF16), ab, wc.astype(BF16), dd


def _s5(u6, wb, ab, wc, dd, l):
    rows = S5_TC * SUBLANES
    return pl.pallas_call(
        _s5_kernel,
        grid=(S5_PAIRS, l // S5_TC),
        in_specs=[pl.BlockSpec((1, rows, LANES), lambda k, c: (k, c, 0)),
                  pl.BlockSpec((1, LANES, 4 * S5_LANES), lambda k, c: (k, 0, 0)),
                  pl.BlockSpec((1, SUBLANES, 2 * S5_LANES), lambda k, c: (k, 0, 0)),
                  pl.BlockSpec((1, 2 * S5_LANES, 2 * LANES), lambda k, c: (k, 0, 0)),
                  pl.BlockSpec((1, SUBLANES, LANES), lambda k, c: (k, 0, 0))],
        out_specs=pl.BlockSpec((1, rows, LANES), lambda k, c: (k, c, 0)),
        out_shape=jax.ShapeDtypeStruct(u6.shape, BF16),
        scratch_shapes=[pltpu.VMEM((rows, 2 * S5_LANES), F32),
                        pltpu.VMEM((SUBLANES, 2 * S5_LANES), F32)],
        compiler_params=_cparams(("arbitrary", "arbitrary")),
        name="s5",
    )(u6, wb, ab, wc, dd)


def _glu_kernel(y_ref, w_ref, o_ref):
    y = jnp.concatenate([y_ref[k] for k in range(S5_PAIRS)], axis=1)
    g = jnp.dot(y, w_ref[...], preferred_element_type=F32)
    o_ref[...] = (y.astype(F32) * jax.nn.sigmoid(g)).astype(o_ref.dtype)


def _glu(y4, w_glu):
    m = y4.shape[1]
    tm = 1024
    return pl.pallas_call(
        _glu_kernel,
        grid=(m // tm,),
        in_specs=[pl.BlockSpec((S5_PAIRS, tm, 2 * LANES), lambda i: (0, i, 0)),
                  pl.BlockSpec((S5_CH, S5_CH), lambda i: (0, 0))],
        out_specs=pl.BlockSpec((tm, S5_CH), lambda i: (i, 0)),
        out_shape=jax.ShapeDtypeStruct((m, S5_CH), BF16),
        compiler_params=_cparams(("arbitrary",)),
        name="glu",
    )(y4, w_glu)


def _merge_kernel(att_ref, z_ref, gm_ref, gs_ref, x_ref, mod_ref, gpost_ref, gpre_ref,
                  wbm_ref, wbs_ref, wo_ref, wrh_ref, wrl_ref, x1_ref, h2_ref, lg_ref):
    ym = jnp.dot(att_ref[...], wbm_ref[...], preferred_element_type=F32)
    ys = jnp.dot(z_ref[...], wbs_ref[...], preferred_element_type=F32)
    mixed_in = (gm_ref[...].astype(F32) * ym + gs_ref[...].astype(F32) * ys).astype(BF16)
    mixed = jnp.dot(mixed_in, wo_ref[...], preferred_element_type=F32)
    x1 = x_ref[0] + mod_ref[0, 2:3, :] * _rms(mixed, gpost_ref[...])
    x1_ref[0] = x1
    h2 = _rms(x1, gpre_ref[...]) * (1.0 + mod_ref[0, 4:5, :]) + mod_ref[0, 3:4, :]
    h2_ref[0] = h2
    h2_hi = h2.astype(BF16)
    h2_lo = (h2 - h2_hi.astype(F32)).astype(BF16)
    nt = (((1,), (1,)), ((), ()))
    lg_ref[...] = (lax.dot_general(wrh_ref[...], h2_hi, nt, preferred_element_type=F32)
                   + lax.dot_general(wrl_ref[...], h2_hi, nt, preferred_element_type=F32)
                   + lax.dot_general(wrh_ref[...], h2_lo, nt, preferred_element_type=F32))


def _merge(att, z_tm, gates, x, mod, g_post, g_pre_ffn, w_br_mla, w_br_s5, w_out, wr_hi, wr_lo):
    bn, l, d = x.shape
    tm = 256
    nl = l // tm
    row = lambda b, i: (b * nl + i, 0)
    const = lambda b, i: (0, 0)
    one = pl.Buffered(1)
    return pl.pallas_call(
        _merge_kernel,
        grid=(bn, nl),
        in_specs=[pl.BlockSpec((tm, MLA_HEADS * V_HEAD), row),
                  pl.BlockSpec((tm, S5_CH), lambda b, i: (i, b)),
                  pl.BlockSpec((tm, d), lambda b, i: (b * nl + i, 0)),
                  pl.BlockSpec((tm, d), lambda b, i: (b * nl + i, 1)),
                  pl.BlockSpec((1, tm, d), lambda b, i: (b, i, 0)),
                  pl.BlockSpec((1, 6, d), lambda b, i: (b, 0, 0)),
                  pl.BlockSpec((1, d), const),
                  pl.BlockSpec((1, d), const),
                  pl.BlockSpec((MLA_HEADS * V_HEAD, d), const, pipeline_mode=one),
                  pl.BlockSpec((S5_CH, d), const, pipeline_mode=one),
                  pl.BlockSpec((d, d), const, pipeline_mode=one),
                  pl.BlockSpec((N_EXPERTS, d), const, pipeline_mode=one),
                  pl.BlockSpec((N_EXPERTS, d), const, pipeline_mode=one)],
        out_specs=[pl.BlockSpec((1, tm, d), lambda b, i: (b, i, 0)),
                   pl.BlockSpec((1, tm, d), lambda b, i: (b, i, 0)),
                   pl.BlockSpec((N_EXPERTS, tm), lambda b, i: (0, b * nl + i))],
        out_shape=[jax.ShapeDtypeStruct((bn, l, d), F32),
                   jax.ShapeDtypeStruct((bn, l, d), F32),
                   jax.ShapeDtypeStruct((N_EXPERTS, bn * l), F32)],
        compiler_params=_cparams(("arbitrary", "arbitrary")),
        name="merge",
    )(att, z_tm, gates, gates, x, mod, g_post.reshape(1, d), g_pre_ffn.reshape(1, d),
      w_br_mla, w_br_s5, w_out, wr_hi, wr_lo)


def _route_kernel(lg_ref, b_ref, ti_ref, tw_ref):
    ng = N_EXPERT_GROUPS
    gsz = N_EXPERTS // ng
    lg = lg_ref[...]
    t = lg.shape[-1]
    sc = jax.nn.sigmoid(lg)
    sel = sc + b_ref[...]
    ninf = -jnp.inf
    i_in = lax.broadcasted_iota(jnp.int32, (ng, gsz, t), 1).astype(F32)
    m1 = jnp.max(sel, axis=1, keepdims=True)
    idx1 = jnp.min(jnp.where(sel == m1, i_in, float(gsz)), axis=1, keepdims=True)
    m2 = jnp.max(jnp.where(i_in == idx1, ninf, sel), axis=1, keepdims=True)
    gs = m1 + m2
    g_i = lax.broadcasted_iota(jnp.int32, (ng, 1, t), 0).astype(F32)
    picked = jnp.zeros((ng, 1, t), F32)
    cur = gs
    for _ in range(TOPK_GROUPS):
        m = jnp.max(cur, axis=0, keepdims=True)
        idx = jnp.min(jnp.where(cur == m, g_i, float(ng)), axis=0, keepdims=True)
        hit = g_i == idx
        picked = jnp.where(hit, 1.0, picked)
        cur = jnp.where(hit, ninf, cur)
    cand = jnp.where(picked > 0.5, sel, ninf)
    e_i = lax.broadcasted_iota(jnp.int32, (ng, gsz, t), 0).astype(F32) * float(gsz) + i_in
    ws = []
    for r in range(TOP_K):
        m = jnp.max(jnp.max(cand, axis=1, keepdims=True), axis=0, keepdims=True)
        idx = jnp.min(jnp.min(jnp.where(cand == m, e_i, float(N_EXPERTS)), axis=1, keepdims=True),
                      axis=0, keepdims=True)
        hit = e_i == idx
        w = jnp.sum(jnp.sum(jnp.where(hit, sc, 0.0), axis=1, keepdims=True), axis=0, keepdims=True)
        ti_ref[r:r + 1, :] = idx[0].astype(jnp.int32)
        ws.append(w[0])
        cand = jnp.where(hit, ninf, cand)
    tot = ws[0]
    for w in ws[1:]:
        tot = tot + w
    for r in range(TOP_K):
        tw_ref[r:r + 1, :] = ws[r] / tot * ROUTED_SCALE
    for r in range(TOP_K, SUBLANES):
        ti_ref[r:r + 1, :] = jnp.zeros((1, t), jnp.int32)
        tw_ref[r:r + 1, :] = jnp.zeros((1, t), F32)


def _route(logits_t, router_bias):
    n = logits_t.shape[1]
    ng = N_EXPERT_GROUPS
    gsz = N_EXPERTS // ng
    tn = 1024
    return pl.pallas_call(
        _route_kernel,
        grid=(n // tn,),
        in_specs=[pl.BlockSpec((ng, gsz, tn), lambda i: (0, 0, i)),
                  pl.BlockSpec((ng, gsz, 1), lambda i: (0, 0, 0))],
        out_specs=[pl.BlockSpec((SUBLANES, tn), lambda i: (0, i)),
                   pl.BlockSpec((SUBLANES, tn), lambda i: (0, i))],
        out_shape=[jax.ShapeDtypeStruct((SUBLANES, n), jnp.int32),
                   jax.ShapeDtypeStruct((SUBLANES, n), F32)],
        compiler_params=_cparams(("arbitrary",)),
        name="route",
    )(logits_t.reshape(ng, gsz, n), router_bias.reshape(ng, gsz, 1))


def _moe_kernel(blk_e_ref, nused_ref, nvalid_ref, a_ref, an_ref, h_hbm, wg_ref, wu_ref, wd_ref,
                y_hbm, xbuf, ybuf, wg_s, wu_s, wd_s, gsem, wsem, *, n_tok):
    j = pl.program_id(0)
    n_used = nused_ref[0]
    slot = lax.rem(j, 2)
    nv = nvalid_ref[j]
    nv_prev = nvalid_ref[jnp.maximum(j - 1, 0)]

    def start_gather(idx_ref, s):
        def body(r, _):
            tok = lax.rem(idx_ref[0, 0, r], n_tok)
            pltpu.make_async_copy(h_hbm.at[pl.ds(tok, 1), :], xbuf.at[s, pl.ds(r, 1), :], gsem.at[s]).start()
            return 0
        lax.fori_loop(0, MOE_TB, body, 0, unroll=8)

    def wait_gather(s):
        pltpu.make_async_copy(h_hbm.at[pl.ds(0, MOE_TB), :], xbuf.at[s], gsem.at[s]).wait()

    def wait_writes(count):
        p = MOE_TB
        while p >= 1:
            @pl.when((count & p) != 0)
            def _(p=p):
                pltpu.make_async_copy(ybuf.at[pl.ds(0, p), :], y_hbm.at[pl.ds(0, p), :], wsem.at[0]).wait()
            p //= 2

    @pl.when(j < n_used)
    def _():
        @pl.when(j == 0)
        def _():
            start_gather(a_ref, 0)

        first = jnp.logical_or(j == 0, blk_e_ref[j] != blk_e_ref[jnp.maximum(j - 1, 0)])

        @pl.when(first)
        def _():
            wg_s[...] = wg_ref[0].astype(BF16)
            wu_s[...] = wu_ref[0].astype(BF16)
            wd_s[...] = wd_ref[0].astype(BF16)

        wait_gather(slot)

        @pl.when(j + 1 < n_used)
        def _():
            start_gather(an_ref, 1 - slot)

        x = xbuf[slot].astype(BF16)
        g = jnp.dot(x, wg_s[...], preferred_element_type=F32)
        u = jnp.dot(x, wu_s[...], preferred_element_type=F32)
        hm = (g * jax.nn.sigmoid(g) * u).astype(BF16)
        y = jnp.dot(hm, wd_s[...], preferred_element_type=F32)

        @pl.when(j > 0)
        def _():
            wait_writes(nv_prev)

        ybuf[...] = y

        def wbody(r, _):
            pltpu.make_async_copy(ybuf.at[pl.ds(r, 1), :], y_hbm.at[pl.ds(a_ref[0, 0, r], 1), :], wsem.at[0]).start()
            return 0
        lax.fori_loop(0, nv, wbody, 0)

        @pl.when(j == n_used - 1)
        def _():
            wait_writes(nv)


def _moe(h2, topi, w_gate, w_up, w_down):
    n, d = h2.shape
    nk = n * TOP_K
    tb = MOE_TB
    n_blocks = nk // tb + N_EXPERTS
    flat_e = topi[:TOP_K].reshape(nk)
    se, order = lax.sort_key_val(flat_e, jnp.arange(nk, dtype=jnp.int32))
    bounds = jnp.searchsorted(se, jnp.arange(N_EXPERTS + 1, dtype=jnp.int32), side='left').astype(jnp.int32)
    start = bounds[:-1]
    counts = bounds[1:] - start
    padded = (counts + tb - 1) // tb * tb
    pad_end = jnp.cumsum(padded)
    pad_start = pad_end - padded
    n_used = (pad_end[-1] // tb).astype(jnp.int32).reshape(1)
    blk_p0 = jnp.arange(n_blocks, dtype=jnp.int32) * tb
    blk_e = jnp.minimum(jnp.searchsorted(pad_end, blk_p0, side='right'), N_EXPERTS - 1).astype(jnp.int32)
    blk_off = blk_p0 - pad_start[blk_e]
    nvalid = jnp.clip(counts[blk_e] - blk_off, 0, tb).astype(jnp.int32)
    rank = (start[blk_e] + blk_off)[:, None] + jnp.arange(tb, dtype=jnp.int32)[None, :]
    a3 = order[jnp.clip(rank, 0, nk - 1)].reshape(n_blocks, 1, tb)

    smem_blk = lambda f: pl.BlockSpec((1, 1, tb), f, memory_space=pltpu.SMEM)
    grid_spec = pltpu.PrefetchScalarGridSpec(
        num_scalar_prefetch=3,
        grid=(n_blocks,),
        in_specs=[smem_blk(lambda j, be, nu, nv: (j, 0, 0)),
                  smem_blk(lambda j, be, nu, nv: (jnp.minimum(j + 1, n_blocks - 1), 0, 0)),
                  pl.BlockSpec(memory_space=pl.ANY),
                  pl.BlockSpec((1, d, D_EXPERT), lambda j, be, nu, nv: (be[j], 0, 0)),
                  pl.BlockSpec((1, d, D_EXPERT), lambda j, be, nu, nv: (be[j], 0, 0)),
                  pl.BlockSpec((1, D_EXPERT, d), lambda j, be, nu, nv: (be[j], 0, 0))],
        out_specs=pl.BlockSpec(memory_space=pl.ANY),
        scratch_shapes=[pltpu.VMEM((2, tb, d), F32),
                        pltpu.VMEM((tb, d), F32),
                        pltpu.VMEM((d, D_EXPERT), BF16),
                        pltpu.VMEM((d, D_EXPERT), BF16),
                        pltpu.VMEM((D_EXPERT, d), BF16),
                        pltpu.SemaphoreType.DMA((2,)),
                        pltpu.SemaphoreType.DMA((1,))],
    )
    return pl.pallas_call(
        functools.partial(_moe_kernel, n_tok=n),
        grid_spec=grid_spec,
        out_shape=jax.ShapeDtypeStruct((TOP_K * n, d), F32),
        compiler_params=_cparams(("arbitrary",)),
        name="moe",
    )(blk_e, n_used, nvalid, a3, a3, h2, w_gate, w_up, w_down)


def _final_kernel(h_ref, y0, y1, y2, y3, y4, y5, tw_ref, x1_ref, mod_ref, g_ref, wg_ref, wu_ref, wd_ref, o_ref):
    h = h_ref[...].astype(BF16)
    g = jnp.dot(h, wg_ref[...], preferred_element_type=F32)
    u = jnp.dot(h, wu_ref[...], preferred_element_type=F32)
    hm = (g * jax.nn.sigmoid(g) * u).astype(BF16)
    ffn = jnp.dot(hm, wd_ref[...], preferred_element_type=F32)
    tw = tw_ref[...]
    for k, y in enumerate((y0, y1, y2, y3, y4, y5)):
        ffn = ffn + y[...] * tw[:, k:k + 1]
    o_ref[0] = x1_ref[0] + mod_ref[0, 5:6, :] * _rms(ffn, g_ref[...])


def _final(h2, y_flat, topw_t, x1, mod, g_post_ffn, w_sg, w_su, w_sd):
    bn, l, d = x1.shape
    n = bn * l
    tm = 256
    nl = l // tm
    const = lambda b, i: (0, 0)
    one = pl.Buffered(1)
    y_specs = [pl.BlockSpec((tm, d), functools.partial(lambda b, i, k: (k * (n // tm) + b * nl + i, 0), k=k))
               for k in range(TOP_K)]
    return pl.pallas_call(
        _final_kernel,
        grid=(bn, nl),
        in_specs=[pl.BlockSpec((tm, d), lambda b, i: (b * nl + i, 0))] + y_specs + [
            pl.BlockSpec((tm, SUBLANES), lambda b, i: (b * nl + i, 0)),
            pl.BlockSpec((1, tm, d), lambda b, i: (b, i, 0)),
            pl.BlockSpec((1, 6, d), lambda b, i: (b, 0, 0)),
            pl.BlockSpec((1, d), const),
            pl.BlockSpec((d, D_EXPERT), const, pipeline_mode=one),
            pl.BlockSpec((d, D_EXPERT), const, pipeline_mode=one),
            pl.BlockSpec((D_EXPERT, d), const, pipeline_mode=one)],
        out_specs=pl.BlockSpec((1, tm, d), lambda b, i: (b, i, 0)),
        out_shape=jax.ShapeDtypeStruct((bn, l, d), F32),
        compiler_params=_cparams(("arbitrary", "arbitrary")),
        name="final",
    )(h2, *([y_flat] * TOP_K), topw_t, x1, mod, g_post_ffn.reshape(1, d), w_sg, w_su, w_sd)


def _layer(x, c, positions, w_ada, b_ada, g_pre_mix, g_post_mix, g_pre_ffn, g_post_ffn, w_in, g_q, g_kv,
           w_uq, w_uk, w_uv, a_re, a_im, log_dt, b_re, b_im, c_re, c_im, d_skip, w_glu, w_br_mla, w_br_s5,
           w_out, w_router, router_bias, w_exp_gate, w_exp_up, w_exp_down, w_sh_gate, w_sh_up, w_sh_down):
    bn, l, d = x.shape
    n = bn * l
    mod = _adaln(c, w_ada, b_ada).reshape(bn, 6, d)

    o_kpe = Q_LORA + KV_LORA
    o_u = o_kpe + QK_ROPE
    o_g = o_u + S5_CH
    w_lat = w_in[:, :o_kpe].astype(BF16)
    w_kpe = jnp.pad(w_in[:, o_kpe:o_u], ((0, 0), (0, LANES - QK_ROPE))).astype(BF16)
    w_u = w_in[:, o_u:o_g].astype(BF16)
    w_gates = w_in[:, o_g:].astype(BF16)
    w_q = jnp.pad(w_uq.reshape(Q_LORA, MLA_HEADS, QK_NOPE + QK_ROPE),
                  ((0, 0), (0, 0), (0, QK_PAD - QK_NOPE - QK_ROPE))).reshape(Q_LORA, MLA_HEADS * QK_PAD).astype(BF16)
    wr_t = w_router.T
    wr_hi = wr_t.astype(BF16)
    wr_lo = (wr_t - wr_hi.astype(F32)).astype(BF16)

    rope_c, rope_s1, rope_s2 = _rope_tables(positions)
    chunk_id = positions // CHUNK

    h = _prenorm(x, mod, g_pre_mix, 0, 1).reshape(n, d)
    lat = _mm(h, w_lat, F32, 1024, 512, name="mm_lat")
    kpe = _mm(h, w_kpe, F32, 1024, LANES, name="mm_kpe")
    gates = _mm(h, w_gates, BF16, 1024, 512, act="sigmoid", name="mm_gates")
    u6 = _mm_u(h, w_u, bn, l).reshape(S5_PAIRS, l * SUBLANES, LANES)

    q, k, v = _qkvproj(lat, g_q, g_kv, w_q, w_uk.astype(BF16), w_uv.astype(BF16), kpe, rope_c, rope_s1, rope_s2)
    att = _attention(q, k, v, chunk_id, bn, l)

    wb, ab, wc, dd = _s5_params(a_re, a_im, log_dt, b_re, b_im, c_re, c_im, d_skip)
    y6 = _s5(u6, wb, ab, wc, dd, l)
    z = _glu(y6.reshape(S5_PAIRS, l * bn, 2 * LANES), w_glu.astype(BF16))
    z_tm = z.reshape(l, bn * S5_CH)

    x1, h2, logits_t = _merge(att, z_tm, gates, x, mod, g_post_mix, g_pre_ffn, w_br_mla.astype(BF16),
                              w_br_s5.astype(BF16), w_out.astype(BF16), wr_hi, wr_lo)

    topi, topw = _route(logits_t, router_bias)
    h2f = h2.reshape(n, d)
    y_flat = _moe(h2f, topi, w_exp_gate, w_exp_up, w_exp_down)
    return _final(h2f, y_flat, topw.T, x1, mod, g_post_ffn, w_sh_gate.astype(BF16), w_sh_up.astype(BF16),
                  w_sh_down.astype(BF16))


def kernel(x, c, positions, w_ada, b_ada, g_pre_mix, g_post_mix, g_pre_ffn, g_post_ffn, w_in, g_q, g_kv, w_uq, w_uk, w_uv, a_re, a_im, log_dt, b_re, b_im, c_re, c_im, d_skip, w_glu, w_br_mla, w_br_s5, w_out, w_router, router_bias, w_exp_gate, w_exp_up, w_exp_down, w_sh_gate, w_sh_up, w_sh_down):
    depth = w_ada.shape[0]
    for li in range(depth):
        x = _layer(x, c, positions, w_ada[li], b_ada[li], g_pre_mix[li], g_post_mix[li], g_pre_ffn[li],
                   g_post_ffn[li], w_in[li], g_q[li], g_kv[li], w_uq[li], w_uk[li], w_uv[li], a_re[li], a_im[li],
                   log_dt[li], b_re[li], b_im[li], c_re[li], c_im[li], d_skip[li], w_glu[li], w_br_mla[li],
                   w_br_s5[li], w_out[li], w_router[li], router_bias[li], w_exp_gate[li], w_exp_up[li],
                   w_exp_down[li], w_sh_gate[li], w_sh_up[li], w_sh_down[li])
    return x
```

```python
import functools

import jax
import jax.numpy as jnp
from jax import lax
from jax.experimental import pallas as pl
from jax.experimental.pallas import tpu as pltpu

F32 = jnp.float32
BF16 = jnp.bfloat16

D_MODEL = 2048
CHUNK = 64
EPS = 1e-6
MLA_HEADS = 8
QK_NOPE = 128
QK_ROPE = 64
V_HEAD = 128
Q_LORA = 512
KV_LORA = 512
ROPE_THETA = 10000.0
S5_CH = 1024
S5_GROUP = 16
S5_GROUPS = S5_CH // S5_GROUP
S5_STATE = 64
N_EXPERTS = 64
TOP_K = 6
N_EXPERT_GROUPS = 8
TOPK_GROUPS = 4
D_EXPERT = 512
ROUTED_SCALE = 2.5

LANES = 128
SUBLANES = 8
QK_PAD = 2 * LANES
VMEM_LIMIT = 56 * 1024 * 1024
NEG = -1e30

S5_GB = LANES // S5_GROUP
S5_NBLK = S5_CH // LANES
S5_PAIRS = S5_NBLK // 2
S5_LANES = S5_GB * S5_STATE
S5_TC = 256
S5_RB = 512

ATT_TQ = 256
ATT_TK = 256
MOE_TB = 256
SLAB = D_MODEL // LANES
MOE_PITCH = 24


def _cparams(sem):
    return pltpu.CompilerParams(dimension_semantics=sem, vmem_limit_bytes=VMEM_LIMIT)


def _rms(x, g):
    return x * lax.rsqrt(jnp.mean(x * x, axis=-1, keepdims=True) + EPS) * g


def _adaln_kernel(c_ref, w_ref, b_ref, o_ref):
    c = c_ref[...]
    a = (c * jax.nn.sigmoid(c)).astype(BF16)
    o_ref[...] = jnp.dot(a, w_ref[...].astype(BF16), preferred_element_type=F32) + b_ref[...]


def _adaln(c, w, b):
    bn, d = c.shape
    n = w.shape[1]
    tn = 1024
    return pl.pallas_call(
        _adaln_kernel,
        grid=(n // tn,),
        in_specs=[pl.BlockSpec((bn, d), lambda j: (0, 0)),
                  pl.BlockSpec((d, tn), lambda j: (0, j)),
                  pl.BlockSpec((1, tn), lambda j: (0, j))],
        out_specs=pl.BlockSpec((bn, tn), lambda j: (0, j)),
        out_shape=jax.ShapeDtypeStruct((bn, n), F32),
        compiler_params=_cparams(("arbitrary",)),
        name="adaln",
    )(c, w, b.reshape(1, n))


def _prenorm_kernel(x_ref, mod_ref, g_ref, o_ref, *, sh_row, sc_row):
    y = _rms(x_ref[0], g_ref[...])
    o_ref[0] = (y * (1.0 + mod_ref[0, sc_row:sc_row + 1, :]) + mod_ref[0, sh_row:sh_row + 1, :]).astype(o_ref.dtype)


def _prenorm(x, mod, g, sh_row, sc_row):
    bn, l, d = x.shape
    tl = 512
    return pl.pallas_call(
        functools.partial(_prenorm_kernel, sh_row=sh_row, sc_row=sc_row),
        grid=(bn, l // tl),
        in_specs=[pl.BlockSpec((1, tl, d), lambda b, i: (b, i, 0)),
                  pl.BlockSpec((1, 6, d), lambda b, i: (b, 0, 0)),
                  pl.BlockSpec((1, d), lambda b, i: (0, 0))],
        out_specs=pl.BlockSpec((1, tl, d), lambda b, i: (b, i, 0)),
        out_shape=jax.ShapeDtypeStruct((bn, l, d), BF16),
        compiler_params=_cparams(("arbitrary", "arbitrary")),
        name="prenorm",
    )(x, mod, g.reshape(1, d))


def _mm_kernel(a_ref, w_ref, o_ref, *, act):
    acc = jnp.dot(a_ref[...], w_ref[...], preferred_element_type=F32)
    if act == "sigmoid":
        acc = jax.nn.sigmoid(acc)
    o_ref[...] = acc.astype(o_ref.dtype)


def _mm(a, w, out_dtype, tm, tn, act=None, name="mm"):
    m, k = a.shape
    n = w.shape[1]
    return pl.pallas_call(
        functools.partial(_mm_kernel, act=act),
        grid=(n // tn, m // tm),
        in_specs=[pl.BlockSpec((tm, k), lambda j, i: (i, 0)),
                  pl.BlockSpec((k, tn), lambda j, i: (0, j))],
        out_specs=pl.BlockSpec((tm, tn), lambda j, i: (i, j)),
        out_shape=jax.ShapeDtypeStruct((m, n), out_dtype),
        compiler_params=_cparams(("arbitrary", "arbitrary")),
        name=name,
    )(a, w)


def _mm_u_kernel(a_ref, w_ref, o_ref):
    o_ref[0] = jnp.dot(a_ref[...], w_ref[...], preferred_element_type=F32)


def _mm_u(h, w, bn, l):
    m, k = h.shape
    tm = 1024
    nl = l // tm
    tn = 2 * LANES
    return pl.pallas_call(
        _mm_u_kernel,
        grid=(S5_PAIRS, bn, nl),
        in_specs=[pl.BlockSpec((tm, k), lambda j, b, i: (b * nl + i, 0)),
                  pl.BlockSpec((k, tn), lambda j, b, i: (0, j))],
        out_specs=pl.BlockSpec((1, tm, tn), lambda j, b, i: (j, i, b)),
        out_shape=jax.ShapeDtypeStruct((S5_PAIRS, l, bn * tn), F32),
        compiler_params=_cparams(("arbitrary", "arbitrary", "arbitrary")),
        name="mm_u",
    )(h, w)


def _rope_tab_kernel(pos_ref, k_ref, c_ref, s1_ref, s2_ref):
    ang = pos_ref[...].astype(F32) * k_ref[0:1, :]
    s = jnp.sin(ang)
    c_ref[...] = jnp.cos(ang) * k_ref[1:2, :]
    s1_ref[...] = s * k_ref[2:3, :]
    s2_ref[...] = s * k_ref[3:4, :]


def _rope_tables(positions):
    n = positions.size
    half = QK_ROPE // 2
    inv_freq = ROPE_THETA ** (-jnp.arange(half, dtype=F32) / half)
    zh, oh = jnp.zeros((half,), F32), jnp.ones((half,), F32)
    z2 = jnp.zeros((LANES - QK_ROPE,), F32)
    rows = [jnp.concatenate([inv_freq, inv_freq, z2]), jnp.concatenate([oh, oh, z2]),
            jnp.concatenate([-oh, zh, z2]), jnp.concatenate([zh, oh, z2])]
    consts = jnp.stack(rows + [jnp.zeros((LANES,), F32)] * (SUBLANES - len(rows)))
    tm = 1024
    tab = jax.ShapeDtypeStruct((n, LANES), F32)
    return pl.pallas_call(
        _rope_tab_kernel,
        grid=(n // tm,),
        in_specs=[pl.BlockSpec((tm, 1), lambda i: (i, 0)),
                  pl.BlockSpec((SUBLANES, LANES), lambda i: (0, 0))],
        out_specs=[pl.BlockSpec((tm, LANES), lambda i: (i, 0))] * 3,
        out_shape=[tab, tab, tab],
        compiler_params=_cparams(("arbitrary",)),
        name="rope_tables",
    )(positions.reshape(n, 1), consts)


def _rope_tile(t, c_ref, s1_ref, s2_ref):
    return (t * c_ref[...] + pltpu.roll(t, LANES - QK_ROPE // 2, 1) * s1_ref[...]
            + pltpu.roll(t, QK_ROPE // 2, 1) * s2_ref[...])


def _qkvproj_kernel(lat_ref, gq_ref, gkv_ref, wq_ref, wk_ref, wv_ref, kpe_ref, c_ref, s1_ref, s2_ref,
                    q_ref, k_ref, v_ref, *, scale):
    lat = lat_ref[...]
    qn = _rms(lat[:, :Q_LORA], gq_ref[...]).astype(BF16)
    cn = _rms(lat[:, Q_LORA:], gkv_ref[...]).astype(BF16)
    q = jnp.dot(qn, wq_ref[...], preferred_element_type=F32)
    kn = jnp.dot(cn, wk_ref[...], preferred_element_type=F32)
    v_ref[...] = jnp.dot(cn, wv_ref[...], preferred_element_type=F32).astype(v_ref.dtype)
    kt = _rope_tile(kpe_ref[...], c_ref, s1_ref, s2_ref).astype(k_ref.dtype)
    for h in range(MLA_HEADS):
        o = h * QK_PAD
        q_ref[:, o:o + LANES] = (q[:, o:o + LANES] * scale).astype(q_ref.dtype)
        qt = _rope_tile(q[:, o + LANES:o + QK_PAD], c_ref, s1_ref, s2_ref)
        q_ref[:, o + LANES:o + QK_PAD] = (qt * scale).astype(q_ref.dtype)
        k_ref[:, o:o + LANES] = kn[:, h * QK_NOPE:(h + 1) * QK_NOPE].astype(k_ref.dtype)
        k_ref[:, o + LANES:o + QK_PAD] = kt


def _qkvproj(lat, g_q, g_kv, w_q, w_k, w_v, kpe, rope_c, rope_s1, rope_s2):
    n = lat.shape[0]
    tm = 512
    row = lambda i: (i, 0)
    const = lambda i: (0, 0)
    tab = pl.BlockSpec((tm, LANES), row)
    return pl.pallas_call(
        functools.partial(_qkvproj_kernel, scale=(QK_NOPE + QK_ROPE) ** -0.5),
        grid=(n // tm,),
        in_specs=[pl.BlockSpec((tm, Q_LORA + KV_LORA), row),
                  pl.BlockSpec((1, Q_LORA), const),
                  pl.BlockSpec((1, KV_LORA), const),
                  pl.BlockSpec((Q_LORA, MLA_HEADS * QK_PAD), const),
                  pl.BlockSpec((KV_LORA, MLA_HEADS * QK_NOPE), const),
                  pl.BlockSpec((KV_LORA, MLA_HEADS * V_HEAD), const),
                  tab, tab, tab, tab],
        out_specs=[pl.BlockSpec((tm, MLA_HEADS * QK_PAD), row),
                   pl.BlockSpec((tm, MLA_HEADS * QK_PAD), row),
                   pl.BlockSpec((tm, MLA_HEADS * V_HEAD), row)],
        out_shape=[jax.ShapeDtypeStruct((n, MLA_HEADS * QK_PAD), BF16),
                   jax.ShapeDtypeStruct((n, MLA_HEADS * QK_PAD), BF16),
                   jax.ShapeDtypeStruct((n, MLA_HEADS * V_HEAD), BF16)],
        compiler_params=_cparams(("arbitrary",)),
        name="qkvproj",
    )(lat, g_q.reshape(1, Q_LORA), g_kv.reshape(1, KV_LORA), w_q, w_k, w_v, kpe, rope_c, rope_s1, rope_s2)


def _attn_kernel(hi_ref, q_ref, k_ref, v_ref, qc_ref, kc_ref, o_ref, m_scr, l_scr, acc_scr, *, nq):
    b = pl.program_id(0)
    i = pl.program_id(1)
    qc = qc_ref[...]
    m_scr[...] = jnp.full(m_scr.shape, NEG, F32)
    l_scr[...] = jnp.zeros(l_scr.shape, F32)
    acc_scr[...] = jnp.zeros(acc_scr.shape, F32)

    def body(j, _):
        off = pl.multiple_of(j * ATT_TK, ATT_TK)
        mask = kc_ref[j] <= qc
        for h in range(MLA_HEADS):
            q = q_ref[:, h * QK_PAD:(h + 1) * QK_PAD]
            k = k_ref[pl.ds(off, ATT_TK), h * QK_PAD:(h + 1) * QK_PAD]
            s = lax.dot_general(q, k, (((1,), (1,)), ((), ())), preferred_element_type=F32)
            s = jnp.where(mask, s, NEG)
            m_old = m_scr[h]
            m_new = jnp.maximum(m_old, jnp.max(s, axis=-1, keepdims=True))
            p = jnp.exp(s - jnp.concatenate([m_new] * (ATT_TK // LANES), axis=1))
            alpha = jnp.exp(m_old - m_new)
            l_scr[h] = alpha * l_scr[h] + jnp.sum(p, axis=-1, keepdims=True)
            v = v_ref[pl.ds(off, ATT_TK), h * V_HEAD:(h + 1) * V_HEAD]
            acc_scr[h] = alpha * acc_scr[h] + jnp.dot(p.astype(BF16), v, preferred_element_type=F32)
            m_scr[h] = m_new
        return 0

    lax.fori_loop(0, hi_ref[b * nq + i], body, 0)
    for h in range(MLA_HEADS):
        o_ref[:, h * V_HEAD:(h + 1) * V_HEAD] = (acc_scr[h] / l_scr[h]).astype(o_ref.dtype)


def _attention(q, k, v, chunk_id, bn, l):
    nq = l // ATT_TQ
    nk = l // ATT_TK
    q_max = jnp.max(chunk_id.reshape(bn, nq, ATT_TQ), axis=-1)
    k_min = jnp.min(chunk_id.reshape(bn, nk, ATT_TK), axis=-1)
    needed = k_min[:, None, :] <= q_max[:, :, None]
    hi = jnp.max(jnp.where(needed, jnp.arange(1, nk + 1, dtype=jnp.int32), 0), axis=-1).reshape(bn * nq)
    qc = chunk_id.reshape(bn * l, 1)
    kc = chunk_id.reshape(bn * nk, 1, ATT_TK)
    grid_spec = pltpu.PrefetchScalarGridSpec(
        num_scalar_prefetch=1,
        grid=(bn, nq),
        in_specs=[pl.BlockSpec((ATT_TQ, MLA_HEADS * QK_PAD), lambda b, i, hi: (b * nq + i, 0)),
                  pl.BlockSpec((l, MLA_HEADS * QK_PAD), lambda b, i, hi: (b, 0)),
                  pl.BlockSpec((l, MLA_HEADS * V_HEAD), lambda b, i, hi: (b, 0)),
                  pl.BlockSpec((ATT_TQ, 1), lambda b, i, hi: (b * nq + i, 0)),
                  pl.BlockSpec((nk, 1, ATT_TK), lambda b, i, hi: (b, 0, 0))],
        out_specs=pl.BlockSpec((ATT_TQ, MLA_HEADS * V_HEAD), lambda b, i, hi: (b * nq + i, 0)),
        scratch_shapes=[pltpu.VMEM((MLA_HEADS, ATT_TQ, LANES), F32),
                        pltpu.VMEM((MLA_HEADS, ATT_TQ, LANES), F32),
                        pltpu.VMEM((MLA_HEADS, ATT_TQ, V_HEAD), F32)],
    )
    return pl.pallas_call(
        functools.partial(_attn_kernel, nq=nq),
        grid_spec=grid_spec,
        out_shape=jax.ShapeDtypeStruct((bn * l, MLA_HEADS * V_HEAD), BF16),
        compiler_params=_cparams(("arbitrary", "arbitrary")),
        name="attention",
    )(hi, q, k, v, qc, kc)


def _s5_kernel(u_ref, wb_ref, a_ref, wc_ref, d_ref, o_ref, x_scr, st_scr):
    rows = S5_TC * SUBLANES
    nsub = rows // S5_RB

    @pl.when(pl.program_id(1) == 0)
    def _():
        st_scr[...] = jnp.zeros_like(st_scr)

    even = (lax.broadcasted_iota(jnp.int32, (S5_RB, 1), 0) & 1) == 0

    def mm_in(r, _):
        off = pl.multiple_of(r * S5_RB, S5_RB)
        out = jnp.dot(u_ref[0, pl.ds(off, S5_RB), :].astype(BF16), wb_ref[0], preferred_element_type=F32)
        x_scr[pl.ds(off, S5_RB), :] = jnp.where(even, out[:, :2 * S5_LANES], out[:, 2 * S5_LANES:])
        return 0

    lax.fori_loop(0, nsub, mm_in, 0)

    ar = a_ref[0, :, :S5_LANES]
    ai = a_ref[0, :, S5_LANES:]

    def step(t, carry):
        xr, xi = carry
        off = pl.multiple_of(t * SUBLANES, SUBLANES)
        nr = ar * xr - ai * xi + x_scr[pl.ds(off, SUBLANES), :S5_LANES]
        ni = ar * xi + ai * xr + x_scr[pl.ds(off, SUBLANES), S5_LANES:]
        x_scr[pl.ds(off, SUBLANES), :S5_LANES] = nr
        x_scr[pl.ds(off, SUBLANES), S5_LANES:] = ni
        return nr, ni

    xr, xi = lax.fori_loop(0, S5_TC, step, (st_scr[:, :S5_LANES], st_scr[:, S5_LANES:]), unroll=8)
    st_scr[:, :S5_LANES] = xr
    st_scr[:, S5_LANES:] = xi

    d = jnp.concatenate([d_ref[0]] * (S5_RB // SUBLANES), axis=0)

    def mm_out(r, _):
        off = pl.multiple_of(r * S5_RB, S5_RB)
        out = jnp.dot(x_scr[pl.ds(off, S5_RB), :].astype(BF16), wc_ref[0], preferred_element_type=F32)
        y = jnp.where(even, out[:, :LANES], out[:, LANES:]) + d * u_ref[0, pl.ds(off, S5_RB), :]
        o_ref[0, pl.ds(off, S5_RB), :] = jax.nn.gelu(y).astype(o_ref.dtype)
        return 0

    lax.fori_loop(0, nsub, mm_out, 0)


def _s5_params(a_re, a_im, log_dt, b_re, b_im, c_re, c_im, d_skip):
    step = jnp.exp(log_dt)[:, None]
    mag = jnp.exp(a_re * step)
    abar_re, abar_im = mag * jnp.cos(a_im * step), mag * jnp.sin(a_im * step)
    den = a_re * a_re + a_im * a_im
    nr, ni = abar_re - 1.0, abar_im
    f_re, f_im = (nr * a_re + ni * a_im) / den, (ni * a_re - nr * a_im) / den
    bbar_re = f_re[..., None] * b_re - f_im[..., None] * b_im
    bbar_im = f_re[..., None] * b_im + f_im[..., None] * b_re
    eye = jnp.eye(S5_GB, dtype=F32)
    bb = jnp.stack([bbar_re, bbar_im]).reshape(2, S5_NBLK, S5_GB, S5_STATE, S5_GROUP)
    wb = jnp.einsum('ab,rjapc->jacrbp', eye, bb).reshape(S5_NBLK, LANES, 2 * S5_LANES)
    wb = wb.reshape(S5_PAIRS, 2, LANES, 2 * S5_LANES).transpose(0, 2, 1, 3).reshape(S5_PAIRS, LANES, 4 * S5_LANES)
    cc = jnp.stack([c_re, -c_im]).reshape(2, S5_NBLK, S5_GB, S5_GROUP, S5_STATE)
    wc = jnp.einsum('ab,rjacp->jrapbc', eye, cc).reshape(S5_NBLK, 2 * S5_LANES, LANES)
    wc = wc.reshape(S5_PAIRS, 2, 2 * S5_LANES, LANES).transpose(0, 2, 1, 3).reshape(S5_PAIRS, 2 * S5_LANES, 2 * LANES)
    ab = jnp.concatenate([abar_re.reshape(S5_NBLK, S5_LANES), abar_im.reshape(S5_NBLK, S5_LANES)], axis=1)
    ab = jnp.tile(ab.reshape(S5_PAIRS, 2, 2 * S5_LANES), (1, SUBLANES // 2, 1))
    dd = jnp.tile(d_skip.reshape(S5_PAIRS, 2, LANES), (1, SUBLANES // 2, 1))
    return wb.astype(BF16), ab, wc.astype(BF16), dd


def _s5(u6, wb, ab, wc, dd, l):
    rows = S5_TC * SUBLANES
    return pl.pallas_call(
        _s5_kernel,
        grid=(S5_PAIRS, l // S5_TC),
        in_specs=[pl.BlockSpec((1, rows, LANES), lambda k, c: (k, c, 0)),
                  pl.BlockSpec((1, LANES, 4 * S5_LANES), lambda k, c: (k, 0, 0)),
                  pl.BlockSpec((1, SUBLANES, 2 * S5_LANES), lambda k, c: (k, 0, 0)),
                  pl.BlockSpec((1, 2 * S5_LANES, 2 * LANES), lambda k, c: (k, 0, 0)),
                  pl.BlockSpec((1, SUBLANES, LANES), lambda k, c: (k, 0, 0))],
        out_specs=pl.BlockSpec((1, rows, LANES), lambda k, c: (k, c, 0)),
        out_shape=jax.ShapeDtypeStruct(u6.shape, BF16),
        scratch_shapes=[pltpu.VMEM((rows, 2 * S5_LANES), F32),
                        pltpu.VMEM((SUBLANES, 2 * S5_LANES), F32)],
        compiler_params=_cparams(("arbitrary", "arbitrary")),
        name="s5",
    )(u6, wb, ab, wc, dd)


def _glu_kernel(y_ref, w_ref, o_ref):
    y = jnp.concatenate([y_ref[k] for k in range(S5_PAIRS)], axis=1)
    g = jnp.dot(y, w_ref[...], preferred_element_type=F32)
    o_ref[...] = (y.astype(F32) * jax.nn.sigmoid(g)).astype(o_ref.dtype)


def _glu(y4, w_glu):
    m = y4.shape[1]
    tm = 1024
    return pl.pallas_call(
        _glu_kernel,
        grid=(m // tm,),
        in_specs=[pl.BlockSpec((S5_PAIRS, tm, 2 * LANES), lambda i: (0, i, 0)),
                  pl.BlockSpec((S5_CH, S5_CH), lambda i: (0, 0))],
        out_specs=pl.BlockSpec((tm, S5_CH), lambda i: (i, 0)),
        out_shape=jax.ShapeDtypeStruct((m, S5_CH), BF16),
        compiler_params=_cparams(("arbitrary",)),
        name="glu",
    )(y4, w_glu)


def _merge_kernel(att_ref, z_ref, gm_ref, gs_ref, x_ref, mod_ref, gpost_ref, gpre_ref,
                  wbm_ref, wbs_ref, wo_ref, wrh_ref, wrl_ref, x1_ref, h2b_ref, h2s_ref, lg_ref):
    ym = jnp.dot(att_ref[...], wbm_ref[...], preferred_element_type=F32)
    ys = jnp.dot(z_ref[...], wbs_ref[...], preferred_element_type=F32)
    mixed_in = (gm_ref[...].astype(F32) * ym + gs_ref[...].astype(F32) * ys).astype(BF16)
    mixed = jnp.dot(mixed_in, wo_ref[...], preferred_element_type=F32)
    x1 = x_ref[0] + mod_ref[0, 2:3, :] * _rms(mixed, gpost_ref[...])
    x1_ref[0] = x1
    h2 = _rms(x1, gpre_ref[...]) * (1.0 + mod_ref[0, 4:5, :]) + mod_ref[0, 3:4, :]
    h2_hi = h2.astype(BF16)
    h2b_ref[0] = h2_hi
    tm = h2.shape[0]
    for s in range(SLAB):
        h2s_ref[pl.ds(s, tm, stride=SLAB), :] = h2[:, s * LANES:(s + 1) * LANES]
    h2_lo = (h2 - h2_hi.astype(F32)).astype(BF16)
    nt = (((1,), (1,)), ((), ()))
    lg_ref[...] = (lax.dot_general(wrh_ref[...], h2_hi, nt, preferred_element_type=F32)
                   + lax.dot_general(wrl_ref[...], h2_hi, nt, preferred_element_type=F32)
                   + lax.dot_general(wrh_ref[...], h2_lo, nt, preferred_element_type=F32))


def _merge(att, z_tm, gates, x, mod, g_post, g_pre_ffn, w_br_mla, w_br_s5, w_out, wr_hi, wr_lo):
    bn, l, d = x.shape
    tm = 256
    nl = l // tm
    row = lambda b, i: (b * nl + i, 0)
    const = lambda b, i: (0, 0)
    one = pl.Buffered(1)
    return pl.pallas_call(
        _merge_kernel,
        grid=(bn, nl),
        in_specs=[pl.BlockSpec((tm, MLA_HEADS * V_HEAD), row),
                  pl.BlockSpec((tm, S5_CH), lambda b, i: (i, b)),
                  pl.BlockSpec((tm, d), lambda b, i: (b * nl + i, 0)),
                  pl.BlockSpec((tm, d), lambda b, i: (b * nl + i, 1)),
                  pl.BlockSpec((1, tm, d), lambda b, i: (b, i, 0)),
                  pl.BlockSpec((1, 6, d), lambda b, i: (b, 0, 0)),
                  pl.BlockSpec((1, d), const),
                  pl.BlockSpec((1, d), const),
                  pl.BlockSpec((MLA_HEADS * V_HEAD, d), const, pipeline_mode=one),
                  pl.BlockSpec((S5_CH, d), const, pipeline_mode=one),
                  pl.BlockSpec((d, d), const, pipeline_mode=one),
                  pl.BlockSpec((N_EXPERTS, d), const, pipeline_mode=one),
                  pl.BlockSpec((N_EXPERTS, d), const, pipeline_mode=one)],
        out_specs=[pl.BlockSpec((1, tm, d), lambda b, i: (b, i, 0)),
                   pl.BlockSpec((1, tm, d), lambda b, i: (b, i, 0)),
                   pl.BlockSpec((tm * SLAB, LANES), row),
                   pl.BlockSpec((N_EXPERTS, tm), lambda b, i: (0, b * nl + i))],
        out_shape=[jax.ShapeDtypeStruct((bn, l, d), F32),
                   jax.ShapeDtypeStruct((bn, l, d), BF16),
                   jax.ShapeDtypeStruct((bn * l * SLAB, LANES), F32),
                   jax.ShapeDtypeStruct((N_EXPERTS, bn * l), F32)],
        compiler_params=_cparams(("arbitrary", "arbitrary")),
        name="merge",
    )(att, z_tm, gates, gates, x, mod, g_post.reshape(1, d), g_pre_ffn.reshape(1, d),
      w_br_mla, w_br_s5, w_out, wr_hi, wr_lo)


def _route_kernel(lg_ref, b_ref, ti_ref, tw_ref):
    ng = N_EXPERT_GROUPS
    gsz = N_EXPERTS // ng
    lg = lg_ref[...]
    t = lg.shape[-1]
    sc = jax.nn.sigmoid(lg)
    sel = sc + b_ref[...]
    ninf = -jnp.inf
    i_in = lax.broadcasted_iota(jnp.int32, (ng, gsz, t), 1).astype(F32)
    m1 = jnp.max(sel, axis=1, keepdims=True)
    idx1 = jnp.min(jnp.where(sel == m1, i_in, float(gsz)), axis=1, keepdims=True)
    m2 = jnp.max(jnp.where(i_in == idx1, ninf, sel), axis=1, keepdims=True)
    gs = m1 + m2
    g_i = lax.broadcasted_iota(jnp.int32, (ng, 1, t), 0).astype(F32)
    picked = jnp.zeros((ng, 1, t), F32)
    cur = gs
    for _ in range(TOPK_GROUPS):
        m = jnp.max(cur, axis=0, keepdims=True)
        idx = jnp.min(jnp.where(cur == m, g_i, float(ng)), axis=0, keepdims=True)
        hit = g_i == idx
        picked = jnp.where(hit, 1.0, picked)
        cur = jnp.where(hit, ninf, cur)
    cand = jnp.where(picked > 0.5, sel, ninf)
    e_i = lax.broadcasted_iota(jnp.int32, (ng, gsz, t), 0).astype(F32) * float(gsz) + i_in
    ws = []
    for r in range(TOP_K):
        m = jnp.max(jnp.max(cand, axis=1, keepdims=True), axis=0, keepdims=True)
        idx = jnp.min(jnp.min(jnp.where(cand == m, e_i, float(N_EXPERTS)), axis=1, keepdims=True),
                      axis=0, keepdims=True)
        hit = e_i == idx
        w = jnp.sum(jnp.sum(jnp.where(hit, sc, 0.0), axis=1, keepdims=True), axis=0, keepdims=True)
        ti_ref[r:r + 1, :] = idx[0].astype(jnp.int32)
        ws.append(w[0])
        cand = jnp.where(hit, ninf, cand)
    tot = ws[0]
    for w in ws[1:]:
        tot = tot + w
    for r in range(TOP_K):
        tw_ref[r:r + 1, :] = ws[r] / tot * ROUTED_SCALE
    for r in range(TOP_K, SUBLANES):
        ti_ref[r:r + 1, :] = jnp.zeros((1, t), jnp.int32)
        tw_ref[r:r + 1, :] = jnp.zeros((1, t), F32)


def _route(logits_t, router_bias):
    n = logits_t.shape[1]
    ng = N_EXPERT_GROUPS
    gsz = N_EXPERTS // ng
    tn = 1024
    return pl.pallas_call(
        _route_kernel,
        grid=(n // tn,),
        in_specs=[pl.BlockSpec((ng, gsz, tn), lambda i: (0, 0, i)),
                  pl.BlockSpec((ng, gsz, 1), lambda i: (0, 0, 0))],
        out_specs=[pl.BlockSpec((SUBLANES, tn), lambda i: (0, i)),
                   pl.BlockSpec((SUBLANES, tn), lambda i: (0, i))],
        out_shape=[jax.ShapeDtypeStruct((SUBLANES, n), jnp.int32),
                   jax.ShapeDtypeStruct((SUBLANES, n), F32)],
        compiler_params=_cparams(("arbitrary",)),
        name="route",
    )(logits_t.reshape(ng, gsz, n), router_bias.reshape(ng, gsz, 1))


def _moe_kernel(blk_e_ref, nused_ref, nvalid_ref, t_ref, tn_ref, a_ref, h_hbm, wg_ref, wu_ref, wd_ref,
                y_hbm, xbuf, ybuf, wg_s, wu_s, wd_s, gsem, wsem):
    j = pl.program_id(0)
    n_used = nused_ref[0]
    slot = lax.rem(j, 2)
    nv = nvalid_ref[j]
    nv_prev = nvalid_ref[jnp.maximum(j - 1, 0)]

    def slab(ref, row0):
        return ref.at[pl.ds(pl.multiple_of(row0, SUBLANES), SLAB), :]

    def start_gather(idx_ref, s):
        def body(r, _):
            pltpu.make_async_copy(slab(h_hbm, idx_ref[0, 0, r]), slab(xbuf.at[s], r * MOE_PITCH), gsem.at[s]).start()
            return 0
        lax.fori_loop(0, MOE_TB, body, 0, unroll=8)

    def wait_gather(s):
        rows = MOE_TB * SLAB
        pltpu.make_async_copy(h_hbm.at[pl.ds(0, rows), :], xbuf.at[s, pl.ds(0, rows), :], gsem.at[s]).wait()

    def wait_writes(count):
        p = MOE_TB
        while p >= 1:
            @pl.when((count & p) != 0)
            def _(p=p):
                rows = p * SLAB
                pltpu.make_async_copy(ybuf.at[pl.ds(0, rows), :], y_hbm.at[pl.ds(0, rows), :], wsem.at[0]).wait()
            p //= 2

    @pl.when(j < n_used)
    def _():
        @pl.when(j == 0)
        def _():
            start_gather(t_ref, 0)

        first = jnp.logical_or(j == 0, blk_e_ref[j] != blk_e_ref[jnp.maximum(j - 1, 0)])

        @pl.when(first)
        def _():
            wg_s[...] = wg_ref[0].astype(BF16)
            wu_s[...] = wu_ref[0].astype(BF16)
            wd_s[...] = wd_ref[0].astype(BF16)

        wait_gather(slot)

        @pl.when(j + 1 < n_used)
        def _():
            start_gather(tn_ref, 1 - slot)

        xs = xbuf.at[slot]
        x = jnp.concatenate([xs[pl.ds(s, MOE_TB, stride=MOE_PITCH), :] for s in range(SLAB)], axis=1).astype(BF16)
        g = jnp.dot(x, wg_s[...], preferred_element_type=F32)
        u = jnp.dot(x, wu_s[...], preferred_element_type=F32)
        hm = (g * jax.nn.sigmoid(g) * u).astype(BF16)
        y = jnp.dot(hm, wd_s[...], preferred_element_type=F32)

        @pl.when(j > 0)
        def _():
            wait_writes(nv_prev)

        for s in range(SLAB):
            ybuf[pl.ds(s, MOE_TB, stride=MOE_PITCH), :] = y[:, s * LANES:(s + 1) * LANES]

        def wbody(r, _):
            pltpu.make_async_copy(slab(ybuf, r * MOE_PITCH), slab(y_hbm, a_ref[0, 0, r]), wsem.at[0]).start()
            return 0
        lax.fori_loop(0, nv, wbody, 0)

        @pl.when(j == n_used - 1)
        def _():
            wait_writes(nv)


def _moe(h2s, topi, w_gate, w_up, w_down):
    n = h2s.shape[0] // SLAB
    d = D_MODEL
    nk = n * TOP_K
    tb = MOE_TB
    n_blocks = nk // tb + N_EXPERTS
    flat_e = topi[:TOP_K].reshape(nk)
    se, order = lax.sort_key_val(flat_e, jnp.arange(nk, dtype=jnp.int32))
    bounds = jnp.searchsorted(se, jnp.arange(N_EXPERTS + 1, dtype=jnp.int32), side='left').astype(jnp.int32)
    start = bounds[:-1]
    counts = bounds[1:] - start
    padded = (counts + tb - 1) // tb * tb
    pad_end = jnp.cumsum(padded)
    pad_start = pad_end - padded
    n_used = (pad_end[-1] // tb).astype(jnp.int32).reshape(1)
    blk_p0 = jnp.arange(n_blocks, dtype=jnp.int32) * tb
    blk_e = jnp.minimum(jnp.searchsorted(pad_end, blk_p0, side='right'), N_EXPERTS - 1).astype(jnp.int32)
    blk_off = blk_p0 - pad_start[blk_e]
    nvalid = jnp.clip(counts[blk_e] - blk_off, 0, tb).astype(jnp.int32)
    rank = (start[blk_e] + blk_off)[:, None] + jnp.arange(tb, dtype=jnp.int32)[None, :]
    a3 = order[jnp.clip(rank, 0, nk - 1)].reshape(n_blocks, 1, tb)
    arow = a3 * SLAB
    trow = (a3 % n) * SLAB

    smem_blk = lambda f: pl.BlockSpec((1, 1, tb), f, memory_space=pltpu.SMEM)
    grid_spec = pltpu.PrefetchScalarGridSpec(
        num_scalar_prefetch=3,
        grid=(n_blocks,),
        in_specs=[smem_blk(lambda j, be, nu, nv: (j, 0, 0)),
                  smem_blk(lambda j, be, nu, nv: (jnp.minimum(j + 1, n_blocks - 1), 0, 0)),
                  smem_blk(lambda j, be, nu, nv: (j, 0, 0)),
                  pl.BlockSpec(memory_space=pl.ANY),
                  pl.BlockSpec((1, d, D_EXPERT), lambda j, be, nu, nv: (be[j], 0, 0)),
                  pl.BlockSpec((1, d, D_EXPERT), lambda j, be, nu, nv: (be[j], 0, 0)),
                  pl.BlockSpec((1, D_EXPERT, d), lambda j, be, nu, nv: (be[j], 0, 0))],
        out_specs=pl.BlockSpec(memory_space=pl.ANY),
        scratch_shapes=[pltpu.VMEM((2, tb * MOE_PITCH, LANES), F32),
                        pltpu.VMEM((tb * MOE_PITCH, LANES), F32),
                        pltpu.VMEM((d, D_EXPERT), BF16),
                        pltpu.VMEM((d, D_EXPERT), BF16),
                        pltpu.VMEM((D_EXPERT, d), BF16),
                        pltpu.SemaphoreType.DMA((2,)),
                        pltpu.SemaphoreType.DMA((1,))],
    )
    return pl.pallas_call(
        _moe_kernel,
        grid_spec=grid_spec,
        out_shape=jax.ShapeDtypeStruct((TOP_K * n * SLAB, LANES), F32),
        compiler_params=_cparams(("arbitrary",)),
        name="moe",
    )(blk_e, n_used, nvalid, trow, trow, arow, h2s, w_gate, w_up, w_down)


def _final_kernel(h_ref, y0, y1, y2, y3, y4, y5, tw_ref, x1_ref, mod_ref, g_ref, wg_ref, wu_ref, wd_ref, o_ref):
    h = h_ref[...]
    tm = h.shape[0]
    g = jnp.dot(h, wg_ref[...], preferred_element_type=F32)
    u = jnp.dot(h, wu_ref[...], preferred_element_type=F32)
    hm = (g * jax.nn.sigmoid(g) * u).astype(BF16)
    ffn = jnp.dot(hm, wd_ref[...], preferred_element_type=F32)
    tw = tw_ref[...]
    for k, y in enumerate((y0, y1, y2, y3, y4, y5)):
        yk = jnp.concatenate([y[pl.ds(s, tm, stride=SLAB), :] for s in range(SLAB)], axis=1)
        ffn = ffn + yk * tw[:, k:k + 1]
    o_ref[0] = x1_ref[0] + mod_ref[0, 5:6, :] * _rms(ffn, g_ref[...])


def _final(h2, y_flat, topw_t, x1, mod, g_post_ffn, w_sg, w_su, w_sd):
    bn, l, d = x1.shape
    n = bn * l
    tm = 256
    nl = l // tm
    const = lambda b, i: (0, 0)
    one = pl.Buffered(1)
    y_specs = [pl.BlockSpec((tm * SLAB, LANES), functools.partial(lambda b, i, k: (k * (n // tm) + b * nl + i, 0), k=k))
               for k in range(TOP_K)]
    return pl.pallas_call(
        _final_kernel,
        grid=(bn, nl),
        in_specs=[pl.BlockSpec((tm, d), lambda b, i: (b * nl + i, 0))] + y_specs + [
            pl.BlockSpec((tm, SUBLANES), lambda b, i: (b * nl + i, 0)),
            pl.BlockSpec((1, tm, d), lambda b, i: (b, i, 0)),
            pl.BlockSpec((1, 6, d), lambda b, i: (b, 0, 0)),
            pl.BlockSpec((1, d), const),
            pl.BlockSpec((d, D_EXPERT), const, pipeline_mode=one),
            pl.BlockSpec((d, D_EXPERT), const, pipeline_mode=one),
            pl.BlockSpec((D_EXPERT, d), const, pipeline_mode=one)],
        out_specs=pl.BlockSpec((1, tm, d), lambda b, i: (b, i, 0)),
        out_shape=jax.ShapeDtypeStruct((bn, l, d), F32),
        compiler_params=_cparams(("arbitrary", "arbitrary")),
        name="final",
    )(h2, *([y_flat] * TOP_K), topw_t, x1, mod, g_post_ffn.reshape(1, d), w_sg, w_su, w_sd)


def _layer(x, c, positions, w_ada, b_ada, g_pre_mix, g_post_mix, g_pre_ffn, g_post_ffn, w_in, g_q, g_kv,
           w_uq, w_uk, w_uv, a_re, a_im, log_dt, b_re, b_im, c_re, c_im, d_skip, w_glu, w_br_mla, w_br_s5,
           w_out, w_router, router_bias, w_exp_gate, w_exp_up, w_exp_down, w_sh_gate, w_sh_up, w_sh_down):
    bn, l, d = x.shape
    n = bn * l
    mod = _adaln(c, w_ada, b_ada).reshape(bn, 6, d)

    o_kpe = Q_LORA + KV_LORA
    o_u = o_kpe + QK_ROPE
    o_g = o_u + S5_CH
    w_lat = w_in[:, :o_kpe].astype(BF16)
    w_kpe = jnp.pad(w_in[:, o_kpe:o_u], ((0, 0), (0, LANES - QK_ROPE))).astype(BF16)
    w_u = w_in[:, o_u:o_g].astype(BF16)
    w_gates = w_in[:, o_g:].astype(BF16)
    w_q = jnp.pad(w_uq.reshape(Q_LORA, MLA_HEADS, QK_NOPE + QK_ROPE),
                  ((0, 0), (0, 0), (0, QK_PAD - QK_NOPE - QK_ROPE))).reshape(Q_LORA, MLA_HEADS * QK_PAD).astype(BF16)
    wr_t = w_router.T
    wr_hi = wr_t.astype(BF16)
    wr_lo = (wr_t - wr_hi.astype(F32)).astype(BF16)

    rope_c, rope_s1, rope_s2 = _rope_tables(positions)
    chunk_id = positions // CHUNK

    h = _prenorm(x, mod, g_pre_mix, 0, 1).reshape(n, d)
    lat = _mm(h, w_lat, F32, 1024, 512, name="mm_lat")
    kpe = _mm(h, w_kpe, F32, 1024, LANES, name="mm_kpe")
    gates = _mm(h, w_gates, BF16, 1024, 512, act="sigmoid", name="mm_gates")
    u6 = _mm_u(h, w_u, bn, l).reshape(S5_PAIRS, l * SUBLANES, LANES)

    q, k, v = _qkvproj(lat, g_q, g_kv, w_q, w_uk.astype(BF16), w_uv.astype(BF16), kpe, rope_c, rope_s1, rope_s2)
    att = _attention(q, k, v, chunk_id, bn, l)

    wb, ab, wc, dd = _s5_params(a_re, a_im, log_dt, b_re, b_im, c_re, c_im, d_skip)
    y6 = _s5(u6, wb, ab, wc, dd, l)
    z = _glu(y6.reshape(S5_PAIRS, l * bn, 2 * LANES), w_glu.astype(BF16))
    z_tm = z.reshape(l, bn * S5_CH)

    x1, h2b, h2s, logits_t = _merge(att, z_tm, gates, x, mod, g_post_mix, g_pre_ffn, w_br_mla.astype(BF16),
                                    w_br_s5.astype(BF16), w_out.astype(BF16), wr_hi, wr_lo)

    topi, topw = _route(logits_t, router_bias)
    y_slabs = _moe(h2s, topi, w_exp_gate, w_exp_up, w_exp_down)
    return _final(h2b.reshape(n, d), y_slabs, topw.T, x1, mod, g_post_ffn, w_sh_gate.astype(BF16),
                  w_sh_up.astype(BF16), w_sh_down.astype(BF16))


def kernel(x, c, positions, w_ada, b_ada, g_pre_mix, g_post_mix, g_pre_ffn, g_post_ffn, w_in, g_q, g_kv, w_uq, w_uk, w_uv, a_re, a_im, log_dt, b_re, b_im, c_re, c_im, d_skip, w_glu, w_br_mla, w_br_s5, w_out, w_router, router_bias, w_exp_gate, w_exp_up, w_exp_down, w_sh_gate, w_sh_up, w_sh_down):
    depth = w_ada.shape[0]
    for li in range(depth):
        x = _layer(x, c, positions, w_ada[li], b_ada[li], g_pre_mix[li], g_post_mix[li], g_pre_ffn[li],
                   g_post_ffn[li], w_in[li], g_q[li], g_kv[li], w_uq[li], w_uk[li], w_uv[li], a_re[li], a_im[li],
                   log_dt[li], b_re[li], b_im[li], c_re[li], c_im[li], d_skip[li], w_glu[li], w_br_mla[li],
                   w_br_s5[li], w_out[li], w_router[li], router_bias[li], w_exp_gate[li], w_exp_up[li],
                   w_exp_down[li], w_sh_gate[li], w_sh_up[li], w_sh_down[li])
    return x
```

```python
import functools

import jax
import jax.numpy as jnp
from jax import lax
from jax.experimental import pallas as pl
from jax.experimental.pallas import tpu as pltpu

F32 = jnp.float32
BF16 = jnp.bfloat16

D_MODEL = 2048
CHUNK = 64
EPS = 1e-6
MLA_HEADS = 8
QK_NOPE = 128
QK_ROPE = 64
V_HEAD = 128
Q_LORA = 512
KV_LORA = 512
ROPE_THETA = 10000.0
S5_CH = 1024
S5_GROUP = 16
S5_GROUPS = S5_CH // S5_GROUP
S5_STATE = 64
N_EXPERTS = 64
TOP_K = 6
N_EXPERT_GROUPS = 8
TOPK_GROUPS = 4
D_EXPERT = 512
ROUTED_SCALE = 2.5

LANES = 128
SUBLANES = 8
QK_PAD = 2 * LANES
VMEM_LIMIT = 56 * 1024 * 1024
NEG = -1e30

S5_GB = LANES // S5_GROUP
S5_NBLK = S5_CH // LANES
S5_PAIRS = S5_NBLK // 2
S5_LANES = S5_GB * S5_STATE
S5_TC = 256
S5_RB = 512

ATT_TQ = 256
ATT_TK = 256
MOE_TB = 256
SLAB = D_MODEL // LANES
MOE_PITCH = 24


def _cparams(sem):
    return pltpu.CompilerParams(dimension_semantics=sem, vmem_limit_bytes=VMEM_LIMIT)


def _rms(x, g):
    return x * lax.rsqrt(jnp.mean(x * x, axis=-1, keepdims=True) + EPS) * g


def _adaln_kernel(c_ref, w_ref, b_ref, o_ref):
    c = c_ref[...]
    a = (c * jax.nn.sigmoid(c)).astype(BF16)
    o_ref[...] = jnp.dot(a, w_ref[...].astype(BF16), preferred_element_type=F32) + b_ref[...]


def _adaln(c, w, b):
    bn, d = c.shape
    n = w.shape[1]
    tn = 1024
    return pl.pallas_call(
        _adaln_kernel,
        grid=(n // tn,),
        in_specs=[pl.BlockSpec((bn, d), lambda j: (0, 0)),
                  pl.BlockSpec((d, tn), lambda j: (0, j)),
                  pl.BlockSpec((1, tn), lambda j: (0, j))],
        out_specs=pl.BlockSpec((bn, tn), lambda j: (0, j)),
        out_shape=jax.ShapeDtypeStruct((bn, n), F32),
        compiler_params=_cparams(("arbitrary",)),
        name="adaln",
    )(c, w, b.reshape(1, n))


def _prenorm_kernel(x_ref, mod_ref, g_ref, o_ref, *, sh_row, sc_row):
    y = _rms(x_ref[0], g_ref[...])
    o_ref[0] = (y * (1.0 + mod_ref[0, sc_row:sc_row + 1, :]) + mod_ref[0, sh_row:sh_row + 1, :]).astype(o_ref.dtype)


def _prenorm(x, mod, g, sh_row, sc_row):
    bn, l, d = x.shape
    tl = 512
    return pl.pallas_call(
        functools.partial(_prenorm_kernel, sh_row=sh_row, sc_row=sc_row),
        grid=(bn, l // tl),
        in_specs=[pl.BlockSpec((1, tl, d), lambda b, i: (b, i, 0)),
                  pl.BlockSpec((1, 6, d), lambda b, i: (b, 0, 0)),
                  pl.BlockSpec((1, d), lambda b, i: (0, 0))],
        out_specs=pl.BlockSpec((1, tl, d), lambda b, i: (b, i, 0)),
        out_shape=jax.ShapeDtypeStruct((bn, l, d), BF16),
        compiler_params=_cparams(("arbitrary", "arbitrary")),
        name="prenorm",
    )(x, mod, g.reshape(1, d))


def _mm_kernel(a_ref, w_ref, o_ref, *, act):
    acc = jnp.dot(a_ref[...], w_ref[...], preferred_element_type=F32)
    if act == "sigmoid":
        acc = jax.nn.sigmoid(acc)
    o_ref[...] = acc.astype(o_ref.dtype)


def _mm(a, w, out_dtype, tm, tn, act=None, name="mm"):
    m, k = a.shape
    n = w.shape[1]
    return pl.pallas_call(
        functools.partial(_mm_kernel, act=act),
        grid=(n // tn, m // tm),
        in_specs=[pl.BlockSpec((tm, k), lambda j, i: (i, 0)),
                  pl.BlockSpec((k, tn), lambda j, i: (0, j))],
        out_specs=pl.BlockSpec((tm, tn), lambda j, i: (i, j)),
        out_shape=jax.ShapeDtypeStruct((m, n), out_dtype),
        compiler_params=_cparams(("arbitrary", "arbitrary")),
        name=name,
    )(a, w)


def _mm_u_kernel(a_ref, w_ref, o_ref):
    o_ref[0] = jnp.dot(a_ref[...], w_ref[...], preferred_element_type=F32)


def _mm_u(h, w, bn, l):
    m, k = h.shape
    tm = 1024
    nl = l // tm
    tn = 2 * LANES
    return pl.pallas_call(
        _mm_u_kernel,
        grid=(S5_PAIRS, bn, nl),
        in_specs=[pl.BlockSpec((tm, k), lambda j, b, i: (b * nl + i, 0)),
                  pl.BlockSpec((k, tn), lambda j, b, i: (0, j))],
        out_specs=pl.BlockSpec((1, tm, tn), lambda j, b, i: (j, i, b)),
        out_shape=jax.ShapeDtypeStruct((S5_PAIRS, l, bn * tn), F32),
        compiler_params=_cparams(("arbitrary", "arbitrary", "arbitrary")),
        name="mm_u",
    )(h, w)


def _rope_tab_kernel(pos_ref, k_ref, c_ref, s1_ref, s2_ref):
    ang = pos_ref[...].astype(F32) * k_ref[0:1, :]
    s = jnp.sin(ang)
    c_ref[...] = jnp.cos(ang) * k_ref[1:2, :]
    s1_ref[...] = s * k_ref[2:3, :]
    s2_ref[...] = s * k_ref[3:4, :]


def _rope_tables(positions):
    n = positions.size
    half = QK_ROPE // 2
    inv_freq = ROPE_THETA ** (-jnp.arange(half, dtype=F32) / half)
    zh, oh = jnp.zeros((half,), F32), jnp.ones((half,), F32)
    z2 = jnp.zeros((LANES - QK_ROPE,), F32)
    rows = [jnp.concatenate([inv_freq, inv_freq, z2]), jnp.concatenate([oh, oh, z2]),
            jnp.concatenate([-oh, zh, z2]), jnp.concatenate([zh, oh, z2])]
    consts = jnp.stack(rows + [jnp.zeros((LANES,), F32)] * (SUBLANES - len(rows)))
    tm = 1024
    tab = jax.ShapeDtypeStruct((n, LANES), F32)
    return pl.pallas_call(
        _rope_tab_kernel,
        grid=(n // tm,),
        in_specs=[pl.BlockSpec((tm, 1), lambda i: (i, 0)),
                  pl.BlockSpec((SUBLANES, LANES), lambda i: (0, 0))],
        out_specs=[pl.BlockSpec((tm, LANES), lambda i: (i, 0))] * 3,
        out_shape=[tab, tab, tab],
        compiler_params=_cparams(("arbitrary",)),
        name="rope_tables",
    )(positions.reshape(n, 1), consts)


def _rope_tile(t, c_ref, s1_ref, s2_ref):
    return (t * c_ref[...] + pltpu.roll(t, LANES - QK_ROPE // 2, 1) * s1_ref[...]
            + pltpu.roll(t, QK_ROPE // 2, 1) * s2_ref[...])


def _qkvproj_kernel(lat_ref, gq_ref, gkv_ref, wq_ref, wk_ref, wv_ref, kpe_ref, c_ref, s1_ref, s2_ref,
                    q_ref, k_ref, v_ref, *, scale):
    lat = lat_ref[...]
    qn = _rms(lat[:, :Q_LORA], gq_ref[...]).astype(BF16)
    cn = _rms(lat[:, Q_LORA:], gkv_ref[...]).astype(BF16)
    q = jnp.dot(qn, wq_ref[...], preferred_element_type=F32)
    kn = jnp.dot(cn, wk_ref[...], preferred_element_type=F32)
    v_ref[...] = jnp.dot(cn, wv_ref[...], preferred_element_type=F32).astype(v_ref.dtype)
    kt = _rope_tile(kpe_ref[...], c_ref, s1_ref, s2_ref).astype(k_ref.dtype)
    for h in range(MLA_HEADS):
        o = h * QK_PAD
        q_ref[:, o:o + LANES] = (q[:, o:o + LANES] * scale).astype(q_ref.dtype)
        qt = _rope_tile(q[:, o + LANES:o + QK_PAD], c_ref, s1_ref, s2_ref)
        q_ref[:, o + LANES:o + QK_PAD] = (qt * scale).astype(q_ref.dtype)
        k_ref[:, o:o + LANES] = kn[:, h * QK_NOPE:(h + 1) * QK_NOPE].astype(k_ref.dtype)
        k_ref[:, o + LANES:o + QK_PAD] = kt


def _qkvproj(lat, g_q, g_kv, w_q, w_k, w_v, kpe, rope_c, rope_s1, rope_s2):
    n = lat.shape[0]
    tm = 512
    row = lambda i: (i, 0)
    const = lambda i: (0, 0)
    tab = pl.BlockSpec((tm, LANES), row)
    return pl.pallas_call(
        functools.partial(_qkvproj_kernel, scale=(QK_NOPE + QK_ROPE) ** -0.5),
        grid=(n // tm,),
        in_specs=[pl.BlockSpec((tm, Q_LORA + KV_LORA), row),
                  pl.BlockSpec((1, Q_LORA), const),
                  pl.BlockSpec((1, KV_LORA), const),
                  pl.BlockSpec((Q_LORA, MLA_HEADS * QK_PAD), const),
                  pl.BlockSpec((KV_LORA, MLA_HEADS * QK_NOPE), const),
                  pl.BlockSpec((KV_LORA, MLA_HEADS * V_HEAD), const),
                  tab, tab, tab, tab],
        out_specs=[pl.BlockSpec((tm, MLA_HEADS * QK_PAD), row),
                   pl.BlockSpec((tm, MLA_HEADS * QK_PAD), row),
                   pl.BlockSpec((tm, MLA_HEADS * V_HEAD), row)],
        out_shape=[jax.ShapeDtypeStruct((n, MLA_HEADS * QK_PAD), BF16),
                   jax.ShapeDtypeStruct((n, MLA_HEADS * QK_PAD), BF16),
                   jax.ShapeDtypeStruct((n, MLA_HEADS * V_HEAD), BF16)],
        compiler_params=_cparams(("arbitrary",)),
        name="qkvproj",
    )(lat, g_q.reshape(1, Q_LORA), g_kv.reshape(1, KV_LORA), w_q, w_k, w_v, kpe, rope_c, rope_s1, rope_s2)


def _attn_kernel(hi_ref, q_ref, k_ref, v_ref, qc_ref, kc_ref, o_ref, m_scr, l_scr, acc_scr, *, nq):
    b = pl.program_id(0)
    i = pl.program_id(1)
    qc = qc_ref[...]
    m_scr[...] = jnp.full(m_scr.shape, NEG, F32)
    l_scr[...] = jnp.zeros(l_scr.shape, F32)
    acc_scr[...] = jnp.zeros(acc_scr.shape, F32)

    def body(j, _):
        off = pl.multiple_of(j * ATT_TK, ATT_TK)
        mask = kc_ref[j] <= qc
        for h in range(MLA_HEADS):
            q = q_ref[:, h * QK_PAD:(h + 1) * QK_PAD]
            k = k_ref[pl.ds(off, ATT_TK), h * QK_PAD:(h + 1) * QK_PAD]
            s = lax.dot_general(q, k, (((1,), (1,)), ((), ())), preferred_element_type=F32)
            s = jnp.where(mask, s, NEG)
            m_old = m_scr[h]
            m_new = jnp.maximum(m_old, jnp.max(s, axis=-1, keepdims=True))
            p = jnp.exp(s - jnp.concatenate([m_new] * (ATT_TK // LANES), axis=1))
            alpha = jnp.exp(m_old - m_new)
            l_scr[h] = alpha * l_scr[h] + jnp.sum(p, axis=-1, keepdims=True)
            v = v_ref[pl.ds(off, ATT_TK), h * V_HEAD:(h + 1) * V_HEAD]
            acc_scr[h] = alpha * acc_scr[h] + jnp.dot(p.astype(BF16), v, preferred_element_type=F32)
            m_scr[h] = m_new
        return 0

    lax.fori_loop(0, hi_ref[b * nq + i], body, 0)
    for h in range(MLA_HEADS):
        o_ref[:, h * V_HEAD:(h + 1) * V_HEAD] = (acc_scr[h] / l_scr[h]).astype(o_ref.dtype)


def _attention(q, k, v, chunk_id, bn, l):
    nq = l // ATT_TQ
    nk = l // ATT_TK
    q_max = jnp.max(chunk_id.reshape(bn, nq, ATT_TQ), axis=-1)
    k_min = jnp.min(chunk_id.reshape(bn, nk, ATT_TK), axis=-1)
    needed = k_min[:, None, :] <= q_max[:, :, None]
    hi = jnp.max(jnp.where(needed, jnp.arange(1, nk + 1, dtype=jnp.int32), 0), axis=-1).reshape(bn * nq)
    qc = chunk_id.reshape(bn * l, 1)
    kc = chunk_id.reshape(bn * nk, 1, ATT_TK)
    grid_spec = pltpu.PrefetchScalarGridSpec(
        num_scalar_prefetch=1,
        grid=(bn, nq),
        in_specs=[pl.BlockSpec((ATT_TQ, MLA_HEADS * QK_PAD), lambda b, i, hi: (b * nq + i, 0)),
                  pl.BlockSpec((l, MLA_HEADS * QK_PAD), lambda b, i, hi: (b, 0)),
                  pl.BlockSpec((l, MLA_HEADS * V_HEAD), lambda b, i, hi: (b, 0)),
                  pl.BlockSpec((ATT_TQ, 1), lambda b, i, hi: (b * nq + i, 0)),
                  pl.BlockSpec((nk, 1, ATT_TK), lambda b, i, hi: (b, 0, 0))],
        out_specs=pl.BlockSpec((ATT_TQ, MLA_HEADS * V_HEAD), lambda b, i, hi: (b * nq + i, 0)),
        scratch_shapes=[pltpu.VMEM((MLA_HEADS, ATT_TQ, LANES), F32),
                        pltpu.VMEM((MLA_HEADS, ATT_TQ, LANES), F32),
                        pltpu.VMEM((MLA_HEADS, ATT_TQ, V_HEAD), F32)],
    )
    return pl.pallas_call(
        functools.partial(_attn_kernel, nq=nq),
        grid_spec=grid_spec,
        out_shape=jax.ShapeDtypeStruct((bn * l, MLA_HEADS * V_HEAD), BF16),
        compiler_params=_cparams(("arbitrary", "arbitrary")),
        name="attention",
    )(hi, q, k, v, qc, kc)


def _s5_kernel(u_ref, wb_ref, a_ref, wc_ref, d_ref, o_ref, x_scr, st_scr):
    rows = S5_TC * SUBLANES
    nsub = rows // S5_RB

    @pl.when(pl.program_id(1) == 0)
    def _():
        st_scr[...] = jnp.zeros_like(st_scr)

    even = (lax.broadcasted_iota(jnp.int32, (S5_RB, 1), 0) & 1) == 0

    def mm_in(r, _):
        off = pl.multiple_of(r * S5_RB, S5_RB)
        out = jnp.dot(u_ref[0, pl.ds(off, S5_RB), :].astype(BF16), wb_ref[0], preferred_element_type=F32)
        x_scr[pl.ds(off, S5_RB), :] = jnp.where(even, out[:, :2 * S5_LANES], out[:, 2 * S5_LANES:])
        return 0

    lax.fori_loop(0, nsub, mm_in, 0)

    ar = a_ref[0, :, :S5_LANES]
    ai = a_ref[0, :, S5_LANES:]

    def step(t, carry):
        xr, xi = carry
        off = pl.multiple_of(t * SUBLANES, SUBLANES)
        nr = ar * xr - ai * xi + x_scr[pl.ds(off, SUBLANES), :S5_LANES]
        ni = ar * xi + ai * xr + x_scr[pl.ds(off, SUBLANES), S5_LANES:]
        x_scr[pl.ds(off, SUBLANES), :S5_LANES] = nr
        x_scr[pl.ds(off, SUBLANES), S5_LANES:] = ni
        return nr, ni

    xr, xi = lax.fori_loop(0, S5_TC, step, (st_scr[:, :S5_LANES], st_scr[:, S5_LANES:]), unroll=8)
    st_scr[:, :S5_LANES] = xr
    st_scr[:, S5_LANES:] = xi

    d = jnp.concatenate([d_ref[0]] * (S5_RB // SUBLANES), axis=0)

    def mm_out(r, _):
        off = pl.multiple_of(r * S5_RB, S5_RB)
        out = jnp.dot(x_scr[pl.ds(off, S5_RB), :].astype(BF16), wc_ref[0], preferred_element_type=F32)
        y = jnp.where(even, out[:, :LANES], out[:, LANES:]) + d * u_ref[0, pl.ds(off, S5_RB), :]
        o_ref[0, pl.ds(off, S5_RB), :] = jax.nn.gelu(y).astype(o_ref.dtype)
        return 0

    lax.fori_loop(0, nsub, mm_out, 0)


def _s5_params(a_re, a_im, log_dt, b_re, b_im, c_re, c_im, d_skip):
    step = jnp.exp(log_dt)[:, None]
    mag = jnp.exp(a_re * step)
    abar_re, abar_im = mag * jnp.cos(a_im * step), mag * jnp.sin(a_im * step)
    den = a_re * a_re + a_im * a_im
    nr, ni = abar_re - 1.0, abar_im
    f_re, f_im = (nr * a_re + ni * a_im) / den, (ni * a_re - nr * a_im) / den
    bbar_re = f_re[..., None] * b_re - f_im[..., None] * b_im
    bbar_im = f_re[..., None] * b_im + f_im[..., None] * b_re
    eye = jnp.eye(S5_GB, dtype=F32)
    bb = jnp.stack([bbar_re, bbar_im]).reshape(2, S5_NBLK, S5_GB, S5_STATE, S5_GROUP)
    wb = jnp.einsum('ab,rjapc->jacrbp', eye, bb).reshape(S5_NBLK, LANES, 2 * S5_LANES)
    wb = wb.reshape(S5_PAIRS, 2, LANES, 2 * S5_LANES).transpose(0, 2, 1, 3).reshape(S5_PAIRS, LANES, 4 * S5_LANES)
    cc = jnp.stack([c_re, -c_im]).reshape(2, S5_NBLK, S5_GB, S5_GROUP, S5_STATE)
    wc = jnp.einsum('ab,rjacp->jrapbc', eye, cc).reshape(S5_NBLK, 2 * S5_LANES, LANES)
    wc = wc.reshape(S5_PAIRS, 2, 2 * S5_LANES, LANES).transpose(0, 2, 1, 3).reshape(S5_PAIRS, 2 * S5_LANES, 2 * LANES)
    ab = jnp.concatenate([abar_re.reshape(S5_NBLK, S5_LANES), abar_im.reshape(S5_NBLK, S5_LANES)], axis=1)
    ab = jnp.tile(ab.reshape(S5_PAIRS, 2, 2 * S5_LANES), (1, SUBLANES // 2, 1))
    dd = jnp.tile(d_skip.reshape(S5_PAIRS, 2, LANES), (1, SUBLANES // 2, 1))
    return wb.astype(BF16), ab, wc.astype(BF16), dd


def _s5(u6, wb, ab, wc, dd, l):
    rows = S5_TC * SUBLANES
    return pl.pallas_call(
        _s5_kernel,
        grid=(S5_PAIRS, l // S5_TC),
        in_specs=[pl.BlockSpec((1, rows, LANES), lambda k, c: (k, c, 0)),
                  pl.BlockSpec((1, LANES, 4 * S5_LANES), lambda k, c: (k, 0, 0)),
                  pl.BlockSpec((1, SUBLANES, 2 * S5_LANES), lambda k, c: (k, 0, 0)),
                  pl.BlockSpec((1, 2 * S5_LANES, 2 * LANES), lambda k, c: (k, 0, 0)),
                  pl.BlockSpec((1, SUBLANES, LANES), lambda k, c: (k, 0, 0))],
        out_specs=pl.BlockSpec((1, rows, LANES), lambda k, c: (k, c, 0)),
        out_shape=jax.ShapeDtypeStruct(u6.shape, BF16),
        scratch_shapes=[pltpu.VMEM((rows, 2 * S5_LANES), F32),
                        pltpu.VMEM((SUBLANES, 2 * S5_LANES), F32)],
        compiler_params=_cparams(("arbitrary", "arbitrary")),
        name="s5",
    )(u6, wb, ab, wc, dd)


def _glu_kernel(y_ref, w_ref, o_ref):
    y = jnp.concatenate([y_ref[k] for k in range(S5_PAIRS)], axis=1)
    g = jnp.dot(y, w_ref[...], preferred_element_type=F32)
    o_ref[...] = (y.astype(F32) * jax.nn.sigmoid(g)).astype(o_ref.dtype)


def _glu(y4, w_glu):
    m = y4.shape[1]
    tm = 1024
    return pl.pallas_call(
        _glu_kernel,
        grid=(m // tm,),
        in_specs=[pl.BlockSpec((S5_PAIRS, tm, 2 * LANES), lambda i: (0, i, 0)),
                  pl.BlockSpec((S5_CH, S5_CH), lambda i: (0, 0))],
        out_specs=pl.BlockSpec((tm, S5_CH), lambda i: (i, 0)),
        out_shape=jax.ShapeDtypeStruct((m, S5_CH), BF16),
        compiler_params=_cparams(("arbitrary",)),
        name="glu",
    )(y4, w_glu)


def _merge_kernel(att_ref, z_ref, gm_ref, gs_ref, x_ref, mod_ref, gpost_ref, gpre_ref,
                  wbm_ref, wbs_ref, wo_ref, wrh_ref, wrl_ref, x1_ref, h2b_ref, h2s_ref, lg_ref):
    ym = jnp.dot(att_ref[...], wbm_ref[...], preferred_element_type=F32)
    ys = jnp.dot(z_ref[...], wbs_ref[...], preferred_element_type=F32)
    mixed_in = (gm_ref[...].astype(F32) * ym + gs_ref[...].astype(F32) * ys).astype(BF16)
    mixed = jnp.dot(mixed_in, wo_ref[...], preferred_element_type=F32)
    x1 = x_ref[0] + mod_ref[0, 2:3, :] * _rms(mixed, gpost_ref[...])
    x1_ref[0] = x1
    h2 = _rms(x1, gpre_ref[...]) * (1.0 + mod_ref[0, 4:5, :]) + mod_ref[0, 3:4, :]
    h2_hi = h2.astype(BF16)
    h2b_ref[0] = h2_hi
    tm = h2.shape[0]
    for s in range(SLAB):
        h2s_ref[pl.ds(s, tm, stride=SLAB), :] = h2[:, s * LANES:(s + 1) * LANES]
    h2_lo = (h2 - h2_hi.astype(F32)).astype(BF16)
    nt = (((1,), (1,)), ((), ()))
    lg_ref[...] = (lax.dot_general(wrh_ref[...], h2_hi, nt, preferred_element_type=F32)
                   + lax.dot_general(wrl_ref[...], h2_hi, nt, preferred_element_type=F32)
                   + lax.dot_general(wrh_ref[...], h2_lo, nt, preferred_element_type=F32))


def _merge(att, z_tm, gates, x, mod, g_post, g_pre_ffn, w_br_mla, w_br_s5, w_out, wr_hi, wr_lo):
    bn, l, d = x.shape
    tm = 256
    nl = l // tm
    row = lambda b, i: (b * nl + i, 0)
    const = lambda b, i: (0, 0)
    one = pl.Buffered(1)
    return pl.pallas_call(
        _merge_kernel,
        grid=(bn, nl),
        in_specs=[pl.BlockSpec((tm, MLA_HEADS * V_HEAD), row),
                  pl.BlockSpec((tm, S5_CH), lambda b, i: (i, b)),
                  pl.BlockSpec((tm, d), lambda b, i: (b * nl + i, 0)),
                  pl.BlockSpec((tm, d), lambda b, i: (b * nl + i, 1)),
                  pl.BlockSpec((1, tm, d), lambda b, i: (b, i, 0)),
                  pl.BlockSpec((1, 6, d), lambda b, i: (b, 0, 0)),
                  pl.BlockSpec((1, d), const),
                  pl.BlockSpec((1, d), const),
                  pl.BlockSpec((MLA_HEADS * V_HEAD, d), const, pipeline_mode=one),
                  pl.BlockSpec((S5_CH, d), const, pipeline_mode=one),
                  pl.BlockSpec((d, d), const, pipeline_mode=one),
                  pl.BlockSpec((N_EXPERTS, d), const, pipeline_mode=one),
                  pl.BlockSpec((N_EXPERTS, d), const, pipeline_mode=one)],
        out_specs=[pl.BlockSpec((1, tm, d), lambda b, i: (b, i, 0)),
                   pl.BlockSpec((1, tm, d), lambda b, i: (b, i, 0)),
                   pl.BlockSpec((tm * SLAB, LANES), row),
                   pl.BlockSpec((N_EXPERTS, tm), lambda b, i: (0, b * nl + i))],
        out_shape=[jax.ShapeDtypeStruct((bn, l, d), F32),
                   jax.ShapeDtypeStruct((bn, l, d), BF16),
                   jax.ShapeDtypeStruct((bn * l * SLAB, LANES), F32),
                   jax.ShapeDtypeStruct((N_EXPERTS, bn * l), F32)],
        compiler_params=_cparams(("arbitrary", "arbitrary")),
        name="merge",
    )(att, z_tm, gates, gates, x, mod, g_post.reshape(1, d), g_pre_ffn.reshape(1, d),
      w_br_mla, w_br_s5, w_out, wr_hi, wr_lo)


def _route_kernel(lg_ref, b_ref, ti_ref, tw_ref):
    ng = N_EXPERT_GROUPS
    gsz = N_EXPERTS // ng
    lg = lg_ref[...]
    t = lg.shape[-1]
    sc = jax.nn.sigmoid(lg)
    sel = sc + b_ref[...]
    ninf = -jnp.inf
    i_in = lax.broadcasted_iota(jnp.int32, (ng, gsz, t), 1).astype(F32)
    m1 = jnp.max(sel, axis=1, keepdims=True)
    idx1 = jnp.min(jnp.where(sel == m1, i_in, float(gsz)), axis=1, keepdims=True)
    m2 = jnp.max(jnp.where(i_in == idx1, ninf, sel), axis=1, keepdims=True)
    gs = m1 + m2
    g_i = lax.broadcasted_iota(jnp.int32, (ng, 1, t), 0).astype(F32)
    picked = jnp.zeros((ng, 1, t), F32)
    cur = gs
    for _ in range(TOPK_GROUPS):
        m = jnp.max(cur, axis=0, keepdims=True)
        idx = jnp.min(jnp.where(cur == m, g_i, float(ng)), axis=0, keepdims=True)
        hit = g_i == idx
        picked = jnp.where(hit, 1.0, picked)
        cur = jnp.where(hit, ninf, cur)
    cand = jnp.where(picked > 0.5, sel, ninf)
    e_i = lax.broadcasted_iota(jnp.int32, (ng, gsz, t), 0).astype(F32) * float(gsz) + i_in
    ws = []
    for r in range(TOP_K):
        m = jnp.max(jnp.max(cand, axis=1, keepdims=True), axis=0, keepdims=True)
        idx = jnp.min(jnp.min(jnp.where(cand == m, e_i, float(N_EXPERTS)), axis=1, keepdims=True),
                      axis=0, keepdims=True)
        hit = e_i == idx
        w = jnp.sum(jnp.sum(jnp.where(hit, sc, 0.0), axis=1, keepdims=True), axis=0, keepdims=True)
        ti_ref[r:r + 1, :] = idx[0].astype(jnp.int32)
        ws.append(w[0])
        cand = jnp.where(hit, ninf, cand)
    tot = ws[0]
    for w in ws[1:]:
        tot = tot + w
    for r in range(TOP_K):
        tw_ref[r:r + 1, :] = ws[r] / tot * ROUTED_SCALE
    for r in range(TOP_K, SUBLANES):
        ti_ref[r:r + 1, :] = jnp.zeros((1, t), jnp.int32)
        tw_ref[r:r + 1, :] = jnp.zeros((1, t), F32)


def _route(logits_t, router_bias):
    n = logits_t.shape[1]
    ng = N_EXPERT_GROUPS
    gsz = N_EXPERTS // ng
    tn = 1024
    return pl.pallas_call(
        _route_kernel,
        grid=(n // tn,),
        in_specs=[pl.BlockSpec((ng, gsz, tn), lambda i: (0, 0, i)),
                  pl.BlockSpec((ng, gsz, 1), lambda i: (0, 0, 0))],
        out_specs=[pl.BlockSpec((SUBLANES, tn), lambda i: (0, i)),
                   pl.BlockSpec((SUBLANES, tn), lambda i: (0, i))],
        out_shape=[jax.ShapeDtypeStruct((SUBLANES, n), jnp.int32),
                   jax.ShapeDtypeStruct((SUBLANES, n), F32)],
        compiler_params=_cparams(("arbitrary",)),
        name="route",
    )(logits_t.reshape(ng, gsz, n), router_bias.reshape(ng, gsz, 1))


def _moe_kernel(blk_e_ref, nused_ref, nvalid_ref, nxt_e_ref, wslot_ref, t_ref, tn_ref, a_ref, h_hbm,
                wg_hbm, wu_hbm, wd_hbm, y_hbm, xbuf, ybuf, wg_buf, wu_buf, wd_buf, wg_s, wu_s, wd_s,
                gsem, wsem, wtsem):
    j = pl.program_id(0)
    n_used = nused_ref[0]
    slot = lax.rem(j, 2)
    nv = nvalid_ref[j]
    nv_prev = nvalid_ref[jnp.maximum(j - 1, 0)]

    def slab(ref, row0):
        return ref.at[pl.ds(pl.multiple_of(row0, SUBLANES), SLAB), :]

    def start_gather(idx_ref, s):
        def body(r, _):
            pltpu.make_async_copy(slab(h_hbm, idx_ref[0, 0, r]), slab(xbuf.at[s], r * MOE_PITCH), gsem.at[s]).start()
            return 0
        lax.fori_loop(0, MOE_TB, body, 0, unroll=8)

    def wait_gather(s):
        rows = MOE_TB * SLAB
        pltpu.make_async_copy(h_hbm.at[pl.ds(0, rows), :], xbuf.at[s, pl.ds(0, rows), :], gsem.at[s]).wait()

    def weight_copies(expert, s):
        return (pltpu.make_async_copy(wg_hbm.at[expert], wg_buf.at[s], wtsem.at[s]),
                pltpu.make_async_copy(wu_hbm.at[expert], wu_buf.at[s], wtsem.at[s]),
                pltpu.make_async_copy(wd_hbm.at[expert], wd_buf.at[s], wtsem.at[s]))

    def wait_writes(count):
        p = MOE_TB
        while p >= 1:
            @pl.when((count & p) != 0)
            def _(p=p):
                rows = p * SLAB
                pltpu.make_async_copy(ybuf.at[pl.ds(0, rows), :], y_hbm.at[pl.ds(0, rows), :], wsem.at[0]).wait()
            p //= 2

    @pl.when(j < n_used)
    def _():
        @pl.when(j == 0)
        def _():
            start_gather(t_ref, 0)

        e = blk_e_ref[j]
        ws = wslot_ref[j]

        @pl.when(j == 0)
        def _():
            for cp in weight_copies(e, ws):
                cp.start()

        first = jnp.logical_or(j == 0, e != blk_e_ref[jnp.maximum(j - 1, 0)])

        @pl.when(first)
        def _():
            for cp in weight_copies(e, ws):
                cp.wait()
            wg_s[...] = wg_buf[ws].astype(BF16)
            wu_s[...] = wu_buf[ws].astype(BF16)
            wd_s[...] = wd_buf[ws].astype(BF16)

            @pl.when(nxt_e_ref[j] >= 0)
            def _():
                for cp in weight_copies(nxt_e_ref[j], 1 - ws):
                    cp.start()

        wait_gather(slot)

        @pl.when(j + 1 < n_used)
        def _():
            start_gather(tn_ref, 1 - slot)

        xs = xbuf.at[slot]
        x = jnp.concatenate([xs[pl.ds(s, MOE_TB, stride=MOE_PITCH), :] for s in range(SLAB)], axis=1).astype(BF16)
        g = jnp.dot(x, wg_s[...], preferred_element_type=F32)
        u = jnp.dot(x, wu_s[...], preferred_element_type=F32)
        hm = (g * jax.nn.sigmoid(g) * u).astype(BF16)
        y = jnp.dot(hm, wd_s[...], preferred_element_type=F32)

        @pl.when(j > 0)
        def _():
            wait_writes(nv_prev)

        for s in range(SLAB):
            ybuf[pl.ds(s, MOE_TB, stride=MOE_PITCH), :] = y[:, s * LANES:(s + 1) * LANES]

        def start_write(r):
            pltpu.make_async_copy(slab(ybuf, r * MOE_PITCH), slab(y_hbm, a_ref[0, 0, r]), wsem.at[0]).start()

        def wgroup(g8, _):
            for q in range(SUBLANES):
                start_write(g8 * SUBLANES + q)
            return 0

        def wtail(r, _):
            start_write(r)
            return 0

        n_groups = lax.shift_right_logical(nv, 3)
        lax.fori_loop(0, n_groups, wgroup, 0)
        lax.fori_loop(n_groups * SUBLANES, nv, wtail, 0)

        @pl.when(j == n_used - 1)
        def _():
            wait_writes(nv)


def _moe(h2s, topi, w_gate, w_up, w_down):
    n = h2s.shape[0] // SLAB
    d = D_MODEL
    nk = n * TOP_K
    tb = MOE_TB
    n_blocks = nk // tb + N_EXPERTS
    flat_e = topi[:TOP_K].reshape(nk)
    se, order = lax.sort_key_val(flat_e, jnp.arange(nk, dtype=jnp.int32))
    bounds = jnp.searchsorted(se, jnp.arange(N_EXPERTS + 1, dtype=jnp.int32), side='left').astype(jnp.int32)
    start = bounds[:-1]
    counts = bounds[1:] - start
    padded = (counts + tb - 1) // tb * tb
    pad_end = jnp.cumsum(padded)
    pad_start = pad_end - padded
    n_used = (pad_end[-1] // tb).astype(jnp.int32).reshape(1)
    blk_p0 = jnp.arange(n_blocks, dtype=jnp.int32) * tb
    blk_e = jnp.minimum(jnp.searchsorted(pad_end, blk_p0, side='right'), N_EXPERTS - 1).astype(jnp.int32)
    blk_off = blk_p0 - pad_start[blk_e]
    nvalid = jnp.clip(counts[blk_e] - blk_off, 0, tb).astype(jnp.int32)
    rank = (start[blk_e] + blk_off)[:, None] + jnp.arange(tb, dtype=jnp.int32)[None, :]
    a3 = order[jnp.clip(rank, 0, nk - 1)].reshape(n_blocks, 1, tb)
    arow = a3 * SLAB
    trow = (a3 % n) * SLAB
    used = counts > 0
    e_ids = jnp.arange(N_EXPERTS, dtype=jnp.int32)
    later = jnp.where(used, e_ids, N_EXPERTS)
    nxt = jnp.concatenate([lax.cummin(later[::-1])[::-1][1:], jnp.full((1,), N_EXPERTS, jnp.int32)])
    nxt_e = jnp.where(nxt < N_EXPERTS, nxt, -1)[blk_e].astype(jnp.int32)
    wslot = ((jnp.cumsum(used.astype(jnp.int32)) - 1) % 2)[blk_e].astype(jnp.int32)

    smem_blk = lambda f: pl.BlockSpec((1, 1, tb), f, memory_space=pltpu.SMEM)
    any_spec = pl.BlockSpec(memory_space=pl.ANY)
    grid_spec = pltpu.PrefetchScalarGridSpec(
        num_scalar_prefetch=5,
        grid=(n_blocks,),
        in_specs=[smem_blk(lambda j, *_: (j, 0, 0)),
                  smem_blk(lambda j, *_: (jnp.minimum(j + 1, n_blocks - 1), 0, 0)),
                  smem_blk(lambda j, *_: (j, 0, 0)),
                  any_spec, any_spec, any_spec, any_spec],
        out_specs=any_spec,
        scratch_shapes=[pltpu.VMEM((2, tb * MOE_PITCH, LANES), F32),
                        pltpu.VMEM((tb * MOE_PITCH, LANES), F32),
                        pltpu.VMEM((2, d, D_EXPERT), F32),
                        pltpu.VMEM((2, d, D_EXPERT), F32),
                        pltpu.VMEM((2, D_EXPERT, d), F32),
                        pltpu.VMEM((d, D_EXPERT), BF16),
                        pltpu.VMEM((d, D_EXPERT), BF16),
                        pltpu.VMEM((D_EXPERT, d), BF16),
                        pltpu.SemaphoreType.DMA((2,)),
                        pltpu.SemaphoreType.DMA((1,)),
                        pltpu.SemaphoreType.DMA((2,))],
    )
    return pl.pallas_call(
        _moe_kernel,
        grid_spec=grid_spec,
        out_shape=jax.ShapeDtypeStruct((TOP_K * n * SLAB, LANES), F32),
        compiler_params=_cparams(("arbitrary",)),
        name="moe",
    )(blk_e, n_used, nvalid, nxt_e, wslot, trow, trow, arow, h2s, w_gate, w_up, w_down)


def _final_kernel(h_ref, y0, y1, y2, y3, y4, y5, tw_ref, x1_ref, mod_ref, g_ref, wg_ref, wu_ref, wd_ref, o_ref):
    h = h_ref[...]
    tm = h.shape[0]
    g = jnp.dot(h, wg_ref[...], preferred_element_type=F32)
    u = jnp.dot(h, wu_ref[...], preferred_element_type=F32)
    hm = (g * jax.nn.sigmoid(g) * u).astype(BF16)
    ffn = jnp.dot(hm, wd_ref[...], preferred_element_type=F32)
    tw = tw_ref[...]
    for k, y in enumerate((y0, y1, y2, y3, y4, y5)):
        yk = jnp.concatenate([y[pl.ds(s, tm, stride=SLAB), :] for s in range(SLAB)], axis=1)
        ffn = ffn + yk * tw[:, k:k + 1]
    o_ref[0] = x1_ref[0] + mod_ref[0, 5:6, :] * _rms(ffn, g_ref[...])


def _final(h2, y_flat, topw_t, x1, mod, g_post_ffn, w_sg, w_su, w_sd):
    bn, l, d = x1.shape
    n = bn * l
    tm = 256
    nl = l // tm
    const = lambda b, i: (0, 0)
    one = pl.Buffered(1)
    y_specs = [pl.BlockSpec((tm * SLAB, LANES), functools.partial(lambda b, i, k: (k * (n // tm) + b * nl + i, 0), k=k))
               for k in range(TOP_K)]
    return pl.pallas_call(
        _final_kernel,
        grid=(bn, nl),
        in_specs=[pl.BlockSpec((tm, d), lambda b, i: (b * nl + i, 0))] + y_specs + [
            pl.BlockSpec((tm, SUBLANES), lambda b, i: (b * nl + i, 0)),
            pl.BlockSpec((1, tm, d), lambda b, i: (b, i, 0)),
            pl.BlockSpec((1, 6, d), lambda b, i: (b, 0, 0)),
            pl.BlockSpec((1, d), const),
            pl.BlockSpec((d, D_EXPERT), const, pipeline_mode=one),
            pl.BlockSpec((d, D_EXPERT), const, pipeline_mode=one),
            pl.BlockSpec((D_EXPERT, d), const, pipeline_mode=one)],
        out_specs=pl.BlockSpec((1, tm, d), lambda b, i: (b, i, 0)),
        out_shape=jax.ShapeDtypeStruct((bn, l, d), F32),
        compiler_params=_cparams(("arbitrary", "arbitrary")),
        name="final",
    )(h2, *([y_flat] * TOP_K), topw_t, x1, mod, g_post_ffn.reshape(1, d), w_sg, w_su, w_sd)


def _layer(x, c, positions, w_ada, b_ada, g_pre_mix, g_post_mix, g_pre_ffn, g_post_ffn, w_in, g_q, g_kv,
           w_uq, w_uk, w_uv, a_re, a_im, log_dt, b_re, b_im, c_re, c_im, d_skip, w_glu, w_br_mla, w_br_s5,
           w_out, w_router, router_bias, w_exp_gate, w_exp_up, w_exp_down, w_sh_gate, w_sh_up, w_sh_down):
    bn, l, d = x.shape
    n = bn * l
    mod = _adaln(c, w_ada, b_ada).reshape(bn, 6, d)

    o_kpe = Q_LORA + KV_LORA
    o_u = o_kpe + QK_ROPE
    o_g = o_u + S5_CH
    w_lat = w_in[:, :o_kpe].astype(BF16)
    w_kpe = jnp.pad(w_in[:, o_kpe:o_u], ((0, 0), (0, LANES - QK_ROPE))).astype(BF16)
    w_u = w_in[:, o_u:o_g].astype(BF16)
    w_gates = w_in[:, o_g:].astype(BF16)
    w_q = jnp.pad(w_uq.reshape(Q_LORA, MLA_HEADS, QK_NOPE + QK_ROPE),
                  ((0, 0), (0, 0), (0, QK_PAD - QK_NOPE - QK_ROPE))).reshape(Q_LORA, MLA_HEADS * QK_PAD).astype(BF16)
    wr_t = w_router.T
    wr_hi = wr_t.astype(BF16)
    wr_lo = (wr_t - wr_hi.astype(F32)).astype(BF16)

    rope_c, rope_s1, rope_s2 = _rope_tables(positions)
    chunk_id = positions // CHUNK

    h = _prenorm(x, mod, g_pre_mix, 0, 1).reshape(n, d)
    lat = _mm(h, w_lat, F32, 1024, 512, name="mm_lat")
    kpe = _mm(h, w_kpe, F32, 1024, LANES, name="mm_kpe")
    gates = _mm(h, w_gates, BF16, 1024, 512, act="sigmoid", name="mm_gates")
    u6 = _mm_u(h, w_u, bn, l).reshape(S5_PAIRS, l * SUBLANES, LANES)

    q, k, v = _qkvproj(lat, g_q, g_kv, w_q, w_uk.astype(BF16), w_uv.astype(BF16), kpe, rope_c, rope_s1, rope_s2)
    att = _attention(q, k, v, chunk_id, bn, l)

    wb, ab, wc, dd = _s5_params(a_re, a_im, log_dt, b_re, b_im, c_re, c_im, d_skip)
    y6 = _s5(u6, wb, ab, wc, dd, l)
    z = _glu(y6.reshape(S5_PAIRS, l * bn, 2 * LANES), w_glu.astype(BF16))
    z_tm = z.reshape(l, bn * S5_CH)

    x1, h2b, h2s, logits_t = _merge(att, z_tm, gates, x, mod, g_post_mix, g_pre_ffn, w_br_mla.astype(BF16),
                                    w_br_s5.astype(BF16), w_out.astype(BF16), wr_hi, wr_lo)

    topi, topw = _route(logits_t, router_bias)
    y_slabs = _moe(h2s, topi, w_exp_gate, w_exp_up, w_exp_down)
    return _final(h2b.reshape(n, d), y_slabs, topw.T, x1, mod, g_post_ffn, w_sh_gate.astype(BF16),
                  w_sh_up.astype(BF16), w_sh_down.astype(BF16))


def kernel(x, c, positions, w_ada, b_ada, g_pre_mix, g_post_mix, g_pre_ffn, g_post_ffn, w_in, g_q, g_kv, w_uq, w_uk, w_uv, a_re, a_im, log_dt, b_re, b_im, c_re, c_im, d_skip, w_glu, w_br_mla, w_br_s5, w_out, w_router, router_bias, w_exp_gate, w_exp_up, w_exp_down, w_sh_gate, w_sh_up, w_sh_down):
    depth = w_ada.shape[0]
    for li in range(depth):
        x = _layer(x, c, positions, w_ada[li], b_ada[li], g_pre_mix[li], g_post_mix[li], g_pre_ffn[li],
                   g_post_ffn[li], w_in[li], g_q[li], g_kv[li], w_uq[li], w_uk[li], w_uv[li], a_re[li], a_im[li],
                   log_dt[li], b_re[li], b_im[li], c_re[li], c_im[li], d_skip[li], w_glu[li], w_br_mla[li],
                   w_br_s5[li], w_out[li], w_router[li], router_bias[li], w_exp_gate[li], w_exp_up[li],
                   w_exp_down[li], w_sh_gate[li], w_sh_up[li], w_sh_down[li])
    return x
```

```python
import functools

import jax
import jax.numpy as jnp
from jax import lax
from jax.experimental import pallas as pl
from jax.experimental.pallas import tpu as pltpu

F32 = jnp.float32
BF16 = jnp.bfloat16

D_MODEL = 2048
CHUNK = 64
EPS = 1e-6
MLA_HEADS = 8
QK_NOPE = 128
QK_ROPE = 64
V_HEAD = 128
Q_LORA = 512
KV_LORA = 512
ROPE_THETA = 10000.0
S5_CH = 1024
S5_GROUP = 16
S5_GROUPS = S5_CH // S5_GROUP
S5_STATE = 64
N_EXPERTS = 64
TOP_K = 6
N_EXPERT_GROUPS = 8
TOPK_GROUPS = 4
D_EXPERT = 512
ROUTED_SCALE = 2.5

LANES = 128
SUBLANES = 8
QK_PAD = 2 * LANES
VMEM_LIMIT = 56 * 1024 * 1024
NEG = -1e30

S5_GB = LANES // S5_GROUP
S5_NBLK = S5_CH // LANES
S5_PAIRS = S5_NBLK // 2
S5_LANES = S5_GB * S5_STATE
S5_TC = 256
S5_RB = 512

ATT_TQ = 256
ATT_TK = 256
MOE_TB = 256
SLAB = D_MODEL // LANES
MOE_PITCH = 24


def _cparams(sem):
    return pltpu.CompilerParams(dimension_semantics=sem, vmem_limit_bytes=VMEM_LIMIT)


def _rms(x, g):
    return x * lax.rsqrt(jnp.mean(x * x, axis=-1, keepdims=True) + EPS) * g


def _adaln_kernel(c_ref, w_ref, b_ref, o_ref):
    c = c_ref[...]
    a = (c * jax.nn.sigmoid(c)).astype(BF16)
    o_ref[...] = jnp.dot(a, w_ref[...].astype(BF16), preferred_element_type=F32) + b_ref[...]


def _adaln(c, w, b):
    bn, d = c.shape
    n = w.shape[1]
    tn = 1024
    return pl.pallas_call(
        _adaln_kernel,
        grid=(n // tn,),
        in_specs=[pl.BlockSpec((bn, d), lambda j: (0, 0)),
                  pl.BlockSpec((d, tn), lambda j: (0, j)),
                  pl.BlockSpec((1, tn), lambda j: (0, j))],
        out_specs=pl.BlockSpec((bn, tn), lambda j: (0, j)),
        out_shape=jax.ShapeDtypeStruct((bn, n), F32),
        compiler_params=_cparams(("arbitrary",)),
        name="adaln",
    )(c, w, b.reshape(1, n))


def _prenorm_kernel(x_ref, mod_ref, g_ref, o_ref, *, sh_row, sc_row):
    y = _rms(x_ref[0], g_ref[...])
    o_ref[0] = (y * (1.0 + mod_ref[0, sc_row:sc_row + 1, :]) + mod_ref[0, sh_row:sh_row + 1, :]).astype(o_ref.dtype)


def _prenorm(x, mod, g, sh_row, sc_row):
    bn, l, d = x.shape
    tl = 512
    return pl.pallas_call(
        functools.partial(_prenorm_kernel, sh_row=sh_row, sc_row=sc_row),
        grid=(bn, l // tl),
        in_specs=[pl.BlockSpec((1, tl, d), lambda b, i: (b, i, 0)),
                  pl.BlockSpec((1, 6, d), lambda b, i: (b, 0, 0)),
                  pl.BlockSpec((1, d), lambda b, i: (0, 0))],
        out_specs=pl.BlockSpec((1, tl, d), lambda b, i: (b, i, 0)),
        out_shape=jax.ShapeDtypeStruct((bn, l, d), BF16),
        compiler_params=_cparams(("arbitrary", "arbitrary")),
        name="prenorm",
    )(x, mod, g.reshape(1, d))


def _mm_kernel(a_ref, w_ref, o_ref, *, act):
    acc = jnp.dot(a_ref[...], w_ref[...], preferred_element_type=F32)
    if act == "sigmoid":
        acc = jax.nn.sigmoid(acc)
    o_ref[...] = acc.astype(o_ref.dtype)


def _mm(a, w, col0, n, out_dtype, tm, tn, act=None, name="mm"):
    m, k = a.shape
    j0 = col0 // tn
    return pl.pallas_call(
        functools.partial(_mm_kernel, act=act),
        grid=(n // tn, m // tm),
        in_specs=[pl.BlockSpec((tm, k), lambda j, i: (i, 0)),
                  pl.BlockSpec((k, tn), lambda j, i: (0, j + j0))],
        out_specs=pl.BlockSpec((tm, tn), lambda j, i: (i, j)),
        out_shape=jax.ShapeDtypeStruct((m, n), out_dtype),
        compiler_params=_cparams(("arbitrary", "arbitrary")),
        name=name,
    )(a, w)


def _mm_u_kernel(a_ref, w_ref, o_ref):
    res = jnp.dot(a_ref[...], w_ref[...], preferred_element_type=F32)
    tm = res.shape[0]
    b = pl.program_id(2)
    for s in range(2):
        o_ref.at[0][pl.ds(b * 2 + s, tm, stride=SUBLANES), :] = res[:, s * LANES:(s + 1) * LANES]


def _mm_u(h, w, col0, bn, l):
    m, k = h.shape
    tm = 1024
    nl = l // tm
    tn = 2 * LANES
    j0 = col0 // tn
    assert bn * 2 == SUBLANES
    return pl.pallas_call(
        _mm_u_kernel,
        grid=(S5_PAIRS, nl, bn),
        in_specs=[pl.BlockSpec((tm, k), lambda j, i, b: (b * nl + i, 0)),
                  pl.BlockSpec((k, tn), lambda j, i, b: (0, j + j0))],
        out_specs=pl.BlockSpec((1, tm * SUBLANES, LANES), lambda j, i, b: (j, i, 0)),
        out_shape=jax.ShapeDtypeStruct((S5_PAIRS, l * SUBLANES, LANES), F32),
        compiler_params=_cparams(("arbitrary", "arbitrary", "arbitrary")),
        name="mm_u",
    )(h, w)


def _rope_tab_kernel(pos_ref, k_ref, c_ref, s1_ref, s2_ref):
    ang = pos_ref[...].astype(F32) * k_ref[0:1, :]
    s = jnp.sin(ang)
    c_ref[...] = jnp.cos(ang) * k_ref[1:2, :]
    s1_ref[...] = s * k_ref[2:3, :]
    s2_ref[...] = s * k_ref[3:4, :]


def _rope_tables(positions):
    n = positions.size
    half = QK_ROPE // 2
    inv_freq = ROPE_THETA ** (-jnp.arange(half, dtype=F32) / half)
    zh, oh = jnp.zeros((half,), F32), jnp.ones((half,), F32)
    z2 = jnp.zeros((LANES - QK_ROPE,), F32)
    rows = [jnp.concatenate([inv_freq, inv_freq, z2]), jnp.concatenate([oh, oh, z2]),
            jnp.concatenate([-oh, zh, z2]), jnp.concatenate([zh, oh, z2])]
    consts = jnp.stack(rows + [jnp.zeros((LANES,), F32)] * (SUBLANES - len(rows)))
    tm = 1024
    tab = jax.ShapeDtypeStruct((n, LANES), F32)
    return pl.pallas_call(
        _rope_tab_kernel,
        grid=(n // tm,),
        in_specs=[pl.BlockSpec((tm, 1), lambda i: (i, 0)),
                  pl.BlockSpec((SUBLANES, LANES), lambda i: (0, 0))],
        out_specs=[pl.BlockSpec((tm, LANES), lambda i: (i, 0))] * 3,
        out_shape=[tab, tab, tab],
        compiler_params=_cparams(("arbitrary",)),
        name="rope_tables",
    )(positions.reshape(n, 1), consts)


def _rope_tile(t, c_ref, s1_ref, s2_ref):
    return (t * c_ref[...] + pltpu.roll(t, LANES - QK_ROPE // 2, 1) * s1_ref[...]
            + pltpu.roll(t, QK_ROPE // 2, 1) * s2_ref[...])


def _qkvproj_kernel(lat_ref, gq_ref, gkv_ref, wq_ref, wk_ref, wv_ref, kpe_ref, c_ref, s1_ref, s2_ref,
                    q_ref, k_ref, v_ref, *, scale):
    lat = lat_ref[...]
    qn = _rms(lat[:, :Q_LORA], gq_ref[...]).astype(BF16)
    cn = _rms(lat[:, Q_LORA:], gkv_ref[...]).astype(BF16)
    q = jnp.dot(qn, wq_ref[...], preferred_element_type=F32)
    kn = jnp.dot(cn, wk_ref[...], preferred_element_type=F32)
    v_ref[...] = jnp.dot(cn, wv_ref[...], preferred_element_type=F32).astype(v_ref.dtype)
    kt = _rope_tile(kpe_ref[...], c_ref, s1_ref, s2_ref).astype(k_ref.dtype)
    for h in range(MLA_HEADS):
        o = h * QK_PAD
        q_ref[:, o:o + LANES] = (q[:, o:o + LANES] * scale).astype(q_ref.dtype)
        qt = _rope_tile(q[:, o + LANES:o + QK_PAD], c_ref, s1_ref, s2_ref)
        q_ref[:, o + LANES:o + QK_PAD] = (qt * scale).astype(q_ref.dtype)
        k_ref[:, o:o + LANES] = kn[:, h * QK_NOPE:(h + 1) * QK_NOPE].astype(k_ref.dtype)
        k_ref[:, o + LANES:o + QK_PAD] = kt


def _qkvproj(lat, g_q, g_kv, w_q, w_k, w_v, kpe, rope_c, rope_s1, rope_s2):
    n = lat.shape[0]
    tm = 512
    row = lambda i: (i, 0)
    const = lambda i: (0, 0)
    tab = pl.BlockSpec((tm, LANES), row)
    return pl.pallas_call(
        functools.partial(_qkvproj_kernel, scale=(QK_NOPE + QK_ROPE) ** -0.5),
        grid=(n // tm,),
        in_specs=[pl.BlockSpec((tm, Q_LORA + KV_LORA), row),
                  pl.BlockSpec((1, Q_LORA), const),
                  pl.BlockSpec((1, KV_LORA), const),
                  pl.BlockSpec((Q_LORA, MLA_HEADS * QK_PAD), const),
                  pl.BlockSpec((KV_LORA, MLA_HEADS * QK_NOPE), const),
                  pl.BlockSpec((KV_LORA, MLA_HEADS * V_HEAD), const),
                  tab, tab, tab, tab],
        out_specs=[pl.BlockSpec((tm, MLA_HEADS * QK_PAD), row),
                   pl.BlockSpec((tm, MLA_HEADS * QK_PAD), row),
                   pl.BlockSpec((tm, MLA_HEADS * V_HEAD), row)],
        out_shape=[jax.ShapeDtypeStruct((n, MLA_HEADS * QK_PAD), BF16),
                   jax.ShapeDtypeStruct((n, MLA_HEADS * QK_PAD), BF16),
                   jax.ShapeDtypeStruct((n, MLA_HEADS * V_HEAD), BF16)],
        compiler_params=_cparams(("arbitrary",)),
        name="qkvproj",
    )(lat, g_q.reshape(1, Q_LORA), g_kv.reshape(1, KV_LORA), w_q, w_k, w_v, kpe, rope_c, rope_s1, rope_s2)


def _attn_kernel(hi_ref, q_ref, k_ref, v_ref, qc_ref, kc_ref, o_ref, m_scr, l_scr, acc_scr, *, nq):
    b = pl.program_id(0)
    i = pl.program_id(1)
    qc = qc_ref[...]
    m_scr[...] = jnp.full(m_scr.shape, NEG, F32)
    l_scr[...] = jnp.zeros(l_scr.shape, F32)
    acc_scr[...] = jnp.zeros(acc_scr.shape, F32)

    def body(j, _):
        off = pl.multiple_of(j * ATT_TK, ATT_TK)
        mask = kc_ref[j] <= qc
        for h in range(MLA_HEADS):
            q = q_ref[:, h * QK_PAD:(h + 1) * QK_PAD]
            k = k_ref[pl.ds(off, ATT_TK), h * QK_PAD:(h + 1) * QK_PAD]
            s = lax.dot_general(q, k, (((1,), (1,)), ((), ())), preferred_element_type=F32)
            s = jnp.where(mask, s, NEG)
            m_old = m_scr[h]
            m_new = jnp.maximum(m_old, jnp.max(s, axis=-1, keepdims=True))
            p = jnp.exp(s - jnp.concatenate([m_new] * (ATT_TK // LANES), axis=1))
            alpha = jnp.exp(m_old - m_new)
            l_scr[h] = alpha * l_scr[h] + jnp.sum(p, axis=-1, keepdims=True)
            v = v_ref[pl.ds(off, ATT_TK), h * V_HEAD:(h + 1) * V_HEAD]
            acc_scr[h] = alpha * acc_scr[h] + jnp.dot(p.astype(BF16), v, preferred_element_type=F32)
            m_scr[h] = m_new
        return 0

    lax.fori_loop(0, hi_ref[b * nq + i], body, 0)
    for h in range(MLA_HEADS):
        o_ref[:, h * V_HEAD:(h + 1) * V_HEAD] = (acc_scr[h] / l_scr[h]).astype(o_ref.dtype)


def _attention(q, k, v, chunk_id, bn, l):
    nq = l // ATT_TQ
    nk = l // ATT_TK
    q_max = jnp.max(chunk_id.reshape(bn, nq, ATT_TQ), axis=-1)
    k_min = jnp.min(chunk_id.reshape(bn, nk, ATT_TK), axis=-1)
    needed = k_min[:, None, :] <= q_max[:, :, None]
    hi = jnp.max(jnp.where(needed, jnp.arange(1, nk + 1, dtype=jnp.int32), 0), axis=-1).reshape(bn * nq)
    qc = chunk_id.reshape(bn * l, 1)
    kc = chunk_id.reshape(bn * nk, 1, ATT_TK)
    grid_spec = pltpu.PrefetchScalarGridSpec(
        num_scalar_prefetch=1,
        grid=(bn, nq),
        in_specs=[pl.BlockSpec((ATT_TQ, MLA_HEADS * QK_PAD), lambda b, i, hi: (b * nq + i, 0)),
                  pl.BlockSpec((l, MLA_HEADS * QK_PAD), lambda b, i, hi: (b, 0)),
                  pl.BlockSpec((l, MLA_HEADS * V_HEAD), lambda b, i, hi: (b, 0)),
                  pl.BlockSpec((ATT_TQ, 1), lambda b, i, hi: (b * nq + i, 0)),
                  pl.BlockSpec((nk, 1, ATT_TK), lambda b, i, hi: (b, 0, 0))],
        out_specs=pl.BlockSpec((ATT_TQ, MLA_HEADS * V_HEAD), lambda b, i, hi: (b * nq + i, 0)),
        scratch_shapes=[pltpu.VMEM((MLA_HEADS, ATT_TQ, LANES), F32),
                        pltpu.VMEM((MLA_HEADS, ATT_TQ, LANES), F32),
                        pltpu.VMEM((MLA_HEADS, ATT_TQ, V_HEAD), F32)],
    )
    return pl.pallas_call(
        functools.partial(_attn_kernel, nq=nq),
        grid_spec=grid_spec,
        out_shape=jax.ShapeDtypeStruct((bn * l, MLA_HEADS * V_HEAD), BF16),
        compiler_params=_cparams(("arbitrary", "arbitrary")),
        name="attention",
    )(hi, q, k, v, qc, kc)


def _s5_kernel(u_ref, wb_ref, a_ref, wc_ref, d_ref, o_ref, x_scr, st_scr):
    rows = S5_TC * SUBLANES
    nsub = rows // S5_RB

    @pl.when(pl.program_id(1) == 0)
    def _():
        st_scr[...] = jnp.zeros_like(st_scr)

    even = (lax.broadcasted_iota(jnp.int32, (S5_RB, 1), 0) & 1) == 0

    def mm_in(r, _):
        off = pl.multiple_of(r * S5_RB, S5_RB)
        out = jnp.dot(u_ref[0, pl.ds(off, S5_RB), :].astype(BF16), wb_ref[0], preferred_element_type=F32)
        x_scr[pl.ds(off, S5_RB), :] = jnp.where(even, out[:, :2 * S5_LANES], out[:, 2 * S5_LANES:])
        return 0

    lax.fori_loop(0, nsub, mm_in, 0)

    ar = a_ref[0, :, :S5_LANES]
    ai = a_ref[0, :, S5_LANES:]

    def step(t, carry):
        xr, xi = carry
        off = pl.multiple_of(t * SUBLANES, SUBLANES)
        nr = ar * xr - ai * xi + x_scr[pl.ds(off, SUBLANES), :S5_LANES]
        ni = ar * xi + ai * xr + x_scr[pl.ds(off, SUBLANES), S5_LANES:]
        x_scr[pl.ds(off, SUBLANES), :S5_LANES] = nr
        x_scr[pl.ds(off, SUBLANES), S5_LANES:] = ni
        return nr, ni

    xr, xi = lax.fori_loop(0, S5_TC, step, (st_scr[:, :S5_LANES], st_scr[:, S5_LANES:]), unroll=8)
    st_scr[:, :S5_LANES] = xr
    st_scr[:, S5_LANES:] = xi

    d = jnp.concatenate([d_ref[0]] * (S5_RB // SUBLANES), axis=0)

    def mm_out(r, _):
        off = pl.multiple_of(r * S5_RB, S5_RB)
        out = jnp.dot(x_scr[pl.ds(off, S5_RB), :].astype(BF16), wc_ref[0], preferred_element_type=F32)
        y = jnp.where(even, out[:, :LANES], out[:, LANES:]) + d * u_ref[0, pl.ds(off, S5_RB), :]
        o_ref[0, pl.ds(off, S5_RB), :] = jax.nn.gelu(y).astype(o_ref.dtype)
        return 0

    lax.fori_loop(0, nsub, mm_out, 0)


def _s5_params(a_re, a_im, log_dt, b_re, b_im, c_re, c_im, d_skip):
    step = jnp.exp(log_dt)[:, None]
    mag = jnp.exp(a_re * step)
    abar_re, abar_im = mag * jnp.cos(a_im * step), mag * jnp.sin(a_im * step)
    den = a_re * a_re + a_im * a_im
    nr, ni = abar_re - 1.0, abar_im
    f_re, f_im = (nr * a_re + ni * a_im) / den, (ni * a_re - nr * a_im) / den
    bbar_re = f_re[..., None] * b_re - f_im[..., None] * b_im
    bbar_im = f_re[..., None] * b_im + f_im[..., None] * b_re
    eye = jnp.eye(S5_GB, dtype=F32)
    bb = jnp.stack([bbar_re, bbar_im]).reshape(2, S5_NBLK, S5_GB, S5_STATE, S5_GROUP)
    wb = jnp.einsum('ab,rjapc->jacrbp', eye, bb).reshape(S5_NBLK, LANES, 2 * S5_LANES)
    wb = wb.reshape(S5_PAIRS, 2, LANES, 2 * S5_LANES).transpose(0, 2, 1, 3).reshape(S5_PAIRS, LANES, 4 * S5_LANES)
    cc = jnp.stack([c_re, -c_im]).reshape(2, S5_NBLK, S5_GB, S5_GROUP, S5_STATE)
    wc = jnp.einsum('ab,rjacp->jrapbc', eye, cc).reshape(S5_NBLK, 2 * S5_LANES, LANES)
    wc = wc.reshape(S5_PAIRS, 2, 2 * S5_LANES, LANES).transpose(0, 2, 1, 3).reshape(S5_PAIRS, 2 * S5_LANES, 2 * LANES)
    ab = jnp.concatenate([abar_re.reshape(S5_NBLK, S5_LANES), abar_im.reshape(S5_NBLK, S5_LANES)], axis=1)
    ab = jnp.tile(ab.reshape(S5_PAIRS, 2, 2 * S5_LANES), (1, SUBLANES // 2, 1))
    dd = jnp.tile(d_skip.reshape(S5_PAIRS, 2, LANES), (1, SUBLANES // 2, 1))
    return wb.astype(BF16), ab, wc.astype(BF16), dd


def _s5(u6, wb, ab, wc, dd, l):
    rows = S5_TC * SUBLANES
    return pl.pallas_call(
        _s5_kernel,
        grid=(S5_PAIRS, l // S5_TC),
        in_specs=[pl.BlockSpec((1, rows, LANES), lambda k, c: (k, c, 0)),
                  pl.BlockSpec((1, LANES, 4 * S5_LANES), lambda k, c: (k, 0, 0)),
                  pl.BlockSpec((1, SUBLANES, 2 * S5_LANES), lambda k, c: (k, 0, 0)),
                  pl.BlockSpec((1, 2 * S5_LANES, 2 * LANES), lambda k, c: (k, 0, 0)),
                  pl.BlockSpec((1, SUBLANES, LANES), lambda k, c: (k, 0, 0))],
        out_specs=pl.BlockSpec((1, rows, LANES), lambda k, c: (k, c, 0)),
        out_shape=jax.ShapeDtypeStruct(u6.shape, F32),
        scratch_shapes=[pltpu.VMEM((rows, 2 * S5_LANES), F32),
                        pltpu.VMEM((SUBLANES, 2 * S5_LANES), F32)],
        compiler_params=_cparams(("arbitrary", "arbitrary")),
        name="s5",
    )(u6, wb, ab, wc, dd)


def _glu_kernel(y_ref, w_ref, o_ref, z_scr, *, bn):
    b = pl.program_id(1)
    rows = z_scr.shape[1]

    @pl.when(b == 0)
    def _():
        y = jnp.concatenate([y_ref.at[k][pl.ds(s, rows, stride=2), :] for k in range(S5_PAIRS) for s in range(2)],
                            axis=1)
        g = jnp.dot(y.astype(BF16), w_ref[...], preferred_element_type=F32)
        z = y * jax.nn.sigmoid(g)
        for c in range(S5_NBLK):
            z_scr[c] = z[:, c * LANES:(c + 1) * LANES]

    o_ref[0] = jnp.concatenate([z_scr.at[c][pl.ds(b, rows // bn, stride=bn), :] for c in range(S5_NBLK)],
                               axis=1).astype(o_ref.dtype)


def _glu(y6, w_glu, bn, l):
    tt = 256
    rows = tt * bn
    return pl.pallas_call(
        functools.partial(_glu_kernel, bn=bn),
        grid=(l // tt, bn),
        in_specs=[pl.BlockSpec((S5_PAIRS, rows * 2, LANES), lambda i, b: (0, i, 0)),
                  pl.BlockSpec((S5_CH, S5_CH), lambda i, b: (0, 0))],
        out_specs=pl.BlockSpec((1, tt, S5_CH), lambda i, b: (b, i, 0)),
        out_shape=jax.ShapeDtypeStruct((bn, l, S5_CH), BF16),
        scratch_shapes=[pltpu.VMEM((S5_NBLK, rows, LANES), F32)],
        compiler_params=_cparams(("arbitrary", "arbitrary")),
        name="glu",
    )(y6, w_glu)


def _merge_kernel(att_ref, z_ref, gm_ref, gs_ref, x_ref, mod_ref, gpost_ref, gpre_ref,
                  wbm_ref, wbs_ref, wo_ref, wrh_ref, wrl_ref, x1_ref, h2b_ref, h2s_ref, lg_ref):
    ym = jnp.dot(att_ref[...], wbm_ref[...], preferred_element_type=F32)
    ys = jnp.dot(z_ref[...], wbs_ref[...], preferred_element_type=F32)
    mixed_in = (gm_ref[...].astype(F32) * ym + gs_ref[...].astype(F32) * ys).astype(BF16)
    mixed = jnp.dot(mixed_in, wo_ref[...], preferred_element_type=F32)
    x1 = x_ref[0] + mod_ref[0, 2:3, :] * _rms(mixed, gpost_ref[...])
    x1_ref[0] = x1
    h2 = _rms(x1, gpre_ref[...]) * (1.0 + mod_ref[0, 4:5, :]) + mod_ref[0, 3:4, :]
    h2_hi = h2.astype(BF16)
    h2b_ref[0] = h2_hi
    tm = h2.shape[0]
    for s in range(SLAB):
        h2s_ref[pl.ds(s, tm, stride=SLAB), :] = h2[:, s * LANES:(s + 1) * LANES]
    h2_lo = (h2 - h2_hi.astype(F32)).astype(BF16)
    nt = (((1,), (1,)), ((), ()))
    lg_ref[...] = (lax.dot_general(wrh_ref[...], h2_hi, nt, preferred_element_type=F32)
                   + lax.dot_general(wrl_ref[...], h2_hi, nt, preferred_element_type=F32)
                   + lax.dot_general(wrh_ref[...], h2_lo, nt, preferred_element_type=F32))


def _merge(att, z_tm, gates, x, mod, g_post, g_pre_ffn, w_br_mla, w_br_s5, w_out, wr_hi, wr_lo):
    bn, l, d = x.shape
    tm = 256
    nl = l // tm
    row = lambda b, i: (b * nl + i, 0)
    const = lambda b, i: (0, 0)
    one = pl.Buffered(1)
    return pl.pallas_call(
        _merge_kernel,
        grid=(bn, nl),
        in_specs=[pl.BlockSpec((tm, MLA_HEADS * V_HEAD), row),
                  pl.BlockSpec((tm, S5_CH), row),
                  pl.BlockSpec((tm, d), lambda b, i: (b * nl + i, 0)),
                  pl.BlockSpec((tm, d), lambda b, i: (b * nl + i, 1)),
                  pl.BlockSpec((1, tm, d), lambda b, i: (b, i, 0)),
                  pl.BlockSpec((1, 6, d), lambda b, i: (b, 0, 0)),
                  pl.BlockSpec((1, d), const),
                  pl.BlockSpec((1, d), const),
                  pl.BlockSpec((MLA_HEADS * V_HEAD, d), const, pipeline_mode=one),
                  pl.BlockSpec((S5_CH, d), const, pipeline_mode=one),
                  pl.BlockSpec((d, d), const, pipeline_mode=one),
                  pl.BlockSpec((N_EXPERTS, d), const, pipeline_mode=one),
                  pl.BlockSpec((N_EXPERTS, d), const, pipeline_mode=one)],
        out_specs=[pl.BlockSpec((1, tm, d), lambda b, i: (b, i, 0)),
                   pl.BlockSpec((1, tm, d), lambda b, i: (b, i, 0)),
                   pl.BlockSpec((tm * SLAB, LANES), row),
                   pl.BlockSpec((N_EXPERTS, tm), lambda b, i: (0, b * nl + i))],
        out_shape=[jax.ShapeDtypeStruct((bn, l, d), F32),
                   jax.ShapeDtypeStruct((bn, l, d), BF16),
                   jax.ShapeDtypeStruct((bn * l * SLAB, LANES), F32),
                   jax.ShapeDtypeStruct((N_EXPERTS, bn * l), F32)],
        compiler_params=_cparams(("arbitrary", "arbitrary")),
        name="merge",
    )(att, z_tm, gates, gates, x, mod, g_post.reshape(1, d), g_pre_ffn.reshape(1, d),
      w_br_mla, w_br_s5, w_out, wr_hi, wr_lo)


def _route_kernel(lg_ref, b_ref, ti_ref, tw_ref, cnt_ref):
    ng = N_EXPERT_GROUPS
    gsz = N_EXPERTS // ng
    lg = lg_ref[...]
    t = lg.shape[-1]
    sc = jax.nn.sigmoid(lg)
    sel = sc + b_ref[...]
    ninf = -jnp.inf
    i_in = lax.broadcasted_iota(jnp.int32, (ng, gsz, t), 1).astype(F32)
    m1 = jnp.max(sel, axis=1, keepdims=True)
    idx1 = jnp.min(jnp.where(sel == m1, i_in, float(gsz)), axis=1, keepdims=True)
    m2 = jnp.max(jnp.where(i_in == idx1, ninf, sel), axis=1, keepdims=True)
    gs = m1 + m2
    g_i = lax.broadcasted_iota(jnp.int32, (ng, 1, t), 0).astype(F32)
    picked = jnp.zeros((ng, 1, t), F32)
    cur = gs
    for _ in range(TOPK_GROUPS):
        m = jnp.max(cur, axis=0, keepdims=True)
        idx = jnp.min(jnp.where(cur == m, g_i, float(ng)), axis=0, keepdims=True)
        hit = g_i == idx
        picked = jnp.where(hit, 1.0, picked)
        cur = jnp.where(hit, ninf, cur)
    cand = jnp.where(picked > 0.5, sel, ninf)
    e_i = lax.broadcasted_iota(jnp.int32, (ng, gsz, t), 0).astype(F32) * float(gsz) + i_in
    ws = []
    hits = jnp.zeros((ng, gsz, t), F32)
    for r in range(TOP_K):
        m = jnp.max(jnp.max(cand, axis=1, keepdims=True), axis=0, keepdims=True)
        idx = jnp.min(jnp.min(jnp.where(cand == m, e_i, float(N_EXPERTS)), axis=1, keepdims=True),
                      axis=0, keepdims=True)
        hit = e_i == idx
        w = jnp.sum(jnp.sum(jnp.where(hit, sc, 0.0), axis=1, keepdims=True), axis=0, keepdims=True)
        ti_ref[r:r + 1, :] = idx[0].astype(jnp.int32)
        ws.append(w[0])
        hits = jnp.where(hit, 1.0, hits)
        cand = jnp.where(hit, ninf, cand)

    @pl.when(pl.program_id(0) == 0)
    def _():
        cnt_ref[...] = jnp.zeros(cnt_ref.shape, F32)

    cnt_ref[...] += jnp.sum(hits, axis=2, keepdims=True)
    tot = ws[0]
    for w in ws[1:]:
        tot = tot + w
    for r in range(TOP_K):
        tw_ref[r:r + 1, :] = ws[r] / tot * ROUTED_SCALE
    for r in range(TOP_K, SUBLANES):
        ti_ref[r:r + 1, :] = jnp.zeros((1, t), jnp.int32)
        tw_ref[r:r + 1, :] = jnp.zeros((1, t), F32)


def _route(logits_t, router_bias):
    n = logits_t.shape[1]
    ng = N_EXPERT_GROUPS
    gsz = N_EXPERTS // ng
    tn = 1024
    return pl.pallas_call(
        _route_kernel,
        grid=(n // tn,),
        in_specs=[pl.BlockSpec((ng, gsz, tn), lambda i: (0, 0, i)),
                  pl.BlockSpec((ng, gsz, 1), lambda i: (0, 0, 0))],
        out_specs=[pl.BlockSpec((SUBLANES, tn), lambda i: (0, i)),
                   pl.BlockSpec((SUBLANES, tn), lambda i: (0, i)),
                   pl.BlockSpec((ng, gsz, 1), lambda i: (0, 0, 0))],
        out_shape=[jax.ShapeDtypeStruct((SUBLANES, n), jnp.int32),
                   jax.ShapeDtypeStruct((SUBLANES, n), F32),
                   jax.ShapeDtypeStruct((ng, gsz, 1), F32)],
        compiler_params=_cparams(("arbitrary",)),
        name="route",
    )(logits_t.reshape(ng, gsz, n), router_bias.reshape(ng, gsz, 1))


def _moe_kernel(blk_e_ref, nused_ref, nvalid_ref, nxt_e_ref, wslot_ref, t_ref, tn_ref, a_ref, h_hbm,
                wg_hbm, wu_hbm, wd_hbm, y_hbm, xbuf, ybuf, wg_buf, wu_buf, wd_buf, wg_s, wu_s, wd_s,
                gsem, wsem, wtsem):
    j = pl.program_id(0)
    n_used = nused_ref[0]
    slot = lax.rem(j, 2)
    nv = nvalid_ref[j]
    nv_prev = nvalid_ref[jnp.maximum(j - 1, 0)]

    def slab(ref, row0):
        return ref.at[pl.ds(pl.multiple_of(row0, SUBLANES), SLAB), :]

    def start_gather(idx_ref, s):
        def body(r, _):
            pltpu.make_async_copy(slab(h_hbm, idx_ref[0, 0, r]), slab(xbuf.at[s], r * MOE_PITCH), gsem.at[s]).start()
            return 0
        lax.fori_loop(0, MOE_TB, body, 0, unroll=8)

    def wait_gather(s):
        rows = MOE_TB * SLAB
        pltpu.make_async_copy(h_hbm.at[pl.ds(0, rows), :], xbuf.at[s, pl.ds(0, rows), :], gsem.at[s]).wait()

    def weight_copies(expert, s):
        return (pltpu.make_async_copy(wg_hbm.at[expert], wg_buf.at[s], wtsem.at[s]),
                pltpu.make_async_copy(wu_hbm.at[expert], wu_buf.at[s], wtsem.at[s]),
                pltpu.make_async_copy(wd_hbm.at[expert], wd_buf.at[s], wtsem.at[s]))

    def wait_writes(count):
        p = MOE_TB
        while p >= 1:
            @pl.when((count & p) != 0)
            def _(p=p):
                rows = p * SLAB
                pltpu.make_async_copy(ybuf.at[pl.ds(0, rows), :], y_hbm.at[pl.ds(0, rows), :], wsem.at[0]).wait()
            p //= 2

    @pl.when(j < n_used)
    def _():
        @pl.when(j == 0)
        def _():
            start_gather(t_ref, 0)

        e = blk_e_ref[j]
        ws = wslot_ref[j]

        @pl.when(j == 0)
        def _():
            for cp in weight_copies(e, ws):
                cp.start()

        first = jnp.logical_or(j == 0, e != blk_e_ref[jnp.maximum(j - 1, 0)])

        @pl.when(first)
        def _():
            for cp in weight_copies(e, ws):
                cp.wait()
            wg_s[...] = wg_buf[ws].astype(BF16)
            wu_s[...] = wu_buf[ws].astype(BF16)
            wd_s[...] = wd_buf[ws].astype(BF16)

            @pl.when(nxt_e_ref[j] >= 0)
            def _():
                for cp in weight_copies(nxt_e_ref[j], 1 - ws):
                    cp.start()

        wait_gather(slot)

        @pl.when(j + 1 < n_used)
        def _():
            start_gather(tn_ref, 1 - slot)

        xs = xbuf.at[slot]
        x = jnp.concatenate([xs[pl.ds(s, MOE_TB, stride=MOE_PITCH), :] for s in range(SLAB)], axis=1).astype(BF16)
        g = jnp.dot(x, wg_s[...], preferred_element_type=F32)
        u = jnp.dot(x, wu_s[...], preferred_element_type=F32)
        hm = (g * jax.nn.sigmoid(g) * u).astype(BF16)
        y = jnp.dot(hm, wd_s[...], preferred_element_type=F32)

        @pl.when(j > 0)
        def _():
            wait_writes(nv_prev)

        for s in range(SLAB):
            ybuf[pl.ds(s, MOE_TB, stride=MOE_PITCH), :] = y[:, s * LANES:(s + 1) * LANES]

        def start_write(r):
            pltpu.make_async_copy(slab(ybuf, r * MOE_PITCH), slab(y_hbm, a_ref[0, 0, r]), wsem.at[0]).start()

        def wgroup(g8, _):
            for q in range(SUBLANES):
                start_write(g8 * SUBLANES + q)
            return 0

        def wtail(r, _):
            start_write(r)
            return 0

        n_groups = lax.shift_right_logical(nv, 3)
        lax.fori_loop(0, n_groups, wgroup, 0)
        lax.fori_loop(n_groups * SUBLANES, nv, wtail, 0)

        @pl.when(j == n_used - 1)
        def _():
            wait_writes(nv)


def _moe(h2s, topi, counts, w_gate, w_up, w_down):
    n = h2s.shape[0] // SLAB
    d = D_MODEL
    nk = n * TOP_K
    tb = MOE_TB
    n_blocks = nk // tb + N_EXPERTS
    flat_e = topi[:TOP_K].reshape(nk)
    _, order = lax.sort_key_val(flat_e, jnp.arange(nk, dtype=jnp.int32))
    start = jnp.cumsum(counts) - counts
    padded = (counts + tb - 1) // tb * tb
    pad_end = jnp.cumsum(padded)
    pad_start = pad_end - padded
    n_used = (pad_end[-1] // tb).astype(jnp.int32).reshape(1)
    blk_p0 = jnp.arange(n_blocks, dtype=jnp.int32) * tb
    blk_e = jnp.minimum(jnp.sum((pad_end[None, :] <= blk_p0[:, None]).astype(jnp.int32), axis=1), N_EXPERTS - 1)
    blk_off = blk_p0 - pad_start[blk_e]
    nvalid = jnp.clip(counts[blk_e] - blk_off, 0, tb).astype(jnp.int32)
    rank = (start[blk_e] + blk_off)[:, None] + jnp.arange(tb, dtype=jnp.int32)[None, :]
    a3 = order[jnp.clip(rank, 0, nk - 1)].reshape(n_blocks, 1, tb)
    arow = a3 * SLAB
    trow = (a3 % n) * SLAB
    used = counts > 0
    e_ids = jnp.arange(N_EXPERTS, dtype=jnp.int32)
    later = jnp.where(used, e_ids, N_EXPERTS)
    nxt = jnp.concatenate([lax.cummin(later[::-1])[::-1][1:], jnp.full((1,), N_EXPERTS, jnp.int32)])
    nxt_e = jnp.where(nxt < N_EXPERTS, nxt, -1)[blk_e].astype(jnp.int32)
    wslot = ((jnp.cumsum(used.astype(jnp.int32)) - 1) % 2)[blk_e].astype(jnp.int32)

    smem_blk = lambda f: pl.BlockSpec((1, 1, tb), f, memory_space=pltpu.SMEM)
    any_spec = pl.BlockSpec(memory_space=pl.ANY)
    grid_spec = pltpu.PrefetchScalarGridSpec(
        num_scalar_prefetch=5,
        grid=(n_blocks,),
        in_specs=[smem_blk(lambda j, *_: (j, 0, 0)),
                  smem_blk(lambda j, *_: (jnp.minimum(j + 1, n_blocks - 1), 0, 0)),
                  smem_blk(lambda j, *_: (j, 0, 0)),
                  any_spec, any_spec, any_spec, any_spec],
        out_specs=any_spec,
        scratch_shapes=[pltpu.VMEM((2, tb * MOE_PITCH, LANES), F32),
                        pltpu.VMEM((tb * MOE_PITCH, LANES), F32),
                        pltpu.VMEM((2, d, D_EXPERT), F32),
                        pltpu.VMEM((2, d, D_EXPERT), F32),
                        pltpu.VMEM((2, D_EXPERT, d), F32),
                        pltpu.VMEM((d, D_EXPERT), BF16),
                        pltpu.VMEM((d, D_EXPERT), BF16),
                        pltpu.VMEM((D_EXPERT, d), BF16),
                        pltpu.SemaphoreType.DMA((2,)),
                        pltpu.SemaphoreType.DMA((1,)),
                        pltpu.SemaphoreType.DMA((2,))],
    )
    return pl.pallas_call(
        _moe_kernel,
        grid_spec=grid_spec,
        out_shape=jax.ShapeDtypeStruct((TOP_K * n * SLAB, LANES), F32),
        compiler_params=_cparams(("arbitrary",)),
        name="moe",
    )(blk_e, n_used, nvalid, nxt_e, wslot, trow, trow, arow, h2s, w_gate, w_up, w_down)


def _final_kernel(h_ref, y0, y1, y2, y3, y4, y5, tw_ref, x1_ref, mod_ref, g_ref, wg_ref, wu_ref, wd_ref, o_ref):
    h = h_ref[...]
    tm = h.shape[0]
    g = jnp.dot(h, wg_ref[...], preferred_element_type=F32)
    u = jnp.dot(h, wu_ref[...], preferred_element_type=F32)
    hm = (g * jax.nn.sigmoid(g) * u).astype(BF16)
    ffn = jnp.dot(hm, wd_ref[...], preferred_element_type=F32)
    tw = tw_ref[...]
    for k, y in enumerate((y0, y1, y2, y3, y4, y5)):
        yk = jnp.concatenate([y[pl.ds(s, tm, stride=SLAB), :] for s in range(SLAB)], axis=1)
        ffn = ffn + yk * tw[:, k:k + 1]
    o_ref[0] = x1_ref[0] + mod_ref[0, 5:6, :] * _rms(ffn, g_ref[...])


def _final(h2, y_flat, topw_t, x1, mod, g_post_ffn, w_sg, w_su, w_sd):
    bn, l, d = x1.shape
    n = bn * l
    tm = 256
    nl = l // tm
    const = lambda b, i: (0, 0)
    one = pl.Buffered(1)
    y_specs = [pl.BlockSpec((tm * SLAB, LANES), functools.partial(lambda b, i, k: (k * (n // tm) + b * nl + i, 0), k=k))
               for k in range(TOP_K)]
    return pl.pallas_call(
        _final_kernel,
        grid=(bn, nl),
        in_specs=[pl.BlockSpec((tm, d), lambda b, i: (b * nl + i, 0))] + y_specs + [
            pl.BlockSpec((tm, SUBLANES), lambda b, i: (b * nl + i, 0)),
            pl.BlockSpec((1, tm, d), lambda b, i: (b, i, 0)),
            pl.BlockSpec((1, 6, d), lambda b, i: (b, 0, 0)),
            pl.BlockSpec((1, d), const),
            pl.BlockSpec((d, D_EXPERT), const, pipeline_mode=one),
            pl.BlockSpec((d, D_EXPERT), const, pipeline_mode=one),
            pl.BlockSpec((D_EXPERT, d), const, pipeline_mode=one)],
        out_specs=pl.BlockSpec((1, tm, d), lambda b, i: (b, i, 0)),
        out_shape=jax.ShapeDtypeStruct((bn, l, d), F32),
        compiler_params=_cparams(("arbitrary", "arbitrary")),
        name="final",
    )(h2, *([y_flat] * TOP_K), topw_t, x1, mod, g_post_ffn.reshape(1, d), w_sg, w_su, w_sd)


def _layer(x, c, positions, w_ada, b_ada, g_pre_mix, g_post_mix, g_pre_ffn, g_post_ffn, w_in, g_q, g_kv,
           w_uq, w_uk, w_uv, a_re, a_im, log_dt, b_re, b_im, c_re, c_im, d_skip, w_glu, w_br_mla, w_br_s5,
           w_out, w_router, router_bias, w_exp_gate, w_exp_up, w_exp_down, w_sh_gate, w_sh_up, w_sh_down):
    bn, l, d = x.shape
    n = bn * l
    mod = _adaln(c, w_ada, b_ada).reshape(bn, 6, d)

    o_kpe = Q_LORA + KV_LORA
    o_u = o_kpe + QK_ROPE
    o_g = o_u + S5_CH
    w_cat = jnp.concatenate([w_in[:, :o_kpe], w_in[:, o_u:], w_in[:, o_kpe:o_u],
                             jnp.zeros((d, LANES - QK_ROPE), w_in.dtype)], axis=1).astype(BF16)
    c_u = o_kpe
    c_g = c_u + S5_CH
    c_kpe = c_g + 2 * d
    w_q = jnp.pad(w_uq.reshape(Q_LORA, MLA_HEADS, QK_NOPE + QK_ROPE),
                  ((0, 0), (0, 0), (0, QK_PAD - QK_NOPE - QK_ROPE))).reshape(Q_LORA, MLA_HEADS * QK_PAD).astype(BF16)
    wr_t = w_router.T
    wr_hi = wr_t.astype(BF16)
    wr_lo = (wr_t - wr_hi.astype(F32)).astype(BF16)

    rope_c, rope_s1, rope_s2 = _rope_tables(positions)
    chunk_id = positions // CHUNK

    h = _prenorm(x, mod, g_pre_mix, 0, 1).reshape(n, d)
    lat = _mm(h, w_cat, 0, o_kpe, F32, 1024, 512, name="mm_lat")
    kpe = _mm(h, w_cat, c_kpe, LANES, F32, 1024, LANES, name="mm_kpe")
    gates = _mm(h, w_cat, c_g, 2 * d, BF16, 1024, 512, act="sigmoid", name="mm_gates")
    u6 = _mm_u(h, w_cat, c_u, bn, l)

    q, k, v = _qkvproj(lat, g_q, g_kv, w_q, w_uk.astype(BF16), w_uv.astype(BF16), kpe, rope_c, rope_s1, rope_s2)
    att = _attention(q, k, v, chunk_id, bn, l)

    wb, ab, wc, dd = _s5_params(a_re, a_im, log_dt, b_re, b_im, c_re, c_im, d_skip)
    y6 = _s5(u6, wb, ab, wc, dd, l)
    z = _glu(y6, w_glu.astype(BF16), bn, l).reshape(n, S5_CH)

    x1, h2b, h2s, logits_t = _merge(att, z, gates, x, mod, g_post_mix, g_pre_ffn, w_br_mla.astype(BF16),
                                    w_br_s5.astype(BF16), w_out.astype(BF16), wr_hi, wr_lo)

    topi, topw, cnt = _route(logits_t, router_bias)
    y_slabs = _moe(h2s, topi, cnt.reshape(N_EXPERTS).astype(jnp.int32), w_exp_gate, w_exp_up, w_exp_down)
    return _final(h2b.reshape(n, d), y_slabs, topw.T, x1, mod, g_post_ffn, w_sh_gate.astype(BF16),
                  w_sh_up.astype(BF16), w_sh_down.astype(BF16))


def kernel(x, c, positions, w_ada, b_ada, g_pre_mix, g_post_mix, g_pre_ffn, g_post_ffn, w_in, g_q, g_kv, w_uq, w_uk, w_uv, a_re, a_im, log_dt, b_re, b_im, c_re, c_im, d_skip, w_glu, w_br_mla, w_br_s5, w_out, w_router, router_bias, w_exp_gate, w_exp_up, w_exp_down, w_sh_gate, w_sh_up, w_sh_down):
    depth = w_ada.shape[0]
    for li in range(depth):
        x = _layer(x, c, positions, w_ada[li], b_ada[li], g_pre_mix[li], g_post_mix[li], g_pre_ffn[li],
                   g_post_ffn[li], w_in[li], g_q[li], g_kv[li], w_uq[li], w_uk[li], w_uv[li], a_re[li], a_im[li],
                   log_dt[li], b_re[li], b_im[li], c_re[li], c_im[li], d_skip[li], w_glu[li], w_br_mla[li],
                   w_br_s5[li], w_out[li], w_router[li], router_bias[li], w_exp_gate[li], w_exp_up[li],
                   w_exp_down[li], w_sh_gate[li], w_sh_up[li], w_sh_down[li])
    return x
```

```python
import functools

import jax
import jax.numpy as jnp
from jax import lax
from jax.experimental import pallas as pl
from jax.experimental.pallas import tpu as pltpu

F32 = jnp.float32
BF16 = jnp.bfloat16

D_MODEL = 2048
CHUNK = 64
EPS = 1e-6
MLA_HEADS = 8
QK_NOPE = 128
QK_ROPE = 64
V_HEAD = 128
Q_LORA = 512
KV_LORA = 512
ROPE_THETA = 10000.0
S5_CH = 1024
S5_GROUP = 16
S5_GROUPS = S5_CH // S5_GROUP
S5_STATE = 64
N_EXPERTS = 64
TOP_K = 6
N_EXPERT_GROUPS = 8
TOPK_GROUPS = 4
D_EXPERT = 512
ROUTED_SCALE = 2.5

LANES = 128
SUBLANES = 8
QK_PAD = 2 * LANES
VMEM_LIMIT = 56 * 1024 * 1024
NEG = -1e30
LOG2E = 1.4426950408889634

S5_GB = LANES // S5_GROUP
S5_NBLK = S5_CH // LANES
S5_PAIRS = S5_NBLK // 2
S5_LANES = S5_GB * S5_STATE
S5_TC = 256
S5_RB = 512

ATT_TQ = 256
ATT_TK = 256
MOE_TB = 256
SLAB = D_MODEL // LANES
MOE_PITCH = 24


def _cparams(sem):
    return pltpu.CompilerParams(dimension_semantics=sem, vmem_limit_bytes=VMEM_LIMIT)


def _rms(x, g):
    return x * lax.rsqrt(jnp.mean(x * x, axis=-1, keepdims=True) + EPS) * g


def _adaln_kernel(c_ref, w_ref, b_ref, o_ref):
    c = c_ref[...]
    a = (c * jax.nn.sigmoid(c)).astype(BF16)
    o_ref[...] = jnp.dot(a, w_ref[...].astype(BF16), preferred_element_type=F32) + b_ref[...]


def _adaln(c, w, b):
    bn, d = c.shape
    n = w.shape[1]
    tn = 1024
    return pl.pallas_call(
        _adaln_kernel,
        grid=(n // tn,),
        in_specs=[pl.BlockSpec((bn, d), lambda j: (0, 0)),
                  pl.BlockSpec((d, tn), lambda j: (0, j)),
                  pl.BlockSpec((1, tn), lambda j: (0, j))],
        out_specs=pl.BlockSpec((bn, tn), lambda j: (0, j)),
        out_shape=jax.ShapeDtypeStruct((bn, n), F32),
        compiler_params=_cparams(("arbitrary",)),
        name="adaln",
    )(c, w, b.reshape(1, n))


def _prenorm_kernel(x_ref, mod_ref, g_ref, o_ref, *, sh_row, sc_row):
    y = _rms(x_ref[0], g_ref[...])
    o_ref[0] = (y * (1.0 + mod_ref[0, sc_row:sc_row + 1, :]) + mod_ref[0, sh_row:sh_row + 1, :]).astype(o_ref.dtype)


def _prenorm(x, mod, g, sh_row, sc_row):
    bn, l, d = x.shape
    tl = 512
    return pl.pallas_call(
        functools.partial(_prenorm_kernel, sh_row=sh_row, sc_row=sc_row),
        grid=(bn, l // tl),
        in_specs=[pl.BlockSpec((1, tl, d), lambda b, i: (b, i, 0)),
                  pl.BlockSpec((1, 6, d), lambda b, i: (b, 0, 0)),
                  pl.BlockSpec((1, d), lambda b, i: (0, 0))],
        out_specs=pl.BlockSpec((1, tl, d), lambda b, i: (b, i, 0)),
        out_shape=jax.ShapeDtypeStruct((bn, l, d), BF16),
        compiler_params=_cparams(("arbitrary", "arbitrary")),
        name="prenorm",
    )(x, mod, g.reshape(1, d))


def _mm_kernel(a_ref, w_ref, o_ref, *w_scr, act):
    if w_scr:
        @pl.when(pl.program_id(1) == 0)
        def _():
            w_scr[0][...] = w_ref[...].astype(BF16)
        w = w_scr[0][...]
    else:
        w = w_ref[...]
    acc = jnp.dot(a_ref[...], w, preferred_element_type=F32)
    if act == "sigmoid":
        acc = jax.nn.sigmoid(acc)
    o_ref[...] = acc.astype(o_ref.dtype)


def _mm(a, w, col0, n, out_dtype, tm, tn, act=None, name="mm"):
    m, k = a.shape
    j0 = col0 // tn
    return pl.pallas_call(
        functools.partial(_mm_kernel, act=act),
        grid=(n // tn, m // tm),
        in_specs=[pl.BlockSpec((tm, k), lambda j, i: (i, 0)),
                  pl.BlockSpec((k, tn), lambda j, i: (0, j + j0))],
        out_specs=pl.BlockSpec((tm, tn), lambda j, i: (i, j)),
        out_shape=jax.ShapeDtypeStruct((m, n), out_dtype),
        scratch_shapes=[] if w.dtype == BF16 else [pltpu.VMEM((k, tn), BF16)],
        compiler_params=_cparams(("arbitrary", "arbitrary")),
        name=name,
    )(a, w)


def _regroup_kernel(a_ref, b_ref, o_ref, *, shift):
    w = jnp.concatenate([a_ref[...], b_ref[...]], axis=1)
    o_ref[...] = w[:, shift:shift + o_ref.shape[1]].astype(o_ref.dtype)


def _regroup(w, col0, n):
    k = w.shape[0]
    tr, tw = 512, 1024
    shift = col0 % LANES
    c0 = col0 - shift
    assert c0 % tw == 0 and n % tw == 0
    return pl.pallas_call(
        functools.partial(_regroup_kernel, shift=shift),
        grid=(n // tw, k // tr),
        in_specs=[pl.BlockSpec((tr, tw), lambda j, r: (r, j + c0 // tw)),
                  pl.BlockSpec((tr, LANES), lambda j, r: (r, (c0 + (j + 1) * tw) // LANES))],
        out_specs=pl.BlockSpec((tr, tw), lambda j, r: (r, j)),
        out_shape=jax.ShapeDtypeStruct((k, n), BF16),
        compiler_params=_cparams(("arbitrary", "arbitrary")),
        name="regroup",
    )(w, w)


def _mm_u_kernel(a_ref, w_ref, o_ref):
    res = jnp.dot(a_ref[...], w_ref[...], preferred_element_type=F32)
    tm = res.shape[0]
    b = pl.program_id(2)
    for s in range(2):
        o_ref.at[0][pl.ds(b * 2 + s, tm, stride=SUBLANES), :] = res[:, s * LANES:(s + 1) * LANES]


def _mm_u(h, w, col0, bn, l):
    m, k = h.shape
    tm = 1024
    nl = l // tm
    tn = 2 * LANES
    j0 = col0 // tn
    assert bn * 2 == SUBLANES
    return pl.pallas_call(
        _mm_u_kernel,
        grid=(S5_PAIRS, nl, bn),
        in_specs=[pl.BlockSpec((tm, k), lambda j, i, b: (b * nl + i, 0)),
                  pl.BlockSpec((k, tn), lambda j, i, b: (0, j + j0))],
        out_specs=pl.BlockSpec((1, tm * SUBLANES, LANES), lambda j, i, b: (j, i, 0)),
        out_shape=jax.ShapeDtypeStruct((S5_PAIRS, l * SUBLANES, LANES), F32),
        compiler_params=_cparams(("arbitrary", "arbitrary", "arbitrary")),
        name="mm_u",
    )(h, w)


def _rope_tab_kernel(pos_ref, k_ref, c_ref, s1_ref, s2_ref):
    ang = pos_ref[...].astype(F32) * k_ref[0:1, :]
    s = jnp.sin(ang)
    c_ref[...] = jnp.cos(ang) * k_ref[1:2, :]
    s1_ref[...] = s * k_ref[2:3, :]
    s2_ref[...] = s * k_ref[3:4, :]


def _rope_tables(positions):
    n = positions.size
    half = QK_ROPE // 2
    inv_freq = ROPE_THETA ** (-jnp.arange(half, dtype=F32) / half)
    zh, oh = jnp.zeros((half,), F32), jnp.ones((half,), F32)
    z2 = jnp.zeros((LANES - QK_ROPE,), F32)
    rows = [jnp.concatenate([inv_freq, inv_freq, z2]), jnp.concatenate([oh, oh, z2]),
            jnp.concatenate([-oh, zh, z2]), jnp.concatenate([zh, oh, z2])]
    consts = jnp.stack(rows + [jnp.zeros((LANES,), F32)] * (SUBLANES - len(rows)))
    tm = 1024
    tab = jax.ShapeDtypeStruct((n, LANES), F32)
    return pl.pallas_call(
        _rope_tab_kernel,
        grid=(n // tm,),
        in_specs=[pl.BlockSpec((tm, 1), lambda i: (i, 0)),
                  pl.BlockSpec((SUBLANES, LANES), lambda i: (0, 0))],
        out_specs=[pl.BlockSpec((tm, LANES), lambda i: (i, 0))] * 3,
        out_shape=[tab, tab, tab],
        compiler_params=_cparams(("arbitrary",)),
        name="rope_tables",
    )(positions.reshape(n, 1), consts)


def _rope_tile(t, c_ref, s1_ref, s2_ref):
    return (t * c_ref[...] + pltpu.roll(t, LANES - QK_ROPE // 2, 1) * s1_ref[...]
            + pltpu.roll(t, QK_ROPE // 2, 1) * s2_ref[...])


def _qkvproj_kernel(lat_ref, gq_ref, gkv_ref, wq_ref, wk_ref, wv_ref, kpe_ref, c_ref, s1_ref, s2_ref,
                    q_ref, k_ref, v_ref, *, scale):
    lat = lat_ref[...]
    qn = _rms(lat[:, :Q_LORA], gq_ref[...]).astype(BF16)
    cn = _rms(lat[:, Q_LORA:], gkv_ref[...]).astype(BF16)
    q = jnp.dot(qn, wq_ref[...], preferred_element_type=F32)
    kn = jnp.dot(cn, wk_ref[...], preferred_element_type=F32)
    v_ref[...] = jnp.dot(cn, wv_ref[...], preferred_element_type=F32).astype(v_ref.dtype)
    kt = _rope_tile(kpe_ref[...], c_ref, s1_ref, s2_ref).astype(k_ref.dtype)
    for h in range(MLA_HEADS):
        o = h * QK_PAD
        q_ref[:, o:o + LANES] = (q[:, o:o + LANES] * scale).astype(q_ref.dtype)
        qt = _rope_tile(q[:, o + LANES:o + QK_PAD], c_ref, s1_ref, s2_ref)
        q_ref[:, o + LANES:o + QK_PAD] = (qt * scale).astype(q_ref.dtype)
        k_ref[:, o:o + LANES] = kn[:, h * QK_NOPE:(h + 1) * QK_NOPE].astype(k_ref.dtype)
        k_ref[:, o + LANES:o + QK_PAD] = kt


def _qkvproj(lat, g_q, g_kv, w_q, w_k, w_v, kpe, rope_c, rope_s1, rope_s2):
    n = lat.shape[0]
    tm = 512
    row = lambda i: (i, 0)
    const = lambda i: (0, 0)
    tab = pl.BlockSpec((tm, LANES), row)
    return pl.pallas_call(
        functools.partial(_qkvproj_kernel, scale=(QK_NOPE + QK_ROPE) ** -0.5 * LOG2E),
        grid=(n // tm,),
        in_specs=[pl.BlockSpec((tm, Q_LORA + KV_LORA), row),
                  pl.BlockSpec((1, Q_LORA), const),
                  pl.BlockSpec((1, KV_LORA), const),
                  pl.BlockSpec((Q_LORA, MLA_HEADS * QK_PAD), const),
                  pl.BlockSpec((KV_LORA, MLA_HEADS * QK_NOPE), const),
                  pl.BlockSpec((KV_LORA, MLA_HEADS * V_HEAD), const),
                  tab, tab, tab, tab],
        out_specs=[pl.BlockSpec((tm, MLA_HEADS * QK_PAD), row),
                   pl.BlockSpec((tm, MLA_HEADS * QK_PAD), row),
                   pl.BlockSpec((tm, MLA_HEADS * V_HEAD), row)],
        out_shape=[jax.ShapeDtypeStruct((n, MLA_HEADS * QK_PAD), BF16),
                   jax.ShapeDtypeStruct((n, MLA_HEADS * QK_PAD), BF16),
                   jax.ShapeDtypeStruct((n, MLA_HEADS * V_HEAD), BF16)],
        compiler_params=_cparams(("arbitrary",)),
        name="qkvproj",
    )(lat, g_q.reshape(1, Q_LORA), g_kv.reshape(1, KV_LORA), w_q, w_k, w_v, kpe, rope_c, rope_s1, rope_s2)


def _attn_kernel(lo_ref, hi_ref, q_ref, k_ref, v_ref, qc_ref, kc_ref, o_ref, m_scr, l_scr, acc_scr, *, nq):
    b = pl.program_id(0)
    i = pl.program_id(1)
    qc = qc_ref[...]
    m_scr[...] = jnp.full(m_scr.shape, NEG, F32)
    l_scr[...] = jnp.zeros(l_scr.shape, F32)
    acc_scr[...] = jnp.zeros(acc_scr.shape, F32)

    def make_body(masked):
        def body(j, _):
            off = pl.multiple_of(j * ATT_TK, ATT_TK)
            if masked:
                mask = kc_ref[j] <= qc
            for h in range(MLA_HEADS):
                q = q_ref[:, h * QK_PAD:(h + 1) * QK_PAD]
                k = k_ref[pl.ds(off, ATT_TK), h * QK_PAD:(h + 1) * QK_PAD]
                s = lax.dot_general(q, k, (((1,), (1,)), ((), ())), preferred_element_type=F32)
                if masked:
                    s = jnp.where(mask, s, NEG)
                m_old = m_scr[h]
                m_new = jnp.maximum(m_old, jnp.max(s, axis=-1, keepdims=True))
                p = jnp.exp2(s - jnp.concatenate([m_new] * (ATT_TK // LANES), axis=1))
                alpha = jnp.exp2(m_old - m_new)
                l_scr[h] = alpha * l_scr[h] + jnp.sum(p, axis=-1, keepdims=True)
                v = v_ref[pl.ds(off, ATT_TK), h * V_HEAD:(h + 1) * V_HEAD]
                acc_scr[h] = alpha * acc_scr[h] + jnp.dot(p.astype(BF16), v, preferred_element_type=F32)
                m_scr[h] = m_new
            return 0
        return body

    lo = lo_ref[b * nq + i]
    lax.fori_loop(0, lo, make_body(False), 0)
    lax.fori_loop(lo, hi_ref[b * nq + i], make_body(True), 0)
    for h in range(MLA_HEADS):
        o_ref[:, h * V_HEAD:(h + 1) * V_HEAD] = (acc_scr[h] / l_scr[h]).astype(o_ref.dtype)


def _attention(q, k, v, chunk_id, bn, l):
    nq = l // ATT_TQ
    nk = l // ATT_TK
    q_max = jnp.max(chunk_id.reshape(bn, nq, ATT_TQ), axis=-1)
    k_min = jnp.min(chunk_id.reshape(bn, nk, ATT_TK), axis=-1)
    needed = k_min[:, None, :] <= q_max[:, :, None]
    hi = jnp.max(jnp.where(needed, jnp.arange(1, nk + 1, dtype=jnp.int32), 0), axis=-1).reshape(bn * nq)
    q_min = jnp.min(chunk_id.reshape(bn, nq, ATT_TQ), axis=-1)
    k_max = jnp.max(chunk_id.reshape(bn, nk, ATT_TK), axis=-1)
    full = (k_max[:, None, :] <= q_min[:, :, None]).astype(jnp.int32)
    lo = jnp.sum(jnp.cumprod(full, axis=-1), axis=-1).astype(jnp.int32).reshape(bn * nq)
    qc = chunk_id.reshape(bn * l, 1)
    kc = chunk_id.reshape(bn * nk, 1, ATT_TK)
    grid_spec = pltpu.PrefetchScalarGridSpec(
        num_scalar_prefetch=2,
        grid=(bn, nq),
        in_specs=[pl.BlockSpec((ATT_TQ, MLA_HEADS * QK_PAD), lambda b, i, *_: (b * nq + i, 0)),
                  pl.BlockSpec((l, MLA_HEADS * QK_PAD), lambda b, i, *_: (b, 0)),
                  pl.BlockSpec((l, MLA_HEADS * V_HEAD), lambda b, i, *_: (b, 0)),
                  pl.BlockSpec((ATT_TQ, 1), lambda b, i, *_: (b * nq + i, 0)),
                  pl.BlockSpec((nk, 1, ATT_TK), lambda b, i, *_: (b, 0, 0))],
        out_specs=pl.BlockSpec((ATT_TQ, MLA_HEADS * V_HEAD), lambda b, i, *_: (b * nq + i, 0)),
        scratch_shapes=[pltpu.VMEM((MLA_HEADS, ATT_TQ, LANES), F32),
                        pltpu.VMEM((MLA_HEADS, ATT_TQ, LANES), F32),
                        pltpu.VMEM((MLA_HEADS, ATT_TQ, V_HEAD), F32)],
    )
    return pl.pallas_call(
        functools.partial(_attn_kernel, nq=nq),
        grid_spec=grid_spec,
        out_shape=jax.ShapeDtypeStruct((bn * l, MLA_HEADS * V_HEAD), BF16),
        compiler_params=_cparams(("arbitrary", "arbitrary")),
        name="attention",
    )(lo, hi, q, k, v, qc, kc)


def _s5_kernel(u_ref, wb_ref, a_ref, wc_ref, d_ref, o_ref, x_scr, st_scr):
    rows = S5_TC * SUBLANES
    nsub = rows // S5_RB

    @pl.when(pl.program_id(1) == 0)
    def _():
        st_scr[...] = jnp.zeros_like(st_scr)

    even = (lax.broadcasted_iota(jnp.int32, (S5_RB, 1), 0) & 1) == 0
    half = S5_RB // 2
    nre = S5_LANES // LANES
    uv = u_ref.at[0]

    def mm_in(r, _):
        off = pl.multiple_of(r * S5_RB, S5_RB)
        for s in range(2):
            us = uv[pl.ds(off + s, half, stride=2), :].astype(BF16)
            out = jnp.dot(us, wb_ref[0, :, s * 2 * S5_LANES:(s + 1) * 2 * S5_LANES], preferred_element_type=F32)
            for c in range(2 * nre):
                x_scr.at[c][pl.ds(off + s, half, stride=2), :] = out[:, c * LANES:(c + 1) * LANES]
        return 0

    lax.fori_loop(0, nsub, mm_in, 0)

    a_c = [a_ref[0, :, c * LANES:(c + 1) * LANES] for c in range(2 * nre)]

    def step(t, carry):
        off = pl.multiple_of(t * SUBLANES, SUBLANES)
        new = [None] * (2 * nre)
        for c in range(nre):
            xr, xi = carry[c], carry[nre + c]
            ar, ai = a_c[c], a_c[nre + c]
            nr = ar * xr - ai * xi + x_scr[c, pl.ds(off, SUBLANES), :]
            ni = ar * xi + ai * xr + x_scr[nre + c, pl.ds(off, SUBLANES), :]
            x_scr[c, pl.ds(off, SUBLANES), :] = nr
            x_scr[nre + c, pl.ds(off, SUBLANES), :] = ni
            new[c], new[nre + c] = nr, ni
        return tuple(new)

    init = tuple(st_scr[:, c * LANES:(c + 1) * LANES] for c in range(2 * nre))
    fin = lax.fori_loop(0, S5_TC, step, init, unroll=8)
    for c in range(2 * nre):
        st_scr[:, c * LANES:(c + 1) * LANES] = fin[c]

    d = jnp.concatenate([d_ref[0]] * (S5_RB // SUBLANES), axis=0)

    def mm_out(r, _):
        off = pl.multiple_of(r * S5_RB, S5_RB)
        x = jnp.concatenate([x_scr[c, pl.ds(off, S5_RB), :] for c in range(2 * nre)], axis=1).astype(BF16)
        out = jnp.dot(x, wc_ref[0], preferred_element_type=F32)
        y = jnp.where(even, out[:, :LANES], out[:, LANES:]) + d * u_ref[0, pl.ds(off, S5_RB), :]
        o_ref[0, pl.ds(off, S5_RB), :] = jax.nn.gelu(y).astype(o_ref.dtype)
        return 0

    lax.fori_loop(0, nsub, mm_out, 0)


def _s5_params(a_re, a_im, log_dt, b_re, b_im, c_re, c_im, d_skip):
    step = jnp.exp(log_dt)[:, None]
    mag = jnp.exp(a_re * step)
    abar_re, abar_im = mag * jnp.cos(a_im * step), mag * jnp.sin(a_im * step)
    den = a_re * a_re + a_im * a_im
    nr, ni = abar_re - 1.0, abar_im
    f_re, f_im = (nr * a_re + ni * a_im) / den, (ni * a_re - nr * a_im) / den
    bbar_re = f_re[..., None] * b_re - f_im[..., None] * b_im
    bbar_im = f_re[..., None] * b_im + f_im[..., None] * b_re
    eye = jnp.eye(S5_GB, dtype=F32)
    bb = jnp.stack([bbar_re, bbar_im]).reshape(2, S5_NBLK, S5_GB, S5_STATE, S5_GROUP)
    wb = jnp.einsum('ab,rjapc->jacrbp', eye, bb).reshape(S5_NBLK, LANES, 2 * S5_LANES)
    wb = wb.reshape(S5_PAIRS, 2, LANES, 2 * S5_LANES).transpose(0, 2, 1, 3).reshape(S5_PAIRS, LANES, 4 * S5_LANES)
    cc = jnp.stack([c_re, -c_im]).reshape(2, S5_NBLK, S5_GB, S5_GROUP, S5_STATE)
    wc = jnp.einsum('ab,rjacp->jrapbc', eye, cc).reshape(S5_NBLK, 2 * S5_LANES, LANES)
    wc = wc.reshape(S5_PAIRS, 2, 2 * S5_LANES, LANES).transpose(0, 2, 1, 3).reshape(S5_PAIRS, 2 * S5_LANES, 2 * LANES)
    ab = jnp.concatenate([abar_re.reshape(S5_NBLK, S5_LANES), abar_im.reshape(S5_NBLK, S5_LANES)], axis=1)
    ab = jnp.tile(ab.reshape(S5_PAIRS, 2, 2 * S5_LANES), (1, SUBLANES // 2, 1))
    dd = jnp.tile(d_skip.reshape(S5_PAIRS, 2, LANES), (1, SUBLANES // 2, 1))
    return wb.astype(BF16), ab, wc.astype(BF16), dd


def _s5(u6, wb, ab, wc, dd, l):
    rows = S5_TC * SUBLANES
    return pl.pallas_call(
        _s5_kernel,
        grid=(S5_PAIRS, l // S5_TC),
        in_specs=[pl.BlockSpec((1, rows, LANES), lambda k, c: (k, c, 0)),
                  pl.BlockSpec((1, LANES, 4 * S5_LANES), lambda k, c: (k, 0, 0)),
                  pl.BlockSpec((1, SUBLANES, 2 * S5_LANES), lambda k, c: (k, 0, 0)),
                  pl.BlockSpec((1, 2 * S5_LANES, 2 * LANES), lambda k, c: (k, 0, 0)),
                  pl.BlockSpec((1, SUBLANES, LANES), lambda k, c: (k, 0, 0))],
        out_specs=pl.BlockSpec((1, rows, LANES), lambda k, c: (k, c, 0)),
        out_shape=jax.ShapeDtypeStruct(u6.shape, F32),
        scratch_shapes=[pltpu.VMEM((2 * S5_LANES // LANES, rows, LANES), F32),
                        pltpu.VMEM((SUBLANES, 2 * S5_LANES), F32)],
        compiler_params=_cparams(("arbitrary", "arbitrary")),
        name="s5",
    )(u6, wb, ab, wc, dd)


def _glu_kernel(y_ref, w_ref, o_ref, z_scr, *, bn):
    b = pl.program_id(1)
    rows = z_scr.shape[1]

    @pl.when(b == 0)
    def _():
        y = jnp.concatenate([y_ref.at[k][pl.ds(s, rows, stride=2), :] for k in range(S5_PAIRS) for s in range(2)],
                            axis=1)
        g = jnp.dot(y.astype(BF16), w_ref[...], preferred_element_type=F32)
        z = y * jax.nn.sigmoid(g)
        for c in range(S5_NBLK):
            z_scr[c] = z[:, c * LANES:(c + 1) * LANES]

    o_ref[0] = jnp.concatenate([z_scr.at[c][pl.ds(b, rows // bn, stride=bn), :] for c in range(S5_NBLK)],
                               axis=1).astype(o_ref.dtype)


def _glu(y6, w_glu, bn, l):
    tt = 256
    rows = tt * bn
    return pl.pallas_call(
        functools.partial(_glu_kernel, bn=bn),
        grid=(l // tt, bn),
        in_specs=[pl.BlockSpec((S5_PAIRS, rows * 2, LANES), lambda i, b: (0, i, 0)),
                  pl.BlockSpec((S5_CH, S5_CH), lambda i, b: (0, 0))],
        out_specs=pl.BlockSpec((1, tt, S5_CH), lambda i, b: (b, i, 0)),
        out_shape=jax.ShapeDtypeStruct((bn, l, S5_CH), BF16),
        scratch_shapes=[pltpu.VMEM((S5_NBLK, rows, LANES), F32)],
        compiler_params=_cparams(("arbitrary", "arbitrary")),
        name="glu",
    )(y6, w_glu)


def _merge_kernel(att_ref, z_ref, gm_ref, gs_ref, x_ref, mod_ref, gpost_ref, gpre_ref,
                  wbm_ref, wbs_ref, wo_ref, wrh_ref, wrl_ref, x1_ref, h2b_ref, h2s_ref, lg_ref):
    ym = jnp.dot(att_ref[...], wbm_ref[...], preferred_element_type=F32)
    ys = jnp.dot(z_ref[...], wbs_ref[...], preferred_element_type=F32)
    mixed_in = (gm_ref[...].astype(F32) * ym + gs_ref[...].astype(F32) * ys).astype(BF16)
    mixed = jnp.dot(mixed_in, wo_ref[...], preferred_element_type=F32)
    x1 = x_ref[0] + mod_ref[0, 2:3, :] * _rms(mixed, gpost_ref[...])
    x1_ref[0] = x1
    h2 = _rms(x1, gpre_ref[...]) * (1.0 + mod_ref[0, 4:5, :]) + mod_ref[0, 3:4, :]
    h2_hi = h2.astype(BF16)
    h2b_ref[0] = h2_hi
    tm = h2.shape[0]
    for s in range(SLAB):
        h2s_ref[pl.ds(s, tm, stride=SLAB), :] = h2[:, s * LANES:(s + 1) * LANES]
    h2_lo = (h2 - h2_hi.astype(F32)).astype(BF16)
    nt = (((1,), (1,)), ((), ()))
    lg_ref[...] = (lax.dot_general(wrh_ref[...], h2_hi, nt, preferred_element_type=F32)
                   + lax.dot_general(wrl_ref[...], h2_hi, nt, preferred_element_type=F32)
                   + lax.dot_general(wrh_ref[...], h2_lo, nt, preferred_element_type=F32))


def _merge(att, z_tm, gates, x, mod, g_post, g_pre_ffn, w_br_mla, w_br_s5, w_out, wr_hi, wr_lo):
    bn, l, d = x.shape
    tm = 256
    nl = l // tm
    row = lambda b, i: (b * nl + i, 0)
    const = lambda b, i: (0, 0)
    one = pl.Buffered(1)
    return pl.pallas_call(
        _merge_kernel,
        grid=(bn, nl),
        in_specs=[pl.BlockSpec((tm, MLA_HEADS * V_HEAD), row),
                  pl.BlockSpec((tm, S5_CH), row),
                  pl.BlockSpec((tm, d), lambda b, i: (b * nl + i, 0)),
                  pl.BlockSpec((tm, d), lambda b, i: (b * nl + i, 1)),
                  pl.BlockSpec((1, tm, d), lambda b, i: (b, i, 0)),
                  pl.BlockSpec((1, 6, d), lambda b, i: (b, 0, 0)),
                  pl.BlockSpec((1, d), const),
                  pl.BlockSpec((1, d), const),
                  pl.BlockSpec((MLA_HEADS * V_HEAD, d), const, pipeline_mode=one),
                  pl.BlockSpec((S5_CH, d), const, pipeline_mode=one),
                  pl.BlockSpec((d, d), const, pipeline_mode=one),
                  pl.BlockSpec((N_EXPERTS, d), const, pipeline_mode=one),
                  pl.BlockSpec((N_EXPERTS, d), const, pipeline_mode=one)],
        out_specs=[pl.BlockSpec((1, tm, d), lambda b, i: (b, i, 0)),
                   pl.BlockSpec((1, tm, d), lambda b, i: (b, i, 0)),
                   pl.BlockSpec((tm * SLAB, LANES), row),
                   pl.BlockSpec((N_EXPERTS, tm), lambda b, i: (0, b * nl + i))],
        out_shape=[jax.ShapeDtypeStruct((bn, l, d), F32),
                   jax.ShapeDtypeStruct((bn, l, d), BF16),
                   jax.ShapeDtypeStruct((bn * l * SLAB, LANES), F32),
                   jax.ShapeDtypeStruct((N_EXPERTS, bn * l), F32)],
        compiler_params=_cparams(("arbitrary", "arbitrary")),
        name="merge",
    )(att, z_tm, gates, gates, x, mod, g_post.reshape(1, d), g_pre_ffn.reshape(1, d),
      w_br_mla, w_br_s5, w_out, wr_hi, wr_lo)


def _route_kernel(lg_ref, b_ref, ti_ref, tw_ref, cnt_ref):
    ng = N_EXPERT_GROUPS
    gsz = N_EXPERTS // ng
    lg = lg_ref[...]
    t = lg.shape[-1]
    sc = jax.nn.sigmoid(lg)
    sel = sc + b_ref[...]
    ninf = -jnp.inf
    i_in = lax.broadcasted_iota(jnp.int32, (ng, gsz, t), 1).astype(F32)
    m1 = jnp.max(sel, axis=1, keepdims=True)
    idx1 = jnp.min(jnp.where(sel == m1, i_in, float(gsz)), axis=1, keepdims=True)
    m2 = jnp.max(jnp.where(i_in == idx1, ninf, sel), axis=1, keepdims=True)
    gs = m1 + m2
    g_i = lax.broadcasted_iota(jnp.int32, (ng, 1, t), 0).astype(F32)
    picked = jnp.zeros((ng, 1, t), F32)
    cur = gs
    for _ in range(TOPK_GROUPS):
        m = jnp.max(cur, axis=0, keepdims=True)
        idx = jnp.min(jnp.where(cur == m, g_i, float(ng)), axis=0, keepdims=True)
        hit = g_i == idx
        picked = jnp.where(hit, 1.0, picked)
        cur = jnp.where(hit, ninf, cur)
    cand = jnp.where(picked > 0.5, sel, ninf)
    e_i = lax.broadcasted_iota(jnp.int32, (ng, gsz, t), 0).astype(F32) * float(gsz) + i_in
    ws = []
    hits = jnp.zeros((ng, gsz, t), F32)
    for r in range(TOP_K):
        m = jnp.max(jnp.max(cand, axis=1, keepdims=True), axis=0, keepdims=True)
        idx = jnp.min(jnp.min(jnp.where(cand == m, e_i, float(N_EXPERTS)), axis=1, keepdims=True),
                      axis=0, keepdims=True)
        hit = e_i == idx
        w = jnp.sum(jnp.sum(jnp.where(hit, sc, 0.0), axis=1, keepdims=True), axis=0, keepdims=True)
        ti_ref[r:r + 1, :] = idx[0].astype(jnp.int32)
        ws.append(w[0])
        hits = jnp.where(hit, 1.0, hits)
        cand = jnp.where(hit, ninf, cand)

    @pl.when(pl.program_id(0) == 0)
    def _():
        cnt_ref[...] = jnp.zeros(cnt_ref.shape, F32)

    cnt_ref[...] += jnp.sum(hits, axis=2, keepdims=True)
    tot = ws[0]
    for w in ws[1:]:
        tot = tot + w
    for r in range(TOP_K):
        tw_ref[r:r + 1, :] = ws[r] / tot * ROUTED_SCALE
    for r in range(TOP_K, SUBLANES):
        ti_ref[r:r + 1, :] = jnp.zeros((1, t), jnp.int32)
        tw_ref[r:r + 1, :] = jnp.zeros((1, t), F32)


def _route(logits_t, router_bias):
    n = logits_t.shape[1]
    ng = N_EXPERT_GROUPS
    gsz = N_EXPERTS // ng
    tn = 1024
    return pl.pallas_call(
        _route_kernel,
        grid=(n // tn,),
        in_specs=[pl.BlockSpec((ng, gsz, tn), lambda i: (0, 0, i)),
                  pl.BlockSpec((ng, gsz, 1), lambda i: (0, 0, 0))],
        out_specs=[pl.BlockSpec((SUBLANES, tn), lambda i: (0, i)),
                   pl.BlockSpec((SUBLANES, tn), lambda i: (0, i)),
                   pl.BlockSpec((ng, gsz, 1), lambda i: (0, 0, 0))],
        out_shape=[jax.ShapeDtypeStruct((SUBLANES, n), jnp.int32),
                   jax.ShapeDtypeStruct((SUBLANES, n), F32),
                   jax.ShapeDtypeStruct((ng, gsz, 1), F32)],
        compiler_params=_cparams(("arbitrary",)),
        name="route",
    )(logits_t.reshape(ng, gsz, n), router_bias.reshape(ng, gsz, 1))


def _moe_kernel(blk_e_ref, nused_ref, nvalid_ref, nxt_e_ref, wslot_ref, t_ref, tn_ref, a_ref, h_hbm,
                wg_hbm, wu_hbm, wd_hbm, y_hbm, xbuf, ybuf, wg_buf, wu_buf, wd_buf, wg_s, wu_s, wd_s,
                gsem, wsem, wtsem):
    j = pl.program_id(0)
    n_used = nused_ref[0]
    slot = lax.rem(j, 2)
    nv = nvalid_ref[j]
    nv_prev = nvalid_ref[jnp.maximum(j - 1, 0)]

    def slab(ref, row0):
        return ref.at[pl.ds(pl.multiple_of(row0, SUBLANES), SLAB), :]

    def start_gather(idx_ref, s):
        def body(r, _):
            pltpu.make_async_copy(slab(h_hbm, idx_ref[0, 0, r]), slab(xbuf.at[s], r * MOE_PITCH), gsem.at[s]).start()
            return 0
        lax.fori_loop(0, MOE_TB, body, 0, unroll=8)

    def wait_gather(s):
        rows = MOE_TB * SLAB
        pltpu.make_async_copy(h_hbm.at[pl.ds(0, rows), :], xbuf.at[s, pl.ds(0, rows), :], gsem.at[s]).wait()

    def weight_copies(expert, s):
        return (pltpu.make_async_copy(wg_hbm.at[expert], wg_buf.at[s], wtsem.at[s]),
                pltpu.make_async_copy(wu_hbm.at[expert], wu_buf.at[s], wtsem.at[s]),
                pltpu.make_async_copy(wd_hbm.at[expert], wd_buf.at[s], wtsem.at[s]))

    def wait_writes(count):
        p = MOE_TB
        while p >= 1:
            @pl.when((count & p) != 0)
            def _(p=p):
                rows = p * SLAB
                pltpu.make_async_copy(ybuf.at[pl.ds(0, rows), :], y_hbm.at[pl.ds(0, rows), :], wsem.at[0]).wait()
            p //= 2

    @pl.when(j < n_used)
    def _():
        @pl.when(j == 0)
        def _():
            start_gather(t_ref, 0)

        e = blk_e_ref[j]
        ws = wslot_ref[j]

        @pl.when(j == 0)
        def _():
            for cp in weight_copies(e, ws):
                cp.start()

        first = jnp.logical_or(j == 0, e != blk_e_ref[jnp.maximum(j - 1, 0)])

        @pl.when(first)
        def _():
            for cp in weight_copies(e, ws):
                cp.wait()
            wg_s[...] = wg_buf[ws].astype(BF16)
            wu_s[...] = wu_buf[ws].astype(BF16)
            wd_s[...] = wd_buf[ws].astype(BF16)

            @pl.when(nxt_e_ref[j] >= 0)
            def _():
                for cp in weight_copies(nxt_e_ref[j], 1 - ws):
                    cp.start()

        wait_gather(slot)

        @pl.when(j + 1 < n_used)
        def _():
            start_gather(tn_ref, 1 - slot)

        xs = xbuf.at[slot]
        x = jnp.concatenate([xs[pl.ds(s, MOE_TB, stride=MOE_PITCH), :] for s in range(SLAB)], axis=1).astype(BF16)
        g = jnp.dot(x, wg_s[...], preferred_element_type=F32)
        u = jnp.dot(x, wu_s[...], preferred_element_type=F32)
        hm = (g * jax.nn.sigmoid(g) * u).astype(BF16)
        y = jnp.dot(hm, wd_s[...], preferred_element_type=F32)

        @pl.when(j > 0)
        def _():
            wait_writes(nv_prev)

        for s in range(SLAB):
            ybuf[pl.ds(s, MOE_TB, stride=MOE_PITCH), :] = y[:, s * LANES:(s + 1) * LANES]

        def start_write(r):
            pltpu.make_async_copy(slab(ybuf, r * MOE_PITCH), slab(y_hbm, a_ref[0, 0, r]), wsem.at[0]).start()

        def wgroup(g8, _):
            for q in range(SUBLANES):
                start_write(g8 * SUBLANES + q)
            return 0

        def wtail(r, _):
            start_write(r)
            return 0

        n_groups = lax.shift_right_logical(nv, 3)
        lax.fori_loop(0, n_groups, wgroup, 0)
        lax.fori_loop(n_groups * SUBLANES, nv, wtail, 0)

        @pl.when(j == n_used - 1)
        def _():
            wait_writes(nv)


def _moe(h2s, topi, counts, w_gate, w_up, w_down):
    n = h2s.shape[0] // SLAB
    d = D_MODEL
    nk = n * TOP_K
    tb = MOE_TB
    n_blocks = nk // tb + N_EXPERTS
    flat_e = topi[:TOP_K].reshape(nk)
    _, order = lax.sort_key_val(flat_e, jnp.arange(nk, dtype=jnp.int32))
    start = jnp.cumsum(counts) - counts
    padded = (counts + tb - 1) // tb * tb
    pad_end = jnp.cumsum(padded)
    pad_start = pad_end - padded
    n_used = (pad_end[-1] // tb).astype(jnp.int32).reshape(1)
    blk_p0 = jnp.arange(n_blocks, dtype=jnp.int32) * tb
    blk_e = jnp.minimum(jnp.sum((pad_end[None, :] <= blk_p0[:, None]).astype(jnp.int32), axis=1), N_EXPERTS - 1)
    blk_off = blk_p0 - pad_start[blk_e]
    nvalid = jnp.clip(counts[blk_e] - blk_off, 0, tb).astype(jnp.int32)
    rank = (start[blk_e] + blk_off)[:, None] + jnp.arange(tb, dtype=jnp.int32)[None, :]
    a3 = order[jnp.clip(rank, 0, nk - 1)].reshape(n_blocks, 1, tb)
    arow = a3 * SLAB
    trow = (a3 % n) * SLAB
    used = counts > 0
    e_ids = jnp.arange(N_EXPERTS, dtype=jnp.int32)
    later = jnp.where(used, e_ids, N_EXPERTS)
    nxt = jnp.concatenate([lax.cummin(later[::-1])[::-1][1:], jnp.full((1,), N_EXPERTS, jnp.int32)])
    nxt_e = jnp.where(nxt < N_EXPERTS, nxt, -1)[blk_e].astype(jnp.int32)
    wslot = ((jnp.cumsum(used.astype(jnp.int32)) - 1) % 2)[blk_e].astype(jnp.int32)

    smem_blk = lambda f: pl.BlockSpec((1, 1, tb), f, memory_space=pltpu.SMEM)
    any_spec = pl.BlockSpec(memory_space=pl.ANY)
    grid_spec = pltpu.PrefetchScalarGridSpec(
        num_scalar_prefetch=5,
        grid=(n_blocks,),
        in_specs=[smem_blk(lambda j, *_: (j, 0, 0)),
                  smem_blk(lambda j, *_: (jnp.minimum(j + 1, n_blocks - 1), 0, 0)),
                  smem_blk(lambda j, *_: (j, 0, 0)),
                  any_spec, any_spec, any_spec, any_spec],
        out_specs=any_spec,
        scratch_shapes=[pltpu.VMEM((2, tb * MOE_PITCH, LANES), F32),
                        pltpu.VMEM((tb * MOE_PITCH, LANES), F32),
                        pltpu.VMEM((2, d, D_EXPERT), F32),
                        pltpu.VMEM((2, d, D_EXPERT), F32),
                        pltpu.VMEM((2, D_EXPERT, d), F32),
                        pltpu.VMEM((d, D_EXPERT), BF16),
                        pltpu.VMEM((d, D_EXPERT), BF16),
                        pltpu.VMEM((D_EXPERT, d), BF16),
                        pltpu.SemaphoreType.DMA((2,)),
                        pltpu.SemaphoreType.DMA((1,)),
                        pltpu.SemaphoreType.DMA((2,))],
    )
    return pl.pallas_call(
        _moe_kernel,
        grid_spec=grid_spec,
        out_shape=jax.ShapeDtypeStruct((TOP_K * n * SLAB, LANES), F32),
        compiler_params=_cparams(("arbitrary",)),
        name="moe",
    )(blk_e, n_used, nvalid, nxt_e, wslot, trow, trow, arow, h2s, w_gate, w_up, w_down)


def _final_kernel(h_ref, y0, y1, y2, y3, y4, y5, tw_ref, x1_ref, mod_ref, g_ref, wg_ref, wu_ref, wd_ref, o_ref):
    h = h_ref[...]
    tm = h.shape[0]
    g = jnp.dot(h, wg_ref[...], preferred_element_type=F32)
    u = jnp.dot(h, wu_ref[...], preferred_element_type=F32)
    hm = (g * jax.nn.sigmoid(g) * u).astype(BF16)
    ffn = jnp.dot(hm, wd_ref[...], preferred_element_type=F32)
    tw = tw_ref[...]
    for k, y in enumerate((y0, y1, y2, y3, y4, y5)):
        yk = jnp.concatenate([y[pl.ds(s, tm, stride=SLAB), :] for s in range(SLAB)], axis=1)
        ffn = ffn + yk * tw[:, k:k + 1]
    o_ref[0] = x1_ref[0] + mod_ref[0, 5:6, :] * _rms(ffn, g_ref[...])


def _final(h2, y_flat, topw_t, x1, mod, g_post_ffn, w_sg, w_su, w_sd):
    bn, l, d = x1.shape
    n = bn * l
    tm = 256
    nl = l // tm
    const = lambda b, i: (0, 0)
    one = pl.Buffered(1)
    y_specs = [pl.BlockSpec((tm * SLAB, LANES), functools.partial(lambda b, i, k: (k * (n // tm) + b * nl + i, 0), k=k))
               for k in range(TOP_K)]
    return pl.pallas_call(
        _final_kernel,
        grid=(bn, nl),
        in_specs=[pl.BlockSpec((tm, d), lambda b, i: (b * nl + i, 0))] + y_specs + [
            pl.BlockSpec((tm, SUBLANES), lambda b, i: (b * nl + i, 0)),
            pl.BlockSpec((1, tm, d), lambda b, i: (b, i, 0)),
            pl.BlockSpec((1, 6, d), lambda b, i: (b, 0, 0)),
            pl.BlockSpec((1, d), const),
            pl.BlockSpec((d, D_EXPERT), const, pipeline_mode=one),
            pl.BlockSpec((d, D_EXPERT), const, pipeline_mode=one),
            pl.BlockSpec((D_EXPERT, d), const, pipeline_mode=one)],
        out_specs=pl.BlockSpec((1, tm, d), lambda b, i: (b, i, 0)),
        out_shape=jax.ShapeDtypeStruct((bn, l, d), F32),
        compiler_params=_cparams(("arbitrary", "arbitrary")),
        name="final",
    )(h2, *([y_flat] * TOP_K), topw_t, x1, mod, g_post_ffn.reshape(1, d), w_sg, w_su, w_sd)


def _layer(x, c, positions, w_ada, b_ada, g_pre_mix, g_post_mix, g_pre_ffn, g_post_ffn, w_in, g_q, g_kv,
           w_uq, w_uk, w_uv, a_re, a_im, log_dt, b_re, b_im, c_re, c_im, d_skip, w_glu, w_br_mla, w_br_s5,
           w_out, w_router, router_bias, w_exp_gate, w_exp_up, w_exp_down, w_sh_gate, w_sh_up, w_sh_down):
    bn, l, d = x.shape
    n = bn * l
    mod = _adaln(c, w_ada, b_ada).reshape(bn, 6, d)

    o_kpe = Q_LORA + KV_LORA
    o_u = o_kpe + QK_ROPE
    o_g = o_u + S5_CH
    w_ug = _regroup(w_in, o_u, S5_CH + 2 * d)
    w_q = jnp.pad(w_uq.reshape(Q_LORA, MLA_HEADS, QK_NOPE + QK_ROPE),
                  ((0, 0), (0, 0), (0, QK_PAD - QK_NOPE - QK_ROPE))).reshape(Q_LORA, MLA_HEADS * QK_PAD).astype(BF16)
    wr_t = w_router.T
    wr_hi = wr_t.astype(BF16)
    wr_lo = (wr_t - wr_hi.astype(F32)).astype(BF16)

    rope_c, rope_s1, rope_s2 = _rope_tables(positions)
    chunk_id = positions // CHUNK

    h = _prenorm(x, mod, g_pre_mix, 0, 1).reshape(n, d)
    lat = _mm(h, w_in, 0, o_kpe, F32, 1024, 512, name="mm_lat")
    kpe = _mm(h, w_in, o_kpe, LANES, F32, 1024, LANES, name="mm_kpe")
    gates = _mm(h, w_ug, S5_CH, 2 * d, BF16, 1024, 1024, act="sigmoid", name="mm_gates")
    u6 = _mm_u(h, w_ug, 0, bn, l)

    q, k, v = _qkvproj(lat, g_q, g_kv, w_q, w_uk.astype(BF16), w_uv.astype(BF16), kpe, rope_c, rope_s1, rope_s2)
    att = _attention(q, k, v, chunk_id, bn, l)

    wb, ab, wc, dd = _s5_params(a_re, a_im, log_dt, b_re, b_im, c_re, c_im, d_skip)
    y6 = _s5(u6, wb, ab, wc, dd, l)
    z = _glu(y6, w_glu.astype(BF16), bn, l).reshape(n, S5_CH)

    x1, h2b, h2s, logits_t = _merge(att, z, gates, x, mod, g_post_mix, g_pre_ffn, w_br_mla.astype(BF16),
                                    w_br_s5.astype(BF16), w_out.astype(BF16), wr_hi, wr_lo)

    topi, topw, cnt = _route(logits_t, router_bias)
    y_slabs = _moe(h2s, topi, cnt.reshape(N_EXPERTS).astype(jnp.int32), w_exp_gate, w_exp_up, w_exp_down)
    return _final(h2b.reshape(n, d), y_slabs, topw.T, x1, mod, g_post_ffn, w_sh_gate.astype(BF16),
                  w_sh_up.astype(BF16), w_sh_down.astype(BF16))


def kernel(x, c, positions, w_ada, b_ada, g_pre_mix, g_post_mix, g_pre_ffn, g_post_ffn, w_in, g_q, g_kv, w_uq, w_uk, w_uv, a_re, a_im, log_dt, b_re, b_im, c_re, c_im, d_skip, w_glu, w_br_mla, w_br_s5, w_out, w_router, router_bias, w_exp_gate, w_exp_up, w_exp_down, w_sh_gate, w_sh_up, w_sh_down):
    depth = w_ada.shape[0]
    for li in range(depth):
        x = _layer(x, c, positions, w_ada[li], b_ada[li], g_pre_mix[li], g_post_mix[li], g_pre_ffn[li],
                   g_post_ffn[li], w_in[li], g_q[li], g_kv[li], w_uq[li], w_uk[li], w_uv[li], a_re[li], a_im[li],
                   log_dt[li], b_re[li], b_im[li], c_re[li], c_im[li], d_skip[li], w_glu[li], w_br_mla[li],
                   w_br_s5[li], w_out[li], w_router[li], router_bias[li], w_exp_gate[li], w_exp_up[li],
                   w_exp_down[li], w_sh_gate[li], w_sh_up[li], w_sh_down[li])
    return x
```

```python
import functools

import jax
import jax.numpy as jnp
from jax import lax
from jax.experimental import pallas as pl
from jax.experimental.pallas import tpu as pltpu

F32 = jnp.float32
BF16 = jnp.bfloat16

D_MODEL = 2048
CHUNK = 64
EPS = 1e-6
MLA_HEADS = 8
QK_NOPE = 128
QK_ROPE = 64
V_HEAD = 128
Q_LORA = 512
KV_LORA = 512
ROPE_THETA = 10000.0
S5_CH = 1024
S5_GROUP = 16
S5_GROUPS = S5_CH // S5_GROUP
S5_STATE = 64
N_EXPERTS = 64
TOP_K = 6
N_EXPERT_GROUPS = 8
TOPK_GROUPS = 4
D_EXPERT = 512
ROUTED_SCALE = 2.5

LANES = 128
SUBLANES = 8
QK_PAD = 2 * LANES
VMEM_LIMIT = 56 * 1024 * 1024
NEG = -1e30
LOG2E = 1.4426950408889634

S5_GB = LANES // S5_GROUP
S5_NBLK = S5_CH // LANES
S5_PAIRS = S5_NBLK // 2
S5_LANES = S5_GB * S5_STATE
S5_TC = 256
S5_RB = 512

ATT_TQ = 256
ATT_TK = 256
MOE_TB = 256
SLAB = D_MODEL // LANES
MOE_PITCH = 24


def _cparams(sem):
    return pltpu.CompilerParams(dimension_semantics=sem, vmem_limit_bytes=VMEM_LIMIT)


def _rms(x, g):
    return x * lax.rsqrt(jnp.mean(x * x, axis=-1, keepdims=True) + EPS) * g


def _adaln_kernel(c_ref, w_ref, b_ref, o_ref):
    c = c_ref[...]
    a = (c * jax.nn.sigmoid(c)).astype(BF16)
    o_ref[...] = jnp.dot(a, w_ref[...].astype(BF16), preferred_element_type=F32) + b_ref[...]


def _adaln(c, w, b):
    bn, d = c.shape
    n = w.shape[1]
    tn = 1024
    return pl.pallas_call(
        _adaln_kernel,
        grid=(n // tn,),
        in_specs=[pl.BlockSpec((bn, d), lambda j: (0, 0)),
                  pl.BlockSpec((d, tn), lambda j: (0, j)),
                  pl.BlockSpec((1, tn), lambda j: (0, j))],
        out_specs=pl.BlockSpec((bn, tn), lambda j: (0, j)),
        out_shape=jax.ShapeDtypeStruct((bn, n), F32),
        compiler_params=_cparams(("arbitrary",)),
        name="adaln",
    )(c, w, b.reshape(1, n))


def _prenorm_kernel(x_ref, mod_ref, g_ref, o_ref, *, sh_row, sc_row):
    y = _rms(x_ref[0], g_ref[...])
    o_ref[0] = (y * (1.0 + mod_ref[0, sc_row:sc_row + 1, :]) + mod_ref[0, sh_row:sh_row + 1, :]).astype(o_ref.dtype)


def _prenorm(x, mod, g, sh_row, sc_row):
    bn, l, d = x.shape
    tl = 512
    return pl.pallas_call(
        functools.partial(_prenorm_kernel, sh_row=sh_row, sc_row=sc_row),
        grid=(bn, l // tl),
        in_specs=[pl.BlockSpec((1, tl, d), lambda b, i: (b, i, 0)),
                  pl.BlockSpec((1, 6, d), lambda b, i: (b, 0, 0)),
                  pl.BlockSpec((1, d), lambda b, i: (0, 0))],
        out_specs=pl.BlockSpec((1, tl, d), lambda b, i: (b, i, 0)),
        out_shape=jax.ShapeDtypeStruct((bn, l, d), BF16),
        compiler_params=_cparams(("arbitrary", "arbitrary")),
        name="prenorm",
    )(x, mod, g.reshape(1, d))


def _mm_kernel(a_ref, w_ref, o_ref, *w_scr, act):
    if w_scr:
        @pl.when(pl.program_id(1) == 0)
        def _():
            w_scr[0][...] = w_ref[...].astype(BF16)
        w = w_scr[0][...]
    else:
        w = w_ref[...]
    acc = jnp.dot(a_ref[...], w, preferred_element_type=F32)
    if act == "sigmoid":
        acc = jax.nn.sigmoid(acc)
    o_ref[...] = acc.astype(o_ref.dtype)


def _w_spec(w, layer, rows, cols, index_map):
    if w.ndim == 2:
        return pl.BlockSpec((rows, cols), index_map)
    return pl.BlockSpec((None, rows, cols), lambda *g: (layer,) + tuple(index_map(*g)))


def _mm(a, w, col0, n, out_dtype, tm, tn, act=None, name="mm", layer=0):
    m, k = a.shape
    j0 = col0 // tn
    return pl.pallas_call(
        functools.partial(_mm_kernel, act=act),
        grid=(n // tn, m // tm),
        in_specs=[pl.BlockSpec((tm, k), lambda j, i: (i, 0)),
                  _w_spec(w, layer, k, tn, lambda j, i: (0, j + j0))],
        out_specs=pl.BlockSpec((tm, tn), lambda j, i: (i, j)),
        out_shape=jax.ShapeDtypeStruct((m, n), out_dtype),
        scratch_shapes=[] if w.dtype == BF16 else [pltpu.VMEM((k, tn), BF16)],
        compiler_params=_cparams(("arbitrary", "arbitrary")),
        name=name,
    )(a, w)


def _regroup_kernel(a_ref, b_ref, o_ref, *, shift):
    w = jnp.concatenate([a_ref[...], b_ref[...]], axis=1)
    o_ref[...] = w[:, shift:shift + o_ref.shape[1]].astype(o_ref.dtype)


def _regroup(w, layer, col0, n):
    k = w.shape[-2]
    tr, tw = 512, 1024
    shift = col0 % LANES
    c0 = col0 - shift
    assert c0 % tw == 0 and n % tw == 0
    return pl.pallas_call(
        functools.partial(_regroup_kernel, shift=shift),
        grid=(n // tw, k // tr),
        in_specs=[_w_spec(w, layer, tr, tw, lambda j, r: (r, j + c0 // tw)),
                  _w_spec(w, layer, tr, LANES, lambda j, r: (r, (c0 + (j + 1) * tw) // LANES))],
        out_specs=pl.BlockSpec((tr, tw), lambda j, r: (r, j)),
        out_shape=jax.ShapeDtypeStruct((k, n), BF16),
        compiler_params=_cparams(("arbitrary", "arbitrary")),
        name="regroup",
    )(w, w)


def _mm_u_kernel(a_ref, w_ref, o_ref):
    res = jnp.dot(a_ref[...], w_ref[...], preferred_element_type=F32)
    tm = res.shape[0]
    b = pl.program_id(1)
    for c in range(S5_NBLK):
        o_ref.at[c // 2][pl.ds(b * 2 + c % 2, tm, stride=SUBLANES), :] = res[:, c * LANES:(c + 1) * LANES]


def _mm_u(h, w, col0, bn, l):
    m, k = h.shape
    tm = 512
    nl = l // tm
    assert bn * 2 == SUBLANES and col0 % S5_CH == 0
    return pl.pallas_call(
        _mm_u_kernel,
        grid=(nl, bn),
        in_specs=[pl.BlockSpec((tm, k), lambda i, b: (b * nl + i, 0)),
                  pl.BlockSpec((k, S5_CH), lambda i, b: (0, col0 // S5_CH))],
        out_specs=pl.BlockSpec((S5_PAIRS, tm * SUBLANES, LANES), lambda i, b: (0, i, 0)),
        out_shape=jax.ShapeDtypeStruct((S5_PAIRS, l * SUBLANES, LANES), F32),
        compiler_params=_cparams(("arbitrary", "arbitrary")),
        name="mm_u",
    )(h, w)


def _rope_tab_kernel(pos_ref, k_ref, c_ref, s1_ref, s2_ref):
    ang = pos_ref[...].astype(F32) * k_ref[0:1, :]
    s = jnp.sin(ang)
    c_ref[...] = jnp.cos(ang) * k_ref[1:2, :]
    s1_ref[...] = s * k_ref[2:3, :]
    s2_ref[...] = s * k_ref[3:4, :]


def _rope_tables(positions):
    n = positions.size
    half = QK_ROPE // 2
    inv_freq = ROPE_THETA ** (-jnp.arange(half, dtype=F32) / half)
    zh, oh = jnp.zeros((half,), F32), jnp.ones((half,), F32)
    z2 = jnp.zeros((LANES - QK_ROPE,), F32)
    rows = [jnp.concatenate([inv_freq, inv_freq, z2]), jnp.concatenate([oh, oh, z2]),
            jnp.concatenate([-oh, zh, z2]), jnp.concatenate([zh, oh, z2])]
    consts = jnp.stack(rows + [jnp.zeros((LANES,), F32)] * (SUBLANES - len(rows)))
    tm = 1024
    tab = jax.ShapeDtypeStruct((n, LANES), F32)
    return pl.pallas_call(
        _rope_tab_kernel,
        grid=(n // tm,),
        in_specs=[pl.BlockSpec((tm, 1), lambda i: (i, 0)),
                  pl.BlockSpec((SUBLANES, LANES), lambda i: (0, 0))],
        out_specs=[pl.BlockSpec((tm, LANES), lambda i: (i, 0))] * 3,
        out_shape=[tab, tab, tab],
        compiler_params=_cparams(("arbitrary",)),
        name="rope_tables",
    )(positions.reshape(n, 1), consts)


def _rope_tile(t, c_ref, s1_ref, s2_ref):
    return (t * c_ref[...] + pltpu.roll(t, LANES - QK_ROPE // 2, 1) * s1_ref[...]
            + pltpu.roll(t, QK_ROPE // 2, 1) * s2_ref[...])


def _qkvproj_kernel(lat_ref, gq_ref, gkv_ref, wq_ref, wk_ref, wv_ref, kpe_ref, c_ref, s1_ref, s2_ref,
                    q_ref, k_ref, v_ref, *, scale):
    lat = lat_ref[...]
    qn = _rms(lat[:, :Q_LORA], gq_ref[...]).astype(BF16)
    cn = _rms(lat[:, Q_LORA:], gkv_ref[...]).astype(BF16)
    q = jnp.dot(qn, wq_ref[...], preferred_element_type=F32)
    kn = jnp.dot(cn, wk_ref[...], preferred_element_type=F32)
    v_ref[...] = jnp.dot(cn, wv_ref[...], preferred_element_type=F32).astype(v_ref.dtype)
    kt = _rope_tile(kpe_ref[...], c_ref, s1_ref, s2_ref).astype(k_ref.dtype)
    for h in range(MLA_HEADS):
        o = h * QK_PAD
        q_ref[:, o:o + LANES] = (q[:, o:o + LANES] * scale).astype(q_ref.dtype)
        qt = _rope_tile(q[:, o + LANES:o + QK_PAD], c_ref, s1_ref, s2_ref)
        q_ref[:, o + LANES:o + QK_PAD] = (qt * scale).astype(q_ref.dtype)
        k_ref[:, o:o + LANES] = kn[:, h * QK_NOPE:(h + 1) * QK_NOPE].astype(k_ref.dtype)
        k_ref[:, o + LANES:o + QK_PAD] = kt


def _qkvproj(lat, g_q, g_kv, w_q, w_k, w_v, kpe, rope_c, rope_s1, rope_s2):
    n = lat.shape[0]
    tm = 512
    row = lambda i: (i, 0)
    const = lambda i: (0, 0)
    tab = pl.BlockSpec((tm, LANES), row)
    return pl.pallas_call(
        functools.partial(_qkvproj_kernel, scale=(QK_NOPE + QK_ROPE) ** -0.5 * LOG2E),
        grid=(n // tm,),
        in_specs=[pl.BlockSpec((tm, Q_LORA + KV_LORA), row),
                  pl.BlockSpec((1, Q_LORA), const),
                  pl.BlockSpec((1, KV_LORA), const),
                  pl.BlockSpec((Q_LORA, MLA_HEADS * QK_PAD), const),
                  pl.BlockSpec((KV_LORA, MLA_HEADS * QK_NOPE), const),
                  pl.BlockSpec((KV_LORA, MLA_HEADS * V_HEAD), const),
                  tab, tab, tab, tab],
        out_specs=[pl.BlockSpec((tm, MLA_HEADS * QK_PAD), row),
                   pl.BlockSpec((tm, MLA_HEADS * QK_PAD), row),
                   pl.BlockSpec((tm, MLA_HEADS * V_HEAD), row)],
        out_shape=[jax.ShapeDtypeStruct((n, MLA_HEADS * QK_PAD), BF16),
                   jax.ShapeDtypeStruct((n, MLA_HEADS * QK_PAD), BF16),
                   jax.ShapeDtypeStruct((n, MLA_HEADS * V_HEAD), BF16)],
        compiler_params=_cparams(("arbitrary",)),
        name="qkvproj",
    )(lat, g_q.reshape(1, Q_LORA), g_kv.reshape(1, KV_LORA), w_q, w_k, w_v, kpe, rope_c, rope_s1, rope_s2)


def _attn_kernel(lo_ref, hi_ref, q_ref, k_ref, v_ref, qc_ref, kc_ref, o_ref, m_scr, l_scr, acc_scr, *, nq):
    b = pl.program_id(0)
    i = pl.program_id(1)
    qc = qc_ref[...]
    m_scr[...] = jnp.full(m_scr.shape, NEG, F32)
    l_scr[...] = jnp.zeros(l_scr.shape, F32)
    acc_scr[...] = jnp.zeros(acc_scr.shape, F32)

    def make_body(masked):
        def body(j, _):
            off = pl.multiple_of(j * ATT_TK, ATT_TK)
            if masked:
                mask = kc_ref[j] <= qc
            for h in range(MLA_HEADS):
                q = q_ref[:, h * QK_PAD:(h + 1) * QK_PAD]
                k = k_ref[pl.ds(off, ATT_TK), h * QK_PAD:(h + 1) * QK_PAD]
                s = lax.dot_general(q, k, (((1,), (1,)), ((), ())), preferred_element_type=F32)
                if masked:
                    s = jnp.where(mask, s, NEG)
                m_old = m_scr[h]
                m_new = jnp.maximum(m_old, jnp.max(s, axis=-1, keepdims=True))
                p = jnp.exp2(s - jnp.concatenate([m_new] * (ATT_TK // LANES), axis=1))
                alpha = jnp.exp2(m_old - m_new)
                l_scr[h] = alpha * l_scr[h] + jnp.sum(p, axis=-1, keepdims=True)
                v = v_ref[pl.ds(off, ATT_TK), h * V_HEAD:(h + 1) * V_HEAD]
                acc_scr[h] = alpha * acc_scr[h] + jnp.dot(p.astype(BF16), v, preferred_element_type=F32)
                m_scr[h] = m_new
            return 0
        return body

    lo = lo_ref[b * nq + i]
    lax.fori_loop(0, lo, make_body(False), 0)
    lax.fori_loop(lo, hi_ref[b * nq + i], make_body(True), 0)
    for h in range(MLA_HEADS):
        o_ref[:, h * V_HEAD:(h + 1) * V_HEAD] = (acc_scr[h] / l_scr[h]).astype(o_ref.dtype)


def _attention(q, k, v, chunk_id, bn, l):
    nq = l // ATT_TQ
    nk = l // ATT_TK
    q_max = jnp.max(chunk_id.reshape(bn, nq, ATT_TQ), axis=-1)
    k_min = jnp.min(chunk_id.reshape(bn, nk, ATT_TK), axis=-1)
    needed = k_min[:, None, :] <= q_max[:, :, None]
    hi = jnp.max(jnp.where(needed, jnp.arange(1, nk + 1, dtype=jnp.int32), 0), axis=-1).reshape(bn * nq)
    q_min = jnp.min(chunk_id.reshape(bn, nq, ATT_TQ), axis=-1)
    k_max = jnp.max(chunk_id.reshape(bn, nk, ATT_TK), axis=-1)
    full = k_max[:, None, :] <= q_min[:, :, None]
    lo = jnp.min(jnp.where(full, nk, jnp.arange(nk, dtype=jnp.int32)), axis=-1).astype(jnp.int32).reshape(bn * nq)
    hi = jnp.maximum(hi, lo)
    qc = chunk_id.reshape(bn * l, 1)
    kc = chunk_id.reshape(bn * nk, 1, ATT_TK)
    grid_spec = pltpu.PrefetchScalarGridSpec(
        num_scalar_prefetch=2,
        grid=(bn, nq),
        in_specs=[pl.BlockSpec((ATT_TQ, MLA_HEADS * QK_PAD), lambda b, i, *_: (b * nq + i, 0)),
                  pl.BlockSpec((l, MLA_HEADS * QK_PAD), lambda b, i, *_: (b, 0)),
                  pl.BlockSpec((l, MLA_HEADS * V_HEAD), lambda b, i, *_: (b, 0)),
                  pl.BlockSpec((ATT_TQ, 1), lambda b, i, *_: (b * nq + i, 0)),
                  pl.BlockSpec((nk, 1, ATT_TK), lambda b, i, *_: (b, 0, 0))],
        out_specs=pl.BlockSpec((ATT_TQ, MLA_HEADS * V_HEAD), lambda b, i, *_: (b * nq + i, 0)),
        scratch_shapes=[pltpu.VMEM((MLA_HEADS, ATT_TQ, LANES), F32),
                        pltpu.VMEM((MLA_HEADS, ATT_TQ, LANES), F32),
                        pltpu.VMEM((MLA_HEADS, ATT_TQ, V_HEAD), F32)],
    )
    return pl.pallas_call(
        functools.partial(_attn_kernel, nq=nq),
        grid_spec=grid_spec,
        out_shape=jax.ShapeDtypeStruct((bn * l, MLA_HEADS * V_HEAD), BF16),
        compiler_params=_cparams(("arbitrary", "arbitrary")),
        name="attention",
    )(lo, hi, q, k, v, qc, kc)


def _s5_kernel(u_ref, wb_ref, a_ref, wc_ref, d_ref, o_ref, x_scr, st_scr):
    rows = S5_TC * SUBLANES
    nsub = rows // S5_RB

    @pl.when(pl.program_id(1) == 0)
    def _():
        st_scr[...] = jnp.zeros_like(st_scr)

    even = (lax.broadcasted_iota(jnp.int32, (S5_RB, 1), 0) & 1) == 0
    half = S5_RB // 2
    nre = S5_LANES // LANES
    uv = u_ref.at[0]

    def mm_in(r, _):
        off = pl.multiple_of(r * S5_RB, S5_RB)
        for s in range(2):
            us = uv[pl.ds(off + s, half, stride=2), :].astype(BF16)
            out = jnp.dot(us, wb_ref[0, :, s * 2 * S5_LANES:(s + 1) * 2 * S5_LANES], preferred_element_type=F32)
            for c in range(2 * nre):
                x_scr.at[c][pl.ds(off + s, half, stride=2), :] = out[:, c * LANES:(c + 1) * LANES]
        return 0

    lax.fori_loop(0, nsub, mm_in, 0)

    a_c = [a_ref[0, :, c * LANES:(c + 1) * LANES] for c in range(2 * nre)]

    def step(t, carry):
        off = pl.multiple_of(t * SUBLANES, SUBLANES)
        new = [None] * (2 * nre)
        for c in range(nre):
            xr, xi = carry[c], carry[nre + c]
            ar, ai = a_c[c], a_c[nre + c]
            nr = ar * xr - ai * xi + x_scr[c, pl.ds(off, SUBLANES), :]
            ni = ar * xi + ai * xr + x_scr[nre + c, pl.ds(off, SUBLANES), :]
            x_scr[c, pl.ds(off, SUBLANES), :] = nr
            x_scr[nre + c, pl.ds(off, SUBLANES), :] = ni
            new[c], new[nre + c] = nr, ni
        return tuple(new)

    init = tuple(st_scr[:, c * LANES:(c + 1) * LANES] for c in range(2 * nre))
    fin = lax.fori_loop(0, S5_TC, step, init, unroll=8)
    for c in range(2 * nre):
        st_scr[:, c * LANES:(c + 1) * LANES] = fin[c]

    d = jnp.concatenate([d_ref[0]] * (S5_RB // SUBLANES), axis=0)

    def mm_out(r, _):
        off = pl.multiple_of(r * S5_RB, S5_RB)
        x = jnp.concatenate([x_scr[c, pl.ds(off, S5_RB), :] for c in range(2 * nre)], axis=1).astype(BF16)
        out = jnp.dot(x, wc_ref[0], preferred_element_type=F32)
        y = jnp.where(even, out[:, :LANES], out[:, LANES:]) + d * u_ref[0, pl.ds(off, S5_RB), :]
        o_ref[0, pl.ds(off, S5_RB), :] = jax.nn.gelu(y).astype(o_ref.dtype)
        return 0

    lax.fori_loop(0, nsub, mm_out, 0)


def _s5_params(a_re, a_im, log_dt, b_re, b_im, c_re, c_im, d_skip):
    step = jnp.exp(log_dt)[:, None]
    mag = jnp.exp(a_re * step)
    abar_re, abar_im = mag * jnp.cos(a_im * step), mag * jnp.sin(a_im * step)
    den = a_re * a_re + a_im * a_im
    nr, ni = abar_re - 1.0, abar_im
    f_re, f_im = (nr * a_re + ni * a_im) / den, (ni * a_re - nr * a_im) / den
    bbar_re = f_re[..., None] * b_re - f_im[..., None] * b_im
    bbar_im = f_re[..., None] * b_im + f_im[..., None] * b_re
    eye = jnp.eye(S5_GB, dtype=F32)
    bb = jnp.stack([bbar_re, bbar_im]).reshape(2, S5_NBLK, S5_GB, S5_STATE, S5_GROUP)
    wb = jnp.einsum('ab,rjapc->jacrbp', eye, bb).reshape(S5_NBLK, LANES, 2 * S5_LANES)
    wb = wb.reshape(S5_PAIRS, 2, LANES, 2 * S5_LANES).transpose(0, 2, 1, 3).reshape(S5_PAIRS, LANES, 4 * S5_LANES)
    cc = jnp.stack([c_re, -c_im]).reshape(2, S5_NBLK, S5_GB, S5_GROUP, S5_STATE)
    wc = jnp.einsum('ab,rjacp->jrapbc', eye, cc).reshape(S5_NBLK, 2 * S5_LANES, LANES)
    wc = wc.reshape(S5_PAIRS, 2, 2 * S5_LANES, LANES).transpose(0, 2, 1, 3).reshape(S5_PAIRS, 2 * S5_LANES, 2 * LANES)
    ab = jnp.concatenate([abar_re.reshape(S5_NBLK, S5_LANES), abar_im.reshape(S5_NBLK, S5_LANES)], axis=1)
    ab = jnp.tile(ab.reshape(S5_PAIRS, 2, 2 * S5_LANES), (1, SUBLANES // 2, 1))
    dd = jnp.tile(d_skip.reshape(S5_PAIRS, 2, LANES), (1, SUBLANES // 2, 1))
    return wb.astype(BF16), ab, wc.astype(BF16), dd


def _s5(u6, wb, ab, wc, dd, l):
    rows = S5_TC * SUBLANES
    return pl.pallas_call(
        _s5_kernel,
        grid=(S5_PAIRS, l // S5_TC),
        in_specs=[pl.BlockSpec((1, rows, LANES), lambda k, c: (k, c, 0)),
                  pl.BlockSpec((1, LANES, 4 * S5_LANES), lambda k, c: (k, 0, 0)),
                  pl.BlockSpec((1, SUBLANES, 2 * S5_LANES), lambda k, c: (k, 0, 0)),
                  pl.BlockSpec((1, 2 * S5_LANES, 2 * LANES), lambda k, c: (k, 0, 0)),
                  pl.BlockSpec((1, SUBLANES, LANES), lambda k, c: (k, 0, 0))],
        out_specs=pl.BlockSpec((1, rows, LANES), lambda k, c: (k, c, 0)),
        out_shape=jax.ShapeDtypeStruct(u6.shape, F32),
        scratch_shapes=[pltpu.VMEM((2 * S5_LANES // LANES, rows, LANES), F32),
                        pltpu.VMEM((SUBLANES, 2 * S5_LANES), F32)],
        compiler_params=_cparams(("arbitrary", "arbitrary")),
        name="s5",
    )(u6, wb, ab, wc, dd)


def _glu_kernel(y_ref, w_ref, o_ref, z_scr, *, bn):
    b = pl.program_id(1)
    rows = z_scr.shape[1]

    @pl.when(b == 0)
    def _():
        y = jnp.concatenate([y_ref.at[k][pl.ds(s, rows, stride=2), :] for k in range(S5_PAIRS) for s in range(2)],
                            axis=1)
        g = jnp.dot(y.astype(BF16), w_ref[...], preferred_element_type=F32)
        z = y * jax.nn.sigmoid(g)
        for c in range(S5_NBLK):
            z_scr[c] = z[:, c * LANES:(c + 1) * LANES]

    o_ref[0] = jnp.concatenate([z_scr.at[c][pl.ds(b, rows // bn, stride=bn), :] for c in range(S5_NBLK)],
                               axis=1).astype(o_ref.dtype)


def _glu(y6, w_glu, bn, l):
    tt = 256
    rows = tt * bn
    return pl.pallas_call(
        functools.partial(_glu_kernel, bn=bn),
        grid=(l // tt, bn),
        in_specs=[pl.BlockSpec((S5_PAIRS, rows * 2, LANES), lambda i, b: (0, i, 0)),
                  pl.BlockSpec((S5_CH, S5_CH), lambda i, b: (0, 0))],
        out_specs=pl.BlockSpec((1, tt, S5_CH), lambda i, b: (b, i, 0)),
        out_shape=jax.ShapeDtypeStruct((bn, l, S5_CH), BF16),
        scratch_shapes=[pltpu.VMEM((S5_NBLK, rows, LANES), F32)],
        compiler_params=_cparams(("arbitrary", "arbitrary")),
        name="glu",
    )(y6, w_glu)


def _merge_kernel(att_ref, z_ref, gm_ref, gs_ref, x_ref, mod_ref, gpost_ref, gpre_ref,
                  wbm_ref, wbs_ref, wo_ref, wrh_ref, wrl_ref, x1_ref, h2b_ref, h2s_ref, lg_ref):
    ym = jnp.dot(att_ref[...], wbm_ref[...], preferred_element_type=F32)
    ys = jnp.dot(z_ref[...], wbs_ref[...], preferred_element_type=F32)
    mixed_in = (gm_ref[...].astype(F32) * ym + gs_ref[...].astype(F32) * ys).astype(BF16)
    mixed = jnp.dot(mixed_in, wo_ref[...], preferred_element_type=F32)
    x1 = x_ref[0] + mod_ref[0, 2:3, :] * _rms(mixed, gpost_ref[...])
    x1_ref[0] = x1
    h2 = _rms(x1, gpre_ref[...]) * (1.0 + mod_ref[0, 4:5, :]) + mod_ref[0, 3:4, :]
    h2_hi = h2.astype(BF16)
    h2b_ref[0] = h2_hi
    tm = h2.shape[0]
    for s in range(SLAB):
        h2s_ref[pl.ds(s, tm, stride=SLAB), :] = h2[:, s * LANES:(s + 1) * LANES]
    h2_lo = (h2 - h2_hi.astype(F32)).astype(BF16)
    nt = (((1,), (1,)), ((), ()))
    lg_ref[...] = (lax.dot_general(wrh_ref[...], h2_hi, nt, preferred_element_type=F32)
                   + lax.dot_general(wrl_ref[...], h2_hi, nt, preferred_element_type=F32)
                   + lax.dot_general(wrh_ref[...], h2_lo, nt, preferred_element_type=F32))


def _merge(att, z_tm, gates, x, mod, g_post, g_pre_ffn, w_br_mla, w_br_s5, w_out, wr_hi, wr_lo):
    bn, l, d = x.shape
    tm = 256
    nl = l // tm
    row = lambda b, i: (b * nl + i, 0)
    const = lambda b, i: (0, 0)
    one = pl.Buffered(1)
    return pl.pallas_call(
        _merge_kernel,
        grid=(bn, nl),
        in_specs=[pl.BlockSpec((tm, MLA_HEADS * V_HEAD), row),
                  pl.BlockSpec((tm, S5_CH), row),
                  pl.BlockSpec((tm, d), lambda b, i: (b * nl + i, 0)),
                  pl.BlockSpec((tm, d), lambda b, i: (b * nl + i, 1)),
                  pl.BlockSpec((1, tm, d), lambda b, i: (b, i, 0)),
                  pl.BlockSpec((1, 6, d), lambda b, i: (b, 0, 0)),
                  pl.BlockSpec((1, d), const),
                  pl.BlockSpec((1, d), const),
                  pl.BlockSpec((MLA_HEADS * V_HEAD, d), const, pipeline_mode=one),
                  pl.BlockSpec((S5_CH, d), const, pipeline_mode=one),
                  pl.BlockSpec((d, d), const, pipeline_mode=one),
                  pl.BlockSpec((N_EXPERTS, d), const, pipeline_mode=one),
                  pl.BlockSpec((N_EXPERTS, d), const, pipeline_mode=one)],
        out_specs=[pl.BlockSpec((1, tm, d), lambda b, i: (b, i, 0)),
                   pl.BlockSpec((1, tm, d), lambda b, i: (b, i, 0)),
                   pl.BlockSpec((tm * SLAB, LANES), row),
                   pl.BlockSpec((N_EXPERTS, tm), lambda b, i: (0, b * nl + i))],
        out_shape=[jax.ShapeDtypeStruct((bn, l, d), F32),
                   jax.ShapeDtypeStruct((bn, l, d), BF16),
                   jax.ShapeDtypeStruct((bn * l * SLAB, LANES), F32),
                   jax.ShapeDtypeStruct((N_EXPERTS, bn * l), F32)],
        compiler_params=_cparams(("arbitrary", "arbitrary")),
        name="merge",
    )(att, z_tm, gates, gates, x, mod, g_post.reshape(1, d), g_pre_ffn.reshape(1, d),
      w_br_mla, w_br_s5, w_out, wr_hi, wr_lo)


def _route_kernel(lg_ref, b_ref, ti_ref, tw_ref, cnt_ref):
    ng = N_EXPERT_GROUPS
    gsz = N_EXPERTS // ng
    lg = lg_ref[...]
    t = lg.shape[-1]
    sc = jax.nn.sigmoid(lg)
    sel = sc + b_ref[...]
    ninf = -jnp.inf
    i_in = lax.broadcasted_iota(jnp.int32, (ng, gsz, t), 1).astype(F32)
    m1 = jnp.max(sel, axis=1, keepdims=True)
    idx1 = jnp.min(jnp.where(sel == m1, i_in, float(gsz)), axis=1, keepdims=True)
    m2 = jnp.max(jnp.where(i_in == idx1, ninf, sel), axis=1, keepdims=True)
    gs = m1 + m2
    g_i = lax.broadcasted_iota(jnp.int32, (ng, 1, t), 0).astype(F32)
    picked = jnp.zeros((ng, 1, t), F32)
    cur = gs
    for _ in range(TOPK_GROUPS):
        m = jnp.max(cur, axis=0, keepdims=True)
        idx = jnp.min(jnp.where(cur == m, g_i, float(ng)), axis=0, keepdims=True)
        hit = g_i == idx
        picked = jnp.where(hit, 1.0, picked)
        cur = jnp.where(hit, ninf, cur)
    cand = jnp.where(picked > 0.5, sel, ninf)
    e_i = lax.broadcasted_iota(jnp.int32, (ng, gsz, t), 0).astype(F32) * float(gsz) + i_in
    ws = []
    hits = jnp.zeros((ng, gsz, t), F32)
    for r in range(TOP_K):
        m = jnp.max(jnp.max(cand, axis=1, keepdims=True), axis=0, keepdims=True)
        idx = jnp.min(jnp.min(jnp.where(cand == m, e_i, float(N_EXPERTS)), axis=1, keepdims=True),
                      axis=0, keepdims=True)
        hit = e_i == idx
        w = jnp.sum(jnp.sum(jnp.where(hit, sc, 0.0), axis=1, keepdims=True), axis=0, keepdims=True)
        ti_ref[r:r + 1, :] = idx[0].astype(jnp.int32)
        ws.append(w[0])
        hits = jnp.where(hit, 1.0, hits)
        cand = jnp.where(hit, ninf, cand)

    @pl.when(pl.program_id(0) == 0)
    def _():
        cnt_ref[...] = jnp.zeros(cnt_ref.shape, F32)

    cnt_ref[...] += jnp.sum(hits, axis=2, keepdims=True)
    tot = ws[0]
    for w in ws[1:]:
        tot = tot + w
    for r in range(TOP_K):
        tw_ref[r:r + 1, :] = ws[r] / tot * ROUTED_SCALE
    for r in range(TOP_K, SUBLANES):
        ti_ref[r:r + 1, :] = jnp.zeros((1, t), jnp.int32)
        tw_ref[r:r + 1, :] = jnp.zeros((1, t), F32)


def _route(logits_t, router_bias):
    n = logits_t.shape[1]
    ng = N_EXPERT_GROUPS
    gsz = N_EXPERTS // ng
    tn = 1024
    return pl.pallas_call(
        _route_kernel,
        grid=(n // tn,),
        in_specs=[pl.BlockSpec((ng, gsz, tn), lambda i: (0, 0, i)),
                  pl.BlockSpec((ng, gsz, 1), lambda i: (0, 0, 0))],
        out_specs=[pl.BlockSpec((SUBLANES, tn), lambda i: (0, i)),
                   pl.BlockSpec((SUBLANES, tn), lambda i: (0, i)),
                   pl.BlockSpec((ng, gsz, 1), lambda i: (0, 0, 0))],
        out_shape=[jax.ShapeDtypeStruct((SUBLANES, n), jnp.int32),
                   jax.ShapeDtypeStruct((SUBLANES, n), F32),
                   jax.ShapeDtypeStruct((ng, gsz, 1), F32)],
        compiler_params=_cparams(("arbitrary",)),
        name="route",
    )(logits_t.reshape(ng, gsz, n), router_bias.reshape(ng, gsz, 1))


def _moe_kernel(blk_e_ref, nused_ref, nvalid_ref, nxt_e_ref, wslot_ref, t_ref, tn_ref, a_ref, h_hbm,
                wg_hbm, wu_hbm, wd_hbm, y_hbm, xbuf, ybuf, wg_buf, wu_buf, wd_buf, wg_s, wu_s, wd_s,
                gsem, wsem, wtsem):
    j = pl.program_id(0)
    n_used = nused_ref[0]
    slot = lax.rem(j, 2)
    nv = nvalid_ref[j]
    nv_prev = nvalid_ref[jnp.maximum(j - 1, 0)]

    def slab(ref, row0):
        return ref.at[pl.ds(pl.multiple_of(row0, SUBLANES), SLAB), :]

    def start_gather(idx_ref, s):
        def body(r, _):
            pltpu.make_async_copy(slab(h_hbm, idx_ref[0, 0, r]), slab(xbuf.at[s], r * MOE_PITCH), gsem.at[s]).start()
            return 0
        lax.fori_loop(0, MOE_TB, body, 0, unroll=8)

    def wait_gather(s):
        rows = MOE_TB * SLAB
        pltpu.make_async_copy(h_hbm.at[pl.ds(0, rows), :], xbuf.at[s, pl.ds(0, rows), :], gsem.at[s]).wait()

    def weight_copies(expert, s):
        return (pltpu.make_async_copy(wg_hbm.at[expert], wg_buf.at[s], wtsem.at[s]),
                pltpu.make_async_copy(wu_hbm.at[expert], wu_buf.at[s], wtsem.at[s]),
                pltpu.make_async_copy(wd_hbm.at[expert], wd_buf.at[s], wtsem.at[s]))

    def wait_writes(count):
        p = MOE_TB
        while p >= 1:
            @pl.when((count & p) != 0)
            def _(p=p):
                rows = p * SLAB
                pltpu.make_async_copy(ybuf.at[pl.ds(0, rows), :], y_hbm.at[pl.ds(0, rows), :], wsem.at[0]).wait()
            p //= 2

    @pl.when(j < n_used)
    def _():
        @pl.when(j == 0)
        def _():
            start_gather(t_ref, 0)

        e = blk_e_ref[j]
        ws = wslot_ref[j]

        @pl.when(j == 0)
        def _():
            for cp in weight_copies(e, ws):
                cp.start()

        first = jnp.logical_or(j == 0, e != blk_e_ref[jnp.maximum(j - 1, 0)])

        @pl.when(first)
        def _():
            for cp in weight_copies(e, ws):
                cp.wait()
            wg_s[...] = wg_buf[ws].astype(BF16)
            wu_s[...] = wu_buf[ws].astype(BF16)
            wd_s[...] = wd_buf[ws].astype(BF16)

            @pl.when(nxt_e_ref[j] >= 0)
            def _():
                for cp in weight_copies(nxt_e_ref[j], 1 - ws):
                    cp.start()

        wait_gather(slot)

        @pl.when(j + 1 < n_used)
        def _():
            start_gather(tn_ref, 1 - slot)

        xs = xbuf.at[slot]
        x = jnp.concatenate([xs[pl.ds(s, MOE_TB, stride=MOE_PITCH), :] for s in range(SLAB)], axis=1).astype(BF16)
        g = jnp.dot(x, wg_s[...], preferred_element_type=F32)
        u = jnp.dot(x, wu_s[...], preferred_element_type=F32)
        hm = (g * jax.nn.sigmoid(g) * u).astype(BF16)
        y = jnp.dot(hm, wd_s[...], preferred_element_type=F32)

        @pl.when(j > 0)
        def _():
            wait_writes(nv_prev)

        for s in range(SLAB):
            ybuf[pl.ds(s, MOE_TB, stride=MOE_PITCH), :] = y[:, s * LANES:(s + 1) * LANES]

        def start_write(r):
            pltpu.make_async_copy(slab(ybuf, r * MOE_PITCH), slab(y_hbm, a_ref[0, 0, r]), wsem.at[0]).start()

        def wgroup(g8, _):
            for q in range(SUBLANES):
                start_write(g8 * SUBLANES + q)
            return 0

        def wtail(r, _):
            start_write(r)
            return 0

        n_groups = lax.shift_right_logical(nv, 3)
        lax.fori_loop(0, n_groups, wgroup, 0)
        lax.fori_loop(n_groups * SUBLANES, nv, wtail, 0)

        @pl.when(j == n_used - 1)
        def _():
            wait_writes(nv)


def _moe(h2s, topi, counts, w_gate, w_up, w_down):
    n = h2s.shape[0] // SLAB
    d = D_MODEL
    nk = n * TOP_K
    tb = MOE_TB
    n_blocks = nk // tb + N_EXPERTS
    flat_e = topi[:TOP_K].reshape(nk)
    _, order = lax.sort_key_val(flat_e, jnp.arange(nk, dtype=jnp.int32))
    start = jnp.cumsum(counts) - counts
    padded = (counts + tb - 1) // tb * tb
    pad_end = jnp.cumsum(padded)
    pad_start = pad_end - padded
    n_used = (pad_end[-1] // tb).astype(jnp.int32).reshape(1)
    blk_p0 = jnp.arange(n_blocks, dtype=jnp.int32) * tb
    blk_e = jnp.minimum(jnp.sum((pad_end[None, :] <= blk_p0[:, None]).astype(jnp.int32), axis=1), N_EXPERTS - 1)
    blk_off = blk_p0 - pad_start[blk_e]
    nvalid = jnp.clip(counts[blk_e] - blk_off, 0, tb).astype(jnp.int32)
    rank = (start[blk_e] + blk_off)[:, None] + jnp.arange(tb, dtype=jnp.int32)[None, :]
    a3 = order[jnp.clip(rank, 0, nk - 1)].reshape(n_blocks, 1, tb)
    arow = a3 * SLAB
    trow = (a3 % n) * SLAB
    used = counts > 0
    e_ids = jnp.arange(N_EXPERTS, dtype=jnp.int32)
    later = jnp.where(used, e_ids, N_EXPERTS)
    nxt = jnp.concatenate([lax.cummin(later[::-1])[::-1][1:], jnp.full((1,), N_EXPERTS, jnp.int32)])
    nxt_e = jnp.where(nxt < N_EXPERTS, nxt, -1)[blk_e].astype(jnp.int32)
    wslot = ((jnp.cumsum(used.astype(jnp.int32)) - 1) % 2)[blk_e].astype(jnp.int32)

    smem_blk = lambda f: pl.BlockSpec((1, 1, tb), f, memory_space=pltpu.SMEM)
    any_spec = pl.BlockSpec(memory_space=pl.ANY)
    grid_spec = pltpu.PrefetchScalarGridSpec(
        num_scalar_prefetch=5,
        grid=(n_blocks,),
        in_specs=[smem_blk(lambda j, *_: (j, 0, 0)),
                  smem_blk(lambda j, *_: (jnp.minimum(j + 1, n_blocks - 1), 0, 0)),
                  smem_blk(lambda j, *_: (j, 0, 0)),
                  any_spec, any_spec, any_spec, any_spec],
        out_specs=any_spec,
        scratch_shapes=[pltpu.VMEM((2, tb * MOE_PITCH, LANES), F32),
                        pltpu.VMEM((tb * MOE_PITCH, LANES), F32),
                        pltpu.VMEM((2, d, D_EXPERT), F32),
                        pltpu.VMEM((2, d, D_EXPERT), F32),
                        pltpu.VMEM((2, D_EXPERT, d), F32),
                        pltpu.VMEM((d, D_EXPERT), BF16),
                        pltpu.VMEM((d, D_EXPERT), BF16),
                        pltpu.VMEM((D_EXPERT, d), BF16),
                        pltpu.SemaphoreType.DMA((2,)),
                        pltpu.SemaphoreType.DMA((1,)),
                        pltpu.SemaphoreType.DMA((2,))],
    )
    return pl.pallas_call(
        _moe_kernel,
        grid_spec=grid_spec,
        out_shape=jax.ShapeDtypeStruct((TOP_K * n * SLAB, LANES), F32),
        compiler_params=_cparams(("arbitrary",)),
        name="moe",
    )(blk_e, n_used, nvalid, nxt_e, wslot, trow, trow, arow, h2s, w_gate, w_up, w_down)


def _final_kernel(h_ref, y0, y1, y2, y3, y4, y5, tw_ref, x1_ref, mod_ref, g_ref, wg_ref, wu_ref, wd_ref, o_ref):
    h = h_ref[...]
    tm = h.shape[0]
    g = jnp.dot(h, wg_ref[...], preferred_element_type=F32)
    u = jnp.dot(h, wu_ref[...], preferred_element_type=F32)
    hm = (g * jax.nn.sigmoid(g) * u).astype(BF16)
    ffn = jnp.dot(hm, wd_ref[...], preferred_element_type=F32)
    tw = tw_ref[...]
    for k, y in enumerate((y0, y1, y2, y3, y4, y5)):
        yk = jnp.concatenate([y[pl.ds(s, tm, stride=SLAB), :] for s in range(SLAB)], axis=1)
        ffn = ffn + yk * tw[:, k:k + 1]
    o_ref[0] = x1_ref[0] + mod_ref[0, 5:6, :] * _rms(ffn, g_ref[...])


def _final(h2, y_flat, topw_t, x1, mod, g_post_ffn, w_sg, w_su, w_sd):
    bn, l, d = x1.shape
    n = bn * l
    tm = 256
    nl = l // tm
    const = lambda b, i: (0, 0)
    one = pl.Buffered(1)
    y_specs = [pl.BlockSpec((tm * SLAB, LANES), functools.partial(lambda b, i, k: (k * (n // tm) + b * nl + i, 0), k=k))
               for k in range(TOP_K)]
    return pl.pallas_call(
        _final_kernel,
        grid=(bn, nl),
        in_specs=[pl.BlockSpec((tm, d), lambda b, i: (b * nl + i, 0))] + y_specs + [
            pl.BlockSpec((tm, SUBLANES), lambda b, i: (b * nl + i, 0)),
            pl.BlockSpec((1, tm, d), lambda b, i: (b, i, 0)),
            pl.BlockSpec((1, 6, d), lambda b, i: (b, 0, 0)),
            pl.BlockSpec((1, d), const),
            pl.BlockSpec((d, D_EXPERT), const, pipeline_mode=one),
            pl.BlockSpec((d, D_EXPERT), const, pipeline_mode=one),
            pl.BlockSpec((D_EXPERT, d), const, pipeline_mode=one)],
        out_specs=pl.BlockSpec((1, tm, d), lambda b, i: (b, i, 0)),
        out_shape=jax.ShapeDtypeStruct((bn, l, d), F32),
        compiler_params=_cparams(("arbitrary", "arbitrary")),
        name="final",
    )(h2, *([y_flat] * TOP_K), topw_t, x1, mod, g_post_ffn.reshape(1, d), w_sg, w_su, w_sd)


def _layer(li, x, c, positions, w_ada, b_ada, g_pre_mix, g_post_mix, g_pre_ffn, g_post_ffn, w_in, g_q, g_kv,
           w_uq, w_uk, w_uv, a_re, a_im, log_dt, b_re, b_im, c_re, c_im, d_skip, w_glu, w_br_mla, w_br_s5,
           w_out, w_router, router_bias, w_exp_gate, w_exp_up, w_exp_down, w_sh_gate, w_sh_up, w_sh_down):
    bn, l, d = x.shape
    n = bn * l
    mod = _adaln(c, w_ada, b_ada).reshape(bn, 6, d)

    o_kpe = Q_LORA + KV_LORA
    o_u = o_kpe + QK_ROPE
    o_g = o_u + S5_CH
    w_ug = _regroup(w_in, li, o_u, S5_CH + 2 * d)
    w_q = jnp.pad(w_uq.reshape(Q_LORA, MLA_HEADS, QK_NOPE + QK_ROPE),
                  ((0, 0), (0, 0), (0, QK_PAD - QK_NOPE - QK_ROPE))).reshape(Q_LORA, MLA_HEADS * QK_PAD).astype(BF16)
    wr_t = w_router.T
    wr_hi = wr_t.astype(BF16)
    wr_lo = (wr_t - wr_hi.astype(F32)).astype(BF16)

    rope_c, rope_s1, rope_s2 = _rope_tables(positions)
    chunk_id = positions // CHUNK

    h = _prenorm(x, mod, g_pre_mix, 0, 1).reshape(n, d)
    lat = _mm(h, w_in, 0, o_kpe, F32, 1024, 512, name="mm_lat", layer=li)
    kpe = _mm(h, w_in, o_kpe, LANES, F32, 1024, LANES, name="mm_kpe", layer=li)
    gates = _mm(h, w_ug, S5_CH, 2 * d, BF16, 1024, 1024, act="sigmoid", name="mm_gates")
    u6 = _mm_u(h, w_ug, 0, bn, l)

    q, k, v = _qkvproj(lat, g_q, g_kv, w_q, w_uk.astype(BF16), w_uv.astype(BF16), kpe, rope_c, rope_s1, rope_s2)
    att = _attention(q, k, v, chunk_id, bn, l)

    wb, ab, wc, dd = _s5_params(a_re, a_im, log_dt, b_re, b_im, c_re, c_im, d_skip)
    y6 = _s5(u6, wb, ab, wc, dd, l)
    z = _glu(y6, w_glu.astype(BF16), bn, l).reshape(n, S5_CH)

    x1, h2b, h2s, logits_t = _merge(att, z, gates, x, mod, g_post_mix, g_pre_ffn, w_br_mla.astype(BF16),
                                    w_br_s5.astype(BF16), w_out.astype(BF16), wr_hi, wr_lo)

    topi, topw, cnt = _route(logits_t, router_bias)
    y_slabs = _moe(h2s, topi, cnt.reshape(N_EXPERTS).astype(jnp.int32), w_exp_gate, w_exp_up, w_exp_down)
    return _final(h2b.reshape(n, d), y_slabs, topw.T, x1, mod, g_post_ffn, w_sh_gate.astype(BF16),
                  w_sh_up.astype(BF16), w_sh_down.astype(BF16))


def kernel(x, c, positions, w_ada, b_ada, g_pre_mix, g_post_mix, g_pre_ffn, g_post_ffn, w_in, g_q, g_kv, w_uq, w_uk, w_uv, a_re, a_im, log_dt, b_re, b_im, c_re, c_im, d_skip, w_glu, w_br_mla, w_br_s5, w_out, w_router, router_bias, w_exp_gate, w_exp_up, w_exp_down, w_sh_gate, w_sh_up, w_sh_down):
    depth = w_ada.shape[0]
    for li in range(depth):
        x = _layer(li, x, c, positions, w_ada[li], b_ada[li], g_pre_mix[li], g_post_mix[li], g_pre_ffn[li],
                   g_post_ffn[li], w_in, g_q[li], g_kv[li], w_uq[li], w_uk[li], w_uv[li], a_re[li], a_im[li],
                   log_dt[li], b_re[li], b_im[li], c_re[li], c_im[li], d_skip[li], w_glu[li], w_br_mla[li],
                   w_br_s5[li], w_out[li], w_router[li], router_bias[li], w_exp_gate[li], w_exp_up[li],
                   w_exp_down[li], w_sh_gate[li], w_sh_up[li], w_sh_down[li])
    return x
```

```python
import functools

import jax
import jax.numpy as jnp
from jax import lax
from jax.experimental import pallas as pl
from jax.experimental.pallas import tpu as pltpu

F32 = jnp.float32
BF16 = jnp.bfloat16

D_MODEL = 2048
CHUNK = 64
EPS = 1e-6
MLA_HEADS = 8
QK_NOPE = 128
QK_ROPE = 64
V_HEAD = 128
Q_LORA = 512
KV_LORA = 512
ROPE_THETA = 10000.0
S5_CH = 1024
S5_GROUP = 16
S5_GROUPS = S5_CH // S5_GROUP
S5_STATE = 64
N_EXPERTS = 64
TOP_K = 6
N_EXPERT_GROUPS = 8
TOPK_GROUPS = 4
D_EXPERT = 512
ROUTED_SCALE = 2.5

LANES = 128
SUBLANES = 8
QK_PAD = 2 * LANES
VMEM_LIMIT = 56 * 1024 * 1024
NEG = -1e30
LOG2E = 1.4426950408889634

S5_GB = LANES // S5_GROUP
S5_NBLK = S5_CH // LANES
S5_PAIRS = S5_NBLK // 2
S5_LANES = S5_GB * S5_STATE
S5_TC = 256
S5_RB = 512

ATT_TQ = 256
ATT_TK = 256
MOE_TB = 256
SLAB = D_MODEL // LANES
MOE_PITCH = 24


def _cparams(sem):
    return pltpu.CompilerParams(dimension_semantics=sem, vmem_limit_bytes=VMEM_LIMIT)


def _rms(x, g):
    return x * lax.rsqrt(jnp.mean(x * x, axis=-1, keepdims=True) + EPS) * g


def _adaln_kernel(c_ref, w_ref, b_ref, o_ref):
    c = c_ref[...]
    a = (c * jax.nn.sigmoid(c)).astype(BF16)
    o_ref[...] = jnp.dot(a, w_ref[...].astype(BF16), preferred_element_type=F32) + b_ref[...]


def _adaln(c, w, b):
    bn, d = c.shape
    n = w.shape[1]
    tn = 1024
    return pl.pallas_call(
        _adaln_kernel,
        grid=(n // tn,),
        in_specs=[pl.BlockSpec((bn, d), lambda j: (0, 0)),
                  pl.BlockSpec((d, tn), lambda j: (0, j)),
                  pl.BlockSpec((1, tn), lambda j: (0, j))],
        out_specs=pl.BlockSpec((bn, tn), lambda j: (0, j)),
        out_shape=jax.ShapeDtypeStruct((bn, n), F32),
        compiler_params=_cparams(("arbitrary",)),
        name="adaln",
    )(c, w, b.reshape(1, n))


def _prenorm_kernel(x_ref, mod_ref, g_ref, o_ref, *, sh_row, sc_row):
    y = _rms(x_ref[0], g_ref[...])
    o_ref[0] = (y * (1.0 + mod_ref[0, sc_row:sc_row + 1, :]) + mod_ref[0, sh_row:sh_row + 1, :]).astype(o_ref.dtype)


def _prenorm(x, mod, g, sh_row, sc_row):
    bn, l, d = x.shape
    tl = 512
    return pl.pallas_call(
        functools.partial(_prenorm_kernel, sh_row=sh_row, sc_row=sc_row),
        grid=(bn, l // tl),
        in_specs=[pl.BlockSpec((1, tl, d), lambda b, i: (b, i, 0)),
                  pl.BlockSpec((1, 6, d), lambda b, i: (b, 0, 0)),
                  pl.BlockSpec((1, d), lambda b, i: (0, 0))],
        out_specs=pl.BlockSpec((1, tl, d), lambda b, i: (b, i, 0)),
        out_shape=jax.ShapeDtypeStruct((bn, l, d), BF16),
        compiler_params=_cparams(("arbitrary", "arbitrary")),
        name="prenorm",
    )(x, mod, g.reshape(1, d))


NT_DIMS = (((1,), (1,)), ((), ()))


def _wt_tile(wa_ref, wb_ref, shift, tn):
    if shift == 0:
        return wa_ref[...].astype(BF16)
    return jnp.concatenate([wa_ref[...], wb_ref[...]], axis=0)[shift:shift + tn].astype(BF16)


def _wt_specs(layer, row0, tn, k, jmap):
    shift = row0 % LANES
    c0 = row0 - shift
    assert c0 % tn == 0 and shift % SUBLANES == 0
    specs = [pl.BlockSpec((None, tn, k), lambda *g: (layer, jmap(*g) + c0 // tn, 0))]
    if shift:
        specs.append(pl.BlockSpec((None, LANES, k), lambda *g: (layer, (c0 + (jmap(*g) + 1) * tn) // LANES, 0)))
    return specs, shift


def _mmt_kernel(a_ref, wa_ref, *rest, act, shift):
    wb_ref = rest[0] if shift else None
    o_ref, w_scr = rest[-2:]

    @pl.when(pl.program_id(1) == 0)
    def _():
        w_scr[...] = _wt_tile(wa_ref, wb_ref, shift, w_scr.shape[0])

    acc = lax.dot_general(a_ref[...], w_scr[...], NT_DIMS, preferred_element_type=F32)
    if act == "sigmoid":
        acc = jax.nn.sigmoid(acc)
    o_ref[...] = acc.astype(o_ref.dtype)


def _mmt(a, w_t, layer, row0, n, out_dtype, tm, tn, act=None, name="mmt"):
    m, k = a.shape
    w_specs, shift = _wt_specs(layer, row0, tn, k, lambda j, i: j)
    return pl.pallas_call(
        functools.partial(_mmt_kernel, act=act, shift=shift),
        grid=(n // tn, m // tm),
        in_specs=[pl.BlockSpec((tm, k), lambda j, i: (i, 0))] + w_specs,
        out_specs=pl.BlockSpec((tm, tn), lambda j, i: (i, j)),
        out_shape=jax.ShapeDtypeStruct((m, n), out_dtype),
        scratch_shapes=[pltpu.VMEM((tn, k), BF16)],
        compiler_params=_cparams(("arbitrary", "arbitrary")),
        name=name,
    )(a, *([w_t] * len(w_specs)))


def _mm_u_kernel(a_ref, wa_ref, *rest, shift):
    wb_ref = rest[0] if shift else None
    o_ref, w_scr = rest[-2:]
    b = pl.program_id(1)

    @pl.when(jnp.logical_and(pl.program_id(0) == 0, b == 0))
    def _():
        w_scr[...] = _wt_tile(wa_ref, wb_ref, shift, w_scr.shape[0])

    res = lax.dot_general(a_ref[...], w_scr[...], NT_DIMS, preferred_element_type=F32)
    tm = res.shape[0]
    for c in range(S5_NBLK):
        o_ref.at[c // 2][pl.ds(b * 2 + c % 2, tm, stride=SUBLANES), :] = res[:, c * LANES:(c + 1) * LANES]


def _mm_u(h, w_t, layer, row0, bn, l):
    m, k = h.shape
    tm = 512
    nl = l // tm
    assert bn * 2 == SUBLANES
    w_specs, shift = _wt_specs(layer, row0, S5_CH, k, lambda i, b: 0)
    return pl.pallas_call(
        functools.partial(_mm_u_kernel, shift=shift),
        grid=(nl, bn),
        in_specs=[pl.BlockSpec((tm, k), lambda i, b: (b * nl + i, 0))] + w_specs,
        out_specs=pl.BlockSpec((S5_PAIRS, tm * SUBLANES, LANES), lambda i, b: (0, i, 0)),
        out_shape=jax.ShapeDtypeStruct((S5_PAIRS, l * SUBLANES, LANES), F32),
        scratch_shapes=[pltpu.VMEM((S5_CH, k), BF16)],
        compiler_params=_cparams(("arbitrary", "arbitrary")),
        name="mm_u",
    )(h, *([w_t] * len(w_specs)))


def _rope_tab_kernel(pos_ref, k_ref, c_ref, s1_ref, s2_ref):
    ang = pos_ref[...].astype(F32) * k_ref[0:1, :]
    s = jnp.sin(ang)
    c_ref[...] = jnp.cos(ang) * k_ref[1:2, :]
    s1_ref[...] = s * k_ref[2:3, :]
    s2_ref[...] = s * k_ref[3:4, :]


def _rope_tables(positions):
    n = positions.size
    half = QK_ROPE // 2
    inv_freq = ROPE_THETA ** (-jnp.arange(half, dtype=F32) / half)
    zh, oh = jnp.zeros((half,), F32), jnp.ones((half,), F32)
    z2 = jnp.zeros((LANES - QK_ROPE,), F32)
    rows = [jnp.concatenate([inv_freq, inv_freq, z2]), jnp.concatenate([oh, oh, z2]),
            jnp.concatenate([-oh, zh, z2]), jnp.concatenate([zh, oh, z2])]
    consts = jnp.stack(rows + [jnp.zeros((LANES,), F32)] * (SUBLANES - len(rows)))
    tm = 1024
    tab = jax.ShapeDtypeStruct((n, LANES), F32)
    return pl.pallas_call(
        _rope_tab_kernel,
        grid=(n // tm,),
        in_specs=[pl.BlockSpec((tm, 1), lambda i: (i, 0)),
                  pl.BlockSpec((SUBLANES, LANES), lambda i: (0, 0))],
        out_specs=[pl.BlockSpec((tm, LANES), lambda i: (i, 0))] * 3,
        out_shape=[tab, tab, tab],
        compiler_params=_cparams(("arbitrary",)),
        name="rope_tables",
    )(positions.reshape(n, 1), consts)


def _rope_tile(t, c_ref, s1_ref, s2_ref):
    return (t * c_ref[...] + pltpu.roll(t, LANES - QK_ROPE // 2, 1) * s1_ref[...]
            + pltpu.roll(t, QK_ROPE // 2, 1) * s2_ref[...])


def _qkvproj_kernel(lat_ref, gq_ref, gkv_ref, wq_ref, wk_ref, wv_ref, kpe_ref, c_ref, s1_ref, s2_ref,
                    q_ref, k_ref, v_ref, *, scale):
    lat = lat_ref[...]
    qn = _rms(lat[:, :Q_LORA], gq_ref[...]).astype(BF16)
    cn = _rms(lat[:, Q_LORA:], gkv_ref[...]).astype(BF16)
    q = jnp.dot(qn, wq_ref[...], preferred_element_type=F32)
    kn = jnp.dot(cn, wk_ref[...], preferred_element_type=F32)
    v_ref[...] = jnp.dot(cn, wv_ref[...], preferred_element_type=F32).astype(v_ref.dtype)
    kt = _rope_tile(kpe_ref[...], c_ref, s1_ref, s2_ref).astype(k_ref.dtype)
    for h in range(MLA_HEADS):
        o = h * QK_PAD
        q_ref[:, o:o + LANES] = (q[:, o:o + LANES] * scale).astype(q_ref.dtype)
        qt = _rope_tile(q[:, o + LANES:o + QK_PAD], c_ref, s1_ref, s2_ref)
        q_ref[:, o + LANES:o + QK_PAD] = (qt * scale).astype(q_ref.dtype)
        k_ref[:, o:o + LANES] = kn[:, h * QK_NOPE:(h + 1) * QK_NOPE].astype(k_ref.dtype)
        k_ref[:, o + LANES:o + QK_PAD] = kt


def _qkvproj(lat, g_q, g_kv, w_q, w_k, w_v, kpe, rope_c, rope_s1, rope_s2):
    n = lat.shape[0]
    tm = 512
    row = lambda i: (i, 0)
    const = lambda i: (0, 0)
    tab = pl.BlockSpec((tm, LANES), row)
    return pl.pallas_call(
        functools.partial(_qkvproj_kernel, scale=(QK_NOPE + QK_ROPE) ** -0.5 * LOG2E),
        grid=(n // tm,),
        in_specs=[pl.BlockSpec((tm, Q_LORA + KV_LORA), row),
                  pl.BlockSpec((1, Q_LORA), const),
                  pl.BlockSpec((1, KV_LORA), const),
                  pl.BlockSpec((Q_LORA, MLA_HEADS * QK_PAD), const),
                  pl.BlockSpec((KV_LORA, MLA_HEADS * QK_NOPE), const),
                  pl.BlockSpec((KV_LORA, MLA_HEADS * V_HEAD), const),
                  tab, tab, tab, tab],
        out_specs=[pl.BlockSpec((tm, MLA_HEADS * QK_PAD), row),
                   pl.BlockSpec((tm, MLA_HEADS * QK_PAD), row),
                   pl.BlockSpec((tm, MLA_HEADS * V_HEAD), row)],
        out_shape=[jax.ShapeDtypeStruct((n, MLA_HEADS * QK_PAD), BF16),
                   jax.ShapeDtypeStruct((n, MLA_HEADS * QK_PAD), BF16),
                   jax.ShapeDtypeStruct((n, MLA_HEADS * V_HEAD), BF16)],
        compiler_params=_cparams(("arbitrary",)),
        name="qkvproj",
    )(lat, g_q.reshape(1, Q_LORA), g_kv.reshape(1, KV_LORA), w_q, w_k, w_v, kpe, rope_c, rope_s1, rope_s2)


def _attn_kernel(lo_ref, hi_ref, q_ref, k_ref, v_ref, qc_ref, kc_ref, o_ref, m_scr, l_scr, acc_scr, *, nq):
    b = pl.program_id(0)
    i = pl.program_id(1)
    qc = qc_ref[...]
    m_scr[...] = jnp.full(m_scr.shape, NEG, F32)
    l_scr[...] = jnp.zeros(l_scr.shape, F32)
    acc_scr[...] = jnp.zeros(acc_scr.shape, F32)

    def make_body(masked):
        def body(j, _):
            off = pl.multiple_of(j * ATT_TK, ATT_TK)
            if masked:
                mask = kc_ref[j] <= qc
            for h in range(MLA_HEADS):
                q = q_ref[:, h * QK_PAD:(h + 1) * QK_PAD]
                k = k_ref[pl.ds(off, ATT_TK), h * QK_PAD:(h + 1) * QK_PAD]
                s = lax.dot_general(q, k, (((1,), (1,)), ((), ())), preferred_element_type=F32)
                if masked:
                    s = jnp.where(mask, s, NEG)
                m_old = m_scr[h]
                m_new = jnp.maximum(m_old, jnp.max(s, axis=-1, keepdims=True))
                p = jnp.exp2(s - jnp.concatenate([m_new] * (ATT_TK // LANES), axis=1))
                alpha = jnp.exp2(m_old - m_new)
                l_scr[h] = alpha * l_scr[h] + jnp.sum(p, axis=-1, keepdims=True)
                v = v_ref[pl.ds(off, ATT_TK), h * V_HEAD:(h + 1) * V_HEAD]
                acc_scr[h] = alpha * acc_scr[h] + jnp.dot(p.astype(BF16), v, preferred_element_type=F32)
                m_scr[h] = m_new
            return 0
        return body

    lo = lo_ref[b * nq + i]
    lax.fori_loop(0, lo, make_body(False), 0)
    lax.fori_loop(lo, hi_ref[b * nq + i], make_body(True), 0)
    for h in range(MLA_HEADS):
        o_ref[:, h * V_HEAD:(h + 1) * V_HEAD] = (acc_scr[h] / l_scr[h]).astype(o_ref.dtype)


def _attention(q, k, v, chunk_id, bn, l):
    nq = l // ATT_TQ
    nk = l // ATT_TK
    q_max = jnp.max(chunk_id.reshape(bn, nq, ATT_TQ), axis=-1)
    k_min = jnp.min(chunk_id.reshape(bn, nk, ATT_TK), axis=-1)
    needed = k_min[:, None, :] <= q_max[:, :, None]
    hi = jnp.max(jnp.where(needed, jnp.arange(1, nk + 1, dtype=jnp.int32), 0), axis=-1).reshape(bn * nq)
    q_min = jnp.min(chunk_id.reshape(bn, nq, ATT_TQ), axis=-1)
    k_max = jnp.max(chunk_id.reshape(bn, nk, ATT_TK), axis=-1)
    full = k_max[:, None, :] <= q_min[:, :, None]
    lo = jnp.min(jnp.where(full, nk, jnp.arange(nk, dtype=jnp.int32)), axis=-1).astype(jnp.int32).reshape(bn * nq)
    hi = jnp.maximum(hi, lo)
    qc = chunk_id.reshape(bn * l, 1)
    kc = chunk_id.reshape(bn * nk, 1, ATT_TK)
    grid_spec = pltpu.PrefetchScalarGridSpec(
        num_scalar_prefetch=2,
        grid=(bn, nq),
        in_specs=[pl.BlockSpec((ATT_TQ, MLA_HEADS * QK_PAD), lambda b, i, *_: (b * nq + i, 0)),
                  pl.BlockSpec((l, MLA_HEADS * QK_PAD), lambda b, i, *_: (b, 0)),
                  pl.BlockSpec((l, MLA_HEADS * V_HEAD), lambda b, i, *_: (b, 0)),
                  pl.BlockSpec((ATT_TQ, 1), lambda b, i, *_: (b * nq + i, 0)),
                  pl.BlockSpec((nk, 1, ATT_TK), lambda b, i, *_: (b, 0, 0))],
        out_specs=pl.BlockSpec((ATT_TQ, MLA_HEADS * V_HEAD), lambda b, i, *_: (b * nq + i, 0)),
        scratch_shapes=[pltpu.VMEM((MLA_HEADS, ATT_TQ, LANES), F32),
                        pltpu.VMEM((MLA_HEADS, ATT_TQ, LANES), F32),
                        pltpu.VMEM((MLA_HEADS, ATT_TQ, V_HEAD), F32)],
    )
    return pl.pallas_call(
        functools.partial(_attn_kernel, nq=nq),
        grid_spec=grid_spec,
        out_shape=jax.ShapeDtypeStruct((bn * l, MLA_HEADS * V_HEAD), BF16),
        compiler_params=_cparams(("arbitrary", "arbitrary")),
        name="attention",
    )(lo, hi, q, k, v, qc, kc)


def _s5_kernel(u_ref, wb_ref, a_ref, wc_ref, d_ref, o_ref, x_scr, st_scr):
    rows = S5_TC * SUBLANES
    nsub = rows // S5_RB

    @pl.when(pl.program_id(1) == 0)
    def _():
        st_scr[...] = jnp.zeros_like(st_scr)

    even = (lax.broadcasted_iota(jnp.int32, (S5_RB, 1), 0) & 1) == 0
    half = S5_RB // 2
    nre = S5_LANES // LANES
    uv = u_ref.at[0]

    def mm_in(r, _):
        off = pl.multiple_of(r * S5_RB, S5_RB)
        for s in range(2):
            us = uv[pl.ds(off + s, half, stride=2), :].astype(BF16)
            out = jnp.dot(us, wb_ref[0, :, s * 2 * S5_LANES:(s + 1) * 2 * S5_LANES], preferred_element_type=F32)
            for c in range(2 * nre):
                x_scr.at[c][pl.ds(off + s, half, stride=2), :] = out[:, c * LANES:(c + 1) * LANES]
        return 0

    lax.fori_loop(0, nsub, mm_in, 0)

    a_c = [a_ref[0, :, c * LANES:(c + 1) * LANES] for c in range(2 * nre)]

    def step(t, carry):
        off = pl.multiple_of(t * SUBLANES, SUBLANES)
        new = [None] * (2 * nre)
        for c in range(nre):
            xr, xi = carry[c], carry[nre + c]
            ar, ai = a_c[c], a_c[nre + c]
            nr = ar * xr - ai * xi + x_scr[c, pl.ds(off, SUBLANES), :]
            ni = ar * xi + ai * xr + x_scr[nre + c, pl.ds(off, SUBLANES), :]
            x_scr[c, pl.ds(off, SUBLANES), :] = nr
            x_scr[nre + c, pl.ds(off, SUBLANES), :] = ni
            new[c], new[nre + c] = nr, ni
        return tuple(new)

    init = tuple(st_scr[:, c * LANES:(c + 1) * LANES] for c in range(2 * nre))
    fin = lax.fori_loop(0, S5_TC, step, init, unroll=8)
    for c in range(2 * nre):
        st_scr[:, c * LANES:(c + 1) * LANES] = fin[c]

    d = jnp.concatenate([d_ref[0]] * (S5_RB // SUBLANES), axis=0)

    def mm_out(r, _):
        off = pl.multiple_of(r * S5_RB, S5_RB)
        x = jnp.concatenate([x_scr[c, pl.ds(off, S5_RB), :] for c in range(2 * nre)], axis=1).astype(BF16)
        out = jnp.dot(x, wc_ref[0], preferred_element_type=F32)
        y = jnp.where(even, out[:, :LANES], out[:, LANES:]) + d * u_ref[0, pl.ds(off, S5_RB), :]
        o_ref[0, pl.ds(off, S5_RB), :] = jax.nn.gelu(y).astype(o_ref.dtype)
        return 0

    lax.fori_loop(0, nsub, mm_out, 0)


def _s5_params(a_re, a_im, log_dt, b_re, b_im, c_re, c_im, d_skip):
    step = jnp.exp(log_dt)[:, None]
    mag = jnp.exp(a_re * step)
    abar_re, abar_im = mag * jnp.cos(a_im * step), mag * jnp.sin(a_im * step)
    den = a_re * a_re + a_im * a_im
    nr, ni = abar_re - 1.0, abar_im
    f_re, f_im = (nr * a_re + ni * a_im) / den, (ni * a_re - nr * a_im) / den
    bbar_re = f_re[..., None] * b_re - f_im[..., None] * b_im
    bbar_im = f_re[..., None] * b_im + f_im[..., None] * b_re
    eye = jnp.eye(S5_GB, dtype=F32)
    bb = jnp.stack([bbar_re, bbar_im]).reshape(2, S5_NBLK, S5_GB, S5_STATE, S5_GROUP)
    wb = jnp.einsum('ab,rjapc->jacrbp', eye, bb).reshape(S5_NBLK, LANES, 2 * S5_LANES)
    wb = wb.reshape(S5_PAIRS, 2, LANES, 2 * S5_LANES).transpose(0, 2, 1, 3).reshape(S5_PAIRS, LANES, 4 * S5_LANES)
    cc = jnp.stack([c_re, -c_im]).reshape(2, S5_NBLK, S5_GB, S5_GROUP, S5_STATE)
    wc = jnp.einsum('ab,rjacp->jrapbc', eye, cc).reshape(S5_NBLK, 2 * S5_LANES, LANES)
    wc = wc.reshape(S5_PAIRS, 2, 2 * S5_LANES, LANES).transpose(0, 2, 1, 3).reshape(S5_PAIRS, 2 * S5_LANES, 2 * LANES)
    ab = jnp.concatenate([abar_re.reshape(S5_NBLK, S5_LANES), abar_im.reshape(S5_NBLK, S5_LANES)], axis=1)
    ab = jnp.tile(ab.reshape(S5_PAIRS, 2, 2 * S5_LANES), (1, SUBLANES // 2, 1))
    dd = jnp.tile(d_skip.reshape(S5_PAIRS, 2, LANES), (1, SUBLANES // 2, 1))
    return wb.astype(BF16), ab, wc.astype(BF16), dd


def _s5(u6, wb, ab, wc, dd, l):
    rows = S5_TC * SUBLANES
    return pl.pallas_call(
        _s5_kernel,
        grid=(S5_PAIRS, l // S5_TC),
        in_specs=[pl.BlockSpec((1, rows, LANES), lambda k, c: (k, c, 0)),
                  pl.BlockSpec((1, LANES, 4 * S5_LANES), lambda k, c: (k, 0, 0)),
                  pl.BlockSpec((1, SUBLANES, 2 * S5_LANES), lambda k, c: (k, 0, 0)),
                  pl.BlockSpec((1, 2 * S5_LANES, 2 * LANES), lambda k, c: (k, 0, 0)),
                  pl.BlockSpec((1, SUBLANES, LANES), lambda k, c: (k, 0, 0))],
        out_specs=pl.BlockSpec((1, rows, LANES), lambda k, c: (k, c, 0)),
        out_shape=jax.ShapeDtypeStruct(u6.shape, F32),
        scratch_shapes=[pltpu.VMEM((2 * S5_LANES // LANES, rows, LANES), F32),
                        pltpu.VMEM((SUBLANES, 2 * S5_LANES), F32)],
        compiler_params=_cparams(("arbitrary", "arbitrary")),
        name="s5",
    )(u6, wb, ab, wc, dd)


def _glu_kernel(y_ref, w_ref, o_ref, z_scr, *, bn):
    b = pl.program_id(1)
    rows = z_scr.shape[1]

    @pl.when(b == 0)
    def _():
        y = jnp.concatenate([y_ref.at[k][pl.ds(s, rows, stride=2), :] for k in range(S5_PAIRS) for s in range(2)],
                            axis=1)
        g = jnp.dot(y.astype(BF16), w_ref[...], preferred_element_type=F32)
        z = y * jax.nn.sigmoid(g)
        for c in range(S5_NBLK):
            z_scr[c] = z[:, c * LANES:(c + 1) * LANES]

    o_ref[0] = jnp.concatenate([z_scr.at[c][pl.ds(b, rows // bn, stride=bn), :] for c in range(S5_NBLK)],
                               axis=1).astype(o_ref.dtype)


def _glu(y6, w_glu, bn, l):
    tt = 256
    rows = tt * bn
    return pl.pallas_call(
        functools.partial(_glu_kernel, bn=bn),
        grid=(l // tt, bn),
        in_specs=[pl.BlockSpec((S5_PAIRS, rows * 2, LANES), lambda i, b: (0, i, 0)),
                  pl.BlockSpec((S5_CH, S5_CH), lambda i, b: (0, 0))],
        out_specs=pl.BlockSpec((1, tt, S5_CH), lambda i, b: (b, i, 0)),
        out_shape=jax.ShapeDtypeStruct((bn, l, S5_CH), BF16),
        scratch_shapes=[pltpu.VMEM((S5_NBLK, rows, LANES), F32)],
        compiler_params=_cparams(("arbitrary", "arbitrary")),
        name="glu",
    )(y6, w_glu)


def _merge_kernel(att_ref, z_ref, gm_ref, gs_ref, x_ref, mod_ref, gpost_ref, gpre_ref,
                  wbm_ref, wbs_ref, wo_ref, wrh_ref, wrl_ref, x1_ref, h2b_ref, h2s_ref, lg_ref):
    ym = jnp.dot(att_ref[...], wbm_ref[...], preferred_element_type=F32)
    ys = jnp.dot(z_ref[...], wbs_ref[...], preferred_element_type=F32)
    mixed_in = (gm_ref[...].astype(F32) * ym + gs_ref[...].astype(F32) * ys).astype(BF16)
    mixed = jnp.dot(mixed_in, wo_ref[...], preferred_element_type=F32)
    x1 = x_ref[0] + mod_ref[0, 2:3, :] * _rms(mixed, gpost_ref[...])
    x1_ref[0] = x1
    h2 = _rms(x1, gpre_ref[...]) * (1.0 + mod_ref[0, 4:5, :]) + mod_ref[0, 3:4, :]
    h2_hi = h2.astype(BF16)
    h2b_ref[0] = h2_hi
    tm = h2.shape[0]
    for s in range(SLAB):
        h2s_ref[pl.ds(s, tm, stride=SLAB), :] = h2[:, s * LANES:(s + 1) * LANES]
    h2_lo = (h2 - h2_hi.astype(F32)).astype(BF16)
    nt = (((1,), (1,)), ((), ()))
    lg_ref[...] = (lax.dot_general(wrh_ref[...], h2_hi, nt, preferred_element_type=F32)
                   + lax.dot_general(wrl_ref[...], h2_hi, nt, preferred_element_type=F32)
                   + lax.dot_general(wrh_ref[...], h2_lo, nt, preferred_element_type=F32))


def _merge(att, z_tm, gates, x, mod, g_post, g_pre_ffn, w_br_mla, w_br_s5, w_out, wr_hi, wr_lo):
    bn, l, d = x.shape
    tm = 256
    nl = l // tm
    row = lambda b, i: (b * nl + i, 0)
    const = lambda b, i: (0, 0)
    one = pl.Buffered(1)
    return pl.pallas_call(
        _merge_kernel,
        grid=(bn, nl),
        in_specs=[pl.BlockSpec((tm, MLA_HEADS * V_HEAD), row),
                  pl.BlockSpec((tm, S5_CH), row),
                  pl.BlockSpec((tm, d), lambda b, i: (b * nl + i, 0)),
                  pl.BlockSpec((tm, d), lambda b, i: (b * nl + i, 1)),
                  pl.BlockSpec((1, tm, d), lambda b, i: (b, i, 0)),
                  pl.BlockSpec((1, 6, d), lambda b, i: (b, 0, 0)),
                  pl.BlockSpec((1, d), const),
                  pl.BlockSpec((1, d), const),
                  pl.BlockSpec((MLA_HEADS * V_HEAD, d), const, pipeline_mode=one),
                  pl.BlockSpec((S5_CH, d), const, pipeline_mode=one),
                  pl.BlockSpec((d, d), const, pipeline_mode=one),
                  pl.BlockSpec((N_EXPERTS, d), const, pipeline_mode=one),
                  pl.BlockSpec((N_EXPERTS, d), const, pipeline_mode=one)],
        out_specs=[pl.BlockSpec((1, tm, d), lambda b, i: (b, i, 0)),
                   pl.BlockSpec((1, tm, d), lambda b, i: (b, i, 0)),
                   pl.BlockSpec((tm * SLAB, LANES), row),
                   pl.BlockSpec((N_EXPERTS, tm), lambda b, i: (0, b * nl + i))],
        out_shape=[jax.ShapeDtypeStruct((bn, l, d), F32),
                   jax.ShapeDtypeStruct((bn, l, d), BF16),
                   jax.ShapeDtypeStruct((bn * l * SLAB, LANES), F32),
                   jax.ShapeDtypeStruct((N_EXPERTS, bn * l), F32)],
        compiler_params=_cparams(("arbitrary", "arbitrary")),
        name="merge",
    )(att, z_tm, gates, gates, x, mod, g_post.reshape(1, d), g_pre_ffn.reshape(1, d),
      w_br_mla, w_br_s5, w_out, wr_hi, wr_lo)


def _route_kernel(lg_ref, b_ref, ti_ref, tw_ref, cnt_ref):
    ng = N_EXPERT_GROUPS
    gsz = N_EXPERTS // ng
    lg = lg_ref[...]
    t = lg.shape[-1]
    sc = jax.nn.sigmoid(lg)
    sel = sc + b_ref[...]
    ninf = -jnp.inf
    i_in = lax.broadcasted_iota(jnp.int32, (ng, gsz, t), 1).astype(F32)
    m1 = jnp.max(sel, axis=1, keepdims=True)
    idx1 = jnp.min(jnp.where(sel == m1, i_in, float(gsz)), axis=1, keepdims=True)
    m2 = jnp.max(jnp.where(i_in == idx1, ninf, sel), axis=1, keepdims=True)
    gs = m1 + m2
    g_i = lax.broadcasted_iota(jnp.int32, (ng, 1, t), 0).astype(F32)
    picked = jnp.zeros((ng, 1, t), F32)
    cur = gs
    for _ in range(TOPK_GROUPS):
        m = jnp.max(cur, axis=0, keepdims=True)
        idx = jnp.min(jnp.where(cur == m, g_i, float(ng)), axis=0, keepdims=True)
        hit = g_i == idx
        picked = jnp.where(hit, 1.0, picked)
        cur = jnp.where(hit, ninf, cur)
    cand = jnp.where(picked > 0.5, sel, ninf)
    e_i = lax.broadcasted_iota(jnp.int32, (ng, gsz, t), 0).astype(F32) * float(gsz) + i_in
    ws = []
    hits = jnp.zeros((ng, gsz, t), F32)
    for r in range(TOP_K):
        m = jnp.max(jnp.max(cand, axis=1, keepdims=True), axis=0, keepdims=True)
        idx = jnp.min(jnp.min(jnp.where(cand == m, e_i, float(N_EXPERTS)), axis=1, keepdims=True),
                      axis=0, keepdims=True)
        hit = e_i == idx
        w = jnp.sum(jnp.sum(jnp.where(hit, sc, 0.0), axis=1, keepdims=True), axis=0, keepdims=True)
        ti_ref[r:r + 1, :] = idx[0].astype(jnp.int32)
        ws.append(w[0])
        hits = jnp.where(hit, 1.0, hits)
        cand = jnp.where(hit, ninf, cand)

    @pl.when(pl.program_id(0) == 0)
    def _():
        cnt_ref[...] = jnp.zeros(cnt_ref.shape, F32)

    cnt_ref[...] += jnp.sum(hits, axis=2, keepdims=True)
    tot = ws[0]
    for w in ws[1:]:
        tot = tot + w
    for r in range(TOP_K):
        tw_ref[r:r + 1, :] = ws[r] / tot * ROUTED_SCALE
    for r in range(TOP_K, SUBLANES):
        ti_ref[r:r + 1, :] = jnp.zeros((1, t), jnp.int32)
        tw_ref[r:r + 1, :] = jnp.zeros((1, t), F32)


def _route(logits_t, router_bias):
    n = logits_t.shape[1]
    ng = N_EXPERT_GROUPS
    gsz = N_EXPERTS // ng
    tn = 1024
    return pl.pallas_call(
        _route_kernel,
        grid=(n // tn,),
        in_specs=[pl.BlockSpec((ng, gsz, tn), lambda i: (0, 0, i)),
                  pl.BlockSpec((ng, gsz, 1), lambda i: (0, 0, 0))],
        out_specs=[pl.BlockSpec((SUBLANES, tn), lambda i: (0, i)),
                   pl.BlockSpec((SUBLANES, tn), lambda i: (0, i)),
                   pl.BlockSpec((ng, gsz, 1), lambda i: (0, 0, 0))],
        out_shape=[jax.ShapeDtypeStruct((SUBLANES, n), jnp.int32),
                   jax.ShapeDtypeStruct((SUBLANES, n), F32),
                   jax.ShapeDtypeStruct((ng, gsz, 1), F32)],
        compiler_params=_cparams(("arbitrary",)),
        name="route",
    )(logits_t.reshape(ng, gsz, n), router_bias.reshape(ng, gsz, 1))


def _moe_kernel(blk_e_ref, nused_ref, nvalid_ref, nxt_e_ref, wslot_ref, t_ref, tn_ref, a_ref, h_hbm,
                wg_hbm, wu_hbm, wd_hbm, y_hbm, xbuf, ybuf, wg_buf, wu_buf, wd_buf, wg_s, wu_s, wd_s,
                gsem, wsem, wtsem):
    j = pl.program_id(0)
    n_used = nused_ref[0]
    slot = lax.rem(j, 2)
    nv = nvalid_ref[j]
    nv_prev = nvalid_ref[jnp.maximum(j - 1, 0)]

    def slab(ref, row0):
        return ref.at[pl.ds(pl.multiple_of(row0, SUBLANES), SLAB), :]

    def start_gather(idx_ref, s):
        def body(r, _):
            pltpu.make_async_copy(slab(h_hbm, idx_ref[0, 0, r]), slab(xbuf.at[s], r * MOE_PITCH), gsem.at[s]).start()
            return 0
        lax.fori_loop(0, MOE_TB, body, 0, unroll=8)

    def wait_gather(s):
        rows = MOE_TB * SLAB
        pltpu.make_async_copy(h_hbm.at[pl.ds(0, rows), :], xbuf.at[s, pl.ds(0, rows), :], gsem.at[s]).wait()

    def weight_copies(expert, s):
        return (pltpu.make_async_copy(wg_hbm.at[expert], wg_buf.at[s], wtsem.at[s]),
                pltpu.make_async_copy(wu_hbm.at[expert], wu_buf.at[s], wtsem.at[s]),
                pltpu.make_async_copy(wd_hbm.at[expert], wd_buf.at[s], wtsem.at[s]))

    def wait_writes(count):
        p = MOE_TB
        while p >= 1:
            @pl.when((count & p) != 0)
            def _(p=p):
                rows = p * SLAB
                pltpu.make_async_copy(ybuf.at[pl.ds(0, rows), :], y_hbm.at[pl.ds(0, rows), :], wsem.at[0]).wait()
            p //= 2

    @pl.when(j < n_used)
    def _():
        @pl.when(j == 0)
        def _():
            start_gather(t_ref, 0)

        e = blk_e_ref[j]
        ws = wslot_ref[j]

        @pl.when(j == 0)
        def _():
            for cp in weight_copies(e, ws):
                cp.start()

        first = jnp.logical_or(j == 0, e != blk_e_ref[jnp.maximum(j - 1, 0)])

        @pl.when(first)
        def _():
            for cp in weight_copies(e, ws):
                cp.wait()
            wg_s[...] = wg_buf[ws].astype(BF16)
            wu_s[...] = wu_buf[ws].astype(BF16)
            wd_s[...] = wd_buf[ws].astype(BF16)

            @pl.when(nxt_e_ref[j] >= 0)
            def _():
                for cp in weight_copies(nxt_e_ref[j], 1 - ws):
                    cp.start()

        wait_gather(slot)

        @pl.when(j + 1 < n_used)
        def _():
            start_gather(tn_ref, 1 - slot)

        xs = xbuf.at[slot]
        x = jnp.concatenate([xs[pl.ds(s, MOE_TB, stride=MOE_PITCH), :] for s in range(SLAB)], axis=1).astype(BF16)
        g = jnp.dot(x, wg_s[...], preferred_element_type=F32)
        u = jnp.dot(x, wu_s[...], preferred_element_type=F32)
        hm = (g * jax.nn.sigmoid(g) * u).astype(BF16)
        y = jnp.dot(hm, wd_s[...], preferred_element_type=F32)

        @pl.when(j > 0)
        def _():
            wait_writes(nv_prev)

        for s in range(SLAB):
            ybuf[pl.ds(s, MOE_TB, stride=MOE_PITCH), :] = y[:, s * LANES:(s + 1) * LANES]

        def start_write(r):
            pltpu.make_async_copy(slab(ybuf, r * MOE_PITCH), slab(y_hbm, a_ref[0, 0, r]), wsem.at[0]).start()

        def wgroup(g8, _):
            for q in range(SUBLANES):
                start_write(g8 * SUBLANES + q)
            return 0

        def wtail(r, _):
            start_write(r)
            return 0

        n_groups = lax.shift_right_logical(nv, 3)
        lax.fori_loop(0, n_groups, wgroup, 0)
        lax.fori_loop(n_groups * SUBLANES, nv, wtail, 0)

        @pl.when(j == n_used - 1)
        def _():
            wait_writes(nv)


def _moe(h2s, topi, counts, w_gate, w_up, w_down):
    n = h2s.shape[0] // SLAB
    d = D_MODEL
    nk = n * TOP_K
    tb = MOE_TB
    n_blocks = nk // tb + N_EXPERTS
    flat_e = topi[:TOP_K].reshape(nk)
    _, order = lax.sort_key_val(flat_e, jnp.arange(nk, dtype=jnp.int32))
    start = jnp.cumsum(counts) - counts
    padded = (counts + tb - 1) // tb * tb
    pad_end = jnp.cumsum(padded)
    pad_start = pad_end - padded
    n_used = (pad_end[-1] // tb).astype(jnp.int32).reshape(1)
    blk_p0 = jnp.arange(n_blocks, dtype=jnp.int32) * tb
    blk_e = jnp.minimum(jnp.sum((pad_end[None, :] <= blk_p0[:, None]).astype(jnp.int32), axis=1), N_EXPERTS - 1)
    blk_off = blk_p0 - pad_start[blk_e]
    nvalid = jnp.clip(counts[blk_e] - blk_off, 0, tb).astype(jnp.int32)
    rank = (start[blk_e] + blk_off)[:, None] + jnp.arange(tb, dtype=jnp.int32)[None, :]
    a3 = order[jnp.clip(rank, 0, nk - 1)].reshape(n_blocks, 1, tb)
    arow = a3 * SLAB
    trow = (a3 % n) * SLAB
    used = counts > 0
    e_ids = jnp.arange(N_EXPERTS, dtype=jnp.int32)
    later = jnp.where(used, e_ids, N_EXPERTS)
    nxt = jnp.concatenate([lax.cummin(later[::-1])[::-1][1:], jnp.full((1,), N_EXPERTS, jnp.int32)])
    nxt_e = jnp.where(nxt < N_EXPERTS, nxt, -1)[blk_e].astype(jnp.int32)
    wslot = ((jnp.cumsum(used.astype(jnp.int32)) - 1) % 2)[blk_e].astype(jnp.int32)

    smem_blk = lambda f: pl.BlockSpec((1, 1, tb), f, memory_space=pltpu.SMEM)
    any_spec = pl.BlockSpec(memory_space=pl.ANY)
    grid_spec = pltpu.PrefetchScalarGridSpec(
        num_scalar_prefetch=5,
        grid=(n_blocks,),
        in_specs=[smem_blk(lambda j, *_: (j, 0, 0)),
                  smem_blk(lambda j, *_: (jnp.minimum(j + 1, n_blocks - 1), 0, 0)),
                  smem_blk(lambda j, *_: (j, 0, 0)),
                  any_spec, any_spec, any_spec, any_spec],
        out_specs=any_spec,
        scratch_shapes=[pltpu.VMEM((2, tb * MOE_PITCH, LANES), F32),
                        pltpu.VMEM((tb * MOE_PITCH, LANES), F32),
                        pltpu.VMEM((2, d, D_EXPERT), F32),
                        pltpu.VMEM((2, d, D_EXPERT), F32),
                        pltpu.VMEM((2, D_EXPERT, d), F32),
                        pltpu.VMEM((d, D_EXPERT), BF16),
                        pltpu.VMEM((d, D_EXPERT), BF16),
                        pltpu.VMEM((D_EXPERT, d), BF16),
                        pltpu.SemaphoreType.DMA((2,)),
                        pltpu.SemaphoreType.DMA((1,)),
                        pltpu.SemaphoreType.DMA((2,))],
    )
    return pl.pallas_call(
        _moe_kernel,
        grid_spec=grid_spec,
        out_shape=jax.ShapeDtypeStruct((TOP_K * n * SLAB, LANES), F32),
        compiler_params=_cparams(("arbitrary",)),
        name="moe",
    )(blk_e, n_used, nvalid, nxt_e, wslot, trow, trow, arow, h2s, w_gate, w_up, w_down)


def _final_kernel(h_ref, y0, y1, y2, y3, y4, y5, tw_ref, x1_ref, mod_ref, g_ref, wg_ref, wu_ref, wd_ref, o_ref):
    h = h_ref[...]
    tm = h.shape[0]
    g = jnp.dot(h, wg_ref[...], preferred_element_type=F32)
    u = jnp.dot(h, wu_ref[...], preferred_element_type=F32)
    hm = (g * jax.nn.sigmoid(g) * u).astype(BF16)
    ffn = jnp.dot(hm, wd_ref[...], preferred_element_type=F32)
    tw = tw_ref[...]
    for k, y in enumerate((y0, y1, y2, y3, y4, y5)):
        yk = jnp.concatenate([y[pl.ds(s, tm, stride=SLAB), :] for s in range(SLAB)], axis=1)
        ffn = ffn + yk * tw[:, k:k + 1]
    o_ref[0] = x1_ref[0] + mod_ref[0, 5:6, :] * _rms(ffn, g_ref[...])


def _final(h2, y_flat, topw_t, x1, mod, g_post_ffn, w_sg, w_su, w_sd):
    bn, l, d = x1.shape
    n = bn * l
    tm = 256
    nl = l // tm
    const = lambda b, i: (0, 0)
    one = pl.Buffered(1)
    y_specs = [pl.BlockSpec((tm * SLAB, LANES), functools.partial(lambda b, i, k: (k * (n // tm) + b * nl + i, 0), k=k))
               for k in range(TOP_K)]
    return pl.pallas_call(
        _final_kernel,
        grid=(bn, nl),
        in_specs=[pl.BlockSpec((tm, d), lambda b, i: (b * nl + i, 0))] + y_specs + [
            pl.BlockSpec((tm, SUBLANES), lambda b, i: (b * nl + i, 0)),
            pl.BlockSpec((1, tm, d), lambda b, i: (b, i, 0)),
            pl.BlockSpec((1, 6, d), lambda b, i: (b, 0, 0)),
            pl.BlockSpec((1, d), const),
            pl.BlockSpec((d, D_EXPERT), const, pipeline_mode=one),
            pl.BlockSpec((d, D_EXPERT), const, pipeline_mode=one),
            pl.BlockSpec((D_EXPERT, d), const, pipeline_mode=one)],
        out_specs=pl.BlockSpec((1, tm, d), lambda b, i: (b, i, 0)),
        out_shape=jax.ShapeDtypeStruct((bn, l, d), F32),
        compiler_params=_cparams(("arbitrary", "arbitrary")),
        name="final",
    )(h2, *([y_flat] * TOP_K), topw_t, x1, mod, g_post_ffn.reshape(1, d), w_sg, w_su, w_sd)


def _layer(li, x, c, positions, w_ada, b_ada, g_pre_mix, g_post_mix, g_pre_ffn, g_post_ffn, w_in, g_q, g_kv,
           w_uq, w_uk, w_uv, a_re, a_im, log_dt, b_re, b_im, c_re, c_im, d_skip, w_glu, w_br_mla, w_br_s5,
           w_out, w_router, router_bias, w_exp_gate, w_exp_up, w_exp_down, w_sh_gate, w_sh_up, w_sh_down):
    bn, l, d = x.shape
    n = bn * l
    mod = _adaln(c, w_ada, b_ada).reshape(bn, 6, d)

    o_kpe = Q_LORA + KV_LORA
    o_u = o_kpe + QK_ROPE
    o_g = o_u + S5_CH
    w_in_t = jnp.swapaxes(w_in, 1, 2)
    w_q = jnp.pad(w_uq.reshape(Q_LORA, MLA_HEADS, QK_NOPE + QK_ROPE),
                  ((0, 0), (0, 0), (0, QK_PAD - QK_NOPE - QK_ROPE))).reshape(Q_LORA, MLA_HEADS * QK_PAD).astype(BF16)
    wr_t = w_router.T
    wr_hi = wr_t.astype(BF16)
    wr_lo = (wr_t - wr_hi.astype(F32)).astype(BF16)

    rope_c, rope_s1, rope_s2 = _rope_tables(positions)
    chunk_id = positions // CHUNK

    h = _prenorm(x, mod, g_pre_mix, 0, 1).reshape(n, d)
    lat = _mmt(h, w_in_t, li, 0, o_kpe, F32, 1024, 512, name="mm_lat")
    kpe = _mmt(h, w_in_t, li, o_kpe, LANES, F32, 1024, LANES, name="mm_kpe")
    gates = _mmt(h, w_in_t, li, o_g, 2 * d, BF16, 1024, 1024, act="sigmoid", name="mm_gates")
    u6 = _mm_u(h, w_in_t, li, o_u, bn, l)

    q, k, v = _qkvproj(lat, g_q, g_kv, w_q, w_uk.astype(BF16), w_uv.astype(BF16), kpe, rope_c, rope_s1, rope_s2)
    att = _attention(q, k, v, chunk_id, bn, l)

    wb, ab, wc, dd = _s5_params(a_re, a_im, log_dt, b_re, b_im, c_re, c_im, d_skip)
    y6 = _s5(u6, wb, ab, wc, dd, l)
    z = _glu(y6, w_glu.astype(BF16), bn, l).reshape(n, S5_CH)

    x1, h2b, h2s, logits_t = _merge(att, z, gates, x, mod, g_post_mix, g_pre_ffn, w_br_mla.astype(BF16),
                                    w_br_s5.astype(BF16), w_out.astype(BF16), wr_hi, wr_lo)

    topi, topw, cnt = _route(logits_t, router_bias)
    y_slabs = _moe(h2s, topi, cnt.reshape(N_EXPERTS).astype(jnp.int32), w_exp_gate, w_exp_up, w_exp_down)
    return _final(h2b.reshape(n, d), y_slabs, topw.T, x1, mod, g_post_ffn, w_sh_gate.astype(BF16),
                  w_sh_up.astype(BF16), w_sh_down.astype(BF16))


def kernel(x, c, positions, w_ada, b_ada, g_pre_mix, g_post_mix, g_pre_ffn, g_post_ffn, w_in, g_q, g_kv, w_uq, w_uk, w_uv, a_re, a_im, log_dt, b_re, b_im, c_re, c_im, d_skip, w_glu, w_br_mla, w_br_s5, w_out, w_router, router_bias, w_exp_gate, w_exp_up, w_exp_down, w_sh_gate, w_sh_up, w_sh_down):
    depth = w_ada.shape[0]
    for li in range(depth):
        x = _layer(li, x, c, positions, w_ada[li], b_ada[li], g_pre_mix[li], g_post_mix[li], g_pre_ffn[li],
                   g_post_ffn[li], w_in, g_q[li], g_kv[li], w_uq[li], w_uk[li], w_uv[li], a_re[li], a_im[li],
                   log_dt[li], b_re[li], b_im[li], c_re[li], c_im[li], d_skip[li], w_glu[li], w_br_mla[li],
                   w_br_s5[li], w_out[li], w_router[li], router_bias[li], w_exp_gate[li], w_exp_up[li],
                   w_exp_down[li], w_sh_gate[li], w_sh_up[li], w_sh_down[li])
    return x
```

```python
import functools

import jax
import jax.numpy as jnp
from jax import lax
from jax.experimental import pallas as pl
from jax.experimental.pallas import tpu as pltpu

F32 = jnp.float32
BF16 = jnp.bfloat16

D_MODEL = 2048
CHUNK = 64
EPS = 1e-6
MLA_HEADS = 8
QK_NOPE = 128
QK_ROPE = 64
V_HEAD = 128
Q_LORA = 512
KV_LORA = 512
ROPE_THETA = 10000.0
S5_CH = 1024
S5_GROUP = 16
S5_GROUPS = S5_CH // S5_GROUP
S5_STATE = 64
N_EXPERTS = 64
TOP_K = 6
N_EXPERT_GROUPS = 8
TOPK_GROUPS = 4
D_EXPERT = 512
ROUTED_SCALE = 2.5

LANES = 128
SUBLANES = 8
QK_PAD = 2 * LANES
VMEM_LIMIT = 56 * 1024 * 1024
NEG = -1e30
LOG2E = 1.4426950408889634

S5_GB = LANES // S5_GROUP
S5_NBLK = S5_CH // LANES
S5_PAIRS = S5_NBLK // 2
S5_LANES = S5_GB * S5_STATE
S5_TC = 256
S5_RB = 512

ATT_TQ = 256
ATT_TK = 256
MOE_TB = 256
SLAB = D_MODEL // LANES
WEIGHT_DMA_PRIORITY = 1
MOE_PITCH = 24


def _cparams(sem):
    return pltpu.CompilerParams(dimension_semantics=sem, vmem_limit_bytes=VMEM_LIMIT)


def _rms(x, g):
    return x * lax.rsqrt(jnp.mean(x * x, axis=-1, keepdims=True) + EPS) * g


def _adaln_kernel(c_ref, w_ref, b_ref, o_ref):
    c = c_ref[...]
    a = (c * jax.nn.sigmoid(c)).astype(BF16)
    o_ref[...] = jnp.dot(a, w_ref[...].astype(BF16), preferred_element_type=F32) + b_ref[...]


def _adaln(c, w, b):
    bn, d = c.shape
    n = w.shape[1]
    tn = 1024
    return pl.pallas_call(
        _adaln_kernel,
        grid=(n // tn,),
        in_specs=[pl.BlockSpec((bn, d), lambda j: (0, 0)),
                  pl.BlockSpec((d, tn), lambda j: (0, j)),
                  pl.BlockSpec((1, tn), lambda j: (0, j))],
        out_specs=pl.BlockSpec((bn, tn), lambda j: (0, j)),
        out_shape=jax.ShapeDtypeStruct((bn, n), F32),
        compiler_params=_cparams(("arbitrary",)),
        name="adaln",
    )(c, w, b.reshape(1, n))


def _prenorm_kernel(x_ref, mod_ref, g_ref, o_ref, *, sh_row, sc_row):
    y = _rms(x_ref[0], g_ref[...])
    o_ref[0] = (y * (1.0 + mod_ref[0, sc_row:sc_row + 1, :]) + mod_ref[0, sh_row:sh_row + 1, :]).astype(o_ref.dtype)


def _prenorm(x, mod, g, sh_row, sc_row):
    bn, l, d = x.shape
    tl = 512
    return pl.pallas_call(
        functools.partial(_prenorm_kernel, sh_row=sh_row, sc_row=sc_row),
        grid=(bn, l // tl),
        in_specs=[pl.BlockSpec((1, tl, d), lambda b, i: (b, i, 0)),
                  pl.BlockSpec((1, 6, d), lambda b, i: (b, 0, 0)),
                  pl.BlockSpec((1, d), lambda b, i: (0, 0))],
        out_specs=pl.BlockSpec((1, tl, d), lambda b, i: (b, i, 0)),
        out_shape=jax.ShapeDtypeStruct((bn, l, d), BF16),
        compiler_params=_cparams(("arbitrary", "arbitrary")),
        name="prenorm",
    )(x, mod, g.reshape(1, d))


NT_DIMS = (((1,), (1,)), ((), ()))


def _wt_tile(wa_ref, wb_ref, shift, tn):
    if shift == 0:
        return wa_ref[...].astype(BF16)
    return jnp.concatenate([wa_ref[...], wb_ref[...]], axis=0)[shift:shift + tn].astype(BF16)


def _wt_specs(layer, row0, tn, k, jmap):
    shift = row0 % LANES
    c0 = row0 - shift
    assert c0 % tn == 0 and shift % SUBLANES == 0
    specs = [pl.BlockSpec((None, tn, k), lambda *g: (layer, jmap(*g) + c0 // tn, 0))]
    if shift:
        specs.append(pl.BlockSpec((None, LANES, k), lambda *g: (layer, (c0 + (jmap(*g) + 1) * tn) // LANES, 0)))
    return specs, shift


def _mmt_kernel(a_ref, wa_ref, *rest, act, shift):
    wb_ref = rest[0] if shift else None
    o_ref, w_scr = rest[-2:]

    @pl.when(pl.program_id(1) == 0)
    def _():
        w_scr[...] = _wt_tile(wa_ref, wb_ref, shift, w_scr.shape[0])

    acc = lax.dot_general(a_ref[...], w_scr[...], NT_DIMS, preferred_element_type=F32)
    if act == "sigmoid":
        acc = jax.nn.sigmoid(acc)
    o_ref[...] = acc.astype(o_ref.dtype)


def _mmt(a, w_t, layer, row0, n, out_dtype, tm, tn, act=None, name="mmt"):
    m, k = a.shape
    w_specs, shift = _wt_specs(layer, row0, tn, k, lambda j, i: j)
    return pl.pallas_call(
        functools.partial(_mmt_kernel, act=act, shift=shift),
        grid=(n // tn, m // tm),
        in_specs=[pl.BlockSpec((tm, k), lambda j, i: (i, 0))] + w_specs,
        out_specs=pl.BlockSpec((tm, tn), lambda j, i: (i, j)),
        out_shape=jax.ShapeDtypeStruct((m, n), out_dtype),
        scratch_shapes=[pltpu.VMEM((tn, k), BF16)],
        compiler_params=_cparams(("arbitrary", "arbitrary")),
        name=name,
    )(a, *([w_t] * len(w_specs)))


def _mm_u_kernel(a_ref, wa_ref, *rest, shift):
    wb_ref = rest[0] if shift else None
    o_ref, w_scr = rest[-2:]
    b = pl.program_id(1)

    @pl.when(jnp.logical_and(pl.program_id(0) == 0, b == 0))
    def _():
        w_scr[...] = _wt_tile(wa_ref, wb_ref, shift, w_scr.shape[0])

    res = lax.dot_general(a_ref[...], w_scr[...], NT_DIMS, preferred_element_type=F32)
    tm = res.shape[0]
    for c in range(S5_NBLK):
        o_ref.at[c // 2][pl.ds(b * 2 + c % 2, tm, stride=SUBLANES), :] = res[:, c * LANES:(c + 1) * LANES]


def _mm_u(h, w_t, layer, row0, bn, l):
    m, k = h.shape
    tm = 512
    nl = l // tm
    assert bn * 2 == SUBLANES
    w_specs, shift = _wt_specs(layer, row0, S5_CH, k, lambda i, b: 0)
    return pl.pallas_call(
        functools.partial(_mm_u_kernel, shift=shift),
        grid=(nl, bn),
        in_specs=[pl.BlockSpec((tm, k), lambda i, b: (b * nl + i, 0))] + w_specs,
        out_specs=pl.BlockSpec((S5_PAIRS, tm * SUBLANES, LANES), lambda i, b: (0, i, 0)),
        out_shape=jax.ShapeDtypeStruct((S5_PAIRS, l * SUBLANES, LANES), F32),
        scratch_shapes=[pltpu.VMEM((S5_CH, k), BF16)],
        compiler_params=_cparams(("arbitrary", "arbitrary")),
        name="mm_u",
    )(h, *([w_t] * len(w_specs)))


def _rope_tab_kernel(pos_ref, k_ref, c_ref, s1_ref, s2_ref):
    ang = pos_ref[...].astype(F32) * k_ref[0:1, :]
    s = jnp.sin(ang)
    c_ref[...] = jnp.cos(ang) * k_ref[1:2, :]
    s1_ref[...] = s * k_ref[2:3, :]
    s2_ref[...] = s * k_ref[3:4, :]


def _rope_tables(positions):
    n = positions.size
    half = QK_ROPE // 2
    inv_freq = ROPE_THETA ** (-jnp.arange(half, dtype=F32) / half)
    zh, oh = jnp.zeros((half,), F32), jnp.ones((half,), F32)
    z2 = jnp.zeros((LANES - QK_ROPE,), F32)
    rows = [jnp.concatenate([inv_freq, inv_freq, z2]), jnp.concatenate([oh, oh, z2]),
            jnp.concatenate([-oh, zh, z2]), jnp.concatenate([zh, oh, z2])]
    consts = jnp.stack(rows + [jnp.zeros((LANES,), F32)] * (SUBLANES - len(rows)))
    tm = 1024
    tab = jax.ShapeDtypeStruct((n, LANES), F32)
    return pl.pallas_call(
        _rope_tab_kernel,
        grid=(n // tm,),
        in_specs=[pl.BlockSpec((tm, 1), lambda i: (i, 0)),
                  pl.BlockSpec((SUBLANES, LANES), lambda i: (0, 0))],
        out_specs=[pl.BlockSpec((tm, LANES), lambda i: (i, 0))] * 3,
        out_shape=[tab, tab, tab],
        compiler_params=_cparams(("arbitrary",)),
        name="rope_tables",
    )(positions.reshape(n, 1), consts)


def _rope_tile(t, c_ref, s1_ref, s2_ref):
    return (t * c_ref[...] + pltpu.roll(t, LANES - QK_ROPE // 2, 1) * s1_ref[...]
            + pltpu.roll(t, QK_ROPE // 2, 1) * s2_ref[...])


def _qkvproj_kernel(lat_ref, gq_ref, gkv_ref, wq_ref, wk_ref, wv_ref, kpe_ref, c_ref, s1_ref, s2_ref,
                    q_ref, k_ref, v_ref, *, scale):
    lat = lat_ref[...]
    qn = _rms(lat[:, :Q_LORA], gq_ref[...]).astype(BF16)
    cn = _rms(lat[:, Q_LORA:], gkv_ref[...]).astype(BF16)
    q = jnp.dot(qn, wq_ref[...], preferred_element_type=F32)
    kn = jnp.dot(cn, wk_ref[...], preferred_element_type=F32)
    v_ref[...] = jnp.dot(cn, wv_ref[...], preferred_element_type=F32).astype(v_ref.dtype)
    kt = _rope_tile(kpe_ref[...], c_ref, s1_ref, s2_ref).astype(k_ref.dtype)
    for h in range(MLA_HEADS):
        o = h * QK_PAD
        q_ref[:, o:o + LANES] = (q[:, o:o + LANES] * scale).astype(q_ref.dtype)
        qt = _rope_tile(q[:, o + LANES:o + QK_PAD], c_ref, s1_ref, s2_ref)
        q_ref[:, o + LANES:o + QK_PAD] = (qt * scale).astype(q_ref.dtype)
        k_ref[:, o:o + LANES] = kn[:, h * QK_NOPE:(h + 1) * QK_NOPE].astype(k_ref.dtype)
        k_ref[:, o + LANES:o + QK_PAD] = kt


def _qkvproj(lat, g_q, g_kv, w_q, w_k, w_v, kpe, rope_c, rope_s1, rope_s2):
    n = lat.shape[0]
    tm = 512
    row = lambda i: (i, 0)
    const = lambda i: (0, 0)
    tab = pl.BlockSpec((tm, LANES), row)
    return pl.pallas_call(
        functools.partial(_qkvproj_kernel, scale=(QK_NOPE + QK_ROPE) ** -0.5 * LOG2E),
        grid=(n // tm,),
        in_specs=[pl.BlockSpec((tm, Q_LORA + KV_LORA), row),
                  pl.BlockSpec((1, Q_LORA), const),
                  pl.BlockSpec((1, KV_LORA), const),
                  pl.BlockSpec((Q_LORA, MLA_HEADS * QK_PAD), const),
                  pl.BlockSpec((KV_LORA, MLA_HEADS * QK_NOPE), const),
                  pl.BlockSpec((KV_LORA, MLA_HEADS * V_HEAD), const),
                  tab, tab, tab, tab],
        out_specs=[pl.BlockSpec((tm, MLA_HEADS * QK_PAD), row),
                   pl.BlockSpec((tm, MLA_HEADS * QK_PAD), row),
                   pl.BlockSpec((tm, MLA_HEADS * V_HEAD), row)],
        out_shape=[jax.ShapeDtypeStruct((n, MLA_HEADS * QK_PAD), BF16),
                   jax.ShapeDtypeStruct((n, MLA_HEADS * QK_PAD), BF16),
                   jax.ShapeDtypeStruct((n, MLA_HEADS * V_HEAD), BF16)],
        compiler_params=_cparams(("arbitrary",)),
        name="qkvproj",
    )(lat, g_q.reshape(1, Q_LORA), g_kv.reshape(1, KV_LORA), w_q, w_k, w_v, kpe, rope_c, rope_s1, rope_s2)


def _attn_kernel(lo_ref, hi_ref, q_ref, k_ref, v_ref, qc_ref, kc_ref, o_ref, m_scr, l_scr, acc_scr, *, nq):
    b = pl.program_id(0)
    i = pl.program_id(1)
    qc = qc_ref[...]
    m_scr[...] = jnp.full(m_scr.shape, NEG, F32)
    l_scr[...] = jnp.zeros(l_scr.shape, F32)
    acc_scr[...] = jnp.zeros(acc_scr.shape, F32)

    def make_body(masked):
        def body(j, _):
            off = pl.multiple_of(j * ATT_TK, ATT_TK)
            if masked:
                mask = kc_ref[j] <= qc
            for h in range(MLA_HEADS):
                q = q_ref[:, h * QK_PAD:(h + 1) * QK_PAD]
                k = k_ref[pl.ds(off, ATT_TK), h * QK_PAD:(h + 1) * QK_PAD]
                s = lax.dot_general(q, k, (((1,), (1,)), ((), ())), preferred_element_type=F32)
                if masked:
                    s = jnp.where(mask, s, NEG)
                m_old = m_scr[h]
                m_new = jnp.maximum(m_old, jnp.max(s, axis=-1, keepdims=True))
                p = jnp.exp2(s - jnp.concatenate([m_new] * (ATT_TK // LANES), axis=1))
                alpha = jnp.exp2(m_old - m_new)
                l_scr[h] = alpha * l_scr[h] + jnp.sum(p, axis=-1, keepdims=True)
                v = v_ref[pl.ds(off, ATT_TK), h * V_HEAD:(h + 1) * V_HEAD]
                acc_scr[h] = alpha * acc_scr[h] + jnp.dot(p.astype(BF16), v, preferred_element_type=F32)
                m_scr[h] = m_new
            return 0
        return body

    lo = lo_ref[b * nq + i]
    lax.fori_loop(0, lo, make_body(False), 0)
    lax.fori_loop(lo, hi_ref[b * nq + i], make_body(True), 0)
    for h in range(MLA_HEADS):
        o_ref[:, h * V_HEAD:(h + 1) * V_HEAD] = (acc_scr[h] / l_scr[h]).astype(o_ref.dtype)


def _attention(q, k, v, chunk_id, bn, l):
    nq = l // ATT_TQ
    nk = l // ATT_TK
    q_max = jnp.max(chunk_id.reshape(bn, nq, ATT_TQ), axis=-1)
    k_min = jnp.min(chunk_id.reshape(bn, nk, ATT_TK), axis=-1)
    needed = k_min[:, None, :] <= q_max[:, :, None]
    hi = jnp.max(jnp.where(needed, jnp.arange(1, nk + 1, dtype=jnp.int32), 0), axis=-1).reshape(bn * nq)
    q_min = jnp.min(chunk_id.reshape(bn, nq, ATT_TQ), axis=-1)
    k_max = jnp.max(chunk_id.reshape(bn, nk, ATT_TK), axis=-1)
    full = k_max[:, None, :] <= q_min[:, :, None]
    lo = jnp.min(jnp.where(full, nk, jnp.arange(nk, dtype=jnp.int32)), axis=-1).astype(jnp.int32).reshape(bn * nq)
    hi = jnp.maximum(hi, lo)
    qc = chunk_id.reshape(bn * l, 1)
    kc = chunk_id.reshape(bn * nk, 1, ATT_TK)
    grid_spec = pltpu.PrefetchScalarGridSpec(
        num_scalar_prefetch=2,
        grid=(bn, nq),
        in_specs=[pl.BlockSpec((ATT_TQ, MLA_HEADS * QK_PAD), lambda b, i, *_: (b * nq + i, 0)),
                  pl.BlockSpec((l, MLA_HEADS * QK_PAD), lambda b, i, *_: (b, 0)),
                  pl.BlockSpec((l, MLA_HEADS * V_HEAD), lambda b, i, *_: (b, 0)),
                  pl.BlockSpec((ATT_TQ, 1), lambda b, i, *_: (b * nq + i, 0)),
                  pl.BlockSpec((nk, 1, ATT_TK), lambda b, i, *_: (b, 0, 0))],
        out_specs=pl.BlockSpec((ATT_TQ, MLA_HEADS * V_HEAD), lambda b, i, *_: (b * nq + i, 0)),
        scratch_shapes=[pltpu.VMEM((MLA_HEADS, ATT_TQ, LANES), F32),
                        pltpu.VMEM((MLA_HEADS, ATT_TQ, LANES), F32),
                        pltpu.VMEM((MLA_HEADS, ATT_TQ, V_HEAD), F32)],
    )
    return pl.pallas_call(
        functools.partial(_attn_kernel, nq=nq),
        grid_spec=grid_spec,
        out_shape=jax.ShapeDtypeStruct((bn * l, MLA_HEADS * V_HEAD), BF16),
        compiler_params=_cparams(("arbitrary", "arbitrary")),
        name="attention",
    )(lo, hi, q, k, v, qc, kc)


def _s5_kernel(u_ref, wb_ref, a_ref, wc_ref, d_ref, o_ref, x_scr, st_scr):
    rows = S5_TC * SUBLANES
    nsub = rows // S5_RB

    @pl.when(pl.program_id(1) == 0)
    def _():
        st_scr[...] = jnp.zeros_like(st_scr)

    even = (lax.broadcasted_iota(jnp.int32, (S5_RB, 1), 0) & 1) == 0
    half = S5_RB // 2
    nre = S5_LANES // LANES
    uv = u_ref.at[0]

    def mm_in(r, _):
        off = pl.multiple_of(r * S5_RB, S5_RB)
        for s in range(2):
            us = uv[pl.ds(off + s, half, stride=2), :].astype(BF16)
            out = jnp.dot(us, wb_ref[0, :, s * 2 * S5_LANES:(s + 1) * 2 * S5_LANES], preferred_element_type=F32)
            for c in range(2 * nre):
                x_scr.at[c][pl.ds(off + s, half, stride=2), :] = out[:, c * LANES:(c + 1) * LANES]
        return 0

    lax.fori_loop(0, nsub, mm_in, 0)

    a_c = [a_ref[0, :, c * LANES:(c + 1) * LANES] for c in range(2 * nre)]

    def step(t, carry):
        off = pl.multiple_of(t * SUBLANES, SUBLANES)
        new = [None] * (2 * nre)
        for c in range(nre):
            xr, xi = carry[c], carry[nre + c]
            ar, ai = a_c[c], a_c[nre + c]
            nr = ar * xr - ai * xi + x_scr[c, pl.ds(off, SUBLANES), :]
            ni = ar * xi + ai * xr + x_scr[nre + c, pl.ds(off, SUBLANES), :]
            x_scr[c, pl.ds(off, SUBLANES), :] = nr
            x_scr[nre + c, pl.ds(off, SUBLANES), :] = ni
            new[c], new[nre + c] = nr, ni
        return tuple(new)

    init = tuple(st_scr[:, c * LANES:(c + 1) * LANES] for c in range(2 * nre))
    fin = lax.fori_loop(0, S5_TC, step, init, unroll=8)
    for c in range(2 * nre):
        st_scr[:, c * LANES:(c + 1) * LANES] = fin[c]

    d = jnp.concatenate([d_ref[0]] * (S5_RB // SUBLANES), axis=0)

    def mm_out(r, _):
        off = pl.multiple_of(r * S5_RB, S5_RB)
        x = jnp.concatenate([x_scr[c, pl.ds(off, S5_RB), :] for c in range(2 * nre)], axis=1).astype(BF16)
        out = jnp.dot(x, wc_ref[0], preferred_element_type=F32)
        y = jnp.where(even, out[:, :LANES], out[:, LANES:]) + d * u_ref[0, pl.ds(off, S5_RB), :]
        o_ref[0, pl.ds(off, S5_RB), :] = jax.nn.gelu(y).astype(o_ref.dtype)
        return 0

    lax.fori_loop(0, nsub, mm_out, 0)


def _s5_params(a_re, a_im, log_dt, b_re, b_im, c_re, c_im, d_skip):
    step = jnp.exp(log_dt)[:, None]
    mag = jnp.exp(a_re * step)
    abar_re, abar_im = mag * jnp.cos(a_im * step), mag * jnp.sin(a_im * step)
    den = a_re * a_re + a_im * a_im
    nr, ni = abar_re - 1.0, abar_im
    f_re, f_im = (nr * a_re + ni * a_im) / den, (ni * a_re - nr * a_im) / den
    bbar_re = f_re[..., None] * b_re - f_im[..., None] * b_im
    bbar_im = f_re[..., None] * b_im + f_im[..., None] * b_re
    eye = jnp.eye(S5_GB, dtype=F32)
    bb = jnp.stack([bbar_re, bbar_im]).reshape(2, S5_NBLK, S5_GB, S5_STATE, S5_GROUP)
    wb = jnp.einsum('ab,rjapc->jacrbp', eye, bb).reshape(S5_NBLK, LANES, 2 * S5_LANES)
    wb = wb.reshape(S5_PAIRS, 2, LANES, 2 * S5_LANES).transpose(0, 2, 1, 3).reshape(S5_PAIRS, LANES, 4 * S5_LANES)
    cc = jnp.stack([c_re, -c_im]).reshape(2, S5_NBLK, S5_GB, S5_GROUP, S5_STATE)
    wc = jnp.einsum('ab,rjacp->jrapbc', eye, cc).reshape(S5_NBLK, 2 * S5_LANES, LANES)
    wc = wc.reshape(S5_PAIRS, 2, 2 * S5_LANES, LANES).transpose(0, 2, 1, 3).reshape(S5_PAIRS, 2 * S5_LANES, 2 * LANES)
    ab = jnp.concatenate([abar_re.reshape(S5_NBLK, S5_LANES), abar_im.reshape(S5_NBLK, S5_LANES)], axis=1)
    ab = jnp.tile(ab.reshape(S5_PAIRS, 2, 2 * S5_LANES), (1, SUBLANES // 2, 1))
    dd = jnp.tile(d_skip.reshape(S5_PAIRS, 2, LANES), (1, SUBLANES // 2, 1))
    return wb.astype(BF16), ab, wc.astype(BF16), dd


def _s5(u6, wb, ab, wc, dd, l):
    rows = S5_TC * SUBLANES
    return pl.pallas_call(
        _s5_kernel,
        grid=(S5_PAIRS, l // S5_TC),
        in_specs=[pl.BlockSpec((1, rows, LANES), lambda k, c: (k, c, 0)),
                  pl.BlockSpec((1, LANES, 4 * S5_LANES), lambda k, c: (k, 0, 0)),
                  pl.BlockSpec((1, SUBLANES, 2 * S5_LANES), lambda k, c: (k, 0, 0)),
                  pl.BlockSpec((1, 2 * S5_LANES, 2 * LANES), lambda k, c: (k, 0, 0)),
                  pl.BlockSpec((1, SUBLANES, LANES), lambda k, c: (k, 0, 0))],
        out_specs=pl.BlockSpec((1, rows, LANES), lambda k, c: (k, c, 0)),
        out_shape=jax.ShapeDtypeStruct(u6.shape, F32),
        scratch_shapes=[pltpu.VMEM((2 * S5_LANES // LANES, rows, LANES), F32),
                        pltpu.VMEM((SUBLANES, 2 * S5_LANES), F32)],
        compiler_params=_cparams(("arbitrary", "arbitrary")),
        name="s5",
    )(u6, wb, ab, wc, dd)


def _glu_kernel(y_ref, w_ref, o_ref, z_scr, *, bn):
    b = pl.program_id(1)
    rows = z_scr.shape[1]

    @pl.when(b == 0)
    def _():
        y = jnp.concatenate([y_ref.at[k][pl.ds(s, rows, stride=2), :] for k in range(S5_PAIRS) for s in range(2)],
                            axis=1)
        g = jnp.dot(y.astype(BF16), w_ref[...], preferred_element_type=F32)
        z = y * jax.nn.sigmoid(g)
        for c in range(S5_NBLK):
            z_scr[c] = z[:, c * LANES:(c + 1) * LANES]

    o_ref[0] = jnp.concatenate([z_scr.at[c][pl.ds(b, rows // bn, stride=bn), :] for c in range(S5_NBLK)],
                               axis=1).astype(o_ref.dtype)


def _glu(y6, w_glu, bn, l):
    tt = 256
    rows = tt * bn
    return pl.pallas_call(
        functools.partial(_glu_kernel, bn=bn),
        grid=(l // tt, bn),
        in_specs=[pl.BlockSpec((S5_PAIRS, rows * 2, LANES), lambda i, b: (0, i, 0)),
                  pl.BlockSpec((S5_CH, S5_CH), lambda i, b: (0, 0))],
        out_specs=pl.BlockSpec((1, tt, S5_CH), lambda i, b: (b, i, 0)),
        out_shape=jax.ShapeDtypeStruct((bn, l, S5_CH), BF16),
        scratch_shapes=[pltpu.VMEM((S5_NBLK, rows, LANES), F32)],
        compiler_params=_cparams(("arbitrary", "arbitrary")),
        name="glu",
    )(y6, w_glu)


def _merge_kernel(att_ref, z_ref, gm_ref, gs_ref, x_ref, mod_ref, gpost_ref, gpre_ref,
                  wbm_ref, wbs_ref, wo_ref, wrh_ref, wrl_ref, x1_ref, h2b_ref, h2s_ref, lg_ref):
    ym = jnp.dot(att_ref[...], wbm_ref[...], preferred_element_type=F32)
    ys = jnp.dot(z_ref[...], wbs_ref[...], preferred_element_type=F32)
    mixed_in = (gm_ref[...].astype(F32) * ym + gs_ref[...].astype(F32) * ys).astype(BF16)
    mixed = jnp.dot(mixed_in, wo_ref[...], preferred_element_type=F32)
    x1 = x_ref[0] + mod_ref[0, 2:3, :] * _rms(mixed, gpost_ref[...])
    x1_ref[0] = x1
    h2 = _rms(x1, gpre_ref[...]) * (1.0 + mod_ref[0, 4:5, :]) + mod_ref[0, 3:4, :]
    h2_hi = h2.astype(BF16)
    h2b_ref[0] = h2_hi
    tm = h2.shape[0]
    for s in range(SLAB):
        h2s_ref[pl.ds(s, tm, stride=SLAB), :] = h2[:, s * LANES:(s + 1) * LANES]
    h2_lo = (h2 - h2_hi.astype(F32)).astype(BF16)
    nt = (((1,), (1,)), ((), ()))
    lg_ref[...] = (lax.dot_general(wrh_ref[...], h2_hi, nt, preferred_element_type=F32)
                   + lax.dot_general(wrl_ref[...], h2_hi, nt, preferred_element_type=F32)
                   + lax.dot_general(wrh_ref[...], h2_lo, nt, preferred_element_type=F32))


def _merge(att, z_tm, gates, x, mod, g_post, g_pre_ffn, w_br_mla, w_br_s5, w_out, wr_hi, wr_lo):
    bn, l, d = x.shape
    tm = 256
    nl = l // tm
    row = lambda b, i: (b * nl + i, 0)
    const = lambda b, i: (0, 0)
    one = pl.Buffered(1)
    return pl.pallas_call(
        _merge_kernel,
        grid=(bn, nl),
        in_specs=[pl.BlockSpec((tm, MLA_HEADS * V_HEAD), row),
                  pl.BlockSpec((tm, S5_CH), row),
                  pl.BlockSpec((tm, d), lambda b, i: (b * nl + i, 0)),
                  pl.BlockSpec((tm, d), lambda b, i: (b * nl + i, 1)),
                  pl.BlockSpec((1, tm, d), lambda b, i: (b, i, 0)),
                  pl.BlockSpec((1, 6, d), lambda b, i: (b, 0, 0)),
                  pl.BlockSpec((1, d), const),
                  pl.BlockSpec((1, d), const),
                  pl.BlockSpec((MLA_HEADS * V_HEAD, d), const, pipeline_mode=one),
                  pl.BlockSpec((S5_CH, d), const, pipeline_mode=one),
                  pl.BlockSpec((d, d), const, pipeline_mode=one),
                  pl.BlockSpec((N_EXPERTS, d), const, pipeline_mode=one),
                  pl.BlockSpec((N_EXPERTS, d), const, pipeline_mode=one)],
        out_specs=[pl.BlockSpec((1, tm, d), lambda b, i: (b, i, 0)),
                   pl.BlockSpec((1, tm, d), lambda b, i: (b, i, 0)),
                   pl.BlockSpec((tm * SLAB, LANES), row),
                   pl.BlockSpec((N_EXPERTS, tm), lambda b, i: (0, b * nl + i))],
        out_shape=[jax.ShapeDtypeStruct((bn, l, d), F32),
                   jax.ShapeDtypeStruct((bn, l, d), BF16),
                   jax.ShapeDtypeStruct((bn * l * SLAB, LANES), F32),
                   jax.ShapeDtypeStruct((N_EXPERTS, bn * l), F32)],
        compiler_params=_cparams(("arbitrary", "arbitrary")),
        name="merge",
    )(att, z_tm, gates, gates, x, mod, g_post.reshape(1, d), g_pre_ffn.reshape(1, d),
      w_br_mla, w_br_s5, w_out, wr_hi, wr_lo)


def _route_kernel(lg_ref, b_ref, ti_ref, tw_ref, cnt_ref):
    ng = N_EXPERT_GROUPS
    gsz = N_EXPERTS // ng
    lg = lg_ref[...]
    t = lg.shape[-1]
    sc = jax.nn.sigmoid(lg)
    sel = sc + b_ref[...]
    ninf = -jnp.inf
    i_in = lax.broadcasted_iota(jnp.int32, (ng, gsz, t), 1).astype(F32)
    m1 = jnp.max(sel, axis=1, keepdims=True)
    idx1 = jnp.min(jnp.where(sel == m1, i_in, float(gsz)), axis=1, keepdims=True)
    m2 = jnp.max(jnp.where(i_in == idx1, ninf, sel), axis=1, keepdims=True)
    gs = m1 + m2
    g_i = lax.broadcasted_iota(jnp.int32, (ng, 1, t), 0).astype(F32)
    picked = jnp.zeros((ng, 1, t), F32)
    cur = gs
    for _ in range(TOPK_GROUPS):
        m = jnp.max(cur, axis=0, keepdims=True)
        idx = jnp.min(jnp.where(cur == m, g_i, float(ng)), axis=0, keepdims=True)
        hit = g_i == idx
        picked = jnp.where(hit, 1.0, picked)
        cur = jnp.where(hit, ninf, cur)
    cand = jnp.where(picked > 0.5, sel, ninf)
    e_i = lax.broadcasted_iota(jnp.int32, (ng, gsz, t), 0).astype(F32) * float(gsz) + i_in
    ws = []
    hits = jnp.zeros((ng, gsz, t), F32)
    for r in range(TOP_K):
        m = jnp.max(jnp.max(cand, axis=1, keepdims=True), axis=0, keepdims=True)
        idx = jnp.min(jnp.min(jnp.where(cand == m, e_i, float(N_EXPERTS)), axis=1, keepdims=True),
                      axis=0, keepdims=True)
        hit = e_i == idx
        w = jnp.sum(jnp.sum(jnp.where(hit, sc, 0.0), axis=1, keepdims=True), axis=0, keepdims=True)
        ti_ref[r:r + 1, :] = idx[0].astype(jnp.int32)
        ws.append(w[0])
        hits = jnp.where(hit, 1.0, hits)
        cand = jnp.where(hit, ninf, cand)

    @pl.when(pl.program_id(0) == 0)
    def _():
        cnt_ref[...] = jnp.zeros(cnt_ref.shape, F32)

    cnt_ref[...] += jnp.sum(hits, axis=2, keepdims=True)
    tot = ws[0]
    for w in ws[1:]:
        tot = tot + w
    for r in range(TOP_K):
        tw_ref[r:r + 1, :] = ws[r] / tot * ROUTED_SCALE
    for r in range(TOP_K, SUBLANES):
        ti_ref[r:r + 1, :] = jnp.zeros((1, t), jnp.int32)
        tw_ref[r:r + 1, :] = jnp.zeros((1, t), F32)


def _route(logits_t, router_bias):
    n = logits_t.shape[1]
    ng = N_EXPERT_GROUPS
    gsz = N_EXPERTS // ng
    tn = 1024
    return pl.pallas_call(
        _route_kernel,
        grid=(n // tn,),
        in_specs=[pl.BlockSpec((ng, gsz, tn), lambda i: (0, 0, i)),
                  pl.BlockSpec((ng, gsz, 1), lambda i: (0, 0, 0))],
        out_specs=[pl.BlockSpec((SUBLANES, tn), lambda i: (0, i)),
                   pl.BlockSpec((SUBLANES, tn), lambda i: (0, i)),
                   pl.BlockSpec((ng, gsz, 1), lambda i: (0, 0, 0))],
        out_shape=[jax.ShapeDtypeStruct((SUBLANES, n), jnp.int32),
                   jax.ShapeDtypeStruct((SUBLANES, n), F32),
                   jax.ShapeDtypeStruct((ng, gsz, 1), F32)],
        compiler_params=_cparams(("arbitrary",)),
        name="route",
    )(logits_t.reshape(ng, gsz, n), router_bias.reshape(ng, gsz, 1))


def _moe_kernel(blk_e_ref, nused_ref, nvalid_ref, nxt_e_ref, wslot_ref, t_ref, tn_ref, a_ref, h_hbm,
                wg_hbm, wu_hbm, wd_hbm, y_hbm, xbuf, ybuf, wg_buf, wu_buf, wd_buf, wg_s, wu_s, wd_s,
                gsem, wsem, wtsem):
    j = pl.program_id(0)
    n_used = nused_ref[0]
    slot = lax.rem(j, 2)
    nv = nvalid_ref[j]
    nv_prev = nvalid_ref[jnp.maximum(j - 1, 0)]

    def slab(ref, row0):
        return ref.at[pl.ds(pl.multiple_of(row0, SUBLANES), SLAB), :]

    def start_gather(idx_ref, s):
        def body(r, _):
            pltpu.make_async_copy(slab(h_hbm, idx_ref[0, 0, r]), slab(xbuf.at[s], r * MOE_PITCH), gsem.at[s]).start()
            return 0
        lax.fori_loop(0, MOE_TB, body, 0, unroll=8)

    def wait_gather(s):
        rows = MOE_TB * SLAB
        pltpu.make_async_copy(h_hbm.at[pl.ds(0, rows), :], xbuf.at[s, pl.ds(0, rows), :], gsem.at[s]).wait()

    def weight_copies(expert, s):
        return (pltpu.make_async_copy(wg_hbm.at[expert], wg_buf.at[s], wtsem.at[s]),
                pltpu.make_async_copy(wu_hbm.at[expert], wu_buf.at[s], wtsem.at[s]),
                pltpu.make_async_copy(wd_hbm.at[expert], wd_buf.at[s], wtsem.at[s]))

    def wait_writes(count):
        p = MOE_TB
        while p >= 1:
            @pl.when((count & p) != 0)
            def _(p=p):
                rows = p * SLAB
                pltpu.make_async_copy(ybuf.at[pl.ds(0, rows), :], y_hbm.at[pl.ds(0, rows), :], wsem.at[0]).wait()
            p //= 2

    @pl.when(j < n_used)
    def _():
        @pl.when(j == 0)
        def _():
            start_gather(t_ref, 0)

        e = blk_e_ref[j]
        ws = wslot_ref[j]

        @pl.when(j == 0)
        def _():
            for cp in weight_copies(e, ws):
                cp.start(priority=WEIGHT_DMA_PRIORITY)

        first = jnp.logical_or(j == 0, e != blk_e_ref[jnp.maximum(j - 1, 0)])

        @pl.when(first)
        def _():
            for cp in weight_copies(e, ws):
                cp.wait()
            wg_s[...] = wg_buf[ws].astype(BF16)
            wu_s[...] = wu_buf[ws].astype(BF16)
            wd_s[...] = wd_buf[ws].astype(BF16)

        wait_gather(slot)

        @pl.when(j + 1 < n_used)
        def _():
            start_gather(tn_ref, 1 - slot)

        @pl.when(jnp.logical_and(first, nxt_e_ref[j] >= 0))
        def _():
            for cp in weight_copies(nxt_e_ref[j], 1 - ws):
                cp.start(priority=WEIGHT_DMA_PRIORITY)

        xs = xbuf.at[slot]
        x = jnp.concatenate([xs[pl.ds(s, MOE_TB, stride=MOE_PITCH), :] for s in range(SLAB)], axis=1).astype(BF16)
        g = jnp.dot(x, wg_s[...], preferred_element_type=F32)
        u = jnp.dot(x, wu_s[...], preferred_element_type=F32)
        hm = (g * jax.nn.sigmoid(g) * u).astype(BF16)
        y = jnp.dot(hm, wd_s[...], preferred_element_type=F32)

        @pl.when(j > 0)
        def _():
            wait_writes(nv_prev)

        for s in range(SLAB):
            ybuf[pl.ds(s, MOE_TB, stride=MOE_PITCH), :] = y[:, s * LANES:(s + 1) * LANES]

        def start_write(r):
            pltpu.make_async_copy(slab(ybuf, r * MOE_PITCH), slab(y_hbm, a_ref[0, 0, r]), wsem.at[0]).start()

        def wgroup(g8, _):
            for q in range(SUBLANES):
                start_write(g8 * SUBLANES + q)
            return 0

        def wtail(r, _):
            start_write(r)
            return 0

        n_groups = lax.shift_right_logical(nv, 3)
        lax.fori_loop(0, n_groups, wgroup, 0)
        lax.fori_loop(n_groups * SUBLANES, nv, wtail, 0)

        @pl.when(j == n_used - 1)
        def _():
            wait_writes(nv)


def _moe(h2s, topi, counts, w_gate, w_up, w_down):
    n = h2s.shape[0] // SLAB
    d = D_MODEL
    nk = n * TOP_K
    tb = MOE_TB
    n_blocks = nk // tb + N_EXPERTS
    flat_e = topi[:TOP_K].reshape(nk)
    _, order = lax.sort_key_val(flat_e, jnp.arange(nk, dtype=jnp.int32))
    start = jnp.cumsum(counts) - counts
    padded = (counts + tb - 1) // tb * tb
    pad_end = jnp.cumsum(padded)
    pad_start = pad_end - padded
    n_used = (pad_end[-1] // tb).astype(jnp.int32).reshape(1)
    blk_p0 = jnp.arange(n_blocks, dtype=jnp.int32) * tb
    blk_e = jnp.minimum(jnp.sum((pad_end[None, :] <= blk_p0[:, None]).astype(jnp.int32), axis=1), N_EXPERTS - 1)
    blk_off = blk_p0 - pad_start[blk_e]
    nvalid = jnp.clip(counts[blk_e] - blk_off, 0, tb).astype(jnp.int32)
    rank = (start[blk_e] + blk_off)[:, None] + jnp.arange(tb, dtype=jnp.int32)[None, :]
    a3 = order[jnp.clip(rank, 0, nk - 1)].reshape(n_blocks, 1, tb)
    arow = a3 * SLAB
    trow = (a3 % n) * SLAB
    used = counts > 0
    e_ids = jnp.arange(N_EXPERTS, dtype=jnp.int32)
    later = jnp.where(used, e_ids, N_EXPERTS)
    nxt = jnp.concatenate([lax.cummin(later[::-1])[::-1][1:], jnp.full((1,), N_EXPERTS, jnp.int32)])
    nxt_e = jnp.where(nxt < N_EXPERTS, nxt, -1)[blk_e].astype(jnp.int32)
    wslot = ((jnp.cumsum(used.astype(jnp.int32)) - 1) % 2)[blk_e].astype(jnp.int32)

    smem_blk = lambda f: pl.BlockSpec((1, 1, tb), f, memory_space=pltpu.SMEM)
    any_spec = pl.BlockSpec(memory_space=pl.ANY)
    grid_spec = pltpu.PrefetchScalarGridSpec(
        num_scalar_prefetch=5,
        grid=(n_blocks,),
        in_specs=[smem_blk(lambda j, *_: (j, 0, 0)),
                  smem_blk(lambda j, *_: (jnp.minimum(j + 1, n_blocks - 1), 0, 0)),
                  smem_blk(lambda j, *_: (j, 0, 0)),
                  any_spec, any_spec, any_spec, any_spec],
        out_specs=any_spec,
        scratch_shapes=[pltpu.VMEM((2, tb * MOE_PITCH, LANES), F32),
                        pltpu.VMEM((tb * MOE_PITCH, LANES), F32),
                        pltpu.VMEM((2, d, D_EXPERT), F32),
                        pltpu.VMEM((2, d, D_EXPERT), F32),
                        pltpu.VMEM((2, D_EXPERT, d), F32),
                        pltpu.VMEM((d, D_EXPERT), BF16),
                        pltpu.VMEM((d, D_EXPERT), BF16),
                        pltpu.VMEM((D_EXPERT, d), BF16),
                        pltpu.SemaphoreType.DMA((2,)),
                        pltpu.SemaphoreType.DMA((1,)),
                        pltpu.SemaphoreType.DMA((2,))],
    )
    return pl.pallas_call(
        _moe_kernel,
        grid_spec=grid_spec,
        out_shape=jax.ShapeDtypeStruct((TOP_K * n * SLAB, LANES), F32),
        compiler_params=_cparams(("arbitrary",)),
        name="moe",
    )(blk_e, n_used, nvalid, nxt_e, wslot, trow, trow, arow, h2s, w_gate, w_up, w_down)


def _final_kernel(h_ref, y0, y1, y2, y3, y4, y5, tw_ref, x1_ref, mod_ref, g_ref, wg_ref, wu_ref, wd_ref, o_ref):
    h = h_ref[...]
    tm = h.shape[0]
    g = jnp.dot(h, wg_ref[...], preferred_element_type=F32)
    u = jnp.dot(h, wu_ref[...], preferred_element_type=F32)
    hm = (g * jax.nn.sigmoid(g) * u).astype(BF16)
    ffn = jnp.dot(hm, wd_ref[...], preferred_element_type=F32)
    tw = tw_ref[...]
    for k, y in enumerate((y0, y1, y2, y3, y4, y5)):
        yk = jnp.concatenate([y[pl.ds(s, tm, stride=SLAB), :] for s in range(SLAB)], axis=1)
        ffn = ffn + yk * tw[:, k:k + 1]
    o_ref[0] = x1_ref[0] + mod_ref[0, 5:6, :] * _rms(ffn, g_ref[...])


def _final(h2, y_flat, topw_t, x1, mod, g_post_ffn, w_sg, w_su, w_sd):
    bn, l, d = x1.shape
    n = bn * l
    tm = 256
    nl = l // tm
    const = lambda b, i: (0, 0)
    one = pl.Buffered(1)
    y_specs = [pl.BlockSpec((tm * SLAB, LANES), functools.partial(lambda b, i, k: (k * (n // tm) + b * nl + i, 0), k=k))
               for k in range(TOP_K)]
    return pl.pallas_call(
        _final_kernel,
        grid=(bn, nl),
        in_specs=[pl.BlockSpec((tm, d), lambda b, i: (b * nl + i, 0))] + y_specs + [
            pl.BlockSpec((tm, SUBLANES), lambda b, i: (b * nl + i, 0)),
            pl.BlockSpec((1, tm, d), lambda b, i: (b, i, 0)),
            pl.BlockSpec((1, 6, d), lambda b, i: (b, 0, 0)),
            pl.BlockSpec((1, d), const),
            pl.BlockSpec((d, D_EXPERT), const, pipeline_mode=one),
            pl.BlockSpec((d, D_EXPERT), const, pipeline_mode=one),
            pl.BlockSpec((D_EXPERT, d), const, pipeline_mode=one)],
        out_specs=pl.BlockSpec((1, tm, d), lambda b, i: (b, i, 0)),
        out_shape=jax.ShapeDtypeStruct((bn, l, d), F32),
        compiler_params=_cparams(("arbitrary", "arbitrary")),
        name="final",
    )(h2, *([y_flat] * TOP_K), topw_t, x1, mod, g_post_ffn.reshape(1, d), w_sg, w_su, w_sd)


def _layer(li, x, c, positions, w_ada, b_ada, g_pre_mix, g_post_mix, g_pre_ffn, g_post_ffn, w_in, g_q, g_kv,
           w_uq, w_uk, w_uv, a_re, a_im, log_dt, b_re, b_im, c_re, c_im, d_skip, w_glu, w_br_mla, w_br_s5,
           w_out, w_router, router_bias, w_exp_gate, w_exp_up, w_exp_down, w_sh_gate, w_sh_up, w_sh_down):
    bn, l, d = x.shape
    n = bn * l
    mod = _adaln(c, w_ada, b_ada).reshape(bn, 6, d)

    o_kpe = Q_LORA + KV_LORA
    o_u = o_kpe + QK_ROPE
    o_g = o_u + S5_CH
    w_in_t = jnp.swapaxes(w_in, 1, 2)
    w_q = jnp.pad(w_uq.reshape(Q_LORA, MLA_HEADS, QK_NOPE + QK_ROPE),
                  ((0, 0), (0, 0), (0, QK_PAD - QK_NOPE - QK_ROPE))).reshape(Q_LORA, MLA_HEADS * QK_PAD).astype(BF16)
    wr_t = w_router.T
    wr_hi = wr_t.astype(BF16)
    wr_lo = (wr_t - wr_hi.astype(F32)).astype(BF16)

    rope_c, rope_s1, rope_s2 = _rope_tables(positions)
    chunk_id = positions // CHUNK

    h = _prenorm(x, mod, g_pre_mix, 0, 1).reshape(n, d)
    lat = _mmt(h, w_in_t, li, 0, o_kpe, F32, 1024, 512, name="mm_lat")
    kpe = _mmt(h, w_in_t, li, o_kpe, LANES, F32, 1024, LANES, name="mm_kpe")
    gates = _mmt(h, w_in_t, li, o_g, 2 * d, BF16, 1024, 1024, act="sigmoid", name="mm_gates")
    u6 = _mm_u(h, w_in_t, li, o_u, bn, l)

    q, k, v = _qkvproj(lat, g_q, g_kv, w_q, w_uk.astype(BF16), w_uv.astype(BF16), kpe, rope_c, rope_s1, rope_s2)
    att = _attention(q, k, v, chunk_id, bn, l)

    wb, ab, wc, dd = _s5_params(a_re, a_im, log_dt, b_re, b_im, c_re, c_im, d_skip)
    y6 = _s5(u6, wb, ab, wc, dd, l)
    z = _glu(y6, w_glu.astype(BF16), bn, l).reshape(n, S5_CH)

    x1, h2b, h2s, logits_t = _merge(att, z, gates, x, mod, g_post_mix, g_pre_ffn, w_br_mla.astype(BF16),
                                    w_br_s5.astype(BF16), w_out.astype(BF16), wr_hi, wr_lo)

    topi, topw, cnt = _route(logits_t, router_bias)
    y_slabs = _moe(h2s, topi, cnt.reshape(N_EXPERTS).astype(jnp.int32), w_exp_gate, w_exp_up, w_exp_down)
    return _final(h2b.reshape(n, d), y_slabs, topw.T, x1, mod, g_post_ffn, w_sh_gate.astype(BF16),
                  w_sh_up.astype(BF16), w_sh_down.astype(BF16))


def kernel(x, c, positions, w_ada, b_ada, g_pre_mix, g_post_mix, g_pre_ffn, g_post_ffn, w_in, g_q, g_kv, w_uq, w_uk, w_uv, a_re, a_im, log_dt, b_re, b_im, c_re, c_im, d_skip, w_glu, w_br_mla, w_br_s5, w_out, w_router, router_bias, w_exp_gate, w_exp_up, w_exp_down, w_sh_gate, w_sh_up, w_sh_down):
    depth = w_ada.shape[0]
    for li in range(depth):
        x = _layer(li, x, c, positions, w_ada[li], b_ada[li], g_pre_mix[li], g_post_mix[li], g_pre_ffn[li],
                   g_post_ffn[li], w_in, g_q[li], g_kv[li], w_uq[li], w_uk[li], w_uv[li], a_re[li], a_im[li],
                   log_dt[li], b_re[li], b_im[li], c_re[li], c_im[li], d_skip[li], w_glu[li], w_br_mla[li],
                   w_br_s5[li], w_out[li], w_router[li], router_bias[li], w_exp_gate[li], w_exp_up[li],
                   w_exp_down[li], w_sh_gate[li], w_sh_up[li], w_sh_down[li])
    return x
```

```python
import functools

import jax
import jax.numpy as jnp
from jax import lax
from jax.experimental import pallas as pl
from jax.experimental.pallas import tpu as pltpu

F32 = jnp.float32
BF16 = jnp.bfloat16

D_MODEL = 2048
CHUNK = 64
EPS = 1e-6
MLA_HEADS = 8
QK_NOPE = 128
QK_ROPE = 64
V_HEAD = 128
Q_LORA = 512
KV_LORA = 512
ROPE_THETA = 10000.0
S5_CH = 1024
S5_GROUP = 16
S5_GROUPS = S5_CH // S5_GROUP
S5_STATE = 64
N_EXPERTS = 64
TOP_K = 6
N_EXPERT_GROUPS = 8
TOPK_GROUPS = 4
D_EXPERT = 512
ROUTED_SCALE = 2.5

LANES = 128
SUBLANES = 8
QK_PAD = 2 * LANES
VMEM_LIMIT = 56 * 1024 * 1024
NEG = -1e30
LOG2E = 1.4426950408889634

S5_GB = LANES // S5_GROUP
S5_NBLK = S5_CH // LANES
S5_PAIRS = S5_NBLK // 2
S5_LANES = S5_GB * S5_STATE
S5_TC = 256
S5_RB = 512

ATT_TQ = 256
ATT_TK = 256
MOE_TB = 256
SLAB = D_MODEL // LANES
GATHER_SLOTS = 3
WEIGHT_DMA_PRIORITY = 1
MOE_PITCH = 24


def _cparams(sem):
    return pltpu.CompilerParams(dimension_semantics=sem, vmem_limit_bytes=VMEM_LIMIT)


def _rms(x, g):
    return x * lax.rsqrt(jnp.mean(x * x, axis=-1, keepdims=True) + EPS) * g


def _adaln_kernel(c_ref, w_ref, b_ref, o_ref):
    c = c_ref[...]
    a = (c * jax.nn.sigmoid(c)).astype(BF16)
    o_ref[...] = jnp.dot(a, w_ref[...].astype(BF16), preferred_element_type=F32) + b_ref[...]


def _adaln(c, w, b):
    bn, d = c.shape
    n = w.shape[1]
    tn = 1024
    return pl.pallas_call(
        _adaln_kernel,
        grid=(n // tn,),
        in_specs=[pl.BlockSpec((bn, d), lambda j: (0, 0)),
                  pl.BlockSpec((d, tn), lambda j: (0, j)),
                  pl.BlockSpec((1, tn), lambda j: (0, j))],
        out_specs=pl.BlockSpec((bn, tn), lambda j: (0, j)),
        out_shape=jax.ShapeDtypeStruct((bn, n), F32),
        compiler_params=_cparams(("arbitrary",)),
        name="adaln",
    )(c, w, b.reshape(1, n))


def _prenorm_kernel(x_ref, mod_ref, g_ref, o_ref, *, sh_row, sc_row):
    y = _rms(x_ref[0], g_ref[...])
    o_ref[0] = (y * (1.0 + mod_ref[0, sc_row:sc_row + 1, :]) + mod_ref[0, sh_row:sh_row + 1, :]).astype(o_ref.dtype)


def _prenorm(x, mod, g, sh_row, sc_row):
    bn, l, d = x.shape
    tl = 512
    return pl.pallas_call(
        functools.partial(_prenorm_kernel, sh_row=sh_row, sc_row=sc_row),
        grid=(bn, l // tl),
        in_specs=[pl.BlockSpec((1, tl, d), lambda b, i: (b, i, 0)),
                  pl.BlockSpec((1, 6, d), lambda b, i: (b, 0, 0)),
                  pl.BlockSpec((1, d), lambda b, i: (0, 0))],
        out_specs=pl.BlockSpec((1, tl, d), lambda b, i: (b, i, 0)),
        out_shape=jax.ShapeDtypeStruct((bn, l, d), BF16),
        compiler_params=_cparams(("arbitrary", "arbitrary")),
        name="prenorm",
    )(x, mod, g.reshape(1, d))


NT_DIMS = (((1,), (1,)), ((), ()))


def _wt_tile(wa_ref, wb_ref, shift, tn):
    if shift == 0:
        return wa_ref[...].astype(BF16)
    return jnp.concatenate([wa_ref[...], wb_ref[...]], axis=0)[shift:shift + tn].astype(BF16)


def _wt_specs(layer, row0, tn, k, jmap):
    shift = row0 % LANES
    c0 = row0 - shift
    assert c0 % tn == 0 and shift % SUBLANES == 0
    specs = [pl.BlockSpec((None, tn, k), lambda *g: (layer, jmap(*g) + c0 // tn, 0))]
    if shift:
        specs.append(pl.BlockSpec((None, LANES, k), lambda *g: (layer, (c0 + (jmap(*g) + 1) * tn) // LANES, 0)))
    return specs, shift


def _mmt_kernel(a_ref, wa_ref, *rest, act, shift):
    wb_ref = rest[0] if shift else None
    o_ref, w_scr = rest[-2:]

    @pl.when(pl.program_id(1) == 0)
    def _():
        w_scr[...] = _wt_tile(wa_ref, wb_ref, shift, w_scr.shape[0])

    acc = lax.dot_general(a_ref[...], w_scr[...], NT_DIMS, preferred_element_type=F32)
    if act == "sigmoid":
        acc = jax.nn.sigmoid(acc)
    o_ref[...] = acc.astype(o_ref.dtype)


def _mmt(a, w_t, layer, row0, n, out_dtype, tm, tn, act=None, name="mmt"):
    m, k = a.shape
    w_specs, shift = _wt_specs(layer, row0, tn, k, lambda j, i: j)
    return pl.pallas_call(
        functools.partial(_mmt_kernel, act=act, shift=shift),
        grid=(n // tn, m // tm),
        in_specs=[pl.BlockSpec((tm, k), lambda j, i: (i, 0))] + w_specs,
        out_specs=pl.BlockSpec((tm, tn), lambda j, i: (i, j)),
        out_shape=jax.ShapeDtypeStruct((m, n), out_dtype),
        scratch_shapes=[pltpu.VMEM((tn, k), BF16)],
        compiler_params=_cparams(("arbitrary", "arbitrary")),
        name=name,
    )(a, *([w_t] * len(w_specs)))


def _mm_u_kernel(a_ref, wa_ref, *rest, shift):
    wb_ref = rest[0] if shift else None
    o_ref, w_scr = rest[-2:]
    b = pl.program_id(1)

    @pl.when(jnp.logical_and(pl.program_id(0) == 0, b == 0))
    def _():
        w_scr[...] = _wt_tile(wa_ref, wb_ref, shift, w_scr.shape[0])

    res = lax.dot_general(a_ref[...], w_scr[...], NT_DIMS, preferred_element_type=F32)
    tm = res.shape[0]
    for c in range(S5_NBLK):
        o_ref.at[c // 2][pl.ds(b * 2 + c % 2, tm, stride=SUBLANES), :] = res[:, c * LANES:(c + 1) * LANES]


def _mm_u(h, w_t, layer, row0, bn, l):
    m, k = h.shape
    tm = 512
    nl = l // tm
    assert bn * 2 == SUBLANES
    w_specs, shift = _wt_specs(layer, row0, S5_CH, k, lambda i, b: 0)
    return pl.pallas_call(
        functools.partial(_mm_u_kernel, shift=shift),
        grid=(nl, bn),
        in_specs=[pl.BlockSpec((tm, k), lambda i, b: (b * nl + i, 0))] + w_specs,
        out_specs=pl.BlockSpec((S5_PAIRS, tm * SUBLANES, LANES), lambda i, b: (0, i, 0)),
        out_shape=jax.ShapeDtypeStruct((S5_PAIRS, l * SUBLANES, LANES), F32),
        scratch_shapes=[pltpu.VMEM((S5_CH, k), BF16)],
        compiler_params=_cparams(("arbitrary", "arbitrary")),
        name="mm_u",
    )(h, *([w_t] * len(w_specs)))


def _rope_tab_kernel(pos_ref, k_ref, c_ref, s1_ref, s2_ref):
    ang = pos_ref[...].astype(F32) * k_ref[0:1, :]
    s = jnp.sin(ang)
    c_ref[...] = jnp.cos(ang) * k_ref[1:2, :]
    s1_ref[...] = s * k_ref[2:3, :]
    s2_ref[...] = s * k_ref[3:4, :]


def _rope_tables(positions):
    n = positions.size
    half = QK_ROPE // 2
    inv_freq = ROPE_THETA ** (-jnp.arange(half, dtype=F32) / half)
    zh, oh = jnp.zeros((half,), F32), jnp.ones((half,), F32)
    z2 = jnp.zeros((LANES - QK_ROPE,), F32)
    rows = [jnp.concatenate([inv_freq, inv_freq, z2]), jnp.concatenate([oh, oh, z2]),
            jnp.concatenate([-oh, zh, z2]), jnp.concatenate([zh, oh, z2])]
    consts = jnp.stack(rows + [jnp.zeros((LANES,), F32)] * (SUBLANES - len(rows)))
    tm = 1024
    tab = jax.ShapeDtypeStruct((n, LANES), F32)
    return pl.pallas_call(
        _rope_tab_kernel,
        grid=(n // tm,),
        in_specs=[pl.BlockSpec((tm, 1), lambda i: (i, 0)),
                  pl.BlockSpec((SUBLANES, LANES), lambda i: (0, 0))],
        out_specs=[pl.BlockSpec((tm, LANES), lambda i: (i, 0))] * 3,
        out_shape=[tab, tab, tab],
        compiler_params=_cparams(("arbitrary",)),
        name="rope_tables",
    )(positions.reshape(n, 1), consts)


def _rope_tile(t, c_ref, s1_ref, s2_ref):
    return (t * c_ref[...] + pltpu.roll(t, LANES - QK_ROPE // 2, 1) * s1_ref[...]
            + pltpu.roll(t, QK_ROPE // 2, 1) * s2_ref[...])


def _qkvproj_kernel(lat_ref, gq_ref, gkv_ref, wq_ref, wk_ref, wv_ref, kpe_ref, c_ref, s1_ref, s2_ref,
                    q_ref, k_ref, v_ref, *, scale):
    lat = lat_ref[...]
    qn = _rms(lat[:, :Q_LORA], gq_ref[...]).astype(BF16)
    cn = _rms(lat[:, Q_LORA:], gkv_ref[...]).astype(BF16)
    q = jnp.dot(qn, wq_ref[...], preferred_element_type=F32)
    kn = jnp.dot(cn, wk_ref[...], preferred_element_type=F32)
    v_ref[...] = jnp.dot(cn, wv_ref[...], preferred_element_type=F32).astype(v_ref.dtype)
    kt = _rope_tile(kpe_ref[...], c_ref, s1_ref, s2_ref).astype(k_ref.dtype)
    for h in range(MLA_HEADS):
        o = h * QK_PAD
        q_ref[:, o:o + LANES] = (q[:, o:o + LANES] * scale).astype(q_ref.dtype)
        qt = _rope_tile(q[:, o + LANES:o + QK_PAD], c_ref, s1_ref, s2_ref)
        q_ref[:, o + LANES:o + QK_PAD] = (qt * scale).astype(q_ref.dtype)
        k_ref[:, o:o + LANES] = kn[:, h * QK_NOPE:(h + 1) * QK_NOPE].astype(k_ref.dtype)
        k_ref[:, o + LANES:o + QK_PAD] = kt


def _qkvproj(lat, g_q, g_kv, w_q, w_k, w_v, kpe, rope_c, rope_s1, rope_s2):
    n = lat.shape[0]
    tm = 512
    row = lambda i: (i, 0)
    const = lambda i: (0, 0)
    tab = pl.BlockSpec((tm, LANES), row)
    return pl.pallas_call(
        functools.partial(_qkvproj_kernel, scale=(QK_NOPE + QK_ROPE) ** -0.5 * LOG2E),
        grid=(n // tm,),
        in_specs=[pl.BlockSpec((tm, Q_LORA + KV_LORA), row),
                  pl.BlockSpec((1, Q_LORA), const),
                  pl.BlockSpec((1, KV_LORA), const),
                  pl.BlockSpec((Q_LORA, MLA_HEADS * QK_PAD), const),
                  pl.BlockSpec((KV_LORA, MLA_HEADS * QK_NOPE), const),
                  pl.BlockSpec((KV_LORA, MLA_HEADS * V_HEAD), const),
                  tab, tab, tab, tab],
        out_specs=[pl.BlockSpec((tm, MLA_HEADS * QK_PAD), row),
                   pl.BlockSpec((tm, MLA_HEADS * QK_PAD), row),
                   pl.BlockSpec((tm, MLA_HEADS * V_HEAD), row)],
        out_shape=[jax.ShapeDtypeStruct((n, MLA_HEADS * QK_PAD), BF16),
                   jax.ShapeDtypeStruct((n, MLA_HEADS * QK_PAD), BF16),
                   jax.ShapeDtypeStruct((n, MLA_HEADS * V_HEAD), BF16)],
        compiler_params=_cparams(("arbitrary",)),
        name="qkvproj",
    )(lat, g_q.reshape(1, Q_LORA), g_kv.reshape(1, KV_LORA), w_q, w_k, w_v, kpe, rope_c, rope_s1, rope_s2)


def _attn_kernel(lo_ref, hi_ref, q_ref, k_ref, v_ref, qc_ref, kc_ref, o_ref, m_scr, l_scr, acc_scr, *, nq):
    b = pl.program_id(0)
    i = pl.program_id(1)
    qc = qc_ref[...]
    m_scr[...] = jnp.full(m_scr.shape, NEG, F32)
    l_scr[...] = jnp.zeros(l_scr.shape, F32)
    acc_scr[...] = jnp.zeros(acc_scr.shape, F32)

    def make_body(masked):
        def body(j, _):
            off = pl.multiple_of(j * ATT_TK, ATT_TK)
            if masked:
                mask = kc_ref[j] <= qc
            for h in range(MLA_HEADS):
                q = q_ref[:, h * QK_PAD:(h + 1) * QK_PAD]
                k = k_ref[pl.ds(off, ATT_TK), h * QK_PAD:(h + 1) * QK_PAD]
                s = lax.dot_general(q, k, (((1,), (1,)), ((), ())), preferred_element_type=F32)
                if masked:
                    s = jnp.where(mask, s, NEG)
                m_old = m_scr[h]
                m_new = jnp.maximum(m_old, jnp.max(s, axis=-1, keepdims=True))
                p = jnp.exp2(s - jnp.concatenate([m_new] * (ATT_TK // LANES), axis=1))
                alpha = jnp.exp2(m_old - m_new)
                l_scr[h] = alpha * l_scr[h] + jnp.sum(p, axis=-1, keepdims=True)
                v = v_ref[pl.ds(off, ATT_TK), h * V_HEAD:(h + 1) * V_HEAD]
                acc_scr[h] = alpha * acc_scr[h] + jnp.dot(p.astype(BF16), v, preferred_element_type=F32)
                m_scr[h] = m_new
            return 0
        return body

    lo = lo_ref[b * nq + i]
    lax.fori_loop(0, lo, make_body(False), 0)
    lax.fori_loop(lo, hi_ref[b * nq + i], make_body(True), 0)
    for h in range(MLA_HEADS):
        o_ref[:, h * V_HEAD:(h + 1) * V_HEAD] = (acc_scr[h] / l_scr[h]).astype(o_ref.dtype)


def _attention(q, k, v, chunk_id, bn, l):
    nq = l // ATT_TQ
    nk = l // ATT_TK
    q_max = jnp.max(chunk_id.reshape(bn, nq, ATT_TQ), axis=-1)
    k_min = jnp.min(chunk_id.reshape(bn, nk, ATT_TK), axis=-1)
    needed = k_min[:, None, :] <= q_max[:, :, None]
    hi = jnp.max(jnp.where(needed, jnp.arange(1, nk + 1, dtype=jnp.int32), 0), axis=-1).reshape(bn * nq)
    q_min = jnp.min(chunk_id.reshape(bn, nq, ATT_TQ), axis=-1)
    k_max = jnp.max(chunk_id.reshape(bn, nk, ATT_TK), axis=-1)
    full = k_max[:, None, :] <= q_min[:, :, None]
    lo = jnp.min(jnp.where(full, nk, jnp.arange(nk, dtype=jnp.int32)), axis=-1).astype(jnp.int32).reshape(bn * nq)
    hi = jnp.maximum(hi, lo)
    qc = chunk_id.reshape(bn * l, 1)
    kc = chunk_id.reshape(bn * nk, 1, ATT_TK)
    grid_spec = pltpu.PrefetchScalarGridSpec(
        num_scalar_prefetch=2,
        grid=(bn, nq),
        in_specs=[pl.BlockSpec((ATT_TQ, MLA_HEADS * QK_PAD), lambda b, i, *_: (b * nq + i, 0)),
                  pl.BlockSpec((l, MLA_HEADS * QK_PAD), lambda b, i, *_: (b, 0)),
                  pl.BlockSpec((l, MLA_HEADS * V_HEAD), lambda b, i, *_: (b, 0)),
                  pl.BlockSpec((ATT_TQ, 1), lambda b, i, *_: (b * nq + i, 0)),
                  pl.BlockSpec((nk, 1, ATT_TK), lambda b, i, *_: (b, 0, 0))],
        out_specs=pl.BlockSpec((ATT_TQ, MLA_HEADS * V_HEAD), lambda b, i, *_: (b * nq + i, 0)),
        scratch_shapes=[pltpu.VMEM((MLA_HEADS, ATT_TQ, LANES), F32),
                        pltpu.VMEM((MLA_HEADS, ATT_TQ, LANES), F32),
                        pltpu.VMEM((MLA_HEADS, ATT_TQ, V_HEAD), F32)],
    )
    return pl.pallas_call(
        functools.partial(_attn_kernel, nq=nq),
        grid_spec=grid_spec,
        out_shape=jax.ShapeDtypeStruct((bn * l, MLA_HEADS * V_HEAD), BF16),
        compiler_params=_cparams(("arbitrary", "arbitrary")),
        name="attention",
    )(lo, hi, q, k, v, qc, kc)


def _s5_kernel(u_ref, wb_ref, a_ref, wc_ref, d_ref, o_ref, x_scr, st_scr):
    rows = S5_TC * SUBLANES
    nsub = rows // S5_RB

    @pl.when(pl.program_id(1) == 0)
    def _():
        st_scr[...] = jnp.zeros_like(st_scr)

    even = (lax.broadcasted_iota(jnp.int32, (S5_RB, 1), 0) & 1) == 0
    half = S5_RB // 2
    nre = S5_LANES // LANES
    uv = u_ref.at[0]

    def mm_in(r, _):
        off = pl.multiple_of(r * S5_RB, S5_RB)
        for s in range(2):
            us = uv[pl.ds(off + s, half, stride=2), :].astype(BF16)
            out = jnp.dot(us, wb_ref[0, :, s * 2 * S5_LANES:(s + 1) * 2 * S5_LANES], preferred_element_type=F32)
            for c in range(2 * nre):
                x_scr.at[c][pl.ds(off + s, half, stride=2), :] = out[:, c * LANES:(c + 1) * LANES]
        return 0

    lax.fori_loop(0, nsub, mm_in, 0)

    a_c = [a_ref[0, :, c * LANES:(c + 1) * LANES] for c in range(2 * nre)]

    def step(t, carry):
        off = pl.multiple_of(t * SUBLANES, SUBLANES)
        new = [None] * (2 * nre)
        for c in range(nre):
            xr, xi = carry[c], carry[nre + c]
            ar, ai = a_c[c], a_c[nre + c]
            nr = ar * xr - ai * xi + x_scr[c, pl.ds(off, SUBLANES), :]
            ni = ar * xi + ai * xr + x_scr[nre + c, pl.ds(off, SUBLANES), :]
            x_scr[c, pl.ds(off, SUBLANES), :] = nr
            x_scr[nre + c, pl.ds(off, SUBLANES), :] = ni
            new[c], new[nre + c] = nr, ni
        return tuple(new)

    init = tuple(st_scr[:, c * LANES:(c + 1) * LANES] for c in range(2 * nre))
    fin = lax.fori_loop(0, S5_TC, step, init, unroll=8)
    for c in range(2 * nre):
        st_scr[:, c * LANES:(c + 1) * LANES] = fin[c]

    d = jnp.concatenate([d_ref[0]] * (S5_RB // SUBLANES), axis=0)

    def mm_out(r, _):
        off = pl.multiple_of(r * S5_RB, S5_RB)
        x = jnp.concatenate([x_scr[c, pl.ds(off, S5_RB), :] for c in range(2 * nre)], axis=1).astype(BF16)
        out = jnp.dot(x, wc_ref[0], preferred_element_type=F32)
        y = jnp.where(even, out[:, :LANES], out[:, LANES:]) + d * u_ref[0, pl.ds(off, S5_RB), :]
        o_ref[0, pl.ds(off, S5_RB), :] = jax.nn.gelu(y).astype(o_ref.dtype)
        return 0

    lax.fori_loop(0, nsub, mm_out, 0)


def _s5_params(a_re, a_im, log_dt, b_re, b_im, c_re, c_im, d_skip):
    step = jnp.exp(log_dt)[:, None]
    mag = jnp.exp(a_re * step)
    abar_re, abar_im = mag * jnp.cos(a_im * step), mag * jnp.sin(a_im * step)
    den = a_re * a_re + a_im * a_im
    nr, ni = abar_re - 1.0, abar_im
    f_re, f_im = (nr * a_re + ni * a_im) / den, (ni * a_re - nr * a_im) / den
    bbar_re = f_re[..., None] * b_re - f_im[..., None] * b_im
    bbar_im = f_re[..., None] * b_im + f_im[..., None] * b_re
    eye = jnp.eye(S5_GB, dtype=F32)
    bb = jnp.stack([bbar_re, bbar_im]).reshape(2, S5_NBLK, S5_GB, S5_STATE, S5_GROUP)
    wb = jnp.einsum('ab,rjapc->jacrbp', eye, bb).reshape(S5_NBLK, LANES, 2 * S5_LANES)
    wb = wb.reshape(S5_PAIRS, 2, LANES, 2 * S5_LANES).transpose(0, 2, 1, 3).reshape(S5_PAIRS, LANES, 4 * S5_LANES)
    cc = jnp.stack([c_re, -c_im]).reshape(2, S5_NBLK, S5_GB, S5_GROUP, S5_STATE)
    wc = jnp.einsum('ab,rjacp->jrapbc', eye, cc).reshape(S5_NBLK, 2 * S5_LANES, LANES)
    wc = wc.reshape(S5_PAIRS, 2, 2 * S5_LANES, LANES).transpose(0, 2, 1, 3).reshape(S5_PAIRS, 2 * S5_LANES, 2 * LANES)
    ab = jnp.concatenate([abar_re.reshape(S5_NBLK, S5_LANES), abar_im.reshape(S5_NBLK, S5_LANES)], axis=1)
    ab = jnp.tile(ab.reshape(S5_PAIRS, 2, 2 * S5_LANES), (1, SUBLANES // 2, 1))
    dd = jnp.tile(d_skip.reshape(S5_PAIRS, 2, LANES), (1, SUBLANES // 2, 1))
    return wb.astype(BF16), ab, wc.astype(BF16), dd


def _s5(u6, wb, ab, wc, dd, l):
    rows = S5_TC * SUBLANES
    return pl.pallas_call(
        _s5_kernel,
        grid=(S5_PAIRS, l // S5_TC),
        in_specs=[pl.BlockSpec((1, rows, LANES), lambda k, c: (k, c, 0)),
                  pl.BlockSpec((1, LANES, 4 * S5_LANES), lambda k, c: (k, 0, 0)),
                  pl.BlockSpec((1, SUBLANES, 2 * S5_LANES), lambda k, c: (k, 0, 0)),
                  pl.BlockSpec((1, 2 * S5_LANES, 2 * LANES), lambda k, c: (k, 0, 0)),
                  pl.BlockSpec((1, SUBLANES, LANES), lambda k, c: (k, 0, 0))],
        out_specs=pl.BlockSpec((1, rows, LANES), lambda k, c: (k, c, 0)),
        out_shape=jax.ShapeDtypeStruct(u6.shape, F32),
        scratch_shapes=[pltpu.VMEM((2 * S5_LANES // LANES, rows, LANES), F32),
                        pltpu.VMEM((SUBLANES, 2 * S5_LANES), F32)],
        compiler_params=_cparams(("arbitrary", "arbitrary")),
        name="s5",
    )(u6, wb, ab, wc, dd)


def _glu_kernel(y_ref, w_ref, o_ref, z_scr, *, bn):
    b = pl.program_id(1)
    rows = z_scr.shape[1]

    @pl.when(b == 0)
    def _():
        y = jnp.concatenate([y_ref.at[k][pl.ds(s, rows, stride=2), :] for k in range(S5_PAIRS) for s in range(2)],
                            axis=1)
        g = jnp.dot(y.astype(BF16), w_ref[...], preferred_element_type=F32)
        z = y * jax.nn.sigmoid(g)
        for c in range(S5_NBLK):
            z_scr[c] = z[:, c * LANES:(c + 1) * LANES]

    o_ref[0] = jnp.concatenate([z_scr.at[c][pl.ds(b, rows // bn, stride=bn), :] for c in range(S5_NBLK)],
                               axis=1).astype(o_ref.dtype)


def _glu(y6, w_glu, bn, l):
    tt = 256
    rows = tt * bn
    return pl.pallas_call(
        functools.partial(_glu_kernel, bn=bn),
        grid=(l // tt, bn),
        in_specs=[pl.BlockSpec((S5_PAIRS, rows * 2, LANES), lambda i, b: (0, i, 0)),
                  pl.BlockSpec((S5_CH, S5_CH), lambda i, b: (0, 0))],
        out_specs=pl.BlockSpec((1, tt, S5_CH), lambda i, b: (b, i, 0)),
        out_shape=jax.ShapeDtypeStruct((bn, l, S5_CH), BF16),
        scratch_shapes=[pltpu.VMEM((S5_NBLK, rows, LANES), F32)],
        compiler_params=_cparams(("arbitrary", "arbitrary")),
        name="glu",
    )(y6, w_glu)


def _merge_kernel(att_ref, z_ref, gm_ref, gs_ref, x_ref, mod_ref, gpost_ref, gpre_ref,
                  wbm_ref, wbs_ref, wo_ref, wrh_ref, wrl_ref, x1_ref, h2b_ref, h2s_ref, lg_ref):
    ym = jnp.dot(att_ref[...], wbm_ref[...], preferred_element_type=F32)
    ys = jnp.dot(z_ref[...], wbs_ref[...], preferred_element_type=F32)
    mixed_in = (gm_ref[...].astype(F32) * ym + gs_ref[...].astype(F32) * ys).astype(BF16)
    mixed = jnp.dot(mixed_in, wo_ref[...], preferred_element_type=F32)
    x1 = x_ref[0] + mod_ref[0, 2:3, :] * _rms(mixed, gpost_ref[...])
    x1_ref[0] = x1
    h2 = _rms(x1, gpre_ref[...]) * (1.0 + mod_ref[0, 4:5, :]) + mod_ref[0, 3:4, :]
    h2_hi = h2.astype(BF16)
    h2b_ref[0] = h2_hi
    tm = h2.shape[0]
    for s in range(SLAB):
        h2s_ref[pl.ds(s, tm, stride=SLAB), :] = h2[:, s * LANES:(s + 1) * LANES]
    h2_lo = (h2 - h2_hi.astype(F32)).astype(BF16)
    nt = (((1,), (1,)), ((), ()))
    lg_ref[...] = (lax.dot_general(wrh_ref[...], h2_hi, nt, preferred_element_type=F32)
                   + lax.dot_general(wrl_ref[...], h2_hi, nt, preferred_element_type=F32)
                   + lax.dot_general(wrh_ref[...], h2_lo, nt, preferred_element_type=F32))


def _merge(att, z_tm, gates, x, mod, g_post, g_pre_ffn, w_br_mla, w_br_s5, w_out, wr_hi, wr_lo):
    bn, l, d = x.shape
    tm = 256
    nl = l // tm
    row = lambda b, i: (b * nl + i, 0)
    const = lambda b, i: (0, 0)
    one = pl.Buffered(1)
    return pl.pallas_call(
        _merge_kernel,
        grid=(bn, nl),
        in_specs=[pl.BlockSpec((tm, MLA_HEADS * V_HEAD), row),
                  pl.BlockSpec((tm, S5_CH), row),
                  pl.BlockSpec((tm, d), lambda b, i: (b * nl + i, 0)),
                  pl.BlockSpec((tm, d), lambda b, i: (b * nl + i, 1)),
                  pl.BlockSpec((1, tm, d), lambda b, i: (b, i, 0)),
                  pl.BlockSpec((1, 6, d), lambda b, i: (b, 0, 0)),
                  pl.BlockSpec((1, d), const),
                  pl.BlockSpec((1, d), const),
                  pl.BlockSpec((MLA_HEADS * V_HEAD, d), const, pipeline_mode=one),
                  pl.BlockSpec((S5_CH, d), const, pipeline_mode=one),
                  pl.BlockSpec((d, d), const, pipeline_mode=one),
                  pl.BlockSpec((N_EXPERTS, d), const, pipeline_mode=one),
                  pl.BlockSpec((N_EXPERTS, d), const, pipeline_mode=one)],
        out_specs=[pl.BlockSpec((1, tm, d), lambda b, i: (b, i, 0)),
                   pl.BlockSpec((1, tm, d), lambda b, i: (b, i, 0)),
                   pl.BlockSpec((tm * SLAB, LANES), row),
                   pl.BlockSpec((N_EXPERTS, tm), lambda b, i: (0, b * nl + i))],
        out_shape=[jax.ShapeDtypeStruct((bn, l, d), F32),
                   jax.ShapeDtypeStruct((bn, l, d), BF16),
                   jax.ShapeDtypeStruct((bn * l * SLAB, LANES), F32),
                   jax.ShapeDtypeStruct((N_EXPERTS, bn * l), F32)],
        compiler_params=_cparams(("arbitrary", "arbitrary")),
        name="merge",
    )(att, z_tm, gates, gates, x, mod, g_post.reshape(1, d), g_pre_ffn.reshape(1, d),
      w_br_mla, w_br_s5, w_out, wr_hi, wr_lo)


def _route_kernel(lg_ref, b_ref, ti_ref, tw_ref, cnt_ref):
    ng = N_EXPERT_GROUPS
    gsz = N_EXPERTS // ng
    lg = lg_ref[...]
    t = lg.shape[-1]
    sc = jax.nn.sigmoid(lg)
    sel = sc + b_ref[...]
    ninf = -jnp.inf
    i_in = lax.broadcasted_iota(jnp.int32, (ng, gsz, t), 1).astype(F32)
    m1 = jnp.max(sel, axis=1, keepdims=True)
    idx1 = jnp.min(jnp.where(sel == m1, i_in, float(gsz)), axis=1, keepdims=True)
    m2 = jnp.max(jnp.where(i_in == idx1, ninf, sel), axis=1, keepdims=True)
    gs = m1 + m2
    g_i = lax.broadcasted_iota(jnp.int32, (ng, 1, t), 0).astype(F32)
    picked = jnp.zeros((ng, 1, t), F32)
    cur = gs
    for _ in range(TOPK_GROUPS):
        m = jnp.max(cur, axis=0, keepdims=True)
        idx = jnp.min(jnp.where(cur == m, g_i, float(ng)), axis=0, keepdims=True)
        hit = g_i == idx
        picked = jnp.where(hit, 1.0, picked)
        cur = jnp.where(hit, ninf, cur)
    cand = jnp.where(picked > 0.5, sel, ninf)
    e_i = lax.broadcasted_iota(jnp.int32, (ng, gsz, t), 0).astype(F32) * float(gsz) + i_in
    ws = []
    hits = jnp.zeros((ng, gsz, t), F32)
    for r in range(TOP_K):
        m = jnp.max(jnp.max(cand, axis=1, keepdims=True), axis=0, keepdims=True)
        idx = jnp.min(jnp.min(jnp.where(cand == m, e_i, float(N_EXPERTS)), axis=1, keepdims=True),
                      axis=0, keepdims=True)
        hit = e_i == idx
        w = jnp.sum(jnp.sum(jnp.where(hit, sc, 0.0), axis=1, keepdims=True), axis=0, keepdims=True)
        ti_ref[r:r + 1, :] = idx[0].astype(jnp.int32)
        ws.append(w[0])
        hits = jnp.where(hit, 1.0, hits)
        cand = jnp.where(hit, ninf, cand)

    @pl.when(pl.program_id(0) == 0)
    def _():
        cnt_ref[...] = jnp.zeros(cnt_ref.shape, F32)

    cnt_ref[...] += jnp.sum(hits, axis=2, keepdims=True)
    tot = ws[0]
    for w in ws[1:]:
        tot = tot + w
    for r in range(TOP_K):
        tw_ref[r:r + 1, :] = ws[r] / tot * ROUTED_SCALE
    for r in range(TOP_K, SUBLANES):
        ti_ref[r:r + 1, :] = jnp.zeros((1, t), jnp.int32)
        tw_ref[r:r + 1, :] = jnp.zeros((1, t), F32)


def _route(logits_t, router_bias):
    n = logits_t.shape[1]
    ng = N_EXPERT_GROUPS
    gsz = N_EXPERTS // ng
    tn = 1024
    return pl.pallas_call(
        _route_kernel,
        grid=(n // tn,),
        in_specs=[pl.BlockSpec((ng, gsz, tn), lambda i: (0, 0, i)),
                  pl.BlockSpec((ng, gsz, 1), lambda i: (0, 0, 0))],
        out_specs=[pl.BlockSpec((SUBLANES, tn), lambda i: (0, i)),
                   pl.BlockSpec((SUBLANES, tn), lambda i: (0, i)),
                   pl.BlockSpec((ng, gsz, 1), lambda i: (0, 0, 0))],
        out_shape=[jax.ShapeDtypeStruct((SUBLANES, n), jnp.int32),
                   jax.ShapeDtypeStruct((SUBLANES, n), F32),
                   jax.ShapeDtypeStruct((ng, gsz, 1), F32)],
        compiler_params=_cparams(("arbitrary",)),
        name="route",
    )(logits_t.reshape(ng, gsz, n), router_bias.reshape(ng, gsz, 1))


def _moe_kernel(blk_e_ref, nused_ref, nvalid_ref, nxt_e_ref, wslot_ref, t_ref, t1_ref, t2_ref, a_ref, h_hbm,
                wg_hbm, wu_hbm, wd_hbm, y_hbm, xbuf, ybuf, wg_buf, wu_buf, wd_buf, wg_s, wu_s, wd_s,
                gsem, wsem, wtsem):
    j = pl.program_id(0)
    n_used = nused_ref[0]
    slot = lax.rem(j, GATHER_SLOTS)
    nv = nvalid_ref[j]
    nv_prev = nvalid_ref[jnp.maximum(j - 1, 0)]

    def slab(ref, row0):
        return ref.at[pl.ds(pl.multiple_of(row0, SUBLANES), SLAB), :]

    def start_gather(idx_ref, s):
        def body(r, _):
            pltpu.make_async_copy(slab(h_hbm, idx_ref[0, 0, r]), slab(xbuf.at[s], r * MOE_PITCH), gsem.at[s]).start()
            return 0
        lax.fori_loop(0, MOE_TB, body, 0, unroll=8)

    def wait_gather(s):
        rows = MOE_TB * SLAB
        pltpu.make_async_copy(h_hbm.at[pl.ds(0, rows), :], xbuf.at[s, pl.ds(0, rows), :], gsem.at[s]).wait()

    def weight_copies(expert, s):
        return (pltpu.make_async_copy(wg_hbm.at[expert], wg_buf.at[s], wtsem.at[s]),
                pltpu.make_async_copy(wu_hbm.at[expert], wu_buf.at[s], wtsem.at[s]),
                pltpu.make_async_copy(wd_hbm.at[expert], wd_buf.at[s], wtsem.at[s]))

    def wait_writes(count):
        p = MOE_TB
        while p >= 1:
            @pl.when((count & p) != 0)
            def _(p=p):
                rows = p * SLAB
                pltpu.make_async_copy(ybuf.at[pl.ds(0, rows), :], y_hbm.at[pl.ds(0, rows), :], wsem.at[0]).wait()
            p //= 2

    @pl.when(j < n_used)
    def _():
        @pl.when(j == 0)
        def _():
            start_gather(t_ref, 0)

            @pl.when(n_used > 1)
            def _():
                start_gather(t1_ref, 1)

        e = blk_e_ref[j]
        ws = wslot_ref[j]

        @pl.when(j == 0)
        def _():
            for cp in weight_copies(e, ws):
                cp.start(priority=WEIGHT_DMA_PRIORITY)

        first = jnp.logical_or(j == 0, e != blk_e_ref[jnp.maximum(j - 1, 0)])

        @pl.when(first)
        def _():
            for cp in weight_copies(e, ws):
                cp.wait()
            wg_s[...] = wg_buf[ws].astype(BF16)
            wu_s[...] = wu_buf[ws].astype(BF16)
            wd_s[...] = wd_buf[ws].astype(BF16)

        wait_gather(slot)

        @pl.when(j + 2 < n_used)
        def _():
            start_gather(t2_ref, lax.rem(j + 2, GATHER_SLOTS))

        @pl.when(jnp.logical_and(first, nxt_e_ref[j] >= 0))
        def _():
            for cp in weight_copies(nxt_e_ref[j], 1 - ws):
                cp.start(priority=WEIGHT_DMA_PRIORITY)

        xs = xbuf.at[slot]
        x = jnp.concatenate([xs[pl.ds(s, MOE_TB, stride=MOE_PITCH), :] for s in range(SLAB)], axis=1).astype(BF16)
        g = jnp.dot(x, wg_s[...], preferred_element_type=F32)
        u = jnp.dot(x, wu_s[...], preferred_element_type=F32)
        hm = (g * jax.nn.sigmoid(g) * u).astype(BF16)
        y = jnp.dot(hm, wd_s[...], preferred_element_type=F32)

        @pl.when(j > 0)
        def _():
            wait_writes(nv_prev)

        for s in range(SLAB):
            ybuf[pl.ds(s, MOE_TB, stride=MOE_PITCH), :] = y[:, s * LANES:(s + 1) * LANES]

        def start_write(r):
            pltpu.make_async_copy(slab(ybuf, r * MOE_PITCH), slab(y_hbm, a_ref[0, 0, r]), wsem.at[0]).start()

        def wgroup(g8, _):
            for q in range(SUBLANES):
                start_write(g8 * SUBLANES + q)
            return 0

        def wtail(r, _):
            start_write(r)
            return 0

        n_groups = lax.shift_right_logical(nv, 3)
        lax.fori_loop(0, n_groups, wgroup, 0)
        lax.fori_loop(n_groups * SUBLANES, nv, wtail, 0)

        @pl.when(j == n_used - 1)
        def _():
            wait_writes(nv)


def _moe(h2s, topi, counts, w_gate, w_up, w_down):
    n = h2s.shape[0] // SLAB
    d = D_MODEL
    nk = n * TOP_K
    tb = MOE_TB
    n_blocks = nk // tb + N_EXPERTS
    flat_e = topi[:TOP_K].reshape(nk)
    _, order = lax.sort_key_val(flat_e, jnp.arange(nk, dtype=jnp.int32))
    start = jnp.cumsum(counts) - counts
    padded = (counts + tb - 1) // tb * tb
    pad_end = jnp.cumsum(padded)
    pad_start = pad_end - padded
    n_used = (pad_end[-1] // tb).astype(jnp.int32).reshape(1)
    blk_p0 = jnp.arange(n_blocks, dtype=jnp.int32) * tb
    blk_e = jnp.minimum(jnp.sum((pad_end[None, :] <= blk_p0[:, None]).astype(jnp.int32), axis=1), N_EXPERTS - 1)
    blk_off = blk_p0 - pad_start[blk_e]
    nvalid = jnp.clip(counts[blk_e] - blk_off, 0, tb).astype(jnp.int32)
    rank = (start[blk_e] + blk_off)[:, None] + jnp.arange(tb, dtype=jnp.int32)[None, :]
    a3 = order[jnp.clip(rank, 0, nk - 1)].reshape(n_blocks, 1, tb)
    arow = a3 * SLAB
    trow = (a3 % n) * SLAB
    used = counts > 0
    e_ids = jnp.arange(N_EXPERTS, dtype=jnp.int32)
    later = jnp.where(used, e_ids, N_EXPERTS)
    nxt = jnp.concatenate([lax.cummin(later[::-1])[::-1][1:], jnp.full((1,), N_EXPERTS, jnp.int32)])
    nxt_e = jnp.where(nxt < N_EXPERTS, nxt, -1)[blk_e].astype(jnp.int32)
    wslot = ((jnp.cumsum(used.astype(jnp.int32)) - 1) % 2)[blk_e].astype(jnp.int32)

    smem_blk = lambda f: pl.BlockSpec((1, 1, tb), f, memory_space=pltpu.SMEM)
    any_spec = pl.BlockSpec(memory_space=pl.ANY)
    grid_spec = pltpu.PrefetchScalarGridSpec(
        num_scalar_prefetch=5,
        grid=(n_blocks,),
        in_specs=[smem_blk(lambda j, *_: (j, 0, 0)),
                  smem_blk(lambda j, *_: (jnp.minimum(j + 1, n_blocks - 1), 0, 0)),
                  smem_blk(lambda j, *_: (jnp.minimum(j + 2, n_blocks - 1), 0, 0)),
                  smem_blk(lambda j, *_: (j, 0, 0)),
                  any_spec, any_spec, any_spec, any_spec],
        out_specs=any_spec,
        scratch_shapes=[pltpu.VMEM((GATHER_SLOTS, tb * MOE_PITCH, LANES), F32),
                        pltpu.VMEM((tb * MOE_PITCH, LANES), F32),
                        pltpu.VMEM((2, d, D_EXPERT), F32),
                        pltpu.VMEM((2, d, D_EXPERT), F32),
                        pltpu.VMEM((2, D_EXPERT, d), F32),
                        pltpu.VMEM((d, D_EXPERT), BF16),
                        pltpu.VMEM((d, D_EXPERT), BF16),
                        pltpu.VMEM((D_EXPERT, d), BF16),
                        pltpu.SemaphoreType.DMA((GATHER_SLOTS,)),
                        pltpu.SemaphoreType.DMA((1,)),
                        pltpu.SemaphoreType.DMA((2,))],
    )
    return pl.pallas_call(
        _moe_kernel,
        grid_spec=grid_spec,
        out_shape=jax.ShapeDtypeStruct((TOP_K * n * SLAB, LANES), F32),
        compiler_params=_cparams(("arbitrary",)),
        name="moe",
    )(blk_e, n_used, nvalid, nxt_e, wslot, trow, trow, trow, arow, h2s, w_gate, w_up, w_down)


def _final_kernel(h_ref, y0, y1, y2, y3, y4, y5, tw_ref, x1_ref, mod_ref, g_ref, wg_ref, wu_ref, wd_ref, o_ref):
    h = h_ref[...]
    tm = h.shape[0]
    g = jnp.dot(h, wg_ref[...], preferred_element_type=F32)
    u = jnp.dot(h, wu_ref[...], preferred_element_type=F32)
    hm = (g * jax.nn.sigmoid(g) * u).astype(BF16)
    ffn = jnp.dot(hm, wd_ref[...], preferred_element_type=F32)
    tw = tw_ref[...]
    for k, y in enumerate((y0, y1, y2, y3, y4, y5)):
        yk = jnp.concatenate([y[pl.ds(s, tm, stride=SLAB), :] for s in range(SLAB)], axis=1)
        ffn = ffn + yk * tw[:, k:k + 1]
    o_ref[0] = x1_ref[0] + mod_ref[0, 5:6, :] * _rms(ffn, g_ref[...])


def _final(h2, y_flat, topw_t, x1, mod, g_post_ffn, w_sg, w_su, w_sd):
    bn, l, d = x1.shape
    n = bn * l
    tm = 256
    nl = l // tm
    const = lambda b, i: (0, 0)
    one = pl.Buffered(1)
    y_specs = [pl.BlockSpec((tm * SLAB, LANES), functools.partial(lambda b, i, k: (k * (n // tm) + b * nl + i, 0), k=k))
               for k in range(TOP_K)]
    return pl.pallas_call(
        _final_kernel,
        grid=(bn, nl),
        in_specs=[pl.BlockSpec((tm, d), lambda b, i: (b * nl + i, 0))] + y_specs + [
            pl.BlockSpec((tm, SUBLANES), lambda b, i: (b * nl + i, 0)),
            pl.BlockSpec((1, tm, d), lambda b, i: (b, i, 0)),
            pl.BlockSpec((1, 6, d), lambda b, i: (b, 0, 0)),
            pl.BlockSpec((1, d), const),
            pl.BlockSpec((d, D_EXPERT), const, pipeline_mode=one),
            pl.BlockSpec((d, D_EXPERT), const, pipeline_mode=one),
            pl.BlockSpec((D_EXPERT, d), const, pipeline_mode=one)],
        out_specs=pl.BlockSpec((1, tm, d), lambda b, i: (b, i, 0)),
        out_shape=jax.ShapeDtypeStruct((bn, l, d), F32),
        compiler_params=_cparams(("arbitrary", "arbitrary")),
        name="final",
    )(h2, *([y_flat] * TOP_K), topw_t, x1, mod, g_post_ffn.reshape(1, d), w_sg, w_su, w_sd)


def _layer(li, x, c, positions, w_ada, b_ada, g_pre_mix, g_post_mix, g_pre_ffn, g_post_ffn, w_in, g_q, g_kv,
           w_uq, w_uk, w_uv, a_re, a_im, log_dt, b_re, b_im, c_re, c_im, d_skip, w_glu, w_br_mla, w_br_s5,
           w_out, w_router, router_bias, w_exp_gate, w_exp_up, w_exp_down, w_sh_gate, w_sh_up, w_sh_down):
    bn, l, d = x.shape
    n = bn * l
    mod = _adaln(c, w_ada, b_ada).reshape(bn, 6, d)

    o_kpe = Q_LORA + KV_LORA
    o_u = o_kpe + QK_ROPE
    o_g = o_u + S5_CH
    w_in_t = jnp.swapaxes(w_in, 1, 2)
    w_q = jnp.pad(w_uq.reshape(Q_LORA, MLA_HEADS, QK_NOPE + QK_ROPE),
                  ((0, 0), (0, 0), (0, QK_PAD - QK_NOPE - QK_ROPE))).reshape(Q_LORA, MLA_HEADS * QK_PAD).astype(BF16)
    wr_t = w_router.T
    wr_hi = wr_t.astype(BF16)
    wr_lo = (wr_t - wr_hi.astype(F32)).astype(BF16)

    rope_c, rope_s1, rope_s2 = _rope_tables(positions)
    chunk_id = positions // CHUNK

    h = _prenorm(x, mod, g_pre_mix, 0, 1).reshape(n, d)
    lat = _mmt(h, w_in_t, li, 0, o_kpe, F32, 1024, 512, name="mm_lat")
    kpe = _mmt(h, w_in_t, li, o_kpe, LANES, F32, 1024, LANES, name="mm_kpe")
    gates = _mmt(h, w_in_t, li, o_g, 2 * d, BF16, 1024, 1024, act="sigmoid", name="mm_gates")
    u6 = _mm_u(h, w_in_t, li, o_u, bn, l)

    q, k, v = _qkvproj(lat, g_q, g_kv, w_q, w_uk.astype(BF16), w_uv.astype(BF16), kpe, rope_c, rope_s1, rope_s2)
    att = _attention(q, k, v, chunk_id, bn, l)

    wb, ab, wc, dd = _s5_params(a_re, a_im, log_dt, b_re, b_im, c_re, c_im, d_skip)
    y6 = _s5(u6, wb, ab, wc, dd, l)
    z = _glu(y6, w_glu.astype(BF16), bn, l).reshape(n, S5_CH)

    x1, h2b, h2s, logits_t = _merge(att, z, gates, x, mod, g_post_mix, g_pre_ffn, w_br_mla.astype(BF16),
                                    w_br_s5.astype(BF16), w_out.astype(BF16), wr_hi, wr_lo)

    topi, topw, cnt = _route(logits_t, router_bias)
    y_slabs = _moe(h2s, topi, cnt.reshape(N_EXPERTS).astype(jnp.int32), w_exp_gate, w_exp_up, w_exp_down)
    return _final(h2b.reshape(n, d), y_slabs, topw.T, x1, mod, g_post_ffn, w_sh_gate.astype(BF16),
                  w_sh_up.astype(BF16), w_sh_down.astype(BF16))


def kernel(x, c, positions, w_ada, b_ada, g_pre_mix, g_post_mix, g_pre_ffn, g_post_ffn, w_in, g_q, g_kv, w_uq, w_uk, w_uv, a_re, a_im, log_dt, b_re, b_im, c_re, c_im, d_skip, w_glu, w_br_mla, w_br_s5, w_out, w_router, router_bias, w_exp_gate, w_exp_up, w_exp_down, w_sh_gate, w_sh_up, w_sh_down):
    depth = w_ada.shape[0]
    for li in range(depth):
        x = _layer(li, x, c, positions, w_ada[li], b_ada[li], g_pre_mix[li], g_post_mix[li], g_pre_ffn[li],
                   g_post_ffn[li], w_in, g_q[li], g_kv[li], w_uq[li], w_uk[li], w_uv[li], a_re[li], a_im[li],
                   log_dt[li], b_re[li], b_im[li], c_re[li], c_im[li], d_skip[li], w_glu[li], w_br_mla[li],
                   w_br_s5[li], w_out[li], w_router[li], router_bias[li], w_exp_gate[li], w_exp_up[li],
                   w_exp_down[li], w_sh_gate[li], w_sh_up[li], w_sh_down[li])
    return x
```

```python
import functools

import jax
import jax.numpy as jnp
from jax import lax
from jax.experimental import pallas as pl
from jax.experimental.pallas import tpu as pltpu

F32 = jnp.float32
BF16 = jnp.bfloat16

D_MODEL = 2048
CHUNK = 64
EPS = 1e-6
MLA_HEADS = 8
QK_NOPE = 128
QK_ROPE = 64
V_HEAD = 128
Q_LORA = 512
KV_LORA = 512
ROPE_THETA = 10000.0
S5_CH = 1024
S5_GROUP = 16
S5_GROUPS = S5_CH // S5_GROUP
S5_STATE = 64
N_EXPERTS = 64
TOP_K = 6
N_EXPERT_GROUPS = 8
TOPK_GROUPS = 4
D_EXPERT = 512
ROUTED_SCALE = 2.5

LANES = 128
SUBLANES = 8
QK_PAD = 2 * LANES
VMEM_LIMIT = 56 * 1024 * 1024
NEG = -1e30
LOG2E = 1.4426950408889634

S5_GB = LANES // S5_GROUP
S5_NBLK = S5_CH // LANES
S5_PAIRS = S5_NBLK // 2
S5_LANES = S5_GB * S5_STATE
S5_TC = 256
S5_RB = 512

ATT_TQ = 256
ATT_TK = 256
MOE_TB = 256
SLAB = D_MODEL // LANES
MERGE_SUBTILES = 2
GATHER_SLOTS = 3
WEIGHT_DMA_PRIORITY = 1
MOE_PITCH = 20
PITCH_ALIGN = 4


def _cparams(sem):
    return pltpu.CompilerParams(dimension_semantics=sem, vmem_limit_bytes=VMEM_LIMIT)


def _rms(x, g):
    return x * lax.rsqrt(jnp.mean(x * x, axis=-1, keepdims=True) + EPS) * g


def _adaln_kernel(c_ref, w_ref, b_ref, o_ref):
    c = c_ref[...]
    a = (c * jax.nn.sigmoid(c)).astype(BF16)
    o_ref[...] = jnp.dot(a, w_ref[...].astype(BF16), preferred_element_type=F32) + b_ref[...]


def _adaln(c, w, b):
    bn, d = c.shape
    n = w.shape[1]
    tn = 1024
    return pl.pallas_call(
        _adaln_kernel,
        grid=(n // tn,),
        in_specs=[pl.BlockSpec((bn, d), lambda j: (0, 0)),
                  pl.BlockSpec((d, tn), lambda j: (0, j)),
                  pl.BlockSpec((1, tn), lambda j: (0, j))],
        out_specs=pl.BlockSpec((bn, tn), lambda j: (0, j)),
        out_shape=jax.ShapeDtypeStruct((bn, n), F32),
        compiler_params=_cparams(("arbitrary",)),
        name="adaln",
    )(c, w, b.reshape(1, n))


def _prenorm_kernel(x_ref, mod_ref, g_ref, o_ref, *, sh_row, sc_row):
    y = _rms(x_ref[0], g_ref[...])
    o_ref[0] = (y * (1.0 + mod_ref[0, sc_row:sc_row + 1, :]) + mod_ref[0, sh_row:sh_row + 1, :]).astype(o_ref.dtype)


def _prenorm(x, mod, g, sh_row, sc_row):
    bn, l, d = x.shape
    tl = 512
    return pl.pallas_call(
        functools.partial(_prenorm_kernel, sh_row=sh_row, sc_row=sc_row),
        grid=(bn, l // tl),
        in_specs=[pl.BlockSpec((1, tl, d), lambda b, i: (b, i, 0)),
                  pl.BlockSpec((1, 6, d), lambda b, i: (b, 0, 0)),
                  pl.BlockSpec((1, d), lambda b, i: (0, 0))],
        out_specs=pl.BlockSpec((1, tl, d), lambda b, i: (b, i, 0)),
        out_shape=jax.ShapeDtypeStruct((bn, l, d), BF16),
        compiler_params=_cparams(("arbitrary", "arbitrary")),
        name="prenorm",
    )(x, mod, g.reshape(1, d))


NT_DIMS = (((1,), (1,)), ((), ()))


def _wt_tile(wa_ref, wb_ref, shift, tn):
    if shift == 0:
        return wa_ref[...].astype(BF16)
    return jnp.concatenate([wa_ref[...], wb_ref[...]], axis=0)[shift:shift + tn].astype(BF16)


def _wt_specs(layer, row0, tn, k, jmap):
    shift = row0 % LANES
    c0 = row0 - shift
    assert c0 % tn == 0 and shift % SUBLANES == 0
    specs = [pl.BlockSpec((None, tn, k), lambda *g: (layer, jmap(*g) + c0 // tn, 0))]
    if shift:
        specs.append(pl.BlockSpec((None, LANES, k), lambda *g: (layer, (c0 + (jmap(*g) + 1) * tn) // LANES, 0)))
    return specs, shift


def _mmt_kernel(a_ref, wa_ref, *rest, act, shift):
    wb_ref = rest[0] if shift else None
    o_ref, w_scr = rest[-2:]

    @pl.when(pl.program_id(1) == 0)
    def _():
        w_scr[...] = _wt_tile(wa_ref, wb_ref, shift, w_scr.shape[0])

    acc = lax.dot_general(a_ref[...], w_scr[...], NT_DIMS, preferred_element_type=F32)
    if act == "sigmoid":
        acc = jax.nn.sigmoid(acc)
    o_ref[...] = acc.astype(o_ref.dtype)


def _mmt(a, w_t, layer, row0, n, out_dtype, tm, tn, act=None, name="mmt"):
    m, k = a.shape
    w_specs, shift = _wt_specs(layer, row0, tn, k, lambda j, i: j)
    return pl.pallas_call(
        functools.partial(_mmt_kernel, act=act, shift=shift),
        grid=(n // tn, m // tm),
        in_specs=[pl.BlockSpec((tm, k), lambda j, i: (i, 0))] + w_specs,
        out_specs=pl.BlockSpec((tm, tn), lambda j, i: (i, j)),
        out_shape=jax.ShapeDtypeStruct((m, n), out_dtype),
        scratch_shapes=[pltpu.VMEM((tn, k), BF16)],
        compiler_params=_cparams(("arbitrary", "arbitrary")),
        name=name,
    )(a, *([w_t] * len(w_specs)))


def _mm_u_kernel(a_ref, wa_ref, *rest, shift):
    wb_ref = rest[0] if shift else None
    o_ref, w_scr = rest[-2:]
    b = pl.program_id(1)

    @pl.when(jnp.logical_and(pl.program_id(0) == 0, b == 0))
    def _():
        w_scr[...] = _wt_tile(wa_ref, wb_ref, shift, w_scr.shape[0])

    res = lax.dot_general(a_ref[...], w_scr[...], NT_DIMS, preferred_element_type=F32)
    tm = res.shape[0]
    for c in range(S5_NBLK):
        o_ref.at[c // 2][pl.ds(b * 2 + c % 2, tm, stride=SUBLANES), :] = res[:, c * LANES:(c + 1) * LANES]


def _mm_u(h, w_t, layer, row0, bn, l):
    m, k = h.shape
    tm = 512
    nl = l // tm
    assert bn * 2 == SUBLANES
    w_specs, shift = _wt_specs(layer, row0, S5_CH, k, lambda i, b: 0)
    return pl.pallas_call(
        functools.partial(_mm_u_kernel, shift=shift),
        grid=(nl, bn),
        in_specs=[pl.BlockSpec((tm, k), lambda i, b: (b * nl + i, 0))] + w_specs,
        out_specs=pl.BlockSpec((S5_PAIRS, tm * SUBLANES, LANES), lambda i, b: (0, i, 0)),
        out_shape=jax.ShapeDtypeStruct((S5_PAIRS, l * SUBLANES, LANES), F32),
        scratch_shapes=[pltpu.VMEM((S5_CH, k), BF16)],
        compiler_params=_cparams(("arbitrary", "arbitrary")),
        name="mm_u",
    )(h, *([w_t] * len(w_specs)))


def _rope_tab_kernel(pos_ref, k_ref, c_ref, s1_ref, s2_ref):
    ang = pos_ref[...].astype(F32) * k_ref[0:1, :]
    s = jnp.sin(ang)
    c_ref[...] = jnp.cos(ang) * k_ref[1:2, :]
    s1_ref[...] = s * k_ref[2:3, :]
    s2_ref[...] = s * k_ref[3:4, :]


def _rope_tables(positions):
    n = positions.size
    half = QK_ROPE // 2
    inv_freq = ROPE_THETA ** (-jnp.arange(half, dtype=F32) / half)
    zh, oh = jnp.zeros((half,), F32), jnp.ones((half,), F32)
    z2 = jnp.zeros((LANES - QK_ROPE,), F32)
    rows = [jnp.concatenate([inv_freq, inv_freq, z2]), jnp.concatenate([oh, oh, z2]),
            jnp.concatenate([-oh, zh, z2]), jnp.concatenate([zh, oh, z2])]
    consts = jnp.stack(rows + [jnp.zeros((LANES,), F32)] * (SUBLANES - len(rows)))
    tm = 1024
    tab = jax.ShapeDtypeStruct((n, LANES), F32)
    return pl.pallas_call(
        _rope_tab_kernel,
        grid=(n // tm,),
        in_specs=[pl.BlockSpec((tm, 1), lambda i: (i, 0)),
                  pl.BlockSpec((SUBLANES, LANES), lambda i: (0, 0))],
        out_specs=[pl.BlockSpec((tm, LANES), lambda i: (i, 0))] * 3,
        out_shape=[tab, tab, tab],
        compiler_params=_cparams(("arbitrary",)),
        name="rope_tables",
    )(positions.reshape(n, 1), consts)


def _rope_tile(t, c_ref, s1_ref, s2_ref):
    return (t * c_ref[...] + pltpu.roll(t, LANES - QK_ROPE // 2, 1) * s1_ref[...]
            + pltpu.roll(t, QK_ROPE // 2, 1) * s2_ref[...])


def _qkvproj_kernel(lat_ref, gq_ref, gkv_ref, wq_ref, wk_ref, wv_ref, kpe_ref, c_ref, s1_ref, s2_ref,
                    q_ref, k_ref, v_ref, *, scale):
    lat = lat_ref[...]
    qn = _rms(lat[:, :Q_LORA], gq_ref[...]).astype(BF16)
    cn = _rms(lat[:, Q_LORA:], gkv_ref[...]).astype(BF16)
    q = jnp.dot(qn, wq_ref[...], preferred_element_type=F32)
    kn = jnp.dot(cn, wk_ref[...], preferred_element_type=F32)
    v_ref[...] = jnp.dot(cn, wv_ref[...], preferred_element_type=F32).astype(v_ref.dtype)
    kt = _rope_tile(kpe_ref[...], c_ref, s1_ref, s2_ref).astype(k_ref.dtype)
    for h in range(MLA_HEADS):
        o = h * QK_PAD
        q_ref[:, o:o + LANES] = (q[:, o:o + LANES] * scale).astype(q_ref.dtype)
        qt = _rope_tile(q[:, o + LANES:o + QK_PAD], c_ref, s1_ref, s2_ref)
        q_ref[:, o + LANES:o + QK_PAD] = (qt * scale).astype(q_ref.dtype)
        k_ref[:, o:o + LANES] = kn[:, h * QK_NOPE:(h + 1) * QK_NOPE].astype(k_ref.dtype)
        k_ref[:, o + LANES:o + QK_PAD] = kt


def _qkvproj(lat, g_q, g_kv, w_q, w_k, w_v, kpe, rope_c, rope_s1, rope_s2):
    n = lat.shape[0]
    tm = 512
    row = lambda i: (i, 0)
    const = lambda i: (0, 0)
    tab = pl.BlockSpec((tm, LANES), row)
    return pl.pallas_call(
        functools.partial(_qkvproj_kernel, scale=(QK_NOPE + QK_ROPE) ** -0.5 * LOG2E),
        grid=(n // tm,),
        in_specs=[pl.BlockSpec((tm, Q_LORA + KV_LORA), row),
                  pl.BlockSpec((1, Q_LORA), const),
                  pl.BlockSpec((1, KV_LORA), const),
                  pl.BlockSpec((Q_LORA, MLA_HEADS * QK_PAD), const),
                  pl.BlockSpec((KV_LORA, MLA_HEADS * QK_NOPE), const),
                  pl.BlockSpec((KV_LORA, MLA_HEADS * V_HEAD), const),
                  tab, tab, tab, tab],
        out_specs=[pl.BlockSpec((tm, MLA_HEADS * QK_PAD), row),
                   pl.BlockSpec((tm, MLA_HEADS * QK_PAD), row),
                   pl.BlockSpec((tm, MLA_HEADS * V_HEAD), row)],
        out_shape=[jax.ShapeDtypeStruct((n, MLA_HEADS * QK_PAD), BF16),
                   jax.ShapeDtypeStruct((n, MLA_HEADS * QK_PAD), BF16),
                   jax.ShapeDtypeStruct((n, MLA_HEADS * V_HEAD), BF16)],
        compiler_params=_cparams(("arbitrary",)),
        name="qkvproj",
    )(lat, g_q.reshape(1, Q_LORA), g_kv.reshape(1, KV_LORA), w_q, w_k, w_v, kpe, rope_c, rope_s1, rope_s2)


def _attn_kernel(lo_ref, hi_ref, q_ref, k_ref, v_ref, qc_ref, kc_ref, o_ref, m_scr, l_scr, acc_scr, *, nq):
    b = pl.program_id(0)
    i = pl.program_id(1)
    qc = qc_ref[...]
    m_scr[...] = jnp.full(m_scr.shape, NEG, F32)
    l_scr[...] = jnp.zeros(l_scr.shape, F32)
    acc_scr[...] = jnp.zeros(acc_scr.shape, F32)

    def make_body(masked):
        def body(j, _):
            off = pl.multiple_of(j * ATT_TK, ATT_TK)
            if masked:
                mask = kc_ref[j] <= qc
            for h in range(MLA_HEADS):
                q = q_ref[:, h * QK_PAD:(h + 1) * QK_PAD]
                k = k_ref[pl.ds(off, ATT_TK), h * QK_PAD:(h + 1) * QK_PAD]
                s = lax.dot_general(q, k, (((1,), (1,)), ((), ())), preferred_element_type=F32)
                if masked:
                    s = jnp.where(mask, s, NEG)
                m_old = m_scr[h]
                m_new = jnp.maximum(m_old, jnp.max(s, axis=-1, keepdims=True))
                p = jnp.exp2(s - jnp.concatenate([m_new] * (ATT_TK // LANES), axis=1))
                alpha = jnp.exp2(m_old - m_new)
                l_scr[h] = alpha * l_scr[h] + jnp.sum(p, axis=-1, keepdims=True)
                v = v_ref[pl.ds(off, ATT_TK), h * V_HEAD:(h + 1) * V_HEAD]
                acc_scr[h] = alpha * acc_scr[h] + jnp.dot(p.astype(BF16), v, preferred_element_type=F32)
                m_scr[h] = m_new
            return 0
        return body

    lo = lo_ref[b * nq + i]
    lax.fori_loop(0, lo, make_body(False), 0)
    lax.fori_loop(lo, hi_ref[b * nq + i], make_body(True), 0)
    for h in range(MLA_HEADS):
        o_ref[:, h * V_HEAD:(h + 1) * V_HEAD] = (acc_scr[h] / l_scr[h]).astype(o_ref.dtype)


def _attention(q, k, v, chunk_id, bn, l):
    nq = l // ATT_TQ
    nk = l // ATT_TK
    q_max = jnp.max(chunk_id.reshape(bn, nq, ATT_TQ), axis=-1)
    k_min = jnp.min(chunk_id.reshape(bn, nk, ATT_TK), axis=-1)
    needed = k_min[:, None, :] <= q_max[:, :, None]
    hi = jnp.max(jnp.where(needed, jnp.arange(1, nk + 1, dtype=jnp.int32), 0), axis=-1).reshape(bn * nq)
    q_min = jnp.min(chunk_id.reshape(bn, nq, ATT_TQ), axis=-1)
    k_max = jnp.max(chunk_id.reshape(bn, nk, ATT_TK), axis=-1)
    full = k_max[:, None, :] <= q_min[:, :, None]
    lo = jnp.min(jnp.where(full, nk, jnp.arange(nk, dtype=jnp.int32)), axis=-1).astype(jnp.int32).reshape(bn * nq)
    hi = jnp.maximum(hi, lo)
    qc = chunk_id.reshape(bn * l, 1)
    kc = chunk_id.reshape(bn * nk, 1, ATT_TK)
    grid_spec = pltpu.PrefetchScalarGridSpec(
        num_scalar_prefetch=2,
        grid=(bn, nq),
        in_specs=[pl.BlockSpec((ATT_TQ, MLA_HEADS * QK_PAD), lambda b, i, *_: (b * nq + i, 0)),
                  pl.BlockSpec((l, MLA_HEADS * QK_PAD), lambda b, i, *_: (b, 0)),
                  pl.BlockSpec((l, MLA_HEADS * V_HEAD), lambda b, i, *_: (b, 0)),
                  pl.BlockSpec((ATT_TQ, 1), lambda b, i, *_: (b * nq + i, 0)),
                  pl.BlockSpec((nk, 1, ATT_TK), lambda b, i, *_: (b, 0, 0))],
        out_specs=pl.BlockSpec((ATT_TQ, MLA_HEADS * V_HEAD), lambda b, i, *_: (b * nq + i, 0)),
        scratch_shapes=[pltpu.VMEM((MLA_HEADS, ATT_TQ, LANES), F32),
                        pltpu.VMEM((MLA_HEADS, ATT_TQ, LANES), F32),
                        pltpu.VMEM((MLA_HEADS, ATT_TQ, V_HEAD), F32)],
    )
    return pl.pallas_call(
        functools.partial(_attn_kernel, nq=nq),
        grid_spec=grid_spec,
        out_shape=jax.ShapeDtypeStruct((bn * l, MLA_HEADS * V_HEAD), BF16),
        compiler_params=_cparams(("arbitrary", "arbitrary")),
        name="attention",
    )(lo, hi, q, k, v, qc, kc)


def _s5_kernel(u_ref, wb_ref, a_ref, wc_ref, d_ref, o_ref, x_scr, st_scr):
    rows = S5_TC * SUBLANES
    nsub = rows // S5_RB

    @pl.when(pl.program_id(1) == 0)
    def _():
        st_scr[...] = jnp.zeros_like(st_scr)

    even = (lax.broadcasted_iota(jnp.int32, (S5_RB, 1), 0) & 1) == 0
    half = S5_RB // 2
    nre = S5_LANES // LANES
    uv = u_ref.at[0]

    def mm_in(r, _):
        off = pl.multiple_of(r * S5_RB, S5_RB)
        for s in range(2):
            us = uv[pl.ds(off + s, half, stride=2), :].astype(BF16)
            out = jnp.dot(us, wb_ref[0, :, s * 2 * S5_LANES:(s + 1) * 2 * S5_LANES], preferred_element_type=F32)
            for c in range(2 * nre):
                x_scr.at[c][pl.ds(off + s, half, stride=2), :] = out[:, c * LANES:(c + 1) * LANES]
        return 0

    lax.fori_loop(0, nsub, mm_in, 0)

    a_c = [a_ref[0, :, c * LANES:(c + 1) * LANES] for c in range(2 * nre)]

    def step(t, carry):
        off = pl.multiple_of(t * SUBLANES, SUBLANES)
        new = [None] * (2 * nre)
        for c in range(nre):
            xr, xi = carry[c], carry[nre + c]
            ar, ai = a_c[c], a_c[nre + c]
            nr = ar * xr - ai * xi + x_scr[c, pl.ds(off, SUBLANES), :]
            ni = ar * xi + ai * xr + x_scr[nre + c, pl.ds(off, SUBLANES), :]
            x_scr[c, pl.ds(off, SUBLANES), :] = nr
            x_scr[nre + c, pl.ds(off, SUBLANES), :] = ni
            new[c], new[nre + c] = nr, ni
        return tuple(new)

    init = tuple(st_scr[:, c * LANES:(c + 1) * LANES] for c in range(2 * nre))
    fin = lax.fori_loop(0, S5_TC, step, init, unroll=8)
    for c in range(2 * nre):
        st_scr[:, c * LANES:(c + 1) * LANES] = fin[c]

    d = jnp.concatenate([d_ref[0]] * (S5_RB // SUBLANES), axis=0)

    def mm_out(r, _):
        off = pl.multiple_of(r * S5_RB, S5_RB)
        x = jnp.concatenate([x_scr[c, pl.ds(off, S5_RB), :] for c in range(2 * nre)], axis=1).astype(BF16)
        out = jnp.dot(x, wc_ref[0], preferred_element_type=F32)
        y = jnp.where(even, out[:, :LANES], out[:, LANES:]) + d * u_ref[0, pl.ds(off, S5_RB), :]
        o_ref[0, pl.ds(off, S5_RB), :] = jax.nn.gelu(y).astype(o_ref.dtype)
        return 0

    lax.fori_loop(0, nsub, mm_out, 0)


def _s5_params(a_re, a_im, log_dt, b_re, b_im, c_re, c_im, d_skip):
    step = jnp.exp(log_dt)[:, None]
    mag = jnp.exp(a_re * step)
    abar_re, abar_im = mag * jnp.cos(a_im * step), mag * jnp.sin(a_im * step)
    den = a_re * a_re + a_im * a_im
    nr, ni = abar_re - 1.0, abar_im
    f_re, f_im = (nr * a_re + ni * a_im) / den, (ni * a_re - nr * a_im) / den
    bbar_re = f_re[..., None] * b_re - f_im[..., None] * b_im
    bbar_im = f_re[..., None] * b_im + f_im[..., None] * b_re
    eye = jnp.eye(S5_GB, dtype=F32)
    bb = jnp.stack([bbar_re, bbar_im]).reshape(2, S5_NBLK, S5_GB, S5_STATE, S5_GROUP)
    wb = jnp.einsum('ab,rjapc->jacrbp', eye, bb).reshape(S5_NBLK, LANES, 2 * S5_LANES)
    wb = wb.reshape(S5_PAIRS, 2, LANES, 2 * S5_LANES).transpose(0, 2, 1, 3).reshape(S5_PAIRS, LANES, 4 * S5_LANES)
    cc = jnp.stack([c_re, -c_im]).reshape(2, S5_NBLK, S5_GB, S5_GROUP, S5_STATE)
    wc = jnp.einsum('ab,rjacp->jrapbc', eye, cc).reshape(S5_NBLK, 2 * S5_LANES, LANES)
    wc = wc.reshape(S5_PAIRS, 2, 2 * S5_LANES, LANES).transpose(0, 2, 1, 3).reshape(S5_PAIRS, 2 * S5_LANES, 2 * LANES)
    ab = jnp.concatenate([abar_re.reshape(S5_NBLK, S5_LANES), abar_im.reshape(S5_NBLK, S5_LANES)], axis=1)
    ab = jnp.tile(ab.reshape(S5_PAIRS, 2, 2 * S5_LANES), (1, SUBLANES // 2, 1))
    dd = jnp.tile(d_skip.reshape(S5_PAIRS, 2, LANES), (1, SUBLANES // 2, 1))
    return wb.astype(BF16), ab, wc.astype(BF16), dd


def _s5(u6, wb, ab, wc, dd, l):
    rows = S5_TC * SUBLANES
    return pl.pallas_call(
        _s5_kernel,
        grid=(S5_PAIRS, l // S5_TC),
        in_specs=[pl.BlockSpec((1, rows, LANES), lambda k, c: (k, c, 0)),
                  pl.BlockSpec((1, LANES, 4 * S5_LANES), lambda k, c: (k, 0, 0)),
                  pl.BlockSpec((1, SUBLANES, 2 * S5_LANES), lambda k, c: (k, 0, 0)),
                  pl.BlockSpec((1, 2 * S5_LANES, 2 * LANES), lambda k, c: (k, 0, 0)),
                  pl.BlockSpec((1, SUBLANES, LANES), lambda k, c: (k, 0, 0))],
        out_specs=pl.BlockSpec((1, rows, LANES), lambda k, c: (k, c, 0)),
        out_shape=jax.ShapeDtypeStruct(u6.shape, F32),
        scratch_shapes=[pltpu.VMEM((2 * S5_LANES // LANES, rows, LANES), F32),
                        pltpu.VMEM((SUBLANES, 2 * S5_LANES), F32)],
        compiler_params=_cparams(("arbitrary", "arbitrary")),
        name="s5",
    )(u6, wb, ab, wc, dd)


def _glu_kernel(y_ref, w_ref, o_ref, z_scr, *, bn):
    b = pl.program_id(1)
    rows = z_scr.shape[1]

    @pl.when(b == 0)
    def _():
        y = jnp.concatenate([y_ref.at[k][pl.ds(s, rows, stride=2), :] for k in range(S5_PAIRS) for s in range(2)],
                            axis=1)
        g = jnp.dot(y.astype(BF16), w_ref[...], preferred_element_type=F32)
        z = y * jax.nn.sigmoid(g)
        for c in range(S5_NBLK):
            z_scr[c] = z[:, c * LANES:(c + 1) * LANES]

    o_ref[0] = jnp.concatenate([z_scr.at[c][pl.ds(b, rows // bn, stride=bn), :] for c in range(S5_NBLK)],
                               axis=1).astype(o_ref.dtype)


def _glu(y6, w_glu, bn, l):
    tt = 256
    rows = tt * bn
    return pl.pallas_call(
        functools.partial(_glu_kernel, bn=bn),
        grid=(l // tt, bn),
        in_specs=[pl.BlockSpec((S5_PAIRS, rows * 2, LANES), lambda i, b: (0, i, 0)),
                  pl.BlockSpec((S5_CH, S5_CH), lambda i, b: (0, 0))],
        out_specs=pl.BlockSpec((1, tt, S5_CH), lambda i, b: (b, i, 0)),
        out_shape=jax.ShapeDtypeStruct((bn, l, S5_CH), BF16),
        scratch_shapes=[pltpu.VMEM((S5_NBLK, rows, LANES), F32)],
        compiler_params=_cparams(("arbitrary", "arbitrary")),
        name="glu",
    )(y6, w_glu)


def _merge_kernel(att_ref, z_ref, gm_ref, gs_ref, x_ref, mod_ref, gpost_ref, gpre_ref,
                  wbm_ref, wbs_ref, wo_ref, wrh_ref, wrl_ref, x1_ref, h2b_ref, h2s_ref, lg_ref):
    tm = x_ref.shape[1]
    th = tm // MERGE_SUBTILES
    for r0 in range(0, tm, th):
        rows = slice(r0, r0 + th)
        ym = jnp.dot(att_ref[rows, :], wbm_ref[...], preferred_element_type=F32)
        ys = jnp.dot(z_ref[rows, :], wbs_ref[...], preferred_element_type=F32)
        mixed_in = (gm_ref[rows, :].astype(F32) * ym + gs_ref[rows, :].astype(F32) * ys).astype(BF16)
        mixed = jnp.dot(mixed_in, wo_ref[...], preferred_element_type=F32)
        x1 = x_ref[0, rows, :] + mod_ref[0, 2:3, :] * _rms(mixed, gpost_ref[...])
        x1_ref[0, rows, :] = x1
        h2 = _rms(x1, gpre_ref[...]) * (1.0 + mod_ref[0, 4:5, :]) + mod_ref[0, 3:4, :]
        h2_hi = h2.astype(BF16)
        h2b_ref[0, rows, :] = h2_hi
        for s in range(SLAB):
            h2s_ref[pl.ds(r0 * SLAB + s, th, stride=SLAB), :] = h2[:, s * LANES:(s + 1) * LANES]
        h2_lo = (h2 - h2_hi.astype(F32)).astype(BF16)
        lg_ref[:, rows] = (lax.dot_general(wrh_ref[...], h2_hi, NT_DIMS, preferred_element_type=F32)
                           + lax.dot_general(wrl_ref[...], h2_hi, NT_DIMS, preferred_element_type=F32)
                           + lax.dot_general(wrh_ref[...], h2_lo, NT_DIMS, preferred_element_type=F32))


def _merge(att, z_tm, gates, x, mod, g_post, g_pre_ffn, w_br_mla, w_br_s5, w_out, wr_hi, wr_lo):
    bn, l, d = x.shape
    tm = 256
    nl = l // tm
    row = lambda b, i: (b * nl + i, 0)
    const = lambda b, i: (0, 0)
    one = pl.Buffered(1)
    return pl.pallas_call(
        _merge_kernel,
        grid=(bn, nl),
        in_specs=[pl.BlockSpec((tm, MLA_HEADS * V_HEAD), row),
                  pl.BlockSpec((tm, S5_CH), row),
                  pl.BlockSpec((tm, d), lambda b, i: (b * nl + i, 0)),
                  pl.BlockSpec((tm, d), lambda b, i: (b * nl + i, 1)),
                  pl.BlockSpec((1, tm, d), lambda b, i: (b, i, 0)),
                  pl.BlockSpec((1, 6, d), lambda b, i: (b, 0, 0)),
                  pl.BlockSpec((1, d), const),
                  pl.BlockSpec((1, d), const),
                  pl.BlockSpec((MLA_HEADS * V_HEAD, d), const, pipeline_mode=one),
                  pl.BlockSpec((S5_CH, d), const, pipeline_mode=one),
                  pl.BlockSpec((d, d), const, pipeline_mode=one),
                  pl.BlockSpec((N_EXPERTS, d), const, pipeline_mode=one),
                  pl.BlockSpec((N_EXPERTS, d), const, pipeline_mode=one)],
        out_specs=[pl.BlockSpec((1, tm, d), lambda b, i: (b, i, 0)),
                   pl.BlockSpec((1, tm, d), lambda b, i: (b, i, 0)),
                   pl.BlockSpec((tm * SLAB, LANES), row),
                   pl.BlockSpec((N_EXPERTS, tm), lambda b, i: (0, b * nl + i))],
        out_shape=[jax.ShapeDtypeStruct((bn, l, d), F32),
                   jax.ShapeDtypeStruct((bn, l, d), BF16),
                   jax.ShapeDtypeStruct((bn * l * SLAB, LANES), F32),
                   jax.ShapeDtypeStruct((N_EXPERTS, bn * l), F32)],
        compiler_params=_cparams(("arbitrary", "arbitrary")),
        name="merge",
    )(att, z_tm, gates, gates, x, mod, g_post.reshape(1, d), g_pre_ffn.reshape(1, d),
      w_br_mla, w_br_s5, w_out, wr_hi, wr_lo)


def _route_kernel(lg_ref, b_ref, ti_ref, tw_ref, cnt_ref):
    ng = N_EXPERT_GROUPS
    gsz = N_EXPERTS // ng
    lg = lg_ref[...]
    t = lg.shape[-1]
    sc = jax.nn.sigmoid(lg)
    sel = sc + b_ref[...]
    ninf = -jnp.inf
    i_in = lax.broadcasted_iota(jnp.int32, (ng, gsz, t), 1).astype(F32)
    m1 = jnp.max(sel, axis=1, keepdims=True)
    idx1 = jnp.min(jnp.where(sel == m1, i_in, float(gsz)), axis=1, keepdims=True)
    m2 = jnp.max(jnp.where(i_in == idx1, ninf, sel), axis=1, keepdims=True)
    gs = m1 + m2
    g_i = lax.broadcasted_iota(jnp.int32, (ng, 1, t), 0).astype(F32)
    picked = jnp.zeros((ng, 1, t), F32)
    cur = gs
    for _ in range(TOPK_GROUPS):
        m = jnp.max(cur, axis=0, keepdims=True)
        idx = jnp.min(jnp.where(cur == m, g_i, float(ng)), axis=0, keepdims=True)
        hit = g_i == idx
        picked = jnp.where(hit, 1.0, picked)
        cur = jnp.where(hit, ninf, cur)
    cand = jnp.where(picked > 0.5, sel, ninf)
    e_i = lax.broadcasted_iota(jnp.int32, (ng, gsz, t), 0).astype(F32) * float(gsz) + i_in
    ws = []
    hits = jnp.zeros((ng, gsz, t), F32)
    for r in range(TOP_K):
        m = jnp.max(jnp.max(cand, axis=1, keepdims=True), axis=0, keepdims=True)
        idx = jnp.min(jnp.min(jnp.where(cand == m, e_i, float(N_EXPERTS)), axis=1, keepdims=True),
                      axis=0, keepdims=True)
        hit = e_i == idx
        w = jnp.sum(jnp.sum(jnp.where(hit, sc, 0.0), axis=1, keepdims=True), axis=0, keepdims=True)
        ti_ref[r:r + 1, :] = idx[0].astype(jnp.int32)
        ws.append(w[0])
        hits = jnp.where(hit, 1.0, hits)
        cand = jnp.where(hit, ninf, cand)

    @pl.when(pl.program_id(0) == 0)
    def _():
        cnt_ref[...] = jnp.zeros(cnt_ref.shape, F32)

    cnt_ref[...] += jnp.sum(hits, axis=2, keepdims=True)
    tot = ws[0]
    for w in ws[1:]:
        tot = tot + w
    for r in range(TOP_K):
        tw_ref[r:r + 1, :] = ws[r] / tot * ROUTED_SCALE
    for r in range(TOP_K, SUBLANES):
        ti_ref[r:r + 1, :] = jnp.zeros((1, t), jnp.int32)
        tw_ref[r:r + 1, :] = jnp.zeros((1, t), F32)


def _route(logits_t, router_bias):
    n = logits_t.shape[1]
    ng = N_EXPERT_GROUPS
    gsz = N_EXPERTS // ng
    tn = 1024
    return pl.pallas_call(
        _route_kernel,
        grid=(n // tn,),
        in_specs=[pl.BlockSpec((ng, gsz, tn), lambda i: (0, 0, i)),
                  pl.BlockSpec((ng, gsz, 1), lambda i: (0, 0, 0))],
        out_specs=[pl.BlockSpec((SUBLANES, tn), lambda i: (0, i)),
                   pl.BlockSpec((SUBLANES, tn), lambda i: (0, i)),
                   pl.BlockSpec((ng, gsz, 1), lambda i: (0, 0, 0))],
        out_shape=[jax.ShapeDtypeStruct((SUBLANES, n), jnp.int32),
                   jax.ShapeDtypeStruct((SUBLANES, n), F32),
                   jax.ShapeDtypeStruct((ng, gsz, 1), F32)],
        compiler_params=_cparams(("arbitrary",)),
        name="route",
    )(logits_t.reshape(ng, gsz, n), router_bias.reshape(ng, gsz, 1))


def _moe_kernel(blk_e_ref, nused_ref, nvalid_ref, nxt_e_ref, wslot_ref, t_ref, t1_ref, t2_ref, a_ref, h_hbm,
                wg_hbm, wu_hbm, wd_hbm, y_hbm, xbuf, ybuf, wg_buf, wu_buf, wd_buf, wg_s, wu_s, wd_s,
                gsem, wsem, wtsem):
    j = pl.program_id(0)
    n_used = nused_ref[0]
    slot = lax.rem(j, GATHER_SLOTS)
    nv = nvalid_ref[j]
    nv_prev = nvalid_ref[jnp.maximum(j - 1, 0)]

    def slab(ref, row0, align=SUBLANES):
        return ref.at[pl.ds(pl.multiple_of(row0, align), SLAB), :]

    def start_gather(idx_ref, s):
        def body(r, _):
            pltpu.make_async_copy(slab(h_hbm, idx_ref[0, 0, r]), slab(xbuf.at[s], r * MOE_PITCH, PITCH_ALIGN),
                                  gsem.at[s]).start()
            return 0
        lax.fori_loop(0, MOE_TB, body, 0, unroll=8)

    def wait_gather(s):
        rows = MOE_TB * SLAB
        pltpu.make_async_copy(h_hbm.at[pl.ds(0, rows), :], xbuf.at[s, pl.ds(0, rows), :], gsem.at[s]).wait()

    def weight_copies(expert, s):
        return (pltpu.make_async_copy(wg_hbm.at[expert], wg_buf.at[s], wtsem.at[s]),
                pltpu.make_async_copy(wu_hbm.at[expert], wu_buf.at[s], wtsem.at[s]),
                pltpu.make_async_copy(wd_hbm.at[expert], wd_buf.at[s], wtsem.at[s]))

    def wait_writes(count):
        p = MOE_TB
        while p >= 1:
            @pl.when((count & p) != 0)
            def _(p=p):
                rows = p * SLAB
                pltpu.make_async_copy(ybuf.at[pl.ds(0, rows), :], y_hbm.at[pl.ds(0, rows), :], wsem.at[0]).wait()
            p //= 2

    @pl.when(j < n_used)
    def _():
        @pl.when(j == 0)
        def _():
            start_gather(t_ref, 0)

            @pl.when(n_used > 1)
            def _():
                start_gather(t1_ref, 1)

        e = blk_e_ref[j]
        ws = wslot_ref[j]

        @pl.when(j == 0)
        def _():
            for cp in weight_copies(e, ws):
                cp.start(priority=WEIGHT_DMA_PRIORITY)

        first = jnp.logical_or(j == 0, e != blk_e_ref[jnp.maximum(j - 1, 0)])

        @pl.when(first)
        def _():
            for cp in weight_copies(e, ws):
                cp.wait()
            wg_s[...] = wg_buf[ws].astype(BF16)
            wu_s[...] = wu_buf[ws].astype(BF16)
            wd_s[...] = wd_buf[ws].astype(BF16)

        wait_gather(slot)

        @pl.when(j + 2 < n_used)
        def _():
            start_gather(t2_ref, lax.rem(j + 2, GATHER_SLOTS))

        @pl.when(jnp.logical_and(first, nxt_e_ref[j] >= 0))
        def _():
            for cp in weight_copies(nxt_e_ref[j], 1 - ws):
                cp.start(priority=WEIGHT_DMA_PRIORITY)

        xs = xbuf.at[slot]
        x = jnp.concatenate([xs[pl.ds(s, MOE_TB, stride=MOE_PITCH), :] for s in range(SLAB)], axis=1).astype(BF16)
        g = jnp.dot(x, wg_s[...], preferred_element_type=F32)
        u = jnp.dot(x, wu_s[...], preferred_element_type=F32)
        hm = (g * jax.nn.sigmoid(g) * u).astype(BF16)
        y = jnp.dot(hm, wd_s[...], preferred_element_type=F32)

        @pl.when(j > 0)
        def _():
            wait_writes(nv_prev)

        for s in range(SLAB):
            ybuf[pl.ds(s, MOE_TB, stride=MOE_PITCH), :] = y[:, s * LANES:(s + 1) * LANES]

        def start_write(r):
            pltpu.make_async_copy(slab(ybuf, r * MOE_PITCH, PITCH_ALIGN), slab(y_hbm, a_ref[0, 0, r]),
                                  wsem.at[0]).start()

        def wgroup(g8, _):
            for q in range(SUBLANES):
                start_write(g8 * SUBLANES + q)
            return 0

        def wtail(r, _):
            start_write(r)
            return 0

        n_groups = lax.shift_right_logical(nv, 3)
        lax.fori_loop(0, n_groups, wgroup, 0)
        lax.fori_loop(n_groups * SUBLANES, nv, wtail, 0)

        @pl.when(j == n_used - 1)
        def _():
            wait_writes(nv)


def _moe(h2s, topi, counts, w_gate, w_up, w_down):
    n = h2s.shape[0] // SLAB
    d = D_MODEL
    nk = n * TOP_K
    tb = MOE_TB
    n_blocks = nk // tb + N_EXPERTS
    flat_e = topi[:TOP_K].reshape(nk)
    _, order = lax.sort_key_val(flat_e, jnp.arange(nk, dtype=jnp.int32))
    start = jnp.cumsum(counts) - counts
    padded = (counts + tb - 1) // tb * tb
    pad_end = jnp.cumsum(padded)
    pad_start = pad_end - padded
    n_used = (pad_end[-1] // tb).astype(jnp.int32).reshape(1)
    blk_p0 = jnp.arange(n_blocks, dtype=jnp.int32) * tb
    blk_e = jnp.minimum(jnp.sum((pad_end[None, :] <= blk_p0[:, None]).astype(jnp.int32), axis=1), N_EXPERTS - 1)
    blk_off = blk_p0 - pad_start[blk_e]
    nvalid = jnp.clip(counts[blk_e] - blk_off, 0, tb).astype(jnp.int32)
    rank = (start[blk_e] + blk_off)[:, None] + jnp.arange(tb, dtype=jnp.int32)[None, :]
    a3 = order[jnp.clip(rank, 0, nk - 1)].reshape(n_blocks, 1, tb)
    arow = a3 * SLAB
    trow = (a3 % n) * SLAB
    used = counts > 0
    e_ids = jnp.arange(N_EXPERTS, dtype=jnp.int32)
    later = jnp.where(used, e_ids, N_EXPERTS)
    nxt = jnp.concatenate([lax.cummin(later[::-1])[::-1][1:], jnp.full((1,), N_EXPERTS, jnp.int32)])
    nxt_e = jnp.where(nxt < N_EXPERTS, nxt, -1)[blk_e].astype(jnp.int32)
    wslot = ((jnp.cumsum(used.astype(jnp.int32)) - 1) % 2)[blk_e].astype(jnp.int32)

    smem_blk = lambda f: pl.BlockSpec((1, 1, tb), f, memory_space=pltpu.SMEM)
    any_spec = pl.BlockSpec(memory_space=pl.ANY)
    grid_spec = pltpu.PrefetchScalarGridSpec(
        num_scalar_prefetch=5,
        grid=(n_blocks,),
        in_specs=[smem_blk(lambda j, *_: (j, 0, 0)),
                  smem_blk(lambda j, *_: (jnp.minimum(j + 1, n_blocks - 1), 0, 0)),
                  smem_blk(lambda j, *_: (jnp.minimum(j + 2, n_blocks - 1), 0, 0)),
                  smem_blk(lambda j, *_: (j, 0, 0)),
                  any_spec, any_spec, any_spec, any_spec],
        out_specs=any_spec,
        scratch_shapes=[pltpu.VMEM((GATHER_SLOTS, tb * MOE_PITCH, LANES), F32),
                        pltpu.VMEM((tb * MOE_PITCH, LANES), F32),
                        pltpu.VMEM((2, d, D_EXPERT), F32),
                        pltpu.VMEM((2, d, D_EXPERT), F32),
                        pltpu.VMEM((2, D_EXPERT, d), F32),
                        pltpu.VMEM((d, D_EXPERT), BF16),
                        pltpu.VMEM((d, D_EXPERT), BF16),
                        pltpu.VMEM((D_EXPERT, d), BF16),
                        pltpu.SemaphoreType.DMA((GATHER_SLOTS,)),
                        pltpu.SemaphoreType.DMA((1,)),
                        pltpu.SemaphoreType.DMA((2,))],
    )
    return pl.pallas_call(
        _moe_kernel,
        grid_spec=grid_spec,
        out_shape=jax.ShapeDtypeStruct((TOP_K * n * SLAB, LANES), F32),
        compiler_params=_cparams(("arbitrary",)),
        name="moe",
    )(blk_e, n_used, nvalid, nxt_e, wslot, trow, trow, trow, arow, h2s, w_gate, w_up, w_down)


def _final_kernel(h_ref, y0, y1, y2, y3, y4, y5, tw_ref, x1_ref, mod_ref, g_ref, wg_ref, wu_ref, wd_ref, o_ref):
    h = h_ref[...]
    tm = h.shape[0]
    g = jnp.dot(h, wg_ref[...], preferred_element_type=F32)
    u = jnp.dot(h, wu_ref[...], preferred_element_type=F32)
    hm = (g * jax.nn.sigmoid(g) * u).astype(BF16)
    ffn = jnp.dot(hm, wd_ref[...], preferred_element_type=F32)
    tw = tw_ref[...]
    for k, y in enumerate((y0, y1, y2, y3, y4, y5)):
        yk = jnp.concatenate([y[pl.ds(s, tm, stride=SLAB), :] for s in range(SLAB)], axis=1)
        ffn = ffn + yk * tw[:, k:k + 1]
    o_ref[0] = x1_ref[0] + mod_ref[0, 5:6, :] * _rms(ffn, g_ref[...])


def _final(h2, y_flat, topw_t, x1, mod, g_post_ffn, w_sg, w_su, w_sd):
    bn, l, d = x1.shape
    n = bn * l
    tm = 256
    nl = l // tm
    const = lambda b, i: (0, 0)
    one = pl.Buffered(1)
    y_specs = [pl.BlockSpec((tm * SLAB, LANES), functools.partial(lambda b, i, k: (k * (n // tm) + b * nl + i, 0), k=k))
               for k in range(TOP_K)]
    return pl.pallas_call(
        _final_kernel,
        grid=(bn, nl),
        in_specs=[pl.BlockSpec((tm, d), lambda b, i: (b * nl + i, 0))] + y_specs + [
            pl.BlockSpec((tm, SUBLANES), lambda b, i: (b * nl + i, 0)),
            pl.BlockSpec((1, tm, d), lambda b, i: (b, i, 0)),
            pl.BlockSpec((1, 6, d), lambda b, i: (b, 0, 0)),
            pl.BlockSpec((1, d), const),
            pl.BlockSpec((d, D_EXPERT), const, pipeline_mode=one),
            pl.BlockSpec((d, D_EXPERT), const, pipeline_mode=one),
            pl.BlockSpec((D_EXPERT, d), const, pipeline_mode=one)],
        out_specs=pl.BlockSpec((1, tm, d), lambda b, i: (b, i, 0)),
        out_shape=jax.ShapeDtypeStruct((bn, l, d), F32),
        compiler_params=_cparams(("arbitrary", "arbitrary")),
        name="final",
    )(h2, *([y_flat] * TOP_K), topw_t, x1, mod, g_post_ffn.reshape(1, d), w_sg, w_su, w_sd)


def _layer(li, x, c, positions, w_ada, b_ada, g_pre_mix, g_post_mix, g_pre_ffn, g_post_ffn, w_in, g_q, g_kv,
           w_uq, w_uk, w_uv, a_re, a_im, log_dt, b_re, b_im, c_re, c_im, d_skip, w_glu, w_br_mla, w_br_s5,
           w_out, w_router, router_bias, w_exp_gate, w_exp_up, w_exp_down, w_sh_gate, w_sh_up, w_sh_down):
    bn, l, d = x.shape
    n = bn * l
    mod = _adaln(c, w_ada, b_ada).reshape(bn, 6, d)

    o_kpe = Q_LORA + KV_LORA
    o_u = o_kpe + QK_ROPE
    o_g = o_u + S5_CH
    w_in_t = jnp.swapaxes(w_in, 1, 2)
    w_q = jnp.pad(w_uq.reshape(Q_LORA, MLA_HEADS, QK_NOPE + QK_ROPE),
                  ((0, 0), (0, 0), (0, QK_PAD - QK_NOPE - QK_ROPE))).reshape(Q_LORA, MLA_HEADS * QK_PAD).astype(BF16)
    wr_t = w_router.T
    wr_hi = wr_t.astype(BF16)
    wr_lo = (wr_t - wr_hi.astype(F32)).astype(BF16)

    rope_c, rope_s1, rope_s2 = _rope_tables(positions)
    chunk_id = positions // CHUNK

    h = _prenorm(x, mod, g_pre_mix, 0, 1).reshape(n, d)
    lat = _mmt(h, w_in_t, li, 0, o_kpe, F32, 1024, 512, name="mm_lat")
    kpe = _mmt(h, w_in_t, li, o_kpe, LANES, F32, 1024, LANES, name="mm_kpe")
    gates = _mmt(h, w_in_t, li, o_g, 2 * d, BF16, 1024, 1024, act="sigmoid", name="mm_gates")
    u6 = _mm_u(h, w_in_t, li, o_u, bn, l)

    q, k, v = _qkvproj(lat, g_q, g_kv, w_q, w_uk.astype(BF16), w_uv.astype(BF16), kpe, rope_c, rope_s1, rope_s2)
    att = _attention(q, k, v, chunk_id, bn, l)

    wb, ab, wc, dd = _s5_params(a_re, a_im, log_dt, b_re, b_im, c_re, c_im, d_skip)
    y6 = _s5(u6, wb, ab, wc, dd, l)
    z = _glu(y6, w_glu.astype(BF16), bn, l).reshape(n, S5_CH)

    x1, h2b, h2s, logits_t = _merge(att, z, gates, x, mod, g_post_mix, g_pre_ffn, w_br_mla.astype(BF16),
                                    w_br_s5.astype(BF16), w_out.astype(BF16), wr_hi, wr_lo)

    topi, topw, cnt = _route(logits_t, router_bias)
    y_slabs = _moe(h2s, topi, cnt.reshape(N_EXPERTS).astype(jnp.int32), w_exp_gate, w_exp_up, w_exp_down)
    return _final(h2b.reshape(n, d), y_slabs, topw.T, x1, mod, g_post_ffn, w_sh_gate.astype(BF16),
                  w_sh_up.astype(BF16), w_sh_down.astype(BF16))


def kernel(x, c, positions, w_ada, b_ada, g_pre_mix, g_post_mix, g_pre_ffn, g_post_ffn, w_in, g_q, g_kv, w_uq, w_uk, w_uv, a_re, a_im, log_dt, b_re, b_im, c_re, c_im, d_skip, w_glu, w_br_mla, w_br_s5, w_out, w_router, router_bias, w_exp_gate, w_exp_up, w_exp_down, w_sh_gate, w_sh_up, w_sh_down):
    depth = w_ada.shape[0]
    for li in range(depth):
        x = _layer(li, x, c, positions, w_ada[li], b_ada[li], g_pre_mix[li], g_post_mix[li], g_pre_ffn[li],
                   g_post_ffn[li], w_in, g_q[li], g_kv[li], w_uq[li], w_uk[li], w_uv[li], a_re[li], a_im[li],
                   log_dt[li], b_re[li], b_im[li], c_re[li], c_im[li], d_skip[li], w_glu[li], w_br_mla[li],
                   w_br_s5[li], w_out[li], w_router[li], router_bias[li], w_exp_gate[li], w_exp_up[li],
                   w_exp_down[li], w_sh_gate[li], w_sh_up[li], w_sh_down[li])
    return x
```

```python
import functools

import jax
import jax.numpy as jnp
from jax import lax
from jax.experimental import pallas as pl
from jax.experimental.pallas import tpu as pltpu

F32 = jnp.float32
BF16 = jnp.bfloat16

D_MODEL = 2048
CHUNK = 64
EPS = 1e-6
MLA_HEADS = 8
QK_NOPE = 128
QK_ROPE = 64
V_HEAD = 128
Q_LORA = 512
KV_LORA = 512
ROPE_THETA = 10000.0
S5_CH = 1024
S5_GROUP = 16
S5_GROUPS = S5_CH // S5_GROUP
S5_STATE = 64
N_EXPERTS = 64
TOP_K = 6
N_EXPERT_GROUPS = 8
TOPK_GROUPS = 4
D_EXPERT = 512
ROUTED_SCALE = 2.5

LANES = 128
SUBLANES = 8
QK_PAD = 2 * LANES
VMEM_LIMIT = 56 * 1024 * 1024
NEG = -1e30
LOG2E = 1.4426950408889634

S5_GB = LANES // S5_GROUP
S5_NBLK = S5_CH // LANES
S5_PAIRS = S5_NBLK // 2
S5_LANES = S5_GB * S5_STATE
S5_TC = 256
S5_RB = 512

ATT_TQ = 256
ATT_TK = 256
MOE_TB = 256
SLAB = D_MODEL // LANES
MERGE_SUBTILES = 2
GATHER_SLOTS = 3
WEIGHT_DMA_PRIORITY = 1
MOE_PITCH = 20
PITCH_ALIGN = 4


def _cparams(sem):
    return pltpu.CompilerParams(dimension_semantics=sem, vmem_limit_bytes=VMEM_LIMIT)


def _rms(x, g):
    return x * lax.rsqrt(jnp.mean(x * x, axis=-1, keepdims=True) + EPS) * g


def _adaln_kernel(c_ref, w_ref, b_ref, o_ref):
    c = c_ref[...]
    a = (c * jax.nn.sigmoid(c)).astype(BF16)
    o_ref[...] = jnp.dot(a, w_ref[...].astype(BF16), preferred_element_type=F32) + b_ref[...]


def _adaln(c, w, b):
    bn, d = c.shape
    n = w.shape[1]
    tn = 1024
    return pl.pallas_call(
        _adaln_kernel,
        grid=(n // tn,),
        in_specs=[pl.BlockSpec((bn, d), lambda j: (0, 0)),
                  pl.BlockSpec((d, tn), lambda j: (0, j)),
                  pl.BlockSpec((1, tn), lambda j: (0, j))],
        out_specs=pl.BlockSpec((bn, tn), lambda j: (0, j)),
        out_shape=jax.ShapeDtypeStruct((bn, n), F32),
        compiler_params=_cparams(("arbitrary",)),
        name="adaln",
    )(c, w, b.reshape(1, n))


def _prenorm_kernel(x_ref, mod_ref, g_ref, o_ref, *, sh_row, sc_row):
    y = _rms(x_ref[0], g_ref[...])
    o_ref[0] = (y * (1.0 + mod_ref[0, sc_row:sc_row + 1, :]) + mod_ref[0, sh_row:sh_row + 1, :]).astype(o_ref.dtype)


def _prenorm(x, mod, g, sh_row, sc_row):
    bn, l, d = x.shape
    tl = 512
    return pl.pallas_call(
        functools.partial(_prenorm_kernel, sh_row=sh_row, sc_row=sc_row),
        grid=(bn, l // tl),
        in_specs=[pl.BlockSpec((1, tl, d), lambda b, i: (b, i, 0)),
                  pl.BlockSpec((1, 6, d), lambda b, i: (b, 0, 0)),
                  pl.BlockSpec((1, d), lambda b, i: (0, 0))],
        out_specs=pl.BlockSpec((1, tl, d), lambda b, i: (b, i, 0)),
        out_shape=jax.ShapeDtypeStruct((bn, l, d), BF16),
        compiler_params=_cparams(("arbitrary", "arbitrary")),
        name="prenorm",
    )(x, mod, g.reshape(1, d))


NT_DIMS = (((1,), (1,)), ((), ()))


def _wt_tile(wa_ref, wb_ref, shift, tn):
    if shift == 0:
        return wa_ref[...].astype(BF16)
    return jnp.concatenate([wa_ref[...], wb_ref[...]], axis=0)[shift:shift + tn].astype(BF16)


def _wt_specs(layer, row0, tn, k, jmap):
    shift = row0 % LANES
    c0 = row0 - shift
    assert c0 % tn == 0 and shift % SUBLANES == 0
    specs = [pl.BlockSpec((None, tn, k), lambda *g: (layer, jmap(*g) + c0 // tn, 0))]
    if shift:
        specs.append(pl.BlockSpec((None, LANES, k), lambda *g: (layer, (c0 + (jmap(*g) + 1) * tn) // LANES, 0)))
    return specs, shift


def _mmt_kernel(a_ref, wa_ref, *rest, act, shift):
    wb_ref = rest[0] if shift else None
    o_ref, w_scr = rest[-2:]

    @pl.when(pl.program_id(1) == 0)
    def _():
        w_scr[...] = _wt_tile(wa_ref, wb_ref, shift, w_scr.shape[0])

    acc = lax.dot_general(a_ref[...], w_scr[...], NT_DIMS, preferred_element_type=F32)
    if act == "sigmoid":
        acc = jax.nn.sigmoid(acc)
    o_ref[...] = acc.astype(o_ref.dtype)


def _mmt(a, w_t, layer, row0, n, out_dtype, tm, tn, act=None, name="mmt"):
    m, k = a.shape
    w_specs, shift = _wt_specs(layer, row0, tn, k, lambda j, i: j)
    return pl.pallas_call(
        functools.partial(_mmt_kernel, act=act, shift=shift),
        grid=(n // tn, m // tm),
        in_specs=[pl.BlockSpec((tm, k), lambda j, i: (i, 0))] + w_specs,
        out_specs=pl.BlockSpec((tm, tn), lambda j, i: (i, j)),
        out_shape=jax.ShapeDtypeStruct((m, n), out_dtype),
        scratch_shapes=[pltpu.VMEM((tn, k), BF16)],
        compiler_params=_cparams(("arbitrary", "arbitrary")),
        name=name,
    )(a, *([w_t] * len(w_specs)))


def _mm_u_kernel(a_ref, wa_ref, *rest, shift):
    wb_ref = rest[0] if shift else None
    o_ref, w_scr = rest[-2:]
    b = pl.program_id(1)

    @pl.when(jnp.logical_and(pl.program_id(0) == 0, b == 0))
    def _():
        w_scr[...] = _wt_tile(wa_ref, wb_ref, shift, w_scr.shape[0])

    res = lax.dot_general(a_ref[...], w_scr[...], NT_DIMS, preferred_element_type=F32)
    tm = res.shape[0]
    for c in range(S5_NBLK):
        o_ref.at[c // 2][pl.ds(b * 2 + c % 2, tm, stride=SUBLANES), :] = res[:, c * LANES:(c + 1) * LANES]


def _mm_u(h, w_t, layer, row0, bn, l):
    m, k = h.shape
    tm = 512
    nl = l // tm
    assert bn * 2 == SUBLANES
    w_specs, shift = _wt_specs(layer, row0, S5_CH, k, lambda i, b: 0)
    return pl.pallas_call(
        functools.partial(_mm_u_kernel, shift=shift),
        grid=(nl, bn),
        in_specs=[pl.BlockSpec((tm, k), lambda i, b: (b * nl + i, 0))] + w_specs,
        out_specs=pl.BlockSpec((S5_PAIRS, tm * SUBLANES, LANES), lambda i, b: (0, i, 0)),
        out_shape=jax.ShapeDtypeStruct((S5_PAIRS, l * SUBLANES, LANES), F32),
        scratch_shapes=[pltpu.VMEM((S5_CH, k), BF16)],
        compiler_params=_cparams(("arbitrary", "arbitrary")),
        name="mm_u",
    )(h, *([w_t] * len(w_specs)))


def _rope_tab_kernel(pos_ref, k_ref, c_ref, s1_ref, s2_ref):
    ang = pos_ref[...].astype(F32) * k_ref[0:1, :]
    s = jnp.sin(ang)
    c_ref[...] = jnp.cos(ang) * k_ref[1:2, :]
    s1_ref[...] = s * k_ref[2:3, :]
    s2_ref[...] = s * k_ref[3:4, :]


def _rope_tables(positions):
    n = positions.size
    half = QK_ROPE // 2
    inv_freq = ROPE_THETA ** (-jnp.arange(half, dtype=F32) / half)
    zh, oh = jnp.zeros((half,), F32), jnp.ones((half,), F32)
    z2 = jnp.zeros((LANES - QK_ROPE,), F32)
    rows = [jnp.concatenate([inv_freq, inv_freq, z2]), jnp.concatenate([oh, oh, z2]),
            jnp.concatenate([-oh, zh, z2]), jnp.concatenate([zh, oh, z2])]
    consts = jnp.stack(rows + [jnp.zeros((LANES,), F32)] * (SUBLANES - len(rows)))
    tm = 1024
    tab = jax.ShapeDtypeStruct((n, LANES), F32)
    return pl.pallas_call(
        _rope_tab_kernel,
        grid=(n // tm,),
        in_specs=[pl.BlockSpec((tm, 1), lambda i: (i, 0)),
                  pl.BlockSpec((SUBLANES, LANES), lambda i: (0, 0))],
        out_specs=[pl.BlockSpec((tm, LANES), lambda i: (i, 0))] * 3,
        out_shape=[tab, tab, tab],
        compiler_params=_cparams(("arbitrary",)),
        name="rope_tables",
    )(positions.reshape(n, 1), consts)


def _rope_tile(t, c_ref, s1_ref, s2_ref):
    return (t * c_ref[...] + pltpu.roll(t, LANES - QK_ROPE // 2, 1) * s1_ref[...]
            + pltpu.roll(t, QK_ROPE // 2, 1) * s2_ref[...])


def _qkvproj_kernel(lat_ref, gq_ref, gkv_ref, wq_ref, wk_ref, wv_ref, kpe_ref, c_ref, s1_ref, s2_ref,
                    q_ref, k_ref, v_ref, *, scale):
    lat = lat_ref[...]
    qn = _rms(lat[:, :Q_LORA], gq_ref[...]).astype(BF16)
    cn = _rms(lat[:, Q_LORA:], gkv_ref[...]).astype(BF16)
    q = jnp.dot(qn, wq_ref[...], preferred_element_type=F32)
    kn = jnp.dot(cn, wk_ref[...], preferred_element_type=F32)
    v_ref[...] = jnp.dot(cn, wv_ref[...], preferred_element_type=F32).astype(v_ref.dtype)
    kt = _rope_tile(kpe_ref[...], c_ref, s1_ref, s2_ref).astype(k_ref.dtype)
    for h in range(MLA_HEADS):
        o = h * QK_PAD
        q_ref[:, o:o + LANES] = (q[:, o:o + LANES] * scale).astype(q_ref.dtype)
        qt = _rope_tile(q[:, o + LANES:o + QK_PAD], c_ref, s1_ref, s2_ref)
        q_ref[:, o + LANES:o + QK_PAD] = (qt * scale).astype(q_ref.dtype)
        k_ref[:, o:o + LANES] = kn[:, h * QK_NOPE:(h + 1) * QK_NOPE].astype(k_ref.dtype)
        k_ref[:, o + LANES:o + QK_PAD] = kt


def _qkvproj(latk, g_q, g_kv, w_q, w_k, w_v, rope_c, rope_s1, rope_s2):
    n = latk.shape[0]
    tm = 512
    row = lambda i: (i, 0)
    const = lambda i: (0, 0)
    tab = pl.BlockSpec((tm, LANES), row)
    kpe_spec = pl.BlockSpec((tm, LANES), lambda i: (i, (Q_LORA + KV_LORA) // LANES))
    return pl.pallas_call(
        functools.partial(_qkvproj_kernel, scale=(QK_NOPE + QK_ROPE) ** -0.5 * LOG2E),
        grid=(n // tm,),
        in_specs=[pl.BlockSpec((tm, Q_LORA + KV_LORA), row),
                  pl.BlockSpec((1, Q_LORA), const),
                  pl.BlockSpec((1, KV_LORA), const),
                  pl.BlockSpec((Q_LORA, MLA_HEADS * QK_PAD), const),
                  pl.BlockSpec((KV_LORA, MLA_HEADS * QK_NOPE), const),
                  pl.BlockSpec((KV_LORA, MLA_HEADS * V_HEAD), const),
                  kpe_spec, tab, tab, tab],
        out_specs=[pl.BlockSpec((tm, MLA_HEADS * QK_PAD), row),
                   pl.BlockSpec((tm, MLA_HEADS * QK_PAD), row),
                   pl.BlockSpec((tm, MLA_HEADS * V_HEAD), row)],
        out_shape=[jax.ShapeDtypeStruct((n, MLA_HEADS * QK_PAD), BF16),
                   jax.ShapeDtypeStruct((n, MLA_HEADS * QK_PAD), BF16),
                   jax.ShapeDtypeStruct((n, MLA_HEADS * V_HEAD), BF16)],
        compiler_params=_cparams(("arbitrary",)),
        name="qkvproj",
    )(latk, g_q.reshape(1, Q_LORA), g_kv.reshape(1, KV_LORA), w_q, w_k, w_v, latk, rope_c, rope_s1, rope_s2)


def _attn_kernel(lo_ref, hi_ref, q_ref, k_ref, v_ref, qc_ref, kc_ref, o_ref, m_scr, l_scr, acc_scr, *, nq):
    b = pl.program_id(0)
    i = pl.program_id(1)
    qc = qc_ref[...]
    m_scr[...] = jnp.full(m_scr.shape, NEG, F32)
    l_scr[...] = jnp.zeros(l_scr.shape, F32)
    acc_scr[...] = jnp.zeros(acc_scr.shape, F32)

    def make_body(masked):
        def body(j, _):
            off = pl.multiple_of(j * ATT_TK, ATT_TK)
            if masked:
                mask = kc_ref[j] <= qc
            for h in range(MLA_HEADS):
                q = q_ref[:, h * QK_PAD:(h + 1) * QK_PAD]
                k = k_ref[pl.ds(off, ATT_TK), h * QK_PAD:(h + 1) * QK_PAD]
                s = lax.dot_general(q, k, (((1,), (1,)), ((), ())), preferred_element_type=F32)
                if masked:
                    s = jnp.where(mask, s, NEG)
                m_old = m_scr[h]
                m_new = jnp.maximum(m_old, jnp.max(s, axis=-1, keepdims=True))
                p = jnp.exp2(s - jnp.concatenate([m_new] * (ATT_TK // LANES), axis=1))
                alpha = jnp.exp2(m_old - m_new)
                l_scr[h] = alpha * l_scr[h] + jnp.sum(p, axis=-1, keepdims=True)
                v = v_ref[pl.ds(off, ATT_TK), h * V_HEAD:(h + 1) * V_HEAD]
                acc_scr[h] = alpha * acc_scr[h] + jnp.dot(p.astype(BF16), v, preferred_element_type=F32)
                m_scr[h] = m_new
            return 0
        return body

    lo = lo_ref[b * nq + i]
    lax.fori_loop(0, lo, make_body(False), 0)
    lax.fori_loop(lo, hi_ref[b * nq + i], make_body(True), 0)
    for h in range(MLA_HEADS):
        o_ref[:, h * V_HEAD:(h + 1) * V_HEAD] = (acc_scr[h] / l_scr[h]).astype(o_ref.dtype)


def _attention(q, k, v, chunk_id, bn, l):
    nq = l // ATT_TQ
    nk = l // ATT_TK
    q_max = jnp.max(chunk_id.reshape(bn, nq, ATT_TQ), axis=-1)
    k_min = jnp.min(chunk_id.reshape(bn, nk, ATT_TK), axis=-1)
    needed = k_min[:, None, :] <= q_max[:, :, None]
    hi = jnp.max(jnp.where(needed, jnp.arange(1, nk + 1, dtype=jnp.int32), 0), axis=-1).reshape(bn * nq)
    q_min = jnp.min(chunk_id.reshape(bn, nq, ATT_TQ), axis=-1)
    k_max = jnp.max(chunk_id.reshape(bn, nk, ATT_TK), axis=-1)
    full = k_max[:, None, :] <= q_min[:, :, None]
    lo = jnp.min(jnp.where(full, nk, jnp.arange(nk, dtype=jnp.int32)), axis=-1).astype(jnp.int32).reshape(bn * nq)
    hi = jnp.maximum(hi, lo)
    qc = chunk_id.reshape(bn * l, 1)
    kc = chunk_id.reshape(bn * nk, 1, ATT_TK)
    grid_spec = pltpu.PrefetchScalarGridSpec(
        num_scalar_prefetch=2,
        grid=(bn, nq),
        in_specs=[pl.BlockSpec((ATT_TQ, MLA_HEADS * QK_PAD), lambda b, i, *_: (b * nq + i, 0)),
                  pl.BlockSpec((l, MLA_HEADS * QK_PAD), lambda b, i, *_: (b, 0)),
                  pl.BlockSpec((l, MLA_HEADS * V_HEAD), lambda b, i, *_: (b, 0)),
                  pl.BlockSpec((ATT_TQ, 1), lambda b, i, *_: (b * nq + i, 0)),
                  pl.BlockSpec((nk, 1, ATT_TK), lambda b, i, *_: (b, 0, 0))],
        out_specs=pl.BlockSpec((ATT_TQ, MLA_HEADS * V_HEAD), lambda b, i, *_: (b * nq + i, 0)),
        scratch_shapes=[pltpu.VMEM((MLA_HEADS, ATT_TQ, LANES), F32),
                        pltpu.VMEM((MLA_HEADS, ATT_TQ, LANES), F32),
                        pltpu.VMEM((MLA_HEADS, ATT_TQ, V_HEAD), F32)],
    )
    return pl.pallas_call(
        functools.partial(_attn_kernel, nq=nq),
        grid_spec=grid_spec,
        out_shape=jax.ShapeDtypeStruct((bn * l, MLA_HEADS * V_HEAD), BF16),
        compiler_params=_cparams(("arbitrary", "arbitrary")),
        name="attention",
    )(lo, hi, q, k, v, qc, kc)


def _s5_kernel(u_ref, wb_ref, a_ref, wc_ref, d_ref, o_ref, x_scr, st_scr):
    rows = S5_TC * SUBLANES
    nsub = rows // S5_RB

    @pl.when(pl.program_id(1) == 0)
    def _():
        st_scr[...] = jnp.zeros_like(st_scr)

    even = (lax.broadcasted_iota(jnp.int32, (S5_RB, 1), 0) & 1) == 0
    half = S5_RB // 2
    nre = S5_LANES // LANES
    uv = u_ref.at[0]

    def mm_in(r, _):
        off = pl.multiple_of(r * S5_RB, S5_RB)
        for s in range(2):
            us = uv[pl.ds(off + s, half, stride=2), :].astype(BF16)
            out = jnp.dot(us, wb_ref[0, :, s * 2 * S5_LANES:(s + 1) * 2 * S5_LANES], preferred_element_type=F32)
            for c in range(2 * nre):
                x_scr.at[c][pl.ds(off + s, half, stride=2), :] = out[:, c * LANES:(c + 1) * LANES]
        return 0

    lax.fori_loop(0, nsub, mm_in, 0)

    a_c = [a_ref[0, :, c * LANES:(c + 1) * LANES] for c in range(2 * nre)]

    def step(t, carry):
        off = pl.multiple_of(t * SUBLANES, SUBLANES)
        new = [None] * (2 * nre)
        for c in range(nre):
            xr, xi = carry[c], carry[nre + c]
            ar, ai = a_c[c], a_c[nre + c]
            nr = ar * xr - ai * xi + x_scr[c, pl.ds(off, SUBLANES), :]
            ni = ar * xi + ai * xr + x_scr[nre + c, pl.ds(off, SUBLANES), :]
            x_scr[c, pl.ds(off, SUBLANES), :] = nr
            x_scr[nre + c, pl.ds(off, SUBLANES), :] = ni
            new[c], new[nre + c] = nr, ni
        return tuple(new)

    init = tuple(st_scr[:, c * LANES:(c + 1) * LANES] for c in range(2 * nre))
    fin = lax.fori_loop(0, S5_TC, step, init, unroll=8)
    for c in range(2 * nre):
        st_scr[:, c * LANES:(c + 1) * LANES] = fin[c]

    d = jnp.concatenate([d_ref[0]] * (S5_RB // SUBLANES), axis=0)

    def mm_out(r, _):
        off = pl.multiple_of(r * S5_RB, S5_RB)
        x = jnp.concatenate([x_scr[c, pl.ds(off, S5_RB), :] for c in range(2 * nre)], axis=1).astype(BF16)
        out = jnp.dot(x, wc_ref[0], preferred_element_type=F32)
        y = jnp.where(even, out[:, :LANES], out[:, LANES:]) + d * u_ref[0, pl.ds(off, S5_RB), :]
        o_ref[0, pl.ds(off, S5_RB), :] = jax.nn.gelu(y).astype(o_ref.dtype)
        return 0

    lax.fori_loop(0, nsub, mm_out, 0)


def _s5_params(a_re, a_im, log_dt, b_re, b_im, c_re, c_im, d_skip):
    step = jnp.exp(log_dt)[:, None]
    mag = jnp.exp(a_re * step)
    abar_re, abar_im = mag * jnp.cos(a_im * step), mag * jnp.sin(a_im * step)
    den = a_re * a_re + a_im * a_im
    nr, ni = abar_re - 1.0, abar_im
    f_re, f_im = (nr * a_re + ni * a_im) / den, (ni * a_re - nr * a_im) / den
    bbar_re = f_re[..., None] * b_re - f_im[..., None] * b_im
    bbar_im = f_re[..., None] * b_im + f_im[..., None] * b_re
    eye = jnp.eye(S5_GB, dtype=F32)
    bb = jnp.stack([bbar_re, bbar_im]).reshape(2, S5_NBLK, S5_GB, S5_STATE, S5_GROUP)
    wb = jnp.einsum('ab,rjapc->jacrbp', eye, bb).reshape(S5_NBLK, LANES, 2 * S5_LANES)
    wb = wb.reshape(S5_PAIRS, 2, LANES, 2 * S5_LANES).transpose(0, 2, 1, 3).reshape(S5_PAIRS, LANES, 4 * S5_LANES)
    cc = jnp.stack([c_re, -c_im]).reshape(2, S5_NBLK, S5_GB, S5_GROUP, S5_STATE)
    wc = jnp.einsum('ab,rjacp->jrapbc', eye, cc).reshape(S5_NBLK, 2 * S5_LANES, LANES)
    wc = wc.reshape(S5_PAIRS, 2, 2 * S5_LANES, LANES).transpose(0, 2, 1, 3).reshape(S5_PAIRS, 2 * S5_LANES, 2 * LANES)
    ab = jnp.concatenate([abar_re.reshape(S5_NBLK, S5_LANES), abar_im.reshape(S5_NBLK, S5_LANES)], axis=1)
    ab = jnp.tile(ab.reshape(S5_PAIRS, 2, 2 * S5_LANES), (1, SUBLANES // 2, 1))
    dd = jnp.tile(d_skip.reshape(S5_PAIRS, 2, LANES), (1, SUBLANES // 2, 1))
    return wb.astype(BF16), ab, wc.astype(BF16), dd


def _s5(u6, wb, ab, wc, dd, l):
    rows = S5_TC * SUBLANES
    return pl.pallas_call(
        _s5_kernel,
        grid=(S5_PAIRS, l // S5_TC),
        in_specs=[pl.BlockSpec((1, rows, LANES), lambda k, c: (k, c, 0)),
                  pl.BlockSpec((1, LANES, 4 * S5_LANES), lambda k, c: (k, 0, 0)),
                  pl.BlockSpec((1, SUBLANES, 2 * S5_LANES), lambda k, c: (k, 0, 0)),
                  pl.BlockSpec((1, 2 * S5_LANES, 2 * LANES), lambda k, c: (k, 0, 0)),
                  pl.BlockSpec((1, SUBLANES, LANES), lambda k, c: (k, 0, 0))],
        out_specs=pl.BlockSpec((1, rows, LANES), lambda k, c: (k, c, 0)),
        out_shape=jax.ShapeDtypeStruct(u6.shape, F32),
        scratch_shapes=[pltpu.VMEM((2 * S5_LANES // LANES, rows, LANES), F32),
                        pltpu.VMEM((SUBLANES, 2 * S5_LANES), F32)],
        compiler_params=_cparams(("arbitrary", "arbitrary")),
        name="s5",
    )(u6, wb, ab, wc, dd)


def _glu_kernel(y_ref, w_ref, o_ref, z_scr, *, bn):
    b = pl.program_id(1)
    rows = z_scr.shape[1]

    @pl.when(b == 0)
    def _():
        y = jnp.concatenate([y_ref.at[k][pl.ds(s, rows, stride=2), :] for k in range(S5_PAIRS) for s in range(2)],
                            axis=1)
        g = jnp.dot(y.astype(BF16), w_ref[...], preferred_element_type=F32)
        z = y * jax.nn.sigmoid(g)
        for c in range(S5_NBLK):
            z_scr[c] = z[:, c * LANES:(c + 1) * LANES]

    o_ref[0] = jnp.concatenate([z_scr.at[c][pl.ds(b, rows // bn, stride=bn), :] for c in range(S5_NBLK)],
                               axis=1).astype(o_ref.dtype)


def _glu(y6, w_glu, bn, l):
    tt = 256
    rows = tt * bn
    return pl.pallas_call(
        functools.partial(_glu_kernel, bn=bn),
        grid=(l // tt, bn),
        in_specs=[pl.BlockSpec((S5_PAIRS, rows * 2, LANES), lambda i, b: (0, i, 0)),
                  pl.BlockSpec((S5_CH, S5_CH), lambda i, b: (0, 0))],
        out_specs=pl.BlockSpec((1, tt, S5_CH), lambda i, b: (b, i, 0)),
        out_shape=jax.ShapeDtypeStruct((bn, l, S5_CH), BF16),
        scratch_shapes=[pltpu.VMEM((S5_NBLK, rows, LANES), F32)],
        compiler_params=_cparams(("arbitrary", "arbitrary")),
        name="glu",
    )(y6, w_glu)


def _merge_kernel(att_ref, z_ref, gm_ref, gs_ref, x_ref, mod_ref, gpost_ref, gpre_ref,
                  wbm_ref, wbs_ref, wo_ref, wrh_ref, wrl_ref, x1_ref, h2b_ref, h2s_ref, lg_ref):
    tm = x_ref.shape[1]
    th = tm // MERGE_SUBTILES
    for r0 in range(0, tm, th):
        rows = slice(r0, r0 + th)
        ym = jnp.dot(att_ref[rows, :], wbm_ref[...], preferred_element_type=F32)
        ys = jnp.dot(z_ref[rows, :], wbs_ref[...], preferred_element_type=F32)
        mixed_in = (gm_ref[rows, :].astype(F32) * ym + gs_ref[rows, :].astype(F32) * ys).astype(BF16)
        mixed = jnp.dot(mixed_in, wo_ref[...], preferred_element_type=F32)
        x1 = x_ref[0, rows, :] + mod_ref[0, 2:3, :] * _rms(mixed, gpost_ref[...])
        x1_ref[0, rows, :] = x1
        h2 = _rms(x1, gpre_ref[...]) * (1.0 + mod_ref[0, 4:5, :]) + mod_ref[0, 3:4, :]
        h2_hi = h2.astype(BF16)
        h2b_ref[0, rows, :] = h2_hi
        for s in range(SLAB):
            h2s_ref[pl.ds(r0 * SLAB + s, th, stride=SLAB), :] = h2[:, s * LANES:(s + 1) * LANES]
        h2_lo = (h2 - h2_hi.astype(F32)).astype(BF16)
        lg_ref[:, rows] = (lax.dot_general(wrh_ref[...], h2_hi, NT_DIMS, preferred_element_type=F32)
                           + lax.dot_general(wrl_ref[...], h2_hi, NT_DIMS, preferred_element_type=F32)
                           + lax.dot_general(wrh_ref[...], h2_lo, NT_DIMS, preferred_element_type=F32))


def _merge(att, z_tm, gates, x, mod, g_post, g_pre_ffn, w_br_mla, w_br_s5, w_out, wr_hi, wr_lo):
    bn, l, d = x.shape
    tm = 256
    nl = l // tm
    row = lambda b, i: (b * nl + i, 0)
    const = lambda b, i: (0, 0)
    one = pl.Buffered(1)
    return pl.pallas_call(
        _merge_kernel,
        grid=(bn, nl),
        in_specs=[pl.BlockSpec((tm, MLA_HEADS * V_HEAD), row),
                  pl.BlockSpec((tm, S5_CH), row),
                  pl.BlockSpec((tm, d), lambda b, i: (b * nl + i, 0)),
                  pl.BlockSpec((tm, d), lambda b, i: (b * nl + i, 1)),
                  pl.BlockSpec((1, tm, d), lambda b, i: (b, i, 0)),
                  pl.BlockSpec((1, 6, d), lambda b, i: (b, 0, 0)),
                  pl.BlockSpec((1, d), const),
                  pl.BlockSpec((1, d), const),
                  pl.BlockSpec((MLA_HEADS * V_HEAD, d), const, pipeline_mode=one),
                  pl.BlockSpec((S5_CH, d), const, pipeline_mode=one),
                  pl.BlockSpec((d, d), const, pipeline_mode=one),
                  pl.BlockSpec((N_EXPERTS, d), const, pipeline_mode=one),
                  pl.BlockSpec((N_EXPERTS, d), const, pipeline_mode=one)],
        out_specs=[pl.BlockSpec((1, tm, d), lambda b, i: (b, i, 0)),
                   pl.BlockSpec((1, tm, d), lambda b, i: (b, i, 0)),
                   pl.BlockSpec((tm * SLAB, LANES), row),
                   pl.BlockSpec((N_EXPERTS, tm), lambda b, i: (0, b * nl + i))],
        out_shape=[jax.ShapeDtypeStruct((bn, l, d), F32),
                   jax.ShapeDtypeStruct((bn, l, d), BF16),
                   jax.ShapeDtypeStruct((bn * l * SLAB, LANES), F32),
                   jax.ShapeDtypeStruct((N_EXPERTS, bn * l), F32)],
        compiler_params=_cparams(("arbitrary", "arbitrary")),
        name="merge",
    )(att, z_tm, gates, gates, x, mod, g_post.reshape(1, d), g_pre_ffn.reshape(1, d),
      w_br_mla, w_br_s5, w_out, wr_hi, wr_lo)


def _route_kernel(lg_ref, b_ref, ti_ref, tw_ref, cnt_ref):
    ng = N_EXPERT_GROUPS
    gsz = N_EXPERTS // ng
    lg = lg_ref[...]
    t = lg.shape[-1]
    sc = jax.nn.sigmoid(lg)
    sel = sc + b_ref[...]
    ninf = -jnp.inf
    i_in = lax.broadcasted_iota(jnp.int32, (ng, gsz, t), 1).astype(F32)
    m1 = jnp.max(sel, axis=1, keepdims=True)
    idx1 = jnp.min(jnp.where(sel == m1, i_in, float(gsz)), axis=1, keepdims=True)
    m2 = jnp.max(jnp.where(i_in == idx1, ninf, sel), axis=1, keepdims=True)
    gs = m1 + m2
    g_i = lax.broadcasted_iota(jnp.int32, (ng, 1, t), 0).astype(F32)
    picked = jnp.zeros((ng, 1, t), F32)
    cur = gs
    for _ in range(TOPK_GROUPS):
        m = jnp.max(cur, axis=0, keepdims=True)
        idx = jnp.min(jnp.where(cur == m, g_i, float(ng)), axis=0, keepdims=True)
        hit = g_i == idx
        picked = jnp.where(hit, 1.0, picked)
        cur = jnp.where(hit, ninf, cur)
    cand = jnp.where(picked > 0.5, sel, ninf)
    e_i = lax.broadcasted_iota(jnp.int32, (ng, gsz, t), 0).astype(F32) * float(gsz) + i_in
    ws = []
    hits = jnp.zeros((ng, gsz, t), F32)
    for r in range(TOP_K):
        m = jnp.max(jnp.max(cand, axis=1, keepdims=True), axis=0, keepdims=True)
        idx = jnp.min(jnp.min(jnp.where(cand == m, e_i, float(N_EXPERTS)), axis=1, keepdims=True),
                      axis=0, keepdims=True)
        hit = e_i == idx
        w = jnp.sum(jnp.sum(jnp.where(hit, sc, 0.0), axis=1, keepdims=True), axis=0, keepdims=True)
        ti_ref[r:r + 1, :] = idx[0].astype(jnp.int32)
        ws.append(w[0])
        hits = jnp.where(hit, 1.0, hits)
        cand = jnp.where(hit, ninf, cand)

    @pl.when(pl.program_id(0) == 0)
    def _():
        cnt_ref[...] = jnp.zeros(cnt_ref.shape, F32)

    cnt_ref[...] += jnp.sum(hits, axis=2, keepdims=True)
    tot = ws[0]
    for w in ws[1:]:
        tot = tot + w
    for r in range(TOP_K):
        tw_ref[r:r + 1, :] = ws[r] / tot * ROUTED_SCALE
    for r in range(TOP_K, SUBLANES):
        ti_ref[r:r + 1, :] = jnp.zeros((1, t), jnp.int32)
        tw_ref[r:r + 1, :] = jnp.zeros((1, t), F32)


def _route(logits_t, router_bias):
    n = logits_t.shape[1]
    ng = N_EXPERT_GROUPS
    gsz = N_EXPERTS // ng
    tn = 1024
    return pl.pallas_call(
        _route_kernel,
        grid=(n // tn,),
        in_specs=[pl.BlockSpec((ng, gsz, tn), lambda i: (0, 0, i)),
                  pl.BlockSpec((ng, gsz, 1), lambda i: (0, 0, 0))],
        out_specs=[pl.BlockSpec((SUBLANES, tn), lambda i: (0, i)),
                   pl.BlockSpec((SUBLANES, tn), lambda i: (0, i)),
                   pl.BlockSpec((ng, gsz, 1), lambda i: (0, 0, 0))],
        out_shape=[jax.ShapeDtypeStruct((SUBLANES, n), jnp.int32),
                   jax.ShapeDtypeStruct((SUBLANES, n), F32),
                   jax.ShapeDtypeStruct((ng, gsz, 1), F32)],
        compiler_params=_cparams(("arbitrary",)),
        name="route",
    )(logits_t.reshape(ng, gsz, n), router_bias.reshape(ng, gsz, 1))


def _moe_kernel(blk_e_ref, nused_ref, nvalid_ref, nxt_e_ref, wslot_ref, idx_ref, h_hbm,
                wg_hbm, wu_hbm, wd_hbm, y_hbm, xbuf, ybuf, wg_buf, wu_buf, wd_buf, wg_s, wu_s, wd_s,
                gsem, wsem, wtsem):
    j = pl.program_id(0)
    n_used = nused_ref[0]
    slot = lax.rem(j, GATHER_SLOTS)
    nv = nvalid_ref[j]
    nv_prev = nvalid_ref[jnp.maximum(j - 1, 0)]

    def slab(ref, row0, align=SUBLANES):
        return ref.at[pl.ds(pl.multiple_of(row0, align), SLAB), :]

    def start_gather(ahead, s):
        def body(r, _):
            pltpu.make_async_copy(slab(h_hbm, idx_ref[0, ahead, r]), slab(xbuf.at[s], r * MOE_PITCH, PITCH_ALIGN),
                                  gsem.at[s]).start()
            return 0
        lax.fori_loop(0, MOE_TB, body, 0, unroll=8)

    def wait_gather(s):
        rows = MOE_TB * SLAB
        pltpu.make_async_copy(h_hbm.at[pl.ds(0, rows), :], xbuf.at[s, pl.ds(0, rows), :], gsem.at[s]).wait()

    def weight_copies(expert, s):
        return (pltpu.make_async_copy(wg_hbm.at[expert], wg_buf.at[s], wtsem.at[s]),
                pltpu.make_async_copy(wu_hbm.at[expert], wu_buf.at[s], wtsem.at[s]),
                pltpu.make_async_copy(wd_hbm.at[expert], wd_buf.at[s], wtsem.at[s]))

    def wait_writes(count):
        p = MOE_TB
        while p >= 1:
            @pl.when((count & p) != 0)
            def _(p=p):
                rows = p * SLAB
                pltpu.make_async_copy(ybuf.at[pl.ds(0, rows), :], y_hbm.at[pl.ds(0, rows), :], wsem.at[0]).wait()
            p //= 2

    @pl.when(j < n_used)
    def _():
        @pl.when(j == 0)
        def _():
            start_gather(0, 0)

            @pl.when(n_used > 1)
            def _():
                start_gather(1, 1)

        e = blk_e_ref[j]
        ws = wslot_ref[j]

        @pl.when(j == 0)
        def _():
            for cp in weight_copies(e, ws):
                cp.start(priority=WEIGHT_DMA_PRIORITY)

        first = jnp.logical_or(j == 0, e != blk_e_ref[jnp.maximum(j - 1, 0)])

        @pl.when(first)
        def _():
            for cp in weight_copies(e, ws):
                cp.wait()
            wg_s[...] = wg_buf[ws].astype(BF16)
            wu_s[...] = wu_buf[ws].astype(BF16)
            wd_s[...] = wd_buf[ws].astype(BF16)

        wait_gather(slot)

        @pl.when(j + 2 < n_used)
        def _():
            start_gather(2, lax.rem(j + 2, GATHER_SLOTS))

        @pl.when(jnp.logical_and(first, nxt_e_ref[j] >= 0))
        def _():
            for cp in weight_copies(nxt_e_ref[j], 1 - ws):
                cp.start(priority=WEIGHT_DMA_PRIORITY)

        xs = xbuf.at[slot]
        x = jnp.concatenate([xs[pl.ds(s, MOE_TB, stride=MOE_PITCH), :] for s in range(SLAB)], axis=1).astype(BF16)
        g = jnp.dot(x, wg_s[...], preferred_element_type=F32)
        u = jnp.dot(x, wu_s[...], preferred_element_type=F32)
        hm = (g * jax.nn.sigmoid(g) * u).astype(BF16)
        y = jnp.dot(hm, wd_s[...], preferred_element_type=F32)

        @pl.when(j > 0)
        def _():
            wait_writes(nv_prev)

        for s in range(SLAB):
            ybuf[pl.ds(s, MOE_TB, stride=MOE_PITCH), :] = y[:, s * LANES:(s + 1) * LANES]

        def start_write(r):
            pltpu.make_async_copy(slab(ybuf, r * MOE_PITCH, PITCH_ALIGN), slab(y_hbm, idx_ref[0, 3, r]),
                                  wsem.at[0]).start()

        def wgroup(g8, _):
            for q in range(SUBLANES):
                start_write(g8 * SUBLANES + q)
            return 0

        def wtail(r, _):
            start_write(r)
            return 0

        n_groups = lax.shift_right_logical(nv, 3)
        lax.fori_loop(0, n_groups, wgroup, 0)
        lax.fori_loop(n_groups * SUBLANES, nv, wtail, 0)

        @pl.when(j == n_used - 1)
        def _():
            wait_writes(nv)


def _moe(h2s, topi, counts, w_gate, w_up, w_down):
    n = h2s.shape[0] // SLAB
    d = D_MODEL
    nk = n * TOP_K
    tb = MOE_TB
    n_blocks = nk // tb + N_EXPERTS
    flat_e = topi[:TOP_K].reshape(nk)
    _, order = lax.sort_key_val(flat_e, jnp.arange(nk, dtype=jnp.int32))
    start = jnp.cumsum(counts) - counts
    padded = (counts + tb - 1) // tb * tb
    pad_end = jnp.cumsum(padded)
    pad_start = pad_end - padded
    n_used = (pad_end[-1] // tb).astype(jnp.int32).reshape(1)
    blk_p0 = jnp.arange(n_blocks, dtype=jnp.int32) * tb
    blk_e = jnp.minimum(jnp.sum((pad_end[None, :] <= blk_p0[:, None]).astype(jnp.int32), axis=1), N_EXPERTS - 1)
    blk_off = blk_p0 - pad_start[blk_e]
    nvalid = jnp.clip(counts[blk_e] - blk_off, 0, tb).astype(jnp.int32)
    rank = (start[blk_e] + blk_off)[:, None] + jnp.arange(tb, dtype=jnp.int32)[None, :]
    a3 = order[jnp.clip(rank, 0, nk - 1)].reshape(n_blocks, 1, tb)
    arow = a3 * SLAB
    trow = (a3 % n) * SLAB
    ahead = jnp.concatenate([trow, trow[-1:], trow[-1:]], axis=0)
    idx = jnp.concatenate([trow, ahead[1:-1], ahead[2:], arow], axis=1)
    used = counts > 0
    e_ids = jnp.arange(N_EXPERTS, dtype=jnp.int32)
    later = jnp.where(used, e_ids, N_EXPERTS)
    nxt = jnp.concatenate([lax.cummin(later[::-1])[::-1][1:], jnp.full((1,), N_EXPERTS, jnp.int32)])
    nxt_e = jnp.where(nxt < N_EXPERTS, nxt, -1)[blk_e].astype(jnp.int32)
    wslot = ((jnp.cumsum(used.astype(jnp.int32)) - 1) % 2)[blk_e].astype(jnp.int32)

    any_spec = pl.BlockSpec(memory_space=pl.ANY)
    grid_spec = pltpu.PrefetchScalarGridSpec(
        num_scalar_prefetch=5,
        grid=(n_blocks,),
        in_specs=[pl.BlockSpec((1, 4, tb), lambda j, *_: (j, 0, 0), memory_space=pltpu.SMEM),
                  any_spec, any_spec, any_spec, any_spec],
        out_specs=any_spec,
        scratch_shapes=[pltpu.VMEM((GATHER_SLOTS, tb * MOE_PITCH, LANES), F32),
                        pltpu.VMEM((tb * MOE_PITCH, LANES), F32),
                        pltpu.VMEM((2, d, D_EXPERT), F32),
                        pltpu.VMEM((2, d, D_EXPERT), F32),
                        pltpu.VMEM((2, D_EXPERT, d), F32),
                        pltpu.VMEM((d, D_EXPERT), BF16),
                        pltpu.VMEM((d, D_EXPERT), BF16),
                        pltpu.VMEM((D_EXPERT, d), BF16),
                        pltpu.SemaphoreType.DMA((GATHER_SLOTS,)),
                        pltpu.SemaphoreType.DMA((1,)),
                        pltpu.SemaphoreType.DMA((2,))],
    )
    return pl.pallas_call(
        _moe_kernel,
        grid_spec=grid_spec,
        out_shape=jax.ShapeDtypeStruct((TOP_K * n * SLAB, LANES), F32),
        compiler_params=_cparams(("arbitrary",)),
        name="moe",
    )(blk_e, n_used, nvalid, nxt_e, wslot, idx, h2s, w_gate, w_up, w_down)


def _final_kernel(h_ref, y0, y1, y2, y3, y4, y5, tw_ref, x1_ref, mod_ref, g_ref, wg_ref, wu_ref, wd_ref, o_ref):
    h = h_ref[...]
    tm = h.shape[0]
    g = jnp.dot(h, wg_ref[...], preferred_element_type=F32)
    u = jnp.dot(h, wu_ref[...], preferred_element_type=F32)
    hm = (g * jax.nn.sigmoid(g) * u).astype(BF16)
    ffn = jnp.dot(hm, wd_ref[...], preferred_element_type=F32)
    tw = tw_ref[...]
    for k, y in enumerate((y0, y1, y2, y3, y4, y5)):
        yk = jnp.concatenate([y[pl.ds(s, tm, stride=SLAB), :] for s in range(SLAB)], axis=1)
        ffn = ffn + yk * tw[:, k:k + 1]
    o_ref[0] = x1_ref[0] + mod_ref[0, 5:6, :] * _rms(ffn, g_ref[...])


def _final(h2, y_flat, topw_t, x1, mod, g_post_ffn, w_sg, w_su, w_sd):
    bn, l, d = x1.shape
    n = bn * l
    tm = 256
    nl = l // tm
    const = lambda b, i: (0, 0)
    one = pl.Buffered(1)
    y_specs = [pl.BlockSpec((tm * SLAB, LANES), functools.partial(lambda b, i, k: (k * (n // tm) + b * nl + i, 0), k=k))
               for k in range(TOP_K)]
    return pl.pallas_call(
        _final_kernel,
        grid=(bn, nl),
        in_specs=[pl.BlockSpec((tm, d), lambda b, i: (b * nl + i, 0))] + y_specs + [
            pl.BlockSpec((tm, SUBLANES), lambda b, i: (b * nl + i, 0)),
            pl.BlockSpec((1, tm, d), lambda b, i: (b, i, 0)),
            pl.BlockSpec((1, 6, d), lambda b, i: (b, 0, 0)),
            pl.BlockSpec((1, d), const),
            pl.BlockSpec((d, D_EXPERT), const, pipeline_mode=one),
            pl.BlockSpec((d, D_EXPERT), const, pipeline_mode=one),
            pl.BlockSpec((D_EXPERT, d), const, pipeline_mode=one)],
        out_specs=pl.BlockSpec((1, tm, d), lambda b, i: (b, i, 0)),
        out_shape=jax.ShapeDtypeStruct((bn, l, d), F32),
        compiler_params=_cparams(("arbitrary", "arbitrary")),
        name="final",
    )(h2, *([y_flat] * TOP_K), topw_t, x1, mod, g_post_ffn.reshape(1, d), w_sg, w_su, w_sd)


def _layer(li, x, c, positions, w_ada, b_ada, g_pre_mix, g_post_mix, g_pre_ffn, g_post_ffn, w_in, g_q, g_kv,
           w_uq, w_uk, w_uv, a_re, a_im, log_dt, b_re, b_im, c_re, c_im, d_skip, w_glu, w_br_mla, w_br_s5,
           w_out, w_router, router_bias, w_exp_gate, w_exp_up, w_exp_down, w_sh_gate, w_sh_up, w_sh_down):
    bn, l, d = x.shape
    n = bn * l
    mod = _adaln(c, w_ada, b_ada).reshape(bn, 6, d)

    o_kpe = Q_LORA + KV_LORA
    o_u = o_kpe + QK_ROPE
    o_g = o_u + S5_CH
    w_in_t = jnp.swapaxes(w_in, 1, 2)
    w_q = jnp.pad(w_uq.reshape(Q_LORA, MLA_HEADS, QK_NOPE + QK_ROPE),
                  ((0, 0), (0, 0), (0, QK_PAD - QK_NOPE - QK_ROPE))).reshape(Q_LORA, MLA_HEADS * QK_PAD).astype(BF16)
    wr_t = w_router.T
    wr_hi = wr_t.astype(BF16)
    wr_lo = (wr_t - wr_hi.astype(F32)).astype(BF16)

    rope_c, rope_s1, rope_s2 = _rope_tables(positions)
    chunk_id = positions // CHUNK

    h = _prenorm(x, mod, g_pre_mix, 0, 1).reshape(n, d)
    latk = _mmt(h, w_in_t, li, 0, o_kpe + LANES, F32, 1024, 3 * LANES, name="mm_lat")
    gates = _mmt(h, w_in_t, li, o_g, 2 * d, BF16, 1024, 1024, act="sigmoid", name="mm_gates")
    u6 = _mm_u(h, w_in_t, li, o_u, bn, l)

    q, k, v = _qkvproj(latk, g_q, g_kv, w_q, w_uk.astype(BF16), w_uv.astype(BF16), rope_c, rope_s1, rope_s2)
    att = _attention(q, k, v, chunk_id, bn, l)

    wb, ab, wc, dd = _s5_params(a_re, a_im, log_dt, b_re, b_im, c_re, c_im, d_skip)
    y6 = _s5(u6, wb, ab, wc, dd, l)
    z = _glu(y6, w_glu.astype(BF16), bn, l).reshape(n, S5_CH)

    x1, h2b, h2s, logits_t = _merge(att, z, gates, x, mod, g_post_mix, g_pre_ffn, w_br_mla.astype(BF16),
                                    w_br_s5.astype(BF16), w_out.astype(BF16), wr_hi, wr_lo)

    topi, topw, cnt = _route(logits_t, router_bias)
    y_slabs = _moe(h2s, topi, cnt.reshape(N_EXPERTS).astype(jnp.int32), w_exp_gate, w_exp_up, w_exp_down)
    return _final(h2b.reshape(n, d), y_slabs, topw.T, x1, mod, g_post_ffn, w_sh_gate.astype(BF16),
                  w_sh_up.astype(BF16), w_sh_down.astype(BF16))


def kernel(x, c, positions, w_ada, b_ada, g_pre_mix, g_post_mix, g_pre_ffn, g_post_ffn, w_in, g_q, g_kv, w_uq, w_uk, w_uv, a_re, a_im, log_dt, b_re, b_im, c_re, c_im, d_skip, w_glu, w_br_mla, w_br_s5, w_out, w_router, router_bias, w_exp_gate, w_exp_up, w_exp_down, w_sh_gate, w_sh_up, w_sh_down):
    depth = w_ada.shape[0]
    for li in range(depth):
        x = _layer(li, x, c, positions, w_ada[li], b_ada[li], g_pre_mix[li], g_post_mix[li], g_pre_ffn[li],
                   g_post_ffn[li], w_in, g_q[li], g_kv[li], w_uq[li], w_uk[li], w_uv[li], a_re[li], a_im[li],
                   log_dt[li], b_re[li], b_im[li], c_re[li], c_im[li], d_skip[li], w_glu[li], w_br_mla[li],
                   w_br_s5[li], w_out[li], w_router[li], router_bias[li], w_exp_gate[li], w_exp_up[li],
                   w_exp_down[li], w_sh_gate[li], w_sh_up[li], w_sh_down[li])
    return x
```

```python
import functools

import jax
import jax.numpy as jnp
from jax import lax
from jax.experimental import pallas as pl
from jax.experimental.pallas import tpu as pltpu

F32 = jnp.float32
BF16 = jnp.bfloat16

D_MODEL = 2048
CHUNK = 64
EPS = 1e-6
MLA_HEADS = 8
QK_NOPE = 128
QK_ROPE = 64
V_HEAD = 128
Q_LORA = 512
KV_LORA = 512
ROPE_THETA = 10000.0
S5_CH = 1024
S5_GROUP = 16
S5_GROUPS = S5_CH // S5_GROUP
S5_STATE = 64
N_EXPERTS = 64
TOP_K = 6
N_EXPERT_GROUPS = 8
TOPK_GROUPS = 4
D_EXPERT = 512
ROUTED_SCALE = 2.5

LANES = 128
SUBLANES = 8
QK_PAD = 2 * LANES
VMEM_LIMIT = 56 * 1024 * 1024
NEG = -1e30
LOG2E = 1.4426950408889634

S5_GB = LANES // S5_GROUP
S5_NBLK = S5_CH // LANES
S5_PAIRS = S5_NBLK // 2
S5_LANES = S5_GB * S5_STATE
S5_TC = 256
S5_RB = 512

ATT_TQ = 256
ATT_TK = 512
MOE_TB = 256
SLAB = D_MODEL // LANES
MERGE_SUBTILES = 2
GATHER_SLOTS = 3
WEIGHT_DMA_PRIORITY = 1
MOE_PITCH = 20
PITCH_ALIGN = 4


def _cparams(sem):
    return pltpu.CompilerParams(dimension_semantics=sem, vmem_limit_bytes=VMEM_LIMIT)


def _rms(x, g):
    return x * lax.rsqrt(jnp.mean(x * x, axis=-1, keepdims=True) + EPS) * g


def _adaln_kernel(c_ref, w_ref, b_ref, o_ref):
    c = c_ref[...]
    a = (c * jax.nn.sigmoid(c)).astype(BF16)
    o_ref[...] = jnp.dot(a, w_ref[...].astype(BF16), preferred_element_type=F32) + b_ref[...]


def _adaln(c, w, b):
    bn, d = c.shape
    n = w.shape[1]
    tn = 1024
    return pl.pallas_call(
        _adaln_kernel,
        grid=(n // tn,),
        in_specs=[pl.BlockSpec((bn, d), lambda j: (0, 0)),
                  pl.BlockSpec((d, tn), lambda j: (0, j)),
                  pl.BlockSpec((1, tn), lambda j: (0, j))],
        out_specs=pl.BlockSpec((bn, tn), lambda j: (0, j)),
        out_shape=jax.ShapeDtypeStruct((bn, n), F32),
        compiler_params=_cparams(("arbitrary",)),
        name="adaln",
    )(c, w, b.reshape(1, n))


def _prenorm_kernel(x_ref, mod_ref, g_ref, o_ref, *, sh_row, sc_row):
    y = _rms(x_ref[0], g_ref[...])
    o_ref[0] = (y * (1.0 + mod_ref[0, sc_row:sc_row + 1, :]) + mod_ref[0, sh_row:sh_row + 1, :]).astype(o_ref.dtype)


def _prenorm(x, mod, g, sh_row, sc_row):
    bn, l, d = x.shape
    tl = 512
    return pl.pallas_call(
        functools.partial(_prenorm_kernel, sh_row=sh_row, sc_row=sc_row),
        grid=(bn, l // tl),
        in_specs=[pl.BlockSpec((1, tl, d), lambda b, i: (b, i, 0)),
                  pl.BlockSpec((1, 6, d), lambda b, i: (b, 0, 0)),
                  pl.BlockSpec((1, d), lambda b, i: (0, 0))],
        out_specs=pl.BlockSpec((1, tl, d), lambda b, i: (b, i, 0)),
        out_shape=jax.ShapeDtypeStruct((bn, l, d), BF16),
        compiler_params=_cparams(("arbitrary", "arbitrary")),
        name="prenorm",
    )(x, mod, g.reshape(1, d))


NT_DIMS = (((1,), (1,)), ((), ()))


def _wt_tile(wa_ref, wb_ref, shift, tn):
    if shift == 0:
        return wa_ref[...].astype(BF16)
    return jnp.concatenate([wa_ref[...], wb_ref[...]], axis=0)[shift:shift + tn].astype(BF16)


def _wt_specs(layer, row0, tn, k, jmap):
    shift = row0 % LANES
    c0 = row0 - shift
    assert c0 % tn == 0 and shift % SUBLANES == 0
    specs = [pl.BlockSpec((None, tn, k), lambda *g: (layer, jmap(*g) + c0 // tn, 0))]
    if shift:
        specs.append(pl.BlockSpec((None, LANES, k), lambda *g: (layer, (c0 + (jmap(*g) + 1) * tn) // LANES, 0)))
    return specs, shift


def _mmt_kernel(a_ref, wa_ref, *rest, act, shift):
    wb_ref = rest[0] if shift else None
    o_ref, w_scr = rest[-2:]

    @pl.when(pl.program_id(1) == 0)
    def _():
        w_scr[...] = _wt_tile(wa_ref, wb_ref, shift, w_scr.shape[0])

    acc = lax.dot_general(a_ref[...], w_scr[...], NT_DIMS, preferred_element_type=F32)
    if act == "sigmoid":
        acc = jax.nn.sigmoid(acc)
    o_ref[...] = acc.astype(o_ref.dtype)


def _mmt(a, w_t, layer, row0, n, out_dtype, tm, tn, act=None, name="mmt"):
    m, k = a.shape
    w_specs, shift = _wt_specs(layer, row0, tn, k, lambda j, i: j)
    return pl.pallas_call(
        functools.partial(_mmt_kernel, act=act, shift=shift),
        grid=(n // tn, m // tm),
        in_specs=[pl.BlockSpec((tm, k), lambda j, i: (i, 0))] + w_specs,
        out_specs=pl.BlockSpec((tm, tn), lambda j, i: (i, j)),
        out_shape=jax.ShapeDtypeStruct((m, n), out_dtype),
        scratch_shapes=[pltpu.VMEM((tn, k), BF16)],
        compiler_params=_cparams(("arbitrary", "arbitrary")),
        name=name,
    )(a, *([w_t] * len(w_specs)))


def _mm_u_kernel(a_ref, wa_ref, *rest, shift):
    wb_ref = rest[0] if shift else None
    o_ref, w_scr = rest[-2:]
    b = pl.program_id(1)

    @pl.when(jnp.logical_and(pl.program_id(0) == 0, b == 0))
    def _():
        w_scr[...] = _wt_tile(wa_ref, wb_ref, shift, w_scr.shape[0])

    res = lax.dot_general(a_ref[...], w_scr[...], NT_DIMS, preferred_element_type=F32)
    tm = res.shape[0]
    for c in range(S5_NBLK):
        o_ref.at[c // 2][pl.ds(b * 2 + c % 2, tm, stride=SUBLANES), :] = res[:, c * LANES:(c + 1) * LANES]


def _mm_u(h, w_t, layer, row0, bn, l):
    m, k = h.shape
    tm = 512
    nl = l // tm
    assert bn * 2 == SUBLANES
    w_specs, shift = _wt_specs(layer, row0, S5_CH, k, lambda i, b: 0)
    return pl.pallas_call(
        functools.partial(_mm_u_kernel, shift=shift),
        grid=(nl, bn),
        in_specs=[pl.BlockSpec((tm, k), lambda i, b: (b * nl + i, 0))] + w_specs,
        out_specs=pl.BlockSpec((S5_PAIRS, tm * SUBLANES, LANES), lambda i, b: (0, i, 0)),
        out_shape=jax.ShapeDtypeStruct((S5_PAIRS, l * SUBLANES, LANES), F32),
        scratch_shapes=[pltpu.VMEM((S5_CH, k), BF16)],
        compiler_params=_cparams(("arbitrary", "arbitrary")),
        name="mm_u",
    )(h, *([w_t] * len(w_specs)))


def _rope_tab_kernel(pos_ref, k_ref, c_ref, s1_ref, s2_ref):
    ang = pos_ref[...].astype(F32) * k_ref[0:1, :]
    s = jnp.sin(ang)
    c_ref[...] = jnp.cos(ang) * k_ref[1:2, :]
    s1_ref[...] = s * k_ref[2:3, :]
    s2_ref[...] = s * k_ref[3:4, :]


def _rope_tables(positions):
    n = positions.size
    half = QK_ROPE // 2
    inv_freq = ROPE_THETA ** (-jnp.arange(half, dtype=F32) / half)
    zh, oh = jnp.zeros((half,), F32), jnp.ones((half,), F32)
    z2 = jnp.zeros((LANES - QK_ROPE,), F32)
    rows = [jnp.concatenate([inv_freq, inv_freq, z2]), jnp.concatenate([oh, oh, z2]),
            jnp.concatenate([-oh, zh, z2]), jnp.concatenate([zh, oh, z2])]
    consts = jnp.stack(rows + [jnp.zeros((LANES,), F32)] * (SUBLANES - len(rows)))
    tm = 1024
    tab = jax.ShapeDtypeStruct((n, LANES), F32)
    return pl.pallas_call(
        _rope_tab_kernel,
        grid=(n // tm,),
        in_specs=[pl.BlockSpec((tm, 1), lambda i: (i, 0)),
                  pl.BlockSpec((SUBLANES, LANES), lambda i: (0, 0))],
        out_specs=[pl.BlockSpec((tm, LANES), lambda i: (i, 0))] * 3,
        out_shape=[tab, tab, tab],
        compiler_params=_cparams(("arbitrary",)),
        name="rope_tables",
    )(positions.reshape(n, 1), consts)


def _rope_tile(t, c_ref, s1_ref, s2_ref):
    return (t * c_ref[...] + pltpu.roll(t, LANES - QK_ROPE // 2, 1) * s1_ref[...]
            + pltpu.roll(t, QK_ROPE // 2, 1) * s2_ref[...])


def _qkvproj_kernel(lat_ref, gq_ref, gkv_ref, wq_ref, wk_ref, wv_ref, kpe_ref, c_ref, s1_ref, s2_ref,
                    q_ref, k_ref, v_ref, *, scale):
    lat = lat_ref[...]
    qn = _rms(lat[:, :Q_LORA], gq_ref[...]).astype(BF16)
    cn = _rms(lat[:, Q_LORA:], gkv_ref[...]).astype(BF16)
    q = jnp.dot(qn, wq_ref[...], preferred_element_type=F32)
    kn = jnp.dot(cn, wk_ref[...], preferred_element_type=F32)
    v_ref[...] = jnp.dot(cn, wv_ref[...], preferred_element_type=F32).astype(v_ref.dtype)
    kt = _rope_tile(kpe_ref[...], c_ref, s1_ref, s2_ref).astype(k_ref.dtype)
    for h in range(MLA_HEADS):
        o = h * QK_PAD
        q_ref[:, o:o + LANES] = (q[:, o:o + LANES] * scale).astype(q_ref.dtype)
        qt = _rope_tile(q[:, o + LANES:o + QK_PAD], c_ref, s1_ref, s2_ref)
        q_ref[:, o + LANES:o + QK_PAD] = (qt * scale).astype(q_ref.dtype)
        k_ref[:, o:o + LANES] = kn[:, h * QK_NOPE:(h + 1) * QK_NOPE].astype(k_ref.dtype)
        k_ref[:, o + LANES:o + QK_PAD] = kt


def _qkvproj(lat, g_q, g_kv, w_q, w_k, w_v, kpe, rope_c, rope_s1, rope_s2):
    n = lat.shape[0]
    tm = 512
    row = lambda i: (i, 0)
    const = lambda i: (0, 0)
    tab = pl.BlockSpec((tm, LANES), row)
    return pl.pallas_call(
        functools.partial(_qkvproj_kernel, scale=(QK_NOPE + QK_ROPE) ** -0.5 * LOG2E),
        grid=(n // tm,),
        in_specs=[pl.BlockSpec((tm, Q_LORA + KV_LORA), row),
                  pl.BlockSpec((1, Q_LORA), const),
                  pl.BlockSpec((1, KV_LORA), const),
                  pl.BlockSpec((Q_LORA, MLA_HEADS * QK_PAD), const),
                  pl.BlockSpec((KV_LORA, MLA_HEADS * QK_NOPE), const),
                  pl.BlockSpec((KV_LORA, MLA_HEADS * V_HEAD), const),
                  tab, tab, tab, tab],
        out_specs=[pl.BlockSpec((tm, MLA_HEADS * QK_PAD), row),
                   pl.BlockSpec((tm, MLA_HEADS * QK_PAD), row),
                   pl.BlockSpec((tm, MLA_HEADS * V_HEAD), row)],
        out_shape=[jax.ShapeDtypeStruct((n, MLA_HEADS * QK_PAD), BF16),
                   jax.ShapeDtypeStruct((n, MLA_HEADS * QK_PAD), BF16),
                   jax.ShapeDtypeStruct((n, MLA_HEADS * V_HEAD), BF16)],
        compiler_params=_cparams(("arbitrary",)),
        name="qkvproj",
    )(lat, g_q.reshape(1, Q_LORA), g_kv.reshape(1, KV_LORA), w_q, w_k, w_v, kpe, rope_c, rope_s1, rope_s2)


def _attn_kernel(lo_ref, hi_ref, q_ref, k_ref, v_ref, qc_ref, kc_ref, o_ref, m_scr, l_scr, acc_scr, *, nq):
    b = pl.program_id(0)
    i = pl.program_id(1)
    qc = qc_ref[...]
    m_scr[...] = jnp.full(m_scr.shape, NEG, F32)
    l_scr[...] = jnp.zeros(l_scr.shape, F32)
    acc_scr[...] = jnp.zeros(acc_scr.shape, F32)

    def make_body(masked):
        def body(j, _):
            off = pl.multiple_of(j * ATT_TK, ATT_TK)
            if masked:
                mask = kc_ref[j] <= qc
            for h in range(MLA_HEADS):
                q = q_ref[:, h * QK_PAD:(h + 1) * QK_PAD]
                k = k_ref[pl.ds(off, ATT_TK), h * QK_PAD:(h + 1) * QK_PAD]
                s = lax.dot_general(q, k, (((1,), (1,)), ((), ())), preferred_element_type=F32)
                if masked:
                    s = jnp.where(mask, s, NEG)
                m_old = m_scr[h]
                m_new = jnp.maximum(m_old, jnp.max(s, axis=-1, keepdims=True))
                p = jnp.exp2(s - jnp.concatenate([m_new] * (ATT_TK // LANES), axis=1))
                alpha = jnp.exp2(m_old - m_new)
                l_scr[h] = alpha * l_scr[h] + jnp.sum(p, axis=-1, keepdims=True)
                v = v_ref[pl.ds(off, ATT_TK), h * V_HEAD:(h + 1) * V_HEAD]
                acc_scr[h] = alpha * acc_scr[h] + jnp.dot(p.astype(BF16), v, preferred_element_type=F32)
                m_scr[h] = m_new
            return 0
        return body

    lo = lo_ref[b * nq + i]
    lax.fori_loop(0, lo, make_body(False), 0)
    lax.fori_loop(lo, hi_ref[b * nq + i], make_body(True), 0)
    for h in range(MLA_HEADS):
        o_ref[:, h * V_HEAD:(h + 1) * V_HEAD] = (acc_scr[h] / l_scr[h]).astype(o_ref.dtype)


def _attention(q, k, v, chunk_id, bn, l):
    nq = l // ATT_TQ
    nk = l // ATT_TK
    q_max = jnp.max(chunk_id.reshape(bn, nq, ATT_TQ), axis=-1)
    k_min = jnp.min(chunk_id.reshape(bn, nk, ATT_TK), axis=-1)
    needed = k_min[:, None, :] <= q_max[:, :, None]
    hi = jnp.max(jnp.where(needed, jnp.arange(1, nk + 1, dtype=jnp.int32), 0), axis=-1).reshape(bn * nq)
    q_min = jnp.min(chunk_id.reshape(bn, nq, ATT_TQ), axis=-1)
    k_max = jnp.max(chunk_id.reshape(bn, nk, ATT_TK), axis=-1)
    full = k_max[:, None, :] <= q_min[:, :, None]
    lo = jnp.min(jnp.where(full, nk, jnp.arange(nk, dtype=jnp.int32)), axis=-1).astype(jnp.int32).reshape(bn * nq)
    hi = jnp.maximum(hi, lo)
    qc = chunk_id.reshape(bn * l, 1)
    kc = chunk_id.reshape(bn * nk, 1, ATT_TK)
    grid_spec = pltpu.PrefetchScalarGridSpec(
        num_scalar_prefetch=2,
        grid=(bn, nq),
        in_specs=[pl.BlockSpec((ATT_TQ, MLA_HEADS * QK_PAD), lambda b, i, *_: (b * nq + i, 0)),
                  pl.BlockSpec((l, MLA_HEADS * QK_PAD), lambda b, i, *_: (b, 0)),
                  pl.BlockSpec((l, MLA_HEADS * V_HEAD), lambda b, i, *_: (b, 0)),
                  pl.BlockSpec((ATT_TQ, 1), lambda b, i, *_: (b * nq + i, 0)),
                  pl.BlockSpec((nk, 1, ATT_TK), lambda b, i, *_: (b, 0, 0))],
        out_specs=pl.BlockSpec((ATT_TQ, MLA_HEADS * V_HEAD), lambda b, i, *_: (b * nq + i, 0)),
        scratch_shapes=[pltpu.VMEM((MLA_HEADS, ATT_TQ, LANES), F32),
                        pltpu.VMEM((MLA_HEADS, ATT_TQ, LANES), F32),
                        pltpu.VMEM((MLA_HEADS, ATT_TQ, V_HEAD), F32)],
    )
    return pl.pallas_call(
        functools.partial(_attn_kernel, nq=nq),
        grid_spec=grid_spec,
        out_shape=jax.ShapeDtypeStruct((bn * l, MLA_HEADS * V_HEAD), BF16),
        compiler_params=_cparams(("arbitrary", "arbitrary")),
        name="attention",
    )(lo, hi, q, k, v, qc, kc)


def _s5_kernel(u_ref, wb_ref, a_ref, wc_ref, d_ref, o_ref, x_scr, st_scr):
    rows = S5_TC * SUBLANES
    nsub = rows // S5_RB

    @pl.when(pl.program_id(1) == 0)
    def _():
        st_scr[...] = jnp.zeros_like(st_scr)

    even = (lax.broadcasted_iota(jnp.int32, (S5_RB, 1), 0) & 1) == 0
    half = S5_RB // 2
    nre = S5_LANES // LANES
    uv = u_ref.at[0]

    def mm_in(r, _):
        off = pl.multiple_of(r * S5_RB, S5_RB)
        for s in range(2):
            us = uv[pl.ds(off + s, half, stride=2), :].astype(BF16)
            out = jnp.dot(us, wb_ref[0, :, s * 2 * S5_LANES:(s + 1) * 2 * S5_LANES], preferred_element_type=F32)
            for c in range(2 * nre):
                x_scr.at[c][pl.ds(off + s, half, stride=2), :] = out[:, c * LANES:(c + 1) * LANES]
        return 0

    lax.fori_loop(0, nsub, mm_in, 0)

    a_c = [a_ref[0, :, c * LANES:(c + 1) * LANES] for c in range(2 * nre)]

    def step(t, carry):
        off = pl.multiple_of(t * SUBLANES, SUBLANES)
        new = [None] * (2 * nre)
        for c in range(nre):
            xr, xi = carry[c], carry[nre + c]
            ar, ai = a_c[c], a_c[nre + c]
            nr = ar * xr - ai * xi + x_scr[c, pl.ds(off, SUBLANES), :]
            ni = ar * xi + ai * xr + x_scr[nre + c, pl.ds(off, SUBLANES), :]
            x_scr[c, pl.ds(off, SUBLANES), :] = nr
            x_scr[nre + c, pl.ds(off, SUBLANES), :] = ni
            new[c], new[nre + c] = nr, ni
        return tuple(new)

    init = tuple(st_scr[:, c * LANES:(c + 1) * LANES] for c in range(2 * nre))
    fin = lax.fori_loop(0, S5_TC, step, init, unroll=8)
    for c in range(2 * nre):
        st_scr[:, c * LANES:(c + 1) * LANES] = fin[c]

    d = jnp.concatenate([d_ref[0]] * (S5_RB // SUBLANES), axis=0)

    def mm_out(r, _):
        off = pl.multiple_of(r * S5_RB, S5_RB)
        x = jnp.concatenate([x_scr[c, pl.ds(off, S5_RB), :] for c in range(2 * nre)], axis=1).astype(BF16)
        out = jnp.dot(x, wc_ref[0], preferred_element_type=F32)
        y = jnp.where(even, out[:, :LANES], out[:, LANES:]) + d * u_ref[0, pl.ds(off, S5_RB), :]
        o_ref[0, pl.ds(off, S5_RB), :] = jax.nn.gelu(y).astype(o_ref.dtype)
        return 0

    lax.fori_loop(0, nsub, mm_out, 0)


def _s5_params(a_re, a_im, log_dt, b_re, b_im, c_re, c_im, d_skip):
    step = jnp.exp(log_dt)[:, None]
    mag = jnp.exp(a_re * step)
    abar_re, abar_im = mag * jnp.cos(a_im * step), mag * jnp.sin(a_im * step)
    den = a_re * a_re + a_im * a_im
    nr, ni = abar_re - 1.0, abar_im
    f_re, f_im = (nr * a_re + ni * a_im) / den, (ni * a_re - nr * a_im) / den
    bbar_re = f_re[..., None] * b_re - f_im[..., None] * b_im
    bbar_im = f_re[..., None] * b_im + f_im[..., None] * b_re
    eye = jnp.eye(S5_GB, dtype=F32)
    bb = jnp.stack([bbar_re, bbar_im]).reshape(2, S5_NBLK, S5_GB, S5_STATE, S5_GROUP)
    wb = jnp.einsum('ab,rjapc->jacrbp', eye, bb).reshape(S5_NBLK, LANES, 2 * S5_LANES)
    wb = wb.reshape(S5_PAIRS, 2, LANES, 2 * S5_LANES).transpose(0, 2, 1, 3).reshape(S5_PAIRS, LANES, 4 * S5_LANES)
    cc = jnp.stack([c_re, -c_im]).reshape(2, S5_NBLK, S5_GB, S5_GROUP, S5_STATE)
    wc = jnp.einsum('ab,rjacp->jrapbc', eye, cc).reshape(S5_NBLK, 2 * S5_LANES, LANES)
    wc = wc.reshape(S5_PAIRS, 2, 2 * S5_LANES, LANES).transpose(0, 2, 1, 3).reshape(S5_PAIRS, 2 * S5_LANES, 2 * LANES)
    ab = jnp.concatenate([abar_re.reshape(S5_NBLK, S5_LANES), abar_im.reshape(S5_NBLK, S5_LANES)], axis=1)
    ab = jnp.tile(ab.reshape(S5_PAIRS, 2, 2 * S5_LANES), (1, SUBLANES // 2, 1))
    dd = jnp.tile(d_skip.reshape(S5_PAIRS, 2, LANES), (1, SUBLANES // 2, 1))
    return wb.astype(BF16), ab, wc.astype(BF16), dd


def _s5(u6, wb, ab, wc, dd, l):
    rows = S5_TC * SUBLANES
    return pl.pallas_call(
        _s5_kernel,
        grid=(S5_PAIRS, l // S5_TC),
        in_specs=[pl.BlockSpec((1, rows, LANES), lambda k, c: (k, c, 0)),
                  pl.BlockSpec((1, LANES, 4 * S5_LANES), lambda k, c: (k, 0, 0)),
                  pl.BlockSpec((1, SUBLANES, 2 * S5_LANES), lambda k, c: (k, 0, 0)),
                  pl.BlockSpec((1, 2 * S5_LANES, 2 * LANES), lambda k, c: (k, 0, 0)),
                  pl.BlockSpec((1, SUBLANES, LANES), lambda k, c: (k, 0, 0))],
        out_specs=pl.BlockSpec((1, rows, LANES), lambda k, c: (k, c, 0)),
        out_shape=jax.ShapeDtypeStruct(u6.shape, F32),
        scratch_shapes=[pltpu.VMEM((2 * S5_LANES // LANES, rows, LANES), F32),
                        pltpu.VMEM((SUBLANES, 2 * S5_LANES), F32)],
        compiler_params=_cparams(("arbitrary", "arbitrary")),
        name="s5",
    )(u6, wb, ab, wc, dd)


def _glu_kernel(y_ref, w_ref, o_ref, z_scr, *, bn):
    b = pl.program_id(1)
    rows = z_scr.shape[1]

    @pl.when(b == 0)
    def _():
        y = jnp.concatenate([y_ref.at[k][pl.ds(s, rows, stride=2), :] for k in range(S5_PAIRS) for s in range(2)],
                            axis=1)
        g = jnp.dot(y.astype(BF16), w_ref[...], preferred_element_type=F32)
        z = y * jax.nn.sigmoid(g)
        for c in range(S5_NBLK):
            z_scr[c] = z[:, c * LANES:(c + 1) * LANES]

    o_ref[0] = jnp.concatenate([z_scr.at[c][pl.ds(b, rows // bn, stride=bn), :] for c in range(S5_NBLK)],
                               axis=1).astype(o_ref.dtype)


def _glu(y6, w_glu, bn, l):
    tt = 256
    rows = tt * bn
    return pl.pallas_call(
        functools.partial(_glu_kernel, bn=bn),
        grid=(l // tt, bn),
        in_specs=[pl.BlockSpec((S5_PAIRS, rows * 2, LANES), lambda i, b: (0, i, 0)),
                  pl.BlockSpec((S5_CH, S5_CH), lambda i, b: (0, 0))],
        out_specs=pl.BlockSpec((1, tt, S5_CH), lambda i, b: (b, i, 0)),
        out_shape=jax.ShapeDtypeStruct((bn, l, S5_CH), BF16),
        scratch_shapes=[pltpu.VMEM((S5_NBLK, rows, LANES), F32)],
        compiler_params=_cparams(("arbitrary", "arbitrary")),
        name="glu",
    )(y6, w_glu)


def _merge_kernel(att_ref, z_ref, gm_ref, gs_ref, x_ref, mod_ref, gpost_ref, gpre_ref,
                  wbm_ref, wbs_ref, wo_ref, wrh_ref, wrl_ref, x1_ref, h2b_ref, h2s_ref, lg_ref):
    tm = x_ref.shape[1]
    th = tm // MERGE_SUBTILES
    for r0 in range(0, tm, th):
        rows = slice(r0, r0 + th)
        ym = jnp.dot(att_ref[rows, :], wbm_ref[...], preferred_element_type=F32)
        ys = jnp.dot(z_ref[rows, :], wbs_ref[...], preferred_element_type=F32)
        mixed_in = (gm_ref[rows, :].astype(F32) * ym + gs_ref[rows, :].astype(F32) * ys).astype(BF16)
        mixed = jnp.dot(mixed_in, wo_ref[...], preferred_element_type=F32)
        x1 = x_ref[0, rows, :] + mod_ref[0, 2:3, :] * _rms(mixed, gpost_ref[...])
        x1_ref[0, rows, :] = x1
        h2 = _rms(x1, gpre_ref[...]) * (1.0 + mod_ref[0, 4:5, :]) + mod_ref[0, 3:4, :]
        h2_hi = h2.astype(BF16)
        h2b_ref[0, rows, :] = h2_hi
        for s in range(SLAB):
            h2s_ref[pl.ds(r0 * SLAB + s, th, stride=SLAB), :] = h2[:, s * LANES:(s + 1) * LANES]
        h2_lo = (h2 - h2_hi.astype(F32)).astype(BF16)
        lg_ref[:, rows] = (lax.dot_general(wrh_ref[...], h2_hi, NT_DIMS, preferred_element_type=F32)
                           + lax.dot_general(wrl_ref[...], h2_hi, NT_DIMS, preferred_element_type=F32)
                           + lax.dot_general(wrh_ref[...], h2_lo, NT_DIMS, preferred_element_type=F32))


def _merge(att, z_tm, gates, x, mod, g_post, g_pre_ffn, w_br_mla, w_br_s5, w_out, wr_hi, wr_lo):
    bn, l, d = x.shape
    tm = 256
    nl = l // tm
    row = lambda b, i: (b * nl + i, 0)
    const = lambda b, i: (0, 0)
    one = pl.Buffered(1)
    return pl.pallas_call(
        _merge_kernel,
        grid=(bn, nl),
        in_specs=[pl.BlockSpec((tm, MLA_HEADS * V_HEAD), row),
                  pl.BlockSpec((tm, S5_CH), row),
                  pl.BlockSpec((tm, d), lambda b, i: (b * nl + i, 0)),
                  pl.BlockSpec((tm, d), lambda b, i: (b * nl + i, 1)),
                  pl.BlockSpec((1, tm, d), lambda b, i: (b, i, 0)),
                  pl.BlockSpec((1, 6, d), lambda b, i: (b, 0, 0)),
                  pl.BlockSpec((1, d), const),
                  pl.BlockSpec((1, d), const),
                  pl.BlockSpec((MLA_HEADS * V_HEAD, d), const, pipeline_mode=one),
                  pl.BlockSpec((S5_CH, d), const, pipeline_mode=one),
                  pl.BlockSpec((d, d), const, pipeline_mode=one),
                  pl.BlockSpec((N_EXPERTS, d), const, pipeline_mode=one),
                  pl.BlockSpec((N_EXPERTS, d), const, pipeline_mode=one)],
        out_specs=[pl.BlockSpec((1, tm, d), lambda b, i: (b, i, 0)),
                   pl.BlockSpec((1, tm, d), lambda b, i: (b, i, 0)),
                   pl.BlockSpec((tm * SLAB, LANES), row),
                   pl.BlockSpec((N_EXPERTS, tm), lambda b, i: (0, b * nl + i))],
        out_shape=[jax.ShapeDtypeStruct((bn, l, d), F32),
                   jax.ShapeDtypeStruct((bn, l, d), BF16),
                   jax.ShapeDtypeStruct((bn * l * SLAB, LANES), F32),
                   jax.ShapeDtypeStruct((N_EXPERTS, bn * l), F32)],
        compiler_params=_cparams(("arbitrary", "arbitrary")),
        name="merge",
    )(att, z_tm, gates, gates, x, mod, g_post.reshape(1, d), g_pre_ffn.reshape(1, d),
      w_br_mla, w_br_s5, w_out, wr_hi, wr_lo)


def _route_kernel(lg_ref, b_ref, ti_ref, tw_ref, cnt_ref):
    ng = N_EXPERT_GROUPS
    gsz = N_EXPERTS // ng
    lg = lg_ref[...]
    t = lg.shape[-1]
    sc = jax.nn.sigmoid(lg)
    sel = sc + b_ref[...]
    ninf = -jnp.inf
    i_in = lax.broadcasted_iota(jnp.int32, (ng, gsz, t), 1).astype(F32)
    m1 = jnp.max(sel, axis=1, keepdims=True)
    idx1 = jnp.min(jnp.where(sel == m1, i_in, float(gsz)), axis=1, keepdims=True)
    m2 = jnp.max(jnp.where(i_in == idx1, ninf, sel), axis=1, keepdims=True)
    gs = m1 + m2
    g_i = lax.broadcasted_iota(jnp.int32, (ng, 1, t), 0).astype(F32)
    picked = jnp.zeros((ng, 1, t), F32)
    cur = gs
    for _ in range(TOPK_GROUPS):
        m = jnp.max(cur, axis=0, keepdims=True)
        idx = jnp.min(jnp.where(cur == m, g_i, float(ng)), axis=0, keepdims=True)
        hit = g_i == idx
        picked = jnp.where(hit, 1.0, picked)
        cur = jnp.where(hit, ninf, cur)
    cand = jnp.where(picked > 0.5, sel, ninf)
    e_i = lax.broadcasted_iota(jnp.int32, (ng, gsz, t), 0).astype(F32) * float(gsz) + i_in
    ws = []
    hits = jnp.zeros((ng, gsz, t), F32)
    for r in range(TOP_K):
        m = jnp.max(jnp.max(cand, axis=1, keepdims=True), axis=0, keepdims=True)
        idx = jnp.min(jnp.min(jnp.where(cand == m, e_i, float(N_EXPERTS)), axis=1, keepdims=True),
                      axis=0, keepdims=True)
        hit = e_i == idx
        w = jnp.sum(jnp.sum(jnp.where(hit, sc, 0.0), axis=1, keepdims=True), axis=0, keepdims=True)
        ti_ref[r:r + 1, :] = idx[0].astype(jnp.int32)
        ws.append(w[0])
        hits = jnp.where(hit, 1.0, hits)
        cand = jnp.where(hit, ninf, cand)

    @pl.when(pl.program_id(0) == 0)
    def _():
        cnt_ref[...] = jnp.zeros(cnt_ref.shape, F32)

    cnt_ref[...] += jnp.sum(hits, axis=2, keepdims=True)
    tot = ws[0]
    for w in ws[1:]:
        tot = tot + w
    for r in range(TOP_K):
        tw_ref[r:r + 1, :] = ws[r] / tot * ROUTED_SCALE
    for r in range(TOP_K, SUBLANES):
        ti_ref[r:r + 1, :] = jnp.zeros((1, t), jnp.int32)
        tw_ref[r:r + 1, :] = jnp.zeros((1, t), F32)


def _route(logits_t, router_bias):
    n = logits_t.shape[1]
    ng = N_EXPERT_GROUPS
    gsz = N_EXPERTS // ng
    tn = 1024
    return pl.pallas_call(
        _route_kernel,
        grid=(n // tn,),
        in_specs=[pl.BlockSpec((ng, gsz, tn), lambda i: (0, 0, i)),
                  pl.BlockSpec((ng, gsz, 1), lambda i: (0, 0, 0))],
        out_specs=[pl.BlockSpec((SUBLANES, tn), lambda i: (0, i)),
                   pl.BlockSpec((SUBLANES, tn), lambda i: (0, i)),
                   pl.BlockSpec((ng, gsz, 1), lambda i: (0, 0, 0))],
        out_shape=[jax.ShapeDtypeStruct((SUBLANES, n), jnp.int32),
                   jax.ShapeDtypeStruct((SUBLANES, n), F32),
                   jax.ShapeDtypeStruct((ng, gsz, 1), F32)],
        compiler_params=_cparams(("arbitrary",)),
        name="route",
    )(logits_t.reshape(ng, gsz, n), router_bias.reshape(ng, gsz, 1))


def _moe_kernel(blk_e_ref, nused_ref, nvalid_ref, nxt_e_ref, wslot_ref, t_ref, t1_ref, t2_ref, a_ref, h_hbm,
                wg_hbm, wu_hbm, wd_hbm, y_hbm, xbuf, ybuf, wg_buf, wu_buf, wd_buf, wg_s, wu_s, wd_s,
                gsem, wsem, wtsem):
    j = pl.program_id(0)
    n_used = nused_ref[0]
    slot = lax.rem(j, GATHER_SLOTS)
    nv = nvalid_ref[j]
    nv_prev = nvalid_ref[jnp.maximum(j - 1, 0)]

    def slab(ref, row0, align=SUBLANES):
        return ref.at[pl.ds(pl.multiple_of(row0, align), SLAB), :]

    def start_gather(idx_ref, s):
        def body(r, _):
            pltpu.make_async_copy(slab(h_hbm, idx_ref[0, 0, r]), slab(xbuf.at[s], r * MOE_PITCH, PITCH_ALIGN),
                                  gsem.at[s]).start()
            return 0
        lax.fori_loop(0, MOE_TB, body, 0, unroll=8)

    def wait_gather(s):
        rows = MOE_TB * SLAB
        pltpu.make_async_copy(h_hbm.at[pl.ds(0, rows), :], xbuf.at[s, pl.ds(0, rows), :], gsem.at[s]).wait()

    def weight_copies(expert, s):
        return (pltpu.make_async_copy(wg_hbm.at[expert], wg_buf.at[s], wtsem.at[s]),
                pltpu.make_async_copy(wu_hbm.at[expert], wu_buf.at[s], wtsem.at[s]),
                pltpu.make_async_copy(wd_hbm.at[expert], wd_buf.at[s], wtsem.at[s]))

    def wait_writes(count):
        p = MOE_TB
        while p >= 1:
            @pl.when((count & p) != 0)
            def _(p=p):
                rows = p * SLAB
                pltpu.make_async_copy(ybuf.at[pl.ds(0, rows), :], y_hbm.at[pl.ds(0, rows), :], wsem.at[0]).wait()
            p //= 2

    @pl.when(j < n_used)
    def _():
        @pl.when(j == 0)
        def _():
            start_gather(t_ref, 0)

            @pl.when(n_used > 1)
            def _():
                start_gather(t1_ref, 1)

        e = blk_e_ref[j]
        ws = wslot_ref[j]

        @pl.when(j == 0)
        def _():
            for cp in weight_copies(e, ws):
                cp.start(priority=WEIGHT_DMA_PRIORITY)

        first = jnp.logical_or(j == 0, e != blk_e_ref[jnp.maximum(j - 1, 0)])

        @pl.when(first)
        def _():
            for cp in weight_copies(e, ws):
                cp.wait()
            wg_s[...] = wg_buf[ws].astype(BF16)
            wu_s[...] = wu_buf[ws].astype(BF16)
            wd_s[...] = wd_buf[ws].astype(BF16)

        wait_gather(slot)

        @pl.when(j + 2 < n_used)
        def _():
            start_gather(t2_ref, lax.rem(j + 2, GATHER_SLOTS))

        @pl.when(jnp.logical_and(first, nxt_e_ref[j] >= 0))
        def _():
            for cp in weight_copies(nxt_e_ref[j], 1 - ws):
                cp.start(priority=WEIGHT_DMA_PRIORITY)

        xs = xbuf.at[slot]
        x = jnp.concatenate([xs[pl.ds(s, MOE_TB, stride=MOE_PITCH), :] for s in range(SLAB)], axis=1).astype(BF16)
        g = jnp.dot(x, wg_s[...], preferred_element_type=F32)
        u = jnp.dot(x, wu_s[...], preferred_element_type=F32)
        hm = (g * jax.nn.sigmoid(g) * u).astype(BF16)
        y = jnp.dot(hm, wd_s[...], preferred_element_type=F32)

        @pl.when(j > 0)
        def _():
            wait_writes(nv_prev)

        for s in range(SLAB):
            ybuf[pl.ds(s, MOE_TB, stride=MOE_PITCH), :] = y[:, s * LANES:(s + 1) * LANES]

        def start_write(r, priority):
            pltpu.make_async_copy(slab(ybuf, r * MOE_PITCH, PITCH_ALIGN), slab(y_hbm, a_ref[0, 0, r]),
                                  wsem.at[0]).start(priority=priority)

        def wgroup(g8, _):
            for q in range(SUBLANES):
                start_write(g8 * SUBLANES + q, q % 2)
            return 0

        def wtail(r, _):
            start_write(r, 0)
            return 0

        n_groups = lax.shift_right_logical(nv, 3)
        lax.fori_loop(0, n_groups, wgroup, 0)
        lax.fori_loop(n_groups * SUBLANES, nv, wtail, 0)

        @pl.when(j == n_used - 1)
        def _():
            wait_writes(nv)


def _moe(h2s, topi, counts, w_gate, w_up, w_down):
    n = h2s.shape[0] // SLAB
    d = D_MODEL
    nk = n * TOP_K
    tb = MOE_TB
    n_blocks = nk // tb + N_EXPERTS
    flat_e = topi[:TOP_K].reshape(nk)
    _, order = lax.sort_key_val(flat_e, jnp.arange(nk, dtype=jnp.int32))
    start = jnp.cumsum(counts) - counts
    padded = (counts + tb - 1) // tb * tb
    pad_end = jnp.cumsum(padded)
    pad_start = pad_end - padded
    n_used = (pad_end[-1] // tb).astype(jnp.int32).reshape(1)
    blk_p0 = jnp.arange(n_blocks, dtype=jnp.int32) * tb
    blk_e = jnp.minimum(jnp.sum((pad_end[None, :] <= blk_p0[:, None]).astype(jnp.int32), axis=1), N_EXPERTS - 1)
    blk_off = blk_p0 - pad_start[blk_e]
    nvalid = jnp.clip(counts[blk_e] - blk_off, 0, tb).astype(jnp.int32)
    rank = (start[blk_e] + blk_off)[:, None] + jnp.arange(tb, dtype=jnp.int32)[None, :]
    a3 = order[jnp.clip(rank, 0, nk - 1)].reshape(n_blocks, 1, tb)
    arow = a3 * SLAB
    trow = (a3 % n) * SLAB
    used = counts > 0
    e_ids = jnp.arange(N_EXPERTS, dtype=jnp.int32)
    later = jnp.where(used, e_ids, N_EXPERTS)
    nxt = jnp.concatenate([lax.cummin(later[::-1])[::-1][1:], jnp.full((1,), N_EXPERTS, jnp.int32)])
    nxt_e = jnp.where(nxt < N_EXPERTS, nxt, -1)[blk_e].astype(jnp.int32)
    wslot = ((jnp.cumsum(used.astype(jnp.int32)) - 1) % 2)[blk_e].astype(jnp.int32)

    smem_blk = lambda f: pl.BlockSpec((1, 1, tb), f, memory_space=pltpu.SMEM)
    any_spec = pl.BlockSpec(memory_space=pl.ANY)
    grid_spec = pltpu.PrefetchScalarGridSpec(
        num_scalar_prefetch=5,
        grid=(n_blocks,),
        in_specs=[smem_blk(lambda j, *_: (j, 0, 0)),
                  smem_blk(lambda j, *_: (jnp.minimum(j + 1, n_blocks - 1), 0, 0)),
                  smem_blk(lambda j, *_: (jnp.minimum(j + 2, n_blocks - 1), 0, 0)),
                  smem_blk(lambda j, *_: (j, 0, 0)),
                  any_spec, any_spec, any_spec, any_spec],
        out_specs=any_spec,
        scratch_shapes=[pltpu.VMEM((GATHER_SLOTS, tb * MOE_PITCH, LANES), F32),
                        pltpu.VMEM((tb * MOE_PITCH, LANES), F32),
                        pltpu.VMEM((2, d, D_EXPERT), F32),
                        pltpu.VMEM((2, d, D_EXPERT), F32),
                        pltpu.VMEM((2, D_EXPERT, d), F32),
                        pltpu.VMEM((d, D_EXPERT), BF16),
                        pltpu.VMEM((d, D_EXPERT), BF16),
                        pltpu.VMEM((D_EXPERT, d), BF16),
                        pltpu.SemaphoreType.DMA((GATHER_SLOTS,)),
                        pltpu.SemaphoreType.DMA((1,)),
                        pltpu.SemaphoreType.DMA((2,))],
    )
    return pl.pallas_call(
        _moe_kernel,
        grid_spec=grid_spec,
        out_shape=jax.ShapeDtypeStruct((TOP_K * n * SLAB, LANES), F32),
        compiler_params=_cparams(("arbitrary",)),
        name="moe",
    )(blk_e, n_used, nvalid, nxt_e, wslot, trow, trow, trow, arow, h2s, w_gate, w_up, w_down)


def _final_kernel(h_ref, y0, y1, y2, y3, y4, y5, tw_ref, x1_ref, mod_ref, g_ref, wg_ref, wu_ref, wd_ref, o_ref):
    h = h_ref[...]
    tm = h.shape[0]
    g = jnp.dot(h, wg_ref[...], preferred_element_type=F32)
    u = jnp.dot(h, wu_ref[...], preferred_element_type=F32)
    hm = (g * jax.nn.sigmoid(g) * u).astype(BF16)
    ffn = jnp.dot(hm, wd_ref[...], preferred_element_type=F32)
    tw = tw_ref[...]
    for k, y in enumerate((y0, y1, y2, y3, y4, y5)):
        yk = jnp.concatenate([y[pl.ds(s, tm, stride=SLAB), :] for s in range(SLAB)], axis=1)
        ffn = ffn + yk * tw[:, k:k + 1]
    o_ref[0] = x1_ref[0] + mod_ref[0, 5:6, :] * _rms(ffn, g_ref[...])


def _final(h2, y_flat, topw_t, x1, mod, g_post_ffn, w_sg, w_su, w_sd):
    bn, l, d = x1.shape
    n = bn * l
    tm = 256
    nl = l // tm
    const = lambda b, i: (0, 0)
    one = pl.Buffered(1)
    y_specs = [pl.BlockSpec((tm * SLAB, LANES), functools.partial(lambda b, i, k: (k * (n // tm) + b * nl + i, 0), k=k))
               for k in range(TOP_K)]
    return pl.pallas_call(
        _final_kernel,
        grid=(bn, nl),
        in_specs=[pl.BlockSpec((tm, d), lambda b, i: (b * nl + i, 0))] + y_specs + [
            pl.BlockSpec((tm, SUBLANES), lambda b, i: (b * nl + i, 0)),
            pl.BlockSpec((1, tm, d), lambda b, i: (b, i, 0)),
            pl.BlockSpec((1, 6, d), lambda b, i: (b, 0, 0)),
            pl.BlockSpec((1, d), const),
            pl.BlockSpec((d, D_EXPERT), const, pipeline_mode=one),
            pl.BlockSpec((d, D_EXPERT), const, pipeline_mode=one),
            pl.BlockSpec((D_EXPERT, d), const, pipeline_mode=one)],
        out_specs=pl.BlockSpec((1, tm, d), lambda b, i: (b, i, 0)),
        out_shape=jax.ShapeDtypeStruct((bn, l, d), F32),
        compiler_params=_cparams(("arbitrary", "arbitrary")),
        name="final",
    )(h2, *([y_flat] * TOP_K), topw_t, x1, mod, g_post_ffn.reshape(1, d), w_sg, w_su, w_sd)


def _layer(li, x, c, positions, w_ada, b_ada, g_pre_mix, g_post_mix, g_pre_ffn, g_post_ffn, w_in, g_q, g_kv,
           w_uq, w_uk, w_uv, a_re, a_im, log_dt, b_re, b_im, c_re, c_im, d_skip, w_glu, w_br_mla, w_br_s5,
           w_out, w_router, router_bias, w_exp_gate, w_exp_up, w_exp_down, w_sh_gate, w_sh_up, w_sh_down):
    bn, l, d = x.shape
    n = bn * l
    mod = _adaln(c, w_ada, b_ada).reshape(bn, 6, d)

    o_kpe = Q_LORA + KV_LORA
    o_u = o_kpe + QK_ROPE
    o_g = o_u + S5_CH
    w_in_t = jnp.swapaxes(w_in, 1, 2)
    w_q = jnp.pad(w_uq.reshape(Q_LORA, MLA_HEADS, QK_NOPE + QK_ROPE),
                  ((0, 0), (0, 0), (0, QK_PAD - QK_NOPE - QK_ROPE))).reshape(Q_LORA, MLA_HEADS * QK_PAD).astype(BF16)
    wr_t = w_router.T
    wr_hi = wr_t.astype(BF16)
    wr_lo = (wr_t - wr_hi.astype(F32)).astype(BF16)

    rope_c, rope_s1, rope_s2 = _rope_tables(positions)
    chunk_id = positions // CHUNK

    h = _prenorm(x, mod, g_pre_mix, 0, 1).reshape(n, d)
    lat = _mmt(h, w_in_t, li, 0, o_kpe, F32, 1024, 512, name="mm_lat")
    kpe = _mmt(h, w_in_t, li, o_kpe, LANES, F32, 1024, LANES, name="mm_kpe")
    gates = _mmt(h, w_in_t, li, o_g, 2 * d, BF16, 1024, 1024, act="sigmoid", name="mm_gates")
    u6 = _mm_u(h, w_in_t, li, o_u, bn, l)

    q, k, v = _qkvproj(lat, g_q, g_kv, w_q, w_uk.astype(BF16), w_uv.astype(BF16), kpe, rope_c, rope_s1, rope_s2)
    att = _attention(q, k, v, chunk_id, bn, l)

    wb, ab, wc, dd = _s5_params(a_re, a_im, log_dt, b_re, b_im, c_re, c_im, d_skip)
    y6 = _s5(u6, wb, ab, wc, dd, l)
    z = _glu(y6, w_glu.astype(BF16), bn, l).reshape(n, S5_CH)

    x1, h2b, h2s, logits_t = _merge(att, z, gates, x, mod, g_post_mix, g_pre_ffn, w_br_mla.astype(BF16),
                                    w_br_s5.astype(BF16), w_out.astype(BF16), wr_hi, wr_lo)

    topi, topw, cnt = _route(logits_t, router_bias)
    y_slabs = _moe(h2s, topi, cnt.reshape(N_EXPERTS).astype(jnp.int32), w_exp_gate, w_exp_up, w_exp_down)
    return _final(h2b.reshape(n, d), y_slabs, topw.T, x1, mod, g_post_ffn, w_sh_gate.astype(BF16),
                  w_sh_up.astype(BF16), w_sh_down.astype(BF16))


def kernel(x, c, positions, w_ada, b_ada, g_pre_mix, g_post_mix, g_pre_ffn, g_post_ffn, w_in, g_q, g_kv, w_uq, w_uk, w_uv, a_re, a_im, log_dt, b_re, b_im, c_re, c_im, d_skip, w_glu, w_br_mla, w_br_s5, w_out, w_router, router_bias, w_exp_gate, w_exp_up, w_exp_down, w_sh_gate, w_sh_up, w_sh_down):
    depth = w_ada.shape[0]
    for li in range(depth):
        x = _layer(li, x, c, positions, w_ada[li], b_ada[li], g_pre_mix[li], g_post_mix[li], g_pre_ffn[li],
                   g_post_ffn[li], w_in, g_q[li], g_kv[li], w_uq[li], w_uk[li], w_uv[li], a_re[li], a_im[li],
                   log_dt[li], b_re[li], b_im[li], c_re[li], c_im[li], d_skip[li], w_glu[li], w_br_mla[li],
                   w_br_s5[li], w_out[li], w_router[li], router_bias[li], w_exp_gate[li], w_exp_up[li],
                   w_exp_down[li], w_sh_gate[li], w_sh_up[li], w_sh_down[li])
    return x
```

```python
import functools

import jax
import jax.numpy as jnp
from jax import lax
from jax.experimental import pallas as pl
from jax.experimental.pallas import tpu as pltpu

F32 = jnp.float32
BF16 = jnp.bfloat16

D_MODEL = 2048
CHUNK = 64
EPS = 1e-6
MLA_HEADS = 8
QK_NOPE = 128
QK_ROPE = 64
V_HEAD = 128
Q_LORA = 512
KV_LORA = 512
ROPE_THETA = 10000.0
S5_CH = 1024
S5_GROUP = 16
S5_GROUPS = S5_CH // S5_GROUP
S5_STATE = 64
N_EXPERTS = 64
TOP_K = 6
N_EXPERT_GROUPS = 8
TOPK_GROUPS = 4
D_EXPERT = 512
ROUTED_SCALE = 2.5

LANES = 128
SUBLANES = 8
QK_PAD = 2 * LANES
VMEM_LIMIT = 56 * 1024 * 1024
NEG = -1e30
LOG2E = 1.4426950408889634

S5_GB = LANES // S5_GROUP
S5_NBLK = S5_CH // LANES
S5_PAIRS = S5_NBLK // 2
S5_LANES = S5_GB * S5_STATE
S5_TC = 256
S5_RB = 512

ATT_TQ = 256
ATT_TK = 256
MOE_TB = 256
SLAB = D_MODEL // LANES
MERGE_SUBTILES = 2
GATHER_SLOTS = 3
WEIGHT_DMA_PRIORITY = 1
MOE_PITCH = 20
PITCH_ALIGN = 4


def _cparams(sem):
    return pltpu.CompilerParams(dimension_semantics=sem, vmem_limit_bytes=VMEM_LIMIT)


def _rms(x, g):
    return x * lax.rsqrt(jnp.mean(x * x, axis=-1, keepdims=True) + EPS) * g


def _adaln_kernel(c_ref, w_ref, b_ref, o_ref):
    c = c_ref[...]
    a = (c * jax.nn.sigmoid(c)).astype(BF16)
    o_ref[...] = jnp.dot(a, w_ref[...].astype(BF16), preferred_element_type=F32) + b_ref[...]


def _adaln(c, w, b):
    bn, d = c.shape
    n = w.shape[1]
    tn = 1024
    return pl.pallas_call(
        _adaln_kernel,
        grid=(n // tn,),
        in_specs=[pl.BlockSpec((bn, d), lambda j: (0, 0)),
                  pl.BlockSpec((d, tn), lambda j: (0, j)),
                  pl.BlockSpec((1, tn), lambda j: (0, j))],
        out_specs=pl.BlockSpec((bn, tn), lambda j: (0, j)),
        out_shape=jax.ShapeDtypeStruct((bn, n), F32),
        compiler_params=_cparams(("arbitrary",)),
        name="adaln",
    )(c, w, b.reshape(1, n))


NT_DIMS = (((1,), (1,)), ((), ()))


def _prenorm_lat_kernel(x_ref, mod_ref, g_ref, w_ref, h_ref, lat_ref, w_scr, *, sh_row, sc_row):
    @pl.when(jnp.logical_and(pl.program_id(0) == 0, pl.program_id(1) == 0))
    def _():
        w_scr[...] = w_ref[...].astype(BF16)

    y = _rms(x_ref[0], g_ref[...])
    h = (y * (1.0 + mod_ref[0, sc_row:sc_row + 1, :]) + mod_ref[0, sh_row:sh_row + 1, :]).astype(BF16)
    h_ref[0] = h
    lat_ref[...] = lax.dot_general(h, w_scr[...], NT_DIMS, preferred_element_type=F32)


def _prenorm_lat(x, mod, g, sh_row, sc_row, w_t, layer, n_lat):
    bn, l, d = x.shape
    tl = 512
    nl = l // tl
    return pl.pallas_call(
        functools.partial(_prenorm_lat_kernel, sh_row=sh_row, sc_row=sc_row),
        grid=(bn, nl),
        in_specs=[pl.BlockSpec((1, tl, d), lambda b, i: (b, i, 0)),
                  pl.BlockSpec((1, 6, d), lambda b, i: (b, 0, 0)),
                  pl.BlockSpec((1, d), lambda b, i: (0, 0)),
                  pl.BlockSpec((None, n_lat, d), lambda b, i: (layer, 0, 0), pipeline_mode=pl.Buffered(1))],
        out_specs=[pl.BlockSpec((1, tl, d), lambda b, i: (b, i, 0)),
                   pl.BlockSpec((tl, n_lat), lambda b, i: (b * nl + i, 0))],
        out_shape=[jax.ShapeDtypeStruct((bn, l, d), BF16),
                   jax.ShapeDtypeStruct((bn * l, n_lat), F32)],
        scratch_shapes=[pltpu.VMEM((n_lat, d), BF16)],
        compiler_params=_cparams(("arbitrary", "arbitrary")),
        name="prenorm_lat",
    )(x, mod, g.reshape(1, d), w_t)


def _wt_tile(wa_ref, wb_ref, shift, tn):
    if shift == 0:
        return wa_ref[...].astype(BF16)
    return jnp.concatenate([wa_ref[...], wb_ref[...]], axis=0)[shift:shift + tn].astype(BF16)


def _wt_specs(layer, row0, tn, k, jmap):
    shift = row0 % LANES
    c0 = row0 - shift
    assert c0 % tn == 0 and shift % SUBLANES == 0
    specs = [pl.BlockSpec((None, tn, k), lambda *g: (layer, jmap(*g) + c0 // tn, 0))]
    if shift:
        specs.append(pl.BlockSpec((None, LANES, k), lambda *g: (layer, (c0 + (jmap(*g) + 1) * tn) // LANES, 0)))
    return specs, shift


def _mmt_kernel(a_ref, wa_ref, *rest, act, shift):
    wb_ref = rest[0] if shift else None
    o_ref, w_scr = rest[-2:]

    @pl.when(pl.program_id(1) == 0)
    def _():
        w_scr[...] = _wt_tile(wa_ref, wb_ref, shift, w_scr.shape[0])

    acc = lax.dot_general(a_ref[...], w_scr[...], NT_DIMS, preferred_element_type=F32)
    if act == "sigmoid":
        acc = jax.nn.sigmoid(acc)
    o_ref[...] = acc.astype(o_ref.dtype)


def _mmt(a, w_t, layer, row0, n, out_dtype, tm, tn, act=None, name="mmt"):
    m, k = a.shape
    w_specs, shift = _wt_specs(layer, row0, tn, k, lambda j, i: j)
    return pl.pallas_call(
        functools.partial(_mmt_kernel, act=act, shift=shift),
        grid=(n // tn, m // tm),
        in_specs=[pl.BlockSpec((tm, k), lambda j, i: (i, 0))] + w_specs,
        out_specs=pl.BlockSpec((tm, tn), lambda j, i: (i, j)),
        out_shape=jax.ShapeDtypeStruct((m, n), out_dtype),
        scratch_shapes=[pltpu.VMEM((tn, k), BF16)],
        compiler_params=_cparams(("arbitrary", "arbitrary")),
        name=name,
    )(a, *([w_t] * len(w_specs)))


def _mm_u_kernel(a_ref, wa_ref, *rest, shift):
    wb_ref = rest[0] if shift else None
    o_ref, w_scr = rest[-2:]
    b = pl.program_id(1)

    @pl.when(jnp.logical_and(pl.program_id(0) == 0, b == 0))
    def _():
        w_scr[...] = _wt_tile(wa_ref, wb_ref, shift, w_scr.shape[0])

    res = lax.dot_general(a_ref[...], w_scr[...], NT_DIMS, preferred_element_type=F32)
    tm = res.shape[0]
    for c in range(S5_NBLK):
        o_ref.at[c // 2][pl.ds(b * 2 + c % 2, tm, stride=SUBLANES), :] = res[:, c * LANES:(c + 1) * LANES]


def _mm_u(h, w_t, layer, row0, bn, l):
    m, k = h.shape
    tm = 512
    nl = l // tm
    assert bn * 2 == SUBLANES
    w_specs, shift = _wt_specs(layer, row0, S5_CH, k, lambda i, b: 0)
    return pl.pallas_call(
        functools.partial(_mm_u_kernel, shift=shift),
        grid=(nl, bn),
        in_specs=[pl.BlockSpec((tm, k), lambda i, b: (b * nl + i, 0))] + w_specs,
        out_specs=pl.BlockSpec((S5_PAIRS, tm * SUBLANES, LANES), lambda i, b: (0, i, 0)),
        out_shape=jax.ShapeDtypeStruct((S5_PAIRS, l * SUBLANES, LANES), F32),
        scratch_shapes=[pltpu.VMEM((S5_CH, k), BF16)],
        compiler_params=_cparams(("arbitrary", "arbitrary")),
        name="mm_u",
    )(h, *([w_t] * len(w_specs)))


def _rope_tab_kernel(pos_ref, k_ref, c_ref, s1_ref, s2_ref):
    ang = pos_ref[...].astype(F32) * k_ref[0:1, :]
    s = jnp.sin(ang)
    c_ref[...] = jnp.cos(ang) * k_ref[1:2, :]
    s1_ref[...] = s * k_ref[2:3, :]
    s2_ref[...] = s * k_ref[3:4, :]


def _rope_tables(positions):
    n = positions.size
    half = QK_ROPE // 2
    inv_freq = ROPE_THETA ** (-jnp.arange(half, dtype=F32) / half)
    zh, oh = jnp.zeros((half,), F32), jnp.ones((half,), F32)
    z2 = jnp.zeros((LANES - QK_ROPE,), F32)
    rows = [jnp.concatenate([inv_freq, inv_freq, z2]), jnp.concatenate([oh, oh, z2]),
            jnp.concatenate([-oh, zh, z2]), jnp.concatenate([zh, oh, z2])]
    consts = jnp.stack(rows + [jnp.zeros((LANES,), F32)] * (SUBLANES - len(rows)))
    tm = 1024
    tab = jax.ShapeDtypeStruct((n, LANES), F32)
    return pl.pallas_call(
        _rope_tab_kernel,
        grid=(n // tm,),
        in_specs=[pl.BlockSpec((tm, 1), lambda i: (i, 0)),
                  pl.BlockSpec((SUBLANES, LANES), lambda i: (0, 0))],
        out_specs=[pl.BlockSpec((tm, LANES), lambda i: (i, 0))] * 3,
        out_shape=[tab, tab, tab],
        compiler_params=_cparams(("arbitrary",)),
        name="rope_tables",
    )(positions.reshape(n, 1), consts)


def _rope_tile(t, c_ref, s1_ref, s2_ref):
    return (t * c_ref[...] + pltpu.roll(t, LANES - QK_ROPE // 2, 1) * s1_ref[...]
            + pltpu.roll(t, QK_ROPE // 2, 1) * s2_ref[...])


def _qkvproj_kernel(lat_ref, gq_ref, gkv_ref, wq_ref, wk_ref, wv_ref, kpe_ref, c_ref, s1_ref, s2_ref,
                    q_ref, k_ref, v_ref, *, scale):
    lat = lat_ref[...]
    qn = _rms(lat[:, :Q_LORA], gq_ref[...]).astype(BF16)
    cn = _rms(lat[:, Q_LORA:], gkv_ref[...]).astype(BF16)
    q = jnp.dot(qn, wq_ref[...], preferred_element_type=F32)
    kn = jnp.dot(cn, wk_ref[...], preferred_element_type=F32)
    v_ref[...] = jnp.dot(cn, wv_ref[...], preferred_element_type=F32).astype(v_ref.dtype)
    kt = _rope_tile(kpe_ref[...], c_ref, s1_ref, s2_ref).astype(k_ref.dtype)
    for h in range(MLA_HEADS):
        o = h * QK_PAD
        q_ref[:, o:o + LANES] = (q[:, o:o + LANES] * scale).astype(q_ref.dtype)
        qt = _rope_tile(q[:, o + LANES:o + QK_PAD], c_ref, s1_ref, s2_ref)
        q_ref[:, o + LANES:o + QK_PAD] = (qt * scale).astype(q_ref.dtype)
        k_ref[:, o:o + LANES] = kn[:, h * QK_NOPE:(h + 1) * QK_NOPE].astype(k_ref.dtype)
        k_ref[:, o + LANES:o + QK_PAD] = kt


def _qkvproj(lat, g_q, g_kv, w_q, w_k, w_v, rope_c, rope_s1, rope_s2):
    n = lat.shape[0]
    tm = 512
    row = lambda i: (i, 0)
    const = lambda i: (0, 0)
    tab = pl.BlockSpec((tm, LANES), row)
    kpe_spec = pl.BlockSpec((tm, LANES), lambda i: (i, (Q_LORA + KV_LORA) // LANES))
    return pl.pallas_call(
        functools.partial(_qkvproj_kernel, scale=(QK_NOPE + QK_ROPE) ** -0.5 * LOG2E),
        grid=(n // tm,),
        in_specs=[pl.BlockSpec((tm, Q_LORA + KV_LORA), row),
                  pl.BlockSpec((1, Q_LORA), const),
                  pl.BlockSpec((1, KV_LORA), const),
                  pl.BlockSpec((Q_LORA, MLA_HEADS * QK_PAD), const),
                  pl.BlockSpec((KV_LORA, MLA_HEADS * QK_NOPE), const),
                  pl.BlockSpec((KV_LORA, MLA_HEADS * V_HEAD), const),
                  kpe_spec, tab, tab, tab],
        out_specs=[pl.BlockSpec((tm, MLA_HEADS * QK_PAD), row),
                   pl.BlockSpec((tm, MLA_HEADS * QK_PAD), row),
                   pl.BlockSpec((tm, MLA_HEADS * V_HEAD), row)],
        out_shape=[jax.ShapeDtypeStruct((n, MLA_HEADS * QK_PAD), BF16),
                   jax.ShapeDtypeStruct((n, MLA_HEADS * QK_PAD), BF16),
                   jax.ShapeDtypeStruct((n, MLA_HEADS * V_HEAD), BF16)],
        compiler_params=_cparams(("arbitrary",)),
        name="qkvproj",
    )(lat, g_q.reshape(1, Q_LORA), g_kv.reshape(1, KV_LORA), w_q, w_k, w_v, lat, rope_c, rope_s1, rope_s2)


def _attn_kernel(lo_ref, hi_ref, q_ref, k_ref, v_ref, qc_ref, kc_ref, o_ref, m_scr, l_scr, acc_scr, *, nq):
    b = pl.program_id(0)
    i = pl.program_id(1)
    qc = qc_ref[...]
    m_scr[...] = jnp.full(m_scr.shape, NEG, F32)
    l_scr[...] = jnp.zeros(l_scr.shape, F32)
    acc_scr[...] = jnp.zeros(acc_scr.shape, F32)

    def make_body(masked):
        def body(j, _):
            off = pl.multiple_of(j * ATT_TK, ATT_TK)
            if masked:
                mask = kc_ref[j] <= qc
            for h in range(MLA_HEADS):
                q = q_ref[:, h * QK_PAD:(h + 1) * QK_PAD]
                k = k_ref[pl.ds(off, ATT_TK), h * QK_PAD:(h + 1) * QK_PAD]
                s = lax.dot_general(q, k, (((1,), (1,)), ((), ())), preferred_element_type=F32)
                if masked:
                    s = jnp.where(mask, s, NEG)
                m_old = m_scr[h]
                m_new = jnp.maximum(m_old, jnp.max(s, axis=-1, keepdims=True))
                p = jnp.exp2(s - jnp.concatenate([m_new] * (ATT_TK // LANES), axis=1))
                alpha = jnp.exp2(m_old - m_new)
                l_scr[h] = alpha * l_scr[h] + jnp.sum(p, axis=-1, keepdims=True)
                v = v_ref[pl.ds(off, ATT_TK), h * V_HEAD:(h + 1) * V_HEAD]
                acc_scr[h] = alpha * acc_scr[h] + jnp.dot(p.astype(BF16), v, preferred_element_type=F32)
                m_scr[h] = m_new
            return 0
        return body

    lo = lo_ref[b * nq + i]
    lax.fori_loop(0, lo, make_body(False), 0)
    lax.fori_loop(lo, hi_ref[b * nq + i], make_body(True), 0)
    for h in range(MLA_HEADS):
        o_ref[:, h * V_HEAD:(h + 1) * V_HEAD] = (acc_scr[h] / l_scr[h]).astype(o_ref.dtype)


def _attention(q, k, v, chunk_id, bn, l):
    nq = l // ATT_TQ
    nk = l // ATT_TK
    q_max = jnp.max(chunk_id.reshape(bn, nq, ATT_TQ), axis=-1)
    k_min = jnp.min(chunk_id.reshape(bn, nk, ATT_TK), axis=-1)
    needed = k_min[:, None, :] <= q_max[:, :, None]
    hi = jnp.max(jnp.where(needed, jnp.arange(1, nk + 1, dtype=jnp.int32), 0), axis=-1).reshape(bn * nq)
    q_min = jnp.min(chunk_id.reshape(bn, nq, ATT_TQ), axis=-1)
    k_max = jnp.max(chunk_id.reshape(bn, nk, ATT_TK), axis=-1)
    full = k_max[:, None, :] <= q_min[:, :, None]
    lo = jnp.min(jnp.where(full, nk, jnp.arange(nk, dtype=jnp.int32)), axis=-1).astype(jnp.int32).reshape(bn * nq)
    hi = jnp.maximum(hi, lo)
    qc = chunk_id.reshape(bn * l, 1)
    kc = chunk_id.reshape(bn * nk, 1, ATT_TK)
    grid_spec = pltpu.PrefetchScalarGridSpec(
        num_scalar_prefetch=2,
        grid=(bn, nq),
        in_specs=[pl.BlockSpec((ATT_TQ, MLA_HEADS * QK_PAD), lambda b, i, *_: (b * nq + i, 0)),
                  pl.BlockSpec((l, MLA_HEADS * QK_PAD), lambda b, i, *_: (b, 0)),
                  pl.BlockSpec((l, MLA_HEADS * V_HEAD), lambda b, i, *_: (b, 0)),
                  pl.BlockSpec((ATT_TQ, 1), lambda b, i, *_: (b * nq + i, 0)),
                  pl.BlockSpec((nk, 1, ATT_TK), lambda b, i, *_: (b, 0, 0))],
        out_specs=pl.BlockSpec((ATT_TQ, MLA_HEADS * V_HEAD), lambda b, i, *_: (b * nq + i, 0)),
        scratch_shapes=[pltpu.VMEM((MLA_HEADS, ATT_TQ, LANES), F32),
                        pltpu.VMEM((MLA_HEADS, ATT_TQ, LANES), F32),
                        pltpu.VMEM((MLA_HEADS, ATT_TQ, V_HEAD), F32)],
    )
    return pl.pallas_call(
        functools.partial(_attn_kernel, nq=nq),
        grid_spec=grid_spec,
        out_shape=jax.ShapeDtypeStruct((bn * l, MLA_HEADS * V_HEAD), BF16),
        compiler_params=_cparams(("arbitrary", "arbitrary")),
        name="attention",
    )(lo, hi, q, k, v, qc, kc)


def _s5_kernel(u_ref, wb_ref, a_ref, wc_ref, d_ref, o_ref, x_scr, st_scr):
    rows = S5_TC * SUBLANES
    nsub = rows // S5_RB

    @pl.when(pl.program_id(1) == 0)
    def _():
        st_scr[...] = jnp.zeros_like(st_scr)

    even = (lax.broadcasted_iota(jnp.int32, (S5_RB, 1), 0) & 1) == 0
    half = S5_RB // 2
    nre = S5_LANES // LANES
    uv = u_ref.at[0]

    def mm_in(r, _):
        off = pl.multiple_of(r * S5_RB, S5_RB)
        for s in range(2):
            us = uv[pl.ds(off + s, half, stride=2), :].astype(BF16)
            out = jnp.dot(us, wb_ref[0, :, s * 2 * S5_LANES:(s + 1) * 2 * S5_LANES], preferred_element_type=F32)
            for c in range(2 * nre):
                x_scr.at[c][pl.ds(off + s, half, stride=2), :] = out[:, c * LANES:(c + 1) * LANES]
        return 0

    lax.fori_loop(0, nsub, mm_in, 0)

    a_c = [a_ref[0, :, c * LANES:(c + 1) * LANES] for c in range(2 * nre)]

    def step(t, carry):
        off = pl.multiple_of(t * SUBLANES, SUBLANES)
        new = [None] * (2 * nre)
        for c in range(nre):
            xr, xi = carry[c], carry[nre + c]
            ar, ai = a_c[c], a_c[nre + c]
            nr = ar * xr - ai * xi + x_scr[c, pl.ds(off, SUBLANES), :]
            ni = ar * xi + ai * xr + x_scr[nre + c, pl.ds(off, SUBLANES), :]
            x_scr[c, pl.ds(off, SUBLANES), :] = nr
            x_scr[nre + c, pl.ds(off, SUBLANES), :] = ni
            new[c], new[nre + c] = nr, ni
        return tuple(new)

    init = tuple(st_scr[:, c * LANES:(c + 1) * LANES] for c in range(2 * nre))
    fin = lax.fori_loop(0, S5_TC, step, init, unroll=8)
    for c in range(2 * nre):
        st_scr[:, c * LANES:(c + 1) * LANES] = fin[c]

    d = jnp.concatenate([d_ref[0]] * (S5_RB // SUBLANES), axis=0)

    def mm_out(r, _):
        off = pl.multiple_of(r * S5_RB, S5_RB)
        x = jnp.concatenate([x_scr[c, pl.ds(off, S5_RB), :] for c in range(2 * nre)], axis=1).astype(BF16)
        out = jnp.dot(x, wc_ref[0], preferred_element_type=F32)
        y = jnp.where(even, out[:, :LANES], out[:, LANES:]) + d * u_ref[0, pl.ds(off, S5_RB), :]
        o_ref[0, pl.ds(off, S5_RB), :] = jax.nn.gelu(y).astype(o_ref.dtype)
        return 0

    lax.fori_loop(0, nsub, mm_out, 0)


def _s5_params(a_re, a_im, log_dt, b_re, b_im, c_re, c_im, d_skip):
    step = jnp.exp(log_dt)[:, None]
    mag = jnp.exp(a_re * step)
    abar_re, abar_im = mag * jnp.cos(a_im * step), mag * jnp.sin(a_im * step)
    den = a_re * a_re + a_im * a_im
    nr, ni = abar_re - 1.0, abar_im
    f_re, f_im = (nr * a_re + ni * a_im) / den, (ni * a_re - nr * a_im) / den
    bbar_re = f_re[..., None] * b_re - f_im[..., None] * b_im
    bbar_im = f_re[..., None] * b_im + f_im[..., None] * b_re
    eye = jnp.eye(S5_GB, dtype=F32)
    bb = jnp.stack([bbar_re, bbar_im]).reshape(2, S5_NBLK, S5_GB, S5_STATE, S5_GROUP)
    wb = jnp.einsum('ab,rjapc->jacrbp', eye, bb).reshape(S5_NBLK, LANES, 2 * S5_LANES)
    wb = wb.reshape(S5_PAIRS, 2, LANES, 2 * S5_LANES).transpose(0, 2, 1, 3).reshape(S5_PAIRS, LANES, 4 * S5_LANES)
    cc = jnp.stack([c_re, -c_im]).reshape(2, S5_NBLK, S5_GB, S5_GROUP, S5_STATE)
    wc = jnp.einsum('ab,rjacp->jrapbc', eye, cc).reshape(S5_NBLK, 2 * S5_LANES, LANES)
    wc = wc.reshape(S5_PAIRS, 2, 2 * S5_LANES, LANES).transpose(0, 2, 1, 3).reshape(S5_PAIRS, 2 * S5_LANES, 2 * LANES)
    ab = jnp.concatenate([abar_re.reshape(S5_NBLK, S5_LANES), abar_im.reshape(S5_NBLK, S5_LANES)], axis=1)
    ab = jnp.tile(ab.reshape(S5_PAIRS, 2, 2 * S5_LANES), (1, SUBLANES // 2, 1))
    dd = jnp.tile(d_skip.reshape(S5_PAIRS, 2, LANES), (1, SUBLANES // 2, 1))
    return wb.astype(BF16), ab, wc.astype(BF16), dd


def _s5(u6, wb, ab, wc, dd, l):
    rows = S5_TC * SUBLANES
    return pl.pallas_call(
        _s5_kernel,
        grid=(S5_PAIRS, l // S5_TC),
        in_specs=[pl.BlockSpec((1, rows, LANES), lambda k, c: (k, c, 0)),
                  pl.BlockSpec((1, LANES, 4 * S5_LANES), lambda k, c: (k, 0, 0)),
                  pl.BlockSpec((1, SUBLANES, 2 * S5_LANES), lambda k, c: (k, 0, 0)),
                  pl.BlockSpec((1, 2 * S5_LANES, 2 * LANES), lambda k, c: (k, 0, 0)),
                  pl.BlockSpec((1, SUBLANES, LANES), lambda k, c: (k, 0, 0))],
        out_specs=pl.BlockSpec((1, rows, LANES), lambda k, c: (k, c, 0)),
        out_shape=jax.ShapeDtypeStruct(u6.shape, F32),
        scratch_shapes=[pltpu.VMEM((2 * S5_LANES // LANES, rows, LANES), F32),
                        pltpu.VMEM((SUBLANES, 2 * S5_LANES), F32)],
        compiler_params=_cparams(("arbitrary", "arbitrary")),
        name="s5",
    )(u6, wb, ab, wc, dd)


def _glu_kernel(y_ref, w_ref, o_ref):
    b = pl.program_id(1)
    tt = o_ref.shape[1]
    y = jnp.concatenate([y_ref.at[k][pl.ds(b * 2 + s, tt, stride=SUBLANES), :]
                         for k in range(S5_PAIRS) for s in range(2)], axis=1)
    g = jnp.dot(y.astype(BF16), w_ref[...], preferred_element_type=F32)
    o_ref[0] = (y * jax.nn.sigmoid(g)).astype(o_ref.dtype)


def _glu(y6, w_glu, bn, l):
    tt = 512
    assert bn * 2 == SUBLANES
    return pl.pallas_call(
        _glu_kernel,
        grid=(l // tt, bn),
        in_specs=[pl.BlockSpec((S5_PAIRS, tt * SUBLANES, LANES), lambda i, b: (0, i, 0)),
                  pl.BlockSpec((S5_CH, S5_CH), lambda i, b: (0, 0))],
        out_specs=pl.BlockSpec((1, tt, S5_CH), lambda i, b: (b, i, 0)),
        out_shape=jax.ShapeDtypeStruct((bn, l, S5_CH), BF16),
        compiler_params=_cparams(("arbitrary", "arbitrary")),
        name="glu",
    )(y6, w_glu)


def _merge_kernel(att_ref, z_ref, gm_ref, gs_ref, x_ref, mod_ref, gpost_ref, gpre_ref,
                  wbm_ref, wbs_ref, wo_ref, wrh_ref, wrl_ref, x1_ref, h2b_ref, h2s_ref, lg_ref):
    tm = x_ref.shape[1]
    th = tm // MERGE_SUBTILES
    for r0 in range(0, tm, th):
        rows = slice(r0, r0 + th)
        ym = jnp.dot(att_ref[rows, :], wbm_ref[...], preferred_element_type=F32)
        ys = jnp.dot(z_ref[rows, :], wbs_ref[...], preferred_element_type=F32)
        mixed_in = (gm_ref[rows, :].astype(F32) * ym + gs_ref[rows, :].astype(F32) * ys).astype(BF16)
        mixed = jnp.dot(mixed_in, wo_ref[...], preferred_element_type=F32)
        x1 = x_ref[0, rows, :] + mod_ref[0, 2:3, :] * _rms(mixed, gpost_ref[...])
        x1_ref[0, rows, :] = x1
        h2 = _rms(x1, gpre_ref[...]) * (1.0 + mod_ref[0, 4:5, :]) + mod_ref[0, 3:4, :]
        h2_hi = h2.astype(BF16)
        h2b_ref[0, rows, :] = h2_hi
        for s in range(SLAB):
            h2s_ref[pl.ds(r0 * SLAB + s, th, stride=SLAB), :] = h2[:, s * LANES:(s + 1) * LANES]
        h2_lo = (h2 - h2_hi.astype(F32)).astype(BF16)
        lg_ref[:, rows] = (lax.dot_general(wrh_ref[...], h2_hi, NT_DIMS, preferred_element_type=F32)
                           + lax.dot_general(wrl_ref[...], h2_hi, NT_DIMS, preferred_element_type=F32)
                           + lax.dot_general(wrh_ref[...], h2_lo, NT_DIMS, preferred_element_type=F32))


def _merge(att, z_tm, gates, x, mod, g_post, g_pre_ffn, w_br_mla, w_br_s5, w_out, wr_hi, wr_lo):
    bn, l, d = x.shape
    tm = 256
    nl = l // tm
    row = lambda b, i: (b * nl + i, 0)
    const = lambda b, i: (0, 0)
    one = pl.Buffered(1)
    return pl.pallas_call(
        _merge_kernel,
        grid=(bn, nl),
        in_specs=[pl.BlockSpec((tm, MLA_HEADS * V_HEAD), row),
                  pl.BlockSpec((tm, S5_CH), row),
                  pl.BlockSpec((tm, d), lambda b, i: (b * nl + i, 0)),
                  pl.BlockSpec((tm, d), lambda b, i: (b * nl + i, 1)),
                  pl.BlockSpec((1, tm, d), lambda b, i: (b, i, 0)),
                  pl.BlockSpec((1, 6, d), lambda b, i: (b, 0, 0)),
                  pl.BlockSpec((1, d), const),
                  pl.BlockSpec((1, d), const),
                  pl.BlockSpec((MLA_HEADS * V_HEAD, d), const, pipeline_mode=one),
                  pl.BlockSpec((S5_CH, d), const, pipeline_mode=one),
                  pl.BlockSpec((d, d), const, pipeline_mode=one),
                  pl.BlockSpec((N_EXPERTS, d), const, pipeline_mode=one),
                  pl.BlockSpec((N_EXPERTS, d), const, pipeline_mode=one)],
        out_specs=[pl.BlockSpec((1, tm, d), lambda b, i: (b, i, 0)),
                   pl.BlockSpec((1, tm, d), lambda b, i: (b, i, 0)),
                   pl.BlockSpec((tm * SLAB, LANES), row),
                   pl.BlockSpec((N_EXPERTS, tm), lambda b, i: (0, b * nl + i))],
        out_shape=[jax.ShapeDtypeStruct((bn, l, d), F32),
                   jax.ShapeDtypeStruct((bn, l, d), BF16),
                   jax.ShapeDtypeStruct((bn * l * SLAB, LANES), F32),
                   jax.ShapeDtypeStruct((N_EXPERTS, bn * l), F32)],
        compiler_params=_cparams(("arbitrary", "arbitrary")),
        name="merge",
    )(att, z_tm, gates, gates, x, mod, g_post.reshape(1, d), g_pre_ffn.reshape(1, d),
      w_br_mla, w_br_s5, w_out, wr_hi, wr_lo)


def _route_kernel(lg_ref, b_ref, ti_ref, tw_ref, cnt_ref):
    ng = N_EXPERT_GROUPS
    gsz = N_EXPERTS // ng
    lg = lg_ref[...]
    t = lg.shape[-1]
    sc = jax.nn.sigmoid(lg)
    sel = sc + b_ref[...]
    ninf = -jnp.inf
    i_in = lax.broadcasted_iota(jnp.int32, (ng, gsz, t), 1).astype(F32)
    m1 = jnp.max(sel, axis=1, keepdims=True)
    idx1 = jnp.min(jnp.where(sel == m1, i_in, float(gsz)), axis=1, keepdims=True)
    m2 = jnp.max(jnp.where(i_in == idx1, ninf, sel), axis=1, keepdims=True)
    gs = m1 + m2
    g_i = lax.broadcasted_iota(jnp.int32, (ng, 1, t), 0).astype(F32)
    picked = jnp.zeros((ng, 1, t), F32)
    cur = gs
    for _ in range(TOPK_GROUPS):
        m = jnp.max(cur, axis=0, keepdims=True)
        idx = jnp.min(jnp.where(cur == m, g_i, float(ng)), axis=0, keepdims=True)
        hit = g_i == idx
        picked = jnp.where(hit, 1.0, picked)
        cur = jnp.where(hit, ninf, cur)
    cand = jnp.where(picked > 0.5, sel, ninf)
    e_i = lax.broadcasted_iota(jnp.int32, (ng, gsz, t), 0).astype(F32) * float(gsz) + i_in
    ws = []
    hits = jnp.zeros((ng, gsz, t), F32)
    for r in range(TOP_K):
        m = jnp.max(jnp.max(cand, axis=1, keepdims=True), axis=0, keepdims=True)
        idx = jnp.min(jnp.min(jnp.where(cand == m, e_i, float(N_EXPERTS)), axis=1, keepdims=True),
                      axis=0, keepdims=True)
        hit = e_i == idx
        w = jnp.sum(jnp.sum(jnp.where(hit, sc, 0.0), axis=1, keepdims=True), axis=0, keepdims=True)
        ti_ref[r:r + 1, :] = idx[0].astype(jnp.int32)
        ws.append(w[0])
        hits = jnp.where(hit, 1.0, hits)
        cand = jnp.where(hit, ninf, cand)

    @pl.when(pl.program_id(0) == 0)
    def _():
        cnt_ref[...] = jnp.zeros(cnt_ref.shape, F32)

    cnt_ref[...] += jnp.sum(hits, axis=2, keepdims=True)
    tot = ws[0]
    for w in ws[1:]:
        tot = tot + w
    for r in range(TOP_K):
        tw_ref[r:r + 1, :] = ws[r] / tot * ROUTED_SCALE
    for r in range(TOP_K, SUBLANES):
        ti_ref[r:r + 1, :] = jnp.zeros((1, t), jnp.int32)
        tw_ref[r:r + 1, :] = jnp.zeros((1, t), F32)


def _route(logits_t, router_bias):
    n = logits_t.shape[1]
    ng = N_EXPERT_GROUPS
    gsz = N_EXPERTS // ng
    tn = 1024
    return pl.pallas_call(
        _route_kernel,
        grid=(n // tn,),
        in_specs=[pl.BlockSpec((ng, gsz, tn), lambda i: (0, 0, i)),
                  pl.BlockSpec((ng, gsz, 1), lambda i: (0, 0, 0))],
        out_specs=[pl.BlockSpec((SUBLANES, tn), lambda i: (0, i)),
                   pl.BlockSpec((SUBLANES, tn), lambda i: (0, i)),
                   pl.BlockSpec((ng, gsz, 1), lambda i: (0, 0, 0))],
        out_shape=[jax.ShapeDtypeStruct((SUBLANES, n), jnp.int32),
                   jax.ShapeDtypeStruct((SUBLANES, n), F32),
                   jax.ShapeDtypeStruct((ng, gsz, 1), F32)],
        compiler_params=_cparams(("arbitrary",)),
        name="route",
    )(logits_t.reshape(ng, gsz, n), router_bias.reshape(ng, gsz, 1))


def _moe_kernel(blk_e_ref, nused_ref, nvalid_ref, nxt_e_ref, wslot_ref, t_ref, t1_ref, t2_ref, a_ref, h_hbm,
                wg_hbm, wu_hbm, wd_hbm, y_hbm, xbuf, ybuf, wg_buf, wu_buf, wd_buf, wg_s, wu_s, wd_s,
                gsem, wsem, wtsem):
    j = pl.program_id(0)
    n_used = nused_ref[0]
    slot = lax.rem(j, GATHER_SLOTS)
    nv = nvalid_ref[j]
    nv_prev = nvalid_ref[jnp.maximum(j - 1, 0)]

    def slab(ref, row0, align=SUBLANES):
        return ref.at[pl.ds(pl.multiple_of(row0, align), SLAB), :]

    def start_gather(idx_ref, s):
        def body(r, _):
            pltpu.make_async_copy(slab(h_hbm, idx_ref[0, 0, r]), slab(xbuf.at[s], r * MOE_PITCH, PITCH_ALIGN),
                                  gsem.at[s]).start()
            return 0
        lax.fori_loop(0, MOE_TB, body, 0, unroll=8)

    def wait_gather(s):
        rows = MOE_TB * SLAB
        pltpu.make_async_copy(h_hbm.at[pl.ds(0, rows), :], xbuf.at[s, pl.ds(0, rows), :], gsem.at[s]).wait()

    def weight_copies(expert, s):
        return (pltpu.make_async_copy(wg_hbm.at[expert], wg_buf.at[s], wtsem.at[s]),
                pltpu.make_async_copy(wu_hbm.at[expert], wu_buf.at[s], wtsem.at[s]),
                pltpu.make_async_copy(wd_hbm.at[expert], wd_buf.at[s], wtsem.at[s]))

    def wait_writes(count):
        p = MOE_TB
        while p >= 1:
            @pl.when((count & p) != 0)
            def _(p=p):
                rows = p * SLAB
                pltpu.make_async_copy(ybuf.at[pl.ds(0, rows), :], y_hbm.at[pl.ds(0, rows), :], wsem.at[0]).wait()
            p //= 2

    @pl.when(j < n_used)
    def _():
        @pl.when(j == 0)
        def _():
            start_gather(t_ref, 0)

            @pl.when(n_used > 1)
            def _():
                start_gather(t1_ref, 1)

        e = blk_e_ref[j]
        ws = wslot_ref[j]

        @pl.when(j == 0)
        def _():
            for cp in weight_copies(e, ws):
                cp.start(priority=WEIGHT_DMA_PRIORITY)

        first = jnp.logical_or(j == 0, e != blk_e_ref[jnp.maximum(j - 1, 0)])

        @pl.when(first)
        def _():
            for cp in weight_copies(e, ws):
                cp.wait()
            wg_s[...] = wg_buf[ws].astype(BF16)
            wu_s[...] = wu_buf[ws].astype(BF16)
            wd_s[...] = wd_buf[ws].astype(BF16)

        wait_gather(slot)

        @pl.when(j + 2 < n_used)
        def _():
            start_gather(t2_ref, lax.rem(j + 2, GATHER_SLOTS))

        @pl.when(jnp.logical_and(first, nxt_e_ref[j] >= 0))
        def _():
            for cp in weight_copies(nxt_e_ref[j], 1 - ws):
                cp.start(priority=WEIGHT_DMA_PRIORITY)

        xs = xbuf.at[slot]
        x = jnp.concatenate([xs[pl.ds(s, MOE_TB, stride=MOE_PITCH), :] for s in range(SLAB)], axis=1).astype(BF16)
        g = jnp.dot(x, wg_s[...], preferred_element_type=F32)
        u = jnp.dot(x, wu_s[...], preferred_element_type=F32)
        hm = (g * jax.nn.sigmoid(g) * u).astype(BF16)
        y = jnp.dot(hm, wd_s[...], preferred_element_type=F32)

        @pl.when(j > 0)
        def _():
            wait_writes(nv_prev)

        for s in range(SLAB):
            ybuf[pl.ds(s, MOE_TB, stride=MOE_PITCH), :] = y[:, s * LANES:(s + 1) * LANES]

        def start_write(r):
            pltpu.make_async_copy(slab(ybuf, r * MOE_PITCH, PITCH_ALIGN), slab(y_hbm, a_ref[0, 0, r]),
                                  wsem.at[0]).start()

        def wgroup(g8, _):
            for q in range(SUBLANES):
                start_write(g8 * SUBLANES + q)
            return 0

        def wtail(r, _):
            start_write(r)
            return 0

        n_groups = lax.shift_right_logical(nv, 3)
        lax.fori_loop(0, n_groups, wgroup, 0)
        lax.fori_loop(n_groups * SUBLANES, nv, wtail, 0)

        @pl.when(j == n_used - 1)
        def _():
            wait_writes(nv)


def _moe(h2s, topi, counts, w_gate, w_up, w_down):
    n = h2s.shape[0] // SLAB
    d = D_MODEL
    nk = n * TOP_K
    tb = MOE_TB
    n_blocks = nk // tb + N_EXPERTS
    flat_e = topi[:TOP_K].reshape(nk)
    _, order = lax.sort_key_val(flat_e, jnp.arange(nk, dtype=jnp.int32))
    start = jnp.cumsum(counts) - counts
    padded = (counts + tb - 1) // tb * tb
    pad_end = jnp.cumsum(padded)
    pad_start = pad_end - padded
    n_used = (pad_end[-1] // tb).astype(jnp.int32).reshape(1)
    blk_p0 = jnp.arange(n_blocks, dtype=jnp.int32) * tb
    blk_e = jnp.minimum(jnp.sum((pad_end[None, :] <= blk_p0[:, None]).astype(jnp.int32), axis=1), N_EXPERTS - 1)
    blk_off = blk_p0 - pad_start[blk_e]
    nvalid = jnp.clip(counts[blk_e] - blk_off, 0, tb).astype(jnp.int32)
    rank = (start[blk_e] + blk_off)[:, None] + jnp.arange(tb, dtype=jnp.int32)[None, :]
    a3 = order[jnp.clip(rank, 0, nk - 1)].reshape(n_blocks, 1, tb)
    arow = a3 * SLAB
    trow = (a3 % n) * SLAB
    used = counts > 0
    e_ids = jnp.arange(N_EXPERTS, dtype=jnp.int32)
    later = jnp.where(used, e_ids, N_EXPERTS)
    nxt = jnp.concatenate([lax.cummin(later[::-1])[::-1][1:], jnp.full((1,), N_EXPERTS, jnp.int32)])
    nxt_e = jnp.where(nxt < N_EXPERTS, nxt, -1)[blk_e].astype(jnp.int32)
    wslot = ((jnp.cumsum(used.astype(jnp.int32)) - 1) % 2)[blk_e].astype(jnp.int32)

    smem_blk = lambda f: pl.BlockSpec((1, 1, tb), f, memory_space=pltpu.SMEM)
    any_spec = pl.BlockSpec(memory_space=pl.ANY)
    grid_spec = pltpu.PrefetchScalarGridSpec(
        num_scalar_prefetch=5,
        grid=(n_blocks,),
        in_specs=[smem_blk(lambda j, *_: (j, 0, 0)),
                  smem_blk(lambda j, *_: (jnp.minimum(j + 1, n_blocks - 1), 0, 0)),
                  smem_blk(lambda j, *_: (jnp.minimum(j + 2, n_blocks - 1), 0, 0)),
                  smem_blk(lambda j, *_: (j, 0, 0)),
                  any_spec, any_spec, any_spec, any_spec],
        out_specs=any_spec,
        scratch_shapes=[pltpu.VMEM((GATHER_SLOTS, tb * MOE_PITCH, LANES), F32),
                        pltpu.VMEM((tb * MOE_PITCH, LANES), F32),
                        pltpu.VMEM((2, d, D_EXPERT), F32),
                        pltpu.VMEM((2, d, D_EXPERT), F32),
                        pltpu.VMEM((2, D_EXPERT, d), F32),
                        pltpu.VMEM((d, D_EXPERT), BF16),
                        pltpu.VMEM((d, D_EXPERT), BF16),
                        pltpu.VMEM((D_EXPERT, d), BF16),
                        pltpu.SemaphoreType.DMA((GATHER_SLOTS,)),
                        pltpu.SemaphoreType.DMA((1,)),
                        pltpu.SemaphoreType.DMA((2,))],
    )
    return pl.pallas_call(
        _moe_kernel,
        grid_spec=grid_spec,
        out_shape=jax.ShapeDtypeStruct((TOP_K * n * SLAB, LANES), F32),
        compiler_params=_cparams(("arbitrary",)),
        name="moe",
    )(blk_e, n_used, nvalid, nxt_e, wslot, trow, trow, trow, arow, h2s, w_gate, w_up, w_down)


def _final_kernel(h_ref, y0, y1, y2, y3, y4, y5, tw_ref, x1_ref, mod_ref, g_ref, wg_ref, wu_ref, wd_ref, o_ref):
    h = h_ref[...]
    tm = h.shape[0]
    g = jnp.dot(h, wg_ref[...], preferred_element_type=F32)
    u = jnp.dot(h, wu_ref[...], preferred_element_type=F32)
    hm = (g * jax.nn.sigmoid(g) * u).astype(BF16)
    ffn = jnp.dot(hm, wd_ref[...], preferred_element_type=F32)
    tw = tw_ref[...]
    for k, y in enumerate((y0, y1, y2, y3, y4, y5)):
        yk = jnp.concatenate([y[pl.ds(s, tm, stride=SLAB), :] for s in range(SLAB)], axis=1)
        ffn = ffn + yk * tw[:, k:k + 1]
    o_ref[0] = x1_ref[0] + mod_ref[0, 5:6, :] * _rms(ffn, g_ref[...])


def _final(h2, y_flat, topw_t, x1, mod, g_post_ffn, w_sg, w_su, w_sd):
    bn, l, d = x1.shape
    n = bn * l
    tm = 256
    nl = l // tm
    const = lambda b, i: (0, 0)
    one = pl.Buffered(1)
    y_specs = [pl.BlockSpec((tm * SLAB, LANES), functools.partial(lambda b, i, k: (k * (n // tm) + b * nl + i, 0), k=k))
               for k in range(TOP_K)]
    return pl.pallas_call(
        _final_kernel,
        grid=(bn, nl),
        in_specs=[pl.BlockSpec((tm, d), lambda b, i: (b * nl + i, 0))] + y_specs + [
            pl.BlockSpec((tm, SUBLANES), lambda b, i: (b * nl + i, 0)),
            pl.BlockSpec((1, tm, d), lambda b, i: (b, i, 0)),
            pl.BlockSpec((1, 6, d), lambda b, i: (b, 0, 0)),
            pl.BlockSpec((1, d), const),
            pl.BlockSpec((d, D_EXPERT), const, pipeline_mode=one),
            pl.BlockSpec((d, D_EXPERT), const, pipeline_mode=one),
            pl.BlockSpec((D_EXPERT, d), const, pipeline_mode=one)],
        out_specs=pl.BlockSpec((1, tm, d), lambda b, i: (b, i, 0)),
        out_shape=jax.ShapeDtypeStruct((bn, l, d), F32),
        compiler_params=_cparams(("arbitrary", "arbitrary")),
        name="final",
    )(h2, *([y_flat] * TOP_K), topw_t, x1, mod, g_post_ffn.reshape(1, d), w_sg, w_su, w_sd)


def _layer(li, x, c, positions, w_ada, b_ada, g_pre_mix, g_post_mix, g_pre_ffn, g_post_ffn, w_in, g_q, g_kv,
           w_uq, w_uk, w_uv, a_re, a_im, log_dt, b_re, b_im, c_re, c_im, d_skip, w_glu, w_br_mla, w_br_s5,
           w_out, w_router, router_bias, w_exp_gate, w_exp_up, w_exp_down, w_sh_gate, w_sh_up, w_sh_down):
    bn, l, d = x.shape
    n = bn * l
    mod = _adaln(c, w_ada, b_ada).reshape(bn, 6, d)

    o_kpe = Q_LORA + KV_LORA
    o_u = o_kpe + QK_ROPE
    o_g = o_u + S5_CH
    w_in_t = jnp.swapaxes(w_in, 1, 2)
    w_q = jnp.pad(w_uq.reshape(Q_LORA, MLA_HEADS, QK_NOPE + QK_ROPE),
                  ((0, 0), (0, 0), (0, QK_PAD - QK_NOPE - QK_ROPE))).reshape(Q_LORA, MLA_HEADS * QK_PAD).astype(BF16)
    wr_t = w_router.T
    wr_hi = wr_t.astype(BF16)
    wr_lo = (wr_t - wr_hi.astype(F32)).astype(BF16)

    rope_c, rope_s1, rope_s2 = _rope_tables(positions)
    chunk_id = positions // CHUNK

    h, lat = _prenorm_lat(x, mod, g_pre_mix, 0, 1, w_in_t, li, o_kpe + LANES)
    h = h.reshape(n, d)
    gates = _mmt(h, w_in_t, li, o_g, 2 * d, BF16, 1024, 1024, act="sigmoid", name="mm_gates")
    u6 = _mm_u(h, w_in_t, li, o_u, bn, l)

    q, k, v = _qkvproj(lat, g_q, g_kv, w_q, w_uk.astype(BF16), w_uv.astype(BF16), rope_c, rope_s1, rope_s2)
    att = _attention(q, k, v, chunk_id, bn, l)

    wb, ab, wc, dd = _s5_params(a_re, a_im, log_dt, b_re, b_im, c_re, c_im, d_skip)
    y6 = _s5(u6, wb, ab, wc, dd, l)
    z = _glu(y6, w_glu.astype(BF16), bn, l).reshape(n, S5_CH)

    x1, h2b, h2s, logits_t = _merge(att, z, gates, x, mod, g_post_mix, g_pre_ffn, w_br_mla.astype(BF16),
                                    w_br_s5.astype(BF16), w_out.astype(BF16), wr_hi, wr_lo)

    topi, topw, cnt = _route(logits_t, router_bias)
    y_slabs = _moe(h2s, topi, cnt.reshape(N_EXPERTS).astype(jnp.int32), w_exp_gate, w_exp_up, w_exp_down)
    return _final(h2b.reshape(n, d), y_slabs, topw.T, x1, mod, g_post_ffn, w_sh_gate.astype(BF16),
                  w_sh_up.astype(BF16), w_sh_down.astype(BF16))


def kernel(x, c, positions, w_ada, b_ada, g_pre_mix, g_post_mix, g_pre_ffn, g_post_ffn, w_in, g_q, g_kv, w_uq, w_uk, w_uv, a_re, a_im, log_dt, b_re, b_im, c_re, c_im, d_skip, w_glu, w_br_mla, w_br_s5, w_out, w_router, router_bias, w_exp_gate, w_exp_up, w_exp_down, w_sh_gate, w_sh_up, w_sh_down):
    depth = w_ada.shape[0]
    for li in range(depth):
        x = _layer(li, x, c, positions, w_ada[li], b_ada[li], g_pre_mix[li], g_post_mix[li], g_pre_ffn[li],
                   g_post_ffn[li], w_in, g_q[li], g_kv[li], w_uq[li], w_uk[li], w_uv[li], a_re[li], a_im[li],
                   log_dt[li], b_re[li], b_im[li], c_re[li], c_im[li], d_skip[li], w_glu[li], w_br_mla[li],
                   w_br_s5[li], w_out[li], w_router[li], router_bias[li], w_exp_gate[li], w_exp_up[li],
                   w_exp_down[li], w_sh_gate[li], w_sh_up[li], w_sh_down[li])
    return x
```

```python
import functools

import jax
import jax.numpy as jnp
from jax import lax
from jax.experimental import pallas as pl
from jax.experimental.pallas import tpu as pltpu

F32 = jnp.float32
BF16 = jnp.bfloat16

D_MODEL = 2048
CHUNK = 64
EPS = 1e-6
MLA_HEADS = 8
QK_NOPE = 128
QK_ROPE = 64
V_HEAD = 128
Q_LORA = 512
KV_LORA = 512
ROPE_THETA = 10000.0
S5_CH = 1024
S5_GROUP = 16
S5_GROUPS = S5_CH // S5_GROUP
S5_STATE = 64
N_EXPERTS = 64
TOP_K = 6
N_EXPERT_GROUPS = 8
TOPK_GROUPS = 4
D_EXPERT = 512
ROUTED_SCALE = 2.5

LANES = 128
SUBLANES = 8
QK_PAD = 2 * LANES
VMEM_LIMIT = 56 * 1024 * 1024
NEG = -1e30
LOG2E = 1.4426950408889634

S5_GB = LANES // S5_GROUP
S5_NBLK = S5_CH // LANES
S5_PAIRS = S5_NBLK // 2
S5_LANES = S5_GB * S5_STATE
S5_PPS = 2
S5_TC = 256
S5_RB = 512

ATT_TQ = 256
ATT_TK = 256
MOE_TB = 256
SLAB = D_MODEL // LANES
MERGE_SUBTILES = 2
GATHER_SLOTS = 3
WEIGHT_DMA_PRIORITY = 1
MOE_PITCH = 20
PITCH_ALIGN = 4


def _cparams(sem):
    return pltpu.CompilerParams(dimension_semantics=sem, vmem_limit_bytes=VMEM_LIMIT)


def _rms(x, g):
    return x * lax.rsqrt(jnp.mean(x * x, axis=-1, keepdims=True) + EPS) * g


def _adaln_kernel(c_ref, w_ref, b_ref, o_ref):
    c = c_ref[...]
    a = (c * jax.nn.sigmoid(c)).astype(BF16)
    o_ref[...] = jnp.dot(a, w_ref[...].astype(BF16), preferred_element_type=F32) + b_ref[...]


def _adaln(c, w, b):
    bn, d = c.shape
    n = w.shape[1]
    tn = 1024
    return pl.pallas_call(
        _adaln_kernel,
        grid=(n // tn,),
        in_specs=[pl.BlockSpec((bn, d), lambda j: (0, 0)),
                  pl.BlockSpec((d, tn), lambda j: (0, j)),
                  pl.BlockSpec((1, tn), lambda j: (0, j))],
        out_specs=pl.BlockSpec((bn, tn), lambda j: (0, j)),
        out_shape=jax.ShapeDtypeStruct((bn, n), F32),
        compiler_params=_cparams(("arbitrary",)),
        name="adaln",
    )(c, w, b.reshape(1, n))


NT_DIMS = (((1,), (1,)), ((), ()))


def _prenorm_lat_kernel(x_ref, mod_ref, g_ref, w_ref, h_ref, lat_ref, w_scr, *, sh_row, sc_row):
    @pl.when(jnp.logical_and(pl.program_id(0) == 0, pl.program_id(1) == 0))
    def _():
        w_scr[...] = w_ref[...].astype(BF16)

    y = _rms(x_ref[0], g_ref[...])
    h = (y * (1.0 + mod_ref[0, sc_row:sc_row + 1, :]) + mod_ref[0, sh_row:sh_row + 1, :]).astype(BF16)
    h_ref[0] = h
    lat_ref[...] = lax.dot_general(h, w_scr[...], NT_DIMS, preferred_element_type=F32)


def _prenorm_lat(x, mod, g, sh_row, sc_row, w_t, layer, n_lat):
    bn, l, d = x.shape
    tl = 512
    nl = l // tl
    return pl.pallas_call(
        functools.partial(_prenorm_lat_kernel, sh_row=sh_row, sc_row=sc_row),
        grid=(bn, nl),
        in_specs=[pl.BlockSpec((1, tl, d), lambda b, i: (b, i, 0)),
                  pl.BlockSpec((1, 6, d), lambda b, i: (b, 0, 0)),
                  pl.BlockSpec((1, d), lambda b, i: (0, 0)),
                  pl.BlockSpec((None, n_lat, d), lambda b, i: (layer, 0, 0), pipeline_mode=pl.Buffered(1))],
        out_specs=[pl.BlockSpec((1, tl, d), lambda b, i: (b, i, 0)),
                   pl.BlockSpec((tl, n_lat), lambda b, i: (b * nl + i, 0))],
        out_shape=[jax.ShapeDtypeStruct((bn, l, d), BF16),
                   jax.ShapeDtypeStruct((bn * l, n_lat), F32)],
        scratch_shapes=[pltpu.VMEM((n_lat, d), BF16)],
        compiler_params=_cparams(("arbitrary", "arbitrary")),
        name="prenorm_lat",
    )(x, mod, g.reshape(1, d), w_t)


def _wt_tile(wa_ref, wb_ref, shift, tn):
    if shift == 0:
        return wa_ref[...].astype(BF16)
    return jnp.concatenate([wa_ref[...], wb_ref[...]], axis=0)[shift:shift + tn].astype(BF16)


def _wt_specs(layer, row0, tn, k, jmap):
    shift = row0 % LANES
    c0 = row0 - shift
    assert c0 % tn == 0 and shift % SUBLANES == 0
    specs = [pl.BlockSpec((None, tn, k), lambda *g: (layer, jmap(*g) + c0 // tn, 0))]
    if shift:
        specs.append(pl.BlockSpec((None, LANES, k), lambda *g: (layer, (c0 + (jmap(*g) + 1) * tn) // LANES, 0)))
    return specs, shift


def _mmt_kernel(a_ref, wa_ref, *rest, act, shift):
    wb_ref = rest[0] if shift else None
    o_ref, w_scr = rest[-2:]

    @pl.when(pl.program_id(1) == 0)
    def _():
        w_scr[...] = _wt_tile(wa_ref, wb_ref, shift, w_scr.shape[0])

    acc = lax.dot_general(a_ref[...], w_scr[...], NT_DIMS, preferred_element_type=F32)
    if act == "sigmoid":
        acc = jax.nn.sigmoid(acc)
    o_ref[...] = acc.astype(o_ref.dtype)


def _mmt(a, w_t, layer, row0, n, out_dtype, tm, tn, act=None, name="mmt"):
    m, k = a.shape
    w_specs, shift = _wt_specs(layer, row0, tn, k, lambda j, i: j)
    return pl.pallas_call(
        functools.partial(_mmt_kernel, act=act, shift=shift),
        grid=(n // tn, m // tm),
        in_specs=[pl.BlockSpec((tm, k), lambda j, i: (i, 0))] + w_specs,
        out_specs=pl.BlockSpec((tm, tn), lambda j, i: (i, j)),
        out_shape=jax.ShapeDtypeStruct((m, n), out_dtype),
        scratch_shapes=[pltpu.VMEM((tn, k), BF16)],
        compiler_params=_cparams(("arbitrary", "arbitrary")),
        name=name,
    )(a, *([w_t] * len(w_specs)))


def _mm_u_kernel(a_ref, wa_ref, *rest, shift):
    wb_ref = rest[0] if shift else None
    o_ref, w_scr = rest[-2:]
    b = pl.program_id(1)

    @pl.when(jnp.logical_and(pl.program_id(0) == 0, b == 0))
    def _():
        w_scr[...] = _wt_tile(wa_ref, wb_ref, shift, w_scr.shape[0])

    res = lax.dot_general(a_ref[...], w_scr[...], NT_DIMS, preferred_element_type=F32)
    tm = res.shape[0]
    for c in range(S5_NBLK):
        o_ref.at[c // 2][pl.ds(b * 2 + c % 2, tm, stride=SUBLANES), :] = res[:, c * LANES:(c + 1) * LANES]


def _mm_u(h, w_t, layer, row0, bn, l):
    m, k = h.shape
    tm = 512
    nl = l // tm
    assert bn * 2 == SUBLANES
    w_specs, shift = _wt_specs(layer, row0, S5_CH, k, lambda i, b: 0)
    return pl.pallas_call(
        functools.partial(_mm_u_kernel, shift=shift),
        grid=(nl, bn),
        in_specs=[pl.BlockSpec((tm, k), lambda i, b: (b * nl + i, 0))] + w_specs,
        out_specs=pl.BlockSpec((S5_PAIRS, tm * SUBLANES, LANES), lambda i, b: (0, i, 0)),
        out_shape=jax.ShapeDtypeStruct((S5_PAIRS, l * SUBLANES, LANES), F32),
        scratch_shapes=[pltpu.VMEM((S5_CH, k), BF16)],
        compiler_params=_cparams(("arbitrary", "arbitrary")),
        name="mm_u",
    )(h, *([w_t] * len(w_specs)))


def _rope_tab_kernel(pos_ref, k_ref, c_ref, s1_ref, s2_ref):
    ang = pos_ref[...].astype(F32) * k_ref[0:1, :]
    s = jnp.sin(ang)
    c_ref[...] = jnp.cos(ang) * k_ref[1:2, :]
    s1_ref[...] = s * k_ref[2:3, :]
    s2_ref[...] = s * k_ref[3:4, :]


def _rope_tables(positions):
    n = positions.size
    half = QK_ROPE // 2
    inv_freq = ROPE_THETA ** (-jnp.arange(half, dtype=F32) / half)
    zh, oh = jnp.zeros((half,), F32), jnp.ones((half,), F32)
    z2 = jnp.zeros((LANES - QK_ROPE,), F32)
    rows = [jnp.concatenate([inv_freq, inv_freq, z2]), jnp.concatenate([oh, oh, z2]),
            jnp.concatenate([-oh, zh, z2]), jnp.concatenate([zh, oh, z2])]
    consts = jnp.stack(rows + [jnp.zeros((LANES,), F32)] * (SUBLANES - len(rows)))
    tm = 1024
    tab = jax.ShapeDtypeStruct((n, LANES), F32)
    return pl.pallas_call(
        _rope_tab_kernel,
        grid=(n // tm,),
        in_specs=[pl.BlockSpec((tm, 1), lambda i: (i, 0)),
                  pl.BlockSpec((SUBLANES, LANES), lambda i: (0, 0))],
        out_specs=[pl.BlockSpec((tm, LANES), lambda i: (i, 0))] * 3,
        out_shape=[tab, tab, tab],
        compiler_params=_cparams(("arbitrary",)),
        name="rope_tables",
    )(positions.reshape(n, 1), consts)


def _rope_tile(t, c_ref, s1_ref, s2_ref):
    return (t * c_ref[...] + pltpu.roll(t, LANES - QK_ROPE // 2, 1) * s1_ref[...]
            + pltpu.roll(t, QK_ROPE // 2, 1) * s2_ref[...])


def _qkvproj_kernel(lat_ref, gq_ref, gkv_ref, wq_ref, wk_ref, wv_ref, kpe_ref, c_ref, s1_ref, s2_ref,
                    q_ref, k_ref, v_ref, *, scale):
    lat = lat_ref[...]
    qn = _rms(lat[:, :Q_LORA], gq_ref[...]).astype(BF16)
    cn = _rms(lat[:, Q_LORA:], gkv_ref[...]).astype(BF16)
    q = jnp.dot(qn, wq_ref[...], preferred_element_type=F32)
    kn = jnp.dot(cn, wk_ref[...], preferred_element_type=F32)
    v_ref[...] = jnp.dot(cn, wv_ref[...], preferred_element_type=F32).astype(v_ref.dtype)
    kt = _rope_tile(kpe_ref[...], c_ref, s1_ref, s2_ref).astype(k_ref.dtype)
    for h in range(MLA_HEADS):
        o = h * QK_PAD
        q_ref[:, o:o + LANES] = (q[:, o:o + LANES] * scale).astype(q_ref.dtype)
        qt = _rope_tile(q[:, o + LANES:o + QK_PAD], c_ref, s1_ref, s2_ref)
        q_ref[:, o + LANES:o + QK_PAD] = (qt * scale).astype(q_ref.dtype)
        k_ref[:, o:o + LANES] = kn[:, h * QK_NOPE:(h + 1) * QK_NOPE].astype(k_ref.dtype)
        k_ref[:, o + LANES:o + QK_PAD] = kt


def _qkvproj(lat, g_q, g_kv, w_q, w_k, w_v, rope_c, rope_s1, rope_s2):
    n = lat.shape[0]
    tm = 512
    row = lambda i: (i, 0)
    const = lambda i: (0, 0)
    tab = pl.BlockSpec((tm, LANES), row)
    kpe_spec = pl.BlockSpec((tm, LANES), lambda i: (i, (Q_LORA + KV_LORA) // LANES))
    return pl.pallas_call(
        functools.partial(_qkvproj_kernel, scale=(QK_NOPE + QK_ROPE) ** -0.5 * LOG2E),
        grid=(n // tm,),
        in_specs=[pl.BlockSpec((tm, Q_LORA + KV_LORA), row),
                  pl.BlockSpec((1, Q_LORA), const),
                  pl.BlockSpec((1, KV_LORA), const),
                  pl.BlockSpec((Q_LORA, MLA_HEADS * QK_PAD), const),
                  pl.BlockSpec((KV_LORA, MLA_HEADS * QK_NOPE), const),
                  pl.BlockSpec((KV_LORA, MLA_HEADS * V_HEAD), const),
                  kpe_spec, tab, tab, tab],
        out_specs=[pl.BlockSpec((tm, MLA_HEADS * QK_PAD), row),
                   pl.BlockSpec((tm, MLA_HEADS * QK_PAD), row),
                   pl.BlockSpec((tm, MLA_HEADS * V_HEAD), row)],
        out_shape=[jax.ShapeDtypeStruct((n, MLA_HEADS * QK_PAD), BF16),
                   jax.ShapeDtypeStruct((n, MLA_HEADS * QK_PAD), BF16),
                   jax.ShapeDtypeStruct((n, MLA_HEADS * V_HEAD), BF16)],
        compiler_params=_cparams(("arbitrary",)),
        name="qkvproj",
    )(lat, g_q.reshape(1, Q_LORA), g_kv.reshape(1, KV_LORA), w_q, w_k, w_v, lat, rope_c, rope_s1, rope_s2)


def _attn_kernel(lo_ref, hi_ref, q_ref, k_ref, v_ref, qc_ref, kc_ref, o_ref, m_scr, l_scr, acc_scr, *, nq):
    b = pl.program_id(0)
    i = pl.program_id(1)
    qc = qc_ref[...]
    m_scr[...] = jnp.full(m_scr.shape, NEG, F32)
    l_scr[...] = jnp.zeros(l_scr.shape, F32)
    acc_scr[...] = jnp.zeros(acc_scr.shape, F32)

    def make_body(masked):
        def body(j, _):
            off = pl.multiple_of(j * ATT_TK, ATT_TK)
            if masked:
                mask = kc_ref[j] <= qc
            for h in range(MLA_HEADS):
                q = q_ref[:, h * QK_PAD:(h + 1) * QK_PAD]
                k = k_ref[pl.ds(off, ATT_TK), h * QK_PAD:(h + 1) * QK_PAD]
                s = lax.dot_general(q, k, (((1,), (1,)), ((), ())), preferred_element_type=F32)
                if masked:
                    s = jnp.where(mask, s, NEG)
                m_old = m_scr[h]
                m_new = jnp.maximum(m_old, jnp.max(s, axis=-1, keepdims=True))
                p = jnp.exp2(s - jnp.concatenate([m_new] * (ATT_TK // LANES), axis=1))
                alpha = jnp.exp2(m_old - m_new)
                l_scr[h] = alpha * l_scr[h] + jnp.sum(p, axis=-1, keepdims=True)
                v = v_ref[pl.ds(off, ATT_TK), h * V_HEAD:(h + 1) * V_HEAD]
                acc_scr[h] = alpha * acc_scr[h] + jnp.dot(p.astype(BF16), v, preferred_element_type=F32)
                m_scr[h] = m_new
            return 0
        return body

    lo = lo_ref[b * nq + i]
    lax.fori_loop(0, lo, make_body(False), 0)
    lax.fori_loop(lo, hi_ref[b * nq + i], make_body(True), 0)
    for h in range(MLA_HEADS):
        o_ref[:, h * V_HEAD:(h + 1) * V_HEAD] = (acc_scr[h] / l_scr[h]).astype(o_ref.dtype)


def _attention(q, k, v, chunk_id, bn, l):
    nq = l // ATT_TQ
    nk = l // ATT_TK
    q_max = jnp.max(chunk_id.reshape(bn, nq, ATT_TQ), axis=-1)
    k_min = jnp.min(chunk_id.reshape(bn, nk, ATT_TK), axis=-1)
    needed = k_min[:, None, :] <= q_max[:, :, None]
    hi = jnp.max(jnp.where(needed, jnp.arange(1, nk + 1, dtype=jnp.int32), 0), axis=-1).reshape(bn * nq)
    q_min = jnp.min(chunk_id.reshape(bn, nq, ATT_TQ), axis=-1)
    k_max = jnp.max(chunk_id.reshape(bn, nk, ATT_TK), axis=-1)
    full = k_max[:, None, :] <= q_min[:, :, None]
    lo = jnp.min(jnp.where(full, nk, jnp.arange(nk, dtype=jnp.int32)), axis=-1).astype(jnp.int32).reshape(bn * nq)
    hi = jnp.maximum(hi, lo)
    qc = chunk_id.reshape(bn * l, 1)
    kc = chunk_id.reshape(bn * nk, 1, ATT_TK)
    grid_spec = pltpu.PrefetchScalarGridSpec(
        num_scalar_prefetch=2,
        grid=(bn, nq),
        in_specs=[pl.BlockSpec((ATT_TQ, MLA_HEADS * QK_PAD), lambda b, i, *_: (b * nq + i, 0)),
                  pl.BlockSpec((l, MLA_HEADS * QK_PAD), lambda b, i, *_: (b, 0)),
                  pl.BlockSpec((l, MLA_HEADS * V_HEAD), lambda b, i, *_: (b, 0)),
                  pl.BlockSpec((ATT_TQ, 1), lambda b, i, *_: (b * nq + i, 0)),
                  pl.BlockSpec((nk, 1, ATT_TK), lambda b, i, *_: (b, 0, 0))],
        out_specs=pl.BlockSpec((ATT_TQ, MLA_HEADS * V_HEAD), lambda b, i, *_: (b * nq + i, 0)),
        scratch_shapes=[pltpu.VMEM((MLA_HEADS, ATT_TQ, LANES), F32),
                        pltpu.VMEM((MLA_HEADS, ATT_TQ, LANES), F32),
                        pltpu.VMEM((MLA_HEADS, ATT_TQ, V_HEAD), F32)],
    )
    return pl.pallas_call(
        functools.partial(_attn_kernel, nq=nq),
        grid_spec=grid_spec,
        out_shape=jax.ShapeDtypeStruct((bn * l, MLA_HEADS * V_HEAD), BF16),
        compiler_params=_cparams(("arbitrary", "arbitrary")),
        name="attention",
    )(lo, hi, q, k, v, qc, kc)


def _s5_kernel(u_ref, wb_ref, a_ref, wc_ref, d_ref, o_ref, x_scr, st_scr):
    rows = S5_TC * SUBLANES
    nsub = rows // S5_RB

    @pl.when(pl.program_id(1) == 0)
    def _():
        st_scr[...] = jnp.zeros_like(st_scr)

    even = (lax.broadcasted_iota(jnp.int32, (S5_RB, 1), 0) & 1) == 0
    half = S5_RB // 2
    nre = S5_LANES // LANES
    nsl = 2 * nre

    def mm_in(r, _):
        off = pl.multiple_of(r * S5_RB, S5_RB)
        for p in range(S5_PPS):
            for s in range(2):
                us = u_ref.at[p][pl.ds(off + s, half, stride=2), :].astype(BF16)
                out = jnp.dot(us, wb_ref[p, :, s * 2 * S5_LANES:(s + 1) * 2 * S5_LANES],
                              preferred_element_type=F32)
                for c in range(nsl):
                    x_scr.at[p * nsl + c][pl.ds(off + s, half, stride=2), :] = out[:, c * LANES:(c + 1) * LANES]
        return 0

    lax.fori_loop(0, nsub, mm_in, 0)

    a_c = [a_ref[p, :, c * LANES:(c + 1) * LANES] for p in range(S5_PPS) for c in range(nsl)]

    def step(t, carry):
        off = pl.multiple_of(t * SUBLANES, SUBLANES)
        new = [None] * (S5_PPS * nsl)
        for p in range(S5_PPS):
            for c in range(nre):
                ir, ii = p * nsl + c, p * nsl + nre + c
                xr, xi = carry[ir], carry[ii]
                ar, ai = a_c[ir], a_c[ii]
                nr = ar * xr - ai * xi + x_scr[ir, pl.ds(off, SUBLANES), :]
                ni = ar * xi + ai * xr + x_scr[ii, pl.ds(off, SUBLANES), :]
                x_scr[ir, pl.ds(off, SUBLANES), :] = nr
                x_scr[ii, pl.ds(off, SUBLANES), :] = ni
                new[ir], new[ii] = nr, ni
        return tuple(new)

    init = tuple(st_scr[p, :, c * LANES:(c + 1) * LANES] for p in range(S5_PPS) for c in range(nsl))
    fin = lax.fori_loop(0, S5_TC, step, init, unroll=8)
    for p in range(S5_PPS):
        for c in range(nsl):
            st_scr[p, :, c * LANES:(c + 1) * LANES] = fin[p * nsl + c]

    def mm_out(r, _):
        off = pl.multiple_of(r * S5_RB, S5_RB)
        for p in range(S5_PPS):
            d = jnp.concatenate([d_ref[p]] * (S5_RB // SUBLANES), axis=0)
            x = jnp.concatenate([x_scr[p * nsl + c, pl.ds(off, S5_RB), :] for c in range(nsl)],
                                axis=1).astype(BF16)
            out = jnp.dot(x, wc_ref[p], preferred_element_type=F32)
            y = jnp.where(even, out[:, :LANES], out[:, LANES:]) + d * u_ref[p, pl.ds(off, S5_RB), :]
            o_ref[p, pl.ds(off, S5_RB), :] = jax.nn.gelu(y).astype(o_ref.dtype)
        return 0

    lax.fori_loop(0, nsub, mm_out, 0)


def _s5_params(a_re, a_im, log_dt, b_re, b_im, c_re, c_im, d_skip):
    step = jnp.exp(log_dt)[:, None]
    mag = jnp.exp(a_re * step)
    abar_re, abar_im = mag * jnp.cos(a_im * step), mag * jnp.sin(a_im * step)
    den = a_re * a_re + a_im * a_im
    nr, ni = abar_re - 1.0, abar_im
    f_re, f_im = (nr * a_re + ni * a_im) / den, (ni * a_re - nr * a_im) / den
    bbar_re = f_re[..., None] * b_re - f_im[..., None] * b_im
    bbar_im = f_re[..., None] * b_im + f_im[..., None] * b_re
    eye = jnp.eye(S5_GB, dtype=F32)
    bb = jnp.stack([bbar_re, bbar_im]).reshape(2, S5_NBLK, S5_GB, S5_STATE, S5_GROUP)
    wb = jnp.einsum('ab,rjapc->jacrbp', eye, bb).reshape(S5_NBLK, LANES, 2 * S5_LANES)
    wb = wb.reshape(S5_PAIRS, 2, LANES, 2 * S5_LANES).transpose(0, 2, 1, 3).reshape(S5_PAIRS, LANES, 4 * S5_LANES)
    cc = jnp.stack([c_re, -c_im]).reshape(2, S5_NBLK, S5_GB, S5_GROUP, S5_STATE)
    wc = jnp.einsum('ab,rjacp->jrapbc', eye, cc).reshape(S5_NBLK, 2 * S5_LANES, LANES)
    wc = wc.reshape(S5_PAIRS, 2, 2 * S5_LANES, LANES).transpose(0, 2, 1, 3).reshape(S5_PAIRS, 2 * S5_LANES, 2 * LANES)
    ab = jnp.concatenate([abar_re.reshape(S5_NBLK, S5_LANES), abar_im.reshape(S5_NBLK, S5_LANES)], axis=1)
    ab = jnp.tile(ab.reshape(S5_PAIRS, 2, 2 * S5_LANES), (1, SUBLANES // 2, 1))
    dd = jnp.tile(d_skip.reshape(S5_PAIRS, 2, LANES), (1, SUBLANES // 2, 1))
    return wb.astype(BF16), ab, wc.astype(BF16), dd


def _s5(u6, wb, ab, wc, dd, l):
    rows = S5_TC * SUBLANES
    return pl.pallas_call(
        _s5_kernel,
        grid=(S5_PAIRS // S5_PPS, l // S5_TC),
        in_specs=[pl.BlockSpec((S5_PPS, rows, LANES), lambda k, c: (k, c, 0)),
                  pl.BlockSpec((S5_PPS, LANES, 4 * S5_LANES), lambda k, c: (k, 0, 0)),
                  pl.BlockSpec((S5_PPS, SUBLANES, 2 * S5_LANES), lambda k, c: (k, 0, 0)),
                  pl.BlockSpec((S5_PPS, 2 * S5_LANES, 2 * LANES), lambda k, c: (k, 0, 0)),
                  pl.BlockSpec((S5_PPS, SUBLANES, LANES), lambda k, c: (k, 0, 0))],
        out_specs=pl.BlockSpec((S5_PPS, rows, LANES), lambda k, c: (k, c, 0)),
        out_shape=jax.ShapeDtypeStruct(u6.shape, F32),
        scratch_shapes=[pltpu.VMEM((S5_PPS * 2 * S5_LANES // LANES, rows, LANES), F32),
                        pltpu.VMEM((S5_PPS, SUBLANES, 2 * S5_LANES), F32)],
        compiler_params=_cparams(("arbitrary", "arbitrary")),
        name="s5",
    )(u6, wb, ab, wc, dd)


def _glu_kernel(y_ref, w_ref, o_ref):
    b = pl.program_id(1)
    tt = o_ref.shape[1]
    y = jnp.concatenate([y_ref.at[k][pl.ds(b * 2 + s, tt, stride=SUBLANES), :]
                         for k in range(S5_PAIRS) for s in range(2)], axis=1)
    g = jnp.dot(y.astype(BF16), w_ref[...], preferred_element_type=F32)
    o_ref[0] = (y * jax.nn.sigmoid(g)).astype(o_ref.dtype)


def _glu(y6, w_glu, bn, l):
    tt = 512
    assert bn * 2 == SUBLANES
    return pl.pallas_call(
        _glu_kernel,
        grid=(l // tt, bn),
        in_specs=[pl.BlockSpec((S5_PAIRS, tt * SUBLANES, LANES), lambda i, b: (0, i, 0)),
                  pl.BlockSpec((S5_CH, S5_CH), lambda i, b: (0, 0))],
        out_specs=pl.BlockSpec((1, tt, S5_CH), lambda i, b: (b, i, 0)),
        out_shape=jax.ShapeDtypeStruct((bn, l, S5_CH), BF16),
        compiler_params=_cparams(("arbitrary", "arbitrary")),
        name="glu",
    )(y6, w_glu)


def _merge_kernel(att_ref, z_ref, gm_ref, gs_ref, x_ref, mod_ref, gpost_ref, gpre_ref,
                  wbm_ref, wbs_ref, wo_ref, wrh_ref, wrl_ref, x1_ref, h2b_ref, h2s_ref, lg_ref):
    tm = x_ref.shape[1]
    th = tm // MERGE_SUBTILES
    for r0 in range(0, tm, th):
        rows = slice(r0, r0 + th)
        ym = jnp.dot(att_ref[rows, :], wbm_ref[...], preferred_element_type=F32)
        ys = jnp.dot(z_ref[rows, :], wbs_ref[...], preferred_element_type=F32)
        mixed_in = (gm_ref[rows, :].astype(F32) * ym + gs_ref[rows, :].astype(F32) * ys).astype(BF16)
        mixed = jnp.dot(mixed_in, wo_ref[...], preferred_element_type=F32)
        x1 = x_ref[0, rows, :] + mod_ref[0, 2:3, :] * _rms(mixed, gpost_ref[...])
        x1_ref[0, rows, :] = x1
        h2 = _rms(x1, gpre_ref[...]) * (1.0 + mod_ref[0, 4:5, :]) + mod_ref[0, 3:4, :]
        h2_hi = h2.astype(BF16)
        h2b_ref[0, rows, :] = h2_hi
        for s in range(SLAB):
            h2s_ref[pl.ds(r0 * SLAB + s, th, stride=SLAB), :] = h2[:, s * LANES:(s + 1) * LANES]
        h2_lo = (h2 - h2_hi.astype(F32)).astype(BF16)
        lg_ref[:, rows] = (lax.dot_general(wrh_ref[...], h2_hi, NT_DIMS, preferred_element_type=F32)
                           + lax.dot_general(wrl_ref[...], h2_hi, NT_DIMS, preferred_element_type=F32)
                           + lax.dot_general(wrh_ref[...], h2_lo, NT_DIMS, preferred_element_type=F32))


def _merge(att, z_tm, gates, x, mod, g_post, g_pre_ffn, w_br_mla, w_br_s5, w_out, wr_hi, wr_lo):
    bn, l, d = x.shape
    tm = 256
    nl = l // tm
    row = lambda b, i: (b * nl + i, 0)
    const = lambda b, i: (0, 0)
    one = pl.Buffered(1)
    return pl.pallas_call(
        _merge_kernel,
        grid=(bn, nl),
        in_specs=[pl.BlockSpec((tm, MLA_HEADS * V_HEAD), row),
                  pl.BlockSpec((tm, S5_CH), row),
                  pl.BlockSpec((tm, d), lambda b, i: (b * nl + i, 0)),
                  pl.BlockSpec((tm, d), lambda b, i: (b * nl + i, 1)),
                  pl.BlockSpec((1, tm, d), lambda b, i: (b, i, 0)),
                  pl.BlockSpec((1, 6, d), lambda b, i: (b, 0, 0)),
                  pl.BlockSpec((1, d), const),
                  pl.BlockSpec((1, d), const),
                  pl.BlockSpec((MLA_HEADS * V_HEAD, d), const, pipeline_mode=one),
                  pl.BlockSpec((S5_CH, d), const, pipeline_mode=one),
                  pl.BlockSpec((d, d), const, pipeline_mode=one),
                  pl.BlockSpec((N_EXPERTS, d), const, pipeline_mode=one),
                  pl.BlockSpec((N_EXPERTS, d), const, pipeline_mode=one)],
        out_specs=[pl.BlockSpec((1, tm, d), lambda b, i: (b, i, 0)),
                   pl.BlockSpec((1, tm, d), lambda b, i: (b, i, 0)),
                   pl.BlockSpec((tm * SLAB, LANES), row),
                   pl.BlockSpec((N_EXPERTS, tm), lambda b, i: (0, b * nl + i))],
        out_shape=[jax.ShapeDtypeStruct((bn, l, d), F32),
                   jax.ShapeDtypeStruct((bn, l, d), BF16),
                   jax.ShapeDtypeStruct((bn * l * SLAB, LANES), F32),
                   jax.ShapeDtypeStruct((N_EXPERTS, bn * l), F32)],
        compiler_params=_cparams(("arbitrary", "arbitrary")),
        name="merge",
    )(att, z_tm, gates, gates, x, mod, g_post.reshape(1, d), g_pre_ffn.reshape(1, d),
      w_br_mla, w_br_s5, w_out, wr_hi, wr_lo)


def _route_kernel(lg_ref, b_ref, ti_ref, tw_ref, cnt_ref):
    ng = N_EXPERT_GROUPS
    gsz = N_EXPERTS // ng
    lg = lg_ref[...]
    t = lg.shape[-1]
    sc = jax.nn.sigmoid(lg)
    sel = sc + b_ref[...]
    ninf = -jnp.inf
    i_in = lax.broadcasted_iota(jnp.int32, (ng, gsz, t), 1).astype(F32)
    m1 = jnp.max(sel, axis=1, keepdims=True)
    idx1 = jnp.min(jnp.where(sel == m1, i_in, float(gsz)), axis=1, keepdims=True)
    m2 = jnp.max(jnp.where(i_in == idx1, ninf, sel), axis=1, keepdims=True)
    gs = m1 + m2
    g_i = lax.broadcasted_iota(jnp.int32, (ng, 1, t), 0).astype(F32)
    picked = jnp.zeros((ng, 1, t), F32)
    cur = gs
    for _ in range(TOPK_GROUPS):
        m = jnp.max(cur, axis=0, keepdims=True)
        idx = jnp.min(jnp.where(cur == m, g_i, float(ng)), axis=0, keepdims=True)
        hit = g_i == idx
        picked = jnp.where(hit, 1.0, picked)
        cur = jnp.where(hit, ninf, cur)
    cand = jnp.where(picked > 0.5, sel, ninf)
    e_i = lax.broadcasted_iota(jnp.int32, (ng, gsz, t), 0).astype(F32) * float(gsz) + i_in
    ws = []
    hits = jnp.zeros((ng, gsz, t), F32)
    for r in range(TOP_K):
        m = jnp.max(jnp.max(cand, axis=1, keepdims=True), axis=0, keepdims=True)
        idx = jnp.min(jnp.min(jnp.where(cand == m, e_i, float(N_EXPERTS)), axis=1, keepdims=True),
                      axis=0, keepdims=True)
        hit = e_i == idx
        w = jnp.sum(jnp.sum(jnp.where(hit, sc, 0.0), axis=1, keepdims=True), axis=0, keepdims=True)
        ti_ref[r:r + 1, :] = idx[0].astype(jnp.int32)
        ws.append(w[0])
        hits = jnp.where(hit, 1.0, hits)
        cand = jnp.where(hit, ninf, cand)

    @pl.when(pl.program_id(0) == 0)
    def _():
        cnt_ref[...] = jnp.zeros(cnt_ref.shape, F32)

    cnt_ref[...] += jnp.sum(hits, axis=2, keepdims=True)
    tot = ws[0]
    for w in ws[1:]:
        tot = tot + w
    for r in range(TOP_K):
        tw_ref[r:r + 1, :] = ws[r] / tot * ROUTED_SCALE
    for r in range(TOP_K, SUBLANES):
        ti_ref[r:r + 1, :] = jnp.zeros((1, t), jnp.int32)
        tw_ref[r:r + 1, :] = jnp.zeros((1, t), F32)


def _route(logits_t, router_bias):
    n = logits_t.shape[1]
    ng = N_EXPERT_GROUPS
    gsz = N_EXPERTS // ng
    tn = 1024
    return pl.pallas_call(
        _route_kernel,
        grid=(n // tn,),
        in_specs=[pl.BlockSpec((ng, gsz, tn), lambda i: (0, 0, i)),
                  pl.BlockSpec((ng, gsz, 1), lambda i: (0, 0, 0))],
        out_specs=[pl.BlockSpec((SUBLANES, tn), lambda i: (0, i)),
                   pl.BlockSpec((SUBLANES, tn), lambda i: (0, i)),
                   pl.BlockSpec((ng, gsz, 1), lambda i: (0, 0, 0))],
        out_shape=[jax.ShapeDtypeStruct((SUBLANES, n), jnp.int32),
                   jax.ShapeDtypeStruct((SUBLANES, n), F32),
                   jax.ShapeDtypeStruct((ng, gsz, 1), F32)],
        compiler_params=_cparams(("arbitrary",)),
        name="route",
    )(logits_t.reshape(ng, gsz, n), router_bias.reshape(ng, gsz, 1))


def _moe_kernel(blk_e_ref, nused_ref, nvalid_ref, nxt_e_ref, wslot_ref, t_ref, t1_ref, t2_ref, a_ref, h_hbm,
                wg_hbm, wu_hbm, wd_hbm, y_hbm, xbuf, ybuf, wg_buf, wu_buf, wd_buf, wg_s, wu_s, wd_s,
                gsem, wsem, wtsem):
    j = pl.program_id(0)
    n_used = nused_ref[0]
    slot = lax.rem(j, GATHER_SLOTS)
    nv = nvalid_ref[j]
    nv_prev = nvalid_ref[jnp.maximum(j - 1, 0)]

    def slab(ref, row0, align=SUBLANES):
        return ref.at[pl.ds(pl.multiple_of(row0, align), SLAB), :]

    def start_gather(idx_ref, s):
        def body(r, _):
            pltpu.make_async_copy(slab(h_hbm, idx_ref[0, 0, r]), slab(xbuf.at[s], r * MOE_PITCH, PITCH_ALIGN),
                                  gsem.at[s]).start()
            return 0
        lax.fori_loop(0, MOE_TB, body, 0, unroll=8)

    def wait_gather(s):
        rows = MOE_TB * SLAB
        pltpu.make_async_copy(h_hbm.at[pl.ds(0, rows), :], xbuf.at[s, pl.ds(0, rows), :], gsem.at[s]).wait()

    def weight_copies(expert, s):
        return (pltpu.make_async_copy(wg_hbm.at[expert], wg_buf.at[s], wtsem.at[s]),
                pltpu.make_async_copy(wu_hbm.at[expert], wu_buf.at[s], wtsem.at[s]),
                pltpu.make_async_copy(wd_hbm.at[expert], wd_buf.at[s], wtsem.at[s]))

    def wait_writes(count):
        p = MOE_TB
        while p >= 1:
            @pl.when((count & p) != 0)
            def _(p=p):
                rows = p * SLAB
                pltpu.make_async_copy(ybuf.at[pl.ds(0, rows), :], y_hbm.at[pl.ds(0, rows), :], wsem.at[0]).wait()
            p //= 2

    @pl.when(j < n_used)
    def _():
        @pl.when(j == 0)
        def _():
            start_gather(t_ref, 0)

            @pl.when(n_used > 1)
            def _():
                start_gather(t1_ref, 1)

        e = blk_e_ref[j]
        ws = wslot_ref[j]

        @pl.when(j == 0)
        def _():
            for cp in weight_copies(e, ws):
                cp.start(priority=WEIGHT_DMA_PRIORITY)

        first = jnp.logical_or(j == 0, e != blk_e_ref[jnp.maximum(j - 1, 0)])

        @pl.when(first)
        def _():
            for cp in weight_copies(e, ws):
                cp.wait()
            wg_s[...] = wg_buf[ws].astype(BF16)
            wu_s[...] = wu_buf[ws].astype(BF16)
            wd_s[...] = wd_buf[ws].astype(BF16)

        wait_gather(slot)

        @pl.when(j + 2 < n_used)
        def _():
            start_gather(t2_ref, lax.rem(j + 2, GATHER_SLOTS))

        @pl.when(jnp.logical_and(first, nxt_e_ref[j] >= 0))
        def _():
            for cp in weight_copies(nxt_e_ref[j], 1 - ws):
                cp.start(priority=WEIGHT_DMA_PRIORITY)

        xs = xbuf.at[slot]
        x = jnp.concatenate([xs[pl.ds(s, MOE_TB, stride=MOE_PITCH), :] for s in range(SLAB)], axis=1).astype(BF16)
        g = jnp.dot(x, wg_s[...], preferred_element_type=F32)
        u = jnp.dot(x, wu_s[...], preferred_element_type=F32)
        hm = (g * jax.nn.sigmoid(g) * u).astype(BF16)
        y = jnp.dot(hm, wd_s[...], preferred_element_type=F32)

        @pl.when(j > 0)
        def _():
            wait_writes(nv_prev)

        for s in range(SLAB):
            ybuf[pl.ds(s, MOE_TB, stride=MOE_PITCH), :] = y[:, s * LANES:(s + 1) * LANES]

        def start_write(r):
            pltpu.make_async_copy(slab(ybuf, r * MOE_PITCH, PITCH_ALIGN), slab(y_hbm, a_ref[0, 0, r]),
                                  wsem.at[0]).start()

        def wgroup(g8, _):
            for q in range(SUBLANES):
                start_write(g8 * SUBLANES + q)
            return 0

        def wtail(r, _):
            start_write(r)
            return 0

        n_groups = lax.shift_right_logical(nv, 3)
        lax.fori_loop(0, n_groups, wgroup, 0)
        lax.fori_loop(n_groups * SUBLANES, nv, wtail, 0)

        @pl.when(j == n_used - 1)
        def _():
            wait_writes(nv)


def _moe(h2s, topi, counts, w_gate, w_up, w_down):
    n = h2s.shape[0] // SLAB
    d = D_MODEL
    nk = n * TOP_K
    tb = MOE_TB
    n_blocks = nk // tb + N_EXPERTS
    flat_e = topi[:TOP_K].reshape(nk)
    _, order = lax.sort_key_val(flat_e, jnp.arange(nk, dtype=jnp.int32))
    start = jnp.cumsum(counts) - counts
    padded = (counts + tb - 1) // tb * tb
    pad_end = jnp.cumsum(padded)
    pad_start = pad_end - padded
    n_used = (pad_end[-1] // tb).astype(jnp.int32).reshape(1)
    blk_p0 = jnp.arange(n_blocks, dtype=jnp.int32) * tb
    blk_e = jnp.minimum(jnp.sum((pad_end[None, :] <= blk_p0[:, None]).astype(jnp.int32), axis=1), N_EXPERTS - 1)
    blk_off = blk_p0 - pad_start[blk_e]
    nvalid = jnp.clip(counts[blk_e] - blk_off, 0, tb).astype(jnp.int32)
    rank = (start[blk_e] + blk_off)[:, None] + jnp.arange(tb, dtype=jnp.int32)[None, :]
    a3 = order[jnp.clip(rank, 0, nk - 1)].reshape(n_blocks, 1, tb)
    arow = a3 * SLAB
    trow = (a3 % n) * SLAB
    used = counts > 0
    e_ids = jnp.arange(N_EXPERTS, dtype=jnp.int32)
    later = jnp.where(used, e_ids, N_EXPERTS)
    nxt = jnp.concatenate([lax.cummin(later[::-1])[::-1][1:], jnp.full((1,), N_EXPERTS, jnp.int32)])
    nxt_e = jnp.where(nxt < N_EXPERTS, nxt, -1)[blk_e].astype(jnp.int32)
    wslot = ((jnp.cumsum(used.astype(jnp.int32)) - 1) % 2)[blk_e].astype(jnp.int32)

    smem_blk = lambda f: pl.BlockSpec((1, 1, tb), f, memory_space=pltpu.SMEM)
    any_spec = pl.BlockSpec(memory_space=pl.ANY)
    grid_spec = pltpu.PrefetchScalarGridSpec(
        num_scalar_prefetch=5,
        grid=(n_blocks,),
        in_specs=[smem_blk(lambda j, *_: (j, 0, 0)),
                  smem_blk(lambda j, *_: (jnp.minimum(j + 1, n_blocks - 1), 0, 0)),
                  smem_blk(lambda j, *_: (jnp.minimum(j + 2, n_blocks - 1), 0, 0)),
                  smem_blk(lambda j, *_: (j, 0, 0)),
                  any_spec, any_spec, any_spec, any_spec],
        out_specs=any_spec,
        scratch_shapes=[pltpu.VMEM((GATHER_SLOTS, tb * MOE_PITCH, LANES), F32),
                        pltpu.VMEM((tb * MOE_PITCH, LANES), F32),
                        pltpu.VMEM((2, d, D_EXPERT), F32),
                        pltpu.VMEM((2, d, D_EXPERT), F32),
                        pltpu.VMEM((2, D_EXPERT, d), F32),
                        pltpu.VMEM((d, D_EXPERT), BF16),
                        pltpu.VMEM((d, D_EXPERT), BF16),
                        pltpu.VMEM((D_EXPERT, d), BF16),
                        pltpu.SemaphoreType.DMA((GATHER_SLOTS,)),
                        pltpu.SemaphoreType.DMA((1,)),
                        pltpu.SemaphoreType.DMA((2,))],
    )
    return pl.pallas_call(
        _moe_kernel,
        grid_spec=grid_spec,
        out_shape=jax.ShapeDtypeStruct((TOP_K * n * SLAB, LANES), F32),
        compiler_params=_cparams(("arbitrary",)),
        name="moe",
    )(blk_e, n_used, nvalid, nxt_e, wslot, trow, trow, trow, arow, h2s, w_gate, w_up, w_down)


def _final_kernel(h_ref, y0, y1, y2, y3, y4, y5, tw_ref, x1_ref, mod_ref, g_ref, wg_ref, wu_ref, wd_ref, o_ref):
    h = h_ref[...]
    tm = h.shape[0]
    g = jnp.dot(h, wg_ref[...], preferred_element_type=F32)
    u = jnp.dot(h, wu_ref[...], preferred_element_type=F32)
    hm = (g * jax.nn.sigmoid(g) * u).astype(BF16)
    ffn = jnp.dot(hm, wd_ref[...], preferred_element_type=F32)
    tw = tw_ref[...]
    for k, y in enumerate((y0, y1, y2, y3, y4, y5)):
        yk = jnp.concatenate([y[pl.ds(s, tm, stride=SLAB), :] for s in range(SLAB)], axis=1)
        ffn = ffn + yk * tw[:, k:k + 1]
    o_ref[0] = x1_ref[0] + mod_ref[0, 5:6, :] * _rms(ffn, g_ref[...])


def _final(h2, y_flat, topw_t, x1, mod, g_post_ffn, w_sg, w_su, w_sd):
    bn, l, d = x1.shape
    n = bn * l
    tm = 256
    nl = l // tm
    const = lambda b, i: (0, 0)
    one = pl.Buffered(1)
    y_specs = [pl.BlockSpec((tm * SLAB, LANES), functools.partial(lambda b, i, k: (k * (n // tm) + b * nl + i, 0), k=k))
               for k in range(TOP_K)]
    return pl.pallas_call(
        _final_kernel,
        grid=(bn, nl),
        in_specs=[pl.BlockSpec((tm, d), lambda b, i: (b * nl + i, 0))] + y_specs + [
            pl.BlockSpec((tm, SUBLANES), lambda b, i: (b * nl + i, 0)),
            pl.BlockSpec((1, tm, d), lambda b, i: (b, i, 0)),
            pl.BlockSpec((1, 6, d), lambda b, i: (b, 0, 0)),
            pl.BlockSpec((1, d), const),
            pl.BlockSpec((d, D_EXPERT), const, pipeline_mode=one),
            pl.BlockSpec((d, D_EXPERT), const, pipeline_mode=one),
            pl.BlockSpec((D_EXPERT, d), const, pipeline_mode=one)],
        out_specs=pl.BlockSpec((1, tm, d), lambda b, i: (b, i, 0)),
        out_shape=jax.ShapeDtypeStruct((bn, l, d), F32),
        compiler_params=_cparams(("arbitrary", "arbitrary")),
        name="final",
    )(h2, *([y_flat] * TOP_K), topw_t, x1, mod, g_post_ffn.reshape(1, d), w_sg, w_su, w_sd)


def _layer(li, x, c, positions, w_ada, b_ada, g_pre_mix, g_post_mix, g_pre_ffn, g_post_ffn, w_in, g_q, g_kv,
           w_uq, w_uk, w_uv, a_re, a_im, log_dt, b_re, b_im, c_re, c_im, d_skip, w_glu, w_br_mla, w_br_s5,
           w_out, w_router, router_bias, w_exp_gate, w_exp_up, w_exp_down, w_sh_gate, w_sh_up, w_sh_down):
    bn, l, d = x.shape
    n = bn * l
    mod = _adaln(c, w_ada, b_ada).reshape(bn, 6, d)

    o_kpe = Q_LORA + KV_LORA
    o_u = o_kpe + QK_ROPE
    o_g = o_u + S5_CH
    w_in_t = jnp.swapaxes(w_in, 1, 2)
    w_q = jnp.pad(w_uq.reshape(Q_LORA, MLA_HEADS, QK_NOPE + QK_ROPE),
                  ((0, 0), (0, 0), (0, QK_PAD - QK_NOPE - QK_ROPE))).reshape(Q_LORA, MLA_HEADS * QK_PAD).astype(BF16)
    wr_t = w_router.T
    wr_hi = wr_t.astype(BF16)
    wr_lo = (wr_t - wr_hi.astype(F32)).astype(BF16)

    rope_c, rope_s1, rope_s2 = _rope_tables(positions)
    chunk_id = positions // CHUNK

    h, lat = _prenorm_lat(x, mod, g_pre_mix, 0, 1, w_in_t, li, o_kpe + LANES)
    h = h.reshape(n, d)
    gates = _mmt(h, w_in_t, li, o_g, 2 * d, BF16, 1024, 1024, act="sigmoid", name="mm_gates")
    u6 = _mm_u(h, w_in_t, li, o_u, bn, l)

    q, k, v = _qkvproj(lat, g_q, g_kv, w_q, w_uk.astype(BF16), w_uv.astype(BF16), rope_c, rope_s1, rope_s2)
    att = _attention(q, k, v, chunk_id, bn, l)

    wb, ab, wc, dd = _s5_params(a_re, a_im, log_dt, b_re, b_im, c_re, c_im, d_skip)
    y6 = _s5(u6, wb, ab, wc, dd, l)
    z = _glu(y6, w_glu.astype(BF16), bn, l).reshape(n, S5_CH)

    x1, h2b, h2s, logits_t = _merge(att, z, gates, x, mod, g_post_mix, g_pre_ffn, w_br_mla.astype(BF16),
                                    w_br_s5.astype(BF16), w_out.astype(BF16), wr_hi, wr_lo)

    topi, topw, cnt = _route(logits_t, router_bias)
    y_slabs = _moe(h2s, topi, cnt.reshape(N_EXPERTS).astype(jnp.int32), w_exp_gate, w_exp_up, w_exp_down)
    return _final(h2b.reshape(n, d), y_slabs, topw.T, x1, mod, g_post_ffn, w_sh_gate.astype(BF16),
                  w_sh_up.astype(BF16), w_sh_down.astype(BF16))


def kernel(x, c, positions, w_ada, b_ada, g_pre_mix, g_post_mix, g_pre_ffn, g_post_ffn, w_in, g_q, g_kv, w_uq, w_uk, w_uv, a_re, a_im, log_dt, b_re, b_im, c_re, c_im, d_skip, w_glu, w_br_mla, w_br_s5, w_out, w_router, router_bias, w_exp_gate, w_exp_up, w_exp_down, w_sh_gate, w_sh_up, w_sh_down):
    depth = w_ada.shape[0]
    for li in range(depth):
        x = _layer(li, x, c, positions, w_ada[li], b_ada[li], g_pre_mix[li], g_post_mix[li], g_pre_ffn[li],
                   g_post_ffn[li], w_in, g_q[li], g_kv[li], w_uq[li], w_uk[li], w_uv[li], a_re[li], a_im[li],
                   log_dt[li], b_re[li], b_im[li], c_re[li], c_im[li], d_skip[li], w_glu[li], w_br_mla[li],
                   w_br_s5[li], w_out[li], w_router[li], router_bias[li], w_exp_gate[li], w_exp_up[li],
                   w_exp_down[li], w_sh_gate[li], w_sh_up[li], w_sh_down[li])
    return x
```

```python
import functools

import jax
import jax.numpy as jnp
from jax import lax
from jax.experimental import pallas as pl
from jax.experimental.pallas import tpu as pltpu

F32 = jnp.float32
BF16 = jnp.bfloat16

D_MODEL = 2048
CHUNK = 64
EPS = 1e-6
MLA_HEADS = 8
QK_NOPE = 128
QK_ROPE = 64
V_HEAD = 128
Q_LORA = 512
KV_LORA = 512
ROPE_THETA = 10000.0
S5_CH = 1024
S5_GROUP = 16
S5_GROUPS = S5_CH // S5_GROUP
S5_STATE = 64
N_EXPERTS = 64
TOP_K = 6
N_EXPERT_GROUPS = 8
TOPK_GROUPS = 4
D_EXPERT = 512
ROUTED_SCALE = 2.5

LANES = 128
SUBLANES = 8
QK_PAD = 2 * LANES
VMEM_LIMIT = 56 * 1024 * 1024
NEG = -1e30
LOG2E = 1.4426950408889634

S5_GB = LANES // S5_GROUP
S5_NBLK = S5_CH // LANES
S5_PAIRS = S5_NBLK // 2
S5_LANES = S5_GB * S5_STATE
S5_PPS = 2
S5_TC = 256
S5_RB = 512

ATT_TQ = 256
ATT_TK = 256
MOE_TB = 256
SLAB = D_MODEL // LANES
SORT_RADIX = 1 << 16
MERGE_SUBTILES = 2
GATHER_SLOTS = 3
WEIGHT_DMA_PRIORITY = 1
MOE_PITCH = 20
PITCH_ALIGN = 4


def _cparams(sem):
    return pltpu.CompilerParams(dimension_semantics=sem, vmem_limit_bytes=VMEM_LIMIT)


def _rms(x, g):
    return x * lax.rsqrt(jnp.mean(x * x, axis=-1, keepdims=True) + EPS) * g


def _adaln_kernel(c_ref, w_ref, b_ref, o_ref):
    c = c_ref[...]
    a = (c * jax.nn.sigmoid(c)).astype(BF16)
    o_ref[...] = jnp.dot(a, w_ref[...].astype(BF16), preferred_element_type=F32) + b_ref[...]


def _adaln(c, w, b):
    bn, d = c.shape
    n = w.shape[1]
    tn = 1024
    return pl.pallas_call(
        _adaln_kernel,
        grid=(n // tn,),
        in_specs=[pl.BlockSpec((bn, d), lambda j: (0, 0)),
                  pl.BlockSpec((d, tn), lambda j: (0, j)),
                  pl.BlockSpec((1, tn), lambda j: (0, j))],
        out_specs=pl.BlockSpec((bn, tn), lambda j: (0, j)),
        out_shape=jax.ShapeDtypeStruct((bn, n), F32),
        compiler_params=_cparams(("arbitrary",)),
        name="adaln",
    )(c, w, b.reshape(1, n))


NT_DIMS = (((1,), (1,)), ((), ()))


def _prenorm_lat_kernel(x_ref, mod_ref, g_ref, w_ref, h_ref, lat_ref, w_scr, *, sh_row, sc_row):
    @pl.when(jnp.logical_and(pl.program_id(0) == 0, pl.program_id(1) == 0))
    def _():
        w_scr[...] = w_ref[...].astype(BF16)

    y = _rms(x_ref[0], g_ref[...])
    h = (y * (1.0 + mod_ref[0, sc_row:sc_row + 1, :]) + mod_ref[0, sh_row:sh_row + 1, :]).astype(BF16)
    h_ref[0] = h
    lat_ref[...] = lax.dot_general(h, w_scr[...], NT_DIMS, preferred_element_type=F32)


def _prenorm_lat(x, mod, g, sh_row, sc_row, w_t, layer, n_lat):
    bn, l, d = x.shape
    tl = 512
    nl = l // tl
    return pl.pallas_call(
        functools.partial(_prenorm_lat_kernel, sh_row=sh_row, sc_row=sc_row),
        grid=(bn, nl),
        in_specs=[pl.BlockSpec((1, tl, d), lambda b, i: (b, i, 0)),
                  pl.BlockSpec((1, 6, d), lambda b, i: (b, 0, 0)),
                  pl.BlockSpec((1, d), lambda b, i: (0, 0)),
                  pl.BlockSpec((None, n_lat, d), lambda b, i: (layer, 0, 0), pipeline_mode=pl.Buffered(1))],
        out_specs=[pl.BlockSpec((1, tl, d), lambda b, i: (b, i, 0)),
                   pl.BlockSpec((tl, n_lat), lambda b, i: (b * nl + i, 0))],
        out_shape=[jax.ShapeDtypeStruct((bn, l, d), BF16),
                   jax.ShapeDtypeStruct((bn * l, n_lat), F32)],
        scratch_shapes=[pltpu.VMEM((n_lat, d), BF16)],
        compiler_params=_cparams(("arbitrary", "arbitrary")),
        name="prenorm_lat",
    )(x, mod, g.reshape(1, d), w_t)


def _wt_tile(wa_ref, wb_ref, shift, tn):
    if shift == 0:
        return wa_ref[...].astype(BF16)
    return jnp.concatenate([wa_ref[...], wb_ref[...]], axis=0)[shift:shift + tn].astype(BF16)


def _wt_specs(layer, row0, tn, k, jmap):
    shift = row0 % LANES
    c0 = row0 - shift
    assert c0 % tn == 0 and shift % SUBLANES == 0
    specs = [pl.BlockSpec((None, tn, k), lambda *g: (layer, jmap(*g) + c0 // tn, 0))]
    if shift:
        specs.append(pl.BlockSpec((None, LANES, k), lambda *g: (layer, (c0 + (jmap(*g) + 1) * tn) // LANES, 0)))
    return specs, shift


def _mmt_kernel(a_ref, wa_ref, *rest, act, shift):
    wb_ref = rest[0] if shift else None
    o_ref, w_scr = rest[-2:]

    @pl.when(pl.program_id(1) == 0)
    def _():
        w_scr[...] = _wt_tile(wa_ref, wb_ref, shift, w_scr.shape[0])

    acc = lax.dot_general(a_ref[...], w_scr[...], NT_DIMS, preferred_element_type=F32)
    if act == "sigmoid":
        acc = jax.nn.sigmoid(acc)
    o_ref[...] = acc.astype(o_ref.dtype)


def _mmt(a, w_t, layer, row0, n, out_dtype, tm, tn, act=None, name="mmt"):
    m, k = a.shape
    w_specs, shift = _wt_specs(layer, row0, tn, k, lambda j, i: j)
    return pl.pallas_call(
        functools.partial(_mmt_kernel, act=act, shift=shift),
        grid=(n // tn, m // tm),
        in_specs=[pl.BlockSpec((tm, k), lambda j, i: (i, 0))] + w_specs,
        out_specs=pl.BlockSpec((tm, tn), lambda j, i: (i, j)),
        out_shape=jax.ShapeDtypeStruct((m, n), out_dtype),
        scratch_shapes=[pltpu.VMEM((tn, k), BF16)],
        compiler_params=_cparams(("arbitrary", "arbitrary")),
        name=name,
    )(a, *([w_t] * len(w_specs)))


def _mm_u_kernel(a_ref, wa_ref, *rest, shift):
    wb_ref = rest[0] if shift else None
    o_ref, w_scr = rest[-2:]
    b = pl.program_id(1)

    @pl.when(jnp.logical_and(pl.program_id(0) == 0, b == 0))
    def _():
        w_scr[...] = _wt_tile(wa_ref, wb_ref, shift, w_scr.shape[0])

    res = lax.dot_general(a_ref[...], w_scr[...], NT_DIMS, preferred_element_type=F32)
    tm = res.shape[0]
    for c in range(S5_NBLK):
        o_ref.at[c // 2][pl.ds(b * 2 + c % 2, tm, stride=SUBLANES), :] = res[:, c * LANES:(c + 1) * LANES]


def _mm_u(h, w_t, layer, row0, bn, l):
    m, k = h.shape
    tm = 512
    nl = l // tm
    assert bn * 2 == SUBLANES
    w_specs, shift = _wt_specs(layer, row0, S5_CH, k, lambda i, b: 0)
    return pl.pallas_call(
        functools.partial(_mm_u_kernel, shift=shift),
        grid=(nl, bn),
        in_specs=[pl.BlockSpec((tm, k), lambda i, b: (b * nl + i, 0))] + w_specs,
        out_specs=pl.BlockSpec((S5_PAIRS, tm * SUBLANES, LANES), lambda i, b: (0, i, 0)),
        out_shape=jax.ShapeDtypeStruct((S5_PAIRS, l * SUBLANES, LANES), F32),
        scratch_shapes=[pltpu.VMEM((S5_CH, k), BF16)],
        compiler_params=_cparams(("arbitrary", "arbitrary")),
        name="mm_u",
    )(h, *([w_t] * len(w_specs)))


def _rope_tab_kernel(pos_ref, k_ref, c_ref, s1_ref, s2_ref):
    ang = pos_ref[...].astype(F32) * k_ref[0:1, :]
    s = jnp.sin(ang)
    c_ref[...] = jnp.cos(ang) * k_ref[1:2, :]
    s1_ref[...] = s * k_ref[2:3, :]
    s2_ref[...] = s * k_ref[3:4, :]


def _rope_tables(positions):
    n = positions.size
    half = QK_ROPE // 2
    inv_freq = ROPE_THETA ** (-jnp.arange(half, dtype=F32) / half)
    zh, oh = jnp.zeros((half,), F32), jnp.ones((half,), F32)
    z2 = jnp.zeros((LANES - QK_ROPE,), F32)
    rows = [jnp.concatenate([inv_freq, inv_freq, z2]), jnp.concatenate([oh, oh, z2]),
            jnp.concatenate([-oh, zh, z2]), jnp.concatenate([zh, oh, z2])]
    consts = jnp.stack(rows + [jnp.zeros((LANES,), F32)] * (SUBLANES - len(rows)))
    tm = 1024
    tab = jax.ShapeDtypeStruct((n, LANES), F32)
    return pl.pallas_call(
        _rope_tab_kernel,
        grid=(n // tm,),
        in_specs=[pl.BlockSpec((tm, 1), lambda i: (i, 0)),
                  pl.BlockSpec((SUBLANES, LANES), lambda i: (0, 0))],
        out_specs=[pl.BlockSpec((tm, LANES), lambda i: (i, 0))] * 3,
        out_shape=[tab, tab, tab],
        compiler_params=_cparams(("arbitrary",)),
        name="rope_tables",
    )(positions.reshape(n, 1), consts)


def _rope_tile(t, c_ref, s1_ref, s2_ref):
    return (t * c_ref[...] + pltpu.roll(t, LANES - QK_ROPE // 2, 1) * s1_ref[...]
            + pltpu.roll(t, QK_ROPE // 2, 1) * s2_ref[...])


def _qkvproj_kernel(lat_ref, gq_ref, gkv_ref, wq_ref, wk_ref, wv_ref, kpe_ref, c_ref, s1_ref, s2_ref,
                    q_ref, k_ref, v_ref, *, scale):
    lat = lat_ref[...]
    qn = _rms(lat[:, :Q_LORA], gq_ref[...]).astype(BF16)
    cn = _rms(lat[:, Q_LORA:], gkv_ref[...]).astype(BF16)
    q = jnp.dot(qn, wq_ref[...], preferred_element_type=F32)
    kn = jnp.dot(cn, wk_ref[...], preferred_element_type=F32)
    v_ref[...] = jnp.dot(cn, wv_ref[...], preferred_element_type=F32).astype(v_ref.dtype)
    kt = _rope_tile(kpe_ref[...], c_ref, s1_ref, s2_ref).astype(k_ref.dtype)
    for h in range(MLA_HEADS):
        o = h * QK_PAD
        q_ref[:, o:o + LANES] = (q[:, o:o + LANES] * scale).astype(q_ref.dtype)
        qt = _rope_tile(q[:, o + LANES:o + QK_PAD], c_ref, s1_ref, s2_ref)
        q_ref[:, o + LANES:o + QK_PAD] = (qt * scale).astype(q_ref.dtype)
        k_ref[:, o:o + LANES] = kn[:, h * QK_NOPE:(h + 1) * QK_NOPE].astype(k_ref.dtype)
        k_ref[:, o + LANES:o + QK_PAD] = kt


def _qkvproj(lat, g_q, g_kv, w_q, w_k, w_v, rope_c, rope_s1, rope_s2):
    n = lat.shape[0]
    tm = 512
    row = lambda i: (i, 0)
    const = lambda i: (0, 0)
    tab = pl.BlockSpec((tm, LANES), row)
    kpe_spec = pl.BlockSpec((tm, LANES), lambda i: (i, (Q_LORA + KV_LORA) // LANES))
    return pl.pallas_call(
        functools.partial(_qkvproj_kernel, scale=(QK_NOPE + QK_ROPE) ** -0.5 * LOG2E),
        grid=(n // tm,),
        in_specs=[pl.BlockSpec((tm, Q_LORA + KV_LORA), row),
                  pl.BlockSpec((1, Q_LORA), const),
                  pl.BlockSpec((1, KV_LORA), const),
                  pl.BlockSpec((Q_LORA, MLA_HEADS * QK_PAD), const),
                  pl.BlockSpec((KV_LORA, MLA_HEADS * QK_NOPE), const),
                  pl.BlockSpec((KV_LORA, MLA_HEADS * V_HEAD), const),
                  kpe_spec, tab, tab, tab],
        out_specs=[pl.BlockSpec((tm, MLA_HEADS * QK_PAD), row),
                   pl.BlockSpec((tm, MLA_HEADS * QK_PAD), row),
                   pl.BlockSpec((tm, MLA_HEADS * V_HEAD), row)],
        out_shape=[jax.ShapeDtypeStruct((n, MLA_HEADS * QK_PAD), BF16),
                   jax.ShapeDtypeStruct((n, MLA_HEADS * QK_PAD), BF16),
                   jax.ShapeDtypeStruct((n, MLA_HEADS * V_HEAD), BF16)],
        compiler_params=_cparams(("arbitrary",)),
        name="qkvproj",
    )(lat, g_q.reshape(1, Q_LORA), g_kv.reshape(1, KV_LORA), w_q, w_k, w_v, lat, rope_c, rope_s1, rope_s2)


def _attn_kernel(lo_ref, hi_ref, q_ref, k_ref, v_ref, qc_ref, kc_ref, o_ref, m_scr, l_scr, acc_scr, *, nq):
    b = pl.program_id(0)
    i = pl.program_id(1)
    qc = qc_ref[...]
    m_scr[...] = jnp.full(m_scr.shape, NEG, F32)
    l_scr[...] = jnp.zeros(l_scr.shape, F32)
    acc_scr[...] = jnp.zeros(acc_scr.shape, F32)

    def make_body(masked):
        def body(j, _):
            off = pl.multiple_of(j * ATT_TK, ATT_TK)
            if masked:
                mask = kc_ref[j] <= qc
            for h in range(MLA_HEADS):
                q = q_ref[:, h * QK_PAD:(h + 1) * QK_PAD]
                k = k_ref[pl.ds(off, ATT_TK), h * QK_PAD:(h + 1) * QK_PAD]
                s = lax.dot_general(q, k, (((1,), (1,)), ((), ())), preferred_element_type=F32)
                if masked:
                    s = jnp.where(mask, s, NEG)
                m_old = m_scr[h]
                m_new = jnp.maximum(m_old, jnp.max(s, axis=-1, keepdims=True))
                p = jnp.exp2(s - jnp.concatenate([m_new] * (ATT_TK // LANES), axis=1))
                alpha = jnp.exp2(m_old - m_new)
                l_scr[h] = alpha * l_scr[h] + jnp.sum(p, axis=-1, keepdims=True)
                v = v_ref[pl.ds(off, ATT_TK), h * V_HEAD:(h + 1) * V_HEAD]
                acc_scr[h] = alpha * acc_scr[h] + jnp.dot(p.astype(BF16), v, preferred_element_type=F32)
                m_scr[h] = m_new
            return 0
        return body

    lo = lo_ref[b * nq + i]
    lax.fori_loop(0, lo, make_body(False), 0)
    lax.fori_loop(lo, hi_ref[b * nq + i], make_body(True), 0)
    for h in range(MLA_HEADS):
        o_ref[:, h * V_HEAD:(h + 1) * V_HEAD] = (acc_scr[h] / l_scr[h]).astype(o_ref.dtype)


def _attention(q, k, v, chunk_id, bn, l):
    nq = l // ATT_TQ
    nk = l // ATT_TK
    q_max = jnp.max(chunk_id.reshape(bn, nq, ATT_TQ), axis=-1)
    k_min = jnp.min(chunk_id.reshape(bn, nk, ATT_TK), axis=-1)
    needed = k_min[:, None, :] <= q_max[:, :, None]
    hi = jnp.max(jnp.where(needed, jnp.arange(1, nk + 1, dtype=jnp.int32), 0), axis=-1).reshape(bn * nq)
    q_min = jnp.min(chunk_id.reshape(bn, nq, ATT_TQ), axis=-1)
    k_max = jnp.max(chunk_id.reshape(bn, nk, ATT_TK), axis=-1)
    full = k_max[:, None, :] <= q_min[:, :, None]
    lo = jnp.min(jnp.where(full, nk, jnp.arange(nk, dtype=jnp.int32)), axis=-1).astype(jnp.int32).reshape(bn * nq)
    hi = jnp.maximum(hi, lo)
    qc = chunk_id.reshape(bn * l, 1)
    kc = chunk_id.reshape(bn * nk, 1, ATT_TK)
    grid_spec = pltpu.PrefetchScalarGridSpec(
        num_scalar_prefetch=2,
        grid=(bn, nq),
        in_specs=[pl.BlockSpec((ATT_TQ, MLA_HEADS * QK_PAD), lambda b, i, *_: (b * nq + i, 0)),
                  pl.BlockSpec((l, MLA_HEADS * QK_PAD), lambda b, i, *_: (b, 0)),
                  pl.BlockSpec((l, MLA_HEADS * V_HEAD), lambda b, i, *_: (b, 0)),
                  pl.BlockSpec((ATT_TQ, 1), lambda b, i, *_: (b * nq + i, 0)),
                  pl.BlockSpec((nk, 1, ATT_TK), lambda b, i, *_: (b, 0, 0))],
        out_specs=pl.BlockSpec((ATT_TQ, MLA_HEADS * V_HEAD), lambda b, i, *_: (b * nq + i, 0)),
        scratch_shapes=[pltpu.VMEM((MLA_HEADS, ATT_TQ, LANES), F32),
                        pltpu.VMEM((MLA_HEADS, ATT_TQ, LANES), F32),
                        pltpu.VMEM((MLA_HEADS, ATT_TQ, V_HEAD), F32)],
    )
    return pl.pallas_call(
        functools.partial(_attn_kernel, nq=nq),
        grid_spec=grid_spec,
        out_shape=jax.ShapeDtypeStruct((bn * l, MLA_HEADS * V_HEAD), BF16),
        compiler_params=_cparams(("arbitrary", "arbitrary")),
        name="attention",
    )(lo, hi, q, k, v, qc, kc)


def _s5_kernel(u_ref, wb_ref, a_ref, wc_ref, d_ref, o_ref, x_scr, st_scr):
    rows = S5_TC * SUBLANES
    nsub = rows // S5_RB

    @pl.when(pl.program_id(1) == 0)
    def _():
        st_scr[...] = jnp.zeros_like(st_scr)

    even = (lax.broadcasted_iota(jnp.int32, (S5_RB, 1), 0) & 1) == 0
    half = S5_RB // 2
    nre = S5_LANES // LANES
    nsl = 2 * nre

    def mm_in(r, _):
        off = pl.multiple_of(r * S5_RB, S5_RB)
        for p in range(S5_PPS):
            for s in range(2):
                us = u_ref.at[p][pl.ds(off + s, half, stride=2), :].astype(BF16)
                out = jnp.dot(us, wb_ref[p, :, s * 2 * S5_LANES:(s + 1) * 2 * S5_LANES],
                              preferred_element_type=F32)
                for c in range(nsl):
                    x_scr.at[p * nsl + c][pl.ds(off + s, half, stride=2), :] = out[:, c * LANES:(c + 1) * LANES]
        return 0

    lax.fori_loop(0, nsub, mm_in, 0)

    a_c = [a_ref[p, :, c * LANES:(c + 1) * LANES] for p in range(S5_PPS) for c in range(nsl)]

    def step(t, carry):
        off = pl.multiple_of(t * SUBLANES, SUBLANES)
        new = [None] * (S5_PPS * nsl)
        for p in range(S5_PPS):
            for c in range(nre):
                ir, ii = p * nsl + c, p * nsl + nre + c
                xr, xi = carry[ir], carry[ii]
                ar, ai = a_c[ir], a_c[ii]
                nr = ar * xr - ai * xi + x_scr[ir, pl.ds(off, SUBLANES), :]
                ni = ar * xi + ai * xr + x_scr[ii, pl.ds(off, SUBLANES), :]
                x_scr[ir, pl.ds(off, SUBLANES), :] = nr
                x_scr[ii, pl.ds(off, SUBLANES), :] = ni
                new[ir], new[ii] = nr, ni
        return tuple(new)

    init = tuple(st_scr[p, :, c * LANES:(c + 1) * LANES] for p in range(S5_PPS) for c in range(nsl))
    fin = lax.fori_loop(0, S5_TC, step, init, unroll=8)
    for p in range(S5_PPS):
        for c in range(nsl):
            st_scr[p, :, c * LANES:(c + 1) * LANES] = fin[p * nsl + c]

    def mm_out(r, _):
        off = pl.multiple_of(r * S5_RB, S5_RB)
        for p in range(S5_PPS):
            d = jnp.concatenate([d_ref[p]] * (S5_RB // SUBLANES), axis=0)
            x = jnp.concatenate([x_scr[p * nsl + c, pl.ds(off, S5_RB), :] for c in range(nsl)],
                                axis=1).astype(BF16)
            out = jnp.dot(x, wc_ref[p], preferred_element_type=F32)
            y = jnp.where(even, out[:, :LANES], out[:, LANES:]) + d * u_ref[p, pl.ds(off, S5_RB), :]
            o_ref[p, pl.ds(off, S5_RB), :] = jax.nn.gelu(y).astype(o_ref.dtype)
        return 0

    lax.fori_loop(0, nsub, mm_out, 0)


def _s5_params(a_re, a_im, log_dt, b_re, b_im, c_re, c_im, d_skip):
    step = jnp.exp(log_dt)[:, None]
    mag = jnp.exp(a_re * step)
    abar_re, abar_im = mag * jnp.cos(a_im * step), mag * jnp.sin(a_im * step)
    den = a_re * a_re + a_im * a_im
    nr, ni = abar_re - 1.0, abar_im
    f_re, f_im = (nr * a_re + ni * a_im) / den, (ni * a_re - nr * a_im) / den
    bbar_re = f_re[..., None] * b_re - f_im[..., None] * b_im
    bbar_im = f_re[..., None] * b_im + f_im[..., None] * b_re
    eye = jnp.eye(S5_GB, dtype=F32)
    bb = jnp.stack([bbar_re, bbar_im]).reshape(2, S5_NBLK, S5_GB, S5_STATE, S5_GROUP)
    wb = jnp.einsum('ab,rjapc->jacrbp', eye, bb).reshape(S5_NBLK, LANES, 2 * S5_LANES)
    wb = wb.reshape(S5_PAIRS, 2, LANES, 2 * S5_LANES).transpose(0, 2, 1, 3).reshape(S5_PAIRS, LANES, 4 * S5_LANES)
    cc = jnp.stack([c_re, -c_im]).reshape(2, S5_NBLK, S5_GB, S5_GROUP, S5_STATE)
    wc = jnp.einsum('ab,rjacp->jrapbc', eye, cc).reshape(S5_NBLK, 2 * S5_LANES, LANES)
    wc = wc.reshape(S5_PAIRS, 2, 2 * S5_LANES, LANES).transpose(0, 2, 1, 3).reshape(S5_PAIRS, 2 * S5_LANES, 2 * LANES)
    ab = jnp.concatenate([abar_re.reshape(S5_NBLK, S5_LANES), abar_im.reshape(S5_NBLK, S5_LANES)], axis=1)
    ab = jnp.tile(ab.reshape(S5_PAIRS, 2, 2 * S5_LANES), (1, SUBLANES // 2, 1))
    dd = jnp.tile(d_skip.reshape(S5_PAIRS, 2, LANES), (1, SUBLANES // 2, 1))
    return wb.astype(BF16), ab, wc.astype(BF16), dd


def _s5(u6, wb, ab, wc, dd, l):
    rows = S5_TC * SUBLANES
    return pl.pallas_call(
        _s5_kernel,
        grid=(S5_PAIRS // S5_PPS, l // S5_TC),
        in_specs=[pl.BlockSpec((S5_PPS, rows, LANES), lambda k, c: (k, c, 0)),
                  pl.BlockSpec((S5_PPS, LANES, 4 * S5_LANES), lambda k, c: (k, 0, 0)),
                  pl.BlockSpec((S5_PPS, SUBLANES, 2 * S5_LANES), lambda k, c: (k, 0, 0)),
                  pl.BlockSpec((S5_PPS, 2 * S5_LANES, 2 * LANES), lambda k, c: (k, 0, 0)),
                  pl.BlockSpec((S5_PPS, SUBLANES, LANES), lambda k, c: (k, 0, 0))],
        out_specs=pl.BlockSpec((S5_PPS, rows, LANES), lambda k, c: (k, c, 0)),
        out_shape=jax.ShapeDtypeStruct(u6.shape, F32),
        scratch_shapes=[pltpu.VMEM((S5_PPS * 2 * S5_LANES // LANES, rows, LANES), F32),
                        pltpu.VMEM((S5_PPS, SUBLANES, 2 * S5_LANES), F32)],
        compiler_params=_cparams(("arbitrary", "arbitrary")),
        name="s5",
    )(u6, wb, ab, wc, dd)


def _glu_kernel(y_ref, w_ref, o_ref):
    b = pl.program_id(1)
    tt = o_ref.shape[1]
    y = jnp.concatenate([y_ref.at[k][pl.ds(b * 2 + s, tt, stride=SUBLANES), :]
                         for k in range(S5_PAIRS) for s in range(2)], axis=1)
    g = jnp.dot(y.astype(BF16), w_ref[...], preferred_element_type=F32)
    o_ref[0] = (y * jax.nn.sigmoid(g)).astype(o_ref.dtype)


def _glu(y6, w_glu, bn, l):
    tt = 512
    assert bn * 2 == SUBLANES
    return pl.pallas_call(
        _glu_kernel,
        grid=(l // tt, bn),
        in_specs=[pl.BlockSpec((S5_PAIRS, tt * SUBLANES, LANES), lambda i, b: (0, i, 0)),
                  pl.BlockSpec((S5_CH, S5_CH), lambda i, b: (0, 0))],
        out_specs=pl.BlockSpec((1, tt, S5_CH), lambda i, b: (b, i, 0)),
        out_shape=jax.ShapeDtypeStruct((bn, l, S5_CH), BF16),
        compiler_params=_cparams(("arbitrary", "arbitrary")),
        name="glu",
    )(y6, w_glu)


def _merge_kernel(att_ref, z_ref, gm_ref, gs_ref, x_ref, mod_ref, gpost_ref, gpre_ref,
                  wbm_ref, wbs_ref, wo_ref, wrh_ref, wrl_ref, x1_ref, h2b_ref, h2s_ref, lg_ref):
    tm = x_ref.shape[1]
    th = tm // MERGE_SUBTILES
    for r0 in range(0, tm, th):
        rows = slice(r0, r0 + th)
        ym = jnp.dot(att_ref[rows, :], wbm_ref[...], preferred_element_type=F32)
        ys = jnp.dot(z_ref[rows, :], wbs_ref[...], preferred_element_type=F32)
        mixed_in = (gm_ref[rows, :].astype(F32) * ym + gs_ref[rows, :].astype(F32) * ys).astype(BF16)
        mixed = jnp.dot(mixed_in, wo_ref[...], preferred_element_type=F32)
        x1 = x_ref[0, rows, :] + mod_ref[0, 2:3, :] * _rms(mixed, gpost_ref[...])
        x1_ref[0, rows, :] = x1
        h2 = _rms(x1, gpre_ref[...]) * (1.0 + mod_ref[0, 4:5, :]) + mod_ref[0, 3:4, :]
        h2_hi = h2.astype(BF16)
        h2b_ref[0, rows, :] = h2_hi
        for s in range(SLAB):
            h2s_ref[pl.ds(r0 * SLAB + s, th, stride=SLAB), :] = h2[:, s * LANES:(s + 1) * LANES]
        h2_lo = (h2 - h2_hi.astype(F32)).astype(BF16)
        lg_ref[:, rows] = (lax.dot_general(wrh_ref[...], h2_hi, NT_DIMS, preferred_element_type=F32)
                           + lax.dot_general(wrl_ref[...], h2_hi, NT_DIMS, preferred_element_type=F32)
                           + lax.dot_general(wrh_ref[...], h2_lo, NT_DIMS, preferred_element_type=F32))


def _merge(att, z_tm, gates, x, mod, g_post, g_pre_ffn, w_br_mla, w_br_s5, w_out, wr_hi, wr_lo):
    bn, l, d = x.shape
    tm = 256
    nl = l // tm
    row = lambda b, i: (b * nl + i, 0)
    const = lambda b, i: (0, 0)
    one = pl.Buffered(1)
    return pl.pallas_call(
        _merge_kernel,
        grid=(bn, nl),
        in_specs=[pl.BlockSpec((tm, MLA_HEADS * V_HEAD), row),
                  pl.BlockSpec((tm, S5_CH), row),
                  pl.BlockSpec((tm, d), lambda b, i: (b * nl + i, 0)),
                  pl.BlockSpec((tm, d), lambda b, i: (b * nl + i, 1)),
                  pl.BlockSpec((1, tm, d), lambda b, i: (b, i, 0)),
                  pl.BlockSpec((1, 6, d), lambda b, i: (b, 0, 0)),
                  pl.BlockSpec((1, d), const),
                  pl.BlockSpec((1, d), const),
                  pl.BlockSpec((MLA_HEADS * V_HEAD, d), const, pipeline_mode=one),
                  pl.BlockSpec((S5_CH, d), const, pipeline_mode=one),
                  pl.BlockSpec((d, d), const, pipeline_mode=one),
                  pl.BlockSpec((N_EXPERTS, d), const, pipeline_mode=one),
                  pl.BlockSpec((N_EXPERTS, d), const, pipeline_mode=one)],
        out_specs=[pl.BlockSpec((1, tm, d), lambda b, i: (b, i, 0)),
                   pl.BlockSpec((1, tm, d), lambda b, i: (b, i, 0)),
                   pl.BlockSpec((tm * SLAB, LANES), row),
                   pl.BlockSpec((N_EXPERTS, tm), lambda b, i: (0, b * nl + i))],
        out_shape=[jax.ShapeDtypeStruct((bn, l, d), F32),
                   jax.ShapeDtypeStruct((bn, l, d), BF16),
                   jax.ShapeDtypeStruct((bn * l * SLAB, LANES), F32),
                   jax.ShapeDtypeStruct((N_EXPERTS, bn * l), F32)],
        compiler_params=_cparams(("arbitrary", "arbitrary")),
        name="merge",
    )(att, z_tm, gates, gates, x, mod, g_post.reshape(1, d), g_pre_ffn.reshape(1, d),
      w_br_mla, w_br_s5, w_out, wr_hi, wr_lo)


def _route_kernel(lg_ref, b_ref, ti_ref, tw_ref, cnt_ref):
    ng = N_EXPERT_GROUPS
    gsz = N_EXPERTS // ng
    lg = lg_ref[...]
    t = lg.shape[-1]
    sc = jax.nn.sigmoid(lg)
    sel = sc + b_ref[...]
    ninf = -jnp.inf
    i_in = lax.broadcasted_iota(jnp.int32, (ng, gsz, t), 1).astype(F32)
    m1 = jnp.max(sel, axis=1, keepdims=True)
    idx1 = jnp.min(jnp.where(sel == m1, i_in, float(gsz)), axis=1, keepdims=True)
    m2 = jnp.max(jnp.where(i_in == idx1, ninf, sel), axis=1, keepdims=True)
    gs = m1 + m2
    g_i = lax.broadcasted_iota(jnp.int32, (ng, 1, t), 0).astype(F32)
    picked = jnp.zeros((ng, 1, t), F32)
    cur = gs
    for _ in range(TOPK_GROUPS):
        m = jnp.max(cur, axis=0, keepdims=True)
        idx = jnp.min(jnp.where(cur == m, g_i, float(ng)), axis=0, keepdims=True)
        hit = g_i == idx
        picked = jnp.where(hit, 1.0, picked)
        cur = jnp.where(hit, ninf, cur)
    cand = jnp.where(picked > 0.5, sel, ninf)
    e_i = lax.broadcasted_iota(jnp.int32, (ng, gsz, t), 0).astype(F32) * float(gsz) + i_in
    ws = []
    hits = jnp.zeros((ng, gsz, t), F32)
    for r in range(TOP_K):
        m = jnp.max(jnp.max(cand, axis=1, keepdims=True), axis=0, keepdims=True)
        idx = jnp.min(jnp.min(jnp.where(cand == m, e_i, float(N_EXPERTS)), axis=1, keepdims=True),
                      axis=0, keepdims=True)
        hit = e_i == idx
        w = jnp.sum(jnp.sum(jnp.where(hit, sc, 0.0), axis=1, keepdims=True), axis=0, keepdims=True)
        ti_ref[r:r + 1, :] = idx[0].astype(jnp.int32)
        ws.append(w[0])
        hits = jnp.where(hit, 1.0, hits)
        cand = jnp.where(hit, ninf, cand)

    @pl.when(pl.program_id(0) == 0)
    def _():
        cnt_ref[...] = jnp.zeros(cnt_ref.shape, F32)

    cnt_ref[...] += jnp.sum(hits, axis=2, keepdims=True)
    tot = ws[0]
    for w in ws[1:]:
        tot = tot + w
    for r in range(TOP_K):
        tw_ref[r:r + 1, :] = ws[r] / tot * ROUTED_SCALE
    for r in range(TOP_K, SUBLANES):
        ti_ref[r:r + 1, :] = jnp.zeros((1, t), jnp.int32)
        tw_ref[r:r + 1, :] = jnp.zeros((1, t), F32)


def _route(logits_t, router_bias):
    n = logits_t.shape[1]
    ng = N_EXPERT_GROUPS
    gsz = N_EXPERTS // ng
    tn = 1024
    return pl.pallas_call(
        _route_kernel,
        grid=(n // tn,),
        in_specs=[pl.BlockSpec((ng, gsz, tn), lambda i: (0, 0, i)),
                  pl.BlockSpec((ng, gsz, 1), lambda i: (0, 0, 0))],
        out_specs=[pl.BlockSpec((SUBLANES, tn), lambda i: (0, i)),
                   pl.BlockSpec((SUBLANES, tn), lambda i: (0, i)),
                   pl.BlockSpec((ng, gsz, 1), lambda i: (0, 0, 0))],
        out_shape=[jax.ShapeDtypeStruct((SUBLANES, n), jnp.int32),
                   jax.ShapeDtypeStruct((SUBLANES, n), F32),
                   jax.ShapeDtypeStruct((ng, gsz, 1), F32)],
        compiler_params=_cparams(("arbitrary",)),
        name="route",
    )(logits_t.reshape(ng, gsz, n), router_bias.reshape(ng, gsz, 1))


def _moe_kernel(blk_e_ref, nused_ref, nvalid_ref, nxt_e_ref, wslot_ref, t_ref, t1_ref, t2_ref, a_ref, h_hbm,
                wg_hbm, wu_hbm, wd_hbm, y_hbm, xbuf, ybuf, wg_buf, wu_buf, wd_buf, wg_s, wu_s, wd_s,
                gsem, wsem, wtsem):
    j = pl.program_id(0)
    n_used = nused_ref[0]
    slot = lax.rem(j, GATHER_SLOTS)
    nv = nvalid_ref[j]
    nv_prev = nvalid_ref[jnp.maximum(j - 1, 0)]

    def slab(ref, row0, align=SUBLANES):
        return ref.at[pl.ds(pl.multiple_of(row0, align), SLAB), :]

    def start_gather(idx_ref, s):
        def body(r, _):
            pltpu.make_async_copy(slab(h_hbm, idx_ref[0, 0, r]), slab(xbuf.at[s], r * MOE_PITCH, PITCH_ALIGN),
                                  gsem.at[s]).start()
            return 0
        lax.fori_loop(0, MOE_TB, body, 0, unroll=8)

    def wait_gather(s):
        rows = MOE_TB * SLAB
        pltpu.make_async_copy(h_hbm.at[pl.ds(0, rows), :], xbuf.at[s, pl.ds(0, rows), :], gsem.at[s]).wait()

    def weight_copies(expert, s):
        return (pltpu.make_async_copy(wg_hbm.at[expert], wg_buf.at[s], wtsem.at[s]),
                pltpu.make_async_copy(wu_hbm.at[expert], wu_buf.at[s], wtsem.at[s]),
                pltpu.make_async_copy(wd_hbm.at[expert], wd_buf.at[s], wtsem.at[s]))

    def wait_writes(count):
        p = MOE_TB
        while p >= 1:
            @pl.when((count & p) != 0)
            def _(p=p):
                rows = p * SLAB
                pltpu.make_async_copy(ybuf.at[pl.ds(0, rows), :], y_hbm.at[pl.ds(0, rows), :], wsem.at[0]).wait()
            p //= 2

    @pl.when(j < n_used)
    def _():
        @pl.when(j == 0)
        def _():
            start_gather(t_ref, 0)

            @pl.when(n_used > 1)
            def _():
                start_gather(t1_ref, 1)

        e = blk_e_ref[j]
        ws = wslot_ref[j]

        @pl.when(j == 0)
        def _():
            for cp in weight_copies(e, ws):
                cp.start(priority=WEIGHT_DMA_PRIORITY)

        first = jnp.logical_or(j == 0, e != blk_e_ref[jnp.maximum(j - 1, 0)])

        @pl.when(first)
        def _():
            for cp in weight_copies(e, ws):
                cp.wait()
            wg_s[...] = wg_buf[ws].astype(BF16)
            wu_s[...] = wu_buf[ws].astype(BF16)
            wd_s[...] = wd_buf[ws].astype(BF16)

        wait_gather(slot)

        @pl.when(j + 2 < n_used)
        def _():
            start_gather(t2_ref, lax.rem(j + 2, GATHER_SLOTS))

        @pl.when(jnp.logical_and(first, nxt_e_ref[j] >= 0))
        def _():
            for cp in weight_copies(nxt_e_ref[j], 1 - ws):
                cp.start(priority=WEIGHT_DMA_PRIORITY)

        xs = xbuf.at[slot]
        x = jnp.concatenate([xs[pl.ds(s, MOE_TB, stride=MOE_PITCH), :] for s in range(SLAB)], axis=1).astype(BF16)
        g = jnp.dot(x, wg_s[...], preferred_element_type=F32)
        u = jnp.dot(x, wu_s[...], preferred_element_type=F32)
        hm = (g * jax.nn.sigmoid(g) * u).astype(BF16)
        y = jnp.dot(hm, wd_s[...], preferred_element_type=F32)

        @pl.when(j > 0)
        def _():
            wait_writes(nv_prev)

        for s in range(SLAB):
            ybuf[pl.ds(s, MOE_TB, stride=MOE_PITCH), :] = y[:, s * LANES:(s + 1) * LANES]

        def start_write(r):
            pltpu.make_async_copy(slab(ybuf, r * MOE_PITCH, PITCH_ALIGN), slab(y_hbm, a_ref[0, 0, r]),
                                  wsem.at[0]).start()

        def wgroup(g8, _):
            for q in range(SUBLANES):
                start_write(g8 * SUBLANES + q)
            return 0

        def wtail(r, _):
            start_write(r)
            return 0

        n_groups = lax.shift_right_logical(nv, 3)
        lax.fori_loop(0, n_groups, wgroup, 0)
        lax.fori_loop(n_groups * SUBLANES, nv, wtail, 0)

        @pl.when(j == n_used - 1)
        def _():
            wait_writes(nv)


def _moe(h2s, topi, counts, w_gate, w_up, w_down):
    n = h2s.shape[0] // SLAB
    d = D_MODEL
    nk = n * TOP_K
    tb = MOE_TB
    n_blocks = nk // tb + N_EXPERTS
    flat_e = topi[:TOP_K].reshape(nk)
    assert nk < SORT_RADIX
    order = jnp.sort(flat_e * SORT_RADIX + jnp.arange(nk, dtype=jnp.int32)) % SORT_RADIX
    start = jnp.cumsum(counts) - counts
    padded = (counts + tb - 1) // tb * tb
    pad_end = jnp.cumsum(padded)
    pad_start = pad_end - padded
    n_used = (pad_end[-1] // tb).astype(jnp.int32).reshape(1)
    blk_p0 = jnp.arange(n_blocks, dtype=jnp.int32) * tb
    blk_e = jnp.minimum(jnp.sum((pad_end[None, :] <= blk_p0[:, None]).astype(jnp.int32), axis=1), N_EXPERTS - 1)
    blk_off = blk_p0 - pad_start[blk_e]
    nvalid = jnp.clip(counts[blk_e] - blk_off, 0, tb).astype(jnp.int32)
    rank0 = jnp.clip(start[blk_e] + blk_off, 0, nk - 1)
    order_p = jnp.concatenate([order, jnp.broadcast_to(order[-1:], (tb,))])
    a3 = jax.vmap(lambda r0: lax.dynamic_slice(order_p, (r0,), (tb,)))(rank0).reshape(n_blocks, 1, tb)
    arow = a3 * SLAB
    trow = (a3 % n) * SLAB
    used = counts > 0
    e_ids = jnp.arange(N_EXPERTS, dtype=jnp.int32)
    later = jnp.where(used, e_ids, N_EXPERTS)
    nxt = jnp.concatenate([lax.cummin(later[::-1])[::-1][1:], jnp.full((1,), N_EXPERTS, jnp.int32)])
    nxt_e = jnp.where(nxt < N_EXPERTS, nxt, -1)[blk_e].astype(jnp.int32)
    wslot = ((jnp.cumsum(used.astype(jnp.int32)) - 1) % 2)[blk_e].astype(jnp.int32)

    smem_blk = lambda f: pl.BlockSpec((1, 1, tb), f, memory_space=pltpu.SMEM)
    any_spec = pl.BlockSpec(memory_space=pl.ANY)
    grid_spec = pltpu.PrefetchScalarGridSpec(
        num_scalar_prefetch=5,
        grid=(n_blocks,),
        in_specs=[smem_blk(lambda j, *_: (j, 0, 0)),
                  smem_blk(lambda j, *_: (jnp.minimum(j + 1, n_blocks - 1), 0, 0)),
                  smem_blk(lambda j, *_: (jnp.minimum(j + 2, n_blocks - 1), 0, 0)),
                  smem_blk(lambda j, *_: (j, 0, 0)),
                  any_spec, any_spec, any_spec, any_spec],
        out_specs=any_spec,
        scratch_shapes=[pltpu.VMEM((GATHER_SLOTS, tb * MOE_PITCH, LANES), F32),
                        pltpu.VMEM((tb * MOE_PITCH, LANES), F32),
                        pltpu.VMEM((2, d, D_EXPERT), F32),
                        pltpu.VMEM((2, d, D_EXPERT), F32),
                        pltpu.VMEM((2, D_EXPERT, d), F32),
                        pltpu.VMEM((d, D_EXPERT), BF16),
                        pltpu.VMEM((d, D_EXPERT), BF16),
                        pltpu.VMEM((D_EXPERT, d), BF16),
                        pltpu.SemaphoreType.DMA((GATHER_SLOTS,)),
                        pltpu.SemaphoreType.DMA((1,)),
                        pltpu.SemaphoreType.DMA((2,))],
    )
    return pl.pallas_call(
        _moe_kernel,
        grid_spec=grid_spec,
        out_shape=jax.ShapeDtypeStruct((TOP_K * n * SLAB, LANES), F32),
        compiler_params=_cparams(("arbitrary",)),
        name="moe",
    )(blk_e, n_used, nvalid, nxt_e, wslot, trow, trow, trow, arow, h2s, w_gate, w_up, w_down)


def _final_kernel(h_ref, y0, y1, y2, y3, y4, y5, tw_ref, x1_ref, mod_ref, g_ref, wg_ref, wu_ref, wd_ref, o_ref):
    h = h_ref[...]
    tm = h.shape[0]
    g = jnp.dot(h, wg_ref[...], preferred_element_type=F32)
    u = jnp.dot(h, wu_ref[...], preferred_element_type=F32)
    hm = (g * jax.nn.sigmoid(g) * u).astype(BF16)
    ffn = jnp.dot(hm, wd_ref[...], preferred_element_type=F32)
    tw = tw_ref[...]
    for k, y in enumerate((y0, y1, y2, y3, y4, y5)):
        yk = jnp.concatenate([y[pl.ds(s, tm, stride=SLAB), :] for s in range(SLAB)], axis=1)
        ffn = ffn + yk * tw[:, k:k + 1]
    o_ref[0] = x1_ref[0] + mod_ref[0, 5:6, :] * _rms(ffn, g_ref[...])


def _final(h2, y_flat, topw_t, x1, mod, g_post_ffn, w_sg, w_su, w_sd):
    bn, l, d = x1.shape
    n = bn * l
    tm = 256
    nl = l // tm
    const = lambda b, i: (0, 0)
    one = pl.Buffered(1)
    y_specs = [pl.BlockSpec((tm * SLAB, LANES), functools.partial(lambda b, i, k: (k * (n // tm) + b * nl + i, 0), k=k))
               for k in range(TOP_K)]
    return pl.pallas_call(
        _final_kernel,
        grid=(bn, nl),
        in_specs=[pl.BlockSpec((tm, d), lambda b, i: (b * nl + i, 0))] + y_specs + [
            pl.BlockSpec((tm, SUBLANES), lambda b, i: (b * nl + i, 0)),
            pl.BlockSpec((1, tm, d), lambda b, i: (b, i, 0)),
            pl.BlockSpec((1, 6, d), lambda b, i: (b, 0, 0)),
            pl.BlockSpec((1, d), const),
            pl.BlockSpec((d, D_EXPERT), const, pipeline_mode=one),
            pl.BlockSpec((d, D_EXPERT), const, pipeline_mode=one),
            pl.BlockSpec((D_EXPERT, d), const, pipeline_mode=one)],
        out_specs=pl.BlockSpec((1, tm, d), lambda b, i: (b, i, 0)),
        out_shape=jax.ShapeDtypeStruct((bn, l, d), F32),
        compiler_params=_cparams(("arbitrary", "arbitrary")),
        name="final",
    )(h2, *([y_flat] * TOP_K), topw_t, x1, mod, g_post_ffn.reshape(1, d), w_sg, w_su, w_sd)


def _layer(li, x, c, positions, w_ada, b_ada, g_pre_mix, g_post_mix, g_pre_ffn, g_post_ffn, w_in, g_q, g_kv,
           w_uq, w_uk, w_uv, a_re, a_im, log_dt, b_re, b_im, c_re, c_im, d_skip, w_glu, w_br_mla, w_br_s5,
           w_out, w_router, router_bias, w_exp_gate, w_exp_up, w_exp_down, w_sh_gate, w_sh_up, w_sh_down):
    bn, l, d = x.shape
    n = bn * l
    mod = _adaln(c, w_ada, b_ada).reshape(bn, 6, d)

    o_kpe = Q_LORA + KV_LORA
    o_u = o_kpe + QK_ROPE
    o_g = o_u + S5_CH
    w_in_t = jnp.swapaxes(w_in, 1, 2)
    w_q = jnp.pad(w_uq.reshape(Q_LORA, MLA_HEADS, QK_NOPE + QK_ROPE),
                  ((0, 0), (0, 0), (0, QK_PAD - QK_NOPE - QK_ROPE))).reshape(Q_LORA, MLA_HEADS * QK_PAD).astype(BF16)
    wr_t = w_router.T
    wr_hi = wr_t.astype(BF16)
    wr_lo = (wr_t - wr_hi.astype(F32)).astype(BF16)

    rope_c, rope_s1, rope_s2 = _rope_tables(positions)
    chunk_id = positions // CHUNK

    h, lat = _prenorm_lat(x, mod, g_pre_mix, 0, 1, w_in_t, li, o_kpe + LANES)
    h = h.reshape(n, d)
    gates = _mmt(h, w_in_t, li, o_g, 2 * d, BF16, 1024, 1024, act="sigmoid", name="mm_gates")
    u6 = _mm_u(h, w_in_t, li, o_u, bn, l)

    q, k, v = _qkvproj(lat, g_q, g_kv, w_q, w_uk.astype(BF16), w_uv.astype(BF16), rope_c, rope_s1, rope_s2)
    att = _attention(q, k, v, chunk_id, bn, l)

    wb, ab, wc, dd = _s5_params(a_re, a_im, log_dt, b_re, b_im, c_re, c_im, d_skip)
    y6 = _s5(u6, wb, ab, wc, dd, l)
    z = _glu(y6, w_glu.astype(BF16), bn, l).reshape(n, S5_CH)

    x1, h2b, h2s, logits_t = _merge(att, z, gates, x, mod, g_post_mix, g_pre_ffn, w_br_mla.astype(BF16),
                                    w_br_s5.astype(BF16), w_out.astype(BF16), wr_hi, wr_lo)

    topi, topw, cnt = _route(logits_t, router_bias)
    y_slabs = _moe(h2s, topi, cnt.reshape(N_EXPERTS).astype(jnp.int32), w_exp_gate, w_exp_up, w_exp_down)
    return _final(h2b.reshape(n, d), y_slabs, topw.T, x1, mod, g_post_ffn, w_sh_gate.astype(BF16),
                  w_sh_up.astype(BF16), w_sh_down.astype(BF16))


def kernel(x, c, positions, w_ada, b_ada, g_pre_mix, g_post_mix, g_pre_ffn, g_post_ffn, w_in, g_q, g_kv, w_uq, w_uk, w_uv, a_re, a_im, log_dt, b_re, b_im, c_re, c_im, d_skip, w_glu, w_br_mla, w_br_s5, w_out, w_router, router_bias, w_exp_gate, w_exp_up, w_exp_down, w_sh_gate, w_sh_up, w_sh_down):
    depth = w_ada.shape[0]
    for li in range(depth):
        x = _layer(li, x, c, positions, w_ada[li], b_ada[li], g_pre_mix[li], g_post_mix[li], g_pre_ffn[li],
                   g_post_ffn[li], w_in, g_q[li], g_kv[li], w_uq[li], w_uk[li], w_uv[li], a_re[li], a_im[li],
                   log_dt[li], b_re[li], b_im[li], c_re[li], c_im[li], d_skip[li], w_glu[li], w_br_mla[li],
                   w_br_s5[li], w_out[li], w_router[li], router_bias[li], w_exp_gate[li], w_exp_up[li],
                   w_exp_down[li], w_sh_gate[li], w_sh_up[li], w_sh_down[li])
    return x
```

```python
import functools

import jax
import jax.numpy as jnp
from jax import lax
from jax.experimental import pallas as pl
from jax.experimental.pallas import tpu as pltpu

F32 = jnp.float32
BF16 = jnp.bfloat16

D_MODEL = 2048
CHUNK = 64
EPS = 1e-6
MLA_HEADS = 8
QK_NOPE = 128
QK_ROPE = 64
V_HEAD = 128
Q_LORA = 512
KV_LORA = 512
ROPE_THETA = 10000.0
S5_CH = 1024
S5_GROUP = 16
S5_GROUPS = S5_CH // S5_GROUP
S5_STATE = 64
N_EXPERTS = 64
TOP_K = 6
N_EXPERT_GROUPS = 8
TOPK_GROUPS = 4
D_EXPERT = 512
ROUTED_SCALE = 2.5

LANES = 128
SUBLANES = 8
QK_PAD = 2 * LANES
VMEM_LIMIT = 56 * 1024 * 1024
NEG = -1e30
LOG2E = 1.4426950408889634

S5_GB = LANES // S5_GROUP
S5_NBLK = S5_CH // LANES
S5_PAIRS = S5_NBLK // 2
S5_LANES = S5_GB * S5_STATE
S5_PPS = 2
S5_TC = 256
S5_RB = 512

ATT_TQ = 256
ATT_TK = 256
MOE_TB = 256
SLAB = D_MODEL // LANES
MERGE_SUBTILES = 2
GATHER_SLOTS = 3
WEIGHT_DMA_PRIORITY = 1
MOE_PITCH = 20
PITCH_ALIGN = 4


def _cparams(sem):
    return pltpu.CompilerParams(dimension_semantics=sem, vmem_limit_bytes=VMEM_LIMIT)


def _rms(x, g):
    return x * lax.rsqrt(jnp.mean(x * x, axis=-1, keepdims=True) + EPS) * g


def _adaln_kernel(c_ref, w_ref, b_ref, o_ref):
    c = c_ref[...]
    a = (c * jax.nn.sigmoid(c)).astype(BF16)
    o_ref[...] = jnp.dot(a, w_ref[...].astype(BF16), preferred_element_type=F32) + b_ref[...]


def _adaln(c, w, b):
    bn, d = c.shape
    n = w.shape[1]
    tn = 1024
    return pl.pallas_call(
        _adaln_kernel,
        grid=(n // tn,),
        in_specs=[pl.BlockSpec((bn, d), lambda j: (0, 0)),
                  pl.BlockSpec((d, tn), lambda j: (0, j)),
                  pl.BlockSpec((1, tn), lambda j: (0, j))],
        out_specs=pl.BlockSpec((bn, tn), lambda j: (0, j)),
        out_shape=jax.ShapeDtypeStruct((bn, n), F32),
        compiler_params=_cparams(("arbitrary",)),
        name="adaln",
    )(c, w, b.reshape(1, n))


NT_DIMS = (((1,), (1,)), ((), ()))


def _prenorm_lat_kernel(x_ref, mod_ref, g_ref, w_ref, h_ref, lat_ref, w_scr, *, sh_row, sc_row):
    @pl.when(jnp.logical_and(pl.program_id(0) == 0, pl.program_id(1) == 0))
    def _():
        w_scr[...] = w_ref[...].astype(BF16)

    y = _rms(x_ref[0], g_ref[...])
    h = (y * (1.0 + mod_ref[0, sc_row:sc_row + 1, :]) + mod_ref[0, sh_row:sh_row + 1, :]).astype(BF16)
    h_ref[0] = h
    lat_ref[...] = lax.dot_general(h, w_scr[...], NT_DIMS, preferred_element_type=F32)


def _prenorm_lat(x, mod, g, sh_row, sc_row, w_t, layer, n_lat):
    bn, l, d = x.shape
    tl = 512
    nl = l // tl
    return pl.pallas_call(
        functools.partial(_prenorm_lat_kernel, sh_row=sh_row, sc_row=sc_row),
        grid=(bn, nl),
        in_specs=[pl.BlockSpec((1, tl, d), lambda b, i: (b, i, 0)),
                  pl.BlockSpec((1, 6, d), lambda b, i: (b, 0, 0)),
                  pl.BlockSpec((1, d), lambda b, i: (0, 0)),
                  pl.BlockSpec((None, n_lat, d), lambda b, i: (layer, 0, 0), pipeline_mode=pl.Buffered(1))],
        out_specs=[pl.BlockSpec((1, tl, d), lambda b, i: (b, i, 0)),
                   pl.BlockSpec((tl, n_lat), lambda b, i: (b * nl + i, 0))],
        out_shape=[jax.ShapeDtypeStruct((bn, l, d), BF16),
                   jax.ShapeDtypeStruct((bn * l, n_lat), F32)],
        scratch_shapes=[pltpu.VMEM((n_lat, d), BF16)],
        compiler_params=_cparams(("arbitrary", "arbitrary")),
        name="prenorm_lat",
    )(x, mod, g.reshape(1, d), w_t)


def _wt_tile(wa_ref, wb_ref, shift, tn):
    if shift == 0:
        return wa_ref[...].astype(BF16)
    return jnp.concatenate([wa_ref[...], wb_ref[...]], axis=0)[shift:shift + tn].astype(BF16)


def _wt_specs(layer, row0, tn, k, jmap):
    shift = row0 % LANES
    c0 = row0 - shift
    assert c0 % tn == 0 and shift % SUBLANES == 0
    specs = [pl.BlockSpec((None, tn, k), lambda *g: (layer, jmap(*g) + c0 // tn, 0))]
    if shift:
        specs.append(pl.BlockSpec((None, LANES, k), lambda *g: (layer, (c0 + (jmap(*g) + 1) * tn) // LANES, 0)))
    return specs, shift


def _mmt_kernel(a_ref, wa_ref, *rest, act, shift):
    wb_ref = rest[0] if shift else None
    o_ref, w_scr = rest[-2:]

    @pl.when(pl.program_id(1) == 0)
    def _():
        w_scr[...] = _wt_tile(wa_ref, wb_ref, shift, w_scr.shape[0])

    acc = lax.dot_general(a_ref[...], w_scr[...], NT_DIMS, preferred_element_type=F32)
    if act == "sigmoid":
        acc = jax.nn.sigmoid(acc)
    o_ref[...] = acc.astype(o_ref.dtype)


def _mmt(a, w_t, layer, row0, n, out_dtype, tm, tn, act=None, name="mmt"):
    m, k = a.shape
    w_specs, shift = _wt_specs(layer, row0, tn, k, lambda j, i: j)
    return pl.pallas_call(
        functools.partial(_mmt_kernel, act=act, shift=shift),
        grid=(n // tn, m // tm),
        in_specs=[pl.BlockSpec((tm, k), lambda j, i: (i, 0))] + w_specs,
        out_specs=pl.BlockSpec((tm, tn), lambda j, i: (i, j)),
        out_shape=jax.ShapeDtypeStruct((m, n), out_dtype),
        scratch_shapes=[pltpu.VMEM((tn, k), BF16)],
        compiler_params=_cparams(("arbitrary", "arbitrary")),
        name=name,
    )(a, *([w_t] * len(w_specs)))


def _mm_u_kernel(a_ref, wa_ref, *rest, shift):
    wb_ref = rest[0] if shift else None
    o_ref, w_scr = rest[-2:]
    b = pl.program_id(1)

    @pl.when(jnp.logical_and(pl.program_id(0) == 0, b == 0))
    def _():
        w_scr[...] = _wt_tile(wa_ref, wb_ref, shift, w_scr.shape[0])

    res = lax.dot_general(a_ref[...], w_scr[...], NT_DIMS, preferred_element_type=F32)
    tm = res.shape[0]
    for c in range(S5_NBLK):
        o_ref.at[c // 2][pl.ds(b * 2 + c % 2, tm, stride=SUBLANES), :] = res[:, c * LANES:(c + 1) * LANES]


def _mm_u(h, w_t, layer, row0, bn, l):
    m, k = h.shape
    tm = 512
    nl = l // tm
    assert bn * 2 == SUBLANES
    w_specs, shift = _wt_specs(layer, row0, S5_CH, k, lambda i, b: 0)
    return pl.pallas_call(
        functools.partial(_mm_u_kernel, shift=shift),
        grid=(nl, bn),
        in_specs=[pl.BlockSpec((tm, k), lambda i, b: (b * nl + i, 0))] + w_specs,
        out_specs=pl.BlockSpec((S5_PAIRS, tm * SUBLANES, LANES), lambda i, b: (0, i, 0)),
        out_shape=jax.ShapeDtypeStruct((S5_PAIRS, l * SUBLANES, LANES), F32),
        scratch_shapes=[pltpu.VMEM((S5_CH, k), BF16)],
        compiler_params=_cparams(("arbitrary", "arbitrary")),
        name="mm_u",
    )(h, *([w_t] * len(w_specs)))


def _rope_tab_kernel(pos_ref, k_ref, c_ref, s1_ref, s2_ref):
    ang = pos_ref[...].astype(F32) * k_ref[0:1, :]
    s = jnp.sin(ang)
    c_ref[...] = jnp.cos(ang) * k_ref[1:2, :]
    s1_ref[...] = s * k_ref[2:3, :]
    s2_ref[...] = s * k_ref[3:4, :]


def _rope_tables(positions):
    n = positions.size
    half = QK_ROPE // 2
    inv_freq = ROPE_THETA ** (-jnp.arange(half, dtype=F32) / half)
    zh, oh = jnp.zeros((half,), F32), jnp.ones((half,), F32)
    z2 = jnp.zeros((LANES - QK_ROPE,), F32)
    rows = [jnp.concatenate([inv_freq, inv_freq, z2]), jnp.concatenate([oh, oh, z2]),
            jnp.concatenate([-oh, zh, z2]), jnp.concatenate([zh, oh, z2])]
    consts = jnp.stack(rows + [jnp.zeros((LANES,), F32)] * (SUBLANES - len(rows)))
    tm = 1024
    tab = jax.ShapeDtypeStruct((n, LANES), F32)
    return pl.pallas_call(
        _rope_tab_kernel,
        grid=(n // tm,),
        in_specs=[pl.BlockSpec((tm, 1), lambda i: (i, 0)),
                  pl.BlockSpec((SUBLANES, LANES), lambda i: (0, 0))],
        out_specs=[pl.BlockSpec((tm, LANES), lambda i: (i, 0))] * 3,
        out_shape=[tab, tab, tab],
        compiler_params=_cparams(("arbitrary",)),
        name="rope_tables",
    )(positions.reshape(n, 1), consts)


def _rope_tile(t, c_ref, s1_ref, s2_ref):
    return (t * c_ref[...] + pltpu.roll(t, LANES - QK_ROPE // 2, 1) * s1_ref[...]
            + pltpu.roll(t, QK_ROPE // 2, 1) * s2_ref[...])


def _qkvproj_kernel(lat_ref, gq_ref, gkv_ref, wq_ref, wk_ref, wv_ref, kpe_ref, c_ref, s1_ref, s2_ref,
                    q_ref, k_ref, v_ref, *, scale):
    lat = lat_ref[...]
    qn = _rms(lat[:, :Q_LORA], gq_ref[...]).astype(BF16)
    cn = _rms(lat[:, Q_LORA:], gkv_ref[...]).astype(BF16)
    q = jnp.dot(qn, wq_ref[...], preferred_element_type=F32)
    kn = jnp.dot(cn, wk_ref[...], preferred_element_type=F32)
    v_ref[...] = jnp.dot(cn, wv_ref[...], preferred_element_type=F32).astype(v_ref.dtype)
    kt = _rope_tile(kpe_ref[...], c_ref, s1_ref, s2_ref).astype(k_ref.dtype)
    for h in range(MLA_HEADS):
        o = h * QK_PAD
        q_ref[:, o:o + LANES] = (q[:, o:o + LANES] * scale).astype(q_ref.dtype)
        qt = _rope_tile(q[:, o + LANES:o + QK_PAD], c_ref, s1_ref, s2_ref)
        q_ref[:, o + LANES:o + QK_PAD] = (qt * scale).astype(q_ref.dtype)
        k_ref[:, o:o + LANES] = kn[:, h * QK_NOPE:(h + 1) * QK_NOPE].astype(k_ref.dtype)
        k_ref[:, o + LANES:o + QK_PAD] = kt


def _qkvproj(lat, g_q, g_kv, w_q, w_k, w_v, rope_c, rope_s1, rope_s2):
    n = lat.shape[0]
    tm = 512
    row = lambda i: (i, 0)
    const = lambda i: (0, 0)
    tab = pl.BlockSpec((tm, LANES), row)
    kpe_spec = pl.BlockSpec((tm, LANES), lambda i: (i, (Q_LORA + KV_LORA) // LANES))
    return pl.pallas_call(
        functools.partial(_qkvproj_kernel, scale=(QK_NOPE + QK_ROPE) ** -0.5 * LOG2E),
        grid=(n // tm,),
        in_specs=[pl.BlockSpec((tm, Q_LORA + KV_LORA), row),
                  pl.BlockSpec((1, Q_LORA), const),
                  pl.BlockSpec((1, KV_LORA), const),
                  pl.BlockSpec((Q_LORA, MLA_HEADS * QK_PAD), const),
                  pl.BlockSpec((KV_LORA, MLA_HEADS * QK_NOPE), const),
                  pl.BlockSpec((KV_LORA, MLA_HEADS * V_HEAD), const),
                  kpe_spec, tab, tab, tab],
        out_specs=[pl.BlockSpec((tm, MLA_HEADS * QK_PAD), row),
                   pl.BlockSpec((tm, MLA_HEADS * QK_PAD), row),
                   pl.BlockSpec((tm, MLA_HEADS * V_HEAD), row)],
        out_shape=[jax.ShapeDtypeStruct((n, MLA_HEADS * QK_PAD), BF16),
                   jax.ShapeDtypeStruct((n, MLA_HEADS * QK_PAD), BF16),
                   jax.ShapeDtypeStruct((n, MLA_HEADS * V_HEAD), BF16)],
        compiler_params=_cparams(("arbitrary",)),
        name="qkvproj",
    )(lat, g_q.reshape(1, Q_LORA), g_kv.reshape(1, KV_LORA), w_q, w_k, w_v, lat, rope_c, rope_s1, rope_s2)


def _attn_kernel(lo_ref, hi_ref, q_ref, k_ref, v_ref, qc_ref, kc_ref, o_ref, m_scr, l_scr, acc_scr, *, nq):
    b = pl.program_id(0)
    i = pl.program_id(1)
    qc = qc_ref[...]
    m_scr[...] = jnp.full(m_scr.shape, NEG, F32)
    l_scr[...] = jnp.zeros(l_scr.shape, F32)
    acc_scr[...] = jnp.zeros(acc_scr.shape, F32)

    def make_body(masked):
        def body(j, _):
            off = pl.multiple_of(j * ATT_TK, ATT_TK)
            if masked:
                mask = kc_ref[j] <= qc
            for h in range(MLA_HEADS):
                q = q_ref[:, h * QK_PAD:(h + 1) * QK_PAD]
                k = k_ref[pl.ds(off, ATT_TK), h * QK_PAD:(h + 1) * QK_PAD]
                s = lax.dot_general(q, k, (((1,), (1,)), ((), ())), preferred_element_type=F32)
                if masked:
                    s = jnp.where(mask, s, NEG)
                m_old = m_scr[h]
                m_new = jnp.maximum(m_old, jnp.max(s, axis=-1, keepdims=True))
                p = jnp.exp2(s - jnp.concatenate([m_new] * (ATT_TK // LANES), axis=1))
                alpha = jnp.exp2(m_old - m_new)
                l_scr[h] = alpha * l_scr[h] + jnp.sum(p, axis=-1, keepdims=True)
                v = v_ref[pl.ds(off, ATT_TK), h * V_HEAD:(h + 1) * V_HEAD]
                acc_scr[h] = alpha * acc_scr[h] + jnp.dot(p.astype(BF16), v, preferred_element_type=F32)
                m_scr[h] = m_new
            return 0
        return body

    lo = lo_ref[b * nq + i]
    lax.fori_loop(0, lo, make_body(False), 0)
    lax.fori_loop(lo, hi_ref[b * nq + i], make_body(True), 0)
    for h in range(MLA_HEADS):
        o_ref[:, h * V_HEAD:(h + 1) * V_HEAD] = (acc_scr[h] / l_scr[h]).astype(o_ref.dtype)


def _attention(q, k, v, chunk_id, bn, l):
    nq = l // ATT_TQ
    nk = l // ATT_TK
    q_max = jnp.max(chunk_id.reshape(bn, nq, ATT_TQ), axis=-1)
    k_min = jnp.min(chunk_id.reshape(bn, nk, ATT_TK), axis=-1)
    needed = k_min[:, None, :] <= q_max[:, :, None]
    hi = jnp.max(jnp.where(needed, jnp.arange(1, nk + 1, dtype=jnp.int32), 0), axis=-1).reshape(bn * nq)
    q_min = jnp.min(chunk_id.reshape(bn, nq, ATT_TQ), axis=-1)
    k_max = jnp.max(chunk_id.reshape(bn, nk, ATT_TK), axis=-1)
    full = k_max[:, None, :] <= q_min[:, :, None]
    lo = jnp.min(jnp.where(full, nk, jnp.arange(nk, dtype=jnp.int32)), axis=-1).astype(jnp.int32).reshape(bn * nq)
    hi = jnp.maximum(hi, lo)
    qc = chunk_id.reshape(bn * l, 1)
    kc = chunk_id.reshape(bn * nk, 1, ATT_TK)
    grid_spec = pltpu.PrefetchScalarGridSpec(
        num_scalar_prefetch=2,
        grid=(bn, nq),
        in_specs=[pl.BlockSpec((ATT_TQ, MLA_HEADS * QK_PAD), lambda b, i, *_: (b * nq + i, 0)),
                  pl.BlockSpec((l, MLA_HEADS * QK_PAD), lambda b, i, *_: (b, 0)),
                  pl.BlockSpec((l, MLA_HEADS * V_HEAD), lambda b, i, *_: (b, 0)),
                  pl.BlockSpec((ATT_TQ, 1), lambda b, i, *_: (b * nq + i, 0)),
                  pl.BlockSpec((nk, 1, ATT_TK), lambda b, i, *_: (b, 0, 0))],
        out_specs=pl.BlockSpec((ATT_TQ, MLA_HEADS * V_HEAD), lambda b, i, *_: (b * nq + i, 0)),
        scratch_shapes=[pltpu.VMEM((MLA_HEADS, ATT_TQ, LANES), F32),
                        pltpu.VMEM((MLA_HEADS, ATT_TQ, LANES), F32),
                        pltpu.VMEM((MLA_HEADS, ATT_TQ, V_HEAD), F32)],
    )
    return pl.pallas_call(
        functools.partial(_attn_kernel, nq=nq),
        grid_spec=grid_spec,
        out_shape=jax.ShapeDtypeStruct((bn * l, MLA_HEADS * V_HEAD), BF16),
        compiler_params=_cparams(("arbitrary", "arbitrary")),
        name="attention",
    )(lo, hi, q, k, v, qc, kc)


def _s5_kernel(u_ref, wb_ref, a_ref, wc_ref, d_ref, o_ref, x_scr, st_scr):
    rows = S5_TC * SUBLANES
    nsub = rows // S5_RB

    @pl.when(pl.program_id(1) == 0)
    def _():
        st_scr[...] = jnp.zeros_like(st_scr)

    even = (lax.broadcasted_iota(jnp.int32, (S5_RB, 1), 0) & 1) == 0
    half = S5_RB // 2
    nre = S5_LANES // LANES
    nsl = 2 * nre

    def mm_in(r, _):
        off = pl.multiple_of(r * S5_RB, S5_RB)
        for p in range(S5_PPS):
            for s in range(2):
                us = u_ref.at[p][pl.ds(off + s, half, stride=2), :].astype(BF16)
                out = jnp.dot(us, wb_ref[p, :, s * 2 * S5_LANES:(s + 1) * 2 * S5_LANES],
                              preferred_element_type=F32)
                for c in range(nsl):
                    x_scr.at[p * nsl + c][pl.ds(off + s, half, stride=2), :] = out[:, c * LANES:(c + 1) * LANES]
        return 0

    lax.fori_loop(0, nsub, mm_in, 0)

    a_c = [a_ref[p, :, c * LANES:(c + 1) * LANES] for p in range(S5_PPS) for c in range(nsl)]

    def step(t, carry):
        off = pl.multiple_of(t * SUBLANES, SUBLANES)
        new = [None] * (S5_PPS * nsl)
        for p in range(S5_PPS):
            for c in range(nre):
                ir, ii = p * nsl + c, p * nsl + nre + c
                xr, xi = carry[ir], carry[ii]
                ar, ai = a_c[ir], a_c[ii]
                nr = ar * xr - ai * xi + x_scr[ir, pl.ds(off, SUBLANES), :]
                ni = ar * xi + ai * xr + x_scr[ii, pl.ds(off, SUBLANES), :]
                x_scr[ir, pl.ds(off, SUBLANES), :] = nr
                x_scr[ii, pl.ds(off, SUBLANES), :] = ni
                new[ir], new[ii] = nr, ni
        return tuple(new)

    init = tuple(st_scr[p, :, c * LANES:(c + 1) * LANES] for p in range(S5_PPS) for c in range(nsl))
    fin = lax.fori_loop(0, S5_TC, step, init, unroll=8)
    for p in range(S5_PPS):
        for c in range(nsl):
            st_scr[p, :, c * LANES:(c + 1) * LANES] = fin[p * nsl + c]

    def mm_out(r, _):
        off = pl.multiple_of(r * S5_RB, S5_RB)
        for p in range(S5_PPS):
            d = jnp.concatenate([d_ref[p]] * (S5_RB // SUBLANES), axis=0)
            x = jnp.concatenate([x_scr[p * nsl + c, pl.ds(off, S5_RB), :] for c in range(nsl)],
                                axis=1).astype(BF16)
            out = jnp.dot(x, wc_ref[p], preferred_element_type=F32)
            y = jnp.where(even, out[:, :LANES], out[:, LANES:]) + d * u_ref[p, pl.ds(off, S5_RB), :]
            o_ref[p, pl.ds(off, S5_RB), :] = jax.nn.gelu(y).astype(o_ref.dtype)
        return 0

    lax.fori_loop(0, nsub, mm_out, 0)


def _s5_params(a_re, a_im, log_dt, b_re, b_im, c_re, c_im, d_skip):
    step = jnp.exp(log_dt)[:, None]
    mag = jnp.exp(a_re * step)
    abar_re, abar_im = mag * jnp.cos(a_im * step), mag * jnp.sin(a_im * step)
    den = a_re * a_re + a_im * a_im
    nr, ni = abar_re - 1.0, abar_im
    f_re, f_im = (nr * a_re + ni * a_im) / den, (ni * a_re - nr * a_im) / den
    bbar_re = f_re[..., None] * b_re - f_im[..., None] * b_im
    bbar_im = f_re[..., None] * b_im + f_im[..., None] * b_re
    eye = jnp.eye(S5_GB, dtype=F32)
    bb = jnp.stack([bbar_re, bbar_im]).reshape(2, S5_NBLK, S5_GB, S5_STATE, S5_GROUP)
    wb = jnp.einsum('ab,rjapc->jacrbp', eye, bb).reshape(S5_NBLK, LANES, 2 * S5_LANES)
    wb = wb.reshape(S5_PAIRS, 2, LANES, 2 * S5_LANES).transpose(0, 2, 1, 3).reshape(S5_PAIRS, LANES, 4 * S5_LANES)
    cc = jnp.stack([c_re, -c_im]).reshape(2, S5_NBLK, S5_GB, S5_GROUP, S5_STATE)
    wc = jnp.einsum('ab,rjacp->jrapbc', eye, cc).reshape(S5_NBLK, 2 * S5_LANES, LANES)
    wc = wc.reshape(S5_PAIRS, 2, 2 * S5_LANES, LANES).transpose(0, 2, 1, 3).reshape(S5_PAIRS, 2 * S5_LANES, 2 * LANES)
    ab = jnp.concatenate([abar_re.reshape(S5_NBLK, S5_LANES), abar_im.reshape(S5_NBLK, S5_LANES)], axis=1)
    ab = jnp.tile(ab.reshape(S5_PAIRS, 2, 2 * S5_LANES), (1, SUBLANES // 2, 1))
    dd = jnp.tile(d_skip.reshape(S5_PAIRS, 2, LANES), (1, SUBLANES // 2, 1))
    return wb.astype(BF16), ab, wc.astype(BF16), dd


def _s5(u6, wb, ab, wc, dd, l):
    rows = S5_TC * SUBLANES
    return pl.pallas_call(
        _s5_kernel,
        grid=(S5_PAIRS // S5_PPS, l // S5_TC),
        in_specs=[pl.BlockSpec((S5_PPS, rows, LANES), lambda k, c: (k, c, 0)),
                  pl.BlockSpec((S5_PPS, LANES, 4 * S5_LANES), lambda k, c: (k, 0, 0)),
                  pl.BlockSpec((S5_PPS, SUBLANES, 2 * S5_LANES), lambda k, c: (k, 0, 0)),
                  pl.BlockSpec((S5_PPS, 2 * S5_LANES, 2 * LANES), lambda k, c: (k, 0, 0)),
                  pl.BlockSpec((S5_PPS, SUBLANES, LANES), lambda k, c: (k, 0, 0))],
        out_specs=pl.BlockSpec((S5_PPS, rows, LANES), lambda k, c: (k, c, 0)),
        out_shape=jax.ShapeDtypeStruct(u6.shape, F32),
        scratch_shapes=[pltpu.VMEM((S5_PPS * 2 * S5_LANES // LANES, rows, LANES), F32),
                        pltpu.VMEM((S5_PPS, SUBLANES, 2 * S5_LANES), F32)],
        compiler_params=_cparams(("arbitrary", "arbitrary")),
        name="s5",
    )(u6, wb, ab, wc, dd)


def _glu_kernel(y_ref, w_ref, o_ref):
    b = pl.program_id(1)
    tt = o_ref.shape[1]
    y = jnp.concatenate([y_ref.at[k][pl.ds(b * 2 + s, tt, stride=SUBLANES), :]
                         for k in range(S5_PAIRS) for s in range(2)], axis=1)
    g = jnp.dot(y.astype(BF16), w_ref[...], preferred_element_type=F32)
    o_ref[0] = (y * jax.nn.sigmoid(g)).astype(o_ref.dtype)


def _glu(y6, w_glu, bn, l):
    tt = 512
    assert bn * 2 == SUBLANES
    return pl.pallas_call(
        _glu_kernel,
        grid=(l // tt, bn),
        in_specs=[pl.BlockSpec((S5_PAIRS, tt * SUBLANES, LANES), lambda i, b: (0, i, 0)),
                  pl.BlockSpec((S5_CH, S5_CH), lambda i, b: (0, 0))],
        out_specs=pl.BlockSpec((1, tt, S5_CH), lambda i, b: (b, i, 0)),
        out_shape=jax.ShapeDtypeStruct((bn, l, S5_CH), BF16),
        compiler_params=_cparams(("arbitrary", "arbitrary")),
        name="glu",
    )(y6, w_glu)


def _merge_kernel(att_ref, z_ref, gm_ref, gs_ref, x_ref, mod_ref, gpost_ref, gpre_ref,
                  wbm_ref, wbs_ref, wo_ref, wrh_ref, wrl_ref, x1_ref, h2b_ref, h2s_ref, lg_ref):
    tm = x_ref.shape[1]
    th = tm // MERGE_SUBTILES
    for r0 in range(0, tm, th):
        rows = slice(r0, r0 + th)
        ym = jnp.dot(att_ref[rows, :], wbm_ref[...], preferred_element_type=F32)
        ys = jnp.dot(z_ref[rows, :], wbs_ref[...], preferred_element_type=F32)
        mixed_in = (gm_ref[rows, :].astype(F32) * ym + gs_ref[rows, :].astype(F32) * ys).astype(BF16)
        mixed = jnp.dot(mixed_in, wo_ref[...], preferred_element_type=F32)
        x1 = x_ref[0, rows, :] + mod_ref[0, 2:3, :] * _rms(mixed, gpost_ref[...])
        x1_ref[0, rows, :] = x1
        h2 = _rms(x1, gpre_ref[...]) * (1.0 + mod_ref[0, 4:5, :]) + mod_ref[0, 3:4, :]
        h2_hi = h2.astype(BF16)
        h2b_ref[0, rows, :] = h2_hi
        for s in range(SLAB):
            h2s_ref[pl.ds(r0 * SLAB + s, th, stride=SLAB), :] = h2[:, s * LANES:(s + 1) * LANES]
        h2_lo = (h2 - h2_hi.astype(F32)).astype(BF16)
        lg_ref[:, rows] = (lax.dot_general(wrh_ref[...], h2_hi, NT_DIMS, preferred_element_type=F32)
                           + lax.dot_general(wrl_ref[...], h2_hi, NT_DIMS, preferred_element_type=F32)
                           + lax.dot_general(wrh_ref[...], h2_lo, NT_DIMS, preferred_element_type=F32))


def _merge(att, z_tm, gates, x, mod, g_post, g_pre_ffn, w_br_mla, w_br_s5, w_out, wr_hi, wr_lo):
    bn, l, d = x.shape
    tm = 256
    nl = l // tm
    row = lambda b, i: (b * nl + i, 0)
    const = lambda b, i: (0, 0)
    one = pl.Buffered(1)
    return pl.pallas_call(
        _merge_kernel,
        grid=(bn, nl),
        in_specs=[pl.BlockSpec((tm, MLA_HEADS * V_HEAD), row),
                  pl.BlockSpec((tm, S5_CH), row),
                  pl.BlockSpec((tm, d), lambda b, i: (b * nl + i, 0)),
                  pl.BlockSpec((tm, d), lambda b, i: (b * nl + i, 1)),
                  pl.BlockSpec((1, tm, d), lambda b, i: (b, i, 0)),
                  pl.BlockSpec((1, 6, d), lambda b, i: (b, 0, 0)),
                  pl.BlockSpec((1, d), const),
                  pl.BlockSpec((1, d), const),
                  pl.BlockSpec((MLA_HEADS * V_HEAD, d), const, pipeline_mode=one),
                  pl.BlockSpec((S5_CH, d), const, pipeline_mode=one),
                  pl.BlockSpec((d, d), const, pipeline_mode=one),
                  pl.BlockSpec((N_EXPERTS, d), const, pipeline_mode=one),
                  pl.BlockSpec((N_EXPERTS, d), const, pipeline_mode=one)],
        out_specs=[pl.BlockSpec((1, tm, d), lambda b, i: (b, i, 0)),
                   pl.BlockSpec((1, tm, d), lambda b, i: (b, i, 0)),
                   pl.BlockSpec((tm * SLAB, LANES), row),
                   pl.BlockSpec((N_EXPERTS, tm), lambda b, i: (0, b * nl + i))],
        out_shape=[jax.ShapeDtypeStruct((bn, l, d), F32),
                   jax.ShapeDtypeStruct((bn, l, d), BF16),
                   jax.ShapeDtypeStruct((bn * l * SLAB, LANES), F32),
                   jax.ShapeDtypeStruct((N_EXPERTS, bn * l), F32)],
        compiler_params=_cparams(("arbitrary", "arbitrary")),
        name="merge",
    )(att, z_tm, gates, gates, x, mod, g_post.reshape(1, d), g_pre_ffn.reshape(1, d),
      w_br_mla, w_br_s5, w_out, wr_hi, wr_lo)


def _route_kernel(lg_ref, b_ref, ti_ref, tw_ref, cnt_ref):
    ng = N_EXPERT_GROUPS
    gsz = N_EXPERTS // ng
    lg = lg_ref[...]
    t = lg.shape[-1]
    sc = jax.nn.sigmoid(lg)
    sel = sc + b_ref[...]
    ninf = -jnp.inf
    i_in = lax.broadcasted_iota(jnp.int32, (ng, gsz, t), 1).astype(F32)
    m1 = jnp.max(sel, axis=1, keepdims=True)
    idx1 = jnp.min(jnp.where(sel == m1, i_in, float(gsz)), axis=1, keepdims=True)
    m2 = jnp.max(jnp.where(i_in == idx1, ninf, sel), axis=1, keepdims=True)
    gs = m1 + m2
    g_i = lax.broadcasted_iota(jnp.int32, (ng, 1, t), 0).astype(F32)
    picked = jnp.zeros((ng, 1, t), F32)
    cur = gs
    for _ in range(TOPK_GROUPS):
        m = jnp.max(cur, axis=0, keepdims=True)
        idx = jnp.min(jnp.where(cur == m, g_i, float(ng)), axis=0, keepdims=True)
        hit = g_i == idx
        picked = jnp.where(hit, 1.0, picked)
        cur = jnp.where(hit, ninf, cur)
    cand = jnp.where(picked > 0.5, sel, ninf)
    e_i = lax.broadcasted_iota(jnp.int32, (ng, gsz, t), 0).astype(F32) * float(gsz) + i_in
    ws = []
    hits = jnp.zeros((ng, gsz, t), F32)
    for r in range(TOP_K):
        m = jnp.max(jnp.max(cand, axis=1, keepdims=True), axis=0, keepdims=True)
        idx = jnp.min(jnp.min(jnp.where(cand == m, e_i, float(N_EXPERTS)), axis=1, keepdims=True),
                      axis=0, keepdims=True)
        hit = e_i == idx
        w = jnp.sum(jnp.sum(jnp.where(hit, sc, 0.0), axis=1, keepdims=True), axis=0, keepdims=True)
        ti_ref[r:r + 1, :] = idx[0].astype(jnp.int32)
        ws.append(w[0])
        hits = jnp.where(hit, 1.0, hits)
        cand = jnp.where(hit, ninf, cand)

    @pl.when(pl.program_id(0) == 0)
    def _():
        cnt_ref[...] = jnp.zeros(cnt_ref.shape, F32)

    cnt_ref[...] += jnp.sum(hits, axis=2, keepdims=True)
    tot = ws[0]
    for w in ws[1:]:
        tot = tot + w
    for r in range(TOP_K):
        tw_ref[r:r + 1, :] = ws[r] / tot * ROUTED_SCALE
    for r in range(TOP_K, SUBLANES):
        ti_ref[r:r + 1, :] = jnp.zeros((1, t), jnp.int32)
        tw_ref[r:r + 1, :] = jnp.zeros((1, t), F32)


def _route(logits_t, router_bias):
    n = logits_t.shape[1]
    ng = N_EXPERT_GROUPS
    gsz = N_EXPERTS // ng
    tn = 1024
    return pl.pallas_call(
        _route_kernel,
        grid=(n // tn,),
        in_specs=[pl.BlockSpec((ng, gsz, tn), lambda i: (0, 0, i)),
                  pl.BlockSpec((ng, gsz, 1), lambda i: (0, 0, 0))],
        out_specs=[pl.BlockSpec((SUBLANES, tn), lambda i: (0, i)),
                   pl.BlockSpec((SUBLANES, tn), lambda i: (0, i)),
                   pl.BlockSpec((ng, gsz, 1), lambda i: (0, 0, 0))],
        out_shape=[jax.ShapeDtypeStruct((SUBLANES, n), jnp.int32),
                   jax.ShapeDtypeStruct((SUBLANES, n), F32),
                   jax.ShapeDtypeStruct((ng, gsz, 1), F32)],
        compiler_params=_cparams(("arbitrary",)),
        name="route",
    )(logits_t.reshape(ng, gsz, n), router_bias.reshape(ng, gsz, 1))


def _moe_kernel(blk_e_ref, nused_ref, nvalid_ref, nxt_e_ref, wslot_ref, t_ref, t1_ref, t2_ref, a_ref, h_hbm,
                wg_hbm, wu_hbm, wd_hbm, y_hbm, xbuf, ybuf, wg_buf, wu_buf, wd_buf, wg_s, wu_s, wd_s,
                gsem, wsem, wtsem):
    j = pl.program_id(0)
    n_used = nused_ref[0]
    slot = lax.rem(j, GATHER_SLOTS)
    nv = nvalid_ref[j]
    nv_prev = nvalid_ref[jnp.maximum(j - 1, 0)]

    def slab(ref, row0, align=SUBLANES):
        return ref.at[pl.ds(pl.multiple_of(row0, align), SLAB), :]

    def start_gather(idx_ref, s):
        def body(r, _):
            pltpu.make_async_copy(slab(h_hbm, idx_ref[0, 0, r]), slab(xbuf.at[s], r * MOE_PITCH, PITCH_ALIGN),
                                  gsem.at[s]).start()
            return 0
        lax.fori_loop(0, MOE_TB, body, 0, unroll=8)

    def wait_gather(s):
        rows = MOE_TB * SLAB
        pltpu.make_async_copy(h_hbm.at[pl.ds(0, rows), :], xbuf.at[s, pl.ds(0, rows), :], gsem.at[s]).wait()

    def weight_copies(expert, s):
        return (pltpu.make_async_copy(wg_hbm.at[expert], wg_buf.at[s], wtsem.at[s]),
                pltpu.make_async_copy(wu_hbm.at[expert], wu_buf.at[s], wtsem.at[s]),
                pltpu.make_async_copy(wd_hbm.at[expert], wd_buf.at[s], wtsem.at[s]))

    def wait_writes(count):
        p = MOE_TB
        while p >= 1:
            @pl.when((count & p) != 0)
            def _(p=p):
                rows = p * SLAB
                pltpu.make_async_copy(ybuf.at[pl.ds(0, rows), :], y_hbm.at[pl.ds(0, rows), :], wsem.at[0]).wait()
            p //= 2

    @pl.when(j < n_used)
    def _():
        @pl.when(j == 0)
        def _():
            start_gather(t_ref, 0)

            @pl.when(n_used > 1)
            def _():
                start_gather(t1_ref, 1)

        e = blk_e_ref[j]
        ws = wslot_ref[j]

        @pl.when(j == 0)
        def _():
            for cp in weight_copies(e, ws):
                cp.start(priority=WEIGHT_DMA_PRIORITY)

        first = jnp.logical_or(j == 0, e != blk_e_ref[jnp.maximum(j - 1, 0)])

        @pl.when(first)
        def _():
            for cp in weight_copies(e, ws):
                cp.wait()
            wg_s[...] = wg_buf[ws].astype(BF16)
            wu_s[...] = wu_buf[ws].astype(BF16)
            wd_s[...] = wd_buf[ws].astype(BF16)

        wait_gather(slot)

        @pl.when(j + 2 < n_used)
        def _():
            start_gather(t2_ref, lax.rem(j + 2, GATHER_SLOTS))

        @pl.when(jnp.logical_and(first, nxt_e_ref[j] >= 0))
        def _():
            for cp in weight_copies(nxt_e_ref[j], 1 - ws):
                cp.start(priority=WEIGHT_DMA_PRIORITY)

        @pl.when(j > 0)
        def _():
            wait_writes(nv_prev)

        def expert_rows(rows):
            xs = xbuf.at[slot]
            x = jnp.concatenate([xs[pl.ds(s, rows, stride=MOE_PITCH), :] for s in range(SLAB)], axis=1).astype(BF16)
            g = jnp.dot(x, wg_s[...], preferred_element_type=F32)
            u = jnp.dot(x, wu_s[...], preferred_element_type=F32)
            hm = (g * jax.nn.sigmoid(g) * u).astype(BF16)
            y = jnp.dot(hm, wd_s[...], preferred_element_type=F32)
            for s in range(SLAB):
                ybuf[pl.ds(s, rows, stride=MOE_PITCH), :] = y[:, s * LANES:(s + 1) * LANES]

        @pl.when(nv > MOE_TB // 2)
        def _():
            expert_rows(MOE_TB)

        @pl.when(nv <= MOE_TB // 2)
        def _():
            expert_rows(MOE_TB // 2)

        def start_write(r):
            pltpu.make_async_copy(slab(ybuf, r * MOE_PITCH, PITCH_ALIGN), slab(y_hbm, a_ref[0, 0, r]),
                                  wsem.at[0]).start()

        def wgroup(g8, _):
            for q in range(SUBLANES):
                start_write(g8 * SUBLANES + q)
            return 0

        def wtail(r, _):
            start_write(r)
            return 0

        n_groups = lax.shift_right_logical(nv, 3)
        lax.fori_loop(0, n_groups, wgroup, 0)
        lax.fori_loop(n_groups * SUBLANES, nv, wtail, 0)

        @pl.when(j == n_used - 1)
        def _():
            wait_writes(nv)


def _moe(h2s, topi, counts, w_gate, w_up, w_down):
    n = h2s.shape[0] // SLAB
    d = D_MODEL
    nk = n * TOP_K
    tb = MOE_TB
    n_blocks = nk // tb + N_EXPERTS
    flat_e = topi[:TOP_K].reshape(nk)
    _, order = lax.sort_key_val(flat_e, jnp.arange(nk, dtype=jnp.int32))
    start = jnp.cumsum(counts) - counts
    padded = (counts + tb - 1) // tb * tb
    pad_end = jnp.cumsum(padded)
    pad_start = pad_end - padded
    n_used = (pad_end[-1] // tb).astype(jnp.int32).reshape(1)
    blk_p0 = jnp.arange(n_blocks, dtype=jnp.int32) * tb
    blk_e = jnp.minimum(jnp.sum((pad_end[None, :] <= blk_p0[:, None]).astype(jnp.int32), axis=1), N_EXPERTS - 1)
    blk_off = blk_p0 - pad_start[blk_e]
    nvalid = jnp.clip(counts[blk_e] - blk_off, 0, tb).astype(jnp.int32)
    rank = (start[blk_e] + blk_off)[:, None] + jnp.arange(tb, dtype=jnp.int32)[None, :]
    a3 = order[jnp.clip(rank, 0, nk - 1)].reshape(n_blocks, 1, tb)
    arow = a3 * SLAB
    trow = (a3 % n) * SLAB
    used = counts > 0
    e_ids = jnp.arange(N_EXPERTS, dtype=jnp.int32)
    later = jnp.where(used, e_ids, N_EXPERTS)
    nxt = jnp.concatenate([lax.cummin(later[::-1])[::-1][1:], jnp.full((1,), N_EXPERTS, jnp.int32)])
    nxt_e = jnp.where(nxt < N_EXPERTS, nxt, -1)[blk_e].astype(jnp.int32)
    wslot = ((jnp.cumsum(used.astype(jnp.int32)) - 1) % 2)[blk_e].astype(jnp.int32)

    smem_blk = lambda f: pl.BlockSpec((1, 1, tb), f, memory_space=pltpu.SMEM)
    any_spec = pl.BlockSpec(memory_space=pl.ANY)
    grid_spec = pltpu.PrefetchScalarGridSpec(
        num_scalar_prefetch=5,
        grid=(n_blocks,),
        in_specs=[smem_blk(lambda j, *_: (j, 0, 0)),
                  smem_blk(lambda j, *_: (jnp.minimum(j + 1, n_blocks - 1), 0, 0)),
                  smem_blk(lambda j, *_: (jnp.minimum(j + 2, n_blocks - 1), 0, 0)),
                  smem_blk(lambda j, *_: (j, 0, 0)),
                  any_spec, any_spec, any_spec, any_spec],
        out_specs=any_spec,
        scratch_shapes=[pltpu.VMEM((GATHER_SLOTS, tb * MOE_PITCH, LANES), F32),
                        pltpu.VMEM((tb * MOE_PITCH, LANES), F32),
                        pltpu.VMEM((2, d, D_EXPERT), F32),
                        pltpu.VMEM((2, d, D_EXPERT), F32),
                        pltpu.VMEM((2, D_EXPERT, d), F32),
                        pltpu.VMEM((d, D_EXPERT), BF16),
                        pltpu.VMEM((d, D_EXPERT), BF16),
                        pltpu.VMEM((D_EXPERT, d), BF16),
                        pltpu.SemaphoreType.DMA((GATHER_SLOTS,)),
                        pltpu.SemaphoreType.DMA((1,)),
                        pltpu.SemaphoreType.DMA((2,))],
    )
    return pl.pallas_call(
        _moe_kernel,
        grid_spec=grid_spec,
        out_shape=jax.ShapeDtypeStruct((TOP_K * n * SLAB, LANES), F32),
        compiler_params=_cparams(("arbitrary",)),
        name="moe",
    )(blk_e, n_used, nvalid, nxt_e, wslot, trow, trow, trow, arow, h2s, w_gate, w_up, w_down)


def _final_kernel(h_ref, y0, y1, y2, y3, y4, y5, tw_ref, x1_ref, mod_ref, g_ref, wg_ref, wu_ref, wd_ref, o_ref):
    h = h_ref[...]
    tm = h.shape[0]
    g = jnp.dot(h, wg_ref[...], preferred_element_type=F32)
    u = jnp.dot(h, wu_ref[...], preferred_element_type=F32)
    hm = (g * jax.nn.sigmoid(g) * u).astype(BF16)
    ffn = jnp.dot(hm, wd_ref[...], preferred_element_type=F32)
    tw = tw_ref[...]
    for k, y in enumerate((y0, y1, y2, y3, y4, y5)):
        yk = jnp.concatenate([y[pl.ds(s, tm, stride=SLAB), :] for s in range(SLAB)], axis=1)
        ffn = ffn + yk * tw[:, k:k + 1]
    o_ref[0] = x1_ref[0] + mod_ref[0, 5:6, :] * _rms(ffn, g_ref[...])


def _final(h2, y_flat, topw_t, x1, mod, g_post_ffn, w_sg, w_su, w_sd):
    bn, l, d = x1.shape
    n = bn * l
    tm = 256
    nl = l // tm
    const = lambda b, i: (0, 0)
    one = pl.Buffered(1)
    y_specs = [pl.BlockSpec((tm * SLAB, LANES), functools.partial(lambda b, i, k: (k * (n // tm) + b * nl + i, 0), k=k))
               for k in range(TOP_K)]
    return pl.pallas_call(
        _final_kernel,
        grid=(bn, nl),
        in_specs=[pl.BlockSpec((tm, d), lambda b, i: (b * nl + i, 0))] + y_specs + [
            pl.BlockSpec((tm, SUBLANES), lambda b, i: (b * nl + i, 0)),
            pl.BlockSpec((1, tm, d), lambda b, i: (b, i, 0)),
            pl.BlockSpec((1, 6, d), lambda b, i: (b, 0, 0)),
            pl.BlockSpec((1, d), const),
            pl.BlockSpec((d, D_EXPERT), const, pipeline_mode=one),
            pl.BlockSpec((d, D_EXPERT), const, pipeline_mode=one),
            pl.BlockSpec((D_EXPERT, d), const, pipeline_mode=one)],
        out_specs=pl.BlockSpec((1, tm, d), lambda b, i: (b, i, 0)),
        out_shape=jax.ShapeDtypeStruct((bn, l, d), F32),
        compiler_params=_cparams(("arbitrary", "arbitrary")),
        name="final",
    )(h2, *([y_flat] * TOP_K), topw_t, x1, mod, g_post_ffn.reshape(1, d), w_sg, w_su, w_sd)


def _layer(li, x, c, positions, w_ada, b_ada, g_pre_mix, g_post_mix, g_pre_ffn, g_post_ffn, w_in, g_q, g_kv,
           w_uq, w_uk, w_uv, a_re, a_im, log_dt, b_re, b_im, c_re, c_im, d_skip, w_glu, w_br_mla, w_br_s5,
           w_out, w_router, router_bias, w_exp_gate, w_exp_up, w_exp_down, w_sh_gate, w_sh_up, w_sh_down):
    bn, l, d = x.shape
    n = bn * l
    mod = _adaln(c, w_ada, b_ada).reshape(bn, 6, d)

    o_kpe = Q_LORA + KV_LORA
    o_u = o_kpe + QK_ROPE
    o_g = o_u + S5_CH
    w_in_t = jnp.swapaxes(w_in, 1, 2)
    w_q = jnp.pad(w_uq.reshape(Q_LORA, MLA_HEADS, QK_NOPE + QK_ROPE),
                  ((0, 0), (0, 0), (0, QK_PAD - QK_NOPE - QK_ROPE))).reshape(Q_LORA, MLA_HEADS * QK_PAD).astype(BF16)
    wr_t = w_router.T
    wr_hi = wr_t.astype(BF16)
    wr_lo = (wr_t - wr_hi.astype(F32)).astype(BF16)

    rope_c, rope_s1, rope_s2 = _rope_tables(positions)
    chunk_id = positions // CHUNK

    h, lat = _prenorm_lat(x, mod, g_pre_mix, 0, 1, w_in_t, li, o_kpe + LANES)
    h = h.reshape(n, d)
    gates = _mmt(h, w_in_t, li, o_g, 2 * d, BF16, 1024, 1024, act="sigmoid", name="mm_gates")
    u6 = _mm_u(h, w_in_t, li, o_u, bn, l)

    q, k, v = _qkvproj(lat, g_q, g_kv, w_q, w_uk.astype(BF16), w_uv.astype(BF16), rope_c, rope_s1, rope_s2)
    att = _attention(q, k, v, chunk_id, bn, l)

    wb, ab, wc, dd = _s5_params(a_re, a_im, log_dt, b_re, b_im, c_re, c_im, d_skip)
    y6 = _s5(u6, wb, ab, wc, dd, l)
    z = _glu(y6, w_glu.astype(BF16), bn, l).reshape(n, S5_CH)

    x1, h2b, h2s, logits_t = _merge(att, z, gates, x, mod, g_post_mix, g_pre_ffn, w_br_mla.astype(BF16),
                                    w_br_s5.astype(BF16), w_out.astype(BF16), wr_hi, wr_lo)

    topi, topw, cnt = _route(logits_t, router_bias)
    y_slabs = _moe(h2s, topi, cnt.reshape(N_EXPERTS).astype(jnp.int32), w_exp_gate, w_exp_up, w_exp_down)
    return _final(h2b.reshape(n, d), y_slabs, topw.T, x1, mod, g_post_ffn, w_sh_gate.astype(BF16),
                  w_sh_up.astype(BF16), w_sh_down.astype(BF16))


def kernel(x, c, positions, w_ada, b_ada, g_pre_mix, g_post_mix, g_pre_ffn, g_post_ffn, w_in, g_q, g_kv, w_uq, w_uk, w_uv, a_re, a_im, log_dt, b_re, b_im, c_re, c_im, d_skip, w_glu, w_br_mla, w_br_s5, w_out, w_router, router_bias, w_exp_gate, w_exp_up, w_exp_down, w_sh_gate, w_sh_up, w_sh_down):
    depth = w_ada.shape[0]
    for li in range(depth):
        x = _layer(li, x, c, positions, w_ada[li], b_ada[li], g_pre_mix[li], g_post_mix[li], g_pre_ffn[li],
                   g_post_ffn[li], w_in, g_q[li], g_kv[li], w_uq[li], w_uk[li], w_uv[li], a_re[li], a_im[li],
                   log_dt[li], b_re[li], b_im[li], c_re[li], c_im[li], d_skip[li], w_glu[li], w_br_mla[li],
                   w_br_s5[li], w_out[li], w_router[li], router_bias[li], w_exp_gate[li], w_exp_up[li],
                   w_exp_down[li], w_sh_gate[li], w_sh_up[li], w_sh_down[li])
    return x
```

```python
import functools

import jax
import jax.numpy as jnp
from jax import lax
from jax.experimental import pallas as pl
from jax.experimental.pallas import tpu as pltpu

F32 = jnp.float32
BF16 = jnp.bfloat16

D_MODEL = 2048
CHUNK = 64
EPS = 1e-6
MLA_HEADS = 8
QK_NOPE = 128
QK_ROPE = 64
V_HEAD = 128
Q_LORA = 512
KV_LORA = 512
ROPE_THETA = 10000.0
S5_CH = 1024
S5_GROUP = 16
S5_GROUPS = S5_CH // S5_GROUP
S5_STATE = 64
N_EXPERTS = 64
TOP_K = 6
N_EXPERT_GROUPS = 8
TOPK_GROUPS = 4
D_EXPERT = 512
ROUTED_SCALE = 2.5

LANES = 128
SUBLANES = 8
QK_PAD = 2 * LANES
VMEM_LIMIT = 56 * 1024 * 1024
NEG = -1e30
LOG2E = 1.4426950408889634

S5_GB = LANES // S5_GROUP
S5_NBLK = S5_CH // LANES
S5_PAIRS = S5_NBLK // 2
S5_LANES = S5_GB * S5_STATE
S5_PPS = 2
S5_TC = 256
S5_RB = 512

ATT_TQ = 256
ATT_TK = 256
MOE_TB = 256
SLAB = D_MODEL // LANES
MERGE_SUBTILES = 2
GATHER_SLOTS = 3
WEIGHT_DMA_PRIORITY = 1
MOE_PITCH = 20
PITCH_ALIGN = 4


def _cparams(sem):
    return pltpu.CompilerParams(dimension_semantics=sem, vmem_limit_bytes=VMEM_LIMIT)


def _rms(x, g):
    return x * lax.rsqrt(jnp.mean(x * x, axis=-1, keepdims=True) + EPS) * g


def _adaln_kernel(c_ref, w_ref, b_ref, o_ref):
    c = c_ref[...]
    a = (c * jax.nn.sigmoid(c)).astype(BF16)
    o_ref[...] = jnp.dot(a, w_ref[...].astype(BF16), preferred_element_type=F32) + b_ref[...]


def _adaln(c, w, b):
    bn, d = c.shape
    n = w.shape[1]
    tn = 1024
    return pl.pallas_call(
        _adaln_kernel,
        grid=(n // tn,),
        in_specs=[pl.BlockSpec((bn, d), lambda j: (0, 0)),
                  pl.BlockSpec((d, tn), lambda j: (0, j)),
                  pl.BlockSpec((1, tn), lambda j: (0, j))],
        out_specs=pl.BlockSpec((bn, tn), lambda j: (0, j)),
        out_shape=jax.ShapeDtypeStruct((bn, n), F32),
        compiler_params=_cparams(("arbitrary",)),
        name="adaln",
    )(c, w, b.reshape(1, n))


NT_DIMS = (((1,), (1,)), ((), ()))


def _prenorm_lat_kernel(x_ref, mod_ref, g_ref, w_ref, h_ref, lat_ref, w_scr, *, sh_row, sc_row):
    @pl.when(jnp.logical_and(pl.program_id(0) == 0, pl.program_id(1) == 0))
    def _():
        w_scr[...] = w_ref[...].astype(BF16)

    y = _rms(x_ref[0], g_ref[...])
    h = (y * (1.0 + mod_ref[0, sc_row:sc_row + 1, :]) + mod_ref[0, sh_row:sh_row + 1, :]).astype(BF16)
    h_ref[0] = h
    lat_ref[...] = lax.dot_general(h, w_scr[...], NT_DIMS, preferred_element_type=F32)


def _prenorm_lat(x, mod, g, sh_row, sc_row, w_t, layer, n_lat):
    bn, l, d = x.shape
    tl = 512
    nl = l // tl
    return pl.pallas_call(
        functools.partial(_prenorm_lat_kernel, sh_row=sh_row, sc_row=sc_row),
        grid=(bn, nl),
        in_specs=[pl.BlockSpec((1, tl, d), lambda b, i: (b, i, 0)),
                  pl.BlockSpec((1, 6, d), lambda b, i: (b, 0, 0)),
                  pl.BlockSpec((1, d), lambda b, i: (0, 0)),
                  pl.BlockSpec((None, n_lat, d), lambda b, i: (layer, 0, 0), pipeline_mode=pl.Buffered(1))],
        out_specs=[pl.BlockSpec((1, tl, d), lambda b, i: (b, i, 0)),
                   pl.BlockSpec((tl, n_lat), lambda b, i: (b * nl + i, 0))],
        out_shape=[jax.ShapeDtypeStruct((bn, l, d), BF16),
                   jax.ShapeDtypeStruct((bn * l, n_lat), F32)],
        scratch_shapes=[pltpu.VMEM((n_lat, d), BF16)],
        compiler_params=_cparams(("arbitrary", "arbitrary")),
        name="prenorm_lat",
    )(x, mod, g.reshape(1, d), w_t)


def _wt_tile(wa_ref, wb_ref, shift, tn):
    if shift == 0:
        return wa_ref[...].astype(BF16)
    return jnp.concatenate([wa_ref[...], wb_ref[...]], axis=0)[shift:shift + tn].astype(BF16)


def _wt_specs(layer, row0, tn, k, jmap):
    shift = row0 % LANES
    c0 = row0 - shift
    assert c0 % tn == 0 and shift % SUBLANES == 0
    specs = [pl.BlockSpec((None, tn, k), lambda *g: (layer, jmap(*g) + c0 // tn, 0))]
    if shift:
        specs.append(pl.BlockSpec((None, LANES, k), lambda *g: (layer, (c0 + (jmap(*g) + 1) * tn) // LANES, 0)))
    return specs, shift


def _mmt_kernel(a_ref, wa_ref, *rest, act, shift):
    wb_ref = rest[0] if shift else None
    o_ref, w_scr = rest[-2:]

    @pl.when(pl.program_id(1) == 0)
    def _():
        w_scr[...] = _wt_tile(wa_ref, wb_ref, shift, w_scr.shape[0])

    acc = lax.dot_general(a_ref[...], w_scr[...], NT_DIMS, preferred_element_type=F32)
    if act == "sigmoid":
        acc = jax.nn.sigmoid(acc)
    o_ref[...] = acc.astype(o_ref.dtype)


def _mmt(a, w_t, layer, row0, n, out_dtype, tm, tn, act=None, name="mmt"):
    m, k = a.shape
    w_specs, shift = _wt_specs(layer, row0, tn, k, lambda j, i: j)
    return pl.pallas_call(
        functools.partial(_mmt_kernel, act=act, shift=shift),
        grid=(n // tn, m // tm),
        in_specs=[pl.BlockSpec((tm, k), lambda j, i: (i, 0))] + w_specs,
        out_specs=pl.BlockSpec((tm, tn), lambda j, i: (i, j)),
        out_shape=jax.ShapeDtypeStruct((m, n), out_dtype),
        scratch_shapes=[pltpu.VMEM((tn, k), BF16)],
        compiler_params=_cparams(("arbitrary", "arbitrary")),
        name=name,
    )(a, *([w_t] * len(w_specs)))


def _mm_u_kernel(a_ref, wa_ref, *rest, shift):
    wb_ref = rest[0] if shift else None
    o_ref, w_scr = rest[-2:]
    b = pl.program_id(1)

    @pl.when(jnp.logical_and(pl.program_id(0) == 0, b == 0))
    def _():
        w_scr[...] = _wt_tile(wa_ref, wb_ref, shift, w_scr.shape[0])

    res = lax.dot_general(a_ref[...], w_scr[...], NT_DIMS, preferred_element_type=F32)
    tm = res.shape[0]
    for c in range(S5_NBLK):
        o_ref.at[c // 2][pl.ds(b * 2 + c % 2, tm, stride=SUBLANES), :] = res[:, c * LANES:(c + 1) * LANES]


def _mm_u(h, w_t, layer, row0, bn, l):
    m, k = h.shape
    tm = 512
    nl = l // tm
    assert bn * 2 == SUBLANES
    w_specs, shift = _wt_specs(layer, row0, S5_CH, k, lambda i, b: 0)
    return pl.pallas_call(
        functools.partial(_mm_u_kernel, shift=shift),
        grid=(nl, bn),
        in_specs=[pl.BlockSpec((tm, k), lambda i, b: (b * nl + i, 0))] + w_specs,
        out_specs=pl.BlockSpec((S5_PAIRS, tm * SUBLANES, LANES), lambda i, b: (0, i, 0)),
        out_shape=jax.ShapeDtypeStruct((S5_PAIRS, l * SUBLANES, LANES), F32),
        scratch_shapes=[pltpu.VMEM((S5_CH, k), BF16)],
        compiler_params=_cparams(("arbitrary", "arbitrary")),
        name="mm_u",
    )(h, *([w_t] * len(w_specs)))


def _rope_tab_kernel(pos_ref, k_ref, c_ref, s1_ref, s2_ref):
    ang = pos_ref[...].astype(F32) * k_ref[0:1, :]
    s = jnp.sin(ang)
    c_ref[...] = jnp.cos(ang) * k_ref[1:2, :]
    s1_ref[...] = s * k_ref[2:3, :]
    s2_ref[...] = s * k_ref[3:4, :]


def _rope_tables(positions):
    n = positions.size
    half = QK_ROPE // 2
    inv_freq = ROPE_THETA ** (-jnp.arange(half, dtype=F32) / half)
    zh, oh = jnp.zeros((half,), F32), jnp.ones((half,), F32)
    z2 = jnp.zeros((LANES - QK_ROPE,), F32)
    rows = [jnp.concatenate([inv_freq, inv_freq, z2]), jnp.concatenate([oh, oh, z2]),
            jnp.concatenate([-oh, zh, z2]), jnp.concatenate([zh, oh, z2])]
    consts = jnp.stack(rows + [jnp.zeros((LANES,), F32)] * (SUBLANES - len(rows)))
    tm = 1024
    tab = jax.ShapeDtypeStruct((n, LANES), F32)
    return pl.pallas_call(
        _rope_tab_kernel,
        grid=(n // tm,),
        in_specs=[pl.BlockSpec((tm, 1), lambda i: (i, 0)),
                  pl.BlockSpec((SUBLANES, LANES), lambda i: (0, 0))],
        out_specs=[pl.BlockSpec((tm, LANES), lambda i: (i, 0))] * 3,
        out_shape=[tab, tab, tab],
        compiler_params=_cparams(("arbitrary",)),
        name="rope_tables",
    )(positions.reshape(n, 1), consts)


def _rope_tile(t, c_ref, s1_ref, s2_ref):
    return (t * c_ref[...] + pltpu.roll(t, LANES - QK_ROPE // 2, 1) * s1_ref[...]
            + pltpu.roll(t, QK_ROPE // 2, 1) * s2_ref[...])


def _qkvproj_kernel(lat_ref, gq_ref, gkv_ref, wq_ref, wk_ref, wv_ref, kpe_ref, c_ref, s1_ref, s2_ref,
                    q_ref, k_ref, v_ref, *, scale):
    lat = lat_ref[...]
    qn = _rms(lat[:, :Q_LORA], gq_ref[...]).astype(BF16)
    cn = _rms(lat[:, Q_LORA:], gkv_ref[...]).astype(BF16)
    q = jnp.dot(qn, wq_ref[...], preferred_element_type=F32)
    kn = jnp.dot(cn, wk_ref[...], preferred_element_type=F32)
    v_ref[...] = jnp.dot(cn, wv_ref[...], preferred_element_type=F32).astype(v_ref.dtype)
    kt = _rope_tile(kpe_ref[...], c_ref, s1_ref, s2_ref).astype(k_ref.dtype)
    for h in range(MLA_HEADS):
        o = h * QK_PAD
        q_ref[:, o:o + LANES] = (q[:, o:o + LANES] * scale).astype(q_ref.dtype)
        qt = _rope_tile(q[:, o + LANES:o + QK_PAD], c_ref, s1_ref, s2_ref)
        q_ref[:, o + LANES:o + QK_PAD] = (qt * scale).astype(q_ref.dtype)
        k_ref[:, o:o + LANES] = kn[:, h * QK_NOPE:(h + 1) * QK_NOPE].astype(k_ref.dtype)
        k_ref[:, o + LANES:o + QK_PAD] = kt


def _qkvproj(lat, g_q, g_kv, w_q, w_k, w_v, rope_c, rope_s1, rope_s2):
    n = lat.shape[0]
    tm = 512
    row = lambda i: (i, 0)
    const = lambda i: (0, 0)
    tab = pl.BlockSpec((tm, LANES), row)
    kpe_spec = pl.BlockSpec((tm, LANES), lambda i: (i, (Q_LORA + KV_LORA) // LANES))
    return pl.pallas_call(
        functools.partial(_qkvproj_kernel, scale=(QK_NOPE + QK_ROPE) ** -0.5 * LOG2E),
        grid=(n // tm,),
        in_specs=[pl.BlockSpec((tm, Q_LORA + KV_LORA), row),
                  pl.BlockSpec((1, Q_LORA), const),
                  pl.BlockSpec((1, KV_LORA), const),
                  pl.BlockSpec((Q_LORA, MLA_HEADS * QK_PAD), const),
                  pl.BlockSpec((KV_LORA, MLA_HEADS * QK_NOPE), const),
                  pl.BlockSpec((KV_LORA, MLA_HEADS * V_HEAD), const),
                  kpe_spec, tab, tab, tab],
        out_specs=[pl.BlockSpec((tm, MLA_HEADS * QK_PAD), row),
                   pl.BlockSpec((tm, MLA_HEADS * QK_PAD), row),
                   pl.BlockSpec((tm, MLA_HEADS * V_HEAD), row)],
        out_shape=[jax.ShapeDtypeStruct((n, MLA_HEADS * QK_PAD), BF16),
                   jax.ShapeDtypeStruct((n, MLA_HEADS * QK_PAD), BF16),
                   jax.ShapeDtypeStruct((n, MLA_HEADS * V_HEAD), BF16)],
        compiler_params=_cparams(("arbitrary",)),
        name="qkvproj",
    )(lat, g_q.reshape(1, Q_LORA), g_kv.reshape(1, KV_LORA), w_q, w_k, w_v, lat, rope_c, rope_s1, rope_s2)


def _attn_kernel(lo_ref, hi_ref, q_ref, k_ref, v_ref, qc_ref, kc_ref, o_ref, m_scr, l_scr, acc_scr, *, nq):
    b = pl.program_id(0)
    i = pl.program_id(1)
    qc = qc_ref[...]
    m_scr[...] = jnp.full(m_scr.shape, NEG, F32)
    l_scr[...] = jnp.zeros(l_scr.shape, F32)
    acc_scr[...] = jnp.zeros(acc_scr.shape, F32)

    def make_body(masked):
        def body(j, _):
            off = pl.multiple_of(j * ATT_TK, ATT_TK)
            if masked:
                mask = kc_ref[j] <= qc
            for h in range(MLA_HEADS):
                q = q_ref[:, h * QK_PAD:(h + 1) * QK_PAD]
                k = k_ref[pl.ds(off, ATT_TK), h * QK_PAD:(h + 1) * QK_PAD]
                s = lax.dot_general(q, k, (((1,), (1,)), ((), ())), preferred_element_type=F32)
                if masked:
                    s = jnp.where(mask, s, NEG)
                m_old = m_scr[h]
                m_new = jnp.maximum(m_old, jnp.max(s, axis=-1, keepdims=True))
                p = jnp.exp2(s - jnp.concatenate([m_new] * (ATT_TK // LANES), axis=1))
                alpha = jnp.exp2(m_old - m_new)
                l_scr[h] = alpha * l_scr[h] + jnp.sum(p, axis=-1, keepdims=True)
                v = v_ref[pl.ds(off, ATT_TK), h * V_HEAD:(h + 1) * V_HEAD]
                acc_scr[h] = alpha * acc_scr[h] + jnp.dot(p.astype(BF16), v, preferred_element_type=F32)
                m_scr[h] = m_new
            return 0
        return body

    lo = lo_ref[b * nq + i]
    lax.fori_loop(0, lo, make_body(False), 0)
    lax.fori_loop(lo, hi_ref[b * nq + i], make_body(True), 0)
    for h in range(MLA_HEADS):
        o_ref[:, h * V_HEAD:(h + 1) * V_HEAD] = (acc_scr[h] / l_scr[h]).astype(o_ref.dtype)


def _attention(q, k, v, chunk_id, bn, l):
    nq = l // ATT_TQ
    nk = l // ATT_TK
    q_max = jnp.max(chunk_id.reshape(bn, nq, ATT_TQ), axis=-1)
    k_min = jnp.min(chunk_id.reshape(bn, nk, ATT_TK), axis=-1)
    needed = k_min[:, None, :] <= q_max[:, :, None]
    hi = jnp.max(jnp.where(needed, jnp.arange(1, nk + 1, dtype=jnp.int32), 0), axis=-1).reshape(bn * nq)
    q_min = jnp.min(chunk_id.reshape(bn, nq, ATT_TQ), axis=-1)
    k_max = jnp.max(chunk_id.reshape(bn, nk, ATT_TK), axis=-1)
    full = k_max[:, None, :] <= q_min[:, :, None]
    lo = jnp.min(jnp.where(full, nk, jnp.arange(nk, dtype=jnp.int32)), axis=-1).astype(jnp.int32).reshape(bn * nq)
    hi = jnp.maximum(hi, lo)
    qc = chunk_id.reshape(bn * l, 1)
    kc = chunk_id.reshape(bn * nk, 1, ATT_TK)
    grid_spec = pltpu.PrefetchScalarGridSpec(
        num_scalar_prefetch=2,
        grid=(bn, nq),
        in_specs=[pl.BlockSpec((ATT_TQ, MLA_HEADS * QK_PAD), lambda b, i, *_: (b * nq + i, 0)),
                  pl.BlockSpec((l, MLA_HEADS * QK_PAD), lambda b, i, *_: (b, 0)),
                  pl.BlockSpec((l, MLA_HEADS * V_HEAD), lambda b, i, *_: (b, 0)),
                  pl.BlockSpec((ATT_TQ, 1), lambda b, i, *_: (b * nq + i, 0)),
                  pl.BlockSpec((nk, 1, ATT_TK), lambda b, i, *_: (b, 0, 0))],
        out_specs=pl.BlockSpec((ATT_TQ, MLA_HEADS * V_HEAD), lambda b, i, *_: (b * nq + i, 0)),
        scratch_shapes=[pltpu.VMEM((MLA_HEADS, ATT_TQ, LANES), F32),
                        pltpu.VMEM((MLA_HEADS, ATT_TQ, LANES), F32),
                        pltpu.VMEM((MLA_HEADS, ATT_TQ, V_HEAD), F32)],
    )
    return pl.pallas_call(
        functools.partial(_attn_kernel, nq=nq),
        grid_spec=grid_spec,
        out_shape=jax.ShapeDtypeStruct((bn * l, MLA_HEADS * V_HEAD), BF16),
        compiler_params=_cparams(("arbitrary", "arbitrary")),
        name="attention",
    )(lo, hi, q, k, v, qc, kc)


def _s5_kernel(u_ref, wb_ref, a_ref, wc_ref, d_ref, o_ref, x_scr, st_scr):
    rows = S5_TC * SUBLANES
    nsub = rows // S5_RB

    @pl.when(pl.program_id(1) == 0)
    def _():
        st_scr[...] = jnp.zeros_like(st_scr)

    even = (lax.broadcasted_iota(jnp.int32, (S5_RB, 1), 0) & 1) == 0
    half = S5_RB // 2
    nre = S5_LANES // LANES
    nsl = 2 * nre

    def mm_in(r, _):
        off = pl.multiple_of(r * S5_RB, S5_RB)
        for p in range(S5_PPS):
            for s in range(2):
                us = u_ref.at[p][pl.ds(off + s, half, stride=2), :].astype(BF16)
                out = jnp.dot(us, wb_ref[p, :, s * 2 * S5_LANES:(s + 1) * 2 * S5_LANES],
                              preferred_element_type=F32)
                for c in range(nsl):
                    x_scr.at[p * nsl + c][pl.ds(off + s, half, stride=2), :] = out[:, c * LANES:(c + 1) * LANES]
        return 0

    lax.fori_loop(0, nsub, mm_in, 0)

    a_c = [a_ref[p, :, c * LANES:(c + 1) * LANES] for p in range(S5_PPS) for c in range(nsl)]

    def step(t, carry):
        off = pl.multiple_of(t * SUBLANES, SUBLANES)
        new = [None] * (S5_PPS * nsl)
        for p in range(S5_PPS):
            for c in range(nre):
                ir, ii = p * nsl + c, p * nsl + nre + c
                xr, xi = carry[ir], carry[ii]
                ar, ai = a_c[ir], a_c[ii]
                nr = ar * xr - ai * xi + x_scr[ir, pl.ds(off, SUBLANES), :]
                ni = ar * xi + ai * xr + x_scr[ii, pl.ds(off, SUBLANES), :]
                x_scr[ir, pl.ds(off, SUBLANES), :] = nr
                x_scr[ii, pl.ds(off, SUBLANES), :] = ni
                new[ir], new[ii] = nr, ni
        return tuple(new)

    init = tuple(st_scr[p, :, c * LANES:(c + 1) * LANES] for p in range(S5_PPS) for c in range(nsl))
    fin = lax.fori_loop(0, S5_TC, step, init, unroll=8)
    for p in range(S5_PPS):
        for c in range(nsl):
            st_scr[p, :, c * LANES:(c + 1) * LANES] = fin[p * nsl + c]

    def mm_out(r, _):
        off = pl.multiple_of(r * S5_RB, S5_RB)
        for p in range(S5_PPS):
            d = jnp.concatenate([d_ref[p]] * (S5_RB // SUBLANES), axis=0)
            x = jnp.concatenate([x_scr[p * nsl + c, pl.ds(off, S5_RB), :] for c in range(nsl)],
                                axis=1).astype(BF16)
            out = jnp.dot(x, wc_ref[p], preferred_element_type=F32)
            y = jnp.where(even, out[:, :LANES], out[:, LANES:]) + d * u_ref[p, pl.ds(off, S5_RB), :]
            o_ref[p, pl.ds(off, S5_RB), :] = jax.nn.gelu(y).astype(o_ref.dtype)
        return 0

    lax.fori_loop(0, nsub, mm_out, 0)


def _s5_params(a_re, a_im, log_dt, b_re, b_im, c_re, c_im, d_skip):
    step = jnp.exp(log_dt)[:, None]
    mag = jnp.exp(a_re * step)
    abar_re, abar_im = mag * jnp.cos(a_im * step), mag * jnp.sin(a_im * step)
    den = a_re * a_re + a_im * a_im
    nr, ni = abar_re - 1.0, abar_im
    f_re, f_im = (nr * a_re + ni * a_im) / den, (ni * a_re - nr * a_im) / den
    bbar_re = f_re[..., None] * b_re - f_im[..., None] * b_im
    bbar_im = f_re[..., None] * b_im + f_im[..., None] * b_re
    eye = jnp.eye(S5_GB, dtype=F32)
    bb = jnp.stack([bbar_re, bbar_im]).reshape(2, S5_NBLK, S5_GB, S5_STATE, S5_GROUP)
    wb = jnp.einsum('ab,rjapc->jacrbp', eye, bb).reshape(S5_NBLK, LANES, 2 * S5_LANES)
    wb = wb.reshape(S5_PAIRS, 2, LANES, 2 * S5_LANES).transpose(0, 2, 1, 3).reshape(S5_PAIRS, LANES, 4 * S5_LANES)
    cc = jnp.stack([c_re, -c_im]).reshape(2, S5_NBLK, S5_GB, S5_GROUP, S5_STATE)
    wc = jnp.einsum('ab,rjacp->jrapbc', eye, cc).reshape(S5_NBLK, 2 * S5_LANES, LANES)
    wc = wc.reshape(S5_PAIRS, 2, 2 * S5_LANES, LANES).transpose(0, 2, 1, 3).reshape(S5_PAIRS, 2 * S5_LANES, 2 * LANES)
    ab = jnp.concatenate([abar_re.reshape(S5_NBLK, S5_LANES), abar_im.reshape(S5_NBLK, S5_LANES)], axis=1)
    ab = jnp.tile(ab.reshape(S5_PAIRS, 2, 2 * S5_LANES), (1, SUBLANES // 2, 1))
    dd = jnp.tile(d_skip.reshape(S5_PAIRS, 2, LANES), (1, SUBLANES // 2, 1))
    return wb.astype(BF16), ab, wc.astype(BF16), dd


def _s5(u6, wb, ab, wc, dd, l):
    rows = S5_TC * SUBLANES
    return pl.pallas_call(
        _s5_kernel,
        grid=(S5_PAIRS // S5_PPS, l // S5_TC),
        in_specs=[pl.BlockSpec((S5_PPS, rows, LANES), lambda k, c: (k, c, 0)),
                  pl.BlockSpec((S5_PPS, LANES, 4 * S5_LANES), lambda k, c: (k, 0, 0)),
                  pl.BlockSpec((S5_PPS, SUBLANES, 2 * S5_LANES), lambda k, c: (k, 0, 0)),
                  pl.BlockSpec((S5_PPS, 2 * S5_LANES, 2 * LANES), lambda k, c: (k, 0, 0)),
                  pl.BlockSpec((S5_PPS, SUBLANES, LANES), lambda k, c: (k, 0, 0))],
        out_specs=pl.BlockSpec((S5_PPS, rows, LANES), lambda k, c: (k, c, 0)),
        out_shape=jax.ShapeDtypeStruct(u6.shape, F32),
        scratch_shapes=[pltpu.VMEM((S5_PPS * 2 * S5_LANES // LANES, rows, LANES), F32),
                        pltpu.VMEM((S5_PPS, SUBLANES, 2 * S5_LANES), F32)],
        compiler_params=_cparams(("arbitrary", "arbitrary")),
        name="s5",
    )(u6, wb, ab, wc, dd)


def _glu_kernel(y_ref, w_ref, o_ref):
    b = pl.program_id(1)
    tt = o_ref.shape[1]
    y = jnp.concatenate([y_ref.at[k][pl.ds(b * 2 + s, tt, stride=SUBLANES), :]
                         for k in range(S5_PAIRS) for s in range(2)], axis=1)
    g = jnp.dot(y.astype(BF16), w_ref[...], preferred_element_type=F32)
    o_ref[0] = (y * jax.nn.sigmoid(g)).astype(o_ref.dtype)


def _glu(y6, w_glu, bn, l):
    tt = 512
    assert bn * 2 == SUBLANES
    return pl.pallas_call(
        _glu_kernel,
        grid=(l // tt, bn),
        in_specs=[pl.BlockSpec((S5_PAIRS, tt * SUBLANES, LANES), lambda i, b: (0, i, 0)),
                  pl.BlockSpec((S5_CH, S5_CH), lambda i, b: (0, 0))],
        out_specs=pl.BlockSpec((1, tt, S5_CH), lambda i, b: (b, i, 0)),
        out_shape=jax.ShapeDtypeStruct((bn, l, S5_CH), BF16),
        compiler_params=_cparams(("arbitrary", "arbitrary")),
        name="glu",
    )(y6, w_glu)


def _merge_kernel(att_ref, z_ref, gm_ref, gs_ref, x_ref, mod_ref, gpost_ref, gpre_ref,
                  wbm_ref, wbs_ref, wo_ref, wrh_ref, wrl_ref, x1_ref, h2b_ref, h2s_ref, lg_ref):
    tm = x_ref.shape[1]
    th = tm // MERGE_SUBTILES
    for r0 in range(0, tm, th):
        rows = slice(r0, r0 + th)
        ym = jnp.dot(att_ref[rows, :], wbm_ref[...], preferred_element_type=F32)
        ys = jnp.dot(z_ref[rows, :], wbs_ref[...], preferred_element_type=F32)
        mixed_in = (gm_ref[rows, :].astype(F32) * ym + gs_ref[rows, :].astype(F32) * ys).astype(BF16)
        mixed = jnp.dot(mixed_in, wo_ref[...], preferred_element_type=F32)
        x1 = x_ref[0, rows, :] + mod_ref[0, 2:3, :] * _rms(mixed, gpost_ref[...])
        x1_ref[0, rows, :] = x1
        h2 = _rms(x1, gpre_ref[...]) * (1.0 + mod_ref[0, 4:5, :]) + mod_ref[0, 3:4, :]
        h2_hi = h2.astype(BF16)
        h2b_ref[0, rows, :] = h2_hi
        for s in range(SLAB):
            h2s_ref[pl.ds(r0 * SLAB + s, th, stride=SLAB), :] = h2[:, s * LANES:(s + 1) * LANES]
        h2_lo = (h2 - h2_hi.astype(F32)).astype(BF16)
        lg_ref[:, rows] = (lax.dot_general(wrh_ref[...], h2_hi, NT_DIMS, preferred_element_type=F32)
                           + lax.dot_general(wrl_ref[...], h2_hi, NT_DIMS, preferred_element_type=F32)
                           + lax.dot_general(wrh_ref[...], h2_lo, NT_DIMS, preferred_element_type=F32))


def _merge(att, z_tm, gates, x, mod, g_post, g_pre_ffn, w_br_mla, w_br_s5, w_out, wr_hi, wr_lo):
    bn, l, d = x.shape
    tm = 256
    nl = l // tm
    row = lambda b, i: (b * nl + i, 0)
    const = lambda b, i: (0, 0)
    one = pl.Buffered(1)
    return pl.pallas_call(
        _merge_kernel,
        grid=(bn, nl),
        in_specs=[pl.BlockSpec((tm, MLA_HEADS * V_HEAD), row),
                  pl.BlockSpec((tm, S5_CH), row),
                  pl.BlockSpec((tm, d), lambda b, i: (b * nl + i, 0)),
                  pl.BlockSpec((tm, d), lambda b, i: (b * nl + i, 1)),
                  pl.BlockSpec((1, tm, d), lambda b, i: (b, i, 0)),
                  pl.BlockSpec((1, 6, d), lambda b, i: (b, 0, 0)),
                  pl.BlockSpec((1, d), const),
                  pl.BlockSpec((1, d), const),
                  pl.BlockSpec((MLA_HEADS * V_HEAD, d), const, pipeline_mode=one),
                  pl.BlockSpec((S5_CH, d), const, pipeline_mode=one),
                  pl.BlockSpec((d, d), const, pipeline_mode=one),
                  pl.BlockSpec((N_EXPERTS, d), const, pipeline_mode=one),
                  pl.BlockSpec((N_EXPERTS, d), const, pipeline_mode=one)],
        out_specs=[pl.BlockSpec((1, tm, d), lambda b, i: (b, i, 0)),
                   pl.BlockSpec((1, tm, d), lambda b, i: (b, i, 0)),
                   pl.BlockSpec((tm * SLAB, LANES), row),
                   pl.BlockSpec((N_EXPERTS, tm), lambda b, i: (0, b * nl + i))],
        out_shape=[jax.ShapeDtypeStruct((bn, l, d), F32),
                   jax.ShapeDtypeStruct((bn, l, d), BF16),
                   jax.ShapeDtypeStruct((bn * l * SLAB, LANES), F32),
                   jax.ShapeDtypeStruct((N_EXPERTS, bn * l), F32)],
        compiler_params=_cparams(("arbitrary", "arbitrary")),
        name="merge",
    )(att, z_tm, gates, gates, x, mod, g_post.reshape(1, d), g_pre_ffn.reshape(1, d),
      w_br_mla, w_br_s5, w_out, wr_hi, wr_lo)


def _route_kernel(lg_ref, b_ref, ti_ref, tw_ref, cnt_ref):
    ng = N_EXPERT_GROUPS
    gsz = N_EXPERTS // ng
    lg = lg_ref[...]
    t = lg.shape[-1]
    sc = jax.nn.sigmoid(lg)
    sel = sc + b_ref[...]
    ninf = -jnp.inf
    i_in = lax.broadcasted_iota(jnp.int32, (ng, gsz, t), 1).astype(F32)
    m1 = jnp.max(sel, axis=1, keepdims=True)
    idx1 = jnp.min(jnp.where(sel == m1, i_in, float(gsz)), axis=1, keepdims=True)
    m2 = jnp.max(jnp.where(i_in == idx1, ninf, sel), axis=1, keepdims=True)
    gs = m1 + m2
    g_i = lax.broadcasted_iota(jnp.int32, (ng, 1, t), 0).astype(F32)
    picked = jnp.zeros((ng, 1, t), F32)
    cur = gs
    for _ in range(TOPK_GROUPS):
        m = jnp.max(cur, axis=0, keepdims=True)
        idx = jnp.min(jnp.where(cur == m, g_i, float(ng)), axis=0, keepdims=True)
        hit = g_i == idx
        picked = jnp.where(hit, 1.0, picked)
        cur = jnp.where(hit, ninf, cur)
    cand = jnp.where(picked > 0.5, sel, ninf)
    e_i = lax.broadcasted_iota(jnp.int32, (ng, gsz, t), 0).astype(F32) * float(gsz) + i_in
    ws = []
    hits = jnp.zeros((ng, gsz, t), F32)
    for r in range(TOP_K):
        m = jnp.max(jnp.max(cand, axis=1, keepdims=True), axis=0, keepdims=True)
        idx = jnp.min(jnp.min(jnp.where(cand == m, e_i, float(N_EXPERTS)), axis=1, keepdims=True),
                      axis=0, keepdims=True)
        hit = e_i == idx
        w = jnp.sum(jnp.sum(jnp.where(hit, sc, 0.0), axis=1, keepdims=True), axis=0, keepdims=True)
        ti_ref[r:r + 1, :] = idx[0].astype(jnp.int32)
        ws.append(w[0])
        hits = jnp.where(hit, 1.0, hits)
        cand = jnp.where(hit, ninf, cand)

    @pl.when(pl.program_id(0) == 0)
    def _():
        cnt_ref[...] = jnp.zeros(cnt_ref.shape, F32)

    cnt_ref[...] += jnp.sum(hits, axis=2, keepdims=True)
    tot = ws[0]
    for w in ws[1:]:
        tot = tot + w
    for r in range(TOP_K):
        tw_ref[r:r + 1, :] = ws[r] / tot * ROUTED_SCALE
    for r in range(TOP_K, SUBLANES):
        ti_ref[r:r + 1, :] = jnp.zeros((1, t), jnp.int32)
        tw_ref[r:r + 1, :] = jnp.zeros((1, t), F32)


def _route(logits_t, router_bias):
    n = logits_t.shape[1]
    ng = N_EXPERT_GROUPS
    gsz = N_EXPERTS // ng
    tn = 1024
    return pl.pallas_call(
        _route_kernel,
        grid=(n // tn,),
        in_specs=[pl.BlockSpec((ng, gsz, tn), lambda i: (0, 0, i)),
                  pl.BlockSpec((ng, gsz, 1), lambda i: (0, 0, 0))],
        out_specs=[pl.BlockSpec((SUBLANES, tn), lambda i: (0, i)),
                   pl.BlockSpec((SUBLANES, tn), lambda i: (0, i)),
                   pl.BlockSpec((ng, gsz, 1), lambda i: (0, 0, 0))],
        out_shape=[jax.ShapeDtypeStruct((SUBLANES, n), jnp.int32),
                   jax.ShapeDtypeStruct((SUBLANES, n), F32),
                   jax.ShapeDtypeStruct((ng, gsz, 1), F32)],
        compiler_params=_cparams(("arbitrary",)),
        name="route",
    )(logits_t.reshape(ng, gsz, n), router_bias.reshape(ng, gsz, 1))


def _moe_kernel(blk_e_ref, nused_ref, nvalid_ref, nxt_e_ref, wslot_ref, t_ref, t1_ref, t2_ref, a_ref, h_hbm,
                wg_hbm, wu_hbm, wd_hbm, y_hbm, xbuf, ybuf, wg_buf, wu_buf, wd_buf, wg_s, wu_s, wd_s,
                gsem, wsem, wtsem):
    j = pl.program_id(0)
    n_used = nused_ref[0]
    slot = lax.rem(j, GATHER_SLOTS)
    nv = nvalid_ref[j]
    nv_prev = nvalid_ref[jnp.maximum(j - 1, 0)]

    def slab(ref, row0, align=SUBLANES):
        return ref.at[pl.ds(pl.multiple_of(row0, align), SLAB), :]

    def start_gather(idx_ref, s):
        def body(r, _):
            pltpu.make_async_copy(slab(h_hbm, idx_ref[0, 0, r]), slab(xbuf.at[s], r * MOE_PITCH, PITCH_ALIGN),
                                  gsem.at[s]).start()
            return 0
        lax.fori_loop(0, MOE_TB, body, 0, unroll=16)

    def wait_gather(s):
        rows = MOE_TB * SLAB
        pltpu.make_async_copy(h_hbm.at[pl.ds(0, rows), :], xbuf.at[s, pl.ds(0, rows), :], gsem.at[s]).wait()

    def weight_copies(expert, s):
        return (pltpu.make_async_copy(wg_hbm.at[expert], wg_buf.at[s], wtsem.at[s]),
                pltpu.make_async_copy(wu_hbm.at[expert], wu_buf.at[s], wtsem.at[s]),
                pltpu.make_async_copy(wd_hbm.at[expert], wd_buf.at[s], wtsem.at[s]))

    def wait_writes(count):
        p = MOE_TB
        while p >= 1:
            @pl.when((count & p) != 0)
            def _(p=p):
                rows = p * SLAB
                pltpu.make_async_copy(ybuf.at[pl.ds(0, rows), :], y_hbm.at[pl.ds(0, rows), :], wsem.at[0]).wait()
            p //= 2

    @pl.when(j < n_used)
    def _():
        @pl.when(j == 0)
        def _():
            start_gather(t_ref, 0)

            @pl.when(n_used > 1)
            def _():
                start_gather(t1_ref, 1)

        e = blk_e_ref[j]
        ws = wslot_ref[j]

        @pl.when(j == 0)
        def _():
            for cp in weight_copies(e, ws):
                cp.start(priority=WEIGHT_DMA_PRIORITY)

        first = jnp.logical_or(j == 0, e != blk_e_ref[jnp.maximum(j - 1, 0)])

        @pl.when(first)
        def _():
            for cp in weight_copies(e, ws):
                cp.wait()
            wg_s[...] = wg_buf[ws].astype(BF16)
            wu_s[...] = wu_buf[ws].astype(BF16)
            wd_s[...] = wd_buf[ws].astype(BF16)

        wait_gather(slot)

        @pl.when(j + 2 < n_used)
        def _():
            start_gather(t2_ref, lax.rem(j + 2, GATHER_SLOTS))

        @pl.when(jnp.logical_and(first, nxt_e_ref[j] >= 0))
        def _():
            for cp in weight_copies(nxt_e_ref[j], 1 - ws):
                cp.start(priority=WEIGHT_DMA_PRIORITY)

        @pl.when(j > 0)
        def _():
            wait_writes(nv_prev)

        def expert_rows(rows):
            xs = xbuf.at[slot]
            x = jnp.concatenate([xs[pl.ds(s, rows, stride=MOE_PITCH), :] for s in range(SLAB)], axis=1).astype(BF16)
            g = jnp.dot(x, wg_s[...], preferred_element_type=F32)
            u = jnp.dot(x, wu_s[...], preferred_element_type=F32)
            hm = (g * jax.nn.sigmoid(g) * u).astype(BF16)
            y = jnp.dot(hm, wd_s[...], preferred_element_type=F32)
            for s in range(SLAB):
                ybuf[pl.ds(s, rows, stride=MOE_PITCH), :] = y[:, s * LANES:(s + 1) * LANES]

        @pl.when(nv > MOE_TB // 2)
        def _():
            expert_rows(MOE_TB)

        @pl.when(nv <= MOE_TB // 2)
        def _():
            expert_rows(MOE_TB // 2)

        def start_write(r):
            pltpu.make_async_copy(slab(ybuf, r * MOE_PITCH, PITCH_ALIGN), slab(y_hbm, a_ref[0, 0, r]),
                                  wsem.at[0]).start()

        def wgroup(g8, _):
            for q in range(SUBLANES):
                start_write(g8 * SUBLANES + q)
            return 0

        def wtail(r, _):
            start_write(r)
            return 0

        n_groups = lax.shift_right_logical(nv, 3)
        lax.fori_loop(0, n_groups, wgroup, 0)
        lax.fori_loop(n_groups * SUBLANES, nv, wtail, 0)

        @pl.when(j == n_used - 1)
        def _():
            wait_writes(nv)


def _moe(h2s, topi, counts, w_gate, w_up, w_down):
    n = h2s.shape[0] // SLAB
    d = D_MODEL
    nk = n * TOP_K
    tb = MOE_TB
    n_blocks = nk // tb + N_EXPERTS
    flat_e = topi[:TOP_K].reshape(nk)
    _, order = lax.sort_key_val(flat_e, jnp.arange(nk, dtype=jnp.int32))
    start = jnp.cumsum(counts) - counts
    padded = (counts + tb - 1) // tb * tb
    pad_end = jnp.cumsum(padded)
    pad_start = pad_end - padded
    n_used = (pad_end[-1] // tb).astype(jnp.int32).reshape(1)
    blk_p0 = jnp.arange(n_blocks, dtype=jnp.int32) * tb
    blk_e = jnp.minimum(jnp.sum((pad_end[None, :] <= blk_p0[:, None]).astype(jnp.int32), axis=1), N_EXPERTS - 1)
    blk_off = blk_p0 - pad_start[blk_e]
    nvalid = jnp.clip(counts[blk_e] - blk_off, 0, tb).astype(jnp.int32)
    rank = (start[blk_e] + blk_off)[:, None] + jnp.arange(tb, dtype=jnp.int32)[None, :]
    a3 = order[jnp.clip(rank, 0, nk - 1)].reshape(n_blocks, 1, tb)
    arow = a3 * SLAB
    trow = (a3 % n) * SLAB
    used = counts > 0
    e_ids = jnp.arange(N_EXPERTS, dtype=jnp.int32)
    later = jnp.where(used, e_ids, N_EXPERTS)
    nxt = jnp.concatenate([lax.cummin(later[::-1])[::-1][1:], jnp.full((1,), N_EXPERTS, jnp.int32)])
    nxt_e = jnp.where(nxt < N_EXPERTS, nxt, -1)[blk_e].astype(jnp.int32)
    wslot = ((jnp.cumsum(used.astype(jnp.int32)) - 1) % 2)[blk_e].astype(jnp.int32)

    smem_blk = lambda f: pl.BlockSpec((1, 1, tb), f, memory_space=pltpu.SMEM)
    any_spec = pl.BlockSpec(memory_space=pl.ANY)
    grid_spec = pltpu.PrefetchScalarGridSpec(
        num_scalar_prefetch=5,
        grid=(n_blocks,),
        in_specs=[smem_blk(lambda j, *_: (j, 0, 0)),
                  smem_blk(lambda j, *_: (jnp.minimum(j + 1, n_blocks - 1), 0, 0)),
                  smem_blk(lambda j, *_: (jnp.minimum(j + 2, n_blocks - 1), 0, 0)),
                  smem_blk(lambda j, *_: (j, 0, 0)),
                  any_spec, any_spec, any_spec, any_spec],
        out_specs=any_spec,
        scratch_shapes=[pltpu.VMEM((GATHER_SLOTS, tb * MOE_PITCH, LANES), F32),
                        pltpu.VMEM((tb * MOE_PITCH, LANES), F32),
                        pltpu.VMEM((2, d, D_EXPERT), F32),
                        pltpu.VMEM((2, d, D_EXPERT), F32),
                        pltpu.VMEM((2, D_EXPERT, d), F32),
                        pltpu.VMEM((d, D_EXPERT), BF16),
                        pltpu.VMEM((d, D_EXPERT), BF16),
                        pltpu.VMEM((D_EXPERT, d), BF16),
                        pltpu.SemaphoreType.DMA((GATHER_SLOTS,)),
                        pltpu.SemaphoreType.DMA((1,)),
                        pltpu.SemaphoreType.DMA((2,))],
    )
    return pl.pallas_call(
        _moe_kernel,
        grid_spec=grid_spec,
        out_shape=jax.ShapeDtypeStruct((TOP_K * n * SLAB, LANES), F32),
        compiler_params=_cparams(("arbitrary",)),
        name="moe",
    )(blk_e, n_used, nvalid, nxt_e, wslot, trow, trow, trow, arow, h2s, w_gate, w_up, w_down)


def _final_kernel(h_ref, y0, y1, y2, y3, y4, y5, tw_ref, x1_ref, mod_ref, g_ref, wg_ref, wu_ref, wd_ref, o_ref):
    h = h_ref[...]
    tm = h.shape[0]
    g = jnp.dot(h, wg_ref[...], preferred_element_type=F32)
    u = jnp.dot(h, wu_ref[...], preferred_element_type=F32)
    hm = (g * jax.nn.sigmoid(g) * u).astype(BF16)
    ffn = jnp.dot(hm, wd_ref[...], preferred_element_type=F32)
    tw = tw_ref[...]
    for k, y in enumerate((y0, y1, y2, y3, y4, y5)):
        yk = jnp.concatenate([y[pl.ds(s, tm, stride=SLAB), :] for s in range(SLAB)], axis=1)
        ffn = ffn + yk * tw[:, k:k + 1]
    o_ref[0] = x1_ref[0] + mod_ref[0, 5:6, :] * _rms(ffn, g_ref[...])


def _final(h2, y_flat, topw_t, x1, mod, g_post_ffn, w_sg, w_su, w_sd):
    bn, l, d = x1.shape
    n = bn * l
    tm = 256
    nl = l // tm
    const = lambda b, i: (0, 0)
    one = pl.Buffered(1)
    y_specs = [pl.BlockSpec((tm * SLAB, LANES), functools.partial(lambda b, i, k: (k * (n // tm) + b * nl + i, 0), k=k))
               for k in range(TOP_K)]
    return pl.pallas_call(
        _final_kernel,
        grid=(bn, nl),
        in_specs=[pl.BlockSpec((tm, d), lambda b, i: (b * nl + i, 0))] + y_specs + [
            pl.BlockSpec((tm, SUBLANES), lambda b, i: (b * nl + i, 0)),
            pl.BlockSpec((1, tm, d), lambda b, i: (b, i, 0)),
            pl.BlockSpec((1, 6, d), lambda b, i: (b, 0, 0)),
            pl.BlockSpec((1, d), const),
            pl.BlockSpec((d, D_EXPERT), const, pipeline_mode=one),
            pl.BlockSpec((d, D_EXPERT), const, pipeline_mode=one),
            pl.BlockSpec((D_EXPERT, d), const, pipeline_mode=one)],
        out_specs=pl.BlockSpec((1, tm, d), lambda b, i: (b, i, 0)),
        out_shape=jax.ShapeDtypeStruct((bn, l, d), F32),
        compiler_params=_cparams(("arbitrary", "arbitrary")),
        name="final",
    )(h2, *([y_flat] * TOP_K), topw_t, x1, mod, g_post_ffn.reshape(1, d), w_sg, w_su, w_sd)


def _layer(li, x, c, positions, w_ada, b_ada, g_pre_mix, g_post_mix, g_pre_ffn, g_post_ffn, w_in, g_q, g_kv,
           w_uq, w_uk, w_uv, a_re, a_im, log_dt, b_re, b_im, c_re, c_im, d_skip, w_glu, w_br_mla, w_br_s5,
           w_out, w_router, router_bias, w_exp_gate, w_exp_up, w_exp_down, w_sh_gate, w_sh_up, w_sh_down):
    bn, l, d = x.shape
    n = bn * l
    mod = _adaln(c, w_ada, b_ada).reshape(bn, 6, d)

    o_kpe = Q_LORA + KV_LORA
    o_u = o_kpe + QK_ROPE
    o_g = o_u + S5_CH
    w_in_t = jnp.swapaxes(w_in, 1, 2)
    w_q = jnp.pad(w_uq.reshape(Q_LORA, MLA_HEADS, QK_NOPE + QK_ROPE),
                  ((0, 0), (0, 0), (0, QK_PAD - QK_NOPE - QK_ROPE))).reshape(Q_LORA, MLA_HEADS * QK_PAD).astype(BF16)
    wr_t = w_router.T
    wr_hi = wr_t.astype(BF16)
    wr_lo = (wr_t - wr_hi.astype(F32)).astype(BF16)

    rope_c, rope_s1, rope_s2 = _rope_tables(positions)
    chunk_id = positions // CHUNK

    h, lat = _prenorm_lat(x, mod, g_pre_mix, 0, 1, w_in_t, li, o_kpe + LANES)
    h = h.reshape(n, d)
    gates = _mmt(h, w_in_t, li, o_g, 2 * d, BF16, 1024, 1024, act="sigmoid", name="mm_gates")
    u6 = _mm_u(h, w_in_t, li, o_u, bn, l)

    q, k, v = _qkvproj(lat, g_q, g_kv, w_q, w_uk.astype(BF16), w_uv.astype(BF16), rope_c, rope_s1, rope_s2)
    att = _attention(q, k, v, chunk_id, bn, l)

    wb, ab, wc, dd = _s5_params(a_re, a_im, log_dt, b_re, b_im, c_re, c_im, d_skip)
    y6 = _s5(u6, wb, ab, wc, dd, l)
    z = _glu(y6, w_glu.astype(BF16), bn, l).reshape(n, S5_CH)

    x1, h2b, h2s, logits_t = _merge(att, z, gates, x, mod, g_post_mix, g_pre_ffn, w_br_mla.astype(BF16),
                                    w_br_s5.astype(BF16), w_out.astype(BF16), wr_hi, wr_lo)

    topi, topw, cnt = _route(logits_t, router_bias)
    y_slabs = _moe(h2s, topi, cnt.reshape(N_EXPERTS).astype(jnp.int32), w_exp_gate, w_exp_up, w_exp_down)
    return _final(h2b.reshape(n, d), y_slabs, topw.T, x1, mod, g_post_ffn, w_sh_gate.astype(BF16),
                  w_sh_up.astype(BF16), w_sh_down.astype(BF16))


def kernel(x, c, positions, w_ada, b_ada, g_pre_mix, g_post_mix, g_pre_ffn, g_post_ffn, w_in, g_q, g_kv, w_uq, w_uk, w_uv, a_re, a_im, log_dt, b_re, b_im, c_re, c_im, d_skip, w_glu, w_br_mla, w_br_s5, w_out, w_router, router_bias, w_exp_gate, w_exp_up, w_exp_down, w_sh_gate, w_sh_up, w_sh_down):
    depth = w_ada.shape[0]
    for li in range(depth):
        x = _layer(li, x, c, positions, w_ada[li], b_ada[li], g_pre_mix[li], g_post_mix[li], g_pre_ffn[li],
                   g_post_ffn[li], w_in, g_q[li], g_kv[li], w_uq[li], w_uk[li], w_uv[li], a_re[li], a_im[li],
                   log_dt[li], b_re[li], b_im[li], c_re[li], c_im[li], d_skip[li], w_glu[li], w_br_mla[li],
                   w_br_s5[li], w_out[li], w_router[li], router_bias[li], w_exp_gate[li], w_exp_up[li],
                   w_exp_down[li], w_sh_gate[li], w_sh_up[li], w_sh_down[li])
    return x
```

```python
import functools

import jax
import jax.numpy as jnp
from jax import lax
from jax.experimental import pallas as pl
from jax.experimental.pallas import tpu as pltpu

F32 = jnp.float32
BF16 = jnp.bfloat16

D_MODEL = 2048
CHUNK = 64
EPS = 1e-6
MLA_HEADS = 8
QK_NOPE = 128
QK_ROPE = 64
V_HEAD = 128
Q_LORA = 512
KV_LORA = 512
ROPE_THETA = 10000.0
S5_CH = 1024
S5_GROUP = 16
S5_GROUPS = S5_CH // S5_GROUP
S5_STATE = 64
N_EXPERTS = 64
TOP_K = 6
N_EXPERT_GROUPS = 8
TOPK_GROUPS = 4
D_EXPERT = 512
ROUTED_SCALE = 2.5

LANES = 128
SUBLANES = 8
QK_PAD = 2 * LANES
VMEM_LIMIT = 56 * 1024 * 1024
NEG = -1e30
LOG2E = 1.4426950408889634

S5_GB = LANES // S5_GROUP
S5_NBLK = S5_CH // LANES
S5_PAIRS = S5_NBLK // 2
S5_LANES = S5_GB * S5_STATE
S5_PPS = 2
S5_TC = 256
S5_RB = 512

ATT_TQ = 256
ATT_TK = 256
MOE_TB = 256
SLAB = D_MODEL // LANES
MERGE_SUBTILES = 2
WRITE_GROUP = 16
GATHER_SLOTS = 3
WEIGHT_DMA_PRIORITY = 1
MOE_PITCH = 20
PITCH_ALIGN = 4


def _cparams(sem):
    return pltpu.CompilerParams(dimension_semantics=sem, vmem_limit_bytes=VMEM_LIMIT)


def _rms(x, g):
    return x * lax.rsqrt(jnp.mean(x * x, axis=-1, keepdims=True) + EPS) * g


def _adaln_kernel(c_ref, w_ref, b_ref, o_ref):
    c = c_ref[...]
    a = (c * jax.nn.sigmoid(c)).astype(BF16)
    o_ref[...] = jnp.dot(a, w_ref[...].astype(BF16), preferred_element_type=F32) + b_ref[...]


def _adaln(c, w, b):
    bn, d = c.shape
    n = w.shape[1]
    tn = 1024
    return pl.pallas_call(
        _adaln_kernel,
        grid=(n // tn,),
        in_specs=[pl.BlockSpec((bn, d), lambda j: (0, 0)),
                  pl.BlockSpec((d, tn), lambda j: (0, j)),
                  pl.BlockSpec((1, tn), lambda j: (0, j))],
        out_specs=pl.BlockSpec((bn, tn), lambda j: (0, j)),
        out_shape=jax.ShapeDtypeStruct((bn, n), F32),
        compiler_params=_cparams(("arbitrary",)),
        name="adaln",
    )(c, w, b.reshape(1, n))


NT_DIMS = (((1,), (1,)), ((), ()))


def _prenorm_lat_kernel(x_ref, mod_ref, g_ref, w_ref, h_ref, lat_ref, w_scr, *, sh_row, sc_row):
    @pl.when(jnp.logical_and(pl.program_id(0) == 0, pl.program_id(1) == 0))
    def _():
        w_scr[...] = w_ref[...].astype(BF16)

    y = _rms(x_ref[0], g_ref[...])
    h = (y * (1.0 + mod_ref[0, sc_row:sc_row + 1, :]) + mod_ref[0, sh_row:sh_row + 1, :]).astype(BF16)
    h_ref[0] = h
    lat_ref[...] = lax.dot_general(h, w_scr[...], NT_DIMS, preferred_element_type=F32)


def _prenorm_lat(x, mod, g, sh_row, sc_row, w_t, layer, n_lat):
    bn, l, d = x.shape
    tl = 512
    nl = l // tl
    return pl.pallas_call(
        functools.partial(_prenorm_lat_kernel, sh_row=sh_row, sc_row=sc_row),
        grid=(bn, nl),
        in_specs=[pl.BlockSpec((1, tl, d), lambda b, i: (b, i, 0)),
                  pl.BlockSpec((1, 6, d), lambda b, i: (b, 0, 0)),
                  pl.BlockSpec((1, d), lambda b, i: (0, 0)),
                  pl.BlockSpec((None, n_lat, d), lambda b, i: (layer, 0, 0), pipeline_mode=pl.Buffered(1))],
        out_specs=[pl.BlockSpec((1, tl, d), lambda b, i: (b, i, 0)),
                   pl.BlockSpec((tl, n_lat), lambda b, i: (b * nl + i, 0))],
        out_shape=[jax.ShapeDtypeStruct((bn, l, d), BF16),
                   jax.ShapeDtypeStruct((bn * l, n_lat), F32)],
        scratch_shapes=[pltpu.VMEM((n_lat, d), BF16)],
        compiler_params=_cparams(("arbitrary", "arbitrary")),
        name="prenorm_lat",
    )(x, mod, g.reshape(1, d), w_t)


def _wt_tile(wa_ref, wb_ref, shift, tn):
    if shift == 0:
        return wa_ref[...].astype(BF16)
    return jnp.concatenate([wa_ref[...], wb_ref[...]], axis=0)[shift:shift + tn].astype(BF16)


def _wt_specs(layer, row0, tn, k, jmap):
    shift = row0 % LANES
    c0 = row0 - shift
    assert c0 % tn == 0 and shift % SUBLANES == 0
    specs = [pl.BlockSpec((None, tn, k), lambda *g: (layer, jmap(*g) + c0 // tn, 0))]
    if shift:
        specs.append(pl.BlockSpec((None, LANES, k), lambda *g: (layer, (c0 + (jmap(*g) + 1) * tn) // LANES, 0)))
    return specs, shift


def _mmt_kernel(a_ref, wa_ref, *rest, act, shift):
    wb_ref = rest[0] if shift else None
    o_ref, w_scr = rest[-2:]

    @pl.when(pl.program_id(1) == 0)
    def _():
        w_scr[...] = _wt_tile(wa_ref, wb_ref, shift, w_scr.shape[0])

    acc = lax.dot_general(a_ref[...], w_scr[...], NT_DIMS, preferred_element_type=F32)
    if act == "sigmoid":
        acc = jax.nn.sigmoid(acc)
    o_ref[...] = acc.astype(o_ref.dtype)


def _mmt(a, w_t, layer, row0, n, out_dtype, tm, tn, act=None, name="mmt"):
    m, k = a.shape
    w_specs, shift = _wt_specs(layer, row0, tn, k, lambda j, i: j)
    return pl.pallas_call(
        functools.partial(_mmt_kernel, act=act, shift=shift),
        grid=(n // tn, m // tm),
        in_specs=[pl.BlockSpec((tm, k), lambda j, i: (i, 0))] + w_specs,
        out_specs=pl.BlockSpec((tm, tn), lambda j, i: (i, j)),
        out_shape=jax.ShapeDtypeStruct((m, n), out_dtype),
        scratch_shapes=[pltpu.VMEM((tn, k), BF16)],
        compiler_params=_cparams(("arbitrary", "arbitrary")),
        name=name,
    )(a, *([w_t] * len(w_specs)))


def _mm_u_kernel(a_ref, wa_ref, *rest, shift):
    wb_ref = rest[0] if shift else None
    o_ref, w_scr = rest[-2:]
    b = pl.program_id(1)

    @pl.when(jnp.logical_and(pl.program_id(0) == 0, b == 0))
    def _():
        w_scr[...] = _wt_tile(wa_ref, wb_ref, shift, w_scr.shape[0])

    res = lax.dot_general(a_ref[...], w_scr[...], NT_DIMS, preferred_element_type=F32)
    tm = res.shape[0]
    for c in range(S5_NBLK):
        o_ref.at[c // 2][pl.ds(b * 2 + c % 2, tm, stride=SUBLANES), :] = res[:, c * LANES:(c + 1) * LANES]


def _mm_u(h, w_t, layer, row0, bn, l):
    m, k = h.shape
    tm = 512
    nl = l // tm
    assert bn * 2 == SUBLANES
    w_specs, shift = _wt_specs(layer, row0, S5_CH, k, lambda i, b: 0)
    return pl.pallas_call(
        functools.partial(_mm_u_kernel, shift=shift),
        grid=(nl, bn),
        in_specs=[pl.BlockSpec((tm, k), lambda i, b: (b * nl + i, 0))] + w_specs,
        out_specs=pl.BlockSpec((S5_PAIRS, tm * SUBLANES, LANES), lambda i, b: (0, i, 0)),
        out_shape=jax.ShapeDtypeStruct((S5_PAIRS, l * SUBLANES, LANES), F32),
        scratch_shapes=[pltpu.VMEM((S5_CH, k), BF16)],
        compiler_params=_cparams(("arbitrary", "arbitrary")),
        name="mm_u",
    )(h, *([w_t] * len(w_specs)))


def _rope_tab_kernel(pos_ref, k_ref, c_ref, s1_ref, s2_ref):
    ang = pos_ref[...].astype(F32) * k_ref[0:1, :]
    s = jnp.sin(ang)
    c_ref[...] = jnp.cos(ang) * k_ref[1:2, :]
    s1_ref[...] = s * k_ref[2:3, :]
    s2_ref[...] = s * k_ref[3:4, :]


def _rope_tables(positions):
    n = positions.size
    half = QK_ROPE // 2
    inv_freq = ROPE_THETA ** (-jnp.arange(half, dtype=F32) / half)
    zh, oh = jnp.zeros((half,), F32), jnp.ones((half,), F32)
    z2 = jnp.zeros((LANES - QK_ROPE,), F32)
    rows = [jnp.concatenate([inv_freq, inv_freq, z2]), jnp.concatenate([oh, oh, z2]),
            jnp.concatenate([-oh, zh, z2]), jnp.concatenate([zh, oh, z2])]
    consts = jnp.stack(rows + [jnp.zeros((LANES,), F32)] * (SUBLANES - len(rows)))
    tm = 1024
    tab = jax.ShapeDtypeStruct((n, LANES), F32)
    return pl.pallas_call(
        _rope_tab_kernel,
        grid=(n // tm,),
        in_specs=[pl.BlockSpec((tm, 1), lambda i: (i, 0)),
                  pl.BlockSpec((SUBLANES, LANES), lambda i: (0, 0))],
        out_specs=[pl.BlockSpec((tm, LANES), lambda i: (i, 0))] * 3,
        out_shape=[tab, tab, tab],
        compiler_params=_cparams(("arbitrary",)),
        name="rope_tables",
    )(positions.reshape(n, 1), consts)


def _rope_tile(t, c_ref, s1_ref, s2_ref):
    return (t * c_ref[...] + pltpu.roll(t, LANES - QK_ROPE // 2, 1) * s1_ref[...]
            + pltpu.roll(t, QK_ROPE // 2, 1) * s2_ref[...])


def _qkvproj_kernel(lat_ref, gq_ref, gkv_ref, wq_ref, wk_ref, wv_ref, kpe_ref, c_ref, s1_ref, s2_ref,
                    q_ref, k_ref, v_ref, *, scale):
    lat = lat_ref[...]
    qn = _rms(lat[:, :Q_LORA], gq_ref[...]).astype(BF16)
    cn = _rms(lat[:, Q_LORA:], gkv_ref[...]).astype(BF16)
    q = jnp.dot(qn, wq_ref[...], preferred_element_type=F32)
    kn = jnp.dot(cn, wk_ref[...], preferred_element_type=F32)
    v_ref[...] = jnp.dot(cn, wv_ref[...], preferred_element_type=F32).astype(v_ref.dtype)
    kt = _rope_tile(kpe_ref[...], c_ref, s1_ref, s2_ref).astype(k_ref.dtype)
    for h in range(MLA_HEADS):
        o = h * QK_PAD
        q_ref[:, o:o + LANES] = (q[:, o:o + LANES] * scale).astype(q_ref.dtype)
        qt = _rope_tile(q[:, o + LANES:o + QK_PAD], c_ref, s1_ref, s2_ref)
        q_ref[:, o + LANES:o + QK_PAD] = (qt * scale).astype(q_ref.dtype)
        k_ref[:, o:o + LANES] = kn[:, h * QK_NOPE:(h + 1) * QK_NOPE].astype(k_ref.dtype)
        k_ref[:, o + LANES:o + QK_PAD] = kt


def _qkvproj(lat, g_q, g_kv, w_q, w_k, w_v, rope_c, rope_s1, rope_s2):
    n = lat.shape[0]
    tm = 512
    row = lambda i: (i, 0)
    const = lambda i: (0, 0)
    tab = pl.BlockSpec((tm, LANES), row)
    kpe_spec = pl.BlockSpec((tm, LANES), lambda i: (i, (Q_LORA + KV_LORA) // LANES))
    return pl.pallas_call(
        functools.partial(_qkvproj_kernel, scale=(QK_NOPE + QK_ROPE) ** -0.5 * LOG2E),
        grid=(n // tm,),
        in_specs=[pl.BlockSpec((tm, Q_LORA + KV_LORA), row),
                  pl.BlockSpec((1, Q_LORA), const),
                  pl.BlockSpec((1, KV_LORA), const),
                  pl.BlockSpec((Q_LORA, MLA_HEADS * QK_PAD), const),
                  pl.BlockSpec((KV_LORA, MLA_HEADS * QK_NOPE), const),
                  pl.BlockSpec((KV_LORA, MLA_HEADS * V_HEAD), const),
                  kpe_spec, tab, tab, tab],
        out_specs=[pl.BlockSpec((tm, MLA_HEADS * QK_PAD), row),
                   pl.BlockSpec((tm, MLA_HEADS * QK_PAD), row),
                   pl.BlockSpec((tm, MLA_HEADS * V_HEAD), row)],
        out_shape=[jax.ShapeDtypeStruct((n, MLA_HEADS * QK_PAD), BF16),
                   jax.ShapeDtypeStruct((n, MLA_HEADS * QK_PAD), BF16),
                   jax.ShapeDtypeStruct((n, MLA_HEADS * V_HEAD), BF16)],
        compiler_params=_cparams(("arbitrary",)),
        name="qkvproj",
    )(lat, g_q.reshape(1, Q_LORA), g_kv.reshape(1, KV_LORA), w_q, w_k, w_v, lat, rope_c, rope_s1, rope_s2)


def _attn_kernel(lo_ref, hi_ref, q_ref, k_ref, v_ref, qc_ref, kc_ref, o_ref, m_scr, l_scr, acc_scr, *, nq):
    b = pl.program_id(0)
    i = pl.program_id(1)
    qc = qc_ref[...]
    m_scr[...] = jnp.full(m_scr.shape, NEG, F32)
    l_scr[...] = jnp.zeros(l_scr.shape, F32)
    acc_scr[...] = jnp.zeros(acc_scr.shape, F32)

    def make_body(masked):
        def body(j, _):
            off = pl.multiple_of(j * ATT_TK, ATT_TK)
            if masked:
                mask = kc_ref[j] <= qc
            for h in range(MLA_HEADS):
                q = q_ref[:, h * QK_PAD:(h + 1) * QK_PAD]
                k = k_ref[pl.ds(off, ATT_TK), h * QK_PAD:(h + 1) * QK_PAD]
                s = lax.dot_general(q, k, (((1,), (1,)), ((), ())), preferred_element_type=F32)
                if masked:
                    s = jnp.where(mask, s, NEG)
                m_old = m_scr[h]
                m_new = jnp.maximum(m_old, jnp.max(s, axis=-1, keepdims=True))
                p = jnp.exp2(s - jnp.concatenate([m_new] * (ATT_TK // LANES), axis=1))
                alpha = jnp.exp2(m_old - m_new)
                l_scr[h] = alpha * l_scr[h] + jnp.sum(p, axis=-1, keepdims=True)
                v = v_ref[pl.ds(off, ATT_TK), h * V_HEAD:(h + 1) * V_HEAD]
                acc_scr[h] = alpha * acc_scr[h] + jnp.dot(p.astype(BF16), v, preferred_element_type=F32)
                m_scr[h] = m_new
            return 0
        return body

    lo = lo_ref[b * nq + i]
    lax.fori_loop(0, lo, make_body(False), 0)
    lax.fori_loop(lo, hi_ref[b * nq + i], make_body(True), 0)
    for h in range(MLA_HEADS):
        o_ref[:, h * V_HEAD:(h + 1) * V_HEAD] = (acc_scr[h] / l_scr[h]).astype(o_ref.dtype)


def _attention(q, k, v, chunk_id, bn, l):
    nq = l // ATT_TQ
    nk = l // ATT_TK
    q_max = jnp.max(chunk_id.reshape(bn, nq, ATT_TQ), axis=-1)
    k_min = jnp.min(chunk_id.reshape(bn, nk, ATT_TK), axis=-1)
    needed = k_min[:, None, :] <= q_max[:, :, None]
    hi = jnp.max(jnp.where(needed, jnp.arange(1, nk + 1, dtype=jnp.int32), 0), axis=-1).reshape(bn * nq)
    q_min = jnp.min(chunk_id.reshape(bn, nq, ATT_TQ), axis=-1)
    k_max = jnp.max(chunk_id.reshape(bn, nk, ATT_TK), axis=-1)
    full = k_max[:, None, :] <= q_min[:, :, None]
    lo = jnp.min(jnp.where(full, nk, jnp.arange(nk, dtype=jnp.int32)), axis=-1).astype(jnp.int32).reshape(bn * nq)
    hi = jnp.maximum(hi, lo)
    qc = chunk_id.reshape(bn * l, 1)
    kc = chunk_id.reshape(bn * nk, 1, ATT_TK)
    grid_spec = pltpu.PrefetchScalarGridSpec(
        num_scalar_prefetch=2,
        grid=(bn, nq),
        in_specs=[pl.BlockSpec((ATT_TQ, MLA_HEADS * QK_PAD), lambda b, i, *_: (b * nq + i, 0)),
                  pl.BlockSpec((l, MLA_HEADS * QK_PAD), lambda b, i, *_: (b, 0)),
                  pl.BlockSpec((l, MLA_HEADS * V_HEAD), lambda b, i, *_: (b, 0)),
                  pl.BlockSpec((ATT_TQ, 1), lambda b, i, *_: (b * nq + i, 0)),
                  pl.BlockSpec((nk, 1, ATT_TK), lambda b, i, *_: (b, 0, 0))],
        out_specs=pl.BlockSpec((ATT_TQ, MLA_HEADS * V_HEAD), lambda b, i, *_: (b * nq + i, 0)),
        scratch_shapes=[pltpu.VMEM((MLA_HEADS, ATT_TQ, LANES), F32),
                        pltpu.VMEM((MLA_HEADS, ATT_TQ, LANES), F32),
                        pltpu.VMEM((MLA_HEADS, ATT_TQ, V_HEAD), F32)],
    )
    return pl.pallas_call(
        functools.partial(_attn_kernel, nq=nq),
        grid_spec=grid_spec,
        out_shape=jax.ShapeDtypeStruct((bn * l, MLA_HEADS * V_HEAD), BF16),
        compiler_params=_cparams(("arbitrary", "arbitrary")),
        name="attention",
    )(lo, hi, q, k, v, qc, kc)


def _s5_kernel(u_ref, wb_ref, a_ref, wc_ref, d_ref, o_ref, x_scr, st_scr):
    rows = S5_TC * SUBLANES
    nsub = rows // S5_RB

    @pl.when(pl.program_id(1) == 0)
    def _():
        st_scr[...] = jnp.zeros_like(st_scr)

    even = (lax.broadcasted_iota(jnp.int32, (S5_RB, 1), 0) & 1) == 0
    half = S5_RB // 2
    nre = S5_LANES // LANES
    nsl = 2 * nre

    def mm_in(r, _):
        off = pl.multiple_of(r * S5_RB, S5_RB)
        for p in range(S5_PPS):
            for s in range(2):
                us = u_ref.at[p][pl.ds(off + s, half, stride=2), :].astype(BF16)
                out = jnp.dot(us, wb_ref[p, :, s * 2 * S5_LANES:(s + 1) * 2 * S5_LANES],
                              preferred_element_type=F32)
                for c in range(nsl):
                    x_scr.at[p * nsl + c][pl.ds(off + s, half, stride=2), :] = out[:, c * LANES:(c + 1) * LANES]
        return 0

    lax.fori_loop(0, nsub, mm_in, 0)

    a_c = [a_ref[p, :, c * LANES:(c + 1) * LANES] for p in range(S5_PPS) for c in range(nsl)]

    def step(t, carry):
        off = pl.multiple_of(t * SUBLANES, SUBLANES)
        new = [None] * (S5_PPS * nsl)
        for p in range(S5_PPS):
            for c in range(nre):
                ir, ii = p * nsl + c, p * nsl + nre + c
                xr, xi = carry[ir], carry[ii]
                ar, ai = a_c[ir], a_c[ii]
                nr = ar * xr - ai * xi + x_scr[ir, pl.ds(off, SUBLANES), :]
                ni = ar * xi + ai * xr + x_scr[ii, pl.ds(off, SUBLANES), :]
                x_scr[ir, pl.ds(off, SUBLANES), :] = nr
                x_scr[ii, pl.ds(off, SUBLANES), :] = ni
                new[ir], new[ii] = nr, ni
        return tuple(new)

    init = tuple(st_scr[p, :, c * LANES:(c + 1) * LANES] for p in range(S5_PPS) for c in range(nsl))
    fin = lax.fori_loop(0, S5_TC, step, init, unroll=8)
    for p in range(S5_PPS):
        for c in range(nsl):
            st_scr[p, :, c * LANES:(c + 1) * LANES] = fin[p * nsl + c]

    def mm_out(r, _):
        off = pl.multiple_of(r * S5_RB, S5_RB)
        for p in range(S5_PPS):
            d = jnp.concatenate([d_ref[p]] * (S5_RB // SUBLANES), axis=0)
            x = jnp.concatenate([x_scr[p * nsl + c, pl.ds(off, S5_RB), :] for c in range(nsl)],
                                axis=1).astype(BF16)
            out = jnp.dot(x, wc_ref[p], preferred_element_type=F32)
            y = jnp.where(even, out[:, :LANES], out[:, LANES:]) + d * u_ref[p, pl.ds(off, S5_RB), :]
            o_ref[p, pl.ds(off, S5_RB), :] = jax.nn.gelu(y).astype(o_ref.dtype)
        return 0

    lax.fori_loop(0, nsub, mm_out, 0)


def _s5_params(a_re, a_im, log_dt, b_re, b_im, c_re, c_im, d_skip):
    step = jnp.exp(log_dt)[:, None]
    mag = jnp.exp(a_re * step)
    abar_re, abar_im = mag * jnp.cos(a_im * step), mag * jnp.sin(a_im * step)
    den = a_re * a_re + a_im * a_im
    nr, ni = abar_re - 1.0, abar_im
    f_re, f_im = (nr * a_re + ni * a_im) / den, (ni * a_re - nr * a_im) / den
    bbar_re = f_re[..., None] * b_re - f_im[..., None] * b_im
    bbar_im = f_re[..., None] * b_im + f_im[..., None] * b_re
    eye = jnp.eye(S5_GB, dtype=F32)
    bb = jnp.stack([bbar_re, bbar_im]).reshape(2, S5_NBLK, S5_GB, S5_STATE, S5_GROUP)
    wb = jnp.einsum('ab,rjapc->jacrbp', eye, bb).reshape(S5_NBLK, LANES, 2 * S5_LANES)
    wb = wb.reshape(S5_PAIRS, 2, LANES, 2 * S5_LANES).transpose(0, 2, 1, 3).reshape(S5_PAIRS, LANES, 4 * S5_LANES)
    cc = jnp.stack([c_re, -c_im]).reshape(2, S5_NBLK, S5_GB, S5_GROUP, S5_STATE)
    wc = jnp.einsum('ab,rjacp->jrapbc', eye, cc).reshape(S5_NBLK, 2 * S5_LANES, LANES)
    wc = wc.reshape(S5_PAIRS, 2, 2 * S5_LANES, LANES).transpose(0, 2, 1, 3).reshape(S5_PAIRS, 2 * S5_LANES, 2 * LANES)
    ab = jnp.concatenate([abar_re.reshape(S5_NBLK, S5_LANES), abar_im.reshape(S5_NBLK, S5_LANES)], axis=1)
    ab = jnp.tile(ab.reshape(S5_PAIRS, 2, 2 * S5_LANES), (1, SUBLANES // 2, 1))
    dd = jnp.tile(d_skip.reshape(S5_PAIRS, 2, LANES), (1, SUBLANES // 2, 1))
    return wb.astype(BF16), ab, wc.astype(BF16), dd


def _s5(u6, wb, ab, wc, dd, l):
    rows = S5_TC * SUBLANES
    return pl.pallas_call(
        _s5_kernel,
        grid=(S5_PAIRS // S5_PPS, l // S5_TC),
        in_specs=[pl.BlockSpec((S5_PPS, rows, LANES), lambda k, c: (k, c, 0)),
                  pl.BlockSpec((S5_PPS, LANES, 4 * S5_LANES), lambda k, c: (k, 0, 0)),
                  pl.BlockSpec((S5_PPS, SUBLANES, 2 * S5_LANES), lambda k, c: (k, 0, 0)),
                  pl.BlockSpec((S5_PPS, 2 * S5_LANES, 2 * LANES), lambda k, c: (k, 0, 0)),
                  pl.BlockSpec((S5_PPS, SUBLANES, LANES), lambda k, c: (k, 0, 0))],
        out_specs=pl.BlockSpec((S5_PPS, rows, LANES), lambda k, c: (k, c, 0)),
        out_shape=jax.ShapeDtypeStruct(u6.shape, F32),
        scratch_shapes=[pltpu.VMEM((S5_PPS * 2 * S5_LANES // LANES, rows, LANES), F32),
                        pltpu.VMEM((S5_PPS, SUBLANES, 2 * S5_LANES), F32)],
        compiler_params=_cparams(("arbitrary", "arbitrary")),
        name="s5",
    )(u6, wb, ab, wc, dd)


def _glu_kernel(y_ref, w_ref, o_ref):
    b = pl.program_id(1)
    tt = o_ref.shape[1]
    y = jnp.concatenate([y_ref.at[k][pl.ds(b * 2 + s, tt, stride=SUBLANES), :]
                         for k in range(S5_PAIRS) for s in range(2)], axis=1)
    g = jnp.dot(y.astype(BF16), w_ref[...], preferred_element_type=F32)
    o_ref[0] = (y * jax.nn.sigmoid(g)).astype(o_ref.dtype)


def _glu(y6, w_glu, bn, l):
    tt = 512
    assert bn * 2 == SUBLANES
    return pl.pallas_call(
        _glu_kernel,
        grid=(l // tt, bn),
        in_specs=[pl.BlockSpec((S5_PAIRS, tt * SUBLANES, LANES), lambda i, b: (0, i, 0)),
                  pl.BlockSpec((S5_CH, S5_CH), lambda i, b: (0, 0))],
        out_specs=pl.BlockSpec((1, tt, S5_CH), lambda i, b: (b, i, 0)),
        out_shape=jax.ShapeDtypeStruct((bn, l, S5_CH), BF16),
        compiler_params=_cparams(("arbitrary", "arbitrary")),
        name="glu",
    )(y6, w_glu)


def _merge_kernel(att_ref, z_ref, gm_ref, gs_ref, x_ref, mod_ref, gpost_ref, gpre_ref,
                  wbm_ref, wbs_ref, wo_ref, wrh_ref, wrl_ref, x1_ref, h2b_ref, h2s_ref, lg_ref):
    tm = x_ref.shape[1]
    th = tm // MERGE_SUBTILES
    for r0 in range(0, tm, th):
        rows = slice(r0, r0 + th)
        ym = jnp.dot(att_ref[rows, :], wbm_ref[...], preferred_element_type=F32)
        ys = jnp.dot(z_ref[rows, :], wbs_ref[...], preferred_element_type=F32)
        mixed_in = (gm_ref[rows, :].astype(F32) * ym + gs_ref[rows, :].astype(F32) * ys).astype(BF16)
        mixed = jnp.dot(mixed_in, wo_ref[...], preferred_element_type=F32)
        x1 = x_ref[0, rows, :] + mod_ref[0, 2:3, :] * _rms(mixed, gpost_ref[...])
        x1_ref[0, rows, :] = x1
        h2 = _rms(x1, gpre_ref[...]) * (1.0 + mod_ref[0, 4:5, :]) + mod_ref[0, 3:4, :]
        h2_hi = h2.astype(BF16)
        h2b_ref[0, rows, :] = h2_hi
        for s in range(SLAB):
            h2s_ref[pl.ds(r0 * SLAB + s, th, stride=SLAB), :] = h2[:, s * LANES:(s + 1) * LANES]
        h2_lo = (h2 - h2_hi.astype(F32)).astype(BF16)
        lg_ref[:, rows] = (lax.dot_general(wrh_ref[...], h2_hi, NT_DIMS, preferred_element_type=F32)
                           + lax.dot_general(wrl_ref[...], h2_hi, NT_DIMS, preferred_element_type=F32)
                           + lax.dot_general(wrh_ref[...], h2_lo, NT_DIMS, preferred_element_type=F32))


def _merge(att, z_tm, gates, x, mod, g_post, g_pre_ffn, w_br_mla, w_br_s5, w_out, wr_hi, wr_lo):
    bn, l, d = x.shape
    tm = 256
    nl = l // tm
    row = lambda b, i: (b * nl + i, 0)
    const = lambda b, i: (0, 0)
    one = pl.Buffered(1)
    return pl.pallas_call(
        _merge_kernel,
        grid=(bn, nl),
        in_specs=[pl.BlockSpec((tm, MLA_HEADS * V_HEAD), row),
                  pl.BlockSpec((tm, S5_CH), row),
                  pl.BlockSpec((tm, d), lambda b, i: (b * nl + i, 0)),
                  pl.BlockSpec((tm, d), lambda b, i: (b * nl + i, 1)),
                  pl.BlockSpec((1, tm, d), lambda b, i: (b, i, 0)),
                  pl.BlockSpec((1, 6, d), lambda b, i: (b, 0, 0)),
                  pl.BlockSpec((1, d), const),
                  pl.BlockSpec((1, d), const),
                  pl.BlockSpec((MLA_HEADS * V_HEAD, d), const, pipeline_mode=one),
                  pl.BlockSpec((S5_CH, d), const, pipeline_mode=one),
                  pl.BlockSpec((d, d), const, pipeline_mode=one),
                  pl.BlockSpec((N_EXPERTS, d), const, pipeline_mode=one),
                  pl.BlockSpec((N_EXPERTS, d), const, pipeline_mode=one)],
        out_specs=[pl.BlockSpec((1, tm, d), lambda b, i: (b, i, 0)),
                   pl.BlockSpec((1, tm, d), lambda b, i: (b, i, 0)),
                   pl.BlockSpec((tm * SLAB, LANES), row),
                   pl.BlockSpec((N_EXPERTS, tm), lambda b, i: (0, b * nl + i))],
        out_shape=[jax.ShapeDtypeStruct((bn, l, d), F32),
                   jax.ShapeDtypeStruct((bn, l, d), BF16),
                   jax.ShapeDtypeStruct((bn * l * SLAB, LANES), F32),
                   jax.ShapeDtypeStruct((N_EXPERTS, bn * l), F32)],
        compiler_params=_cparams(("arbitrary", "arbitrary")),
        name="merge",
    )(att, z_tm, gates, gates, x, mod, g_post.reshape(1, d), g_pre_ffn.reshape(1, d),
      w_br_mla, w_br_s5, w_out, wr_hi, wr_lo)


def _route_kernel(lg_ref, b_ref, ti_ref, tw_ref, cnt_ref):
    ng = N_EXPERT_GROUPS
    gsz = N_EXPERTS // ng
    lg = lg_ref[...]
    t = lg.shape[-1]
    sc = jax.nn.sigmoid(lg)
    sel = sc + b_ref[...]
    ninf = -jnp.inf
    i_in = lax.broadcasted_iota(jnp.int32, (ng, gsz, t), 1).astype(F32)
    m1 = jnp.max(sel, axis=1, keepdims=True)
    idx1 = jnp.min(jnp.where(sel == m1, i_in, float(gsz)), axis=1, keepdims=True)
    m2 = jnp.max(jnp.where(i_in == idx1, ninf, sel), axis=1, keepdims=True)
    gs = m1 + m2
    g_i = lax.broadcasted_iota(jnp.int32, (ng, 1, t), 0).astype(F32)
    picked = jnp.zeros((ng, 1, t), F32)
    cur = gs
    for _ in range(TOPK_GROUPS):
        m = jnp.max(cur, axis=0, keepdims=True)
        idx = jnp.min(jnp.where(cur == m, g_i, float(ng)), axis=0, keepdims=True)
        hit = g_i == idx
        picked = jnp.where(hit, 1.0, picked)
        cur = jnp.where(hit, ninf, cur)
    cand = jnp.where(picked > 0.5, sel, ninf)
    e_i = lax.broadcasted_iota(jnp.int32, (ng, gsz, t), 0).astype(F32) * float(gsz) + i_in
    ws = []
    hits = jnp.zeros((ng, gsz, t), F32)
    for r in range(TOP_K):
        m = jnp.max(jnp.max(cand, axis=1, keepdims=True), axis=0, keepdims=True)
        idx = jnp.min(jnp.min(jnp.where(cand == m, e_i, float(N_EXPERTS)), axis=1, keepdims=True),
                      axis=0, keepdims=True)
        hit = e_i == idx
        w = jnp.sum(jnp.sum(jnp.where(hit, sc, 0.0), axis=1, keepdims=True), axis=0, keepdims=True)
        ti_ref[r:r + 1, :] = idx[0].astype(jnp.int32)
        ws.append(w[0])
        hits = jnp.where(hit, 1.0, hits)
        cand = jnp.where(hit, ninf, cand)

    @pl.when(pl.program_id(0) == 0)
    def _():
        cnt_ref[...] = jnp.zeros(cnt_ref.shape, F32)

    cnt_ref[...] += jnp.sum(hits, axis=2, keepdims=True)
    tot = ws[0]
    for w in ws[1:]:
        tot = tot + w
    for r in range(TOP_K):
        tw_ref[r:r + 1, :] = ws[r] / tot * ROUTED_SCALE
    for r in range(TOP_K, SUBLANES):
        ti_ref[r:r + 1, :] = jnp.zeros((1, t), jnp.int32)
        tw_ref[r:r + 1, :] = jnp.zeros((1, t), F32)


def _route(logits_t, router_bias):
    n = logits_t.shape[1]
    ng = N_EXPERT_GROUPS
    gsz = N_EXPERTS // ng
    tn = 1024
    return pl.pallas_call(
        _route_kernel,
        grid=(n // tn,),
        in_specs=[pl.BlockSpec((ng, gsz, tn), lambda i: (0, 0, i)),
                  pl.BlockSpec((ng, gsz, 1), lambda i: (0, 0, 0))],
        out_specs=[pl.BlockSpec((SUBLANES, tn), lambda i: (0, i)),
                   pl.BlockSpec((SUBLANES, tn), lambda i: (0, i)),
                   pl.BlockSpec((ng, gsz, 1), lambda i: (0, 0, 0))],
        out_shape=[jax.ShapeDtypeStruct((SUBLANES, n), jnp.int32),
                   jax.ShapeDtypeStruct((SUBLANES, n), F32),
                   jax.ShapeDtypeStruct((ng, gsz, 1), F32)],
        compiler_params=_cparams(("arbitrary",)),
        name="route",
    )(logits_t.reshape(ng, gsz, n), router_bias.reshape(ng, gsz, 1))


def _moe_kernel(blk_e_ref, nused_ref, nvalid_ref, nxt_e_ref, wslot_ref, t_ref, t1_ref, t2_ref, a_ref, h_hbm,
                wg_hbm, wu_hbm, wd_hbm, y_hbm, xbuf, ybuf, wg_buf, wu_buf, wd_buf, wg_s, wu_s, wd_s,
                gsem, wsem, wtsem):
    j = pl.program_id(0)
    n_used = nused_ref[0]
    slot = lax.rem(j, GATHER_SLOTS)
    nv = nvalid_ref[j]
    nv_prev = nvalid_ref[jnp.maximum(j - 1, 0)]

    def slab(ref, row0, align=SUBLANES):
        return ref.at[pl.ds(pl.multiple_of(row0, align), SLAB), :]

    def start_gather(idx_ref, s):
        def body(r, _):
            pltpu.make_async_copy(slab(h_hbm, idx_ref[0, 0, r]), slab(xbuf.at[s], r * MOE_PITCH, PITCH_ALIGN),
                                  gsem.at[s]).start()
            return 0
        lax.fori_loop(0, MOE_TB, body, 0, unroll=16)

    def wait_gather(s):
        rows = MOE_TB * SLAB
        pltpu.make_async_copy(h_hbm.at[pl.ds(0, rows), :], xbuf.at[s, pl.ds(0, rows), :], gsem.at[s]).wait()

    def weight_copies(expert, s):
        return (pltpu.make_async_copy(wg_hbm.at[expert], wg_buf.at[s], wtsem.at[s]),
                pltpu.make_async_copy(wu_hbm.at[expert], wu_buf.at[s], wtsem.at[s]),
                pltpu.make_async_copy(wd_hbm.at[expert], wd_buf.at[s], wtsem.at[s]))

    def wait_writes(count):
        p = MOE_TB
        while p >= 1:
            @pl.when((count & p) != 0)
            def _(p=p):
                rows = p * SLAB
                pltpu.make_async_copy(ybuf.at[pl.ds(0, rows), :], y_hbm.at[pl.ds(0, rows), :], wsem.at[0]).wait()
            p //= 2

    @pl.when(j < n_used)
    def _():
        @pl.when(j == 0)
        def _():
            start_gather(t_ref, 0)

            @pl.when(n_used > 1)
            def _():
                start_gather(t1_ref, 1)

        e = blk_e_ref[j]
        ws = wslot_ref[j]

        @pl.when(j == 0)
        def _():
            for cp in weight_copies(e, ws):
                cp.start(priority=WEIGHT_DMA_PRIORITY)

        first = jnp.logical_or(j == 0, e != blk_e_ref[jnp.maximum(j - 1, 0)])

        @pl.when(first)
        def _():
            for cp in weight_copies(e, ws):
                cp.wait()
            wg_s[...] = wg_buf[ws].astype(BF16)
            wu_s[...] = wu_buf[ws].astype(BF16)
            wd_s[...] = wd_buf[ws].astype(BF16)

        wait_gather(slot)

        @pl.when(j + 2 < n_used)
        def _():
            start_gather(t2_ref, lax.rem(j + 2, GATHER_SLOTS))

        @pl.when(jnp.logical_and(first, nxt_e_ref[j] >= 0))
        def _():
            for cp in weight_copies(nxt_e_ref[j], 1 - ws):
                cp.start(priority=WEIGHT_DMA_PRIORITY)

        @pl.when(j > 0)
        def _():
            wait_writes(nv_prev)

        def expert_rows(rows):
            xs = xbuf.at[slot]
            x = jnp.concatenate([xs[pl.ds(s, rows, stride=MOE_PITCH), :] for s in range(SLAB)], axis=1).astype(BF16)
            g = jnp.dot(x, wg_s[...], preferred_element_type=F32)
            u = jnp.dot(x, wu_s[...], preferred_element_type=F32)
            hm = (g * jax.nn.sigmoid(g) * u).astype(BF16)
            y = jnp.dot(hm, wd_s[...], preferred_element_type=F32)
            for s in range(SLAB):
                ybuf[pl.ds(s, rows, stride=MOE_PITCH), :] = y[:, s * LANES:(s + 1) * LANES]

        @pl.when(nv > MOE_TB // 2)
        def _():
            expert_rows(MOE_TB)

        @pl.when(nv <= MOE_TB // 2)
        def _():
            expert_rows(MOE_TB // 2)

        def start_write(r):
            pltpu.make_async_copy(slab(ybuf, r * MOE_PITCH, PITCH_ALIGN), slab(y_hbm, a_ref[0, 0, r]),
                                  wsem.at[0]).start()

        def wgroup(gi, _):
            for q in range(WRITE_GROUP):
                start_write(gi * WRITE_GROUP + q)
            return 0

        def wtail(r, _):
            start_write(r)
            return 0

        n_groups = lax.shift_right_logical(nv, WRITE_GROUP.bit_length() - 1)
        lax.fori_loop(0, n_groups, wgroup, 0)
        lax.fori_loop(n_groups * WRITE_GROUP, nv, wtail, 0)

        @pl.when(j == n_used - 1)
        def _():
            wait_writes(nv)


def _moe(h2s, topi, counts, w_gate, w_up, w_down):
    n = h2s.shape[0] // SLAB
    d = D_MODEL
    nk = n * TOP_K
    tb = MOE_TB
    n_blocks = nk // tb + N_EXPERTS
    flat_e = topi[:TOP_K].reshape(nk)
    _, order = lax.sort_key_val(flat_e, jnp.arange(nk, dtype=jnp.int32))
    start = jnp.cumsum(counts) - counts
    padded = (counts + tb - 1) // tb * tb
    pad_end = jnp.cumsum(padded)
    pad_start = pad_end - padded
    n_used = (pad_end[-1] // tb).astype(jnp.int32).reshape(1)
    blk_p0 = jnp.arange(n_blocks, dtype=jnp.int32) * tb
    blk_e = jnp.minimum(jnp.sum((pad_end[None, :] <= blk_p0[:, None]).astype(jnp.int32), axis=1), N_EXPERTS - 1)
    blk_off = blk_p0 - pad_start[blk_e]
    nvalid = jnp.clip(counts[blk_e] - blk_off, 0, tb).astype(jnp.int32)
    rank = (start[blk_e] + blk_off)[:, None] + jnp.arange(tb, dtype=jnp.int32)[None, :]
    a3 = order[jnp.clip(rank, 0, nk - 1)].reshape(n_blocks, 1, tb)
    arow = a3 * SLAB
    trow = (a3 % n) * SLAB
    used = counts > 0
    e_ids = jnp.arange(N_EXPERTS, dtype=jnp.int32)
    later = jnp.where(used, e_ids, N_EXPERTS)
    nxt = jnp.concatenate([lax.cummin(later[::-1])[::-1][1:], jnp.full((1,), N_EXPERTS, jnp.int32)])
    nxt_e = jnp.where(nxt < N_EXPERTS, nxt, -1)[blk_e].astype(jnp.int32)
    wslot = ((jnp.cumsum(used.astype(jnp.int32)) - 1) % 2)[blk_e].astype(jnp.int32)

    smem_blk = lambda f: pl.BlockSpec((1, 1, tb), f, memory_space=pltpu.SMEM)
    any_spec = pl.BlockSpec(memory_space=pl.ANY)
    grid_spec = pltpu.PrefetchScalarGridSpec(
        num_scalar_prefetch=5,
        grid=(n_blocks,),
        in_specs=[smem_blk(lambda j, *_: (j, 0, 0)),
                  smem_blk(lambda j, *_: (jnp.minimum(j + 1, n_blocks - 1), 0, 0)),
                  smem_blk(lambda j, *_: (jnp.minimum(j + 2, n_blocks - 1), 0, 0)),
                  smem_blk(lambda j, *_: (j, 0, 0)),
                  any_spec, any_spec, any_spec, any_spec],
        out_specs=any_spec,
        scratch_shapes=[pltpu.VMEM((GATHER_SLOTS, tb * MOE_PITCH, LANES), F32),
                        pltpu.VMEM((tb * MOE_PITCH, LANES), F32),
                        pltpu.VMEM((2, d, D_EXPERT), F32),
                        pltpu.VMEM((2, d, D_EXPERT), F32),
                        pltpu.VMEM((2, D_EXPERT, d), F32),
                        pltpu.VMEM((d, D_EXPERT), BF16),
                        pltpu.VMEM((d, D_EXPERT), BF16),
                        pltpu.VMEM((D_EXPERT, d), BF16),
                        pltpu.SemaphoreType.DMA((GATHER_SLOTS,)),
                        pltpu.SemaphoreType.DMA((1,)),
                        pltpu.SemaphoreType.DMA((2,))],
    )
    return pl.pallas_call(
        _moe_kernel,
        grid_spec=grid_spec,
        out_shape=jax.ShapeDtypeStruct((TOP_K * n * SLAB, LANES), F32),
        compiler_params=_cparams(("arbitrary",)),
        name="moe",
    )(blk_e, n_used, nvalid, nxt_e, wslot, trow, trow, trow, arow, h2s, w_gate, w_up, w_down)


def _final_kernel(h_ref, y0, y1, y2, y3, y4, y5, tw_ref, x1_ref, mod_ref, g_ref, wg_ref, wu_ref, wd_ref, o_ref):
    h = h_ref[...]
    tm = h.shape[0]
    g = jnp.dot(h, wg_ref[...], preferred_element_type=F32)
    u = jnp.dot(h, wu_ref[...], preferred_element_type=F32)
    hm = (g * jax.nn.sigmoid(g) * u).astype(BF16)
    ffn = jnp.dot(hm, wd_ref[...], preferred_element_type=F32)
    tw = tw_ref[...]
    for k, y in enumerate((y0, y1, y2, y3, y4, y5)):
        yk = jnp.concatenate([y[pl.ds(s, tm, stride=SLAB), :] for s in range(SLAB)], axis=1)
        ffn = ffn + yk * tw[:, k:k + 1]
    o_ref[0] = x1_ref[0] + mod_ref[0, 5:6, :] * _rms(ffn, g_ref[...])


def _final(h2, y_flat, topw_t, x1, mod, g_post_ffn, w_sg, w_su, w_sd):
    bn, l, d = x1.shape
    n = bn * l
    tm = 256
    nl = l // tm
    const = lambda b, i: (0, 0)
    one = pl.Buffered(1)
    y_specs = [pl.BlockSpec((tm * SLAB, LANES), functools.partial(lambda b, i, k: (k * (n // tm) + b * nl + i, 0), k=k))
               for k in range(TOP_K)]
    return pl.pallas_call(
        _final_kernel,
        grid=(bn, nl),
        in_specs=[pl.BlockSpec((tm, d), lambda b, i: (b * nl + i, 0))] + y_specs + [
            pl.BlockSpec((tm, SUBLANES), lambda b, i: (b * nl + i, 0)),
            pl.BlockSpec((1, tm, d), lambda b, i: (b, i, 0)),
            pl.BlockSpec((1, 6, d), lambda b, i: (b, 0, 0)),
            pl.BlockSpec((1, d), const),
            pl.BlockSpec((d, D_EXPERT), const, pipeline_mode=one),
            pl.BlockSpec((d, D_EXPERT), const, pipeline_mode=one),
            pl.BlockSpec((D_EXPERT, d), const, pipeline_mode=one)],
        out_specs=pl.BlockSpec((1, tm, d), lambda b, i: (b, i, 0)),
        out_shape=jax.ShapeDtypeStruct((bn, l, d), F32),
        compiler_params=_cparams(("arbitrary", "arbitrary")),
        name="final",
    )(h2, *([y_flat] * TOP_K), topw_t, x1, mod, g_post_ffn.reshape(1, d), w_sg, w_su, w_sd)


def _layer(li, x, c, positions, w_ada, b_ada, g_pre_mix, g_post_mix, g_pre_ffn, g_post_ffn, w_in, g_q, g_kv,
           w_uq, w_uk, w_uv, a_re, a_im, log_dt, b_re, b_im, c_re, c_im, d_skip, w_glu, w_br_mla, w_br_s5,
           w_out, w_router, router_bias, w_exp_gate, w_exp_up, w_exp_down, w_sh_gate, w_sh_up, w_sh_down):
    bn, l, d = x.shape
    n = bn * l
    mod = _adaln(c, w_ada, b_ada).reshape(bn, 6, d)

    o_kpe = Q_LORA + KV_LORA
    o_u = o_kpe + QK_ROPE
    o_g = o_u + S5_CH
    w_in_t = jnp.swapaxes(w_in, 1, 2)
    w_q = jnp.pad(w_uq.reshape(Q_LORA, MLA_HEADS, QK_NOPE + QK_ROPE),
                  ((0, 0), (0, 0), (0, QK_PAD - QK_NOPE - QK_ROPE))).reshape(Q_LORA, MLA_HEADS * QK_PAD).astype(BF16)
    wr_t = w_router.T
    wr_hi = wr_t.astype(BF16)
    wr_lo = (wr_t - wr_hi.astype(F32)).astype(BF16)

    rope_c, rope_s1, rope_s2 = _rope_tables(positions)
    chunk_id = positions // CHUNK

    h, lat = _prenorm_lat(x, mod, g_pre_mix, 0, 1, w_in_t, li, o_kpe + LANES)
    h = h.reshape(n, d)
    gates = _mmt(h, w_in_t, li, o_g, 2 * d, BF16, 1024, 1024, act="sigmoid", name="mm_gates")
    u6 = _mm_u(h, w_in_t, li, o_u, bn, l)

    q, k, v = _qkvproj(lat, g_q, g_kv, w_q, w_uk.astype(BF16), w_uv.astype(BF16), rope_c, rope_s1, rope_s2)
    att = _attention(q, k, v, chunk_id, bn, l)

    wb, ab, wc, dd = _s5_params(a_re, a_im, log_dt, b_re, b_im, c_re, c_im, d_skip)
    y6 = _s5(u6, wb, ab, wc, dd, l)
    z = _glu(y6, w_glu.astype(BF16), bn, l).reshape(n, S5_CH)

    x1, h2b, h2s, logits_t = _merge(att, z, gates, x, mod, g_post_mix, g_pre_ffn, w_br_mla.astype(BF16),
                                    w_br_s5.astype(BF16), w_out.astype(BF16), wr_hi, wr_lo)

    topi, topw, cnt = _route(logits_t, router_bias)
    y_slabs = _moe(h2s, topi, cnt.reshape(N_EXPERTS).astype(jnp.int32), w_exp_gate, w_exp_up, w_exp_down)
    return _final(h2b.reshape(n, d), y_slabs, topw.T, x1, mod, g_post_ffn, w_sh_gate.astype(BF16),
                  w_sh_up.astype(BF16), w_sh_down.astype(BF16))


def kernel(x, c, positions, w_ada, b_ada, g_pre_mix, g_post_mix, g_pre_ffn, g_post_ffn, w_in, g_q, g_kv, w_uq, w_uk, w_uv, a_re, a_im, log_dt, b_re, b_im, c_re, c_im, d_skip, w_glu, w_br_mla, w_br_s5, w_out, w_router, router_bias, w_exp_gate, w_exp_up, w_exp_down, w_sh_gate, w_sh_up, w_sh_down):
    depth = w_ada.shape[0]
    for li in range(depth):
        x = _layer(li, x, c, positions, w_ada[li], b_ada[li], g_pre_mix[li], g_post_mix[li], g_pre_ffn[li],
                   g_post_ffn[li], w_in, g_q[li], g_kv[li], w_uq[li], w_uk[li], w_uv[li], a_re[li], a_im[li],
                   log_dt[li], b_re[li], b_im[li], c_re[li], c_im[li], d_skip[li], w_glu[li], w_br_mla[li],
                   w_br_s5[li], w_out[li], w_router[li], router_bias[li], w_exp_gate[li], w_exp_up[li],
                   w_exp_down[li], w_sh_gate[li], w_sh_up[li], w_sh_down[li])
    return x
```
